```python
import math
import jax, jax.numpy as jnp
from jax import lax
import numpy as np

D_MODEL = 1024
BATCH = 16
SEQ = 4096
DEPTH = 1

LRU_WIDTH = D_MODEL
LRU_HEADS = 16
LRU_HEAD_DIM = LRU_WIDTH // LRU_HEADS
LRU_C = 8.0
CONV_WIDTH = 4
SSD_INNER = 2 * D_MODEL
SSD_HEAD_DIM = 64
SSD_HEADS = SSD_INNER // SSD_HEAD_DIM
SSD_GROUPS = 8
SSD_HPG = SSD_HEADS // SSD_GROUPS
SSD_STATE = 128
SSD_CHUNK = 128
SSD_CONV_DIM = SSD_INNER + 2 * SSD_GROUPS * SSD_STATE
N_BRANCH = 2
D_FF = 4 * D_MODEL
N_MOD = 6
EPS = 1e-6
IN_SIZES = (LRU_WIDTH, LRU_WIDTH, SSD_INNER, SSD_CONV_DIM, SSD_HEADS, N_BRANCH * D_MODEL)
IN_DIM = sum(IN_SIZES)

kernel_name = "hybrid_rglru_ssd_gated_merge_block"


def rmsnorm(x, w):
    xf = x.astype(jnp.float32)
    y = xf * lax.rsqrt(jnp.mean(xf * xf, axis=-1, keepdims=True) + EPS)
    return (y * w.astype(jnp.float32)).astype(x.dtype)


def grouped_rmsnorm(y, w):
    b, s, d = y.shape
    yf = y.astype(jnp.float32).reshape(b, s, SSD_GROUPS, d // SSD_GROUPS)
    yf = yf * lax.rsqrt(jnp.mean(yf * yf, axis=-1, keepdims=True) + EPS)
    return (yf.reshape(b, s, d) * w.astype(jnp.float32)).astype(y.dtype)


def causal_dwconv(x, w, b):
    s = x.shape[1]
    xp = jnp.pad(x, ((0, 0), (CONV_WIDTH - 1, 0), (0, 0)))
    return sum((xp[:, k:k + s] * w[k] for k in range(CONV_WIDTH)), b)


def rglru(x, w_a, b_a, w_x, b_x, lam):
    bsz, s, _ = x.shape
    xh = x.reshape(bsz, s, LRU_HEADS, LRU_HEAD_DIM)
    r = jax.nn.sigmoid(jnp.einsum('bshi,hij->bshj', xh, w_a).reshape(bsz, s, LRU_WIDTH) + b_a)
    i = jax.nn.sigmoid(jnp.einsum('bshi,hij->bshj', xh, w_x).reshape(bsz, s, LRU_WIDTH) + b_x)
    log_a = (-LRU_C * r.astype(jnp.float32)) * jax.nn.softplus(-lam.astype(jnp.float32))
    a = jnp.exp(log_a)
    u = jnp.sqrt(-jnp.expm1(2.0 * log_a)) * (i * x).astype(jnp.float32)

    def combine(left, right):
        a1, b1 = left
        a2, b2 = right
        return a1 * a2, a2 * b1 + b2

    _, h = lax.associative_scan(combine, (a, u), axis=1)
    return h.astype(x.dtype)


def ssd_chunked_scan(xs, dt, A, Bm, Cm):
    bsz, s = xs.shape[:2]
    nc, L = s // SSD_CHUNK, SSD_CHUNK

    def to_chunks(t):
        return jnp.moveaxis(t.reshape((bsz, nc, L) + t.shape[2:]), 1, 0)

    xdt = (xs * dt[..., None]).reshape(bsz, s, SSD_GROUPS, SSD_HPG, SSD_HEAD_DIM)
    dA = (dt * A).reshape(bsz, s, SSD_GROUPS, SSD_HPG)
    causal = jnp.tril(jnp.ones((L, L), dtype=bool))[None, :, :, None, None]

    def step(state, inp):
        x_c, dA_c, B_c, C_c = inp
        cs = jnp.cumsum(dA_c, axis=1)
        seg = cs[:, :, None] - cs[:, None, :]
        decay = jnp.exp(jnp.where(causal, seg, -jnp.inf))
        cb = jnp.einsum('blgn,bsgn->blsg', C_c, B_c)
        y_diag = jnp.einsum('blsge,bsgep->blgep', cb[..., None] * decay, x_c)
        y_off = jnp.einsum('blgn,bgepn->blgep', C_c, state) * jnp.exp(cs)[..., None]
        decay_to_end = jnp.exp(cs[:, -1:] - cs)
        new_state = state * jnp.exp(cs[:, -1])[..., None, None] + jnp.einsum(
            'blgn,blgep->bgepn', B_c, x_c * decay_to_end[..., None])
        return new_state, y_diag + y_off

    init = jnp.zeros((bsz, SSD_GROUPS, SSD_HPG, SSD_HEAD_DIM, SSD_STATE), jnp.float32)
    _, y = lax.scan(step, init, (to_chunks(xdt), to_chunks(dA), to_chunks(Bm), to_chunks(Cm)))
    return jnp.moveaxis(y, 0, 1).reshape(bsz, s, SSD_HEADS, SSD_HEAD_DIM)


def ssd_branch(z, xbc, dt_raw, conv_w, conv_b, dt_bias, a_log, d_skip, norm_w):
    bsz, s, _ = z.shape
    xbc = jax.nn.silu(causal_dwconv(xbc, conv_w, conv_b))
    xs, Bm, Cm = jnp.split(xbc, [SSD_INNER, SSD_INNER + SSD_GROUPS * SSD_STATE], axis=-1)
    xs = xs.astype(jnp.float32).reshape(bsz, s, SSD_HEADS, SSD_HEAD_DIM)
    Bm = Bm.astype(jnp.float32).reshape(bsz, s, SSD_GROUPS, SSD_STATE)
    Cm = Cm.astype(jnp.float32).reshape(bsz, s, SSD_GROUPS, SSD_STATE)
    dt = jax.nn.softplus(dt_raw.astype(jnp.float32) + dt_bias.astype(jnp.float32))
    A = -jnp.exp(a_log.astype(jnp.float32))
    y = ssd_chunked_scan(xs, dt, A, Bm, Cm)
    y = y + d_skip.astype(jnp.float32)[:, None] * xs
    y = y.reshape(bsz, s, SSD_INNER) * jax.nn.silu(z.astype(jnp.float32))
    return grouped_rmsnorm(y, norm_w).astype(z.dtype)


def temporal_mixer(h, w_in, b_gate, lru_conv_w, lru_conv_b, lru_wa, lru_ba, lru_wx, lru_bx,
                   lru_lambda, w_pa, ssd_conv_w, ssd_conv_b, ssd_dt_bias, ssd_a_log, ssd_d,
                   ssd_norm_w, w_pb, w_out):
    bsz, s, _ = h.shape
    proj = h @ w_in
    lru_x, lru_g, ssd_z, ssd_xbc, ssd_dt, gates = jnp.split(
        proj, [int(v) for v in np.cumsum(IN_SIZES)[:-1]], axis=-1)
    ra = rglru(causal_dwconv(lru_x, lru_conv_w, lru_conv_b), lru_wa, lru_ba, lru_wx, lru_bx, lru_lambda)
    y_a = (ra * jax.nn.gelu(lru_g)) @ w_pa
    y_b = ssd_branch(ssd_z, ssd_xbc, ssd_dt, ssd_conv_w, ssd_conv_b, ssd_dt_bias, ssd_a_log,
                     ssd_d, ssd_norm_w) @ w_pb
    g = jax.nn.sigmoid(gates + b_gate).reshape(bsz, s, N_BRANCH, D_MODEL)
    merged = g[:, :, 0] * y_a + g[:, :, 1] * y_b
    return merged @ w_out


def _fwd_setup_inputs(seed: int = 0) -> dict:
    key = jax.random.key(seed)
    ks = jax.random.split(key, 32)
    f32 = jnp.float32

    def nrm(k, shape, scale):
        return jax.random.normal(k, shape, f32) * scale

    def gain(k, shape):
        return 1.0 + 0.1 * jax.random.normal(k, shape, f32)

    L = DEPTH
    u = jax.random.uniform(ks[12], (L, LRU_WIDTH), f32, minval=0.9, maxval=0.999)
    sroot = u ** (1.0 / LRU_C)
    lru_lambda = jnp.log(sroot) - jnp.log1p(-sroot)
    dt0 = jnp.exp(jax.random.uniform(ks[16], (L, SSD_HEADS), f32,
                                     minval=math.log(1e-3), maxval=math.log(1e-1)))
    ssd_dt_bias = dt0 + jnp.log(-jnp.expm1(-dt0))
    ssd_a_log = jnp.log(jax.random.uniform(ks[17], (L, SSD_HEADS), f32, minval=1.0, maxval=16.0))
    return {
        "x": jax.random.normal(ks[0], (BATCH, SEQ, D_MODEL), f32),
        "c": jax.random.normal(ks[1], (BATCH, D_MODEL), f32),
        "w_ada": nrm(ks[2], (L, D_MODEL, N_MOD * D_MODEL), 0.5 * D_MODEL ** -0.5),
        "b_ada": nrm(ks[3], (L, N_MOD * D_MODEL), 0.01),
        "pre_norm1": gain(ks[4], (L, D_MODEL)),
        "post_norm1": gain(ks[5], (L, D_MODEL)),
        "w_in": nrm(ks[6], (L, D_MODEL, IN_DIM), D_MODEL ** -0.5),
        "b_gate": nrm(ks[7], (L, N_BRANCH * D_MODEL), 0.01),
        "lru_conv_w": nrm(ks[8], (L, CONV_WIDTH, LRU_WIDTH), CONV_WIDTH ** -0.5),
        "lru_conv_b": nrm(ks[9], (L, LRU_WIDTH), 0.01),
        "lru_wa": nrm(ks[10], (L, LRU_HEADS, LRU_HEAD_DIM, LRU_HEAD_DIM), LRU_HEAD_DIM ** -0.5),
        "lru_ba": nrm(ks[11], (L, LRU_WIDTH), 0.01),
        "lru_wx": nrm(ks[13], (L, LRU_HEADS, LRU_HEAD_DIM, LRU_HEAD_DIM), LRU_HEAD_DIM ** -0.5),
        "lru_bx": nrm(ks[14], (L, LRU_WIDTH), 0.01),
        "lru_lambda": lru_lambda,
        "w_pa": nrm(ks[15], (L, LRU_WIDTH, D_MODEL), LRU_WIDTH ** -0.5),
        "ssd_conv_w": nrm(ks[18], (L, CONV_WIDTH, SSD_CONV_DIM), CONV_WIDTH ** -0.5),
        "ssd_conv_b": nrm(ks[19], (L, SSD_CONV_DIM), 0.01),
        "ssd_dt_bias": ssd_dt_bias,
        "ssd_a_log": ssd_a_log,
        "ssd_d": gain(ks[20], (L, SSD_HEADS)),
        "ssd_norm_w": gain(ks[21], (L, SSD_INNER)),
        "w_pb": nrm(ks[22], (L, SSD_INNER, D_MODEL), SSD_INNER ** -0.5),
        "w_out": nrm(ks[23], (L, D_MODEL, D_MODEL), D_MODEL ** -0.5),
        "pre_norm2": gain(ks[24], (L, D_MODEL)),
        "post_norm2": gain(ks[25], (L, D_MODEL)),
        "w_ff1": nrm(ks[26], (L, D_MODEL, D_FF), D_MODEL ** -0.5),
        "w_ff2": nrm(ks[27], (L, D_FF, D_MODEL), D_FF ** -0.5),
    }


def _fwd_reference(x, c, w_ada, b_ada, pre_norm1, post_norm1, w_in, b_gate, lru_conv_w, lru_conv_b,
              lru_wa, lru_ba, lru_wx, lru_bx, lru_lambda, w_pa, ssd_conv_w, ssd_conv_b,
              ssd_dt_bias, ssd_a_log, ssd_d, ssd_norm_w, w_pb, w_out, pre_norm2, post_norm2,
              w_ff1, w_ff2):
    c_act = jax.nn.silu(c)
    for l in range(DEPTH):
        mod = (c_act @ w_ada[l] + b_ada[l])[:, None, :]
        sh1, sc1, g1, sh2, sc2, g2 = jnp.split(mod, N_MOD, axis=-1)
        h = rmsnorm(x, pre_norm1[l]) * (1.0 + sc1) + sh1
        y = temporal_mixer(h, w_in[l], b_gate[l], lru_conv_w[l], lru_conv_b[l], lru_wa[l],
                           lru_ba[l], lru_wx[l], lru_bx[l], lru_lambda[l], w_pa[l],
                           ssd_conv_w[l], ssd_conv_b[l], ssd_dt_bias[l], ssd_a_log[l], ssd_d[l],
                           ssd_norm_w[l], w_pb[l], w_out[l])
        x = x + g1 * rmsnorm(y, post_norm1[l])
        h = rmsnorm(x, pre_norm2[l]) * (1.0 + sc2) + sh2
        y = jnp.square(jax.nn.relu(h @ w_ff1[l])) @ w_ff2[l]
        x = x + g2 * rmsnorm(y, post_norm2[l])
    return x


import jax as _jax
import jax.numpy as _jnp

TWIN_FORMAT = 'train_step'
FWD_PARAMS = ['x', 'c', 'w_ada', 'b_ada', 'pre_norm1', 'post_norm1', 'w_in', 'b_gate', 'lru_conv_w', 'lru_conv_b', 'lru_wa', 'lru_ba', 'lru_wx', 'lru_bx', 'lru_lambda', 'w_pa', 'ssd_conv_w', 'ssd_conv_b', 'ssd_dt_bias', 'ssd_a_log', 'ssd_d', 'ssd_norm_w', 'w_pb', 'w_out', 'pre_norm2', 'post_norm2', 'w_ff1', 'w_ff2']
TWIN_WEIGHTS = ['w_ada', 'b_ada', 'pre_norm1', 'post_norm1', 'w_in', 'b_gate', 'lru_conv_w', 'lru_conv_b', 'lru_wa', 'lru_ba', 'lru_wx', 'lru_bx', 'lru_lambda', 'w_pa', 'ssd_conv_w', 'ssd_conv_b', 'ssd_dt_bias', 'ssd_a_log', 'ssd_d', 'ssd_norm_w', 'w_pb', 'w_out', 'pre_norm2', 'post_norm2', 'w_ff1', 'w_ff2']
TWIN_DIFF_INPUT = 'x'
TWIN_INPUTS = ['x', 'c', 'w_ada', 'b_ada', 'pre_norm1', 'post_norm1', 'w_in', 'b_gate', 'lru_conv_w', 'lru_conv_b', 'lru_wa', 'lru_ba', 'lru_wx', 'lru_bx', 'lru_lambda', 'w_pa', 'ssd_conv_w', 'ssd_conv_b', 'ssd_dt_bias', 'ssd_a_log', 'ssd_d', 'ssd_norm_w', 'w_pb', 'w_out', 'pre_norm2', 'post_norm2', 'w_ff1', 'w_ff2', 'loss_target', 'm_w_ada', 'm_b_ada', 'm_pre_norm1', 'm_post_norm1', 'm_w_in', 'm_b_gate', 'm_lru_conv_w', 'm_lru_conv_b', 'm_lru_wa', 'm_lru_ba', 'm_lru_wx', 'm_lru_bx', 'm_lru_lambda', 'm_w_pa', 'm_ssd_conv_w', 'm_ssd_conv_b', 'm_ssd_dt_bias', 'm_ssd_a_log', 'm_ssd_d', 'm_ssd_norm_w', 'm_w_pb', 'm_w_out', 'm_pre_norm2', 'm_post_norm2', 'm_w_ff1', 'm_w_ff2', 'v_w_ada', 'v_b_ada', 'v_pre_norm1', 'v_post_norm1', 'v_w_in', 'v_b_gate', 'v_lru_conv_w', 'v_lru_conv_b', 'v_lru_wa', 'v_lru_ba', 'v_lru_wx', 'v_lru_bx', 'v_lru_lambda', 'v_w_pa', 'v_ssd_conv_w', 'v_ssd_conv_b', 'v_ssd_dt_bias', 'v_ssd_a_log', 'v_ssd_d', 'v_ssd_norm_w', 'v_w_pb', 'v_w_out', 'v_pre_norm2', 'v_post_norm2', 'v_w_ff1', 'v_w_ff2']
TWIN_OUTPUTS = ['loss', 'grad_x', 'grad_w_ada', 'grad_b_ada', 'grad_pre_norm1', 'grad_post_norm1', 'grad_w_in', 'grad_b_gate', 'grad_lru_conv_w', 'grad_lru_conv_b', 'grad_lru_wa', 'grad_lru_ba', 'grad_lru_wx', 'grad_lru_bx', 'grad_lru_lambda', 'grad_w_pa', 'grad_ssd_conv_w', 'grad_ssd_conv_b', 'grad_ssd_dt_bias', 'grad_ssd_a_log', 'grad_ssd_d', 'grad_ssd_norm_w', 'grad_w_pb', 'grad_w_out', 'grad_pre_norm2', 'grad_post_norm2', 'grad_w_ff1', 'grad_w_ff2', 'delta_w_ada', 'delta_b_ada', 'delta_pre_norm1', 'delta_post_norm1', 'delta_w_in', 'delta_b_gate', 'delta_lru_conv_w', 'delta_lru_conv_b', 'delta_lru_wa', 'delta_lru_ba', 'delta_lru_wx', 'delta_lru_bx', 'delta_lru_lambda', 'delta_w_pa', 'delta_ssd_conv_w', 'delta_ssd_conv_b', 'delta_ssd_dt_bias', 'delta_ssd_a_log', 'delta_ssd_d', 'delta_ssd_norm_w', 'delta_w_pb', 'delta_w_out', 'delta_pre_norm2', 'delta_post_norm2', 'delta_w_ff1', 'delta_w_ff2', 'new_m_w_ada', 'new_m_b_ada', 'new_m_pre_norm1', 'new_m_post_norm1', 'new_m_w_in', 'new_m_b_gate', 'new_m_lru_conv_w', 'new_m_lru_conv_b', 'new_m_lru_wa', 'new_m_lru_ba', 'new_m_lru_wx', 'new_m_lru_bx', 'new_m_lru_lambda', 'new_m_w_pa', 'new_m_ssd_conv_w', 'new_m_ssd_conv_b', 'new_m_ssd_dt_bias', 'new_m_ssd_a_log', 'new_m_ssd_d', 'new_m_ssd_norm_w', 'new_m_w_pb', 'new_m_w_out', 'new_m_pre_norm2', 'new_m_post_norm2', 'new_m_w_ff1', 'new_m_w_ff2', 'new_v_w_ada', 'new_v_b_ada', 'new_v_pre_norm1', 'new_v_post_norm1', 'new_v_w_in', 'new_v_b_gate', 'new_v_lru_conv_w', 'new_v_lru_conv_b', 'new_v_lru_wa', 'new_v_lru_ba', 'new_v_lru_wx', 'new_v_lru_bx', 'new_v_lru_lambda', 'new_v_w_pa', 'new_v_ssd_conv_w', 'new_v_ssd_conv_b', 'new_v_ssd_dt_bias', 'new_v_ssd_a_log', 'new_v_ssd_d', 'new_v_ssd_norm_w', 'new_v_w_pb', 'new_v_w_out', 'new_v_pre_norm2', 'new_v_post_norm2', 'new_v_w_ff1', 'new_v_w_ff2']
TWIN_LEAF_KINDS = {'loss': 'loss', 'grad_x': 'grad_x', 'grad_w_ada': 'grad_w', 'grad_b_ada': 'grad_w', 'grad_pre_norm1': 'grad_w', 'grad_post_norm1': 'grad_w', 'grad_w_in': 'grad_w', 'grad_b_gate': 'grad_w', 'grad_lru_conv_w': 'grad_w', 'grad_lru_conv_b': 'grad_w', 'grad_lru_wa': 'grad_w', 'grad_lru_ba': 'grad_w', 'grad_lru_wx': 'grad_w', 'grad_lru_bx': 'grad_w', 'grad_lru_lambda': 'grad_w', 'grad_w_pa': 'grad_w', 'grad_ssd_conv_w': 'grad_w', 'grad_ssd_conv_b': 'grad_w', 'grad_ssd_dt_bias': 'grad_w', 'grad_ssd_a_log': 'grad_w', 'grad_ssd_d': 'grad_w', 'grad_ssd_norm_w': 'grad_w', 'grad_w_pb': 'grad_w', 'grad_w_out': 'grad_w', 'grad_pre_norm2': 'grad_w', 'grad_post_norm2': 'grad_w', 'grad_w_ff1': 'grad_w', 'grad_w_ff2': 'grad_w', 'delta_w_ada': 'delta_w', 'delta_b_ada': 'delta_w', 'delta_pre_norm1': 'delta_w', 'delta_post_norm1': 'delta_w', 'delta_w_in': 'delta_w', 'delta_b_gate': 'delta_w', 'delta_lru_conv_w': 'delta_w', 'delta_lru_conv_b': 'delta_w', 'delta_lru_wa': 'delta_w', 'delta_lru_ba': 'delta_w', 'delta_lru_wx': 'delta_w', 'delta_lru_bx': 'delta_w', 'delta_lru_lambda': 'delta_w', 'delta_w_pa': 'delta_w', 'delta_ssd_conv_w': 'delta_w', 'delta_ssd_conv_b': 'delta_w', 'delta_ssd_dt_bias': 'delta_w', 'delta_ssd_a_log': 'delta_w', 'delta_ssd_d': 'delta_w', 'delta_ssd_norm_w': 'delta_w', 'delta_w_pb': 'delta_w', 'delta_w_out': 'delta_w', 'delta_pre_norm2': 'delta_w', 'delta_post_norm2': 'delta_w', 'delta_w_ff1': 'delta_w', 'delta_w_ff2': 'delta_w', 'new_m_w_ada': 'new_m', 'new_m_b_ada': 'new_m', 'new_m_pre_norm1': 'new_m', 'new_m_post_norm1': 'new_m', 'new_m_w_in': 'new_m', 'new_m_b_gate': 'new_m', 'new_m_lru_conv_w': 'new_m', 'new_m_lru_conv_b': 'new_m', 'new_m_lru_wa': 'new_m', 'new_m_lru_ba': 'new_m', 'new_m_lru_wx': 'new_m', 'new_m_lru_bx': 'new_m', 'new_m_lru_lambda': 'new_m', 'new_m_w_pa': 'new_m', 'new_m_ssd_conv_w': 'new_m', 'new_m_ssd_conv_b': 'new_m', 'new_m_ssd_dt_bias': 'new_m', 'new_m_ssd_a_log': 'new_m', 'new_m_ssd_d': 'new_m', 'new_m_ssd_norm_w': 'new_m', 'new_m_w_pb': 'new_m', 'new_m_w_out': 'new_m', 'new_m_pre_norm2': 'new_m', 'new_m_post_norm2': 'new_m', 'new_m_w_ff1': 'new_m', 'new_m_w_ff2': 'new_m', 'new_v_w_ada': 'new_v', 'new_v_b_ada': 'new_v', 'new_v_pre_norm1': 'new_v', 'new_v_post_norm1': 'new_v', 'new_v_w_in': 'new_v', 'new_v_b_gate': 'new_v', 'new_v_lru_conv_w': 'new_v', 'new_v_lru_conv_b': 'new_v', 'new_v_lru_wa': 'new_v', 'new_v_lru_ba': 'new_v', 'new_v_lru_wx': 'new_v', 'new_v_lru_bx': 'new_v', 'new_v_lru_lambda': 'new_v', 'new_v_w_pa': 'new_v', 'new_v_ssd_conv_w': 'new_v', 'new_v_ssd_conv_b': 'new_v', 'new_v_ssd_dt_bias': 'new_v', 'new_v_ssd_a_log': 'new_v', 'new_v_ssd_d': 'new_v', 'new_v_ssd_norm_w': 'new_v', 'new_v_w_pb': 'new_v', 'new_v_w_out': 'new_v', 'new_v_pre_norm2': 'new_v', 'new_v_post_norm2': 'new_v', 'new_v_w_ff1': 'new_v', 'new_v_w_ff2': 'new_v'}


def _forward(args):
    return _fwd_reference(*[args[k] for k in FWD_PARAMS])


def _output_shape():
    out = _jax.eval_shape(lambda: _forward(_fwd_setup_inputs(0)))
    return out.shape, out.dtype

N_MICROBATCH = 1
ADAM_LR = 0.001
ADAM_B1 = 0.9
ADAM_B2 = 0.999
ADAM_EPS = 1e-08
ADAM_WD = 0.01
ADAM_STEP = 10
PER_EXAMPLE_BATCH_AXIS = {'x': 0, 'c': 0, 'loss_target': 0}
SHARED_INPUTS = []
_WEIGHT_DTYPES = {'w_ada': _jnp.float32, 'b_ada': _jnp.float32, 'pre_norm1': _jnp.float32, 'post_norm1': _jnp.float32, 'w_in': _jnp.float32, 'b_gate': _jnp.float32, 'lru_conv_w': _jnp.float32, 'lru_conv_b': _jnp.float32, 'lru_wa': _jnp.float32, 'lru_ba': _jnp.float32, 'lru_wx': _jnp.float32, 'lru_bx': _jnp.float32, 'lru_lambda': _jnp.float32, 'w_pa': _jnp.float32, 'ssd_conv_w': _jnp.float32, 'ssd_conv_b': _jnp.float32, 'ssd_dt_bias': _jnp.float32, 'ssd_a_log': _jnp.float32, 'ssd_d': _jnp.float32, 'ssd_norm_w': _jnp.float32, 'w_pb': _jnp.float32, 'w_out': _jnp.float32, 'pre_norm2': _jnp.float32, 'post_norm2': _jnp.float32, 'w_ff1': _jnp.float32, 'w_ff2': _jnp.float32}
MOMENT_SCALE = {'w_ada': 3.775916e+00, 'b_ada': 7.054629e+00, 'pre_norm1': 1.779329e-01, 'post_norm1': 7.592230e+00, 'w_in': 1.917411e-01, 'b_gate': 1.907433e-01, 'lru_conv_w': 7.468987e-01, 'lru_conv_b': 2.598418e+00, 'lru_wa': 7.557950e-02, 'lru_ba': 1.250382e-01, 'lru_wx': 1.858841e-01, 'lru_bx': 2.662810e-01, 'lru_lambda': 2.996617e-01, 'w_pa': 7.857701e-01, 'ssd_conv_w': 6.391345e-02, 'ssd_conv_b': 1.578449e-01, 'ssd_dt_bias': 1.158064e-01, 'ssd_a_log': 5.222633e-01, 'ssd_d': 3.145548e-01, 'ssd_norm_w': 1.345043e-01, 'w_pb': 1.801664e-01, 'w_out': 6.601359e-01, 'pre_norm2': 2.061128e-01, 'post_norm2': 7.749385e+00, 'w_ff1': 1.585389e-01, 'w_ff2': 7.904445e-01}


def _to_microbatches(a, axis):
    t = _jnp.moveaxis(a, axis, 0)
    t = t.reshape((N_MICROBATCH, t.shape[0] // N_MICROBATCH) + t.shape[1:])
    return _jnp.moveaxis(t, 1, axis + 1)


def setup_inputs(seed: int = 0) -> dict:
    inp = _fwd_setup_inputs(seed)
    key = _jax.random.fold_in(_jax.random.key(seed), 7919)
    shape, _ = _output_shape()
    out = dict(inp)
    out["loss_target"] = _jax.random.normal(_jax.random.fold_in(key, 0), shape, _jnp.float32)
    for i, name in enumerate(TWIN_WEIGHTS):
        w = inp[name].astype(_jnp.float32)
        if MOMENT_SCALE is None:
            s = _jnp.sqrt(_jnp.mean(_jnp.square(w)) + 1e-30)
        else:
            s = MOMENT_SCALE[name]
        km, kv = _jax.random.split(_jax.random.fold_in(key, i + 1))
        out[name] = w
        out["m_" + name] = s * _jax.random.normal(km, w.shape, _jnp.float32)
        out["v_" + name] = (s * s) * _jax.random.uniform(kv, w.shape, _jnp.float32, 0.5, 1.5)
    if N_MICROBATCH > 1:
        for name, axis in PER_EXAMPLE_BATCH_AXIS.items():
            out[name] = _to_microbatches(out[name], axis)
    return {'x': out['x'], 'c': out['c'], 'w_ada': out['w_ada'], 'b_ada': out['b_ada'], 'pre_norm1': out['pre_norm1'], 'post_norm1': out['post_norm1'], 'w_in': out['w_in'], 'b_gate': out['b_gate'], 'lru_conv_w': out['lru_conv_w'], 'lru_conv_b': out['lru_conv_b'], 'lru_wa': out['lru_wa'], 'lru_ba': out['lru_ba'], 'lru_wx': out['lru_wx'], 'lru_bx': out['lru_bx'], 'lru_lambda': out['lru_lambda'], 'w_pa': out['w_pa'], 'ssd_conv_w': out['ssd_conv_w'], 'ssd_conv_b': out['ssd_conv_b'], 'ssd_dt_bias': out['ssd_dt_bias'], 'ssd_a_log': out['ssd_a_log'], 'ssd_d': out['ssd_d'], 'ssd_norm_w': out['ssd_norm_w'], 'w_pb': out['w_pb'], 'w_out': out['w_out'], 'pre_norm2': out['pre_norm2'], 'post_norm2': out['post_norm2'], 'w_ff1': out['w_ff1'], 'w_ff2': out['w_ff2'], 'loss_target': out['loss_target'], 'm_w_ada': out['m_w_ada'], 'm_b_ada': out['m_b_ada'], 'm_pre_norm1': out['m_pre_norm1'], 'm_post_norm1': out['m_post_norm1'], 'm_w_in': out['m_w_in'], 'm_b_gate': out['m_b_gate'], 'm_lru_conv_w': out['m_lru_conv_w'], 'm_lru_conv_b': out['m_lru_conv_b'], 'm_lru_wa': out['m_lru_wa'], 'm_lru_ba': out['m_lru_ba'], 'm_lru_wx': out['m_lru_wx'], 'm_lru_bx': out['m_lru_bx'], 'm_lru_lambda': out['m_lru_lambda'], 'm_w_pa': out['m_w_pa'], 'm_ssd_conv_w': out['m_ssd_conv_w'], 'm_ssd_conv_b': out['m_ssd_conv_b'], 'm_ssd_dt_bias': out['m_ssd_dt_bias'], 'm_ssd_a_log': out['m_ssd_a_log'], 'm_ssd_d': out['m_ssd_d'], 'm_ssd_norm_w': out['m_ssd_norm_w'], 'm_w_pb': out['m_w_pb'], 'm_w_out': out['m_w_out'], 'm_pre_norm2': out['m_pre_norm2'], 'm_post_norm2': out['m_post_norm2'], 'm_w_ff1': out['m_w_ff1'], 'm_w_ff2': out['m_w_ff2'], 'v_w_ada': out['v_w_ada'], 'v_b_ada': out['v_b_ada'], 'v_pre_norm1': out['v_pre_norm1'], 'v_post_norm1': out['v_post_norm1'], 'v_w_in': out['v_w_in'], 'v_b_gate': out['v_b_gate'], 'v_lru_conv_w': out['v_lru_conv_w'], 'v_lru_conv_b': out['v_lru_conv_b'], 'v_lru_wa': out['v_lru_wa'], 'v_lru_ba': out['v_lru_ba'], 'v_lru_wx': out['v_lru_wx'], 'v_lru_bx': out['v_lru_bx'], 'v_lru_lambda': out['v_lru_lambda'], 'v_w_pa': out['v_w_pa'], 'v_ssd_conv_w': out['v_ssd_conv_w'], 'v_ssd_conv_b': out['v_ssd_conv_b'], 'v_ssd_dt_bias': out['v_ssd_dt_bias'], 'v_ssd_a_log': out['v_ssd_a_log'], 'v_ssd_d': out['v_ssd_d'], 'v_ssd_norm_w': out['v_ssd_norm_w'], 'v_w_pb': out['v_w_pb'], 'v_w_out': out['v_w_out'], 'v_pre_norm2': out['v_pre_norm2'], 'v_post_norm2': out['v_post_norm2'], 'v_w_ff1': out['v_w_ff1'], 'v_w_ff2': out['v_w_ff2']}


def _loss(weights, diff, rest, loss_target):
    with _jax.named_scope("forward"):
        args = {**rest, TWIN_DIFF_INPUT: diff, **{k: w.astype(_WEIGHT_DTYPES[k]) for k, w in weights.items()}}
        y = _forward(args)
    with _jax.named_scope("loss_head"):
        err = _jnp.square(y.astype(_jnp.float32) - loss_target)
        return 0.5 * _jnp.sum(_jnp.mean(err, axis=-1)) if err.ndim else 0.5 * err


def _adamw(w, g, m, v):
    m = ADAM_B1 * m + (1.0 - ADAM_B1) * g
    v = ADAM_B2 * v + (1.0 - ADAM_B2) * _jnp.square(g)
    m_hat = m / (1.0 - ADAM_B1 ** ADAM_STEP)
    v_hat = v / (1.0 - ADAM_B2 ** ADAM_STEP)
    delta = -ADAM_LR * (m_hat / (_jnp.sqrt(v_hat) + ADAM_EPS) + ADAM_WD * w)
    return delta, m, v


def reference(x, c, w_ada, b_ada, pre_norm1, post_norm1, w_in, b_gate, lru_conv_w, lru_conv_b, lru_wa, lru_ba, lru_wx, lru_bx, lru_lambda, w_pa, ssd_conv_w, ssd_conv_b, ssd_dt_bias, ssd_a_log, ssd_d, ssd_norm_w, w_pb, w_out, pre_norm2, post_norm2, w_ff1, w_ff2, loss_target, m_w_ada, m_b_ada, m_pre_norm1, m_post_norm1, m_w_in, m_b_gate, m_lru_conv_w, m_lru_conv_b, m_lru_wa, m_lru_ba, m_lru_wx, m_lru_bx, m_lru_lambda, m_w_pa, m_ssd_conv_w, m_ssd_conv_b, m_ssd_dt_bias, m_ssd_a_log, m_ssd_d, m_ssd_norm_w, m_w_pb, m_w_out, m_pre_norm2, m_post_norm2, m_w_ff1, m_w_ff2, v_w_ada, v_b_ada, v_pre_norm1, v_post_norm1, v_w_in, v_b_gate, v_lru_conv_w, v_lru_conv_b, v_lru_wa, v_lru_ba, v_lru_wx, v_lru_bx, v_lru_lambda, v_w_pa, v_ssd_conv_w, v_ssd_conv_b, v_ssd_dt_bias, v_ssd_a_log, v_ssd_d, v_ssd_norm_w, v_w_pb, v_w_out, v_pre_norm2, v_post_norm2, v_w_ff1, v_w_ff2):
    given = dict(x=x, c=c, w_ada=w_ada, b_ada=b_ada, pre_norm1=pre_norm1, post_norm1=post_norm1, w_in=w_in, b_gate=b_gate, lru_conv_w=lru_conv_w, lru_conv_b=lru_conv_b, lru_wa=lru_wa, lru_ba=lru_ba, lru_wx=lru_wx, lru_bx=lru_bx, lru_lambda=lru_lambda, w_pa=w_pa, ssd_conv_w=ssd_conv_w, ssd_conv_b=ssd_conv_b, ssd_dt_bias=ssd_dt_bias, ssd_a_log=ssd_a_log, ssd_d=ssd_d, ssd_norm_w=ssd_norm_w, w_pb=w_pb, w_out=w_out, pre_norm2=pre_norm2, post_norm2=post_norm2, w_ff1=w_ff1, w_ff2=w_ff2, loss_target=loss_target, m_w_ada=m_w_ada, m_b_ada=m_b_ada, m_pre_norm1=m_pre_norm1, m_post_norm1=m_post_norm1, m_w_in=m_w_in, m_b_gate=m_b_gate, m_lru_conv_w=m_lru_conv_w, m_lru_conv_b=m_lru_conv_b, m_lru_wa=m_lru_wa, m_lru_ba=m_lru_ba, m_lru_wx=m_lru_wx, m_lru_bx=m_lru_bx, m_lru_lambda=m_lru_lambda, m_w_pa=m_w_pa, m_ssd_conv_w=m_ssd_conv_w, m_ssd_conv_b=m_ssd_conv_b, m_ssd_dt_bias=m_ssd_dt_bias, m_ssd_a_log=m_ssd_a_log, m_ssd_d=m_ssd_d, m_ssd_norm_w=m_ssd_norm_w, m_w_pb=m_w_pb, m_w_out=m_w_out, m_pre_norm2=m_pre_norm2, m_post_norm2=m_post_norm2, m_w_ff1=m_w_ff1, m_w_ff2=m_w_ff2, v_w_ada=v_w_ada, v_b_ada=v_b_ada, v_pre_norm1=v_pre_norm1, v_post_norm1=v_post_norm1, v_w_in=v_w_in, v_b_gate=v_b_gate, v_lru_conv_w=v_lru_conv_w, v_lru_conv_b=v_lru_conv_b, v_lru_wa=v_lru_wa, v_lru_ba=v_lru_ba, v_lru_wx=v_lru_wx, v_lru_bx=v_lru_bx, v_lru_lambda=v_lru_lambda, v_w_pa=v_w_pa, v_ssd_conv_w=v_ssd_conv_w, v_ssd_conv_b=v_ssd_conv_b, v_ssd_dt_bias=v_ssd_dt_bias, v_ssd_a_log=v_ssd_a_log, v_ssd_d=v_ssd_d, v_ssd_norm_w=v_ssd_norm_w, v_w_pb=v_w_pb, v_w_out=v_w_out, v_pre_norm2=v_pre_norm2, v_post_norm2=v_post_norm2, v_w_ff1=v_w_ff1, v_w_ff2=v_w_ff2)
    weights = {n: given[n] for n in TWIN_WEIGHTS}
    shared = {n: given[n] for n in SHARED_INPUTS}
    per_example = {n: given[n] for n in ['x', 'c']}
    grad_fn = _jax.value_and_grad(_loss, argnums=(0, 1))

    def one_microbatch(ex, loss_target):
        ex = dict(ex)
        diff = ex.pop(TWIN_DIFF_INPUT)
        return grad_fn(weights, diff, {**shared, **ex}, loss_target)

    if N_MICROBATCH == 1:
        loss, (grad_w, grad_x) = one_microbatch(per_example, given["loss_target"])
    else:
        def body(carry, xs):
            loss_sum, grad_sum = carry
            l_k, (gw_k, gx_k) = one_microbatch(xs[0], xs[1])
            with _jax.named_scope("update"):
                return (loss_sum + l_k, _jax.tree.map(_jnp.add, grad_sum, gw_k)), gx_k

        init = (_jnp.zeros((), _jnp.float32), _jax.tree.map(_jnp.zeros_like, weights))
        (loss, grad_w), grad_x = _jax.lax.scan(body, init, (per_example, given["loss_target"]))
    with _jax.named_scope("update"):
        delta_w, new_m, new_v = {}, {}, {}
        for n in TWIN_WEIGHTS:
            delta_w[n], new_m[n], new_v[n] = _adamw(weights[n], grad_w[n], given["m_" + n], given["v_" + n])
    return (loss, grad_x, *[grad_w[n] for n in TWIN_WEIGHTS], *[delta_w[n] for n in TWIN_WEIGHTS],
            *[new_m[n] for n in TWIN_WEIGHTS], *[new_v[n] for n in TWIN_WEIGHTS])
```

```python
import functools

import numpy as np
import jax
import jax.numpy as jnp
from jax import lax
from jax.experimental import pallas as pl
from jax.experimental.pallas import tpu as pltpu

F32 = jnp.float32
BF16 = jnp.bfloat16

D_MODEL = 1024
LRU_HEADS = 16
LRU_BLOCK = 256
LRU_C = 8.0
SSD_INNER = 2048
SSD_HEADS = 32
SSD_P = 64
SSD_G = 8
SSD_N = 128
SSD_L = 128
SSD_GW = SSD_INNER // SSD_G
D_FF = 4096
N_MOD = 6
EPS = 1e-6
NDEV = 8

C_LRU_X, C_LRU_G, C_Z, C_XBC, C_GATES, PROJ_MAIN = 0, 1024, 2048, 4096, 8192, 10240
IN_DIM = 10272
DT_COL0 = 8192
HALO = 16

ADAM_LR, ADAM_B1, ADAM_B2, ADAM_EPS, ADAM_WD, ADAM_STEP = 0.001, 0.9, 0.999, 1e-08, 0.01, 10

VMEM_LIMIT = 60 * 1024 * 1024
MESH = pl.DeviceIdType.MESH
ANY = pl.BlockSpec(memory_space=pl.ANY)
VMEM_FULL = pl.BlockSpec(memory_space=pltpu.VMEM)


def _cp(*sem):
    return pltpu.CompilerParams(dimension_semantics=sem, vmem_limit_bytes=VMEM_LIMIT)


def _dot(a, b):
    return jnp.dot(a, b, preferred_element_type=F32)


def _dot_nt(a, b):
    return lax.dot_general(a, b, (((1,), (1,)), ((), ())), preferred_element_type=F32)


def _dot_tn(a, b):
    return lax.dot_general(a, b, (((0,), (0,)), ((), ())), preferred_element_type=F32)


def _dot_hi(a, b):
    return jnp.dot(a, b, precision=lax.Precision.HIGHEST, preferred_element_type=F32)


def _sigmoid(x):
    return 1.0 / (1.0 + jnp.exp(-x))


def _gelu_and_grad(x):
    k0, k1 = 0.7978845608028654, 0.044715
    t = jnp.tanh(k0 * (x + k1 * x * x * x))
    g = 0.5 * x * (1.0 + t)
    dg = 0.5 * (1.0 + t) + 0.5 * x * (1.0 - t * t) * k0 * (1.0 + 3.0 * k1 * x * x)
    return g, dg


def _neg_expm1(y):
    p = 1.0 + y * (1.0 / 7.0)
    p = 1.0 + y * (1.0 / 6.0) * p
    p = 1.0 + y * (1.0 / 5.0) * p
    p = 1.0 + y * (1.0 / 4.0) * p
    p = 1.0 + y * (1.0 / 3.0) * p
    p = 1.0 + y * 0.5 * p
    return jnp.where(y > -0.3, -y * p, 1.0 - jnp.exp(y))


def _colsum(v):
    return jnp.sum(v, axis=0, keepdims=True)


def _rowmean(v):
    return jnp.mean(v, axis=-1, keepdims=True)


def matmul(a, b, *, ta=False, tb=False, out_dtype=F32, tm, tn, tk, name, n=None, b_off=0):
    m = a.shape[1] if ta else a.shape[0]
    kdim = a.shape[0] if ta else a.shape[1]
    n = n or (b.shape[0] if tb else b.shape[1])
    tm, tn, tk = min(tm, m), min(tn, n), min(tk, kdim)
    nk = kdim // tk
    dn = (((0 if ta else 1,), (1 if tb else 0,)), ((), ()))

    def body(a_ref, b_ref, o_ref, acc_ref):
        k = pl.program_id(2)
        p = lax.dot_general(a_ref[...], b_ref[...], dn, preferred_element_type=F32)
        if nk == 1:
            o_ref[...] = p.astype(out_dtype)
        else:
            @pl.when(k == 0)
            def _():
                acc_ref[...] = p

            @pl.when(k > 0)
            def _():
                acc_ref[...] += p

            @pl.when(k == nk - 1)
            def _():
                o_ref[...] = acc_ref[...].astype(out_dtype)

    a_spec = pl.BlockSpec((tk, tm), lambda i, j, k: (k, i)) if ta else pl.BlockSpec((tm, tk), lambda i, j, k: (i, k))
    b_spec = (pl.BlockSpec((tn, tk), lambda i, j, k: (j, k)) if tb
              else pl.BlockSpec((tk, tn), lambda i, j, k: (k, j + b_off)))
    return pl.pallas_call(
        body, name=name, grid=(m // tm, n // tn, nk),
        in_specs=[a_spec, b_spec], out_specs=pl.BlockSpec((tm, tn), lambda i, j, k: (i, j)),
        out_shape=jax.ShapeDtypeStruct((m, n), out_dtype),
        scratch_shapes=[pltpu.VMEM((tm, tn), F32)],
        compiler_params=_cp("parallel", "parallel", "arbitrary"),
    )(a, b)


def in_proj_fwd(x2, mod8, pre1, w_main, w_dt, seq):
    t = x2.shape[0]
    tm = min(1024, seq)
    tn = 2048
    per_seq = seq // tm

    def body(x_ref, mod_ref, pre_ref, w_ref, wdt_ref, proj_ref, h_ref, dt_ref, h_scr):
        @pl.when(pl.program_id(1) == 0)
        def _():
            xv = x_ref[...]
            y = xv * lax.rsqrt(_rowmean(xv * xv) + EPS) * pre_ref[...]
            m = mod_ref[0]
            h = (y * (1.0 + m[1:2, :]) + m[0:1, :]).astype(BF16)
            h_scr[...] = h
            h_ref[...] = h
            dt_ref[...] = _dot(h, wdt_ref[...])

        proj_ref[...] = _dot(h_scr[...], w_ref[...]).astype(BF16)

    return pl.pallas_call(
        body, name="in_proj_fwd", grid=(t // tm, PROJ_MAIN // tn),
        in_specs=[pl.BlockSpec((tm, D_MODEL), lambda i, j: (i, 0)),
                  pl.BlockSpec((1, 8, D_MODEL), lambda i, j: (i // per_seq, 0, 0)),
                  pl.BlockSpec((1, D_MODEL), lambda i, j: (0, 0)),
                  pl.BlockSpec((D_MODEL, tn), lambda i, j: (0, j)),
                  pl.BlockSpec((D_MODEL, 128), lambda i, j: (0, 0))],
        out_specs=[pl.BlockSpec((tm, tn), lambda i, j: (i, j)),
                   pl.BlockSpec((tm, D_MODEL), lambda i, j: (i, 0)),
                   pl.BlockSpec((tm, 128), lambda i, j: (i, 0))],
        out_shape=[jax.ShapeDtypeStruct((t, PROJ_MAIN), BF16), jax.ShapeDtypeStruct((t, D_MODEL), BF16),
                   jax.ShapeDtypeStruct((t, 128), F32)],
        scratch_shapes=[pltpu.VMEM((tm, D_MODEL), BF16)],
        compiler_params=_cp("parallel", "arbitrary"),
    )(x2, mod8, pre1, w_main, w_dt)


def conv_fwd(src, col0, width, w4, bias, nb, seq, act, name):
    t = src.shape[0]
    tt = min(512, seq)
    tc = 512
    ns = seq // tt
    cb0 = col0 // tc

    def body(cur_ref, prev_ref, w_ref, b_ref, o_ref):
        s = pl.program_id(1)
        cur = cur_ref[...].astype(F32)
        prev = jnp.where(s == 0, 0.0, prev_ref[...].astype(F32))
        xx = jnp.concatenate([prev, cur], axis=0)
        w = w_ref[...]
        acc = cur * w[3:4, :] + b_ref[...]
        for d in (1, 2, 3):
            acc = acc + pltpu.roll(xx, d, axis=0)[HALO:, :] * w[3 - d:4 - d, :]
        if act:
            acc = acc * _sigmoid(acc)
        o_ref[...] = acc.astype(BF16)

    return pl.pallas_call(
        body, name=name, grid=(nb, ns, width // tc),
        in_specs=[pl.BlockSpec((tt, tc), lambda b, s, j: (b * ns + s, cb0 + j)),
                  pl.BlockSpec((HALO, tc), lambda b, s, j: (jnp.maximum((b * seq + s * tt) // HALO - 1, 0), cb0 + j)),
                  pl.BlockSpec((4, tc), lambda b, s, j: (0, j)),
                  pl.BlockSpec((1, tc), lambda b, s, j: (0, j))],
        out_specs=pl.BlockSpec((tt, tc), lambda b, s, j: (b * ns + s, j)),
        out_shape=jax.ShapeDtypeStruct((t, width), BF16),
        compiler_params=_cp("parallel", "parallel", "parallel"),
    )(src, src, w4, bias)


def conv_bwd(src, col0, width, w4, bias, dout, nb, seq, act, name):
    t = src.shape[0]
    tt = min(512, seq)
    tc = 512
    ns = seq // tt
    cb0 = col0 // tc
    nh = t // HALO

    def body(cur_ref, prev_ref, next_ref, w_ref, b_ref, do_ref, don_ref, dx_ref, acc_ref):
        b, s = pl.program_id(1), pl.program_id(2)

        @pl.when((b == 0) & (s == 0))
        def _():
            acc_ref[...] = jnp.zeros_like(acc_ref)

        cur = cur_ref[...].astype(F32)
        prev = jnp.where(s == 0, 0.0, prev_ref[...].astype(F32))
        nxt = next_ref[...].astype(F32)
        xx = jnp.concatenate([prev, cur, nxt], axis=0)
        w = w_ref[...]
        do_ext = jnp.concatenate([do_ref[...].astype(F32),
                                  jnp.where(s == ns - 1, 0.0, don_ref[...].astype(F32))], axis=0)
        ne = tt + HALO
        xs = [xx[HALO:HALO + ne, :]] + [pltpu.roll(xx, d, axis=0)[HALO:HALO + ne, :] for d in (1, 2, 3)]
        if act:
            c = b_ref[...] + xs[0] * w[3:4, :] + xs[1] * w[2:3, :] + xs[2] * w[1:2, :] + xs[3] * w[0:1, :]
            sg = _sigmoid(c)
            dc = do_ext * (sg * (1.0 + c * (1.0 - sg)))
        else:
            dc = do_ext
        dx = dc[:tt, :] * w[3:4, :]
        for d in (1, 2, 3):
            dx = dx + pltpu.roll(dc, ne - d, axis=0)[:tt, :] * w[3 - d:4 - d, :]
        dx_ref[...] = dx.astype(BF16)
        dcc = dc[:tt, :]
        rows = [_colsum(dcc * xs[3 - r][:tt, :]) for r in range(4)] + [_colsum(dcc)]
        acc_ref[...] += jnp.concatenate(rows + [jnp.zeros((3, tc), F32)], axis=0)

    return pl.pallas_call(
        body, name=name, grid=(width // tc, nb, ns),
        in_specs=[pl.BlockSpec((tt, tc), lambda j, b, s: (b * ns + s, cb0 + j)),
                  pl.BlockSpec((HALO, tc), lambda j, b, s: (jnp.maximum((b * seq + s * tt) // HALO - 1, 0), cb0 + j)),
                  pl.BlockSpec((HALO, tc), lambda j, b, s: (jnp.minimum((b * seq + (s + 1) * tt) // HALO, nh - 1), cb0 + j)),
                  pl.BlockSpec((4, tc), lambda j, b, s: (0, j)),
                  pl.BlockSpec((1, tc), lambda j, b, s: (0, j)),
                  pl.BlockSpec((tt, tc), lambda j, b, s: (b * ns + s, j)),
                  pl.BlockSpec((HALO, tc), lambda j, b, s: (jnp.minimum((b * seq + (s + 1) * tt) // HALO, nh - 1), j))],
        out_specs=[pl.BlockSpec((tt, tc), lambda j, b, s: (b * ns + s, j)),
                   pl.BlockSpec((8, tc), lambda j, b, s: (0, j))],
        out_shape=[jax.ShapeDtypeStruct((t, width), BF16), jax.ShapeDtypeStruct((8, width), F32)],
        compiler_params=_cp("parallel", "arbitrary", "arbitrary"),
    )(src, src, src, w4, bias, dout, dout)


def _lru_gates(xa, wa_ref, wx_ref, ba, bx, sp):
    nblk = D_MODEL // LRU_BLOCK
    pr = jnp.concatenate([_dot(xa[:, j * LRU_BLOCK:(j + 1) * LRU_BLOCK], wa_ref[j]) for j in range(nblk)], axis=1) + ba
    pi = jnp.concatenate([_dot(xa[:, j * LRU_BLOCK:(j + 1) * LRU_BLOCK], wx_ref[j]) for j in range(nblk)], axis=1) + bx
    r = _sigmoid(pr)
    i = _sigmoid(pi)
    log_a = (-LRU_C * r) * sp
    return r, i, jnp.exp(log_a), _neg_expm1(2.0 * log_a)


def lru_fwd(xa, proj, wa_bd, wx_bd, vec, nb, seq):
    t = xa.shape[0]
    tc = min(512, seq)
    nk = seq // tc
    gb = C_LRU_G // D_MODEL

    def body(xa_ref, g_ref, wa_ref, wx_ref, vec_ref, ya_ref, h_ref, a_scr, u_scr, hc_scr):
        @pl.when(pl.program_id(1) == 0)
        def _():
            hc_scr[...] = jnp.zeros_like(hc_scr)

        xa_v = xa_ref[...]
        v = vec_ref[...]
        r, i, a, e = _lru_gates(xa_v, wa_ref, wx_ref, v[0:1, :], v[1:2, :], v[2:3, :])
        a_scr[...] = a
        u_scr[...] = jnp.sqrt(e) * (i * xa_v.astype(F32))

        def step(tt, h):
            h = a_scr[pl.ds(tt, 1), :] * h + u_scr[pl.ds(tt, 1), :]
            h_ref[pl.ds(tt, 1), :] = h
            return h

        hc_scr[...] = lax.fori_loop(0, tc, step, hc_scr[...], unroll=8)
        gel, _ = _gelu_and_grad(g_ref[...].astype(F32))
        ya_ref[...] = (h_ref[...] * gel).astype(BF16)

    return pl.pallas_call(
        body, name="lru_fwd", grid=(nb, nk),
        in_specs=[pl.BlockSpec((tc, D_MODEL), lambda b, k: (b * nk + k, 0)),
                  pl.BlockSpec((tc, D_MODEL), lambda b, k: (b * nk + k, gb)),
                  VMEM_FULL, VMEM_FULL, VMEM_FULL],
        out_specs=[pl.BlockSpec((tc, D_MODEL), lambda b, k: (b * nk + k, 0)),
                   pl.BlockSpec((tc, D_MODEL), lambda b, k: (b * nk + k, 0))],
        out_shape=[jax.ShapeDtypeStruct((t, D_MODEL), BF16), jax.ShapeDtypeStruct((t, D_MODEL), F32)],
        scratch_shapes=[pltpu.VMEM((tc, D_MODEL), F32), pltpu.VMEM((tc, D_MODEL), F32), pltpu.VMEM((1, D_MODEL), F32)],
        compiler_params=_cp("arbitrary", "arbitrary"),
    )(xa, proj, wa_bd, wx_bd, vec)


def lru_bwd(dya, xa, proj, h, wa_bd, wx_bd, vec, nb, seq):
    t = xa.shape[0]
    tc = min(512, seq)
    nk = seq // tc
    gb = C_LRU_G // D_MODEL
    nblk = D_MODEL // LRU_BLOCK

    def chunk(b, k):
        return b * nk + (nk - 1 - k)

    def body(dya_ref, xa_ref, g_ref, h_ref, hp_ref, wa_ref, wx_ref, vec_ref,
             dxa_ref, dg_ref, dwa_ref, dwx_ref, dvec_ref, a_scr, dh_scr, c_scr):
        b, k = pl.program_id(0), pl.program_id(1)

        @pl.when((b == 0) & (k == 0))
        def _():
            dwa_ref[...] = jnp.zeros_like(dwa_ref)
            dwx_ref[...] = jnp.zeros_like(dwx_ref)
            dvec_ref[...] = jnp.zeros_like(dvec_ref)

        @pl.when(k == 0)
        def _():
            c_scr[...] = jnp.zeros_like(c_scr)

        xa_v = xa_ref[...]
        xaf = xa_v.astype(F32)
        v = vec_ref[...]
        sp = v[2:3, :]
        r, i, a, e = _lru_gates(xa_v, wa_ref, wx_ref, v[0:1, :], v[1:2, :], sp)
        gel, dgel = _gelu_and_grad(g_ref[...].astype(F32))
        hv = h_ref[...]
        dyv = dya_ref[...].astype(F32)
        dg_ref[...] = (dyv * hv * dgel).astype(BF16)
        a_scr[...] = a
        dh_scr[...] = dyv * gel

        def step(j, c):
            tt = tc - 1 - j
            dh = dh_scr[pl.ds(tt, 1), :] + c
            dh_scr[pl.ds(tt, 1), :] = dh
            return a_scr[pl.ds(tt, 1), :] * dh

        c_scr[...] = lax.fori_loop(0, tc, step, c_scr[...], unroll=8)
        dh = dh_scr[...]
        h_last = jnp.where(k == nk - 1, 0.0, hp_ref[HALO // 2 - 1:HALO // 2, :])
        row = lax.broadcasted_iota(jnp.int32, (tc, 1), 0)
        h_prev = jnp.where(row == 0, h_last, pltpu.roll(hv, 1, axis=0))
        s = jnp.sqrt(e)
        da = dh * h_prev
        ix = i * xaf
        dlog_a = da * a - (dh * ix) * (a * a) * lax.rsqrt(jnp.maximum(e, 1e-30))
        di = dh * s * xaf
        dpr = (dlog_a * (-LRU_C * sp)) * (r * (1.0 - r))
        dpi = di * (i * (1.0 - i))
        dprb, dpib = dpr.astype(BF16), dpi.astype(BF16)
        dxa = dh * s * i
        dxa = dxa + jnp.concatenate(
            [_dot_nt(dprb[:, j * LRU_BLOCK:(j + 1) * LRU_BLOCK], wa_ref[j])
             + _dot_nt(dpib[:, j * LRU_BLOCK:(j + 1) * LRU_BLOCK], wx_ref[j]) for j in range(nblk)], axis=1)
        dxa_ref[...] = dxa.astype(BF16)
        for j in range(nblk):
            sl = slice(j * LRU_BLOCK, (j + 1) * LRU_BLOCK)
            dwa_ref[j] += _dot_tn(xa_v[:, sl], dprb[:, sl])
            dwx_ref[j] += _dot_tn(xa_v[:, sl], dpib[:, sl])
        dvec_ref[...] += jnp.concatenate(
            [_colsum(dpr), _colsum(dpi), _colsum(dlog_a * (-LRU_C * r)), jnp.zeros((5, D_MODEL), F32)], axis=0)

    hh = HALO // 2
    return pl.pallas_call(
        body, name="lru_bwd", grid=(nb, nk),
        in_specs=[pl.BlockSpec((tc, D_MODEL), lambda b, k: (chunk(b, k), 0)),
                  pl.BlockSpec((tc, D_MODEL), lambda b, k: (chunk(b, k), 0)),
                  pl.BlockSpec((tc, D_MODEL), lambda b, k: (chunk(b, k), gb)),
                  pl.BlockSpec((tc, D_MODEL), lambda b, k: (chunk(b, k), 0)),
                  pl.BlockSpec((hh, D_MODEL), lambda b, k: (jnp.maximum(chunk(b, k) * (tc // hh) - 1, 0), 0)),
                  VMEM_FULL, VMEM_FULL, VMEM_FULL],
        out_specs=[pl.BlockSpec((tc, D_MODEL), lambda b, k: (chunk(b, k), 0)),
                   pl.BlockSpec((tc, D_MODEL), lambda b, k: (chunk(b, k), 0)),
                   pl.BlockSpec((nblk, LRU_BLOCK, LRU_BLOCK), lambda b, k: (0, 0, 0)),
                   pl.BlockSpec((nblk, LRU_BLOCK, LRU_BLOCK), lambda b, k: (0, 0, 0)),
                   pl.BlockSpec((8, D_MODEL), lambda b, k: (0, 0))],
        out_shape=[jax.ShapeDtypeStruct((t, D_MODEL), BF16), jax.ShapeDtypeStruct((t, D_MODEL), BF16),
                   jax.ShapeDtypeStruct((nblk, LRU_BLOCK, LRU_BLOCK), F32),
                   jax.ShapeDtypeStruct((nblk, LRU_BLOCK, LRU_BLOCK), F32),
                   jax.ShapeDtypeStruct((8, D_MODEL), F32)],
        scratch_shapes=[pltpu.VMEM((tc, D_MODEL), F32), pltpu.VMEM((tc, D_MODEL), F32), pltpu.VMEM((1, D_MODEL), F32)],
        compiler_params=_cp("arbitrary", "arbitrary"),
    )(dya, xa, proj, h, h, wa_bd, wx_bd, vec)


def merge_fwd(ya_in, yb_in, proj, x2, mod8, bgate, post1, w_pa, w_pb, w_out, seq):
    t = x2.shape[0]
    tm = min(512, seq)
    per_seq = seq // tm
    gcb = C_GATES // SSD_INNER

    def body(ya_ref, yb_ref, gt_ref, x_ref, mod_ref, bg_ref, post_ref, wpa_ref, wpb_ref, wo_ref,
             yab_ref, out1_ref, x1_ref):
        y_a = _dot(ya_ref[...], wpa_ref[...])
        y_b = _dot(yb_ref[...], wpb_ref[...])
        g = _sigmoid(gt_ref[...].astype(F32) + bg_ref[...])
        merged = g[:, :D_MODEL] * y_a + g[:, D_MODEL:] * y_b
        out1 = _dot(merged.astype(BF16), wo_ref[...])
        n = out1 * lax.rsqrt(_rowmean(out1 * out1) + EPS)
        yab_ref[...] = jnp.concatenate([y_a, y_b], axis=1).astype(BF16)
        out1_ref[...] = out1
        x1_ref[...] = x_ref[...] + mod_ref[0][2:3, :] * (n * post_ref[...])

    row = lambda w: pl.BlockSpec((tm, w), lambda i: (i, 0))
    return pl.pallas_call(
        body, name="merge_fwd", grid=(t // tm,),
        in_specs=[row(D_MODEL), row(SSD_INNER), pl.BlockSpec((tm, SSD_INNER), lambda i: (i, gcb)), row(D_MODEL),
                  pl.BlockSpec((1, 8, D_MODEL), lambda i: (i // per_seq, 0, 0)),
                  VMEM_FULL, VMEM_FULL, VMEM_FULL, VMEM_FULL, VMEM_FULL],
        out_specs=[row(SSD_INNER), row(D_MODEL), row(D_MODEL)],
        out_shape=[jax.ShapeDtypeStruct((t, SSD_INNER), BF16), jax.ShapeDtypeStruct((t, D_MODEL), F32),
                   jax.ShapeDtypeStruct((t, D_MODEL), F32)],
        compiler_params=_cp("parallel"),
    )(ya_in, yb_in, proj, x2, mod8, bgate, post1, w_pa, w_pb, w_out)


def merge_bwd(dx1, out1, yab, proj, mod8, bgate, post1, w_pa, w_pb, w_out, nb, seq):
    t = dx1.shape[0]
    tm = min(512, seq)
    per_seq = seq // tm
    gcb = C_GATES // SSD_INNER

    def body(dx1_ref, out1_ref, yab_ref, gt_ref, mod_ref, bg_ref, post_ref, wpa_ref, wpb_ref, wo_ref,
             dya_ref, dyb_ref, dgt_ref, dyab_ref, dout1_ref, mg_ref, vacc_ref, dmod_ref):
        b, s = pl.program_id(0), pl.program_id(1)

        @pl.when((b == 0) & (s == 0))
        def _():
            vacc_ref[...] = jnp.zeros_like(vacc_ref)

        @pl.when(s == 0)
        def _():
            dmod_ref[...] = jnp.zeros_like(dmod_ref)

        dx1v = dx1_ref[...]
        out1 = out1_ref[...]
        post = post_ref[...]
        rs = lax.rsqrt(_rowmean(out1 * out1) + EPS)
        n = out1 * rs
        do = dx1v * mod_ref[0][2:3, :]
        dn = do * post
        dout1 = rs * (dn - n * _rowmean(dn * n))
        dout1b = dout1.astype(BF16)
        dout1_ref[...] = dout1b
        dmerged = _dot_nt(dout1b, wo_ref[...])
        g = _sigmoid(gt_ref[...].astype(F32) + bg_ref[...])
        yab_v = yab_ref[...].astype(F32)
        gy = g * yab_v
        mg_ref[...] = (gy[:, :D_MODEL] + gy[:, D_MODEL:]).astype(BF16)
        dm2 = jnp.concatenate([dmerged, dmerged], axis=1)
        dyab = (dm2 * g).astype(BF16)
        dyab_ref[...] = dyab
        dgt = dm2 * gy * (1.0 - g)
        dgt_ref[...] = dgt.astype(BF16)
        dya_ref[...] = _dot_nt(dyab[:, :D_MODEL], wpa_ref[...]).astype(BF16)
        dyb_ref[...] = _dot_nt(dyab[:, D_MODEL:], wpb_ref[...]).astype(BF16)
        vacc_ref[...] += jnp.concatenate(
            [_colsum(dgt), jnp.concatenate([_colsum(do * n), jnp.zeros((1, D_MODEL), F32)], axis=1),
             jnp.zeros((6, SSD_INNER), F32)], axis=0)
        dmod_ref[0] += jnp.concatenate(
            [jnp.zeros((2, D_MODEL), F32), _colsum(dx1v * (n * post)), jnp.zeros((5, D_MODEL), F32)], axis=0)

    row = lambda w: pl.BlockSpec((tm, w), lambda b, s: (b * per_seq + s, 0))
    return pl.pallas_call(
        body, name="merge_bwd", grid=(nb, per_seq),
        in_specs=[row(D_MODEL), row(D_MODEL), row(SSD_INNER),
                  pl.BlockSpec((tm, SSD_INNER), lambda b, s: (b * per_seq + s, gcb)),
                  pl.BlockSpec((1, 8, D_MODEL), lambda b, s: (b, 0, 0)),
                  VMEM_FULL, VMEM_FULL, VMEM_FULL, VMEM_FULL, VMEM_FULL],
        out_specs=[row(D_MODEL), row(SSD_INNER), row(SSD_INNER), row(SSD_INNER), row(D_MODEL), row(D_MODEL),
                   pl.BlockSpec((8, SSD_INNER), lambda b, s: (0, 0)),
                   pl.BlockSpec((1, 8, D_MODEL), lambda b, s: (b, 0, 0))],
        out_shape=[jax.ShapeDtypeStruct((t, D_MODEL), BF16), jax.ShapeDtypeStruct((t, SSD_INNER), BF16),
                   jax.ShapeDtypeStruct((t, SSD_INNER), BF16), jax.ShapeDtypeStruct((t, SSD_INNER), BF16),
                   jax.ShapeDtypeStruct((t, D_MODEL), BF16), jax.ShapeDtypeStruct((t, D_MODEL), BF16),
                   jax.ShapeDtypeStruct((8, SSD_INNER), F32), jax.ShapeDtypeStruct((nb, 8, D_MODEL), F32)],
        compiler_params=_cp("arbitrary", "arbitrary"),
    )(dx1, out1, yab, proj, mod8, bgate, post1, w_pa, w_pb, w_out)


def mlp_fwd_bwd(x1, tgt, mod8, pre2, post2, w_ff1, w_ff2, nb, seq):
    t = x1.shape[0]
    tm = min(256, seq)
    per_seq = seq // tm
    fc = 1024
    nfc = D_FF // fc

    def body(x1_ref, tgt_ref, mod_ref, pre_ref, post_ref, w1_ref, w2_ref,
             dx1_ref, h2_ref, da1_ref, act_ref, dy2_ref, loss_ref, vacc_ref, dmod_ref, r_scr):
        b, s = pl.program_id(0), pl.program_id(1)

        @pl.when((b == 0) & (s == 0))
        def _():
            vacc_ref[...] = jnp.zeros_like(vacc_ref)
            loss_ref[...] = jnp.zeros_like(loss_ref)

        @pl.when(s == 0)
        def _():
            dmod_ref[...] = jnp.zeros_like(dmod_ref)

        m = mod_ref[0]
        sh2, sc2, g2 = m[3:4, :], m[4:5, :], m[5:6, :]
        pre, post = pre_ref[...], post_ref[...]
        x1v = x1_ref[...]
        rs1 = lax.rsqrt(_rowmean(x1v * x1v) + EPS)
        n1 = x1v * rs1
        y1 = n1 * pre
        h2b = (y1 * (1.0 + sc2) + sh2).astype(BF16)
        h2_ref[...] = h2b
        y2 = jnp.zeros((tm, D_MODEL), F32)
        for c in range(nfc):
            r = jnp.maximum(_dot(h2b, w1_ref[:, c * fc:(c + 1) * fc]), 0.0)
            r_scr[:, c * fc:(c + 1) * fc] = r
            a = (r * r).astype(BF16)
            act_ref[:, c * fc:(c + 1) * fc] = a
            y2 = y2 + _dot(a, w2_ref[c * fc:(c + 1) * fc, :])
        rs2 = lax.rsqrt(_rowmean(y2 * y2) + EPS)
        n2 = y2 * rs2
        o2 = n2 * post
        diff = x1v + g2 * o2 - tgt_ref[...]
        loss_ref[...] += 0.5 * jnp.sum(_rowmean(diff * diff))
        dx2 = diff * (1.0 / D_MODEL)
        do2 = dx2 * g2
        dn2 = do2 * post
        dy2b = (rs2 * (dn2 - n2 * _rowmean(dn2 * n2))).astype(BF16)
        dy2_ref[...] = dy2b
        dh2 = jnp.zeros((tm, D_MODEL), F32)
        for c in range(nfc):
            dact = _dot_nt(dy2b, w2_ref[c * fc:(c + 1) * fc, :])
            da = (dact * (2.0 * r_scr[:, c * fc:(c + 1) * fc])).astype(BF16)
            da1_ref[:, c * fc:(c + 1) * fc] = da
            dh2 = dh2 + _dot_nt(da, w1_ref[:, c * fc:(c + 1) * fc])
        dy1 = dh2 * (1.0 + sc2)
        dn1 = dy1 * pre
        dx1_ref[...] = dx2 + rs1 * (dn1 - n1 * _rowmean(dn1 * n1))
        vacc_ref[...] += jnp.concatenate([_colsum(dy1 * n1), _colsum(do2 * n2), jnp.zeros((6, D_MODEL), F32)], axis=0)
        dmod_ref[0] += jnp.concatenate(
            [jnp.zeros((3, D_MODEL), F32), _colsum(dh2), _colsum(dh2 * y1), _colsum(dx2 * o2),
             jnp.zeros((2, D_MODEL), F32)], axis=0)

    row = lambda w: pl.BlockSpec((tm, w), lambda b, s: (b * per_seq + s, 0))
    return pl.pallas_call(
        body, name="mlp_fwd_bwd", grid=(nb, per_seq),
        in_specs=[row(D_MODEL), row(D_MODEL), pl.BlockSpec((1, 8, D_MODEL), lambda b, s: (b, 0, 0)),
                  VMEM_FULL, VMEM_FULL, VMEM_FULL, VMEM_FULL],
        out_specs=[row(D_MODEL), row(D_MODEL), row(D_FF), row(D_FF), row(D_MODEL),
                   pl.BlockSpec((8, 128), lambda b, s: (0, 0)),
                   pl.BlockSpec((8, D_MODEL), lambda b, s: (0, 0)),
                   pl.BlockSpec((1, 8, D_MODEL), lambda b, s: (b, 0, 0))],
        out_shape=[jax.ShapeDtypeStruct((t, D_MODEL), F32), jax.ShapeDtypeStruct((t, D_MODEL), BF16),
                   jax.ShapeDtypeStruct((t, D_FF), BF16), jax.ShapeDtypeStruct((t, D_FF), BF16),
                   jax.ShapeDtypeStruct((t, D_MODEL), BF16), jax.ShapeDtypeStruct((8, 128), F32),
                   jax.ShapeDtypeStruct((8, D_MODEL), F32), jax.ShapeDtypeStruct((nb, 8, D_MODEL), F32)],
        scratch_shapes=[pltpu.VMEM((tm, D_FF), F32)],
        compiler_params=_cp("arbitrary", "arbitrary"),
    )(x1, tgt, mod8, pre2, post2, w_ff1, w_ff2)


_PIECES = ((C_LRU_X, 1024), (C_LRU_G, 1024), (C_Z, 2048), (C_XBC, 2048), (C_XBC + 2048, 1024), (C_XBC + 3072, 1024),
           (C_GATES, 2048))
_NP = len(_PIECES)


def _piece_of(k, tk):
    col = k * tk
    for p, (c0, w) in enumerate(_PIECES):
        if c0 <= col < c0 + w:
            return p, (col - c0) // tk
    raise ValueError(col)


def in_proj_bwd(pieces, ddt, dx1, x2, mod8, pre1, w_main, w_dt, nb, seq):
    t = x2.shape[0]
    tm = min(512, seq)
    per_seq = seq // tm
    tk = 1024
    nk = PROJ_MAIN // tk
    where = [_piece_of(k, tk) for k in range(nk)]

    def piece_spec(p):
        first = min(k for k in range(nk) if where[k][0] == p)
        nblk = _PIECES[p][1] // tk
        return pl.BlockSpec((tm, tk), lambda b, s, k: (b * per_seq + s, jnp.clip(k - first, 0, nblk - 1)))

    def body(*refs):
        prefs = refs[:_NP]
        ddt_ref, dx1_ref, x_ref, mod_ref, pre_ref, w_ref, wdt_ref, gx_ref, vacc_ref, dmod_ref, acc_ref = refs[_NP:]
        b, s, k = pl.program_id(0), pl.program_id(1), pl.program_id(2)

        @pl.when((b == 0) & (s == 0) & (k == 0))
        def _():
            vacc_ref[...] = jnp.zeros_like(vacc_ref)

        @pl.when((s == 0) & (k == 0))
        def _():
            dmod_ref[...] = jnp.zeros_like(dmod_ref)

        @pl.when(k == 0)
        def _():
            acc_ref[...] = _dot_nt(ddt_ref[...], wdt_ref[...])

        for kk in range(nk):
            @pl.when(k == kk)
            def _(kk=kk):
                acc_ref[...] += _dot_nt(prefs[where[kk][0]][...], w_ref[...])

        @pl.when(k == nk - 1)
        def _():
            dh = acc_ref[...]
            m = mod_ref[0]
            pre = pre_ref[...]
            xv = x_ref[...]
            rs = lax.rsqrt(_rowmean(xv * xv) + EPS)
            n = xv * rs
            dy = dh * (1.0 + m[1:2, :])
            dn = dy * pre
            gx_ref[...] = dx1_ref[...] + rs * (dn - n * _rowmean(dn * n))
            vacc_ref[...] += jnp.concatenate([_colsum(dy * n), jnp.zeros((7, D_MODEL), F32)], axis=0)
            dmod_ref[0] += jnp.concatenate([_colsum(dh), _colsum(dh * (n * pre)), jnp.zeros((6, D_MODEL), F32)], axis=0)

    row = lambda w: pl.BlockSpec((tm, w), lambda b, s, k: (b * per_seq + s, 0))
    return pl.pallas_call(
        body, name="in_proj_bwd", grid=(nb, per_seq, nk),
        in_specs=[piece_spec(p) for p in range(_NP)] + [
            row(128), row(D_MODEL), row(D_MODEL), pl.BlockSpec((1, 8, D_MODEL), lambda b, s, k: (b, 0, 0)),
            pl.BlockSpec((1, D_MODEL), lambda b, s, k: (0, 0)),
            pl.BlockSpec((D_MODEL, tk), lambda b, s, k: (0, k)),
            pl.BlockSpec((D_MODEL, 128), lambda b, s, k: (0, 0))],
        out_specs=[row(D_MODEL), pl.BlockSpec((8, D_MODEL), lambda b, s, k: (0, 0)),
                   pl.BlockSpec((1, 8, D_MODEL), lambda b, s, k: (b, 0, 0))],
        out_shape=[jax.ShapeDtypeStruct((t, D_MODEL), F32), jax.ShapeDtypeStruct((8, D_MODEL), F32),
                   jax.ShapeDtypeStruct((nb, 8, D_MODEL), F32)],
        scratch_shapes=[pltpu.VMEM((tm, D_MODEL), F32)],
        compiler_params=_cp("arbitrary", "arbitrary", "arbitrary"),
    )(*pieces, ddt, dx1, x2, mod8, pre1, w_main, w_dt)


def in_proj_wgrad(h1, pieces, ddt, name="in_proj_wgrad"):
    t = h1.shape[0]
    tt = min(1024, t)
    tn = 1024
    nn = PROJ_MAIN // tn
    nt = t // tt
    where = [_piece_of(n, tn) for n in range(nn)]

    def piece_spec(p):
        first = min(n for n in range(nn) if where[n][0] == p)
        nblk = _PIECES[p][1] // tn
        return pl.BlockSpec((tt, tn), lambda n, k: (k, jnp.clip(n - first, 0, nblk - 1)))

    def body(h_ref, *refs):
        prefs = refs[:_NP]
        ddt_ref, dw_ref, dwdt_ref, acc_ref, accdt_ref = refs[_NP:]
        n, k = pl.program_id(0), pl.program_id(1)
        hv = h_ref[...]
        for nn_ in range(nn):
            @pl.when(n == nn_)
            def _(nn_=nn_):
                p = _dot_tn(hv, prefs[where[nn_][0]][...])

                @pl.when(k == 0)
                def _():
                    acc_ref[...] = p

                @pl.when(k > 0)
                def _():
                    acc_ref[...] += p

        @pl.when(n == 0)
        def _():
            p = _dot_tn(hv, ddt_ref[...])

            @pl.when(k == 0)
            def _():
                accdt_ref[...] = p

            @pl.when(k > 0)
            def _():
                accdt_ref[...] += p

        @pl.when(k == nt - 1)
        def _():
            dw_ref[...] = acc_ref[...].astype(BF16)

        @pl.when((n == 0) & (k == nt - 1))
        def _():
            dwdt_ref[...] = accdt_ref[...].astype(BF16)

    return pl.pallas_call(
        body, name=name, grid=(nn, nt),
        in_specs=[pl.BlockSpec((tt, D_MODEL), lambda n, k: (k, 0))] + [piece_spec(p) for p in range(_NP)]
        + [pl.BlockSpec((tt, 128), lambda n, k: (k, 0))],
        out_specs=[pl.BlockSpec((D_MODEL, tn), lambda n, k: (0, n)), pl.BlockSpec((D_MODEL, 128), lambda n, k: (0, 0))],
        out_shape=[jax.ShapeDtypeStruct((D_MODEL, PROJ_MAIN), BF16), jax.ShapeDtypeStruct((D_MODEL, 128), BF16)],
        scratch_shapes=[pltpu.VMEM((D_MODEL, tn), F32), pltpu.VMEM((D_MODEL, 128), F32)],
        compiler_params=_cp("arbitrary", "arbitrary"),
    )(h1, *pieces, ddt)


def _log1p(u):
    w = 1.0 + u
    return jnp.log(w) - ((w - 1.0) - u) / w


def _softplus(x):
    return jnp.maximum(x, 0.0) + _log1p(jnp.exp(-jnp.abs(x)))


def _head_mask(h):
    lane = lax.broadcasted_iota(jnp.int32, (1, SSD_GW), 1)
    return (lane >= SSD_P * h) & (lane < SSD_P * (h + 1))


def _expand4(m):
    lane = lax.broadcasted_iota(jnp.int32, (1, SSD_GW), 1)
    return jnp.where(lane < SSD_P, m[:, 0:1],
                     jnp.where(lane < 2 * SSD_P, m[:, 1:2], jnp.where(lane < 3 * SSD_P, m[:, 2:3], m[:, 3:4])))


def _reduce4(v):
    lane = lax.broadcasted_iota(jnp.int32, (1, SSD_N), 1)
    out = jnp.zeros((v.shape[0], SSD_N), F32)
    for h in range(4):
        s = jnp.sum(jnp.where(_head_mask(h), v, 0.0), axis=1, keepdims=True)
        out = out + jnp.where(lane == h, s, 0.0)
    return out


def _ssd_chunk(xs_b, bm_b, cm_b, dtraw, sel, pv, tri, st):
    ll = SSD_L
    xs = xs_b.astype(F32)
    bias, a_row = pv[2:3, :SSD_N], pv[3:4, :SSD_N]
    xdt = _dot_hi(dtraw, sel) + bias
    dt = _softplus(xdt)
    cs = _dot_hi(tri, dt * a_row)
    cs_t = cs.T
    cs_last = cs[ll - 1:ll, :]
    e_x = _expand4(jnp.exp(cs))
    w_x = _expand4(jnp.exp(cs_last - cs))
    el_x = _expand4(jnp.exp(cs_last))
    dt_x = _expand4(dt)
    xd = xs * dt_x
    gcb = _dot_nt(cm_b, bm_b)
    ri = lax.broadcasted_iota(jnp.int32, (ll, ll), 0)
    ci = lax.broadcasted_iota(jnp.int32, (ll, ll), 1)
    dks = []
    ydiag = jnp.zeros((ll, SSD_GW), F32)
    for h in range(4):
        dk = jnp.exp(jnp.where(ri >= ci, cs[:, h:h + 1] - cs_t[h:h + 1, :], -1e30))
        dks.append(dk)
        ydiag = ydiag + _dot((gcb * dk).astype(BF16), jnp.where(_head_mask(h), xd, 0.0).astype(BF16))
    yoff = _dot(cm_b, st.astype(BF16)) * e_x
    y = ydiag + yoff + pv[1:2, :] * xs
    st_new = st * el_x + _dot(bm_b.astype(F32).T.astype(BF16), (xd * w_x).astype(BF16))
    return dict(xs=xs, xdt=xdt, dt=dt, a_row=a_row, cs=cs, cs_t=cs_t, e_x=e_x, w_x=w_x, el_x=el_x, dt_x=dt_x, xd=xd,
                gcb=gcb, dks=dks, yoff=yoff, y=y, st_new=st_new, ri=ri, ci=ci)


def ssd_consts():
    hh = np.arange(SSD_N)
    sel = np.stack([(hh[:, None] == 4 * g + hh[None, :]) & (hh[None, :] < 4) for g in range(SSD_G)]).astype(np.float32)
    tri = (hh[:, None] >= hh[None, :]).astype(np.float32)
    return jnp.asarray(sel), jnp.asarray(sel.transpose(0, 2, 1)), jnp.asarray(tri), jnp.asarray(tri.T)


def ssd_group_params(dt_bias, a_log, d_skip, norm_w):
    pad = lambda v: jnp.pad(v.reshape(SSD_G, 4), ((0, 0), (0, SSD_GW - 4)))
    rows = [norm_w.reshape(SSD_G, SSD_GW), jnp.repeat(d_skip, SSD_P).reshape(SSD_G, SSD_GW), pad(dt_bias),
            pad(-jnp.exp(a_log))]
    return jnp.concatenate([jnp.stack(rows, axis=1), jnp.zeros((SSD_G, 4, SSD_GW), F32)], axis=1)


def _ssd_specs(nc):
    row = lambda b, c: b * nc + c
    return lambda rc: [
        pl.BlockSpec((SSD_L, SSD_GW), lambda b, c, g: (row(b, rc(c)), g)),
        pl.BlockSpec((SSD_L, SSD_N), lambda b, c, g: (row(b, rc(c)), SSD_INNER // SSD_N + g)),
        pl.BlockSpec((SSD_L, SSD_N), lambda b, c, g: (row(b, rc(c)), SSD_INNER // SSD_N + SSD_G + g)),
        pl.BlockSpec((SSD_L, SSD_GW), lambda b, c, g: (row(b, rc(c)), C_Z // SSD_GW + g)),
        pl.BlockSpec((SSD_L, SSD_N), lambda b, c, g: (row(b, rc(c)), 0)),
        pl.BlockSpec((1, SSD_N, SSD_N), lambda b, c, g: (g, 0, 0)),
        pl.BlockSpec((1, 8, SSD_GW), lambda b, c, g: (g, 0, 0)),
    ]


def ssd_fwd(xbc, proj, dtraw, sel, pv, tri, nb, seq):
    t = xbc.shape[0]
    nc = seq // SSD_L

    def body(xs_ref, b_ref, c_ref, z_ref, dt_ref, sel_ref, pv_ref, tri_ref, y_ref, sts_ref, st_scr):
        g = pl.program_id(2)

        @pl.when(pl.program_id(1) == 0)
        def _():
            st_scr[g] = jnp.zeros((SSD_N, SSD_GW), F32)

        st = st_scr[g]
        sts_ref[0, 0] = st
        pv = pv_ref[0]
        f = _ssd_chunk(xs_ref[...], b_ref[...], c_ref[...], dt_ref[...], sel_ref[0], pv, tri_ref[...], st)
        st_scr[g] = f["st_new"]
        zf = z_ref[...].astype(F32)
        yg = f["y"] * (zf * _sigmoid(zf))
        y_ref[...] = (yg * lax.rsqrt(_rowmean(yg * yg) + EPS) * pv[0:1, :]).astype(BF16)

    return pl.pallas_call(
        body, name="ssd_fwd", grid=(nb, nc, SSD_G),
        in_specs=_ssd_specs(nc)(lambda c: c) + [VMEM_FULL],
        out_specs=[pl.BlockSpec((SSD_L, SSD_GW), lambda b, c, g: (b * nc + c, g)),
                   pl.BlockSpec((1, 1, SSD_N, SSD_GW), lambda b, c, g: (b * nc + c, g, 0, 0))],
        out_shape=[jax.ShapeDtypeStruct((t, SSD_INNER), BF16),
                   jax.ShapeDtypeStruct((nb * nc, SSD_G, SSD_N, SSD_GW), F32)],
        scratch_shapes=[pltpu.VMEM((SSD_G, SSD_N, SSD_GW), F32)],
        compiler_params=_cp("arbitrary", "arbitrary", "arbitrary"),
    )(xbc, xbc, xbc, proj, dtraw, sel, pv, tri)


def ssd_bwd(xbc, proj, dtraw, sel, sel_t, pv, tri, triu, states, dyn, nb, seq):
    t = xbc.shape[0]
    nc = seq // SSD_L
    ll = SSD_L

    def body(xs_ref, b_ref, c_ref, z_ref, dt_ref, sel_ref, pv_ref, selt_ref, sts_ref, dy_ref, tri_ref, triu_ref,
             dxs_ref, db_ref, dc_ref, dz_ref, ddt_ref, pg_ref, dst_scr, pg_scr):
        b, c_i, g = pl.program_id(0), pl.program_id(1), pl.program_id(2)

        @pl.when((b == 0) & (c_i == 0) & (g == 0))
        def _():
            pg_scr[...] = jnp.zeros_like(pg_scr)

        @pl.when(c_i == 0)
        def _():
            dst_scr[g] = jnp.zeros((SSD_N, SSD_GW), F32)

        st = sts_ref[0, 0]
        dst = dst_scr[g]
        pv = pv_ref[0]
        bm_b, cm_b = b_ref[...], c_ref[...]
        f = _ssd_chunk(xs_ref[...], bm_b, cm_b, dt_ref[...], sel_ref[0], pv, tri_ref[...], st)
        xs, xd, cs, cs_t, gcb = f["xs"], f["xd"], f["cs"], f["cs_t"], f["gcb"]
        e_x, w_x, el_x, dt_x, ri, ci = f["e_x"], f["w_x"], f["el_x"], f["dt_x"], f["ri"], f["ci"]
        stb, dstb = st.astype(BF16), dst.astype(BF16)
        zf = z_ref[...].astype(F32)
        sg = _sigmoid(zf)
        sz = zf * sg
        yg = f["y"] * sz
        rstd = lax.rsqrt(_rowmean(yg * yg) + EPS)
        n = yg * rstd
        dyn_v = dy_ref[...].astype(F32)
        dn = dyn_v * pv[0:1, :]
        dyg = rstd * (dn - n * _rowmean(dn * n))
        dy = dyg * sz
        dz_ref[...] = (dyg * f["y"] * (sg * (1.0 + zf * (1.0 - sg)))).astype(BF16)
        dyb = dy.astype(BF16)
        r_ = _dot(bm_b, dstb)
        dxd = w_x * r_
        dqb = (dy * e_x).astype(BF16)
        dcm = _dot_nt(dqb, stb)
        dst_scr[g] = dst * el_x + _dot(cm_b.astype(F32).T.astype(BF16), dqb)
        dbm = _dot_nt((xd * w_x).astype(BF16), dstb)
        gcb_t = _dot_nt(bm_b, cm_b)
        xdb = xd.astype(BF16)
        lane = lax.broadcasted_iota(jnp.int32, (1, SSD_N), 1)
        dgm = jnp.zeros((ll, ll), F32)
        dgm_t = jnp.zeros((ll, ll), F32)
        rm = jnp.zeros((ll, SSD_N), F32)
        cm_acc = jnp.zeros((ll, SSD_N), F32)
        for h in range(4):
            hm = _head_mask(h)
            dk = f["dks"][h]
            dk_t = jnp.exp(jnp.where(ci >= ri, cs_t[h:h + 1, :] - cs[:, h:h + 1], -1e30))
            dxd = dxd + jnp.where(hm, _dot((gcb_t * dk_t).astype(BF16), dyb), 0.0)
            dyh = jnp.where(hm, dy, 0.0).astype(BF16)
            dm = _dot_nt(dyh, xdb) * dk
            dm_t = _dot_nt(xdb, dyh) * dk_t
            dgm = dgm + dm
            dgm_t = dgm_t + dm_t
            rm = rm + jnp.where(lane == h, jnp.sum(dm * gcb, axis=1, keepdims=True), 0.0)
            cm_acc = cm_acc + jnp.where(lane == h, jnp.sum(dm_t * gcb_t, axis=1, keepdims=True), 0.0)
        dc_ref[...] = (dcm + _dot(dgm.astype(BF16), bm_b)).astype(BF16)
        db_ref[...] = (dbm + _dot(dgm_t.astype(BF16), cm_b)).astype(BF16)
        v = _reduce4(r_ * xd * w_x)
        last = _colsum(v) + _reduce4(_colsum(dst * st) * el_x)
        rowi = lax.broadcasted_iota(jnp.int32, (ll, 1), 0)
        dcs = rm - cm_acc + _reduce4(dy * f["yoff"]) - v + jnp.where(rowi == ll - 1, last, 0.0)
        da = _dot_hi(triu_ref[...], dcs)
        ddt_g = (_reduce4(dxd * xs) + da * f["a_row"]) * _sigmoid(f["xdt"])
        contrib = _dot_hi(ddt_g, selt_ref[0])

        @pl.when(g == 0)
        def _():
            ddt_ref[...] = contrib

        @pl.when(g > 0)
        def _():
            ddt_ref[...] += contrib

        dxs_ref[...] = (pv[1:2, :] * dy + dxd * dt_x).astype(BF16)
        pad = lambda r: jnp.concatenate([r, jnp.zeros((1, SSD_GW - SSD_N), F32)], axis=1)
        pg_scr[g] += jnp.concatenate(
            [_colsum(dyn_v * n), _colsum(dy * xs), pad(_colsum(ddt_g)), pad(_colsum(da * f["dt"])),
             jnp.zeros((4, SSD_GW), F32)], axis=0)

        @pl.when((b == nb - 1) & (c_i == nc - 1) & (g == SSD_G - 1))
        def _():
            pg_ref[...] = pg_scr[...]

    rc = lambda c: nc - 1 - c
    rowm = lambda b, c: b * nc + rc(c)
    return pl.pallas_call(
        body, name="ssd_bwd", grid=(nb, nc, SSD_G),
        in_specs=_ssd_specs(nc)(rc) + [
            pl.BlockSpec((1, SSD_N, SSD_N), lambda b, c, g: (g, 0, 0)),
            pl.BlockSpec((1, 1, SSD_N, SSD_GW), lambda b, c, g: (rowm(b, c), g, 0, 0)),
            pl.BlockSpec((SSD_L, SSD_GW), lambda b, c, g: (rowm(b, c), g)),
            VMEM_FULL, VMEM_FULL],
        out_specs=[pl.BlockSpec((SSD_L, SSD_GW), lambda b, c, g: (rowm(b, c), g)),
                   pl.BlockSpec((SSD_L, SSD_N), lambda b, c, g: (rowm(b, c), g)),
                   pl.BlockSpec((SSD_L, SSD_N), lambda b, c, g: (rowm(b, c), g)),
                   pl.BlockSpec((SSD_L, SSD_GW), lambda b, c, g: (rowm(b, c), g)),
                   pl.BlockSpec((SSD_L, SSD_N), lambda b, c, g: (rowm(b, c), 0)),
                   pl.BlockSpec((SSD_G, 8, SSD_GW), lambda b, c, g: (0, 0, 0))],
        out_shape=[jax.ShapeDtypeStruct((t, SSD_INNER), BF16), jax.ShapeDtypeStruct((t, SSD_G * SSD_N), BF16),
                   jax.ShapeDtypeStruct((t, SSD_G * SSD_N), BF16), jax.ShapeDtypeStruct((t, SSD_INNER), BF16),
                   jax.ShapeDtypeStruct((t, SSD_N), F32), jax.ShapeDtypeStruct((SSD_G, 8, SSD_GW), F32)],
        scratch_shapes=[pltpu.VMEM((SSD_G, SSD_N, SSD_GW), F32), pltpu.VMEM((SSD_G, 8, SSD_GW), F32)],
        compiler_params=_cp("arbitrary", "arbitrary", "arbitrary"),
    )(xbc, xbc, xbc, proj, dtraw, sel, pv, sel_t, states, dyn, tri, triu)


def ada_fwd(c_all, w_cols, b_cols):
    def body(c_ref, w_ref, b_ref, o_ref):
        cv = c_ref[...]
        o_ref[...] = _dot_hi(cv * _sigmoid(cv), w_ref[...]) + b_ref[...]

    return pl.pallas_call(body, name="ada_fwd", out_shape=jax.ShapeDtypeStruct((c_all.shape[0], w_cols.shape[1]), F32),
                          compiler_params=pltpu.CompilerParams(vmem_limit_bytes=VMEM_LIMIT))(c_all, w_cols, b_cols)


def ada_bwd(c_all, dmod_cols, dmod_all):
    def body(c_ref, dc_ref, da_ref, gw_ref, gb_ref):
        cv = c_ref[...]
        gw_ref[...] = lax.dot_general(cv * _sigmoid(cv), dc_ref[...], (((0,), (0,)), ((), ())),
                                      precision=lax.Precision.HIGHEST, preferred_element_type=F32)
        gb_ref[...] = _colsum(da_ref[...])

    return pl.pallas_call(
        body, name="ada_bwd",
        out_shape=[jax.ShapeDtypeStruct((c_all.shape[1], dmod_cols.shape[1]), F32),
                   jax.ShapeDtypeStruct((1, dmod_all.shape[1]), F32)],
        compiler_params=pltpu.CompilerParams(vmem_limit_bytes=VMEM_LIMIT))(c_all, dmod_cols, dmod_all)


def adamw(parts, w, m, v, name):
    n, r, c = parts.shape
    tr = r if r <= 256 else 128

    def body(p_ref, w_ref, m_ref, v_ref, g_ref, d_ref, nm_ref, nv_ref):
        g = p_ref[0].astype(F32)
        for s in range(1, n):
            g = g + p_ref[s].astype(F32)
        m2 = ADAM_B1 * m_ref[...] + (1.0 - ADAM_B1) * g
        v2 = ADAM_B2 * v_ref[...] + (1.0 - ADAM_B2) * (g * g)
        m_hat = m2 / (1.0 - ADAM_B1 ** ADAM_STEP)
        v_hat = v2 / (1.0 - ADAM_B2 ** ADAM_STEP)
        g_ref[...] = g
        d_ref[...] = -ADAM_LR * (m_hat / (jnp.sqrt(v_hat) + ADAM_EPS) + ADAM_WD * w_ref[...])
        nm_ref[...] = m2
        nv_ref[...] = v2

    blk = pl.BlockSpec((tr, c), lambda i: (i, 0))
    return pl.pallas_call(
        body, name=name, grid=(r // tr,),
        in_specs=[pl.BlockSpec((n, tr, c), lambda i: (0, i, 0)), blk, blk, blk], out_specs=[blk] * 4,
        out_shape=[jax.ShapeDtypeStruct((r, c), F32)] * 4,
        compiler_params=_cp("parallel"),
    )(parts, w, m, v)


def _dev_index(px, py, pc):
    return 4 * px + 2 * py + pc


def all_gather(arrs, name):
    na = len(arrs)

    def body(*refs):
        ins, outs = refs[:na], refs[na:2 * na]
        send_sems, recv_sems, local_sems = refs[2 * na:]
        x, y, c = lax.axis_index("x"), lax.axis_index("y"), lax.axis_index("c")
        me, sibling = (x, y, c), (x, y, 1 - c)
        chips = [(1 - x, y), (x, 1 - y), (1 - x, 1 - y)]

        def copy(a, k, block, to, src=None):
            dst = outs[a].at[_dev_index(*block)]
            return pltpu.make_async_remote_copy(
                src_ref=dst if src is None else src, dst_ref=dst, send_sem=send_sems.at[a * 7 + k],
                recv_sem=recv_sems.at[a * 7 + k], device_id=to, device_id_type=MESH)

        mine = [pltpu.make_async_copy(ins[a], outs[a].at[_dev_index(*me)], local_sems.at[a]) for a in range(na)]
        for cp in mine:
            cp.start()
        first = []
        for a in range(na):
            first.append(copy(a, 0, me, sibling, src=ins[a]))
            first += [copy(a, 1 + j, me, (*chip, c), src=ins[a]) for j, chip in enumerate(chips)]
        for cp in first:
            cp.start()
        passed = []
        for j, chip in enumerate(chips):
            for a in range(na):
                copy(a, 1 + j, (*chip, c), me).wait_recv()
                cp = copy(a, 4 + j, (*chip, c), sibling)
                cp.start()
                passed.append(cp)
        for a in range(na):
            copy(a, 0, sibling, me).wait_recv()
            for j, chip in enumerate(chips):
                copy(a, 4 + j, (*chip, 1 - c), me).wait_recv()
        for cp in first + passed:
            cp.wait_send()
        for cp in mine:
            cp.wait()

    return pl.pallas_call(
        body, name=name, in_specs=[ANY] * na, out_specs=[ANY] * na,
        out_shape=[jax.ShapeDtypeStruct((NDEV,) + a.shape, a.dtype) for a in arrs],
        scratch_shapes=[pltpu.SemaphoreType.DMA((7 * na,)), pltpu.SemaphoreType.DMA((7 * na,)),
                        pltpu.SemaphoreType.DMA((na,))],
    )(*arrs)


def exchange_blocks(arrs, name):
    na = len(arrs)

    def body(*refs):
        ins, outs = refs[:na], refs[na:2 * na]
        send_sems, recv_sems, local_sems = refs[2 * na:]
        x, y, c = lax.axis_index("x"), lax.axis_index("y"), lax.axis_index("c")
        me = _dev_index(x, y, c)
        masks = [(mx, my, mc) for mx in (0, 1) for my in (0, 1) for mc in (0, 1)][1:]
        flip = lambda v, bit: 1 - v if bit else v
        mine = [pltpu.make_async_copy(ins[a].at[me], outs[a].at[me], local_sems.at[a]) for a in range(na)]
        for cp in mine:
            cp.start()
        sends, recvs = [], []
        for k, (mx, my, mc) in enumerate(masks):
            peer = (flip(x, mx), flip(y, my), flip(c, mc))
            pidx = _dev_index(*peer)
            for a in range(na):
                sems = dict(send_sem=send_sems.at[a * 7 + k], recv_sem=recv_sems.at[a * 7 + k], device_id=peer,
                            device_id_type=MESH)
                sends.append(pltpu.make_async_remote_copy(src_ref=ins[a].at[pidx], dst_ref=outs[a].at[me], **sems))
                recvs.append(pltpu.make_async_remote_copy(src_ref=ins[a].at[pidx], dst_ref=outs[a].at[pidx], **sems))
        for cp in sends:
            cp.start()
        for cp in recvs:
            cp.wait_recv()
        for cp in sends:
            cp.wait_send()
        for cp in mine:
            cp.wait()

    return pl.pallas_call(
        body, name=name, in_specs=[ANY] * na, out_specs=[ANY] * na,
        out_shape=[jax.ShapeDtypeStruct(a.shape, a.dtype) for a in arrs],
        scratch_shapes=[pltpu.SemaphoreType.DMA((7 * na,)), pltpu.SemaphoreType.DMA((7 * na,)),
                        pltpu.SemaphoreType.DMA((na,))],
    )(*arrs)


WEIGHTS = ('w_ada', 'b_ada', 'pre_norm1', 'post_norm1', 'w_in', 'b_gate', 'lru_conv_w', 'lru_conv_b', 'lru_wa',
           'lru_ba', 'lru_wx', 'lru_bx', 'lru_lambda', 'w_pa', 'ssd_conv_w', 'ssd_conv_b', 'ssd_dt_bias', 'ssd_a_log',
           'ssd_d', 'ssd_norm_w', 'w_pb', 'w_out', 'pre_norm2', 'post_norm2', 'w_ff1', 'w_ff2')
BIG = ('w_in', 'w_pa', 'w_pb', 'w_out', 'w_ff1', 'w_ff2')
REPL = ('pre_norm1', 'post_norm1', 'b_gate', 'lru_conv_b', 'lru_wa', 'lru_ba', 'lru_wx', 'lru_bx', 'lru_lambda',
        'ssd_conv_b', 'ssd_dt_bias', 'ssd_a_log', 'ssd_d', 'ssd_norm_w', 'pre_norm2', 'post_norm2')
LANES = 1024


def _rows(n):
    return -(-n // LANES)


def _pack(vals, total_rows):
    parts = []
    for v in vals:
        f = v.reshape(-1).astype(F32)
        parts.append(jnp.pad(f, (0, _rows(f.shape[0]) * LANES - f.shape[0])))
    flat = jnp.concatenate(parts)
    return jnp.pad(flat.reshape(-1, LANES), ((0, total_rows - flat.shape[0] // LANES), (0, 0)))


def _unpack(slab, shapes):
    out, r = [], 0
    for s in shapes:
        n = int(np.prod(s))
        out.append(slab[r:r + _rows(n)].reshape(-1)[:n].reshape(s))
        r += _rows(n)
    return out


def _block_diag4(w):
    w4 = w.reshape(4, 4, 64, 64)
    eye = jnp.eye(4, dtype=w.dtype)
    return (w4[:, :, :, None, :] * eye[None, :, None, :, None]).reshape(4, LRU_BLOCK, LRU_BLOCK)


def _diag_blocks4(m):
    m5 = m.reshape(4, 4, 64, 4, 64)
    return jnp.stack([m5[:, a, :, a, :] for a in range(4)], axis=1).reshape(LRU_HEADS, 64, 64)


def kernel(x, c, w_ada, b_ada, pre_norm1, post_norm1, w_in, b_gate, lru_conv_w, lru_conv_b, lru_wa, lru_ba, lru_wx, lru_bx, lru_lambda, w_pa, ssd_conv_w, ssd_conv_b, ssd_dt_bias, ssd_a_log, ssd_d, ssd_norm_w, w_pb, w_out, pre_norm2, post_norm2, w_ff1, w_ff2, loss_target, m_w_ada, m_b_ada, m_pre_norm1, m_post_norm1, m_w_in, m_b_gate, m_lru_conv_w, m_lru_conv_b, m_lru_wa, m_lru_ba, m_lru_wx, m_lru_bx, m_lru_lambda, m_w_pa, m_ssd_conv_w, m_ssd_conv_b, m_ssd_dt_bias, m_ssd_a_log, m_ssd_d, m_ssd_norm_w, m_w_pb, m_w_out, m_pre_norm2, m_post_norm2, m_w_ff1, m_w_ff2, v_w_ada, v_b_ada, v_pre_norm1, v_post_norm1, v_w_in, v_b_gate, v_lru_conv_w, v_lru_conv_b, v_lru_wa, v_lru_ba, v_lru_wx, v_lru_bx, v_lru_lambda, v_w_pa, v_ssd_conv_w, v_ssd_conv_b, v_ssd_dt_bias, v_ssd_a_log, v_ssd_d, v_ssd_norm_w, v_w_pb, v_w_out, v_pre_norm2, v_post_norm2, v_w_ff1, v_w_ff2):
    given = dict(locals())
    w = {k: given[k] for k in WEIGHTS}
    mom = {k: given["m_" + k] for k in WEIGHTS}
    var = {k: given["v_" + k] for k in WEIGHTS}
    nb, seq, _ = x.shape
    assert nb == 2 and seq % 512 == 0, (nb, seq)
    t = nb * seq
    me = _dev_index(lax.axis_index("x"), lax.axis_index("y"), lax.axis_index("c"))
    x2 = x.reshape(t, D_MODEL)
    tgt2 = loss_target.reshape(t, D_MODEL)
    ada_cols = w_ada.shape[2]

    slab = jnp.zeros((16, LANES), F32)
    slab = slab.at[0:nb].set(c)
    slab = slab.at[2:6, 0:lru_conv_w.shape[2]].set(lru_conv_w[0])
    slab = slab.at[6:10, 0:ssd_conv_w.shape[2]].set(ssd_conv_w[0])
    (g1,) = all_gather([slab], "gather_cond")
    c_all = g1[:, 0:nb].reshape(NDEV * nb, D_MODEL)
    lru_cw = g1[:, 2:6, 0:lru_conv_w.shape[2]].transpose(1, 0, 2).reshape(4, D_MODEL)
    ssd_cw = g1[:, 6:10, 0:ssd_conv_w.shape[2]].transpose(1, 0, 2).reshape(4, 2 * SSD_INNER)
    b_cols = lax.dynamic_slice(b_ada, (0, me * ada_cols), (1, ada_cols))
    mod_cols = ada_fwd(c_all, w_ada[0], b_cols)
    (g2,) = all_gather([mod_cols], "gather_mod")
    mod_all = g2.transpose(1, 0, 2).reshape(NDEV * nb, N_MOD * D_MODEL)
    mod_mine = lax.dynamic_slice(mod_all, (me * nb, 0), (nb, N_MOD * D_MODEL)).reshape(nb, N_MOD, D_MODEL)
    mod8 = jnp.pad(mod_mine, ((0, 0), (0, 8 - N_MOD), (0, 0)))

    gw = all_gather([w[k][0].astype(BF16) for k in BIG], "gather_weights")
    w_nat = gw[0].transpose(1, 0, 2).reshape(D_MODEL, IN_DIM)
    w_main = jnp.concatenate([w_nat[:, :DT_COL0], w_nat[:, DT_COL0 + SSD_HEADS:]], axis=1)
    w_dt = jnp.pad(w_nat[:, DT_COL0:DT_COL0 + SSD_HEADS], ((0, 0), (0, 128 - SSD_HEADS)))
    w_pa_f = gw[1].reshape(D_MODEL, D_MODEL)
    w_pb_f = gw[2].reshape(SSD_INNER, D_MODEL)
    w_out_f = gw[3].reshape(D_MODEL, D_MODEL)
    w_ff1_f = gw[4].transpose(1, 0, 2).reshape(D_MODEL, D_FF)
    w_ff2_f = gw[5].reshape(D_FF, D_MODEL)

    wa_bd = _block_diag4(lru_wa[0]).astype(BF16)
    wx_bd = _block_diag4(lru_wx[0]).astype(BF16)
    lam = lru_lambda[0]
    vec = _pack([lru_ba, lru_bx, jax.nn.softplus(-lam)], 8)
    sel, sel_t, tri, triu = ssd_consts()
    pv = ssd_group_params(ssd_dt_bias[0], ssd_a_log[0], ssd_d[0], ssd_norm_w[0])

    proj, h1, dtraw = in_proj_fwd(x2, mod8, pre_norm1, w_main, w_dt, seq)
    xa = conv_fwd(proj, C_LRU_X, D_MODEL, lru_cw, lru_conv_b, nb, seq, False, "conv_lru_fwd")
    xbc = conv_fwd(proj, C_XBC, 2 * SSD_INNER, ssd_cw, ssd_conv_b, nb, seq, True, "conv_ssd_fwd")
    ya_in, hst = lru_fwd(xa, proj, wa_bd, wx_bd, vec, nb, seq)
    yb_in, states = ssd_fwd(xbc, proj, dtraw, sel, pv, tri, nb, seq)
    yab, out1, x1 = merge_fwd(ya_in, yb_in, proj, x2, mod8, b_gate, post_norm1, w_pa_f, w_pb_f, w_out_f, seq)

    dx1, h2, da1, act, dy2, loss8, vacc_mlp, dmod_mlp = mlp_fwd_bwd(
        x1, tgt2, mod8, pre_norm2, post_norm2, w_ff1_f, w_ff2_f, nb, seq)
    wg = dict(out_dtype=BF16, ta=True, tm=1024, tn=1024, tk=1024)
    dw_ff1 = matmul(h2, da1, name="wgrad_ff1", **wg)
    dw_ff2 = matmul(act, dy2, name="wgrad_ff2", **wg)
    dya_in, dyb_in, dgates, dyab, dout1, merged, vacc_mg, dmod_mg = merge_bwd(
        dx1, out1, yab, proj, mod8, b_gate, post_norm1, w_pa_f, w_pb_f, w_out_f, nb, seq)
    dw_out = matmul(merged, dout1, name="wgrad_out", **wg)
    dw_pa = matmul(ya_in, dyab, name="wgrad_pa", n=D_MODEL, b_off=0, **wg)
    dw_pb = matmul(yb_in, dyab, name="wgrad_pb", n=D_MODEL, b_off=1, **wg)
    dxa, dlg, dwa_bd, dwx_bd, dvec = lru_bwd(dya_in, xa, proj, hst, wa_bd, wx_bd, vec, nb, seq)
    dxs, dbm, dcm, dz, ddt, pg = ssd_bwd(xbc, proj, dtraw, sel, sel_t, pv, tri, triu, states, dyb_in, nb, seq)
    dlx, acc_l = conv_bwd(proj, C_LRU_X, D_MODEL, lru_cw, lru_conv_b, dxa, nb, seq, False, "conv_lru_bwd")
    s0, s1, s2 = SSD_INNER, SSD_INNER + SSD_G * SSD_N, 2 * SSD_INNER
    dxr_s, acc_s = conv_bwd(proj, C_XBC, s0, ssd_cw[:, :s0], ssd_conv_b[:, :s0], dxs, nb, seq, True, "conv_xs_bwd")
    dxr_b, acc_b = conv_bwd(proj, C_XBC + s0, s1 - s0, ssd_cw[:, s0:s1], ssd_conv_b[:, s0:s1], dbm, nb, seq, True,
                            "conv_b_bwd")
    dxr_c, acc_c = conv_bwd(proj, C_XBC + s1, s2 - s1, ssd_cw[:, s1:], ssd_conv_b[:, s1:], dcm, nb, seq, True,
                            "conv_c_bwd")
    pieces = (dlx, dlg, dz, dxr_s, dxr_b, dxr_c, dgates)
    ddt_b = ddt.astype(BF16)
    grad_x, vacc_in, dmod_in = in_proj_bwd(pieces, ddt_b, dx1, x2, mod8, pre_norm1, w_main, w_dt, nb, seq)
    dw_main, dw_dt = in_proj_wgrad(h1, pieces, ddt_b)

    dmod = (dmod_in + dmod_mg + dmod_mlp)[:, :N_MOD].reshape(nb, N_MOD * D_MODEL)
    (g3,) = all_gather([jnp.pad(dmod, ((0, 8 - nb), (0, 0)))], "gather_dmod")
    dmod_all = g3[:, :nb].reshape(NDEV * nb, N_MOD * D_MODEL)
    dmod_cols = lax.dynamic_slice(dmod_all, (0, me * ada_cols), (NDEV * nb, ada_cols))
    g_w_ada, g_b_ada = ada_bwd(c_all, dmod_cols, dmod_all)

    dw_nat = jnp.concatenate([dw_main[:, :DT_COL0], dw_dt[:, :SSD_HEADS], dw_main[:, DT_COL0:]], axis=1)
    blocks = [dw_nat.reshape(D_MODEL, NDEV, IN_DIM // NDEV).transpose(1, 0, 2),
              dw_pa.reshape(NDEV, D_MODEL // NDEV, D_MODEL), dw_pb.reshape(NDEV, SSD_INNER // NDEV, D_MODEL),
              dw_out.reshape(NDEV, D_MODEL // NDEV, D_MODEL),
              dw_ff1.reshape(D_MODEL, NDEV, D_FF // NDEV).transpose(1, 0, 2),
              dw_ff2.reshape(NDEV, D_FF // NDEV, D_MODEL)]
    parts = exchange_blocks(blocks, "scatter_wgrads")
    res = {}
    for k, p in zip(BIG, parts):
        res[k] = adamw(p, w[k][0], mom[k][0], var[k][0], "adamw_" + k)
    res['w_ada'] = adamw(g_w_ada[None], w_ada[0], m_w_ada[0], v_w_ada[0], "adamw_w_ada")

    a_neg = -jnp.exp(ssd_a_log[0])
    small = {
        'pre_norm1': vacc_in[0], 'post_norm1': vacc_mg[1, :D_MODEL], 'b_gate': vacc_mg[0],
        'lru_conv_b': acc_l[4], 'lru_wa': _diag_blocks4(dwa_bd), 'lru_ba': dvec[0], 'lru_wx': _diag_blocks4(dwx_bd),
        'lru_bx': dvec[1], 'lru_lambda': dvec[2] * (-jax.nn.sigmoid(-lam)),
        'ssd_conv_b': jnp.concatenate([acc_s[4], acc_b[4], acc_c[4]]),
        'ssd_dt_bias': pg[:, 2, :4].reshape(SSD_HEADS), 'ssd_a_log': pg[:, 3, :4].reshape(SSD_HEADS) * a_neg,
        'ssd_d': pg[:, 1].reshape(SSD_HEADS, SSD_P).sum(axis=-1), 'ssd_norm_w': pg[:, 0].reshape(SSD_INNER),
        'pre_norm2': vacc_mlp[0], 'post_norm2': vacc_mlp[1],
    }
    conv_full = [acc_l[:4], jnp.concatenate([acc_s[:4], acc_b[:4], acc_c[:4]], axis=1)]
    nra = sum(_rows(int(np.prod(w[k].shape))) for k in REPL)
    nrc = sum(_rows(int(np.prod(v.shape))) for v in conv_full)
    rows_a = -(-(nra + nrc) // 8) * 8
    gslab = _pack([small[k] for k in REPL] + conv_full, rows_a)
    (g4,) = all_gather([gslab], "gather_small_grads")
    res_a = adamw(g4, _pack([w[k] for k in REPL], rows_a), _pack([mom[k] for k in REPL], rows_a),
                  _pack([var[k] for k in REPL], rows_a), "adamw_small")
    for j, slab_j in enumerate(res_a):
        for k, val in zip(REPL, _unpack(slab_j, [w[k].shape for k in REPL])):
            res.setdefault(k, [None] * 4)[j] = val
    g_lru_cw, g_ssd_cw = _unpack(res_a[0][nra:], [v.shape for v in conv_full])
    lcw, scw = lru_conv_w.shape[2], ssd_conv_w.shape[2]
    sharded = {'b_ada': g_b_ada, 'lru_conv_w': lax.dynamic_slice(g_lru_cw, (0, me * lcw), (4, lcw)),
               'ssd_conv_w': lax.dynamic_slice(g_ssd_cw, (0, me * scw), (4, scw))}
    names_b = tuple(sharded)
    res_b = adamw(_pack([sharded[k] for k in names_b], 16)[None], _pack([w[k] for k in names_b], 16),
                  _pack([mom[k] for k in names_b], 16), _pack([var[k] for k in names_b], 16), "adamw_small_sharded")
    for j, slab_j in enumerate(res_b):
        for k, val in zip(names_b, _unpack(slab_j, [w[k].shape for k in names_b])):
            res.setdefault(k, [None] * 4)[j] = val

    loss = lax.psum(loss8[0, 0], ("x", "y", "c"))
    outs = [[res[k][j].reshape(w[k].shape) for k in WEIGHTS] for j in range(4)]
    return (loss, grad_x.reshape(x.shape), *outs[0], *outs[1], *outs[2], *outs[3])
```

```python
import functools

import numpy as np
import jax
import jax.numpy as jnp
from jax import lax
from jax.experimental import pallas as pl
from jax.experimental.pallas import tpu as pltpu

F32 = jnp.float32
BF16 = jnp.bfloat16

D_MODEL = 1024
LRU_HEADS = 16
LRU_BLOCK = 256
LRU_C = 8.0
SSD_INNER = 2048
SSD_HEADS = 32
SSD_P = 64
SSD_G = 8
SSD_N = 128
SSD_L = 128
SSD_GW = SSD_INNER // SSD_G
D_FF = 4096
N_MOD = 6
EPS = 1e-6
NDEV = 8

C_LRU_X, C_LRU_G, C_Z, C_XBC, C_GATES, PROJ_MAIN = 0, 1024, 2048, 4096, 8192, 10240
IN_DIM = 10272
DT_COL0 = 8192
HALO = 16

ADAM_LR, ADAM_B1, ADAM_B2, ADAM_EPS, ADAM_WD, ADAM_STEP = 0.001, 0.9, 0.999, 1e-08, 0.01, 10

VMEM_LIMIT = 60 * 1024 * 1024
MESH = pl.DeviceIdType.MESH
ANY = pl.BlockSpec(memory_space=pl.ANY)
VMEM_FULL = pl.BlockSpec(memory_space=pltpu.VMEM)


def _cp(*sem):
    return pltpu.CompilerParams(dimension_semantics=sem, vmem_limit_bytes=VMEM_LIMIT)


def _dot(a, b):
    return jnp.dot(a, b, preferred_element_type=F32)


def _dot_nt(a, b):
    return lax.dot_general(a, b, (((1,), (1,)), ((), ())), preferred_element_type=F32)


def _dot_tn(a, b):
    return lax.dot_general(a, b, (((0,), (0,)), ((), ())), preferred_element_type=F32)


def _dot_hi(a, b):
    return jnp.dot(a, b, precision=lax.Precision.HIGHEST, preferred_element_type=F32)


def _sigmoid(x):
    return 1.0 / (1.0 + jnp.exp(-x))


def _gelu_and_grad(x):
    k0, k1 = 0.7978845608028654, 0.044715
    t = jnp.tanh(k0 * (x + k1 * x * x * x))
    g = 0.5 * x * (1.0 + t)
    dg = 0.5 * (1.0 + t) + 0.5 * x * (1.0 - t * t) * k0 * (1.0 + 3.0 * k1 * x * x)
    return g, dg


def _neg_expm1(y):
    p = 1.0 + y * (1.0 / 7.0)
    p = 1.0 + y * (1.0 / 6.0) * p
    p = 1.0 + y * (1.0 / 5.0) * p
    p = 1.0 + y * (1.0 / 4.0) * p
    p = 1.0 + y * (1.0 / 3.0) * p
    p = 1.0 + y * 0.5 * p
    return jnp.where(y > -0.3, -y * p, 1.0 - jnp.exp(y))


def _colsum(v):
    return jnp.sum(v, axis=0, keepdims=True)


def _rowmean(v):
    return jnp.mean(v, axis=-1, keepdims=True)


def matmul(a, b, *, ta=False, tb=False, out_dtype=F32, tm, tn, tk, name, n=None, b_off=0):
    m = a.shape[1] if ta else a.shape[0]
    kdim = a.shape[0] if ta else a.shape[1]
    n = n or (b.shape[0] if tb else b.shape[1])
    tm, tn, tk = min(tm, m), min(tn, n), min(tk, kdim)
    nk = kdim // tk
    dn = (((0 if ta else 1,), (1 if tb else 0,)), ((), ()))

    def body(a_ref, b_ref, o_ref, acc_ref):
        k = pl.program_id(2)
        p = lax.dot_general(a_ref[...], b_ref[...], dn, preferred_element_type=F32)
        if nk == 1:
            o_ref[...] = p.astype(out_dtype)
        else:
            @pl.when(k == 0)
            def _():
                acc_ref[...] = p

            @pl.when(k > 0)
            def _():
                acc_ref[...] += p

            @pl.when(k == nk - 1)
            def _():
                o_ref[...] = acc_ref[...].astype(out_dtype)

    a_spec = pl.BlockSpec((tk, tm), lambda i, j, k: (k, i)) if ta else pl.BlockSpec((tm, tk), lambda i, j, k: (i, k))
    b_spec = (pl.BlockSpec((tn, tk), lambda i, j, k: (j, k)) if tb
              else pl.BlockSpec((tk, tn), lambda i, j, k: (k, j + b_off)))
    return pl.pallas_call(
        body, name=name, grid=(m // tm, n // tn, nk),
        in_specs=[a_spec, b_spec], out_specs=pl.BlockSpec((tm, tn), lambda i, j, k: (i, j)),
        out_shape=jax.ShapeDtypeStruct((m, n), out_dtype),
        scratch_shapes=[pltpu.VMEM((tm, tn), F32)],
        compiler_params=_cp("parallel", "parallel", "arbitrary"),
    )(a, b)


def in_proj_fwd(x2, mod8, pre1, w_main, w_dt, seq, side=None):
    t = x2.shape[0]
    tm = min(1024, seq)
    tn = 2048
    per_seq = seq // tm

    def body(x_ref, mod_ref, pre_ref, w_ref, wdt_ref, proj_ref, h_ref, dt_ref, h_scr):
        @pl.when(pl.program_id(1) == 0)
        def _():
            xv = x_ref[...]
            y = xv * lax.rsqrt(_rowmean(xv * xv) + EPS) * pre_ref[...]
            m = mod_ref[0]
            h = (y * (1.0 + m[1:2, :]) + m[0:1, :]).astype(BF16)
            h_scr[...] = h
            h_ref[...] = h
            dt_ref[...] = _dot(h, wdt_ref[...])

        proj_ref[...] = _dot(h_scr[...], w_ref[...]).astype(BF16)

    return _call(
        body, name="in_proj_fwd", grid=(t // tm, PROJ_MAIN // tn), side=side, sem=("parallel", "arbitrary"),
        args=(x2, mod8, pre1, w_main, w_dt),
        in_specs=[pl.BlockSpec((tm, D_MODEL), lambda i, j: (i, 0)),
                  pl.BlockSpec((1, 8, D_MODEL), lambda i, j: (i // per_seq, 0, 0)),
                  pl.BlockSpec((1, D_MODEL), lambda i, j: (0, 0)),
                  pl.BlockSpec((D_MODEL, tn), lambda i, j: (0, j)),
                  pl.BlockSpec((D_MODEL, 128), lambda i, j: (0, 0))],
        out_specs=[pl.BlockSpec((tm, tn), lambda i, j: (i, j)),
                   pl.BlockSpec((tm, D_MODEL), lambda i, j: (i, 0)),
                   pl.BlockSpec((tm, 128), lambda i, j: (i, 0))],
        out_shape=[jax.ShapeDtypeStruct((t, PROJ_MAIN), BF16), jax.ShapeDtypeStruct((t, D_MODEL), BF16),
                   jax.ShapeDtypeStruct((t, 128), F32)],
        scratch_shapes=[pltpu.VMEM((tm, D_MODEL), BF16)])


def conv_fwd(src, col0, width, w4, bias, nb, seq, act, name):
    t = src.shape[0]
    tt = min(512, seq)
    tc = 512
    ns = seq // tt
    cb0 = col0 // tc

    def body(cur_ref, prev_ref, w_ref, b_ref, o_ref):
        s = pl.program_id(1)
        cur = cur_ref[...].astype(F32)
        prev = jnp.where(s == 0, 0.0, prev_ref[...].astype(F32))
        xx = jnp.concatenate([prev, cur], axis=0)
        w = w_ref[...]
        acc = cur * w[3:4, :] + b_ref[...]
        for d in (1, 2, 3):
            acc = acc + pltpu.roll(xx, d, axis=0)[HALO:, :] * w[3 - d:4 - d, :]
        if act:
            acc = acc * _sigmoid(acc)
        o_ref[...] = acc.astype(BF16)

    return pl.pallas_call(
        body, name=name, grid=(nb, ns, width // tc),
        in_specs=[pl.BlockSpec((tt, tc), lambda b, s, j: (b * ns + s, cb0 + j)),
                  pl.BlockSpec((HALO, tc), lambda b, s, j: (jnp.maximum((b * seq + s * tt) // HALO - 1, 0), cb0 + j)),
                  pl.BlockSpec((4, tc), lambda b, s, j: (0, j)),
                  pl.BlockSpec((1, tc), lambda b, s, j: (0, j))],
        out_specs=pl.BlockSpec((tt, tc), lambda b, s, j: (b * ns + s, j)),
        out_shape=jax.ShapeDtypeStruct((t, width), BF16),
        compiler_params=_cp("parallel", "parallel", "parallel"),
    )(src, src, w4, bias)


def conv_bwd(src, col0, width, w4, bias, dout, nb, seq, act, name, side=None):
    t = src.shape[0]
    tt = min(512, seq)
    tc = 512
    ns = seq // tt
    cb0 = col0 // tc
    nh = t // HALO

    def body(cur_ref, prev_ref, next_ref, w_ref, b_ref, do_ref, don_ref, dx_ref, acc_ref):
        b, s = pl.program_id(1), pl.program_id(2)

        @pl.when((b == 0) & (s == 0))
        def _():
            acc_ref[...] = jnp.zeros_like(acc_ref)

        cur = cur_ref[...].astype(F32)
        prev = jnp.where(s == 0, 0.0, prev_ref[...].astype(F32))
        nxt = next_ref[...].astype(F32)
        xx = jnp.concatenate([prev, cur, nxt], axis=0)
        w = w_ref[...]
        do_ext = jnp.concatenate([do_ref[...].astype(F32),
                                  jnp.where(s == ns - 1, 0.0, don_ref[...].astype(F32))], axis=0)
        ne = tt + HALO
        xs = [xx[HALO:HALO + ne, :]] + [pltpu.roll(xx, d, axis=0)[HALO:HALO + ne, :] for d in (1, 2, 3)]
        if act:
            c = b_ref[...] + xs[0] * w[3:4, :] + xs[1] * w[2:3, :] + xs[2] * w[1:2, :] + xs[3] * w[0:1, :]
            sg = _sigmoid(c)
            dc = do_ext * (sg * (1.0 + c * (1.0 - sg)))
        else:
            dc = do_ext
        dx = dc[:tt, :] * w[3:4, :]
        for d in (1, 2, 3):
            dx = dx + pltpu.roll(dc, ne - d, axis=0)[:tt, :] * w[3 - d:4 - d, :]
        dx_ref[...] = dx.astype(BF16)
        dcc = dc[:tt, :]
        rows = [_colsum(dcc * xs[3 - r][:tt, :]) for r in range(4)] + [_colsum(dcc)]
        acc_ref[...] += jnp.concatenate(rows + [jnp.zeros((3, tc), F32)], axis=0)

    return _call(
        body, name=name, grid=(width // tc, nb, ns), side=side, sem=("parallel", "arbitrary", "arbitrary"),
        args=(src, src, src, w4, bias, dout, dout),
        in_specs=[pl.BlockSpec((tt, tc), lambda j, b, s: (b * ns + s, cb0 + j)),
                  pl.BlockSpec((HALO, tc), lambda j, b, s: (jnp.maximum((b * seq + s * tt) // HALO - 1, 0), cb0 + j)),
                  pl.BlockSpec((HALO, tc), lambda j, b, s: (jnp.minimum((b * seq + (s + 1) * tt) // HALO, nh - 1), cb0 + j)),
                  pl.BlockSpec((4, tc), lambda j, b, s: (0, j)),
                  pl.BlockSpec((1, tc), lambda j, b, s: (0, j)),
                  pl.BlockSpec((tt, tc), lambda j, b, s: (b * ns + s, j)),
                  pl.BlockSpec((HALO, tc), lambda j, b, s: (jnp.minimum((b * seq + (s + 1) * tt) // HALO, nh - 1), j))],
        out_specs=[pl.BlockSpec((tt, tc), lambda j, b, s: (b * ns + s, j)),
                   pl.BlockSpec((8, tc), lambda j, b, s: (0, j))],
        out_shape=[jax.ShapeDtypeStruct((t, width), BF16), jax.ShapeDtypeStruct((8, width), F32)])


def _lru_gates(xa, wa_ref, wx_ref, ba, bx, sp):
    nblk = D_MODEL // LRU_BLOCK
    pr = jnp.concatenate([_dot(xa[:, j * LRU_BLOCK:(j + 1) * LRU_BLOCK], wa_ref[j]) for j in range(nblk)], axis=1) + ba
    pi = jnp.concatenate([_dot(xa[:, j * LRU_BLOCK:(j + 1) * LRU_BLOCK], wx_ref[j]) for j in range(nblk)], axis=1) + bx
    r = _sigmoid(pr)
    i = _sigmoid(pi)
    log_a = (-LRU_C * r) * sp
    return r, i, jnp.exp(log_a), _neg_expm1(2.0 * log_a)


def lru_fwd(xa, proj, wa_bd, wx_bd, vec, nb, seq):
    t = xa.shape[0]
    tc = min(512, seq)
    nk = seq // tc
    gb = C_LRU_G // D_MODEL

    def body(xa_ref, g_ref, wa_ref, wx_ref, vec_ref, ya_ref, h_ref, a_scr, u_scr, hc_scr):
        @pl.when(pl.program_id(1) == 0)
        def _():
            hc_scr[...] = jnp.zeros_like(hc_scr)

        xa_v = xa_ref[...]
        v = vec_ref[...]
        r, i, a, e = _lru_gates(xa_v, wa_ref, wx_ref, v[0:1, :], v[1:2, :], v[2:3, :])
        a_scr[...] = a
        u_scr[...] = jnp.sqrt(e) * (i * xa_v.astype(F32))

        def step(tt, h):
            h = a_scr[pl.ds(tt, 1), :] * h + u_scr[pl.ds(tt, 1), :]
            h_ref[pl.ds(tt, 1), :] = h
            return h

        hc_scr[...] = lax.fori_loop(0, tc, step, hc_scr[...], unroll=8)
        gel, _ = _gelu_and_grad(g_ref[...].astype(F32))
        ya_ref[...] = (h_ref[...] * gel).astype(BF16)

    return pl.pallas_call(
        body, name="lru_fwd", grid=(nb, nk),
        in_specs=[pl.BlockSpec((tc, D_MODEL), lambda b, k: (b * nk + k, 0)),
                  pl.BlockSpec((tc, D_MODEL), lambda b, k: (b * nk + k, gb)),
                  VMEM_FULL, VMEM_FULL, VMEM_FULL],
        out_specs=[pl.BlockSpec((tc, D_MODEL), lambda b, k: (b * nk + k, 0)),
                   pl.BlockSpec((tc, D_MODEL), lambda b, k: (b * nk + k, 0))],
        out_shape=[jax.ShapeDtypeStruct((t, D_MODEL), BF16), jax.ShapeDtypeStruct((t, D_MODEL), F32)],
        scratch_shapes=[pltpu.VMEM((tc, D_MODEL), F32), pltpu.VMEM((tc, D_MODEL), F32), pltpu.VMEM((1, D_MODEL), F32)],
        compiler_params=_cp("arbitrary", "arbitrary"),
    )(xa, proj, wa_bd, wx_bd, vec)


def lru_bwd(dya, xa, proj, h, wa_bd, wx_bd, vec, nb, seq, side=None):
    t = xa.shape[0]
    tc = min(512, seq)
    nk = seq // tc
    gb = C_LRU_G // D_MODEL
    nblk = D_MODEL // LRU_BLOCK

    def chunk(b, k):
        return b * nk + (nk - 1 - k)

    def body(dya_ref, xa_ref, g_ref, h_ref, hp_ref, wa_ref, wx_ref, vec_ref,
             dxa_ref, dg_ref, dwa_ref, dwx_ref, dvec_ref, a_scr, dh_scr, c_scr):
        b, k = pl.program_id(0), pl.program_id(1)

        @pl.when((b == 0) & (k == 0))
        def _():
            dwa_ref[...] = jnp.zeros_like(dwa_ref)
            dwx_ref[...] = jnp.zeros_like(dwx_ref)
            dvec_ref[...] = jnp.zeros_like(dvec_ref)

        @pl.when(k == 0)
        def _():
            c_scr[...] = jnp.zeros_like(c_scr)

        xa_v = xa_ref[...]
        xaf = xa_v.astype(F32)
        v = vec_ref[...]
        sp = v[2:3, :]
        r, i, a, e = _lru_gates(xa_v, wa_ref, wx_ref, v[0:1, :], v[1:2, :], sp)
        gel, dgel = _gelu_and_grad(g_ref[...].astype(F32))
        hv = h_ref[...]
        dyv = dya_ref[...].astype(F32)
        dg_ref[...] = (dyv * hv * dgel).astype(BF16)
        a_scr[...] = a
        dh_scr[...] = dyv * gel

        def step(j, c):
            tt = tc - 1 - j
            dh = dh_scr[pl.ds(tt, 1), :] + c
            dh_scr[pl.ds(tt, 1), :] = dh
            return a_scr[pl.ds(tt, 1), :] * dh

        c_scr[...] = lax.fori_loop(0, tc, step, c_scr[...], unroll=8)
        dh = dh_scr[...]
        h_last = jnp.where(k == nk - 1, 0.0, hp_ref[HALO // 2 - 1:HALO // 2, :])
        row = lax.broadcasted_iota(jnp.int32, (tc, 1), 0)
        h_prev = jnp.where(row == 0, h_last, pltpu.roll(hv, 1, axis=0))
        s = jnp.sqrt(e)
        da = dh * h_prev
        ix = i * xaf
        dlog_a = da * a - (dh * ix) * (a * a) * lax.rsqrt(jnp.maximum(e, 1e-30))
        di = dh * s * xaf
        dpr = (dlog_a * (-LRU_C * sp)) * (r * (1.0 - r))
        dpi = di * (i * (1.0 - i))
        dprb, dpib = dpr.astype(BF16), dpi.astype(BF16)
        dxa = dh * s * i
        dxa = dxa + jnp.concatenate(
            [_dot_nt(dprb[:, j * LRU_BLOCK:(j + 1) * LRU_BLOCK], wa_ref[j])
             + _dot_nt(dpib[:, j * LRU_BLOCK:(j + 1) * LRU_BLOCK], wx_ref[j]) for j in range(nblk)], axis=1)
        dxa_ref[...] = dxa.astype(BF16)
        for j in range(nblk):
            sl = slice(j * LRU_BLOCK, (j + 1) * LRU_BLOCK)
            dwa_ref[j] += _dot_tn(xa_v[:, sl], dprb[:, sl])
            dwx_ref[j] += _dot_tn(xa_v[:, sl], dpib[:, sl])
        dvec_ref[...] += jnp.concatenate(
            [_colsum(dpr), _colsum(dpi), _colsum(dlog_a * (-LRU_C * r)), jnp.zeros((5, D_MODEL), F32)], axis=0)

    hh = HALO // 2
    return _call(
        body, name="lru_bwd", grid=(nb, nk), side=side, sem=("arbitrary", "arbitrary"),
        args=(dya, xa, proj, h, h, wa_bd, wx_bd, vec),
        in_specs=[pl.BlockSpec((tc, D_MODEL), lambda b, k: (chunk(b, k), 0)),
                  pl.BlockSpec((tc, D_MODEL), lambda b, k: (chunk(b, k), 0)),
                  pl.BlockSpec((tc, D_MODEL), lambda b, k: (chunk(b, k), gb)),
                  pl.BlockSpec((tc, D_MODEL), lambda b, k: (chunk(b, k), 0)),
                  pl.BlockSpec((hh, D_MODEL), lambda b, k: (jnp.maximum(chunk(b, k) * (tc // hh) - 1, 0), 0)),
                  VMEM_FULL, VMEM_FULL, VMEM_FULL],
        out_specs=[pl.BlockSpec((tc, D_MODEL), lambda b, k: (chunk(b, k), 0)),
                   pl.BlockSpec((tc, D_MODEL), lambda b, k: (chunk(b, k), 0)),
                   pl.BlockSpec((nblk, LRU_BLOCK, LRU_BLOCK), lambda b, k: (0, 0, 0)),
                   pl.BlockSpec((nblk, LRU_BLOCK, LRU_BLOCK), lambda b, k: (0, 0, 0)),
                   pl.BlockSpec((8, D_MODEL), lambda b, k: (0, 0))],
        out_shape=[jax.ShapeDtypeStruct((t, D_MODEL), BF16), jax.ShapeDtypeStruct((t, D_MODEL), BF16),
                   jax.ShapeDtypeStruct((nblk, LRU_BLOCK, LRU_BLOCK), F32),
                   jax.ShapeDtypeStruct((nblk, LRU_BLOCK, LRU_BLOCK), F32),
                   jax.ShapeDtypeStruct((8, D_MODEL), F32)],
        scratch_shapes=[pltpu.VMEM((tc, D_MODEL), F32), pltpu.VMEM((tc, D_MODEL), F32), pltpu.VMEM((1, D_MODEL), F32)])


def merge_fwd(ya_in, yb_in, proj, x2, mod8, bgate, post1, w_pa, w_pb, w_out, seq):
    t = x2.shape[0]
    tm = min(512, seq)
    per_seq = seq // tm
    gcb = C_GATES // SSD_INNER

    def body(ya_ref, yb_ref, gt_ref, x_ref, mod_ref, bg_ref, post_ref, wpa_ref, wpb_ref, wo_ref,
             yab_ref, out1_ref, x1_ref):
        y_a = _dot(ya_ref[...], wpa_ref[...])
        y_b = _dot(yb_ref[...], wpb_ref[...])
        g = _sigmoid(gt_ref[...].astype(F32) + bg_ref[...])
        merged = g[:, :D_MODEL] * y_a + g[:, D_MODEL:] * y_b
        out1 = _dot(merged.astype(BF16), wo_ref[...])
        n = out1 * lax.rsqrt(_rowmean(out1 * out1) + EPS)
        yab_ref[...] = jnp.concatenate([y_a, y_b], axis=1).astype(BF16)
        out1_ref[...] = out1
        x1_ref[...] = x_ref[...] + mod_ref[0][2:3, :] * (n * post_ref[...])

    row = lambda w: pl.BlockSpec((tm, w), lambda i: (i, 0))
    return pl.pallas_call(
        body, name="merge_fwd", grid=(t // tm,),
        in_specs=[row(D_MODEL), row(SSD_INNER), pl.BlockSpec((tm, SSD_INNER), lambda i: (i, gcb)), row(D_MODEL),
                  pl.BlockSpec((1, 8, D_MODEL), lambda i: (i // per_seq, 0, 0)),
                  VMEM_FULL, VMEM_FULL, VMEM_FULL, VMEM_FULL, VMEM_FULL],
        out_specs=[row(SSD_INNER), row(D_MODEL), row(D_MODEL)],
        out_shape=[jax.ShapeDtypeStruct((t, SSD_INNER), BF16), jax.ShapeDtypeStruct((t, D_MODEL), F32),
                   jax.ShapeDtypeStruct((t, D_MODEL), F32)],
        compiler_params=_cp("parallel"),
    )(ya_in, yb_in, proj, x2, mod8, bgate, post1, w_pa, w_pb, w_out)


def merge_bwd(dx1, out1, yab, proj, mod8, bgate, post1, w_pa, w_pb, w_out, nb, seq):
    t = dx1.shape[0]
    tm = min(512, seq)
    per_seq = seq // tm
    gcb = C_GATES // SSD_INNER

    def body(dx1_ref, out1_ref, yab_ref, gt_ref, mod_ref, bg_ref, post_ref, wpa_ref, wpb_ref, wo_ref,
             dya_ref, dyb_ref, dgt_ref, dyab_ref, dout1_ref, mg_ref, vacc_ref, dmod_ref):
        b, s = pl.program_id(0), pl.program_id(1)

        @pl.when((b == 0) & (s == 0))
        def _():
            vacc_ref[...] = jnp.zeros_like(vacc_ref)

        @pl.when(s == 0)
        def _():
            dmod_ref[...] = jnp.zeros_like(dmod_ref)

        dx1v = dx1_ref[...]
        out1 = out1_ref[...]
        post = post_ref[...]
        rs = lax.rsqrt(_rowmean(out1 * out1) + EPS)
        n = out1 * rs
        do = dx1v * mod_ref[0][2:3, :]
        dn = do * post
        dout1 = rs * (dn - n * _rowmean(dn * n))
        dout1b = dout1.astype(BF16)
        dout1_ref[...] = dout1b
        dmerged = _dot_nt(dout1b, wo_ref[...])
        g = _sigmoid(gt_ref[...].astype(F32) + bg_ref[...])
        yab_v = yab_ref[...].astype(F32)
        gy = g * yab_v
        mg_ref[...] = (gy[:, :D_MODEL] + gy[:, D_MODEL:]).astype(BF16)
        dm2 = jnp.concatenate([dmerged, dmerged], axis=1)
        dyab = (dm2 * g).astype(BF16)
        dyab_ref[...] = dyab
        dgt = dm2 * gy * (1.0 - g)
        dgt_ref[...] = dgt.astype(BF16)
        dya_ref[...] = _dot_nt(dyab[:, :D_MODEL], wpa_ref[...]).astype(BF16)
        dyb_ref[...] = _dot_nt(dyab[:, D_MODEL:], wpb_ref[...]).astype(BF16)
        vacc_ref[...] += jnp.concatenate(
            [_colsum(dgt), jnp.concatenate([_colsum(do * n), jnp.zeros((1, D_MODEL), F32)], axis=1),
             jnp.zeros((6, SSD_INNER), F32)], axis=0)
        dmod_ref[0] += jnp.concatenate(
            [jnp.zeros((2, D_MODEL), F32), _colsum(dx1v * (n * post)), jnp.zeros((5, D_MODEL), F32)], axis=0)

    row = lambda w: pl.BlockSpec((tm, w), lambda b, s: (b * per_seq + s, 0))
    return pl.pallas_call(
        body, name="merge_bwd", grid=(nb, per_seq),
        in_specs=[row(D_MODEL), row(D_MODEL), row(SSD_INNER),
                  pl.BlockSpec((tm, SSD_INNER), lambda b, s: (b * per_seq + s, gcb)),
                  pl.BlockSpec((1, 8, D_MODEL), lambda b, s: (b, 0, 0)),
                  VMEM_FULL, VMEM_FULL, VMEM_FULL, VMEM_FULL, VMEM_FULL],
        out_specs=[row(D_MODEL), row(SSD_INNER), row(SSD_INNER), row(SSD_INNER), row(D_MODEL), row(D_MODEL),
                   pl.BlockSpec((8, SSD_INNER), lambda b, s: (0, 0)),
                   pl.BlockSpec((1, 8, D_MODEL), lambda b, s: (b, 0, 0))],
        out_shape=[jax.ShapeDtypeStruct((t, D_MODEL), BF16), jax.ShapeDtypeStruct((t, SSD_INNER), BF16),
                   jax.ShapeDtypeStruct((t, SSD_INNER), BF16), jax.ShapeDtypeStruct((t, SSD_INNER), BF16),
                   jax.ShapeDtypeStruct((t, D_MODEL), BF16), jax.ShapeDtypeStruct((t, D_MODEL), BF16),
                   jax.ShapeDtypeStruct((8, SSD_INNER), F32), jax.ShapeDtypeStruct((nb, 8, D_MODEL), F32)],
        compiler_params=_cp("arbitrary", "arbitrary"),
    )(dx1, out1, yab, proj, mod8, bgate, post1, w_pa, w_pb, w_out)


def mlp_fwd_bwd(x1, tgt, mod8, pre2, post2, w_ff1, w_ff2, nb, seq):
    t = x1.shape[0]
    tm = min(256, seq)
    per_seq = seq // tm
    fc = 1024
    nfc = D_FF // fc

    def body(x1_ref, tgt_ref, mod_ref, pre_ref, post_ref, w1_ref, w2_ref,
             dx1_ref, h2_ref, da1_ref, act_ref, dy2_ref, loss_ref, vacc_ref, dmod_ref, r_scr):
        b, s = pl.program_id(0), pl.program_id(1)

        @pl.when((b == 0) & (s == 0))
        def _():
            vacc_ref[...] = jnp.zeros_like(vacc_ref)
            loss_ref[...] = jnp.zeros_like(loss_ref)

        @pl.when(s == 0)
        def _():
            dmod_ref[...] = jnp.zeros_like(dmod_ref)

        m = mod_ref[0]
        sh2, sc2, g2 = m[3:4, :], m[4:5, :], m[5:6, :]
        pre, post = pre_ref[...], post_ref[...]
        x1v = x1_ref[...]
        rs1 = lax.rsqrt(_rowmean(x1v * x1v) + EPS)
        n1 = x1v * rs1
        y1 = n1 * pre
        h2b = (y1 * (1.0 + sc2) + sh2).astype(BF16)
        h2_ref[...] = h2b
        y2 = jnp.zeros((tm, D_MODEL), F32)
        for c in range(nfc):
            r = jnp.maximum(_dot(h2b, w1_ref[:, c * fc:(c + 1) * fc]), 0.0)
            r_scr[:, c * fc:(c + 1) * fc] = r
            a = (r * r).astype(BF16)
            act_ref[:, c * fc:(c + 1) * fc] = a
            y2 = y2 + _dot(a, w2_ref[c * fc:(c + 1) * fc, :])
        rs2 = lax.rsqrt(_rowmean(y2 * y2) + EPS)
        n2 = y2 * rs2
        o2 = n2 * post
        diff = x1v + g2 * o2 - tgt_ref[...]
        loss_ref[...] += 0.5 * jnp.sum(_rowmean(diff * diff))
        dx2 = diff * (1.0 / D_MODEL)
        do2 = dx2 * g2
        dn2 = do2 * post
        dy2b = (rs2 * (dn2 - n2 * _rowmean(dn2 * n2))).astype(BF16)
        dy2_ref[...] = dy2b
        dh2 = jnp.zeros((tm, D_MODEL), F32)
        for c in range(nfc):
            dact = _dot_nt(dy2b, w2_ref[c * fc:(c + 1) * fc, :])
            da = (dact * (2.0 * r_scr[:, c * fc:(c + 1) * fc])).astype(BF16)
            da1_ref[:, c * fc:(c + 1) * fc] = da
            dh2 = dh2 + _dot_nt(da, w1_ref[:, c * fc:(c + 1) * fc])
        dy1 = dh2 * (1.0 + sc2)
        dn1 = dy1 * pre
        dx1_ref[...] = dx2 + rs1 * (dn1 - n1 * _rowmean(dn1 * n1))
        vacc_ref[...] += jnp.concatenate([_colsum(dy1 * n1), _colsum(do2 * n2), jnp.zeros((6, D_MODEL), F32)], axis=0)
        dmod_ref[0] += jnp.concatenate(
            [jnp.zeros((3, D_MODEL), F32), _colsum(dh2), _colsum(dh2 * y1), _colsum(dx2 * o2),
             jnp.zeros((2, D_MODEL), F32)], axis=0)

    row = lambda w: pl.BlockSpec((tm, w), lambda b, s: (b * per_seq + s, 0))
    return pl.pallas_call(
        body, name="mlp_fwd_bwd", grid=(nb, per_seq),
        in_specs=[row(D_MODEL), row(D_MODEL), pl.BlockSpec((1, 8, D_MODEL), lambda b, s: (b, 0, 0)),
                  VMEM_FULL, VMEM_FULL, VMEM_FULL, VMEM_FULL],
        out_specs=[row(D_MODEL), row(D_MODEL), row(D_FF), row(D_FF), row(D_MODEL),
                   pl.BlockSpec((8, 128), lambda b, s: (0, 0)),
                   pl.BlockSpec((8, D_MODEL), lambda b, s: (0, 0)),
                   pl.BlockSpec((1, 8, D_MODEL), lambda b, s: (b, 0, 0))],
        out_shape=[jax.ShapeDtypeStruct((t, D_MODEL), F32), jax.ShapeDtypeStruct((t, D_MODEL), BF16),
                   jax.ShapeDtypeStruct((t, D_FF), BF16), jax.ShapeDtypeStruct((t, D_FF), BF16),
                   jax.ShapeDtypeStruct((t, D_MODEL), BF16), jax.ShapeDtypeStruct((8, 128), F32),
                   jax.ShapeDtypeStruct((8, D_MODEL), F32), jax.ShapeDtypeStruct((nb, 8, D_MODEL), F32)],
        scratch_shapes=[pltpu.VMEM((tm, D_FF), F32)],
        compiler_params=_cp("arbitrary", "arbitrary"),
    )(x1, tgt, mod8, pre2, post2, w_ff1, w_ff2)


_PIECES = ((C_LRU_X, 1024), (C_LRU_G, 1024), (C_Z, 2048), (C_XBC, 4096), (C_GATES, 2048))
_NP = len(_PIECES)


def _piece_of(k, tk):
    col = k * tk
    for p, (c0, w) in enumerate(_PIECES):
        if c0 <= col < c0 + w:
            return p, (col - c0) // tk
    raise ValueError(col)


def in_proj_bwd(pieces, ddt, dx1, x2, mod8, pre1, w_main, w_dt, nb, seq, side=None):
    t = x2.shape[0]
    tm = min(512, seq)
    per_seq = seq // tm
    tk = 1024
    nk = PROJ_MAIN // tk
    where = [_piece_of(k, tk) for k in range(nk)]

    def piece_spec(p):
        first = min(k for k in range(nk) if where[k][0] == p)
        nblk = _PIECES[p][1] // tk
        return pl.BlockSpec((tm, tk), lambda b, s, k: (b * per_seq + s, jnp.clip(k - first, 0, nblk - 1)))

    def body(*refs):
        prefs = refs[:_NP]
        ddt_ref, dx1_ref, x_ref, mod_ref, pre_ref, w_ref, wdt_ref, gx_ref, vacc_ref, dmod_ref, acc_ref = refs[_NP:]
        b, s, k = pl.program_id(0), pl.program_id(1), pl.program_id(2)

        @pl.when((b == 0) & (s == 0) & (k == 0))
        def _():
            vacc_ref[...] = jnp.zeros_like(vacc_ref)

        @pl.when((s == 0) & (k == 0))
        def _():
            dmod_ref[...] = jnp.zeros_like(dmod_ref)

        @pl.when(k == 0)
        def _():
            acc_ref[...] = _dot_nt(ddt_ref[...], wdt_ref[...])

        for kk in range(nk):
            @pl.when(k == kk)
            def _(kk=kk):
                acc_ref[...] += _dot_nt(prefs[where[kk][0]][...], w_ref[...])

        @pl.when(k == nk - 1)
        def _():
            dh = acc_ref[...]
            m = mod_ref[0]
            pre = pre_ref[...]
            xv = x_ref[...]
            rs = lax.rsqrt(_rowmean(xv * xv) + EPS)
            n = xv * rs
            dy = dh * (1.0 + m[1:2, :])
            dn = dy * pre
            gx_ref[...] = dx1_ref[...] + rs * (dn - n * _rowmean(dn * n))
            vacc_ref[...] += jnp.concatenate([_colsum(dy * n), jnp.zeros((7, D_MODEL), F32)], axis=0)
            dmod_ref[0] += jnp.concatenate([_colsum(dh), _colsum(dh * (n * pre)), jnp.zeros((6, D_MODEL), F32)], axis=0)

    row = lambda w: pl.BlockSpec((tm, w), lambda b, s, k: (b * per_seq + s, 0))
    return _call(
        body, name="in_proj_bwd", grid=(nb, per_seq, nk), side=side, sem=("arbitrary", "arbitrary", "arbitrary"),
        args=(*pieces, ddt, dx1, x2, mod8, pre1, w_main, w_dt),
        in_specs=[piece_spec(p) for p in range(_NP)] + [
            row(128), row(D_MODEL), row(D_MODEL), pl.BlockSpec((1, 8, D_MODEL), lambda b, s, k: (b, 0, 0)),
            pl.BlockSpec((1, D_MODEL), lambda b, s, k: (0, 0)),
            pl.BlockSpec((D_MODEL, tk), lambda b, s, k: (0, k)),
            pl.BlockSpec((D_MODEL, 128), lambda b, s, k: (0, 0))],
        out_specs=[row(D_MODEL), pl.BlockSpec((8, D_MODEL), lambda b, s, k: (0, 0)),
                   pl.BlockSpec((1, 8, D_MODEL), lambda b, s, k: (b, 0, 0))],
        out_shape=[jax.ShapeDtypeStruct((t, D_MODEL), F32), jax.ShapeDtypeStruct((8, D_MODEL), F32),
                   jax.ShapeDtypeStruct((nb, 8, D_MODEL), F32)],
        scratch_shapes=[pltpu.VMEM((tm, D_MODEL), F32)])


def in_proj_wgrad(h1, pieces, ddt, name="in_proj_wgrad"):
    t = h1.shape[0]
    tt = min(1024, t)
    tn = 1024
    nn = PROJ_MAIN // tn
    nt = t // tt
    where = [_piece_of(n, tn) for n in range(nn)]

    def piece_spec(p):
        first = min(n for n in range(nn) if where[n][0] == p)
        nblk = _PIECES[p][1] // tn
        return pl.BlockSpec((tt, tn), lambda n, k: (jnp.where((n >= first) & (n < first + nblk), k, 0),
                                                    jnp.clip(n - first, 0, nblk - 1)))

    def body(h_ref, *refs):
        prefs = refs[:_NP]
        ddt_ref, dw_ref, dwdt_ref, acc_ref, accdt_ref = refs[_NP:]
        n, k = pl.program_id(0), pl.program_id(1)
        hv = h_ref[...]
        for nn_ in range(nn):
            @pl.when(n == nn_)
            def _(nn_=nn_):
                p = _dot_tn(hv, prefs[where[nn_][0]][...])

                @pl.when(k == 0)
                def _():
                    acc_ref[...] = p

                @pl.when(k > 0)
                def _():
                    acc_ref[...] += p

        @pl.when(n == 0)
        def _():
            p = _dot_tn(hv, ddt_ref[...])

            @pl.when(k == 0)
            def _():
                accdt_ref[...] = p

            @pl.when(k > 0)
            def _():
                accdt_ref[...] += p

        @pl.when(k == nt - 1)
        def _():
            dw_ref[...] = acc_ref[...].astype(BF16)

        @pl.when((n == 0) & (k == nt - 1))
        def _():
            dwdt_ref[...] = accdt_ref[...].astype(BF16)

    return pl.pallas_call(
        body, name=name, grid=(nn, nt),
        in_specs=[pl.BlockSpec((tt, D_MODEL), lambda n, k: (k, 0))] + [piece_spec(p) for p in range(_NP)]
        + [pl.BlockSpec((tt, 128), lambda n, k: (k, 0))],
        out_specs=[pl.BlockSpec((D_MODEL, tn), lambda n, k: (0, n)), pl.BlockSpec((D_MODEL, 128), lambda n, k: (0, 0))],
        out_shape=[jax.ShapeDtypeStruct((D_MODEL, PROJ_MAIN), BF16), jax.ShapeDtypeStruct((D_MODEL, 128), BF16)],
        scratch_shapes=[pltpu.VMEM((D_MODEL, tn), F32), pltpu.VMEM((D_MODEL, 128), F32)],
        compiler_params=_cp("arbitrary", "arbitrary"),
    )(h1, *pieces, ddt)


def _log1p(u):
    w = 1.0 + u
    return jnp.log(w) - ((w - 1.0) - u) / w


def _softplus(x):
    return jnp.maximum(x, 0.0) + _log1p(jnp.exp(-jnp.abs(x)))


def _head_mask(h):
    lane = lax.broadcasted_iota(jnp.int32, (1, SSD_GW), 1)
    return (lane >= SSD_P * h) & (lane < SSD_P * (h + 1))


def _expand4(m, g):
    lane = lax.broadcasted_iota(jnp.int32, (1, SSD_GW), 1)
    col = lambda h: m[:, 4 * g + h:4 * g + h + 1]
    return jnp.where(lane < SSD_P, col(0), jnp.where(lane < 2 * SSD_P, col(1), jnp.where(lane < 3 * SSD_P, col(2), col(3))))


def _reduce4(v, g):
    lane = lax.broadcasted_iota(jnp.int32, (1, SSD_N), 1)
    out = jnp.zeros((v.shape[0], SSD_N), F32)
    for h in range(4):
        s = jnp.sum(jnp.where(_head_mask(h), v, 0.0), axis=1, keepdims=True)
        out = out + jnp.where(lane == 4 * g + h, s, 0.0)
    return out


def _ssd_heads(dtraw, hp, tri):
    xdt = dtraw + hp[0:1, :]
    dt = _softplus(xdt)
    cs = _dot_hi(tri, dt * hp[1:2, :])
    cs_last = cs[SSD_L - 1:SSD_L, :]
    return dict(xdt=xdt, dt=dt, cs=cs, cs_t=cs.T, e=jnp.exp(cs), w=jnp.exp(cs_last - cs), el=jnp.exp(cs_last))


def _ssd_group(g, hd, xs_b, bm_b, cm_b, d_x, st):
    ll = SSD_L
    xs = xs_b.astype(F32)
    cs, cs_t = hd["cs"], hd["cs_t"]
    e_x, w_x, el_x, dt_x = _expand4(hd["e"], g), _expand4(hd["w"], g), _expand4(hd["el"], g), _expand4(hd["dt"], g)
    xd = xs * dt_x
    gcb = _dot_nt(cm_b, bm_b)
    ri = lax.broadcasted_iota(jnp.int32, (ll, ll), 0)
    ci = lax.broadcasted_iota(jnp.int32, (ll, ll), 1)
    dks = []
    ydiag = jnp.zeros((ll, SSD_GW), F32)
    for h in range(4):
        k = 4 * g + h
        dk = jnp.exp(jnp.where(ri >= ci, cs[:, k:k + 1] - cs_t[k:k + 1, :], -1e30))
        dks.append(dk)
        ydiag = ydiag + _dot((gcb * dk).astype(BF16), jnp.where(_head_mask(h), xd, 0.0).astype(BF16))
    yoff = _dot(cm_b, st.astype(BF16)) * e_x
    y = ydiag + yoff + d_x * xs
    st_new = st * el_x + _dot(bm_b.astype(F32).T.astype(BF16), (xd * w_x).astype(BF16))
    return dict(xs=xs, e_x=e_x, w_x=w_x, el_x=el_x, dt_x=dt_x, xd=xd, gcb=gcb, dks=dks, yoff=yoff, y=y, st_new=st_new,
                ri=ri, ci=ci)


def ssd_consts():
    hh = np.arange(SSD_N)
    tri = (hh[:, None] >= hh[None, :]).astype(np.float32)
    return jnp.asarray(tri), jnp.asarray(tri.T)


def ssd_params(dt_bias, a_log, d_skip, norm_w):
    padh = lambda v: jnp.pad(v.reshape(1, SSD_HEADS), ((0, 0), (0, SSD_N - SSD_HEADS)))
    hp = jnp.concatenate([padh(dt_bias), padh(-jnp.exp(a_log)), jnp.zeros((6, SSD_N), F32)], axis=0)
    lp = jnp.concatenate([norm_w.reshape(1, SSD_INNER), jnp.repeat(d_skip, SSD_P).reshape(1, SSD_INNER),
                          jnp.zeros((6, SSD_INNER), F32)], axis=0)
    return hp, lp


def _b_cols(g):
    return slice(SSD_INNER + g * SSD_N, SSD_INNER + (g + 1) * SSD_N)


def _c_cols(g):
    return slice(SSD_INNER + (SSD_G + g) * SSD_N, SSD_INNER + (SSD_G + g + 1) * SSD_N)


def _ssd_specs(nc, rc):
    return [pl.BlockSpec((SSD_L, 2 * SSD_INNER), lambda b, c: (b * nc + rc(c), 0)),
            pl.BlockSpec((SSD_L, SSD_INNER), lambda b, c: (b * nc + rc(c), C_Z // SSD_INNER)),
            pl.BlockSpec((SSD_L, SSD_N), lambda b, c: (b * nc + rc(c), 0))]


def ssd_fwd(xbc, proj, dtraw, hp, lp, tri, nb, seq):
    t = xbc.shape[0]
    nc = seq // SSD_L

    def body(xbc_ref, z_ref, dt_ref, hp_ref, lp_ref, tri_ref, y_ref, sts_ref, st_scr):
        @pl.when(pl.program_id(1) == 0)
        def _():
            st_scr[...] = jnp.zeros_like(st_scr)

        hd = _ssd_heads(dt_ref[...], hp_ref[...], tri_ref[...])
        for g in range(SSD_G):
            gs = slice(g * SSD_GW, (g + 1) * SSD_GW)
            st = st_scr[g]
            sts_ref[0, g] = st
            f = _ssd_group(g, hd, xbc_ref[:, gs], xbc_ref[:, _b_cols(g)], xbc_ref[:, _c_cols(g)], lp_ref[1:2, gs], st)
            st_scr[g] = f["st_new"]
            zf = z_ref[:, gs].astype(F32)
            yg = f["y"] * (zf * _sigmoid(zf))
            y_ref[:, gs] = (yg * lax.rsqrt(_rowmean(yg * yg) + EPS) * lp_ref[0:1, gs]).astype(BF16)

    return pl.pallas_call(
        body, name="ssd_fwd", grid=(nb, nc),
        in_specs=_ssd_specs(nc, lambda c: c) + [VMEM_FULL, VMEM_FULL, VMEM_FULL],
        out_specs=[pl.BlockSpec((SSD_L, SSD_INNER), lambda b, c: (b * nc + c, 0)),
                   pl.BlockSpec((1, SSD_G, SSD_N, SSD_GW), lambda b, c: (b * nc + c, 0, 0, 0))],
        out_shape=[jax.ShapeDtypeStruct((t, SSD_INNER), BF16),
                   jax.ShapeDtypeStruct((nb * nc, SSD_G, SSD_N, SSD_GW), F32)],
        scratch_shapes=[pltpu.VMEM((SSD_G, SSD_N, SSD_GW), F32)],
        compiler_params=_cp("arbitrary", "arbitrary"),
    )(xbc, proj, dtraw, hp, lp, tri)


def ssd_bwd(xbc, proj, dtraw, hp, lp, tri, triu, states, dyn, nb, seq):
    t = xbc.shape[0]
    nc = seq // SSD_L
    ll = SSD_L

    def body(xbc_ref, z_ref, dt_ref, sts_ref, dy_ref, hp_ref, lp_ref, tri_ref, triu_ref,
             dxbc_ref, dz_ref, ddt_ref, hpg_ref, lpg_ref, dst_scr):
        b, c_i = pl.program_id(0), pl.program_id(1)

        @pl.when((b == 0) & (c_i == 0))
        def _():
            hpg_ref[...] = jnp.zeros_like(hpg_ref)
            lpg_ref[...] = jnp.zeros_like(lpg_ref)

        @pl.when(c_i == 0)
        def _():
            dst_scr[...] = jnp.zeros_like(dst_scr)

        hp = hp_ref[...]
        hd = _ssd_heads(dt_ref[...], hp, tri_ref[...])
        cs, cs_t = hd["cs"], hd["cs_t"]
        lane = lax.broadcasted_iota(jnp.int32, (1, SSD_N), 1)
        dcs = jnp.zeros((ll, SSD_N), F32)
        last = jnp.zeros((1, SSD_N), F32)
        dxx = jnp.zeros((ll, SSD_N), F32)
        for g in range(SSD_G):
            gs = slice(g * SSD_GW, (g + 1) * SSD_GW)
            st = sts_ref[0, g]
            dst = dst_scr[g]
            bm_b, cm_b = xbc_ref[:, _b_cols(g)], xbc_ref[:, _c_cols(g)]
            d_x = lp_ref[1:2, gs]
            f = _ssd_group(g, hd, xbc_ref[:, gs], bm_b, cm_b, d_x, st)
            xs, xd, gcb = f["xs"], f["xd"], f["gcb"]
            e_x, w_x, el_x, dt_x, ri, ci = f["e_x"], f["w_x"], f["el_x"], f["dt_x"], f["ri"], f["ci"]
            stb, dstb = st.astype(BF16), dst.astype(BF16)
            zf = z_ref[:, gs].astype(F32)
            sg = _sigmoid(zf)
            sz = zf * sg
            yg = f["y"] * sz
            rstd = lax.rsqrt(_rowmean(yg * yg) + EPS)
            n = yg * rstd
            dyn_v = dy_ref[:, gs].astype(F32)
            dn = dyn_v * lp_ref[0:1, gs]
            dyg = rstd * (dn - n * _rowmean(dn * n))
            dy = dyg * sz
            dz_ref[:, gs] = (dyg * f["y"] * (sg * (1.0 + zf * (1.0 - sg)))).astype(BF16)
            dyb = dy.astype(BF16)
            r_ = _dot(bm_b, dstb)
            dxd = w_x * r_
            dqb = (dy * e_x).astype(BF16)
            dcm = _dot_nt(dqb, stb)
            dst_scr[g] = dst * el_x + _dot(cm_b.astype(F32).T.astype(BF16), dqb)
            dbm = _dot_nt((xd * w_x).astype(BF16), dstb)
            gcb_t = _dot_nt(bm_b, cm_b)
            xdb = xd.astype(BF16)
            dgm = jnp.zeros((ll, ll), F32)
            dgm_t = jnp.zeros((ll, ll), F32)
            for h in range(4):
                k = 4 * g + h
                hm = _head_mask(h)
                dk = f["dks"][h]
                dk_t = jnp.exp(jnp.where(ci >= ri, cs_t[k:k + 1, :] - cs[:, k:k + 1], -1e30))
                dxd = dxd + jnp.where(hm, _dot((gcb_t * dk_t).astype(BF16), dyb), 0.0)
                dyh = jnp.where(hm, dy, 0.0).astype(BF16)
                dm = _dot_nt(dyh, xdb) * dk
                dm_t = _dot_nt(xdb, dyh) * dk_t
                dgm = dgm + dm
                dgm_t = dgm_t + dm_t
                dcs = dcs + jnp.where(lane == k, jnp.sum(dm * gcb, axis=1, keepdims=True)
                                      - jnp.sum(dm_t * gcb_t, axis=1, keepdims=True), 0.0)
            dxbc_ref[:, _c_cols(g)] = (dcm + _dot(dgm.astype(BF16), bm_b)).astype(BF16)
            dxbc_ref[:, _b_cols(g)] = (dbm + _dot(dgm_t.astype(BF16), cm_b)).astype(BF16)
            v = _reduce4(r_ * xd * w_x, g)
            dcs = dcs + _reduce4(dy * f["yoff"], g) - v
            last = last + _colsum(v) + _reduce4(_colsum(dst * st) * el_x, g)
            dxx = dxx + _reduce4(dxd * xs, g)
            dxbc_ref[:, gs] = (d_x * dy + dxd * dt_x).astype(BF16)
            lpg_ref[0:1, gs] += _colsum(dyn_v * n)
            lpg_ref[1:2, gs] += _colsum(dy * xs)
        rowi = lax.broadcasted_iota(jnp.int32, (ll, 1), 0)
        da = _dot_hi(triu_ref[...], dcs + jnp.where(rowi == ll - 1, last, 0.0))
        ddt = (dxx + da * hp[1:2, :]) * _sigmoid(hd["xdt"])
        ddt_ref[...] = ddt
        hpg_ref[...] += jnp.concatenate([_colsum(ddt), _colsum(da * hd["dt"]), jnp.zeros((6, SSD_N), F32)], axis=0)

    rc = lambda c: nc - 1 - c
    return pl.pallas_call(
        body, name="ssd_bwd", grid=(nb, nc),
        in_specs=_ssd_specs(nc, rc) + [
            pl.BlockSpec((1, SSD_G, SSD_N, SSD_GW), lambda b, c: (b * nc + rc(c), 0, 0, 0)),
            pl.BlockSpec((SSD_L, SSD_INNER), lambda b, c: (b * nc + rc(c), 0)),
            VMEM_FULL, VMEM_FULL, VMEM_FULL, VMEM_FULL],
        out_specs=[pl.BlockSpec((SSD_L, 2 * SSD_INNER), lambda b, c: (b * nc + rc(c), 0)),
                   pl.BlockSpec((SSD_L, SSD_INNER), lambda b, c: (b * nc + rc(c), 0)),
                   pl.BlockSpec((SSD_L, SSD_N), lambda b, c: (b * nc + rc(c), 0)),
                   pl.BlockSpec((8, SSD_N), lambda b, c: (0, 0)),
                   pl.BlockSpec((8, SSD_INNER), lambda b, c: (0, 0))],
        out_shape=[jax.ShapeDtypeStruct((t, 2 * SSD_INNER), BF16), jax.ShapeDtypeStruct((t, SSD_INNER), BF16),
                   jax.ShapeDtypeStruct((t, SSD_N), F32), jax.ShapeDtypeStruct((8, SSD_N), F32),
                   jax.ShapeDtypeStruct((8, SSD_INNER), F32)],
        scratch_shapes=[pltpu.VMEM((SSD_G, SSD_N, SSD_GW), F32)],
        compiler_params=_cp("arbitrary", "arbitrary"),
    )(xbc, proj, dtraw, states, dyn, hp, lp, tri, triu)


def ada_fwd(c_all, w_cols, b_cols):
    def body(c_ref, w_ref, b_ref, o_ref):
        cv = c_ref[...]
        o_ref[...] = _dot_hi(cv * _sigmoid(cv), w_ref[...]) + b_ref[...]

    return pl.pallas_call(body, name="ada_fwd", out_shape=jax.ShapeDtypeStruct((c_all.shape[0], w_cols.shape[1]), F32),
                          compiler_params=pltpu.CompilerParams(vmem_limit_bytes=VMEM_LIMIT))(c_all, w_cols, b_cols)


def ada_bwd(c_all, dmod_cols, dmod_all):
    def body(c_ref, dc_ref, da_ref, gw_ref, gb_ref):
        cv = c_ref[...]
        gw_ref[...] = lax.dot_general(cv * _sigmoid(cv), dc_ref[...], (((0,), (0,)), ((), ())),
                                      precision=lax.Precision.HIGHEST, preferred_element_type=F32)
        gb_ref[...] = _colsum(da_ref[...])

    return pl.pallas_call(
        body, name="ada_bwd",
        out_shape=[jax.ShapeDtypeStruct((c_all.shape[1], dmod_cols.shape[1]), F32),
                   jax.ShapeDtypeStruct((1, dmod_all.shape[1]), F32)],
        compiler_params=pltpu.CompilerParams(vmem_limit_bytes=VMEM_LIMIT))(c_all, dmod_cols, dmod_all)


def adamw(parts, w, m, v, name):
    n, r, c = parts.shape
    tr = r if r <= 256 else 128

    def body(p_ref, w_ref, m_ref, v_ref, g_ref, d_ref, nm_ref, nv_ref):
        g = p_ref[0].astype(F32)
        for s in range(1, n):
            g = g + p_ref[s].astype(F32)
        m2 = ADAM_B1 * m_ref[...] + (1.0 - ADAM_B1) * g
        v2 = ADAM_B2 * v_ref[...] + (1.0 - ADAM_B2) * (g * g)
        m_hat = m2 / (1.0 - ADAM_B1 ** ADAM_STEP)
        v_hat = v2 / (1.0 - ADAM_B2 ** ADAM_STEP)
        g_ref[...] = g
        d_ref[...] = -ADAM_LR * (m_hat / (jnp.sqrt(v_hat) + ADAM_EPS) + ADAM_WD * w_ref[...])
        nm_ref[...] = m2
        nv_ref[...] = v2

    blk = pl.BlockSpec((tr, c), lambda i: (i, 0))
    return pl.pallas_call(
        body, name=name, grid=(r // tr,),
        in_specs=[pl.BlockSpec((n, tr, c), lambda i: (0, i, 0)), blk, blk, blk], out_specs=[blk] * 4,
        out_shape=[jax.ShapeDtypeStruct((r, c), F32)] * 4,
        compiler_params=_cp("parallel"),
    )(parts, w, m, v)


def _dev_index(px, py, pc):
    return 4 * px + 2 * py + pc


class _Exchange:
    def __init__(self, arrs):
        self.arrs = list(arrs)
        self.na = len(self.arrs)
        self.scratch = [pltpu.SemaphoreType.DMA((7 * self.na,)), pltpu.SemaphoreType.DMA((7 * self.na,)),
                        pltpu.SemaphoreType.DMA((self.na,))]


class Gather(_Exchange):
    def __init__(self, arrs):
        super().__init__(arrs)
        self.out_shape = [jax.ShapeDtypeStruct((NDEV,) + a.shape, a.dtype) for a in self.arrs]

    def _plan(self, ins, outs, sems):
        na = self.na
        send_sems, recv_sems, local_sems = sems
        x, y, c = lax.axis_index("x"), lax.axis_index("y"), lax.axis_index("c")
        me, sibling = (x, y, c), (x, y, 1 - c)
        chips = [(1 - x, y), (x, 1 - y), (1 - x, 1 - y)]

        def copy(a, k, block, to, src=None):
            dst = outs[a].at[_dev_index(*block)]
            return pltpu.make_async_remote_copy(
                src_ref=dst if src is None else src, dst_ref=dst, send_sem=send_sems.at[a * 7 + k],
                recv_sem=recv_sems.at[a * 7 + k], device_id=to, device_id_type=MESH)

        mine = [pltpu.make_async_copy(ins[a], outs[a].at[_dev_index(*me)], local_sems.at[a]) for a in range(na)]
        first = []
        for a in range(na):
            first.append(copy(a, 0, me, sibling, src=ins[a]))
            first += [copy(a, 1 + j, me, (*chip, c), src=ins[a]) for j, chip in enumerate(chips)]
        return copy, mine, first, me, sibling, chips, c

    def start(self, ins, outs, sems):
        _, mine, first, *_ = self._plan(ins, outs, sems)
        for cp in mine + first:
            cp.start()

    def finish(self, ins, outs, sems):
        copy, mine, first, me, sibling, chips, c = self._plan(ins, outs, sems)
        passed = []
        for j, chip in enumerate(chips):
            for a in range(self.na):
                copy(a, 1 + j, (*chip, c), me).wait_recv()
                cp = copy(a, 4 + j, (*chip, c), sibling)
                cp.start()
                passed.append(cp)
        for a in range(self.na):
            copy(a, 0, sibling, me).wait_recv()
            for j, chip in enumerate(chips):
                copy(a, 4 + j, (*chip, 1 - c), me).wait_recv()
        for cp in first + passed:
            cp.wait_send()
        for cp in mine:
            cp.wait()


class Scatter(_Exchange):
    def __init__(self, arrs):
        super().__init__(arrs)
        self.out_shape = [jax.ShapeDtypeStruct(a.shape, a.dtype) for a in self.arrs]

    def _plan(self, ins, outs, sems, arrivals):
        send_sems, recv_sems, local_sems = sems
        x, y, c = lax.axis_index("x"), lax.axis_index("y"), lax.axis_index("c")
        me = _dev_index(x, y, c)
        masks = [(mx, my, mc) for mx in (0, 1) for my in (0, 1) for mc in (0, 1)][1:]
        flip = lambda v, bit: 1 - v if bit else v
        mine = [pltpu.make_async_copy(ins[a].at[me], outs[a].at[me], local_sems.at[a]) for a in range(self.na)]
        sends, recvs = [], []
        for k, (mx, my, mc) in enumerate(masks):
            peer = (flip(x, mx), flip(y, my), flip(c, mc))
            pidx = _dev_index(*peer)
            for a in range(self.na):
                on = dict(send_sem=send_sems.at[a * 7 + k], recv_sem=recv_sems.at[a * 7 + k], device_id=peer,
                          device_id_type=MESH)
                sends.append(pltpu.make_async_remote_copy(src_ref=ins[a].at[pidx], dst_ref=outs[a].at[me], **on))
                if arrivals:
                    recvs.append(pltpu.make_async_remote_copy(src_ref=ins[a].at[pidx], dst_ref=outs[a].at[pidx], **on))
        return mine, sends, recvs

    def start(self, ins, outs, sems):
        mine, sends, _ = self._plan(ins, outs, sems, arrivals=False)
        for cp in mine + sends:
            cp.start()

    def finish(self, ins, outs, sems):
        mine, sends, recvs = self._plan(ins, outs, sems, arrivals=True)
        for cp in recvs:
            cp.wait_recv()
        for cp in sends:
            cp.wait_send()
        for cp in mine:
            cp.wait()


def exchange_call(ex, name):
    na = ex.na

    def body(*refs):
        ins, outs, sems = refs[:na], refs[na:2 * na], refs[2 * na:]
        ex.start(ins, outs, sems)
        ex.finish(ins, outs, sems)

    return pl.pallas_call(body, name=name, in_specs=[ANY] * na, out_specs=[ANY] * na, out_shape=ex.out_shape,
                          scratch_shapes=ex.scratch)(*ex.arrs)


def all_gather(arrs, name):
    return exchange_call(Gather(arrs), name)


def _call(body, *, name, grid, in_specs, out_specs, out_shape, scratch_shapes=(), sem, args, side=None):
    if side is None:
        outs = pl.pallas_call(body, name=name, grid=grid, in_specs=list(in_specs), out_specs=list(out_specs),
                              out_shape=list(out_shape), scratch_shapes=list(scratch_shapes),
                              compiler_params=_cp(*sem))(*args)
        return outs, []
    ni, no, ns, na = len(in_specs), len(out_specs), len(scratch_shapes), side.na

    def wrapped(*refs):
        ins, s_in = refs[:ni], refs[ni:ni + na]
        outs, s_out = refs[ni + na:ni + na + no], refs[ni + na + no:ni + 2 * na + no]
        scr, sems = refs[ni + 2 * na + no:ni + 2 * na + no + ns], refs[ni + 2 * na + no + ns:]
        pids = [pl.program_id(i) for i in range(len(grid))]
        first = functools.reduce(lambda p, q: p & q, [p == 0 for p in pids])
        last = functools.reduce(lambda p, q: p & q, [p == g - 1 for p, g in zip(pids, grid)])

        @pl.when(first)
        def _():
            side.start(s_in, s_out, sems)

        body(*ins, *outs, *scr)

        @pl.when(last)
        def _():
            side.finish(s_in, s_out, sems)

    outs = pl.pallas_call(
        wrapped, name=name, grid=grid, in_specs=list(in_specs) + [ANY] * na, out_specs=list(out_specs) + [ANY] * na,
        out_shape=list(out_shape) + side.out_shape, scratch_shapes=list(scratch_shapes) + side.scratch,
        compiler_params=_cp(*["arbitrary"] * len(grid)))(*args, *side.arrs)
    return outs[:no], outs[no:]


WEIGHTS = ('w_ada', 'b_ada', 'pre_norm1', 'post_norm1', 'w_in', 'b_gate', 'lru_conv_w', 'lru_conv_b', 'lru_wa',
           'lru_ba', 'lru_wx', 'lru_bx', 'lru_lambda', 'w_pa', 'ssd_conv_w', 'ssd_conv_b', 'ssd_dt_bias', 'ssd_a_log',
           'ssd_d', 'ssd_norm_w', 'w_pb', 'w_out', 'pre_norm2', 'post_norm2', 'w_ff1', 'w_ff2')
BIG = ('w_in', 'w_pa', 'w_pb', 'w_out', 'w_ff1', 'w_ff2')
REPL = ('pre_norm1', 'post_norm1', 'b_gate', 'lru_conv_b', 'lru_wa', 'lru_ba', 'lru_wx', 'lru_bx', 'lru_lambda',
        'ssd_conv_b', 'ssd_dt_bias', 'ssd_a_log', 'ssd_d', 'ssd_norm_w', 'pre_norm2', 'post_norm2')
LANES = 1024


def _rows(n):
    return -(-n // LANES)


def _pack(vals, total_rows):
    parts = []
    for v in vals:
        f = v.reshape(-1).astype(F32)
        parts.append(jnp.pad(f, (0, _rows(f.shape[0]) * LANES - f.shape[0])))
    flat = jnp.concatenate(parts)
    return jnp.pad(flat.reshape(-1, LANES), ((0, total_rows - flat.shape[0] // LANES), (0, 0)))


def _unpack(slab, shapes):
    out, r = [], 0
    for s in shapes:
        n = int(np.prod(s))
        out.append(slab[r:r + _rows(n)].reshape(-1)[:n].reshape(s))
        r += _rows(n)
    return out


def _block_diag4(w):
    w4 = w.reshape(4, 4, 64, 64)
    eye = jnp.eye(4, dtype=w.dtype)
    return (w4[:, :, :, None, :] * eye[None, :, None, :, None]).reshape(4, LRU_BLOCK, LRU_BLOCK)


def _diag_blocks4(m):
    m5 = m.reshape(4, 4, 64, 4, 64)
    return jnp.stack([m5[:, a, :, a, :] for a in range(4)], axis=1).reshape(LRU_HEADS, 64, 64)


def kernel(x, c, w_ada, b_ada, pre_norm1, post_norm1, w_in, b_gate, lru_conv_w, lru_conv_b, lru_wa, lru_ba, lru_wx, lru_bx, lru_lambda, w_pa, ssd_conv_w, ssd_conv_b, ssd_dt_bias, ssd_a_log, ssd_d, ssd_norm_w, w_pb, w_out, pre_norm2, post_norm2, w_ff1, w_ff2, loss_target, m_w_ada, m_b_ada, m_pre_norm1, m_post_norm1, m_w_in, m_b_gate, m_lru_conv_w, m_lru_conv_b, m_lru_wa, m_lru_ba, m_lru_wx, m_lru_bx, m_lru_lambda, m_w_pa, m_ssd_conv_w, m_ssd_conv_b, m_ssd_dt_bias, m_ssd_a_log, m_ssd_d, m_ssd_norm_w, m_w_pb, m_w_out, m_pre_norm2, m_post_norm2, m_w_ff1, m_w_ff2, v_w_ada, v_b_ada, v_pre_norm1, v_post_norm1, v_w_in, v_b_gate, v_lru_conv_w, v_lru_conv_b, v_lru_wa, v_lru_ba, v_lru_wx, v_lru_bx, v_lru_lambda, v_w_pa, v_ssd_conv_w, v_ssd_conv_b, v_ssd_dt_bias, v_ssd_a_log, v_ssd_d, v_ssd_norm_w, v_w_pb, v_w_out, v_pre_norm2, v_post_norm2, v_w_ff1, v_w_ff2):
    given = dict(locals())
    w = {k: given[k] for k in WEIGHTS}
    mom = {k: given["m_" + k] for k in WEIGHTS}
    var = {k: given["v_" + k] for k in WEIGHTS}
    nb, seq, _ = x.shape
    assert nb == 2 and seq % 512 == 0, (nb, seq)
    t = nb * seq
    me = _dev_index(lax.axis_index("x"), lax.axis_index("y"), lax.axis_index("c"))
    x2 = x.reshape(t, D_MODEL)
    tgt2 = loss_target.reshape(t, D_MODEL)
    ada_cols = w_ada.shape[2]

    slab = jnp.zeros((16, LANES), F32)
    slab = slab.at[0:nb].set(c)
    slab = slab.at[2:6, 0:lru_conv_w.shape[2]].set(lru_conv_w[0])
    slab = slab.at[6:10, 0:ssd_conv_w.shape[2]].set(ssd_conv_w[0])
    (g1,) = all_gather([slab], "gather_cond")
    c_all = g1[:, 0:nb].reshape(NDEV * nb, D_MODEL)
    lru_cw = g1[:, 2:6, 0:lru_conv_w.shape[2]].transpose(1, 0, 2).reshape(4, D_MODEL)
    ssd_cw = g1[:, 6:10, 0:ssd_conv_w.shape[2]].transpose(1, 0, 2).reshape(4, 2 * SSD_INNER)
    b_cols = lax.dynamic_slice(b_ada, (0, me * ada_cols), (1, ada_cols))
    mod_cols = ada_fwd(c_all, w_ada[0], b_cols)
    (g2,) = all_gather([mod_cols], "gather_mod")
    mod_all = g2.transpose(1, 0, 2).reshape(NDEV * nb, N_MOD * D_MODEL)
    mod_mine = lax.dynamic_slice(mod_all, (me * nb, 0), (nb, N_MOD * D_MODEL)).reshape(nb, N_MOD, D_MODEL)
    mod8 = jnp.pad(mod_mine, ((0, 0), (0, 8 - N_MOD), (0, 0)))

    (gw_in,) = all_gather([w_in[0].astype(BF16)], "gather_w_in")
    w_nat = gw_in.transpose(1, 0, 2).reshape(D_MODEL, IN_DIM)
    w_main = jnp.concatenate([w_nat[:, :DT_COL0], w_nat[:, DT_COL0 + SSD_HEADS:]], axis=1)
    w_dt = jnp.pad(w_nat[:, DT_COL0:DT_COL0 + SSD_HEADS], ((0, 0), (0, 128 - SSD_HEADS)))

    wa_bd = _block_diag4(lru_wa[0]).astype(BF16)
    wx_bd = _block_diag4(lru_wx[0]).astype(BF16)
    lam = lru_lambda[0]
    vec = _pack([lru_ba, lru_bx, jax.nn.softplus(-lam)], 8)
    tri, triu = ssd_consts()
    hp, lp = ssd_params(ssd_dt_bias[0], ssd_a_log[0], ssd_d[0], ssd_norm_w[0])

    rest = Gather([w[k][0].astype(BF16) for k in BIG[1:]])
    (proj, h1, dtraw), gw = in_proj_fwd(x2, mod8, pre_norm1, w_main, w_dt, seq, side=rest)
    w_pa_f = gw[0].reshape(D_MODEL, D_MODEL)
    w_pb_f = gw[1].reshape(SSD_INNER, D_MODEL)
    w_out_f = gw[2].reshape(D_MODEL, D_MODEL)
    w_ff1_f = gw[3].transpose(1, 0, 2).reshape(D_MODEL, D_FF)
    w_ff2_f = gw[4].reshape(D_FF, D_MODEL)
    xa = conv_fwd(proj, C_LRU_X, D_MODEL, lru_cw, lru_conv_b, nb, seq, False, "conv_lru_fwd")
    xbc = conv_fwd(proj, C_XBC, 2 * SSD_INNER, ssd_cw, ssd_conv_b, nb, seq, True, "conv_ssd_fwd")
    ya_in, hst = lru_fwd(xa, proj, wa_bd, wx_bd, vec, nb, seq)
    yb_in, states = ssd_fwd(xbc, proj, dtraw, hp, lp, tri, nb, seq)
    yab, out1, x1 = merge_fwd(ya_in, yb_in, proj, x2, mod8, b_gate, post_norm1, w_pa_f, w_pb_f, w_out_f, seq)

    dx1, h2, da1, act, dy2, loss8, vacc_mlp, dmod_mlp = mlp_fwd_bwd(
        x1, tgt2, mod8, pre_norm2, post_norm2, w_ff1_f, w_ff2_f, nb, seq)
    wg = dict(out_dtype=BF16, ta=True, tm=1024, tn=1024, tk=1024)
    dw_ff1 = matmul(h2, da1, name="wgrad_ff1", **wg)
    dw_ff2 = matmul(act, dy2, name="wgrad_ff2", **wg)
    dya_in, dyb_in, dgates, dyab, dout1, merged, vacc_mg, dmod_mg = merge_bwd(
        dx1, out1, yab, proj, mod8, b_gate, post_norm1, w_pa_f, w_pb_f, w_out_f, nb, seq)
    dw_out = matmul(merged, dout1, name="wgrad_out", **wg)
    dw_pa = matmul(ya_in, dyab, name="wgrad_pa", n=D_MODEL, b_off=0, **wg)
    dw_pb = matmul(yb_in, dyab, name="wgrad_pb", n=D_MODEL, b_off=1, **wg)
    by_rows = lambda g: g.reshape(NDEV, g.shape[0] // NDEV, g.shape[1])
    by_cols = lambda g: g.reshape(g.shape[0], NDEV, g.shape[1] // NDEV).transpose(1, 0, 2)
    (dxa, dlg, dwa_bd, dwx_bd, dvec), parts_ff = lru_bwd(
        dya_in, xa, proj, hst, wa_bd, wx_bd, vec, nb, seq, side=Scatter([by_cols(dw_ff1), by_rows(dw_ff2)]))
    dxbc, dz, ddt, hpg, lpg = ssd_bwd(xbc, proj, dtraw, hp, lp, tri, triu, states, dyb_in, nb, seq)
    (dlx, acc_l), _ = conv_bwd(proj, C_LRU_X, D_MODEL, lru_cw, lru_conv_b, dxa, nb, seq, False, "conv_lru_bwd")
    (dxr, acc_s), parts_mg = conv_bwd(proj, C_XBC, 2 * SSD_INNER, ssd_cw, ssd_conv_b, dxbc, nb, seq, True,
                                      "conv_ssd_bwd", side=Scatter([by_rows(dw_pa), by_rows(dw_pb), by_rows(dw_out)]))
    pieces = (dlx, dlg, dz, dxr, dgates)
    ddt_b = ddt.astype(BF16)
    dw_main, dw_dt = in_proj_wgrad(h1, pieces, ddt_b)
    dw_nat = jnp.concatenate([dw_main[:, :DT_COL0], dw_dt[:, :SSD_HEADS], dw_main[:, DT_COL0:]], axis=1)
    (grad_x, vacc_in, dmod_in), parts_in = in_proj_bwd(pieces, ddt_b, dx1, x2, mod8, pre_norm1, w_main, w_dt, nb, seq,
                                                       side=Scatter([by_cols(dw_nat)]))
    parts = dict(zip(BIG, (parts_in[0], *parts_mg, *parts_ff)))

    dmod = (dmod_in + dmod_mg + dmod_mlp)[:, :N_MOD].reshape(nb, N_MOD * D_MODEL)
    (g3,) = all_gather([jnp.pad(dmod, ((0, 8 - nb), (0, 0)))], "gather_dmod")
    dmod_all = g3[:, :nb].reshape(NDEV * nb, N_MOD * D_MODEL)
    dmod_cols = lax.dynamic_slice(dmod_all, (0, me * ada_cols), (NDEV * nb, ada_cols))
    g_w_ada, g_b_ada = ada_bwd(c_all, dmod_cols, dmod_all)

    res = {}
    for k in BIG:
        res[k] = adamw(parts[k], w[k][0], mom[k][0], var[k][0], "adamw_" + k)
    res['w_ada'] = adamw(g_w_ada[None], w_ada[0], m_w_ada[0], v_w_ada[0], "adamw_w_ada")

    a_neg = -jnp.exp(ssd_a_log[0])
    small = {
        'pre_norm1': vacc_in[0], 'post_norm1': vacc_mg[1, :D_MODEL], 'b_gate': vacc_mg[0],
        'lru_conv_b': acc_l[4], 'lru_wa': _diag_blocks4(dwa_bd), 'lru_ba': dvec[0], 'lru_wx': _diag_blocks4(dwx_bd),
        'lru_bx': dvec[1], 'lru_lambda': dvec[2] * (-jax.nn.sigmoid(-lam)),
        'ssd_conv_b': acc_s[4],
        'ssd_dt_bias': hpg[0, :SSD_HEADS], 'ssd_a_log': hpg[1, :SSD_HEADS] * a_neg,
        'ssd_d': lpg[1].reshape(SSD_HEADS, SSD_P).sum(axis=-1), 'ssd_norm_w': lpg[0],
        'pre_norm2': vacc_mlp[0], 'post_norm2': vacc_mlp[1],
    }
    conv_full = [acc_l[:4], acc_s[:4]]
    nra = sum(_rows(int(np.prod(w[k].shape))) for k in REPL)
    nrc = sum(_rows(int(np.prod(v.shape))) for v in conv_full)
    rows_a = -(-(nra + nrc) // 8) * 8
    gslab = _pack([small[k] for k in REPL] + conv_full, rows_a)
    (g4,) = all_gather([gslab], "gather_small_grads")
    res_a = adamw(g4, _pack([w[k] for k in REPL], rows_a), _pack([mom[k] for k in REPL], rows_a),
                  _pack([var[k] for k in REPL], rows_a), "adamw_small")
    for j, slab_j in enumerate(res_a):
        for k, val in zip(REPL, _unpack(slab_j, [w[k].shape for k in REPL])):
            res.setdefault(k, [None] * 4)[j] = val
    g_lru_cw, g_ssd_cw = _unpack(res_a[0][nra:], [v.shape for v in conv_full])
    lcw, scw = lru_conv_w.shape[2], ssd_conv_w.shape[2]
    sharded = {'b_ada': g_b_ada, 'lru_conv_w': lax.dynamic_slice(g_lru_cw, (0, me * lcw), (4, lcw)),
               'ssd_conv_w': lax.dynamic_slice(g_ssd_cw, (0, me * scw), (4, scw))}
    names_b = tuple(sharded)
    res_b = adamw(_pack([sharded[k] for k in names_b], 16)[None], _pack([w[k] for k in names_b], 16),
                  _pack([mom[k] for k in names_b], 16), _pack([var[k] for k in names_b], 16), "adamw_small_sharded")
    for j, slab_j in enumerate(res_b):
        for k, val in zip(names_b, _unpack(slab_j, [w[k].shape for k in names_b])):
            res.setdefault(k, [None] * 4)[j] = val

    loss = lax.psum(loss8[0, 0], ("x", "y", "c"))
    outs = [[res[k][j].reshape(w[k].shape) for k in WEIGHTS] for j in range(4)]
    return (loss, grad_x.reshape(x.shape), *outs[0], *outs[1], *outs[2], *outs[3])
```

```python
import functools

import numpy as np
import jax
import jax.numpy as jnp
from jax import lax
from jax.experimental import pallas as pl
from jax.experimental.pallas import tpu as pltpu

F32 = jnp.float32
BF16 = jnp.bfloat16

D_MODEL = 1024
LRU_HEADS = 16
LRU_BLOCK = 256
LRU_C = 8.0
SSD_INNER = 2048
SSD_HEADS = 32
SSD_P = 64
SSD_G = 8
SSD_N = 128
SSD_L = 128
SSD_GW = SSD_INNER // SSD_G
D_FF = 4096
N_MOD = 6
EPS = 1e-6
NDEV = 8

C_LRU_X, C_LRU_G, C_Z, C_XBC, C_GATES, PROJ_MAIN = 0, 1024, 2048, 4096, 8192, 10240
IN_DIM = 10272
DT_COL0 = 8192
HALO = 16

ADAM_LR, ADAM_B1, ADAM_B2, ADAM_EPS, ADAM_WD, ADAM_STEP = 0.001, 0.9, 0.999, 1e-08, 0.01, 10

VMEM_LIMIT = 60 * 1024 * 1024
MESH = pl.DeviceIdType.MESH
ANY = pl.BlockSpec(memory_space=pl.ANY)
VMEM_FULL = pl.BlockSpec(memory_space=pltpu.VMEM)


def _cp(*sem):
    return pltpu.CompilerParams(dimension_semantics=sem, vmem_limit_bytes=VMEM_LIMIT)


def _dot(a, b):
    return jnp.dot(a, b, preferred_element_type=F32)


def _dot_nt(a, b):
    return lax.dot_general(a, b, (((1,), (1,)), ((), ())), preferred_element_type=F32)


def _dot_tn(a, b):
    return lax.dot_general(a, b, (((0,), (0,)), ((), ())), preferred_element_type=F32)


def _dot_hi(a, b):
    return jnp.dot(a, b, precision=lax.Precision.HIGHEST, preferred_element_type=F32)


def _sigmoid(x):
    return 1.0 / (1.0 + jnp.exp(-x))


def _gelu_and_grad(x):
    k0, k1 = 0.7978845608028654, 0.044715
    t = jnp.tanh(k0 * (x + k1 * x * x * x))
    g = 0.5 * x * (1.0 + t)
    dg = 0.5 * (1.0 + t) + 0.5 * x * (1.0 - t * t) * k0 * (1.0 + 3.0 * k1 * x * x)
    return g, dg


def _neg_expm1(y):
    p = 1.0 + y * (1.0 / 7.0)
    p = 1.0 + y * (1.0 / 6.0) * p
    p = 1.0 + y * (1.0 / 5.0) * p
    p = 1.0 + y * (1.0 / 4.0) * p
    p = 1.0 + y * (1.0 / 3.0) * p
    p = 1.0 + y * 0.5 * p
    return jnp.where(y > -0.3, -y * p, 1.0 - jnp.exp(y))


def _colsum(v):
    return jnp.sum(v, axis=0, keepdims=True)


def _rowmean(v):
    return jnp.mean(v, axis=-1, keepdims=True)


def matmul(a, b, *, ta=False, tb=False, out_dtype=F32, tm, tn, tk, name, n=None, b_off=0):
    m = a.shape[1] if ta else a.shape[0]
    kdim = a.shape[0] if ta else a.shape[1]
    n = n or (b.shape[0] if tb else b.shape[1])
    tm, tn, tk = min(tm, m), min(tn, n), min(tk, kdim)
    nk = kdim // tk
    dn = (((0 if ta else 1,), (1 if tb else 0,)), ((), ()))

    def body(a_ref, b_ref, o_ref, acc_ref):
        k = pl.program_id(2)
        p = lax.dot_general(a_ref[...], b_ref[...], dn, preferred_element_type=F32)
        if nk == 1:
            o_ref[...] = p.astype(out_dtype)
        else:
            @pl.when(k == 0)
            def _():
                acc_ref[...] = p

            @pl.when(k > 0)
            def _():
                acc_ref[...] += p

            @pl.when(k == nk - 1)
            def _():
                o_ref[...] = acc_ref[...].astype(out_dtype)

    a_spec = pl.BlockSpec((tk, tm), lambda i, j, k: (k, i)) if ta else pl.BlockSpec((tm, tk), lambda i, j, k: (i, k))
    b_spec = (pl.BlockSpec((tn, tk), lambda i, j, k: (j, k)) if tb
              else pl.BlockSpec((tk, tn), lambda i, j, k: (k, j + b_off)))
    return pl.pallas_call(
        body, name=name, grid=(m // tm, n // tn, nk),
        in_specs=[a_spec, b_spec], out_specs=pl.BlockSpec((tm, tn), lambda i, j, k: (i, j)),
        out_shape=jax.ShapeDtypeStruct((m, n), out_dtype),
        scratch_shapes=[pltpu.VMEM((tm, tn), F32)],
        compiler_params=_cp("parallel", "parallel", "arbitrary"),
    )(a, b)


def in_proj_fwd(x2, mod8, pre1, w_main, w_dt, seq, side=None):
    t = x2.shape[0]
    tm = min(1024, seq)
    tn = 2048
    per_seq = seq // tm

    def body(x_ref, mod_ref, pre_ref, w_ref, wdt_ref, proj_ref, h_ref, dt_ref, h_scr):
        @pl.when(pl.program_id(1) == 0)
        def _():
            xv = x_ref[...]
            y = xv * lax.rsqrt(_rowmean(xv * xv) + EPS) * pre_ref[...]
            m = mod_ref[0]
            h = (y * (1.0 + m[1:2, :]) + m[0:1, :]).astype(BF16)
            h_scr[...] = h
            h_ref[...] = h
            dt_ref[...] = _dot(h, wdt_ref[...])

        proj_ref[...] = _dot(h_scr[...], w_ref[...]).astype(BF16)

    return _call(
        body, name="in_proj_fwd", grid=(t // tm, PROJ_MAIN // tn), side=side, sem=("parallel", "arbitrary"),
        args=(x2, mod8, pre1, w_main, w_dt),
        in_specs=[pl.BlockSpec((tm, D_MODEL), lambda i, j: (i, 0)),
                  pl.BlockSpec((1, 8, D_MODEL), lambda i, j: (i // per_seq, 0, 0)),
                  pl.BlockSpec((1, D_MODEL), lambda i, j: (0, 0)),
                  pl.BlockSpec((D_MODEL, tn), lambda i, j: (0, j)),
                  pl.BlockSpec((D_MODEL, 128), lambda i, j: (0, 0))],
        out_specs=[pl.BlockSpec((tm, tn), lambda i, j: (i, j)),
                   pl.BlockSpec((tm, D_MODEL), lambda i, j: (i, 0)),
                   pl.BlockSpec((tm, 128), lambda i, j: (i, 0))],
        out_shape=[jax.ShapeDtypeStruct((t, PROJ_MAIN), BF16), jax.ShapeDtypeStruct((t, D_MODEL), BF16),
                   jax.ShapeDtypeStruct((t, 128), F32)],
        scratch_shapes=[pltpu.VMEM((tm, D_MODEL), BF16)])


def conv_fwd(src, col0, width, w4, bias, nb, seq, act, name):
    t = src.shape[0]
    tt = min(512, seq)
    tc = 512
    ns = seq // tt
    cb0 = col0 // tc

    def body(cur_ref, prev_ref, w_ref, b_ref, o_ref):
        s = pl.program_id(1)
        cur = cur_ref[...].astype(F32)
        prev = jnp.where(s == 0, 0.0, prev_ref[...].astype(F32))
        xx = jnp.concatenate([prev, cur], axis=0)
        w = w_ref[...]
        acc = cur * w[3:4, :] + b_ref[...]
        for d in (1, 2, 3):
            acc = acc + pltpu.roll(xx, d, axis=0)[HALO:, :] * w[3 - d:4 - d, :]
        if act:
            acc = acc * _sigmoid(acc)
        o_ref[...] = acc.astype(BF16)

    return pl.pallas_call(
        body, name=name, grid=(nb, ns, width // tc),
        in_specs=[pl.BlockSpec((tt, tc), lambda b, s, j: (b * ns + s, cb0 + j)),
                  pl.BlockSpec((HALO, tc), lambda b, s, j: (jnp.maximum((b * seq + s * tt) // HALO - 1, 0), cb0 + j)),
                  pl.BlockSpec((4, tc), lambda b, s, j: (0, j)),
                  pl.BlockSpec((1, tc), lambda b, s, j: (0, j))],
        out_specs=pl.BlockSpec((tt, tc), lambda b, s, j: (b * ns + s, j)),
        out_shape=jax.ShapeDtypeStruct((t, width), BF16),
        compiler_params=_cp("parallel", "parallel", "parallel"),
    )(src, src, w4, bias)


def conv_bwd(src, col0, width, w4, bias, dout, nb, seq, act, name, side=None):
    t = src.shape[0]
    tt = min(512, seq)
    tc = 512
    ns = seq // tt
    cb0 = col0 // tc
    nh = t // HALO

    def body(cur_ref, prev_ref, next_ref, w_ref, b_ref, do_ref, don_ref, dx_ref, acc_ref):
        b, s = pl.program_id(1), pl.program_id(2)

        @pl.when((b == 0) & (s == 0))
        def _():
            acc_ref[...] = jnp.zeros_like(acc_ref)

        cur = cur_ref[...].astype(F32)
        prev = jnp.where(s == 0, 0.0, prev_ref[...].astype(F32))
        nxt = next_ref[...].astype(F32)
        xx = jnp.concatenate([prev, cur, nxt], axis=0)
        w = w_ref[...]
        do_ext = jnp.concatenate([do_ref[...].astype(F32),
                                  jnp.where(s == ns - 1, 0.0, don_ref[...].astype(F32))], axis=0)
        ne = tt + HALO
        xs = [xx[HALO:HALO + ne, :]] + [pltpu.roll(xx, d, axis=0)[HALO:HALO + ne, :] for d in (1, 2, 3)]
        if act:
            c = b_ref[...] + xs[0] * w[3:4, :] + xs[1] * w[2:3, :] + xs[2] * w[1:2, :] + xs[3] * w[0:1, :]
            sg = _sigmoid(c)
            dc = do_ext * (sg * (1.0 + c * (1.0 - sg)))
        else:
            dc = do_ext
        dx = dc[:tt, :] * w[3:4, :]
        for d in (1, 2, 3):
            dx = dx + pltpu.roll(dc, ne - d, axis=0)[:tt, :] * w[3 - d:4 - d, :]
        dx_ref[...] = dx.astype(BF16)
        dcc = dc[:tt, :]
        rows = [_colsum(dcc * xs[3 - r][:tt, :]) for r in range(4)] + [_colsum(dcc)]
        acc_ref[...] += jnp.concatenate(rows + [jnp.zeros((3, tc), F32)], axis=0)

    return _call(
        body, name=name, grid=(width // tc, nb, ns), side=side, sem=("parallel", "arbitrary", "arbitrary"),
        args=(src, src, src, w4, bias, dout, dout),
        in_specs=[pl.BlockSpec((tt, tc), lambda j, b, s: (b * ns + s, cb0 + j)),
                  pl.BlockSpec((HALO, tc), lambda j, b, s: (jnp.maximum((b * seq + s * tt) // HALO - 1, 0), cb0 + j)),
                  pl.BlockSpec((HALO, tc), lambda j, b, s: (jnp.minimum((b * seq + (s + 1) * tt) // HALO, nh - 1), cb0 + j)),
                  pl.BlockSpec((4, tc), lambda j, b, s: (0, j)),
                  pl.BlockSpec((1, tc), lambda j, b, s: (0, j)),
                  pl.BlockSpec((tt, tc), lambda j, b, s: (b * ns + s, j)),
                  pl.BlockSpec((HALO, tc), lambda j, b, s: (jnp.minimum((b * seq + (s + 1) * tt) // HALO, nh - 1), j))],
        out_specs=[pl.BlockSpec((tt, tc), lambda j, b, s: (b * ns + s, j)),
                   pl.BlockSpec((8, tc), lambda j, b, s: (0, j))],
        out_shape=[jax.ShapeDtypeStruct((t, width), BF16), jax.ShapeDtypeStruct((8, width), F32)])


def _lru_gates(xa, wa_ref, wx_ref, ba, bx, sp):
    nblk = D_MODEL // LRU_BLOCK
    pr = jnp.concatenate([_dot(xa[:, j * LRU_BLOCK:(j + 1) * LRU_BLOCK], wa_ref[j]) for j in range(nblk)], axis=1) + ba
    pi = jnp.concatenate([_dot(xa[:, j * LRU_BLOCK:(j + 1) * LRU_BLOCK], wx_ref[j]) for j in range(nblk)], axis=1) + bx
    r = _sigmoid(pr)
    i = _sigmoid(pi)
    log_a = (-LRU_C * r) * sp
    return r, i, jnp.exp(log_a), _neg_expm1(2.0 * log_a)


def lru_fwd(xa, proj, wa_bd, wx_bd, vec, nb, seq):
    t = xa.shape[0]
    tc = min(512, seq)
    nk = seq // tc
    gb = C_LRU_G // D_MODEL

    def body(xa_ref, g_ref, wa_ref, wx_ref, vec_ref, ya_ref, h_ref, a_scr, u_scr, hc_scr):
        @pl.when(pl.program_id(1) == 0)
        def _():
            hc_scr[...] = jnp.zeros_like(hc_scr)

        xa_v = xa_ref[...]
        v = vec_ref[...]
        r, i, a, e = _lru_gates(xa_v, wa_ref, wx_ref, v[0:1, :], v[1:2, :], v[2:3, :])
        a_scr[...] = a
        u_scr[...] = jnp.sqrt(e) * (i * xa_v.astype(F32))
        row = lax.broadcasted_iota(jnp.int32, (8, 1), 0)

        def tile(j, h):
            r0 = pl.multiple_of(j * 8, 8)
            av, uv = a_scr[pl.ds(r0, 8), :], u_scr[pl.ds(r0, 8), :]
            for d in (1, 2, 4):
                uv = uv + av * jnp.where(row >= d, pltpu.roll(uv, d, axis=0), 0.0)
                av = av * jnp.where(row >= d, pltpu.roll(av, d, axis=0), 1.0)
            hv = uv + av * h
            h_ref[pl.ds(r0, 8), :] = hv
            return hv[7:8, :]

        hc_scr[...] = lax.fori_loop(0, tc // 8, tile, hc_scr[...], unroll=2)
        gel, _ = _gelu_and_grad(g_ref[...].astype(F32))
        ya_ref[...] = (h_ref[...] * gel).astype(BF16)

    return pl.pallas_call(
        body, name="lru_fwd", grid=(nb, nk),
        in_specs=[pl.BlockSpec((tc, D_MODEL), lambda b, k: (b * nk + k, 0)),
                  pl.BlockSpec((tc, D_MODEL), lambda b, k: (b * nk + k, gb)),
                  VMEM_FULL, VMEM_FULL, VMEM_FULL],
        out_specs=[pl.BlockSpec((tc, D_MODEL), lambda b, k: (b * nk + k, 0)),
                   pl.BlockSpec((tc, D_MODEL), lambda b, k: (b * nk + k, 0))],
        out_shape=[jax.ShapeDtypeStruct((t, D_MODEL), BF16), jax.ShapeDtypeStruct((t, D_MODEL), F32)],
        scratch_shapes=[pltpu.VMEM((tc, D_MODEL), F32), pltpu.VMEM((tc, D_MODEL), F32), pltpu.VMEM((1, D_MODEL), F32)],
        compiler_params=_cp("arbitrary", "arbitrary"),
    )(xa, proj, wa_bd, wx_bd, vec)


def lru_bwd(dya, xa, proj, h, wa_bd, wx_bd, vec, nb, seq, side=None):
    t = xa.shape[0]
    tc = min(512, seq)
    nk = seq // tc
    gb = C_LRU_G // D_MODEL
    nblk = D_MODEL // LRU_BLOCK

    def chunk(b, k):
        return b * nk + (nk - 1 - k)

    def body(dya_ref, xa_ref, g_ref, h_ref, hp_ref, wa_ref, wx_ref, vec_ref,
             dxa_ref, dg_ref, dwa_ref, dwx_ref, dvec_ref, a_scr, dh_scr, c_scr):
        b, k = pl.program_id(0), pl.program_id(1)

        @pl.when((b == 0) & (k == 0))
        def _():
            dwa_ref[...] = jnp.zeros_like(dwa_ref)
            dwx_ref[...] = jnp.zeros_like(dwx_ref)
            dvec_ref[...] = jnp.zeros_like(dvec_ref)

        @pl.when(k == 0)
        def _():
            c_scr[...] = jnp.zeros_like(c_scr)

        xa_v = xa_ref[...]
        xaf = xa_v.astype(F32)
        v = vec_ref[...]
        sp = v[2:3, :]
        r, i, a, e = _lru_gates(xa_v, wa_ref, wx_ref, v[0:1, :], v[1:2, :], sp)
        gel, dgel = _gelu_and_grad(g_ref[...].astype(F32))
        hv = h_ref[...]
        dyv = dya_ref[...].astype(F32)
        dg_ref[...] = (dyv * hv * dgel).astype(BF16)
        a_scr[...] = a
        dh_scr[...] = dyv * gel

        row8 = lax.broadcasted_iota(jnp.int32, (8, 1), 0)

        def tile(j, c):
            r0 = pl.multiple_of((tc // 8 - 1 - j) * 8, 8)
            av, dout = a_scr[pl.ds(r0, 8), :], dh_scr[pl.ds(r0, 8), :]
            zv = av * dout
            for d in (1, 2, 4):
                zv = zv + av * jnp.where(row8 < 8 - d, pltpu.roll(zv, 8 - d, axis=0), 0.0)
                av = av * jnp.where(row8 < 8 - d, pltpu.roll(av, 8 - d, axis=0), 1.0)
            zv = zv + av * c
            dh_scr[pl.ds(r0, 8), :] = dout + jnp.where(row8 < 7, pltpu.roll(zv, 7, axis=0), c)
            return zv[0:1, :]

        c_scr[...] = lax.fori_loop(0, tc // 8, tile, c_scr[...], unroll=2)
        dh = dh_scr[...]
        h_last = jnp.where(k == nk - 1, 0.0, hp_ref[HALO // 2 - 1:HALO // 2, :])
        row = lax.broadcasted_iota(jnp.int32, (tc, 1), 0)
        h_prev = jnp.where(row == 0, h_last, pltpu.roll(hv, 1, axis=0))
        s = jnp.sqrt(e)
        da = dh * h_prev
        ix = i * xaf
        dlog_a = da * a - (dh * ix) * (a * a) * lax.rsqrt(jnp.maximum(e, 1e-30))
        di = dh * s * xaf
        dpr = (dlog_a * (-LRU_C * sp)) * (r * (1.0 - r))
        dpi = di * (i * (1.0 - i))
        dprb, dpib = dpr.astype(BF16), dpi.astype(BF16)
        dxa = dh * s * i
        dxa = dxa + jnp.concatenate(
            [_dot_nt(dprb[:, j * LRU_BLOCK:(j + 1) * LRU_BLOCK], wa_ref[j])
             + _dot_nt(dpib[:, j * LRU_BLOCK:(j + 1) * LRU_BLOCK], wx_ref[j]) for j in range(nblk)], axis=1)
        dxa_ref[...] = dxa.astype(BF16)
        for j in range(nblk):
            sl = slice(j * LRU_BLOCK, (j + 1) * LRU_BLOCK)
            dwa_ref[j] += _dot_tn(xa_v[:, sl], dprb[:, sl])
            dwx_ref[j] += _dot_tn(xa_v[:, sl], dpib[:, sl])
        dvec_ref[...] += jnp.concatenate(
            [_colsum(dpr), _colsum(dpi), _colsum(dlog_a * (-LRU_C * r)), jnp.zeros((5, D_MODEL), F32)], axis=0)

    hh = HALO // 2
    return _call(
        body, name="lru_bwd", grid=(nb, nk), side=side, sem=("arbitrary", "arbitrary"),
        args=(dya, xa, proj, h, h, wa_bd, wx_bd, vec),
        in_specs=[pl.BlockSpec((tc, D_MODEL), lambda b, k: (chunk(b, k), 0)),
                  pl.BlockSpec((tc, D_MODEL), lambda b, k: (chunk(b, k), 0)),
                  pl.BlockSpec((tc, D_MODEL), lambda b, k: (chunk(b, k), gb)),
                  pl.BlockSpec((tc, D_MODEL), lambda b, k: (chunk(b, k), 0)),
                  pl.BlockSpec((hh, D_MODEL), lambda b, k: (jnp.maximum(chunk(b, k) * (tc // hh) - 1, 0), 0)),
                  VMEM_FULL, VMEM_FULL, VMEM_FULL],
        out_specs=[pl.BlockSpec((tc, D_MODEL), lambda b, k: (chunk(b, k), 0)),
                   pl.BlockSpec((tc, D_MODEL), lambda b, k: (chunk(b, k), 0)),
                   pl.BlockSpec((nblk, LRU_BLOCK, LRU_BLOCK), lambda b, k: (0, 0, 0)),
                   pl.BlockSpec((nblk, LRU_BLOCK, LRU_BLOCK), lambda b, k: (0, 0, 0)),
                   pl.BlockSpec((8, D_MODEL), lambda b, k: (0, 0))],
        out_shape=[jax.ShapeDtypeStruct((t, D_MODEL), BF16), jax.ShapeDtypeStruct((t, D_MODEL), BF16),
                   jax.ShapeDtypeStruct((nblk, LRU_BLOCK, LRU_BLOCK), F32),
                   jax.ShapeDtypeStruct((nblk, LRU_BLOCK, LRU_BLOCK), F32),
                   jax.ShapeDtypeStruct((8, D_MODEL), F32)],
        scratch_shapes=[pltpu.VMEM((tc, D_MODEL), F32), pltpu.VMEM((tc, D_MODEL), F32), pltpu.VMEM((1, D_MODEL), F32)])


def merge_fwd(ya_in, yb_in, proj, x2, mod8, bgate, post1, w_pa, w_pb, w_out, seq):
    t = x2.shape[0]
    tm = min(512, seq)
    per_seq = seq // tm
    gcb = C_GATES // SSD_INNER

    def body(ya_ref, yb_ref, gt_ref, x_ref, mod_ref, bg_ref, post_ref, wpa_ref, wpb_ref, wo_ref,
             yab_ref, out1_ref, x1_ref):
        y_a = _dot(ya_ref[...], wpa_ref[...])
        y_b = _dot(yb_ref[...], wpb_ref[...])
        g = _sigmoid(gt_ref[...].astype(F32) + bg_ref[...])
        merged = g[:, :D_MODEL] * y_a + g[:, D_MODEL:] * y_b
        out1 = _dot(merged.astype(BF16), wo_ref[...])
        n = out1 * lax.rsqrt(_rowmean(out1 * out1) + EPS)
        yab_ref[...] = jnp.concatenate([y_a, y_b], axis=1).astype(BF16)
        out1_ref[...] = out1
        x1_ref[...] = x_ref[...] + mod_ref[0][2:3, :] * (n * post_ref[...])

    row = lambda w: pl.BlockSpec((tm, w), lambda i: (i, 0))
    return pl.pallas_call(
        body, name="merge_fwd", grid=(t // tm,),
        in_specs=[row(D_MODEL), row(SSD_INNER), pl.BlockSpec((tm, SSD_INNER), lambda i: (i, gcb)), row(D_MODEL),
                  pl.BlockSpec((1, 8, D_MODEL), lambda i: (i // per_seq, 0, 0)),
                  VMEM_FULL, VMEM_FULL, VMEM_FULL, VMEM_FULL, VMEM_FULL],
        out_specs=[row(SSD_INNER), row(D_MODEL), row(D_MODEL)],
        out_shape=[jax.ShapeDtypeStruct((t, SSD_INNER), BF16), jax.ShapeDtypeStruct((t, D_MODEL), F32),
                   jax.ShapeDtypeStruct((t, D_MODEL), F32)],
        compiler_params=_cp("parallel"),
    )(ya_in, yb_in, proj, x2, mod8, bgate, post1, w_pa, w_pb, w_out)


def merge_bwd(dx1, out1, yab, proj, mod8, bgate, post1, w_pa, w_pb, w_out, nb, seq):
    t = dx1.shape[0]
    tm = min(512, seq)
    per_seq = seq // tm
    gcb = C_GATES // SSD_INNER

    def body(dx1_ref, out1_ref, yab_ref, gt_ref, mod_ref, bg_ref, post_ref, wpa_ref, wpb_ref, wo_ref,
             dya_ref, dyb_ref, dgt_ref, dyab_ref, dout1_ref, mg_ref, vacc_ref, dmod_ref):
        b, s = pl.program_id(0), pl.program_id(1)

        @pl.when((b == 0) & (s == 0))
        def _():
            vacc_ref[...] = jnp.zeros_like(vacc_ref)

        @pl.when(s == 0)
        def _():
            dmod_ref[...] = jnp.zeros_like(dmod_ref)

        dx1v = dx1_ref[...]
        out1 = out1_ref[...]
        post = post_ref[...]
        rs = lax.rsqrt(_rowmean(out1 * out1) + EPS)
        n = out1 * rs
        do = dx1v * mod_ref[0][2:3, :]
        dn = do * post
        dout1 = rs * (dn - n * _rowmean(dn * n))
        dout1b = dout1.astype(BF16)
        dout1_ref[...] = dout1b
        dmerged = _dot_nt(dout1b, wo_ref[...])
        g = _sigmoid(gt_ref[...].astype(F32) + bg_ref[...])
        yab_v = yab_ref[...].astype(F32)
        gy = g * yab_v
        mg_ref[...] = (gy[:, :D_MODEL] + gy[:, D_MODEL:]).astype(BF16)
        dm2 = jnp.concatenate([dmerged, dmerged], axis=1)
        dyab = (dm2 * g).astype(BF16)
        dyab_ref[...] = dyab
        dgt = dm2 * gy * (1.0 - g)
        dgt_ref[...] = dgt.astype(BF16)
        dya_ref[...] = _dot_nt(dyab[:, :D_MODEL], wpa_ref[...]).astype(BF16)
        dyb_ref[...] = _dot_nt(dyab[:, D_MODEL:], wpb_ref[...]).astype(BF16)
        vacc_ref[...] += jnp.concatenate(
            [_colsum(dgt), jnp.concatenate([_colsum(do * n), jnp.zeros((1, D_MODEL), F32)], axis=1),
             jnp.zeros((6, SSD_INNER), F32)], axis=0)
        dmod_ref[0] += jnp.concatenate(
            [jnp.zeros((2, D_MODEL), F32), _colsum(dx1v * (n * post)), jnp.zeros((5, D_MODEL), F32)], axis=0)

    row = lambda w: pl.BlockSpec((tm, w), lambda b, s: (b * per_seq + s, 0))
    return pl.pallas_call(
        body, name="merge_bwd", grid=(nb, per_seq),
        in_specs=[row(D_MODEL), row(D_MODEL), row(SSD_INNER),
                  pl.BlockSpec((tm, SSD_INNER), lambda b, s: (b * per_seq + s, gcb)),
                  pl.BlockSpec((1, 8, D_MODEL), lambda b, s: (b, 0, 0)),
                  VMEM_FULL, VMEM_FULL, VMEM_FULL, VMEM_FULL, VMEM_FULL],
        out_specs=[row(D_MODEL), row(SSD_INNER), row(SSD_INNER), row(SSD_INNER), row(D_MODEL), row(D_MODEL),
                   pl.BlockSpec((8, SSD_INNER), lambda b, s: (0, 0)),
                   pl.BlockSpec((1, 8, D_MODEL), lambda b, s: (b, 0, 0))],
        out_shape=[jax.ShapeDtypeStruct((t, D_MODEL), BF16), jax.ShapeDtypeStruct((t, SSD_INNER), BF16),
                   jax.ShapeDtypeStruct((t, SSD_INNER), BF16), jax.ShapeDtypeStruct((t, SSD_INNER), BF16),
                   jax.ShapeDtypeStruct((t, D_MODEL), BF16), jax.ShapeDtypeStruct((t, D_MODEL), BF16),
                   jax.ShapeDtypeStruct((8, SSD_INNER), F32), jax.ShapeDtypeStruct((nb, 8, D_MODEL), F32)],
        compiler_params=_cp("arbitrary", "arbitrary"),
    )(dx1, out1, yab, proj, mod8, bgate, post1, w_pa, w_pb, w_out)


def mlp_fwd_bwd(x1, tgt, mod8, pre2, post2, w_ff1, w_ff2, nb, seq):
    t = x1.shape[0]
    tm = min(256, seq)
    per_seq = seq // tm
    fc = 1024
    nfc = D_FF // fc

    def body(x1_ref, tgt_ref, mod_ref, pre_ref, post_ref, w1_ref, w2_ref,
             dx1_ref, h2_ref, da1_ref, act_ref, dy2_ref, loss_ref, vacc_ref, dmod_ref, r_scr):
        b, s = pl.program_id(0), pl.program_id(1)

        @pl.when((b == 0) & (s == 0))
        def _():
            vacc_ref[...] = jnp.zeros_like(vacc_ref)
            loss_ref[...] = jnp.zeros_like(loss_ref)

        @pl.when(s == 0)
        def _():
            dmod_ref[...] = jnp.zeros_like(dmod_ref)

        m = mod_ref[0]
        sh2, sc2, g2 = m[3:4, :], m[4:5, :], m[5:6, :]
        pre, post = pre_ref[...], post_ref[...]
        x1v = x1_ref[...]
        rs1 = lax.rsqrt(_rowmean(x1v * x1v) + EPS)
        n1 = x1v * rs1
        y1 = n1 * pre
        h2b = (y1 * (1.0 + sc2) + sh2).astype(BF16)
        h2_ref[...] = h2b
        y2 = jnp.zeros((tm, D_MODEL), F32)
        for c in range(nfc):
            r = jnp.maximum(_dot(h2b, w1_ref[:, c * fc:(c + 1) * fc]), 0.0)
            r_scr[:, c * fc:(c + 1) * fc] = r
            a = (r * r).astype(BF16)
            act_ref[:, c * fc:(c + 1) * fc] = a
            y2 = y2 + _dot(a, w2_ref[c * fc:(c + 1) * fc, :])
        rs2 = lax.rsqrt(_rowmean(y2 * y2) + EPS)
        n2 = y2 * rs2
        o2 = n2 * post
        diff = x1v + g2 * o2 - tgt_ref[...]
        loss_ref[...] += 0.5 * jnp.sum(_rowmean(diff * diff))
        dx2 = diff * (1.0 / D_MODEL)
        do2 = dx2 * g2
        dn2 = do2 * post
        dy2b = (rs2 * (dn2 - n2 * _rowmean(dn2 * n2))).astype(BF16)
        dy2_ref[...] = dy2b
        dh2 = jnp.zeros((tm, D_MODEL), F32)
        for c in range(nfc):
            dact = _dot_nt(dy2b, w2_ref[c * fc:(c + 1) * fc, :])
            da = (dact * (2.0 * r_scr[:, c * fc:(c + 1) * fc])).astype(BF16)
            da1_ref[:, c * fc:(c + 1) * fc] = da
            dh2 = dh2 + _dot_nt(da, w1_ref[:, c * fc:(c + 1) * fc])
        dy1 = dh2 * (1.0 + sc2)
        dn1 = dy1 * pre
        dx1_ref[...] = dx2 + rs1 * (dn1 - n1 * _rowmean(dn1 * n1))
        vacc_ref[...] += jnp.concatenate([_colsum(dy1 * n1), _colsum(do2 * n2), jnp.zeros((6, D_MODEL), F32)], axis=0)
        dmod_ref[0] += jnp.concatenate(
            [jnp.zeros((3, D_MODEL), F32), _colsum(dh2), _colsum(dh2 * y1), _colsum(dx2 * o2),
             jnp.zeros((2, D_MODEL), F32)], axis=0)

    row = lambda w: pl.BlockSpec((tm, w), lambda b, s: (b * per_seq + s, 0))
    return pl.pallas_call(
        body, name="mlp_fwd_bwd", grid=(nb, per_seq),
        in_specs=[row(D_MODEL), row(D_MODEL), pl.BlockSpec((1, 8, D_MODEL), lambda b, s: (b, 0, 0)),
                  VMEM_FULL, VMEM_FULL, VMEM_FULL, VMEM_FULL],
        out_specs=[row(D_MODEL), row(D_MODEL), row(D_FF), row(D_FF), row(D_MODEL),
                   pl.BlockSpec((8, 128), lambda b, s: (0, 0)),
                   pl.BlockSpec((8, D_MODEL), lambda b, s: (0, 0)),
                   pl.BlockSpec((1, 8, D_MODEL), lambda b, s: (b, 0, 0))],
        out_shape=[jax.ShapeDtypeStruct((t, D_MODEL), F32), jax.ShapeDtypeStruct((t, D_MODEL), BF16),
                   jax.ShapeDtypeStruct((t, D_FF), BF16), jax.ShapeDtypeStruct((t, D_FF), BF16),
                   jax.ShapeDtypeStruct((t, D_MODEL), BF16), jax.ShapeDtypeStruct((8, 128), F32),
                   jax.ShapeDtypeStruct((8, D_MODEL), F32), jax.ShapeDtypeStruct((nb, 8, D_MODEL), F32)],
        scratch_shapes=[pltpu.VMEM((tm, D_FF), F32)],
        compiler_params=_cp("arbitrary", "arbitrary"),
    )(x1, tgt, mod8, pre2, post2, w_ff1, w_ff2)


_PIECES = ((C_LRU_X, 1024), (C_LRU_G, 1024), (C_Z, 2048), (C_XBC, 4096), (C_GATES, 2048))
_NP = len(_PIECES)


def _piece_of(k, tk):
    col = k * tk
    for p, (c0, w) in enumerate(_PIECES):
        if c0 <= col < c0 + w:
            return p, (col - c0) // tk
    raise ValueError(col)


def in_proj_bwd(pieces, ddt, dx1, x2, mod8, pre1, w_main, w_dt, nb, seq, side=None):
    t = x2.shape[0]
    tm = min(512, seq)
    per_seq = seq // tm
    tk = 1024
    nk = PROJ_MAIN // tk
    where = [_piece_of(k, tk) for k in range(nk)]

    def piece_spec(p):
        first = min(k for k in range(nk) if where[k][0] == p)
        nblk = _PIECES[p][1] // tk
        return pl.BlockSpec((tm, tk), lambda b, s, k: (b * per_seq + s, jnp.clip(k - first, 0, nblk - 1)))

    def body(*refs):
        prefs = refs[:_NP]
        ddt_ref, dx1_ref, x_ref, mod_ref, pre_ref, w_ref, wdt_ref, gx_ref, vacc_ref, dmod_ref, acc_ref = refs[_NP:]
        b, s, k = pl.program_id(0), pl.program_id(1), pl.program_id(2)

        @pl.when((b == 0) & (s == 0) & (k == 0))
        def _():
            vacc_ref[...] = jnp.zeros_like(vacc_ref)

        @pl.when((s == 0) & (k == 0))
        def _():
            dmod_ref[...] = jnp.zeros_like(dmod_ref)

        @pl.when(k == 0)
        def _():
            acc_ref[...] = _dot_nt(ddt_ref[...], wdt_ref[...])

        for kk in range(nk):
            @pl.when(k == kk)
            def _(kk=kk):
                acc_ref[...] += _dot_nt(prefs[where[kk][0]][...], w_ref[...])

        @pl.when(k == nk - 1)
        def _():
            dh = acc_ref[...]
            m = mod_ref[0]
            pre = pre_ref[...]
            xv = x_ref[...]
            rs = lax.rsqrt(_rowmean(xv * xv) + EPS)
            n = xv * rs
            dy = dh * (1.0 + m[1:2, :])
            dn = dy * pre
            gx_ref[...] = dx1_ref[...] + rs * (dn - n * _rowmean(dn * n))
            vacc_ref[...] += jnp.concatenate([_colsum(dy * n), jnp.zeros((7, D_MODEL), F32)], axis=0)
            dmod_ref[0] += jnp.concatenate([_colsum(dh), _colsum(dh * (n * pre)), jnp.zeros((6, D_MODEL), F32)], axis=0)

    row = lambda w: pl.BlockSpec((tm, w), lambda b, s, k: (b * per_seq + s, 0))
    return _call(
        body, name="in_proj_bwd", grid=(nb, per_seq, nk), side=side, sem=("arbitrary", "arbitrary", "arbitrary"),
        args=(*pieces, ddt, dx1, x2, mod8, pre1, w_main, w_dt),
        in_specs=[piece_spec(p) for p in range(_NP)] + [
            row(128), row(D_MODEL), row(D_MODEL), pl.BlockSpec((1, 8, D_MODEL), lambda b, s, k: (b, 0, 0)),
            pl.BlockSpec((1, D_MODEL), lambda b, s, k: (0, 0)),
            pl.BlockSpec((D_MODEL, tk), lambda b, s, k: (0, k)),
            pl.BlockSpec((D_MODEL, 128), lambda b, s, k: (0, 0))],
        out_specs=[row(D_MODEL), pl.BlockSpec((8, D_MODEL), lambda b, s, k: (0, 0)),
                   pl.BlockSpec((1, 8, D_MODEL), lambda b, s, k: (b, 0, 0))],
        out_shape=[jax.ShapeDtypeStruct((t, D_MODEL), F32), jax.ShapeDtypeStruct((8, D_MODEL), F32),
                   jax.ShapeDtypeStruct((nb, 8, D_MODEL), F32)],
        scratch_shapes=[pltpu.VMEM((tm, D_MODEL), F32)])


def in_proj_wgrad(h1, pieces, ddt, name="in_proj_wgrad"):
    t = h1.shape[0]
    tt = min(1024, t)
    tn = 1024
    nn = PROJ_MAIN // tn
    nt = t // tt
    where = [_piece_of(n, tn) for n in range(nn)]

    def piece_spec(p):
        first = min(n for n in range(nn) if where[n][0] == p)
        nblk = _PIECES[p][1] // tn
        return pl.BlockSpec((tt, tn), lambda n, k: (jnp.where((n >= first) & (n < first + nblk), k, 0),
                                                    jnp.clip(n - first, 0, nblk - 1)))

    def body(h_ref, *refs):
        prefs = refs[:_NP]
        ddt_ref, dw_ref, dwdt_ref, acc_ref, accdt_ref = refs[_NP:]
        n, k = pl.program_id(0), pl.program_id(1)
        hv = h_ref[...]
        for nn_ in range(nn):
            @pl.when(n == nn_)
            def _(nn_=nn_):
                p = _dot_tn(hv, prefs[where[nn_][0]][...])

                @pl.when(k == 0)
                def _():
                    acc_ref[...] = p

                @pl.when(k > 0)
                def _():
                    acc_ref[...] += p

        @pl.when(n == 0)
        def _():
            p = _dot_tn(hv, ddt_ref[...])

            @pl.when(k == 0)
            def _():
                accdt_ref[...] = p

            @pl.when(k > 0)
            def _():
                accdt_ref[...] += p

        @pl.when(k == nt - 1)
        def _():
            dw_ref[...] = acc_ref[...].astype(BF16)

        @pl.when((n == 0) & (k == nt - 1))
        def _():
            dwdt_ref[...] = accdt_ref[...].astype(BF16)

    return pl.pallas_call(
        body, name=name, grid=(nn, nt),
        in_specs=[pl.BlockSpec((tt, D_MODEL), lambda n, k: (k, 0))] + [piece_spec(p) for p in range(_NP)]
        + [pl.BlockSpec((tt, 128), lambda n, k: (k, 0))],
        out_specs=[pl.BlockSpec((D_MODEL, tn), lambda n, k: (0, n)), pl.BlockSpec((D_MODEL, 128), lambda n, k: (0, 0))],
        out_shape=[jax.ShapeDtypeStruct((D_MODEL, PROJ_MAIN), BF16), jax.ShapeDtypeStruct((D_MODEL, 128), BF16)],
        scratch_shapes=[pltpu.VMEM((D_MODEL, tn), F32), pltpu.VMEM((D_MODEL, 128), F32)],
        compiler_params=_cp("arbitrary", "arbitrary"),
    )(h1, *pieces, ddt)


def _log1p(u):
    w = 1.0 + u
    return jnp.log(w) - ((w - 1.0) - u) / w


def _softplus(x):
    return jnp.maximum(x, 0.0) + _log1p(jnp.exp(-jnp.abs(x)))


def _head_mask(h):
    lane = lax.broadcasted_iota(jnp.int32, (1, SSD_GW), 1)
    return (lane >= SSD_P * h) & (lane < SSD_P * (h + 1))


def _expand4(m, g):
    lane = lax.broadcasted_iota(jnp.int32, (1, SSD_GW), 1)
    col = lambda h: m[:, 4 * g + h:4 * g + h + 1]
    return jnp.where(lane < SSD_P, col(0), jnp.where(lane < 2 * SSD_P, col(1), jnp.where(lane < 3 * SSD_P, col(2), col(3))))


def _reduce4(v, g):
    lane = lax.broadcasted_iota(jnp.int32, (1, SSD_N), 1)
    out = jnp.zeros((v.shape[0], SSD_N), F32)
    for h in range(4):
        s = jnp.sum(jnp.where(_head_mask(h), v, 0.0), axis=1, keepdims=True)
        out = out + jnp.where(lane == 4 * g + h, s, 0.0)
    return out


def _ssd_heads(dtraw, hp, tri):
    xdt = dtraw + hp[0:1, :]
    dt = _softplus(xdt)
    cs = _dot_hi(tri, dt * hp[1:2, :])
    cs_last = cs[SSD_L - 1:SSD_L, :]
    return dict(xdt=xdt, dt=dt, cs=cs, cs_t=cs.T, e=jnp.exp(cs), w=jnp.exp(cs_last - cs), el=jnp.exp(cs_last))


def _ssd_group(g, hd, xs_b, bm_b, cm_b, d_x, st):
    ll = SSD_L
    xs = xs_b.astype(F32)
    cs, cs_t = hd["cs"], hd["cs_t"]
    e_x, w_x, el_x, dt_x = _expand4(hd["e"], g), _expand4(hd["w"], g), _expand4(hd["el"], g), _expand4(hd["dt"], g)
    xd = xs * dt_x
    gcb = _dot_nt(cm_b, bm_b)
    ri = lax.broadcasted_iota(jnp.int32, (ll, ll), 0)
    ci = lax.broadcasted_iota(jnp.int32, (ll, ll), 1)
    dks, ms = [], []
    ydiag = jnp.zeros((ll, SSD_GW), F32)
    for h in range(4):
        k = 4 * g + h
        dk = jnp.exp(jnp.where(ri >= ci, cs[:, k:k + 1] - cs_t[k:k + 1, :], -1e30))
        dks.append(dk)
        ms.append((gcb * dk).astype(BF16))
        ydiag = ydiag + _dot(ms[h], jnp.where(_head_mask(h), xd, 0.0).astype(BF16))
    yoff = _dot(cm_b, st.astype(BF16)) * e_x
    y = ydiag + yoff + d_x * xs
    st_new = st * el_x + _dot(bm_b.astype(F32).T.astype(BF16), (xd * w_x).astype(BF16))
    return dict(xs=xs, e_x=e_x, w_x=w_x, el_x=el_x, dt_x=dt_x, xd=xd, gcb=gcb, dks=dks, ms=ms, yoff=yoff, y=y,
                st_new=st_new)


def ssd_consts():
    hh = np.arange(SSD_N)
    tri = (hh[:, None] >= hh[None, :]).astype(np.float32)
    return jnp.asarray(tri), jnp.asarray(tri.T)


def ssd_params(dt_bias, a_log, d_skip, norm_w):
    padh = lambda v: jnp.pad(v.reshape(1, SSD_HEADS), ((0, 0), (0, SSD_N - SSD_HEADS)))
    hp = jnp.concatenate([padh(dt_bias), padh(-jnp.exp(a_log)), jnp.zeros((6, SSD_N), F32)], axis=0)
    lp = jnp.concatenate([norm_w.reshape(1, SSD_INNER), jnp.repeat(d_skip, SSD_P).reshape(1, SSD_INNER),
                          jnp.zeros((6, SSD_INNER), F32)], axis=0)
    return hp, lp


def _b_cols(g):
    return slice(SSD_INNER + g * SSD_N, SSD_INNER + (g + 1) * SSD_N)


def _c_cols(g):
    return slice(SSD_INNER + (SSD_G + g) * SSD_N, SSD_INNER + (SSD_G + g + 1) * SSD_N)


def _ssd_specs(nc, rc):
    return [pl.BlockSpec((SSD_L, 2 * SSD_INNER), lambda b, c: (b * nc + rc(c), 0)),
            pl.BlockSpec((SSD_L, SSD_INNER), lambda b, c: (b * nc + rc(c), C_Z // SSD_INNER)),
            pl.BlockSpec((SSD_L, SSD_N), lambda b, c: (b * nc + rc(c), 0))]


def ssd_fwd(xbc, proj, dtraw, hp, lp, tri, nb, seq):
    t = xbc.shape[0]
    nc = seq // SSD_L

    def body(xbc_ref, z_ref, dt_ref, hp_ref, lp_ref, tri_ref, y_ref, sts_ref, st_scr):
        @pl.when(pl.program_id(1) == 0)
        def _():
            st_scr[...] = jnp.zeros_like(st_scr)

        hd = _ssd_heads(dt_ref[...], hp_ref[...], tri_ref[...])
        for g in range(SSD_G):
            gs = slice(g * SSD_GW, (g + 1) * SSD_GW)
            st = st_scr[g]
            sts_ref[0, g] = st
            f = _ssd_group(g, hd, xbc_ref[:, gs], xbc_ref[:, _b_cols(g)], xbc_ref[:, _c_cols(g)], lp_ref[1:2, gs], st)
            st_scr[g] = f["st_new"]
            zf = z_ref[:, gs].astype(F32)
            yg = f["y"] * (zf * _sigmoid(zf))
            y_ref[:, gs] = (yg * lax.rsqrt(_rowmean(yg * yg) + EPS) * lp_ref[0:1, gs]).astype(BF16)

    return pl.pallas_call(
        body, name="ssd_fwd", grid=(nb, nc),
        in_specs=_ssd_specs(nc, lambda c: c) + [VMEM_FULL, VMEM_FULL, VMEM_FULL],
        out_specs=[pl.BlockSpec((SSD_L, SSD_INNER), lambda b, c: (b * nc + c, 0)),
                   pl.BlockSpec((1, SSD_G, SSD_N, SSD_GW), lambda b, c: (b * nc + c, 0, 0, 0))],
        out_shape=[jax.ShapeDtypeStruct((t, SSD_INNER), BF16),
                   jax.ShapeDtypeStruct((nb * nc, SSD_G, SSD_N, SSD_GW), F32)],
        scratch_shapes=[pltpu.VMEM((SSD_G, SSD_N, SSD_GW), F32)],
        compiler_params=_cp("arbitrary", "arbitrary"),
    )(xbc, proj, dtraw, hp, lp, tri)


def ssd_bwd(xbc, proj, dtraw, hp, lp, tri, triu, states, dyn, nb, seq):
    t = xbc.shape[0]
    nc = seq // SSD_L
    ll = SSD_L

    def body(xbc_ref, z_ref, dt_ref, sts_ref, dy_ref, hp_ref, lp_ref, tri_ref, triu_ref,
             dxbc_ref, dz_ref, ddt_ref, hpg_ref, lpg_ref, dst_scr):
        b, c_i = pl.program_id(0), pl.program_id(1)

        @pl.when((b == 0) & (c_i == 0))
        def _():
            hpg_ref[...] = jnp.zeros_like(hpg_ref)
            lpg_ref[...] = jnp.zeros_like(lpg_ref)

        @pl.when(c_i == 0)
        def _():
            dst_scr[...] = jnp.zeros_like(dst_scr)

        hp = hp_ref[...]
        hd = _ssd_heads(dt_ref[...], hp, tri_ref[...])
        lane = lax.broadcasted_iota(jnp.int32, (1, SSD_N), 1)
        subl = lax.broadcasted_iota(jnp.int32, (SSD_N, 1), 0)
        dcs = jnp.zeros((ll, SSD_N), F32)
        dcs_t = jnp.zeros((SSD_N, ll), F32)
        last = jnp.zeros((1, SSD_N), F32)
        dxx = jnp.zeros((ll, SSD_N), F32)
        for g in range(SSD_G):
            gs = slice(g * SSD_GW, (g + 1) * SSD_GW)
            st = sts_ref[0, g]
            dst = dst_scr[g]
            bm_b, cm_b = xbc_ref[:, _b_cols(g)], xbc_ref[:, _c_cols(g)]
            d_x = lp_ref[1:2, gs]
            f = _ssd_group(g, hd, xbc_ref[:, gs], bm_b, cm_b, d_x, st)
            xs, xd, gcb = f["xs"], f["xd"], f["gcb"]
            e_x, w_x, el_x, dt_x = f["e_x"], f["w_x"], f["el_x"], f["dt_x"]
            stb, dstb = st.astype(BF16), dst.astype(BF16)
            zf = z_ref[:, gs].astype(F32)
            sg = _sigmoid(zf)
            sz = zf * sg
            yg = f["y"] * sz
            rstd = lax.rsqrt(_rowmean(yg * yg) + EPS)
            n = yg * rstd
            dyn_v = dy_ref[:, gs].astype(F32)
            dn = dyn_v * lp_ref[0:1, gs]
            dyg = rstd * (dn - n * _rowmean(dn * n))
            dy = dyg * sz
            dz_ref[:, gs] = (dyg * f["y"] * (sg * (1.0 + zf * (1.0 - sg)))).astype(BF16)
            dyb = dy.astype(BF16)
            r_ = _dot(bm_b, dstb)
            dxd = w_x * r_
            dqb = (dy * e_x).astype(BF16)
            dcm = _dot_nt(dqb, stb)
            dst_scr[g] = dst * el_x + _dot_tn(cm_b, dqb)
            dbm = _dot_nt((xd * w_x).astype(BF16), dstb)
            xdb = xd.astype(BF16)
            dgm = jnp.zeros((ll, ll), F32)
            for h in range(4):
                k = 4 * g + h
                hm = _head_mask(h)
                dxd = dxd + jnp.where(hm, _dot_tn(f["ms"][h], dyb), 0.0)
                dm = _dot_nt(jnp.where(hm, dy, 0.0).astype(BF16), xdb) * f["dks"][h]
                dgm = dgm + dm
                dseg = dm * gcb
                dcs = dcs + jnp.where(lane == k, jnp.sum(dseg, axis=1, keepdims=True), 0.0)
                dcs_t = dcs_t + jnp.where(subl == k, jnp.sum(dseg, axis=0, keepdims=True), 0.0)
            dgmb = dgm.astype(BF16)
            dxbc_ref[:, _c_cols(g)] = (dcm + _dot(dgmb, bm_b)).astype(BF16)
            dxbc_ref[:, _b_cols(g)] = (dbm + _dot_tn(dgmb, cm_b)).astype(BF16)
            v = _reduce4(r_ * xd * w_x, g)
            dcs = dcs + _reduce4(dy * f["yoff"], g) - v
            last = last + _colsum(v) + _reduce4(_colsum(dst * st) * el_x, g)
            dxx = dxx + _reduce4(dxd * xs, g)
            dxbc_ref[:, gs] = (d_x * dy + dxd * dt_x).astype(BF16)
            lpg_ref[0:1, gs] += _colsum(dyn_v * n)
            lpg_ref[1:2, gs] += _colsum(dy * xs)
        rowi = lax.broadcasted_iota(jnp.int32, (ll, 1), 0)
        da = _dot_hi(triu_ref[...], dcs - dcs_t.T + jnp.where(rowi == ll - 1, last, 0.0))
        ddt = (dxx + da * hp[1:2, :]) * _sigmoid(hd["xdt"])
        ddt_ref[...] = ddt
        hpg_ref[...] += jnp.concatenate([_colsum(ddt), _colsum(da * hd["dt"]), jnp.zeros((6, SSD_N), F32)], axis=0)

    rc = lambda c: nc - 1 - c
    return pl.pallas_call(
        body, name="ssd_bwd", grid=(nb, nc),
        in_specs=_ssd_specs(nc, rc) + [
            pl.BlockSpec((1, SSD_G, SSD_N, SSD_GW), lambda b, c: (b * nc + rc(c), 0, 0, 0)),
            pl.BlockSpec((SSD_L, SSD_INNER), lambda b, c: (b * nc + rc(c), 0)),
            VMEM_FULL, VMEM_FULL, VMEM_FULL, VMEM_FULL],
        out_specs=[pl.BlockSpec((SSD_L, 2 * SSD_INNER), lambda b, c: (b * nc + rc(c), 0)),
                   pl.BlockSpec((SSD_L, SSD_INNER), lambda b, c: (b * nc + rc(c), 0)),
                   pl.BlockSpec((SSD_L, SSD_N), lambda b, c: (b * nc + rc(c), 0)),
                   pl.BlockSpec((8, SSD_N), lambda b, c: (0, 0)),
                   pl.BlockSpec((8, SSD_INNER), lambda b, c: (0, 0))],
        out_shape=[jax.ShapeDtypeStruct((t, 2 * SSD_INNER), BF16), jax.ShapeDtypeStruct((t, SSD_INNER), BF16),
                   jax.ShapeDtypeStruct((t, SSD_N), F32), jax.ShapeDtypeStruct((8, SSD_N), F32),
                   jax.ShapeDtypeStruct((8, SSD_INNER), F32)],
        scratch_shapes=[pltpu.VMEM((SSD_G, SSD_N, SSD_GW), F32)],
        compiler_params=_cp("arbitrary", "arbitrary"),
    )(xbc, proj, dtraw, states, dyn, hp, lp, tri, triu)


def ada_fwd(c_all, w_cols, b_cols):
    def body(c_ref, w_ref, b_ref, o_ref):
        cv = c_ref[...]
        o_ref[...] = _dot_hi(cv * _sigmoid(cv), w_ref[...]) + b_ref[...]

    return pl.pallas_call(body, name="ada_fwd", out_shape=jax.ShapeDtypeStruct((c_all.shape[0], w_cols.shape[1]), F32),
                          compiler_params=pltpu.CompilerParams(vmem_limit_bytes=VMEM_LIMIT))(c_all, w_cols, b_cols)


def ada_bwd(c_all, dmod_cols, dmod_all):
    def body(c_ref, dc_ref, da_ref, gw_ref, gb_ref):
        cv = c_ref[...]
        gw_ref[...] = lax.dot_general(cv * _sigmoid(cv), dc_ref[...], (((0,), (0,)), ((), ())),
                                      precision=lax.Precision.HIGHEST, preferred_element_type=F32)
        gb_ref[...] = _colsum(da_ref[...])

    return pl.pallas_call(
        body, name="ada_bwd",
        out_shape=[jax.ShapeDtypeStruct((c_all.shape[1], dmod_cols.shape[1]), F32),
                   jax.ShapeDtypeStruct((1, dmod_all.shape[1]), F32)],
        compiler_params=pltpu.CompilerParams(vmem_limit_bytes=VMEM_LIMIT))(c_all, dmod_cols, dmod_all)


def adamw(parts, w, m, v, name):
    n, r, c = parts.shape
    tr = r if r <= 256 else 128

    def body(p_ref, w_ref, m_ref, v_ref, g_ref, d_ref, nm_ref, nv_ref):
        g = p_ref[0].astype(F32)
        for s in range(1, n):
            g = g + p_ref[s].astype(F32)
        m2 = ADAM_B1 * m_ref[0] + (1.0 - ADAM_B1) * g
        v2 = ADAM_B2 * v_ref[0] + (1.0 - ADAM_B2) * (g * g)
        m_hat = m2 / (1.0 - ADAM_B1 ** ADAM_STEP)
        v_hat = v2 / (1.0 - ADAM_B2 ** ADAM_STEP)
        g_ref[0] = g
        d_ref[0] = -ADAM_LR * (m_hat / (jnp.sqrt(v_hat) + ADAM_EPS) + ADAM_WD * w_ref[0])
        nm_ref[0] = m2
        nv_ref[0] = v2

    blk = pl.BlockSpec((1, tr, c), lambda i: (0, i, 0))
    return pl.pallas_call(
        body, name=name, grid=(r // tr,),
        in_specs=[pl.BlockSpec((n, tr, c), lambda i: (0, i, 0)), blk, blk, blk], out_specs=[blk] * 4,
        out_shape=[jax.ShapeDtypeStruct((1, r, c), F32)] * 4,
        compiler_params=_cp("parallel"),
    )(parts, w, m, v)


def _dev_index(px, py, pc):
    return 4 * px + 2 * py + pc


class _Exchange:
    def __init__(self, arrs):
        self.arrs = list(arrs)
        self.na = len(self.arrs)
        self.scratch = [pltpu.SemaphoreType.DMA((7 * self.na,)), pltpu.SemaphoreType.DMA((7 * self.na,)),
                        pltpu.SemaphoreType.DMA((self.na,))]


class Gather(_Exchange):
    def __init__(self, arrs):
        super().__init__(arrs)
        self.out_shape = [jax.ShapeDtypeStruct((NDEV,) + a.shape, a.dtype) for a in self.arrs]

    def _plan(self, ins, outs, sems):
        na = self.na
        send_sems, recv_sems, local_sems = sems
        x, y, c = lax.axis_index("x"), lax.axis_index("y"), lax.axis_index("c")
        me, sibling = (x, y, c), (x, y, 1 - c)
        chips = [(1 - x, y), (x, 1 - y), (1 - x, 1 - y)]

        def copy(a, k, block, to, src=None):
            dst = outs[a].at[_dev_index(*block)]
            return pltpu.make_async_remote_copy(
                src_ref=dst if src is None else src, dst_ref=dst, send_sem=send_sems.at[a * 7 + k],
                recv_sem=recv_sems.at[a * 7 + k], device_id=to, device_id_type=MESH)

        mine = [pltpu.make_async_copy(ins[a], outs[a].at[_dev_index(*me)], local_sems.at[a]) for a in range(na)]
        first = []
        for a in range(na):
            first.append(copy(a, 0, me, sibling, src=ins[a]))
            first += [copy(a, 1 + j, me, (*chip, c), src=ins[a]) for j, chip in enumerate(chips)]
        return copy, mine, first, me, sibling, chips, c

    def start(self, ins, outs, sems):
        _, mine, first, *_ = self._plan(ins, outs, sems)
        for cp in mine + first:
            cp.start()

    def finish(self, ins, outs, sems):
        copy, mine, first, me, sibling, chips, c = self._plan(ins, outs, sems)
        passed = []
        for j, chip in enumerate(chips):
            for a in range(self.na):
                copy(a, 1 + j, (*chip, c), me).wait_recv()
                cp = copy(a, 4 + j, (*chip, c), sibling)
                cp.start()
                passed.append(cp)
        for a in range(self.na):
            copy(a, 0, sibling, me).wait_recv()
            for j, chip in enumerate(chips):
                copy(a, 4 + j, (*chip, 1 - c), me).wait_recv()
        for cp in first + passed:
            cp.wait_send()
        for cp in mine:
            cp.wait()


class Scatter(_Exchange):
    def __init__(self, arrs):
        super().__init__(arrs)
        self.out_shape = [jax.ShapeDtypeStruct(a.shape, a.dtype) for a in self.arrs]

    def _plan(self, ins, outs, sems, arrivals):
        send_sems, recv_sems, local_sems = sems
        x, y, c = lax.axis_index("x"), lax.axis_index("y"), lax.axis_index("c")
        me = _dev_index(x, y, c)
        masks = [(mx, my, mc) for mx in (0, 1) for my in (0, 1) for mc in (0, 1)][1:]
        flip = lambda v, bit: 1 - v if bit else v
        mine = [pltpu.make_async_copy(ins[a].at[me], outs[a].at[me], local_sems.at[a]) for a in range(self.na)]
        sends, recvs = [], []
        for k, (mx, my, mc) in enumerate(masks):
            peer = (flip(x, mx), flip(y, my), flip(c, mc))
            pidx = _dev_index(*peer)
            for a in range(self.na):
                on = dict(send_sem=send_sems.at[a * 7 + k], recv_sem=recv_sems.at[a * 7 + k], device_id=peer,
                          device_id_type=MESH)
                sends.append(pltpu.make_async_remote_copy(src_ref=ins[a].at[pidx], dst_ref=outs[a].at[me], **on))
                if arrivals:
                    recvs.append(pltpu.make_async_remote_copy(src_ref=ins[a].at[pidx], dst_ref=outs[a].at[pidx], **on))
        return mine, sends, recvs

    def start(self, ins, outs, sems):
        mine, sends, _ = self._plan(ins, outs, sems, arrivals=False)
        for cp in mine + sends:
            cp.start()

    def finish(self, ins, outs, sems):
        mine, sends, recvs = self._plan(ins, outs, sems, arrivals=True)
        for cp in recvs:
            cp.wait_recv()
        for cp in sends:
            cp.wait_send()
        for cp in mine:
            cp.wait()


def exchange_call(ex, name):
    na = ex.na

    def body(*refs):
        ins, outs, sems = refs[:na], refs[na:2 * na], refs[2 * na:]
        ex.start(ins, outs, sems)
        ex.finish(ins, outs, sems)

    return pl.pallas_call(body, name=name, in_specs=[ANY] * na, out_specs=[ANY] * na, out_shape=ex.out_shape,
                          scratch_shapes=ex.scratch)(*ex.arrs)


def all_gather(arrs, name):
    return exchange_call(Gather(arrs), name)


def _call(body, *, name, grid, in_specs, out_specs, out_shape, scratch_shapes=(), sem, args, side=None):
    if side is None:
        outs = pl.pallas_call(body, name=name, grid=grid, in_specs=list(in_specs), out_specs=list(out_specs),
                              out_shape=list(out_shape), scratch_shapes=list(scratch_shapes),
                              compiler_params=_cp(*sem))(*args)
        return outs, []
    ni, no, ns, na = len(in_specs), len(out_specs), len(scratch_shapes), side.na

    def wrapped(*refs):
        ins, s_in = refs[:ni], refs[ni:ni + na]
        outs, s_out = refs[ni + na:ni + na + no], refs[ni + na + no:ni + 2 * na + no]
        scr, sems = refs[ni + 2 * na + no:ni + 2 * na + no + ns], refs[ni + 2 * na + no + ns:]
        pids = [pl.program_id(i) for i in range(len(grid))]
        first = functools.reduce(lambda p, q: p & q, [p == 0 for p in pids])
        last = functools.reduce(lambda p, q: p & q, [p == g - 1 for p, g in zip(pids, grid)])

        @pl.when(first)
        def _():
            side.start(s_in, s_out, sems)

        body(*ins, *outs, *scr)

        @pl.when(last)
        def _():
            side.finish(s_in, s_out, sems)

    outs = pl.pallas_call(
        wrapped, name=name, grid=grid, in_specs=list(in_specs) + [ANY] * na, out_specs=list(out_specs) + [ANY] * na,
        out_shape=list(out_shape) + side.out_shape, scratch_shapes=list(scratch_shapes) + side.scratch,
        compiler_params=_cp(*["arbitrary"] * len(grid)))(*args, *side.arrs)
    return outs[:no], outs[no:]


WEIGHTS = ('w_ada', 'b_ada', 'pre_norm1', 'post_norm1', 'w_in', 'b_gate', 'lru_conv_w', 'lru_conv_b', 'lru_wa',
           'lru_ba', 'lru_wx', 'lru_bx', 'lru_lambda', 'w_pa', 'ssd_conv_w', 'ssd_conv_b', 'ssd_dt_bias', 'ssd_a_log',
           'ssd_d', 'ssd_norm_w', 'w_pb', 'w_out', 'pre_norm2', 'post_norm2', 'w_ff1', 'w_ff2')
BIG = ('w_in', 'w_pa', 'w_pb', 'w_out', 'w_ff1', 'w_ff2')
REPL = ('pre_norm1', 'post_norm1', 'b_gate', 'lru_conv_b', 'lru_wa', 'lru_ba', 'lru_wx', 'lru_bx', 'lru_lambda',
        'ssd_conv_b', 'ssd_dt_bias', 'ssd_a_log', 'ssd_d', 'ssd_norm_w', 'pre_norm2', 'post_norm2')
LANES = 1024


def _rows(n):
    return -(-n // LANES)


def _pack(vals, total_rows):
    parts = []
    for v in vals:
        f = v.reshape(-1).astype(F32)
        parts.append(jnp.pad(f, (0, _rows(f.shape[0]) * LANES - f.shape[0])))
    flat = jnp.concatenate(parts)
    return jnp.pad(flat.reshape(-1, LANES), ((0, total_rows - flat.shape[0] // LANES), (0, 0)))


def _unpack(slab, shapes):
    out, r = [], 0
    for s in shapes:
        n = int(np.prod(s))
        out.append(slab[r:r + _rows(n)].reshape(-1)[:n].reshape(s))
        r += _rows(n)
    return out


def _block_diag4(w):
    w4 = w.reshape(4, 4, 64, 64)
    eye = jnp.eye(4, dtype=w.dtype)
    return (w4[:, :, :, None, :] * eye[None, :, None, :, None]).reshape(4, LRU_BLOCK, LRU_BLOCK)


def _diag_blocks4(m):
    m5 = m.reshape(4, 4, 64, 4, 64)
    return jnp.stack([m5[:, a, :, a, :] for a in range(4)], axis=1).reshape(LRU_HEADS, 64, 64)


def kernel(x, c, w_ada, b_ada, pre_norm1, post_norm1, w_in, b_gate, lru_conv_w, lru_conv_b, lru_wa, lru_ba, lru_wx, lru_bx, lru_lambda, w_pa, ssd_conv_w, ssd_conv_b, ssd_dt_bias, ssd_a_log, ssd_d, ssd_norm_w, w_pb, w_out, pre_norm2, post_norm2, w_ff1, w_ff2, loss_target, m_w_ada, m_b_ada, m_pre_norm1, m_post_norm1, m_w_in, m_b_gate, m_lru_conv_w, m_lru_conv_b, m_lru_wa, m_lru_ba, m_lru_wx, m_lru_bx, m_lru_lambda, m_w_pa, m_ssd_conv_w, m_ssd_conv_b, m_ssd_dt_bias, m_ssd_a_log, m_ssd_d, m_ssd_norm_w, m_w_pb, m_w_out, m_pre_norm2, m_post_norm2, m_w_ff1, m_w_ff2, v_w_ada, v_b_ada, v_pre_norm1, v_post_norm1, v_w_in, v_b_gate, v_lru_conv_w, v_lru_conv_b, v_lru_wa, v_lru_ba, v_lru_wx, v_lru_bx, v_lru_lambda, v_w_pa, v_ssd_conv_w, v_ssd_conv_b, v_ssd_dt_bias, v_ssd_a_log, v_ssd_d, v_ssd_norm_w, v_w_pb, v_w_out, v_pre_norm2, v_post_norm2, v_w_ff1, v_w_ff2):
    given = dict(locals())
    w = {k: given[k] for k in WEIGHTS}
    mom = {k: given["m_" + k] for k in WEIGHTS}
    var = {k: given["v_" + k] for k in WEIGHTS}
    nb, seq, _ = x.shape
    assert nb == 2 and seq % 512 == 0, (nb, seq)
    t = nb * seq
    me = _dev_index(lax.axis_index("x"), lax.axis_index("y"), lax.axis_index("c"))
    x2 = x.reshape(t, D_MODEL)
    tgt2 = loss_target.reshape(t, D_MODEL)
    ada_cols = w_ada.shape[2]

    slab = jnp.zeros((16, LANES), F32)
    slab = slab.at[0:nb].set(c)
    slab = slab.at[2:6, 0:lru_conv_w.shape[2]].set(lru_conv_w[0])
    slab = slab.at[6:10, 0:ssd_conv_w.shape[2]].set(ssd_conv_w[0])
    (g1,) = all_gather([slab], "gather_cond")
    c_all = g1[:, 0:nb].reshape(NDEV * nb, D_MODEL)
    lru_cw = g1[:, 2:6, 0:lru_conv_w.shape[2]].transpose(1, 0, 2).reshape(4, D_MODEL)
    ssd_cw = g1[:, 6:10, 0:ssd_conv_w.shape[2]].transpose(1, 0, 2).reshape(4, 2 * SSD_INNER)
    b_cols = lax.dynamic_slice(b_ada, (0, me * ada_cols), (1, ada_cols))
    mod_cols = ada_fwd(c_all, w_ada[0], b_cols)
    (g2,) = all_gather([mod_cols], "gather_mod")
    mod_all = g2.transpose(1, 0, 2).reshape(NDEV * nb, N_MOD * D_MODEL)
    mod_mine = lax.dynamic_slice(mod_all, (me * nb, 0), (nb, N_MOD * D_MODEL)).reshape(nb, N_MOD, D_MODEL)
    mod8 = jnp.pad(mod_mine, ((0, 0), (0, 8 - N_MOD), (0, 0)))

    (gw_in,) = all_gather([w_in.astype(BF16)], "gather_w_in")
    w_nat = gw_in[:, 0].transpose(1, 0, 2).reshape(D_MODEL, IN_DIM)
    w_main = jnp.concatenate([w_nat[:, :DT_COL0], w_nat[:, DT_COL0 + SSD_HEADS:]], axis=1)
    w_dt = jnp.pad(w_nat[:, DT_COL0:DT_COL0 + SSD_HEADS], ((0, 0), (0, 128 - SSD_HEADS)))

    wa_bd = _block_diag4(lru_wa[0]).astype(BF16)
    wx_bd = _block_diag4(lru_wx[0]).astype(BF16)
    lam = lru_lambda[0]
    vec = _pack([lru_ba, lru_bx, jax.nn.softplus(-lam)], 8)
    tri, triu = ssd_consts()
    hp, lp = ssd_params(ssd_dt_bias[0], ssd_a_log[0], ssd_d[0], ssd_norm_w[0])

    rest = Gather([w[k].astype(BF16) for k in BIG[1:]])
    (proj, h1, dtraw), gw = in_proj_fwd(x2, mod8, pre_norm1, w_main, w_dt, seq, side=rest)
    w_pa_f = gw[0].reshape(D_MODEL, D_MODEL)
    w_pb_f = gw[1].reshape(SSD_INNER, D_MODEL)
    w_out_f = gw[2].reshape(D_MODEL, D_MODEL)
    w_ff1_f = gw[3][:, 0].transpose(1, 0, 2).reshape(D_MODEL, D_FF)
    w_ff2_f = gw[4].reshape(D_FF, D_MODEL)
    xa = conv_fwd(proj, C_LRU_X, D_MODEL, lru_cw, lru_conv_b, nb, seq, False, "conv_lru_fwd")
    xbc = conv_fwd(proj, C_XBC, 2 * SSD_INNER, ssd_cw, ssd_conv_b, nb, seq, True, "conv_ssd_fwd")
    ya_in, hst = lru_fwd(xa, proj, wa_bd, wx_bd, vec, nb, seq)
    yb_in, states = ssd_fwd(xbc, proj, dtraw, hp, lp, tri, nb, seq)
    yab, out1, x1 = merge_fwd(ya_in, yb_in, proj, x2, mod8, b_gate, post_norm1, w_pa_f, w_pb_f, w_out_f, seq)

    dx1, h2, da1, act, dy2, loss8, vacc_mlp, dmod_mlp = mlp_fwd_bwd(
        x1, tgt2, mod8, pre_norm2, post_norm2, w_ff1_f, w_ff2_f, nb, seq)
    wg = dict(out_dtype=BF16, ta=True, tm=1024, tn=1024, tk=1024)
    dw_ff1 = matmul(h2, da1, name="wgrad_ff1", **wg)
    dw_ff2 = matmul(act, dy2, name="wgrad_ff2", **wg)
    dya_in, dyb_in, dgates, dyab, dout1, merged, vacc_mg, dmod_mg = merge_bwd(
        dx1, out1, yab, proj, mod8, b_gate, post_norm1, w_pa_f, w_pb_f, w_out_f, nb, seq)
    dw_out = matmul(merged, dout1, name="wgrad_out", **wg)
    dw_pa = matmul(ya_in, dyab, name="wgrad_pa", n=D_MODEL, b_off=0, **wg)
    dw_pb = matmul(yb_in, dyab, name="wgrad_pb", n=D_MODEL, b_off=1, **wg)
    by_rows = lambda g: g.reshape(NDEV, g.shape[0] // NDEV, g.shape[1])
    by_cols = lambda g: g.reshape(g.shape[0], NDEV, g.shape[1] // NDEV).transpose(1, 0, 2)
    (dxa, dlg, dwa_bd, dwx_bd, dvec), parts_ff = lru_bwd(
        dya_in, xa, proj, hst, wa_bd, wx_bd, vec, nb, seq, side=Scatter([by_cols(dw_ff1), by_rows(dw_ff2)]))
    dxbc, dz, ddt, hpg, lpg = ssd_bwd(xbc, proj, dtraw, hp, lp, tri, triu, states, dyb_in, nb, seq)
    (dlx, acc_l), _ = conv_bwd(proj, C_LRU_X, D_MODEL, lru_cw, lru_conv_b, dxa, nb, seq, False, "conv_lru_bwd")
    (dxr, acc_s), parts_mg = conv_bwd(proj, C_XBC, 2 * SSD_INNER, ssd_cw, ssd_conv_b, dxbc, nb, seq, True,
                                      "conv_ssd_bwd", side=Scatter([by_rows(dw_pa), by_rows(dw_pb), by_rows(dw_out)]))
    pieces = (dlx, dlg, dz, dxr, dgates)
    ddt_b = ddt.astype(BF16)
    dw_main, dw_dt = in_proj_wgrad(h1, pieces, ddt_b)
    dw_nat = jnp.concatenate([dw_main[:, :DT_COL0], dw_dt[:, :SSD_HEADS], dw_main[:, DT_COL0:]], axis=1)
    (grad_x, vacc_in, dmod_in), parts_in = in_proj_bwd(pieces, ddt_b, dx1, x2, mod8, pre_norm1, w_main, w_dt, nb, seq,
                                                       side=Scatter([by_cols(dw_nat)]))
    parts = dict(zip(BIG, (parts_in[0], *parts_mg, *parts_ff)))

    dmod = (dmod_in + dmod_mg + dmod_mlp)[:, :N_MOD].reshape(nb, N_MOD * D_MODEL)
    (g3,) = all_gather([jnp.pad(dmod, ((0, 8 - nb), (0, 0)))], "gather_dmod")
    dmod_all = g3[:, :nb].reshape(NDEV * nb, N_MOD * D_MODEL)
    dmod_cols = lax.dynamic_slice(dmod_all, (0, me * ada_cols), (NDEV * nb, ada_cols))
    g_w_ada, g_b_ada = ada_bwd(c_all, dmod_cols, dmod_all)

    res = {}
    for k in BIG:
        res[k] = adamw(parts[k], w[k], mom[k], var[k], "adamw_" + k)
    res['w_ada'] = adamw(g_w_ada[None], w_ada, m_w_ada, v_w_ada, "adamw_w_ada")

    a_neg = -jnp.exp(ssd_a_log[0])
    small = {
        'pre_norm1': vacc_in[0], 'post_norm1': vacc_mg[1, :D_MODEL], 'b_gate': vacc_mg[0],
        'lru_conv_b': acc_l[4], 'lru_wa': _diag_blocks4(dwa_bd), 'lru_ba': dvec[0], 'lru_wx': _diag_blocks4(dwx_bd),
        'lru_bx': dvec[1], 'lru_lambda': dvec[2] * (-jax.nn.sigmoid(-lam)),
        'ssd_conv_b': acc_s[4],
        'ssd_dt_bias': hpg[0, :SSD_HEADS], 'ssd_a_log': hpg[1, :SSD_HEADS] * a_neg,
        'ssd_d': lpg[1].reshape(SSD_HEADS, SSD_P).sum(axis=-1), 'ssd_norm_w': lpg[0],
        'pre_norm2': vacc_mlp[0], 'post_norm2': vacc_mlp[1],
    }
    conv_full = [acc_l[:4], acc_s[:4]]
    nra = sum(_rows(int(np.prod(w[k].shape))) for k in REPL)
    nrc = sum(_rows(int(np.prod(v.shape))) for v in conv_full)
    rows_a = -(-(nra + nrc) // 8) * 8
    gslab = _pack([small[k] for k in REPL] + conv_full, rows_a)
    (g4,) = all_gather([gslab], "gather_small_grads")
    res_a = adamw(g4, _pack([w[k] for k in REPL], rows_a)[None], _pack([mom[k] for k in REPL], rows_a)[None],
                  _pack([var[k] for k in REPL], rows_a)[None], "adamw_small")
    for j, slab_j in enumerate(res_a):
        for k, val in zip(REPL, _unpack(slab_j[0], [w[k].shape for k in REPL])):
            res.setdefault(k, [None] * 4)[j] = val
    g_lru_cw, g_ssd_cw = _unpack(res_a[0][0, nra:], [v.shape for v in conv_full])
    lcw, scw = lru_conv_w.shape[2], ssd_conv_w.shape[2]
    sharded = {'b_ada': g_b_ada, 'lru_conv_w': lax.dynamic_slice(g_lru_cw, (0, me * lcw), (4, lcw)),
               'ssd_conv_w': lax.dynamic_slice(g_ssd_cw, (0, me * scw), (4, scw))}
    names_b = tuple(sharded)
    res_b = adamw(_pack([sharded[k] for k in names_b], 16)[None], _pack([w[k] for k in names_b], 16)[None],
                  _pack([mom[k] for k in names_b], 16)[None], _pack([var[k] for k in names_b], 16)[None],
                  "adamw_small_sharded")
    for j, slab_j in enumerate(res_b):
        for k, val in zip(names_b, _unpack(slab_j[0], [w[k].shape for k in names_b])):
            res.setdefault(k, [None] * 4)[j] = val

    loss = lax.psum(loss8[0, 0], ("x", "y", "c"))
    outs = [[res[k][j].reshape(w[k].shape) for k in WEIGHTS] for j in range(4)]
    return (loss, grad_x.reshape(x.shape), *outs[0], *outs[1], *outs[2], *outs[3])
```

```python
import functools

import numpy as np
import jax
import jax.numpy as jnp
from jax import lax
from jax.experimental import pallas as pl
from jax.experimental.pallas import tpu as pltpu

F32 = jnp.float32
BF16 = jnp.bfloat16

D_MODEL = 1024
LRU_HEADS = 16
LRU_BLOCK = 256
LRU_C = 8.0
SSD_INNER = 2048
SSD_HEADS = 32
SSD_P = 64
SSD_G = 8
SSD_N = 128
SSD_L = 128
SSD_GW = SSD_INNER // SSD_G
D_FF = 4096
N_MOD = 6
EPS = 1e-6
NDEV = 8

C_LRU_X, C_LRU_G, C_Z, C_XBC, C_GATES, PROJ_MAIN = 0, 1024, 2048, 4096, 8192, 10240
IN_DIM = 10272
DT_COL0 = 8192
HALO = 16

ADAM_LR, ADAM_B1, ADAM_B2, ADAM_EPS, ADAM_WD, ADAM_STEP = 0.001, 0.9, 0.999, 1e-08, 0.01, 10

VMEM_LIMIT = 60 * 1024 * 1024
MESH = pl.DeviceIdType.MESH
ANY = pl.BlockSpec(memory_space=pl.ANY)
VMEM_FULL = pl.BlockSpec(memory_space=pltpu.VMEM)


def _cp(*sem):
    return pltpu.CompilerParams(dimension_semantics=sem, vmem_limit_bytes=VMEM_LIMIT)


def _dot(a, b):
    return jnp.dot(a, b, preferred_element_type=F32)


def _dot_nt(a, b):
    return lax.dot_general(a, b, (((1,), (1,)), ((), ())), preferred_element_type=F32)


def _dot_tn(a, b):
    return lax.dot_general(a, b, (((0,), (0,)), ((), ())), preferred_element_type=F32)


def _dot_hi(a, b):
    return jnp.dot(a, b, precision=lax.Precision.HIGHEST, preferred_element_type=F32)


def _sigmoid(x):
    return 1.0 / (1.0 + jnp.exp(-x))


def _gelu_and_grad(x):
    k0, k1 = 0.7978845608028654, 0.044715
    t = jnp.tanh(k0 * (x + k1 * x * x * x))
    g = 0.5 * x * (1.0 + t)
    dg = 0.5 * (1.0 + t) + 0.5 * x * (1.0 - t * t) * k0 * (1.0 + 3.0 * k1 * x * x)
    return g, dg


def _neg_expm1(y):
    p = 1.0 + y * (1.0 / 7.0)
    p = 1.0 + y * (1.0 / 6.0) * p
    p = 1.0 + y * (1.0 / 5.0) * p
    p = 1.0 + y * (1.0 / 4.0) * p
    p = 1.0 + y * (1.0 / 3.0) * p
    p = 1.0 + y * 0.5 * p
    return jnp.where(y > -0.3, -y * p, 1.0 - jnp.exp(y))


def _colsum(v):
    return jnp.sum(v, axis=0, keepdims=True)


def _rowmean(v):
    return jnp.mean(v, axis=-1, keepdims=True)


def matmul(a, b, *, ta=False, tb=False, out_dtype=F32, tm, tn, tk, name, n=None, b_off=0):
    m = a.shape[1] if ta else a.shape[0]
    kdim = a.shape[0] if ta else a.shape[1]
    n = n or (b.shape[0] if tb else b.shape[1])
    tm, tn, tk = min(tm, m), min(tn, n), min(tk, kdim)
    nk = kdim // tk
    dn = (((0 if ta else 1,), (1 if tb else 0,)), ((), ()))

    def body(a_ref, b_ref, o_ref, acc_ref):
        k = pl.program_id(2)
        p = lax.dot_general(a_ref[...], b_ref[...], dn, preferred_element_type=F32)
        if nk == 1:
            o_ref[...] = p.astype(out_dtype)
        else:
            @pl.when(k == 0)
            def _():
                acc_ref[...] = p

            @pl.when(k > 0)
            def _():
                acc_ref[...] += p

            @pl.when(k == nk - 1)
            def _():
                o_ref[...] = acc_ref[...].astype(out_dtype)

    a_spec = pl.BlockSpec((tk, tm), lambda i, j, k: (k, i)) if ta else pl.BlockSpec((tm, tk), lambda i, j, k: (i, k))
    b_spec = (pl.BlockSpec((tn, tk), lambda i, j, k: (j, k)) if tb
              else pl.BlockSpec((tk, tn), lambda i, j, k: (k, j + b_off)))
    return pl.pallas_call(
        body, name=name, grid=(m // tm, n // tn, nk),
        in_specs=[a_spec, b_spec], out_specs=pl.BlockSpec((tm, tn), lambda i, j, k: (i, j)),
        out_shape=jax.ShapeDtypeStruct((m, n), out_dtype),
        scratch_shapes=[pltpu.VMEM((tm, tn), F32)],
        compiler_params=_cp("parallel", "parallel", "arbitrary"),
    )(a, b)


def in_proj_fwd(x2, mod8, pre1, w_main, w_dt, seq, side=None):
    t = x2.shape[0]
    tm = min(1024, seq)
    tn = 2048
    per_seq = seq // tm

    def body(x_ref, mod_ref, pre_ref, w_ref, wdt_ref, proj_ref, h_ref, dt_ref, h_scr):
        @pl.when(pl.program_id(1) == 0)
        def _():
            xv = x_ref[...]
            y = xv * lax.rsqrt(_rowmean(xv * xv) + EPS) * pre_ref[...]
            m = mod_ref[0]
            hf = y * (1.0 + m[1:2, :]) + m[0:1, :]
            h = hf.astype(BF16)
            h_scr[...] = h
            h_ref[...] = hf.T.astype(BF16)
            dt_ref[...] = _dot(h, wdt_ref[...])

        proj_ref[...] = _dot(h_scr[...], w_ref[...]).astype(BF16)

    return _call(
        body, name="in_proj_fwd", grid=(t // tm, PROJ_MAIN // tn), side=side, sem=("parallel", "arbitrary"),
        args=(x2, mod8, pre1, w_main, w_dt),
        in_specs=[pl.BlockSpec((tm, D_MODEL), lambda i, j: (i, 0)),
                  pl.BlockSpec((1, 8, D_MODEL), lambda i, j: (i // per_seq, 0, 0)),
                  pl.BlockSpec((1, D_MODEL), lambda i, j: (0, 0)),
                  pl.BlockSpec((D_MODEL, tn), lambda i, j: (0, j)),
                  pl.BlockSpec((D_MODEL, 128), lambda i, j: (0, 0))],
        out_specs=[pl.BlockSpec((tm, tn), lambda i, j: (i, j)),
                   pl.BlockSpec((None, D_MODEL, tm), lambda i, j: (i, 0, 0)),
                   pl.BlockSpec((tm, 128), lambda i, j: (i, 0))],
        out_shape=[jax.ShapeDtypeStruct((t, PROJ_MAIN), BF16), jax.ShapeDtypeStruct((t // tm, D_MODEL, tm), BF16),
                   jax.ShapeDtypeStruct((t, 128), F32)],
        scratch_shapes=[pltpu.VMEM((tm, D_MODEL), BF16)])


def conv_fwd(src, col0, width, w4, bias, nb, seq, act, name):
    t = src.shape[0]
    tt = min(512, seq)
    tc = 512
    ns = seq // tt
    cb0 = col0 // tc

    def body(cur_ref, prev_ref, w_ref, b_ref, o_ref):
        s = pl.program_id(1)
        cur = cur_ref[...].astype(F32)
        prev = jnp.where(s == 0, 0.0, prev_ref[...].astype(F32))
        xx = jnp.concatenate([prev, cur], axis=0)
        w = w_ref[...]
        acc = cur * w[3:4, :] + b_ref[...]
        for d in (1, 2, 3):
            acc = acc + pltpu.roll(xx, d, axis=0)[HALO:, :] * w[3 - d:4 - d, :]
        if act:
            acc = acc * _sigmoid(acc)
        o_ref[...] = acc.astype(BF16)

    return pl.pallas_call(
        body, name=name, grid=(nb, ns, width // tc),
        in_specs=[pl.BlockSpec((tt, tc), lambda b, s, j: (b * ns + s, cb0 + j)),
                  pl.BlockSpec((HALO, tc), lambda b, s, j: (jnp.maximum((b * seq + s * tt) // HALO - 1, 0), cb0 + j)),
                  pl.BlockSpec((4, tc), lambda b, s, j: (0, j)),
                  pl.BlockSpec((1, tc), lambda b, s, j: (0, j))],
        out_specs=pl.BlockSpec((tt, tc), lambda b, s, j: (b * ns + s, j)),
        out_shape=jax.ShapeDtypeStruct((t, width), BF16),
        compiler_params=_cp("parallel", "parallel", "parallel"),
    )(src, src, w4, bias)


def conv_bwd(src, col0, width, w4, bias, dout, nb, seq, act, name, side=None):
    t = src.shape[0]
    tt = min(512, seq)
    tc = 512
    ns = seq // tt
    cb0 = col0 // tc
    nh = t // HALO

    def body(cur_ref, prev_ref, next_ref, w_ref, b_ref, do_ref, don_ref, dx_ref, acc_ref):
        b, s = pl.program_id(1), pl.program_id(2)

        @pl.when((b == 0) & (s == 0))
        def _():
            acc_ref[...] = jnp.zeros_like(acc_ref)

        cur = cur_ref[...].astype(F32)
        prev = jnp.where(s == 0, 0.0, prev_ref[...].astype(F32))
        nxt = next_ref[...].astype(F32)
        xx = jnp.concatenate([prev, cur, nxt], axis=0)
        w = w_ref[...]
        do_ext = jnp.concatenate([do_ref[...].astype(F32),
                                  jnp.where(s == ns - 1, 0.0, don_ref[...].astype(F32))], axis=0)
        ne = tt + HALO
        xs = [xx[HALO:HALO + ne, :]] + [pltpu.roll(xx, d, axis=0)[HALO:HALO + ne, :] for d in (1, 2, 3)]
        if act:
            c = b_ref[...] + xs[0] * w[3:4, :] + xs[1] * w[2:3, :] + xs[2] * w[1:2, :] + xs[3] * w[0:1, :]
            sg = _sigmoid(c)
            dc = do_ext * (sg * (1.0 + c * (1.0 - sg)))
        else:
            dc = do_ext
        dx = dc[:tt, :] * w[3:4, :]
        for d in (1, 2, 3):
            dx = dx + pltpu.roll(dc, ne - d, axis=0)[:tt, :] * w[3 - d:4 - d, :]
        dx_ref[...] = dx.astype(BF16)
        dcc = dc[:tt, :]
        rows = [_colsum(dcc * xs[3 - r][:tt, :]) for r in range(4)] + [_colsum(dcc)]
        acc_ref[...] += jnp.concatenate(rows + [jnp.zeros((3, tc), F32)], axis=0)

    return _call(
        body, name=name, grid=(width // tc, nb, ns), side=side, sem=("parallel", "arbitrary", "arbitrary"),
        args=(src, src, src, w4, bias, dout, dout),
        in_specs=[pl.BlockSpec((tt, tc), lambda j, b, s: (b * ns + s, cb0 + j)),
                  pl.BlockSpec((HALO, tc), lambda j, b, s: (jnp.maximum((b * seq + s * tt) // HALO - 1, 0), cb0 + j)),
                  pl.BlockSpec((HALO, tc), lambda j, b, s: (jnp.minimum((b * seq + (s + 1) * tt) // HALO, nh - 1), cb0 + j)),
                  pl.BlockSpec((4, tc), lambda j, b, s: (0, j)),
                  pl.BlockSpec((1, tc), lambda j, b, s: (0, j)),
                  pl.BlockSpec((tt, tc), lambda j, b, s: (b * ns + s, j)),
                  pl.BlockSpec((HALO, tc), lambda j, b, s: (jnp.minimum((b * seq + (s + 1) * tt) // HALO, nh - 1), j))],
        out_specs=[pl.BlockSpec((tt, tc), lambda j, b, s: (b * ns + s, j)),
                   pl.BlockSpec((8, tc), lambda j, b, s: (0, j))],
        out_shape=[jax.ShapeDtypeStruct((t, width), BF16), jax.ShapeDtypeStruct((8, width), F32)])


def _lru_gates(xa, wa_ref, wx_ref, ba, bx, sp):
    nblk = D_MODEL // LRU_BLOCK
    pr = jnp.concatenate([_dot(xa[:, j * LRU_BLOCK:(j + 1) * LRU_BLOCK], wa_ref[j]) for j in range(nblk)], axis=1) + ba
    pi = jnp.concatenate([_dot(xa[:, j * LRU_BLOCK:(j + 1) * LRU_BLOCK], wx_ref[j]) for j in range(nblk)], axis=1) + bx
    r = _sigmoid(pr)
    i = _sigmoid(pi)
    log_a = (-LRU_C * r) * sp
    return r, i, jnp.exp(log_a), _neg_expm1(2.0 * log_a)


def lru_fwd(xa, proj, wa_bd, wx_bd, vec, nb, seq):
    t = xa.shape[0]
    tc = min(512, seq)
    nk = seq // tc
    gb = C_LRU_G // D_MODEL

    def body(xa_ref, g_ref, wa_ref, wx_ref, vec_ref, ya_ref, h_ref, a_scr, u_scr, hc_scr):
        @pl.when(pl.program_id(1) == 0)
        def _():
            hc_scr[...] = jnp.zeros_like(hc_scr)

        xa_v = xa_ref[...]
        v = vec_ref[...]
        r, i, a, e = _lru_gates(xa_v, wa_ref, wx_ref, v[0:1, :], v[1:2, :], v[2:3, :])
        a_scr[...] = a
        u_scr[...] = jnp.sqrt(e) * (i * xa_v.astype(F32))
        row = lax.broadcasted_iota(jnp.int32, (8, 1), 0)

        def tile(j, h):
            r0 = pl.multiple_of(j * 8, 8)
            av, uv = a_scr[pl.ds(r0, 8), :], u_scr[pl.ds(r0, 8), :]
            for d in (1, 2, 4):
                uv = uv + av * jnp.where(row >= d, pltpu.roll(uv, d, axis=0), 0.0)
                av = av * jnp.where(row >= d, pltpu.roll(av, d, axis=0), 1.0)
            hv = uv + av * h
            h_ref[pl.ds(r0, 8), :] = hv
            return hv[7:8, :]

        hc_scr[...] = lax.fori_loop(0, tc // 8, tile, hc_scr[...], unroll=2)
        gel, _ = _gelu_and_grad(g_ref[...].astype(F32))
        ya_ref[...] = (h_ref[...] * gel).astype(BF16)

    return pl.pallas_call(
        body, name="lru_fwd", grid=(nb, nk),
        in_specs=[pl.BlockSpec((tc, D_MODEL), lambda b, k: (b * nk + k, 0)),
                  pl.BlockSpec((tc, D_MODEL), lambda b, k: (b * nk + k, gb)),
                  VMEM_FULL, VMEM_FULL, VMEM_FULL],
        out_specs=[pl.BlockSpec((tc, D_MODEL), lambda b, k: (b * nk + k, 0)),
                   pl.BlockSpec((tc, D_MODEL), lambda b, k: (b * nk + k, 0))],
        out_shape=[jax.ShapeDtypeStruct((t, D_MODEL), BF16), jax.ShapeDtypeStruct((t, D_MODEL), F32)],
        scratch_shapes=[pltpu.VMEM((tc, D_MODEL), F32), pltpu.VMEM((tc, D_MODEL), F32), pltpu.VMEM((1, D_MODEL), F32)],
        compiler_params=_cp("arbitrary", "arbitrary"),
    )(xa, proj, wa_bd, wx_bd, vec)


def lru_bwd(dya, xa, proj, h, wa_bd, wx_bd, vec, nb, seq, side=None):
    t = xa.shape[0]
    tc = min(512, seq)
    nk = seq // tc
    gb = C_LRU_G // D_MODEL
    nblk = D_MODEL // LRU_BLOCK

    def chunk(b, k):
        return b * nk + (nk - 1 - k)

    def body(dya_ref, xa_ref, g_ref, h_ref, hp_ref, wa_ref, wx_ref, vec_ref,
             dxa_ref, dg_ref, dwa_ref, dwx_ref, dvec_ref, a_scr, dh_scr, c_scr):
        b, k = pl.program_id(0), pl.program_id(1)

        @pl.when((b == 0) & (k == 0))
        def _():
            dwa_ref[...] = jnp.zeros_like(dwa_ref)
            dwx_ref[...] = jnp.zeros_like(dwx_ref)
            dvec_ref[...] = jnp.zeros_like(dvec_ref)

        @pl.when(k == 0)
        def _():
            c_scr[...] = jnp.zeros_like(c_scr)

        xa_v = xa_ref[...]
        xaf = xa_v.astype(F32)
        v = vec_ref[...]
        sp = v[2:3, :]
        r, i, a, e = _lru_gates(xa_v, wa_ref, wx_ref, v[0:1, :], v[1:2, :], sp)
        gel, dgel = _gelu_and_grad(g_ref[...].astype(F32))
        hv = h_ref[...]
        dyv = dya_ref[...].astype(F32)
        dg_ref[...] = (dyv * hv * dgel).astype(BF16)
        a_scr[...] = a
        dh_scr[...] = dyv * gel

        row8 = lax.broadcasted_iota(jnp.int32, (8, 1), 0)

        def tile(j, c):
            r0 = pl.multiple_of((tc // 8 - 1 - j) * 8, 8)
            av, dout = a_scr[pl.ds(r0, 8), :], dh_scr[pl.ds(r0, 8), :]
            zv = av * dout
            for d in (1, 2, 4):
                zv = zv + av * jnp.where(row8 < 8 - d, pltpu.roll(zv, 8 - d, axis=0), 0.0)
                av = av * jnp.where(row8 < 8 - d, pltpu.roll(av, 8 - d, axis=0), 1.0)
            zv = zv + av * c
            dh_scr[pl.ds(r0, 8), :] = dout + jnp.where(row8 < 7, pltpu.roll(zv, 7, axis=0), c)
            return zv[0:1, :]

        c_scr[...] = lax.fori_loop(0, tc // 8, tile, c_scr[...], unroll=2)
        dh = dh_scr[...]
        h_last = jnp.where(k == nk - 1, 0.0, hp_ref[HALO // 2 - 1:HALO // 2, :])
        row = lax.broadcasted_iota(jnp.int32, (tc, 1), 0)
        h_prev = jnp.where(row == 0, h_last, pltpu.roll(hv, 1, axis=0))
        s = jnp.sqrt(e)
        da = dh * h_prev
        ix = i * xaf
        dlog_a = da * a - (dh * ix) * (a * a) * lax.rsqrt(jnp.maximum(e, 1e-30))
        di = dh * s * xaf
        dpr = (dlog_a * (-LRU_C * sp)) * (r * (1.0 - r))
        dpi = di * (i * (1.0 - i))
        dprb, dpib = dpr.astype(BF16), dpi.astype(BF16)
        dxa = dh * s * i
        dxa = dxa + jnp.concatenate(
            [_dot_nt(dprb[:, j * LRU_BLOCK:(j + 1) * LRU_BLOCK], wa_ref[j])
             + _dot_nt(dpib[:, j * LRU_BLOCK:(j + 1) * LRU_BLOCK], wx_ref[j]) for j in range(nblk)], axis=1)
        dxa_ref[...] = dxa.astype(BF16)
        for j in range(nblk):
            sl = slice(j * LRU_BLOCK, (j + 1) * LRU_BLOCK)
            dwa_ref[j] += _dot_tn(xa_v[:, sl], dprb[:, sl])
            dwx_ref[j] += _dot_tn(xa_v[:, sl], dpib[:, sl])
        dvec_ref[...] += jnp.concatenate(
            [_colsum(dpr), _colsum(dpi), _colsum(dlog_a * (-LRU_C * r)), jnp.zeros((5, D_MODEL), F32)], axis=0)

    hh = HALO // 2
    return _call(
        body, name="lru_bwd", grid=(nb, nk), side=side, sem=("arbitrary", "arbitrary"),
        args=(dya, xa, proj, h, h, wa_bd, wx_bd, vec),
        in_specs=[pl.BlockSpec((tc, D_MODEL), lambda b, k: (chunk(b, k), 0)),
                  pl.BlockSpec((tc, D_MODEL), lambda b, k: (chunk(b, k), 0)),
                  pl.BlockSpec((tc, D_MODEL), lambda b, k: (chunk(b, k), gb)),
                  pl.BlockSpec((tc, D_MODEL), lambda b, k: (chunk(b, k), 0)),
                  pl.BlockSpec((hh, D_MODEL), lambda b, k: (jnp.maximum(chunk(b, k) * (tc // hh) - 1, 0), 0)),
                  VMEM_FULL, VMEM_FULL, VMEM_FULL],
        out_specs=[pl.BlockSpec((tc, D_MODEL), lambda b, k: (chunk(b, k), 0)),
                   pl.BlockSpec((tc, D_MODEL), lambda b, k: (chunk(b, k), 0)),
                   pl.BlockSpec((nblk, LRU_BLOCK, LRU_BLOCK), lambda b, k: (0, 0, 0)),
                   pl.BlockSpec((nblk, LRU_BLOCK, LRU_BLOCK), lambda b, k: (0, 0, 0)),
                   pl.BlockSpec((8, D_MODEL), lambda b, k: (0, 0))],
        out_shape=[jax.ShapeDtypeStruct((t, D_MODEL), BF16), jax.ShapeDtypeStruct((t, D_MODEL), BF16),
                   jax.ShapeDtypeStruct((nblk, LRU_BLOCK, LRU_BLOCK), F32),
                   jax.ShapeDtypeStruct((nblk, LRU_BLOCK, LRU_BLOCK), F32),
                   jax.ShapeDtypeStruct((8, D_MODEL), F32)],
        scratch_shapes=[pltpu.VMEM((tc, D_MODEL), F32), pltpu.VMEM((tc, D_MODEL), F32), pltpu.VMEM((1, D_MODEL), F32)])


def merge_fwd(ya_in, yb_in, proj, x2, mod8, bgate, post1, w_pa, w_pb, w_out, seq):
    t = x2.shape[0]
    tm = min(512, seq)
    per_seq = seq // tm
    gcb = C_GATES // SSD_INNER

    def body(ya_ref, yb_ref, gt_ref, x_ref, mod_ref, bg_ref, post_ref, wpa_ref, wpb_ref, wo_ref,
             yab_ref, out1_ref, x1_ref):
        y_a = _dot(ya_ref[...], wpa_ref[...])
        y_b = _dot(yb_ref[...], wpb_ref[...])
        g = _sigmoid(gt_ref[...].astype(F32) + bg_ref[...])
        merged = g[:, :D_MODEL] * y_a + g[:, D_MODEL:] * y_b
        out1 = _dot(merged.astype(BF16), wo_ref[...])
        n = out1 * lax.rsqrt(_rowmean(out1 * out1) + EPS)
        yab_ref[...] = jnp.concatenate([y_a, y_b], axis=1).astype(BF16)
        out1_ref[...] = out1
        x1_ref[...] = x_ref[...] + mod_ref[0][2:3, :] * (n * post_ref[...])

    row = lambda w: pl.BlockSpec((tm, w), lambda i: (i, 0))
    return pl.pallas_call(
        body, name="merge_fwd", grid=(t // tm,),
        in_specs=[row(D_MODEL), row(SSD_INNER), pl.BlockSpec((tm, SSD_INNER), lambda i: (i, gcb)), row(D_MODEL),
                  pl.BlockSpec((1, 8, D_MODEL), lambda i: (i // per_seq, 0, 0)),
                  VMEM_FULL, VMEM_FULL, VMEM_FULL, VMEM_FULL, VMEM_FULL],
        out_specs=[row(SSD_INNER), row(D_MODEL), row(D_MODEL)],
        out_shape=[jax.ShapeDtypeStruct((t, SSD_INNER), BF16), jax.ShapeDtypeStruct((t, D_MODEL), F32),
                   jax.ShapeDtypeStruct((t, D_MODEL), F32)],
        compiler_params=_cp("parallel"),
    )(ya_in, yb_in, proj, x2, mod8, bgate, post1, w_pa, w_pb, w_out)


def merge_bwd(dx1, out1, yab, proj, mod8, bgate, post1, w_pa, w_pb, w_out, nb, seq):
    t = dx1.shape[0]
    tm = min(512, seq)
    per_seq = seq // tm
    gcb = C_GATES // SSD_INNER

    def body(dx1_ref, out1_ref, yab_ref, gt_ref, mod_ref, bg_ref, post_ref, wpa_ref, wpb_ref, wo_ref,
             dya_ref, dyb_ref, dgt_ref, dyab_ref, dout1_ref, mg_ref, vacc_ref, dmod_ref):
        b, s = pl.program_id(0), pl.program_id(1)

        @pl.when((b == 0) & (s == 0))
        def _():
            vacc_ref[...] = jnp.zeros_like(vacc_ref)

        @pl.when(s == 0)
        def _():
            dmod_ref[...] = jnp.zeros_like(dmod_ref)

        dx1v = dx1_ref[...]
        out1 = out1_ref[...]
        post = post_ref[...]
        rs = lax.rsqrt(_rowmean(out1 * out1) + EPS)
        n = out1 * rs
        do = dx1v * mod_ref[0][2:3, :]
        dn = do * post
        dout1 = rs * (dn - n * _rowmean(dn * n))
        dout1b = dout1.astype(BF16)
        dout1_ref[...] = dout1b
        dmerged = _dot_nt(dout1b, wo_ref[...])
        g = _sigmoid(gt_ref[...].astype(F32) + bg_ref[...])
        yab_v = yab_ref[...].astype(F32)
        gy = g * yab_v
        mg_ref[...] = (gy[:, :D_MODEL] + gy[:, D_MODEL:]).astype(BF16)
        dm2 = jnp.concatenate([dmerged, dmerged], axis=1)
        dyab = (dm2 * g).astype(BF16)
        dyab_ref[...] = dyab
        dgt = dm2 * gy * (1.0 - g)
        dgt_ref[...] = dgt.astype(BF16)
        dya_ref[...] = _dot_nt(dyab[:, :D_MODEL], wpa_ref[...]).astype(BF16)
        dyb_ref[...] = _dot_nt(dyab[:, D_MODEL:], wpb_ref[...]).astype(BF16)
        vacc_ref[...] += jnp.concatenate(
            [_colsum(dgt), jnp.concatenate([_colsum(do * n), jnp.zeros((1, D_MODEL), F32)], axis=1),
             jnp.zeros((6, SSD_INNER), F32)], axis=0)
        dmod_ref[0] += jnp.concatenate(
            [jnp.zeros((2, D_MODEL), F32), _colsum(dx1v * (n * post)), jnp.zeros((5, D_MODEL), F32)], axis=0)

    row = lambda w: pl.BlockSpec((tm, w), lambda b, s: (b * per_seq + s, 0))
    return pl.pallas_call(
        body, name="merge_bwd", grid=(nb, per_seq),
        in_specs=[row(D_MODEL), row(D_MODEL), row(SSD_INNER),
                  pl.BlockSpec((tm, SSD_INNER), lambda b, s: (b * per_seq + s, gcb)),
                  pl.BlockSpec((1, 8, D_MODEL), lambda b, s: (b, 0, 0)),
                  VMEM_FULL, VMEM_FULL, VMEM_FULL, VMEM_FULL, VMEM_FULL],
        out_specs=[row(D_MODEL), row(SSD_INNER), row(SSD_INNER), row(SSD_INNER), row(D_MODEL), row(D_MODEL),
                   pl.BlockSpec((8, SSD_INNER), lambda b, s: (0, 0)),
                   pl.BlockSpec((1, 8, D_MODEL), lambda b, s: (b, 0, 0))],
        out_shape=[jax.ShapeDtypeStruct((t, D_MODEL), BF16), jax.ShapeDtypeStruct((t, SSD_INNER), BF16),
                   jax.ShapeDtypeStruct((t, SSD_INNER), BF16), jax.ShapeDtypeStruct((t, SSD_INNER), BF16),
                   jax.ShapeDtypeStruct((t, D_MODEL), BF16), jax.ShapeDtypeStruct((t, D_MODEL), BF16),
                   jax.ShapeDtypeStruct((8, SSD_INNER), F32), jax.ShapeDtypeStruct((nb, 8, D_MODEL), F32)],
        compiler_params=_cp("arbitrary", "arbitrary"),
    )(dx1, out1, yab, proj, mod8, bgate, post1, w_pa, w_pb, w_out)


def mlp_fwd_bwd(x1, tgt, mod8, pre2, post2, w_ff1, w_ff2, nb, seq):
    t = x1.shape[0]
    tm = min(256, seq)
    per_seq = seq // tm
    fc = 1024
    nfc = D_FF // fc

    def body(x1_ref, tgt_ref, mod_ref, pre_ref, post_ref, w1_ref, w2_ref,
             dx1_ref, h2_ref, da1_ref, act_ref, dy2_ref, loss_ref, vacc_ref, dmod_ref, r_scr):
        b, s = pl.program_id(0), pl.program_id(1)

        @pl.when((b == 0) & (s == 0))
        def _():
            vacc_ref[...] = jnp.zeros_like(vacc_ref)
            loss_ref[...] = jnp.zeros_like(loss_ref)

        @pl.when(s == 0)
        def _():
            dmod_ref[...] = jnp.zeros_like(dmod_ref)

        m = mod_ref[0]
        sh2, sc2, g2 = m[3:4, :], m[4:5, :], m[5:6, :]
        pre, post = pre_ref[...], post_ref[...]
        x1v = x1_ref[...]
        rs1 = lax.rsqrt(_rowmean(x1v * x1v) + EPS)
        n1 = x1v * rs1
        y1 = n1 * pre
        h2b = (y1 * (1.0 + sc2) + sh2).astype(BF16)
        h2_ref[...] = h2b
        y2 = jnp.zeros((tm, D_MODEL), F32)
        for c in range(nfc):
            r = jnp.maximum(_dot(h2b, w1_ref[:, c * fc:(c + 1) * fc]), 0.0)
            r_scr[:, c * fc:(c + 1) * fc] = r
            a = (r * r).astype(BF16)
            act_ref[:, c * fc:(c + 1) * fc] = a
            y2 = y2 + _dot(a, w2_ref[c * fc:(c + 1) * fc, :])
        rs2 = lax.rsqrt(_rowmean(y2 * y2) + EPS)
        n2 = y2 * rs2
        o2 = n2 * post
        diff = x1v + g2 * o2 - tgt_ref[...]
        loss_ref[...] += 0.5 * jnp.sum(_rowmean(diff * diff))
        dx2 = diff * (1.0 / D_MODEL)
        do2 = dx2 * g2
        dn2 = do2 * post
        dy2b = (rs2 * (dn2 - n2 * _rowmean(dn2 * n2))).astype(BF16)
        dy2_ref[...] = dy2b
        dh2 = jnp.zeros((tm, D_MODEL), F32)
        for c in range(nfc):
            dact = _dot_nt(dy2b, w2_ref[c * fc:(c + 1) * fc, :])
            da = (dact * (2.0 * r_scr[:, c * fc:(c + 1) * fc])).astype(BF16)
            da1_ref[:, c * fc:(c + 1) * fc] = da
            dh2 = dh2 + _dot_nt(da, w1_ref[:, c * fc:(c + 1) * fc])
        dy1 = dh2 * (1.0 + sc2)
        dn1 = dy1 * pre
        dx1_ref[...] = dx2 + rs1 * (dn1 - n1 * _rowmean(dn1 * n1))
        vacc_ref[...] += jnp.concatenate([_colsum(dy1 * n1), _colsum(do2 * n2), jnp.zeros((6, D_MODEL), F32)], axis=0)
        dmod_ref[0] += jnp.concatenate(
            [jnp.zeros((3, D_MODEL), F32), _colsum(dh2), _colsum(dh2 * y1), _colsum(dx2 * o2),
             jnp.zeros((2, D_MODEL), F32)], axis=0)

    row = lambda w: pl.BlockSpec((tm, w), lambda b, s: (b * per_seq + s, 0))
    return pl.pallas_call(
        body, name="mlp_fwd_bwd", grid=(nb, per_seq),
        in_specs=[row(D_MODEL), row(D_MODEL), pl.BlockSpec((1, 8, D_MODEL), lambda b, s: (b, 0, 0)),
                  VMEM_FULL, VMEM_FULL, VMEM_FULL, VMEM_FULL],
        out_specs=[row(D_MODEL), row(D_MODEL), row(D_FF), row(D_FF), row(D_MODEL),
                   pl.BlockSpec((8, 128), lambda b, s: (0, 0)),
                   pl.BlockSpec((8, D_MODEL), lambda b, s: (0, 0)),
                   pl.BlockSpec((1, 8, D_MODEL), lambda b, s: (b, 0, 0))],
        out_shape=[jax.ShapeDtypeStruct((t, D_MODEL), F32), jax.ShapeDtypeStruct((t, D_MODEL), BF16),
                   jax.ShapeDtypeStruct((t, D_FF), BF16), jax.ShapeDtypeStruct((t, D_FF), BF16),
                   jax.ShapeDtypeStruct((t, D_MODEL), BF16), jax.ShapeDtypeStruct((8, 128), F32),
                   jax.ShapeDtypeStruct((8, D_MODEL), F32), jax.ShapeDtypeStruct((nb, 8, D_MODEL), F32)],
        scratch_shapes=[pltpu.VMEM((tm, D_FF), F32)],
        compiler_params=_cp("arbitrary", "arbitrary"),
    )(x1, tgt, mod8, pre2, post2, w_ff1, w_ff2)


_PIECES = ((C_LRU_X, 1024), (C_LRU_G, 1024), (C_Z, 2048), (C_XBC, 4096), (C_GATES, 2048))
_NP = len(_PIECES)


def _piece_of(k, tk):
    col = k * tk
    for p, (c0, w) in enumerate(_PIECES):
        if c0 <= col < c0 + w:
            return p, (col - c0) // tk
    raise ValueError(col)


def in_proj_bwd(pieces, ddt, dx1, x2, mod8, pre1, w_main, w_dt, nb, seq, side=None):
    t = x2.shape[0]
    tm = min(512, seq)
    per_seq = seq // tm
    tk = 1024
    nk = PROJ_MAIN // tk
    where = [_piece_of(k, tk) for k in range(nk)]

    def piece_spec(p):
        first = min(k for k in range(nk) if where[k][0] == p)
        nblk = _PIECES[p][1] // tk
        return pl.BlockSpec((tm, tk), lambda b, s, k: (b * per_seq + s, jnp.clip(k - first, 0, nblk - 1)))

    def body(*refs):
        prefs = refs[:_NP]
        ddt_ref, dx1_ref, x_ref, mod_ref, pre_ref, w_ref, wdt_ref, gx_ref, vacc_ref, dmod_ref, acc_ref = refs[_NP:]
        b, s, k = pl.program_id(0), pl.program_id(1), pl.program_id(2)

        @pl.when((b == 0) & (s == 0) & (k == 0))
        def _():
            vacc_ref[...] = jnp.zeros_like(vacc_ref)

        @pl.when((s == 0) & (k == 0))
        def _():
            dmod_ref[...] = jnp.zeros_like(dmod_ref)

        @pl.when(k == 0)
        def _():
            acc_ref[...] = _dot_nt(ddt_ref[...], wdt_ref[...])

        for kk in range(nk):
            @pl.when(k == kk)
            def _(kk=kk):
                acc_ref[...] += _dot_nt(prefs[where[kk][0]][...], w_ref[:, kk * tk:(kk + 1) * tk])

        @pl.when(k == nk - 1)
        def _():
            dh = acc_ref[...]
            m = mod_ref[0]
            pre = pre_ref[...]
            xv = x_ref[...]
            rs = lax.rsqrt(_rowmean(xv * xv) + EPS)
            n = xv * rs
            dy = dh * (1.0 + m[1:2, :])
            dn = dy * pre
            gx_ref[...] = dx1_ref[...] + rs * (dn - n * _rowmean(dn * n))
            vacc_ref[...] += jnp.concatenate([_colsum(dy * n), jnp.zeros((7, D_MODEL), F32)], axis=0)
            dmod_ref[0] += jnp.concatenate([_colsum(dh), _colsum(dh * (n * pre)), jnp.zeros((6, D_MODEL), F32)], axis=0)

    row = lambda w: pl.BlockSpec((tm, w), lambda b, s, k: (b * per_seq + s, 0))
    return _call(
        body, name="in_proj_bwd", grid=(nb, per_seq, nk), side=side, sem=("arbitrary", "arbitrary", "arbitrary"),
        args=(*pieces, ddt, dx1, x2, mod8, pre1, w_main, w_dt),
        in_specs=[piece_spec(p) for p in range(_NP)] + [
            row(128), row(D_MODEL), row(D_MODEL), pl.BlockSpec((1, 8, D_MODEL), lambda b, s, k: (b, 0, 0)),
            pl.BlockSpec((1, D_MODEL), lambda b, s, k: (0, 0)),
            VMEM_FULL,
            pl.BlockSpec((D_MODEL, 128), lambda b, s, k: (0, 0))],
        out_specs=[row(D_MODEL), pl.BlockSpec((8, D_MODEL), lambda b, s, k: (0, 0)),
                   pl.BlockSpec((1, 8, D_MODEL), lambda b, s, k: (b, 0, 0))],
        out_shape=[jax.ShapeDtypeStruct((t, D_MODEL), F32), jax.ShapeDtypeStruct((8, D_MODEL), F32),
                   jax.ShapeDtypeStruct((nb, 8, D_MODEL), F32)],
        scratch_shapes=[pltpu.VMEM((tm, D_MODEL), F32)])


def in_proj_wgrad(h1t, pieces, ddt, name="in_proj_wgrad"):
    nt, _, tt = h1t.shape
    tn = 1024
    nn = PROJ_MAIN // tn
    where = [_piece_of(n, tn) for n in range(nn)]

    def piece_spec(p):
        first = min(n for n in range(nn) if where[n][0] == p)
        nblk = _PIECES[p][1] // tn
        return pl.BlockSpec((tt, tn), lambda n, k: (jnp.where((n >= first) & (n < first + nblk), k, 0),
                                                    jnp.clip(n - first, 0, nblk - 1)))

    def body(h_ref, *refs):
        prefs = refs[:_NP]
        ddt_ref, dw_ref, dwdt_ref, acc_ref, accdt_ref = refs[_NP:]
        n, k = pl.program_id(0), pl.program_id(1)
        hv = h_ref[k]
        for nn_ in range(nn):
            @pl.when(n == nn_)
            def _(nn_=nn_):
                p = _dot(hv, prefs[where[nn_][0]][...])

                @pl.when(k == 0)
                def _():
                    acc_ref[...] = p

                @pl.when(k > 0)
                def _():
                    acc_ref[...] += p

        @pl.when(n == 0)
        def _():
            p = _dot(hv, ddt_ref[...])

            @pl.when(k == 0)
            def _():
                accdt_ref[...] = p

            @pl.when(k > 0)
            def _():
                accdt_ref[...] += p

        @pl.when(k == nt - 1)
        def _():
            dw_ref[...] = acc_ref[...].astype(BF16)

        @pl.when((n == 0) & (k == nt - 1))
        def _():
            dwdt_ref[...] = accdt_ref[...].astype(BF16)

    return pl.pallas_call(
        body, name=name, grid=(nn, nt),
        in_specs=[VMEM_FULL] + [piece_spec(p) for p in range(_NP)]
        + [pl.BlockSpec((tt, 128), lambda n, k: (k, 0))],
        out_specs=[pl.BlockSpec((D_MODEL, tn), lambda n, k: (0, n)), pl.BlockSpec((D_MODEL, 128), lambda n, k: (0, 0))],
        out_shape=[jax.ShapeDtypeStruct((D_MODEL, PROJ_MAIN), BF16), jax.ShapeDtypeStruct((D_MODEL, 128), BF16)],
        scratch_shapes=[pltpu.VMEM((D_MODEL, tn), F32), pltpu.VMEM((D_MODEL, 128), F32)],
        compiler_params=_cp("arbitrary", "arbitrary"),
    )(h1t, *pieces, ddt)


def _log1p(u):
    w = 1.0 + u
    return jnp.log(w) - ((w - 1.0) - u) / w


def _softplus(x):
    return jnp.maximum(x, 0.0) + _log1p(jnp.exp(-jnp.abs(x)))


def _head_mask(h):
    lane = lax.broadcasted_iota(jnp.int32, (1, SSD_GW), 1)
    return (lane >= SSD_P * h) & (lane < SSD_P * (h + 1))


def _pair(p):
    return slice(2 * SSD_P * p, 2 * SSD_P * (p + 1))


def _expand4(m, g):
    lane = lax.broadcasted_iota(jnp.int32, (1, SSD_GW), 1)
    col = lambda h: m[:, 4 * g + h:4 * g + h + 1]
    return jnp.where(lane < SSD_P, col(0), jnp.where(lane < 2 * SSD_P, col(1), jnp.where(lane < 3 * SSD_P, col(2), col(3))))


def _reduce4(v, g):
    lane = lax.broadcasted_iota(jnp.int32, (1, SSD_N), 1)
    out = jnp.zeros((v.shape[0], SSD_N), F32)
    for h in range(4):
        s = jnp.sum(jnp.where(_head_mask(h), v, 0.0), axis=1, keepdims=True)
        out = out + jnp.where(lane == 4 * g + h, s, 0.0)
    return out


def _ssd_heads(dtraw, hp, tri):
    xdt = dtraw + hp[0:1, :]
    dt = _softplus(xdt)
    cs = _dot_hi(tri, dt * hp[1:2, :])
    cs_last = cs[SSD_L - 1:SSD_L, :]
    return dict(xdt=xdt, dt=dt, cs=cs, cs_t=cs.T, e=jnp.exp(cs), w=jnp.exp(cs_last - cs), el=jnp.exp(cs_last))


def _ssd_group(g, hd, xs_b, bm_b, cm_b, d_x, st):
    ll = SSD_L
    xs = xs_b.astype(F32)
    cs, cs_t = hd["cs"], hd["cs_t"]
    e_x, w_x, el_x, dt_x = _expand4(hd["e"], g), _expand4(hd["w"], g), _expand4(hd["el"], g), _expand4(hd["dt"], g)
    xd = xs * dt_x
    gcb = _dot_nt(cm_b, bm_b)
    ri = lax.broadcasted_iota(jnp.int32, (ll, ll), 0)
    ci = lax.broadcasted_iota(jnp.int32, (ll, ll), 1)
    dks, ms = [], []
    for h in range(4):
        k = 4 * g + h
        dk = jnp.exp(jnp.where(ri >= ci, cs[:, k:k + 1] - cs_t[k:k + 1, :], -1e30))
        dks.append(dk)
        ms.append((gcb * dk).astype(BF16))
    xdb = xd.astype(BF16)
    first = lax.broadcasted_iota(jnp.int32, (1, 2 * SSD_P), 1) < SSD_P
    ydiag = jnp.concatenate(
        [jnp.where(first, _dot(ms[2 * p], xdb[:, _pair(p)]), _dot(ms[2 * p + 1], xdb[:, _pair(p)])) for p in range(2)],
        axis=1)
    yoff = _dot(cm_b, st.astype(BF16)) * e_x
    y = ydiag + yoff + d_x * xs
    st_new = st * el_x + _dot(bm_b.astype(F32).T.astype(BF16), (xd * w_x).astype(BF16))
    return dict(xs=xs, e_x=e_x, w_x=w_x, el_x=el_x, dt_x=dt_x, xd=xd, xdb=xdb, gcb=gcb, dks=dks, ms=ms, yoff=yoff, y=y,
                st_new=st_new, first=first)


def ssd_consts():
    hh = np.arange(SSD_N)
    tri = (hh[:, None] >= hh[None, :]).astype(np.float32)
    return jnp.asarray(tri), jnp.asarray(tri.T)


def ssd_params(dt_bias, a_log, d_skip, norm_w):
    padh = lambda v: jnp.pad(v.reshape(1, SSD_HEADS), ((0, 0), (0, SSD_N - SSD_HEADS)))
    hp = jnp.concatenate([padh(dt_bias), padh(-jnp.exp(a_log)), jnp.zeros((6, SSD_N), F32)], axis=0)
    lp = jnp.concatenate([norm_w.reshape(1, SSD_INNER), jnp.repeat(d_skip, SSD_P).reshape(1, SSD_INNER),
                          jnp.zeros((6, SSD_INNER), F32)], axis=0)
    return hp, lp


def _b_cols(g):
    return slice(SSD_INNER + g * SSD_N, SSD_INNER + (g + 1) * SSD_N)


def _c_cols(g):
    return slice(SSD_INNER + (SSD_G + g) * SSD_N, SSD_INNER + (SSD_G + g + 1) * SSD_N)


def _ssd_specs(nc, rc):
    return [pl.BlockSpec((SSD_L, 2 * SSD_INNER), lambda b, c: (b * nc + rc(c), 0)),
            pl.BlockSpec((SSD_L, SSD_INNER), lambda b, c: (b * nc + rc(c), C_Z // SSD_INNER)),
            pl.BlockSpec((SSD_L, SSD_N), lambda b, c: (b * nc + rc(c), 0))]


def ssd_fwd(xbc, proj, dtraw, hp, lp, tri, nb, seq):
    t = xbc.shape[0]
    nc = seq // SSD_L

    def body(xbc_ref, z_ref, dt_ref, hp_ref, lp_ref, tri_ref, y_ref, sts_ref, st_scr):
        @pl.when(pl.program_id(1) == 0)
        def _():
            st_scr[...] = jnp.zeros_like(st_scr)

        hd = _ssd_heads(dt_ref[...], hp_ref[...], tri_ref[...])
        for g in range(SSD_G):
            gs = slice(g * SSD_GW, (g + 1) * SSD_GW)
            st = st_scr[g]
            sts_ref[0, g] = st
            f = _ssd_group(g, hd, xbc_ref[:, gs], xbc_ref[:, _b_cols(g)], xbc_ref[:, _c_cols(g)], lp_ref[1:2, gs], st)
            st_scr[g] = f["st_new"]
            zf = z_ref[:, gs].astype(F32)
            yg = f["y"] * (zf * _sigmoid(zf))
            y_ref[:, gs] = (yg * lax.rsqrt(_rowmean(yg * yg) + EPS) * lp_ref[0:1, gs]).astype(BF16)

    return pl.pallas_call(
        body, name="ssd_fwd", grid=(nb, nc),
        in_specs=_ssd_specs(nc, lambda c: c) + [VMEM_FULL, VMEM_FULL, VMEM_FULL],
        out_specs=[pl.BlockSpec((SSD_L, SSD_INNER), lambda b, c: (b * nc + c, 0)),
                   pl.BlockSpec((1, SSD_G, SSD_N, SSD_GW), lambda b, c: (b * nc + c, 0, 0, 0))],
        out_shape=[jax.ShapeDtypeStruct((t, SSD_INNER), BF16),
                   jax.ShapeDtypeStruct((nb * nc, SSD_G, SSD_N, SSD_GW), F32)],
        scratch_shapes=[pltpu.VMEM((SSD_G, SSD_N, SSD_GW), F32)],
        compiler_params=_cp("arbitrary", "arbitrary"),
    )(xbc, proj, dtraw, hp, lp, tri)


def ssd_bwd(xbc, proj, dtraw, hp, lp, tri, triu, states, dyn, nb, seq):
    t = xbc.shape[0]
    nc = seq // SSD_L
    ll = SSD_L

    def body(xbc_ref, z_ref, dt_ref, sts_ref, dy_ref, hp_ref, lp_ref, tri_ref, triu_ref,
             dxbc_ref, dz_ref, ddt_ref, hpg_ref, lpg_ref, dst_scr):
        b, c_i = pl.program_id(0), pl.program_id(1)

        @pl.when((b == 0) & (c_i == 0))
        def _():
            hpg_ref[...] = jnp.zeros_like(hpg_ref)
            lpg_ref[...] = jnp.zeros_like(lpg_ref)

        @pl.when(c_i == 0)
        def _():
            dst_scr[...] = jnp.zeros_like(dst_scr)

        hp = hp_ref[...]
        hd = _ssd_heads(dt_ref[...], hp, tri_ref[...])
        lane = lax.broadcasted_iota(jnp.int32, (1, SSD_N), 1)
        subl = lax.broadcasted_iota(jnp.int32, (SSD_N, 1), 0)
        dcs = jnp.zeros((ll, SSD_N), F32)
        dcs_t = jnp.zeros((SSD_N, ll), F32)
        last = jnp.zeros((1, SSD_N), F32)
        dxx = jnp.zeros((ll, SSD_N), F32)
        for g in range(SSD_G):
            gs = slice(g * SSD_GW, (g + 1) * SSD_GW)
            st = sts_ref[0, g]
            dst = dst_scr[g]
            bm_b, cm_b = xbc_ref[:, _b_cols(g)], xbc_ref[:, _c_cols(g)]
            d_x = lp_ref[1:2, gs]
            f = _ssd_group(g, hd, xbc_ref[:, gs], bm_b, cm_b, d_x, st)
            xs, xd, gcb = f["xs"], f["xd"], f["gcb"]
            e_x, w_x, el_x, dt_x = f["e_x"], f["w_x"], f["el_x"], f["dt_x"]
            stb, dstb = st.astype(BF16), dst.astype(BF16)
            zf = z_ref[:, gs].astype(F32)
            sg = _sigmoid(zf)
            sz = zf * sg
            yg = f["y"] * sz
            rstd = lax.rsqrt(_rowmean(yg * yg) + EPS)
            n = yg * rstd
            dyn_v = dy_ref[:, gs].astype(F32)
            dn = dyn_v * lp_ref[0:1, gs]
            dyg = rstd * (dn - n * _rowmean(dn * n))
            dy = dyg * sz
            dz_ref[:, gs] = (dyg * f["y"] * (sg * (1.0 + zf * (1.0 - sg)))).astype(BF16)
            dyb = dy.astype(BF16)
            r_ = _dot(bm_b, dstb)
            dxd = w_x * r_
            dqb = (dy * e_x).astype(BF16)
            dcm = _dot_nt(dqb, stb)
            dst_scr[g] = dst * el_x + _dot_tn(cm_b, dqb)
            dbm = _dot_nt((xd * w_x).astype(BF16), dstb)
            xdb = f["xdb"]
            dgm = jnp.zeros((ll, ll), F32)
            for h in range(4):
                k = 4 * g + h
                hm = _head_mask(h)
                dxd = dxd + jnp.where(hm, _dot_tn(f["ms"][h], dyb), 0.0)
                dm = _dot_nt(jnp.where(hm, dy, 0.0).astype(BF16), xdb) * f["dks"][h]
                dgm = dgm + dm
                dseg = dm * gcb
                dcs = dcs + jnp.where(lane == k, jnp.sum(dseg, axis=1, keepdims=True), 0.0)
                dcs_t = dcs_t + jnp.where(subl == k, jnp.sum(dseg, axis=0, keepdims=True), 0.0)
            dgmb = dgm.astype(BF16)
            dxbc_ref[:, _c_cols(g)] = (dcm + _dot(dgmb, bm_b)).astype(BF16)
            dxbc_ref[:, _b_cols(g)] = (dbm + _dot_tn(dgmb, cm_b)).astype(BF16)
            v = _reduce4(r_ * xd * w_x, g)
            dcs = dcs + _reduce4(dy * f["yoff"], g) - v
            last = last + _colsum(v) + _reduce4(_colsum(dst * st) * el_x, g)
            dxx = dxx + _reduce4(dxd * xs, g)
            dxbc_ref[:, gs] = (d_x * dy + dxd * dt_x).astype(BF16)
            lpg_ref[0:1, gs] += _colsum(dyn_v * n)
            lpg_ref[1:2, gs] += _colsum(dy * xs)
        rowi = lax.broadcasted_iota(jnp.int32, (ll, 1), 0)
        da = _dot_hi(triu_ref[...], dcs - dcs_t.T + jnp.where(rowi == ll - 1, last, 0.0))
        ddt = (dxx + da * hp[1:2, :]) * _sigmoid(hd["xdt"])
        ddt_ref[...] = ddt
        hpg_ref[...] += jnp.concatenate([_colsum(ddt), _colsum(da * hd["dt"]), jnp.zeros((6, SSD_N), F32)], axis=0)

    rc = lambda c: nc - 1 - c
    return pl.pallas_call(
        body, name="ssd_bwd", grid=(nb, nc),
        in_specs=_ssd_specs(nc, rc) + [
            pl.BlockSpec((1, SSD_G, SSD_N, SSD_GW), lambda b, c: (b * nc + rc(c), 0, 0, 0)),
            pl.BlockSpec((SSD_L, SSD_INNER), lambda b, c: (b * nc + rc(c), 0)),
            VMEM_FULL, VMEM_FULL, VMEM_FULL, VMEM_FULL],
        out_specs=[pl.BlockSpec((SSD_L, 2 * SSD_INNER), lambda b, c: (b * nc + rc(c), 0)),
                   pl.BlockSpec((SSD_L, SSD_INNER), lambda b, c: (b * nc + rc(c), 0)),
                   pl.BlockSpec((SSD_L, SSD_N), lambda b, c: (b * nc + rc(c), 0)),
                   pl.BlockSpec((8, SSD_N), lambda b, c: (0, 0)),
                   pl.BlockSpec((8, SSD_INNER), lambda b, c: (0, 0))],
        out_shape=[jax.ShapeDtypeStruct((t, 2 * SSD_INNER), BF16), jax.ShapeDtypeStruct((t, SSD_INNER), BF16),
                   jax.ShapeDtypeStruct((t, SSD_N), F32), jax.ShapeDtypeStruct((8, SSD_N), F32),
                   jax.ShapeDtypeStruct((8, SSD_INNER), F32)],
        scratch_shapes=[pltpu.VMEM((SSD_G, SSD_N, SSD_GW), F32)],
        compiler_params=_cp("arbitrary", "arbitrary"),
    )(xbc, proj, dtraw, states, dyn, hp, lp, tri, triu)


def ada_fwd(c_all, w_cols, b_cols):
    def body(c_ref, w_ref, b_ref, o_ref):
        cv = c_ref[...]
        o_ref[...] = _dot_hi(cv * _sigmoid(cv), w_ref[...]) + b_ref[...]

    return pl.pallas_call(body, name="ada_fwd", out_shape=jax.ShapeDtypeStruct((c_all.shape[0], w_cols.shape[1]), F32),
                          compiler_params=pltpu.CompilerParams(vmem_limit_bytes=VMEM_LIMIT))(c_all, w_cols, b_cols)


def ada_bwd(c_all, dmod_cols, dmod_all):
    def body(c_ref, dc_ref, da_ref, gw_ref, gb_ref):
        cv = c_ref[...]
        gw_ref[...] = lax.dot_general(cv * _sigmoid(cv), dc_ref[...], (((0,), (0,)), ((), ())),
                                      precision=lax.Precision.HIGHEST, preferred_element_type=F32)
        gb_ref[...] = _colsum(da_ref[...])

    return pl.pallas_call(
        body, name="ada_bwd",
        out_shape=[jax.ShapeDtypeStruct((c_all.shape[1], dmod_cols.shape[1]), F32),
                   jax.ShapeDtypeStruct((1, dmod_all.shape[1]), F32)],
        compiler_params=pltpu.CompilerParams(vmem_limit_bytes=VMEM_LIMIT))(c_all, dmod_cols, dmod_all)


def adamw(parts, w, m, v, name):
    n, r, c = parts.shape
    tr = r if r <= 256 else 128

    def body(p_ref, w_ref, m_ref, v_ref, g_ref, d_ref, nm_ref, nv_ref):
        g = p_ref[0].astype(F32)
        for s in range(1, n):
            g = g + p_ref[s].astype(F32)
        m2 = ADAM_B1 * m_ref[0] + (1.0 - ADAM_B1) * g
        v2 = ADAM_B2 * v_ref[0] + (1.0 - ADAM_B2) * (g * g)
        m_hat = m2 / (1.0 - ADAM_B1 ** ADAM_STEP)
        v_hat = v2 / (1.0 - ADAM_B2 ** ADAM_STEP)
        g_ref[0] = g
        d_ref[0] = -ADAM_LR * (m_hat / (jnp.sqrt(v_hat) + ADAM_EPS) + ADAM_WD * w_ref[0])
        nm_ref[0] = m2
        nv_ref[0] = v2

    blk = pl.BlockSpec((1, tr, c), lambda i: (0, i, 0))
    return pl.pallas_call(
        body, name=name, grid=(r // tr,),
        in_specs=[pl.BlockSpec((n, tr, c), lambda i: (0, i, 0)), blk, blk, blk], out_specs=[blk] * 4,
        out_shape=[jax.ShapeDtypeStruct((1, r, c), F32)] * 4,
        compiler_params=_cp("parallel"),
    )(parts, w, m, v)


def _dev_index(px, py, pc):
    return 4 * px + 2 * py + pc


class _Exchange:
    def __init__(self, arrs):
        self.arrs = list(arrs)
        self.na = len(self.arrs)
        self.scratch = [pltpu.SemaphoreType.DMA((7 * self.na,)), pltpu.SemaphoreType.DMA((7 * self.na,)),
                        pltpu.SemaphoreType.DMA((self.na,))]


class Gather(_Exchange):
    def __init__(self, arrs):
        super().__init__(arrs)
        self.out_shape = [jax.ShapeDtypeStruct((NDEV,) + a.shape, a.dtype) for a in self.arrs]

    def _plan(self, ins, outs, sems):
        na = self.na
        send_sems, recv_sems, local_sems = sems
        x, y, c = lax.axis_index("x"), lax.axis_index("y"), lax.axis_index("c")
        me, sibling = (x, y, c), (x, y, 1 - c)
        chips = [(1 - x, y), (x, 1 - y), (1 - x, 1 - y)]

        def copy(a, k, block, to, src=None):
            dst = outs[a].at[_dev_index(*block)]
            return pltpu.make_async_remote_copy(
                src_ref=dst if src is None else src, dst_ref=dst, send_sem=send_sems.at[a * 7 + k],
                recv_sem=recv_sems.at[a * 7 + k], device_id=to, device_id_type=MESH)

        mine = [pltpu.make_async_copy(ins[a], outs[a].at[_dev_index(*me)], local_sems.at[a]) for a in range(na)]
        first = []
        for a in range(na):
            first.append(copy(a, 0, me, sibling, src=ins[a]))
            first += [copy(a, 1 + j, me, (*chip, c), src=ins[a]) for j, chip in enumerate(chips)]
        return copy, mine, first, me, sibling, chips, c

    def start(self, ins, outs, sems):
        _, mine, first, *_ = self._plan(ins, outs, sems)
        for cp in mine + first:
            cp.start()

    def finish(self, ins, outs, sems):
        copy, mine, first, me, sibling, chips, c = self._plan(ins, outs, sems)
        passed = []
        for j, chip in enumerate(chips):
            for a in range(self.na):
                copy(a, 1 + j, (*chip, c), me).wait_recv()
                cp = copy(a, 4 + j, (*chip, c), sibling)
                cp.start()
                passed.append(cp)
        for a in range(self.na):
            copy(a, 0, sibling, me).wait_recv()
            for j, chip in enumerate(chips):
                copy(a, 4 + j, (*chip, 1 - c), me).wait_recv()
        for cp in first + passed:
            cp.wait_send()
        for cp in mine:
            cp.wait()


class Scatter(_Exchange):
    def __init__(self, arrs):
        super().__init__(arrs)
        self.out_shape = [jax.ShapeDtypeStruct(a.shape, a.dtype) for a in self.arrs]

    def _plan(self, ins, outs, sems, arrivals):
        send_sems, recv_sems, local_sems = sems
        x, y, c = lax.axis_index("x"), lax.axis_index("y"), lax.axis_index("c")
        me = _dev_index(x, y, c)
        masks = [(mx, my, mc) for mx in (0, 1) for my in (0, 1) for mc in (0, 1)][1:]
        flip = lambda v, bit: 1 - v if bit else v
        mine = [pltpu.make_async_copy(ins[a].at[me], outs[a].at[me], local_sems.at[a]) for a in range(self.na)]
        sends, recvs = [], []
        for k, (mx, my, mc) in enumerate(masks):
            peer = (flip(x, mx), flip(y, my), flip(c, mc))
            pidx = _dev_index(*peer)
            for a in range(self.na):
                on = dict(send_sem=send_sems.at[a * 7 + k], recv_sem=recv_sems.at[a * 7 + k], device_id=peer,
                          device_id_type=MESH)
                sends.append(pltpu.make_async_remote_copy(src_ref=ins[a].at[pidx], dst_ref=outs[a].at[me], **on))
                if arrivals:
                    recvs.append(pltpu.make_async_remote_copy(src_ref=ins[a].at[pidx], dst_ref=outs[a].at[pidx], **on))
        return mine, sends, recvs

    def start(self, ins, outs, sems):
        mine, sends, _ = self._plan(ins, outs, sems, arrivals=False)
        for cp in mine + sends:
            cp.start()

    def finish(self, ins, outs, sems):
        mine, sends, recvs = self._plan(ins, outs, sems, arrivals=True)
        for cp in recvs:
            cp.wait_recv()
        for cp in sends:
            cp.wait_send()
        for cp in mine:
            cp.wait()


def exchange_call(ex, name):
    na = ex.na

    def body(*refs):
        ins, outs, sems = refs[:na], refs[na:2 * na], refs[2 * na:]
        ex.start(ins, outs, sems)
        ex.finish(ins, outs, sems)

    return pl.pallas_call(body, name=name, in_specs=[ANY] * na, out_specs=[ANY] * na, out_shape=ex.out_shape,
                          scratch_shapes=ex.scratch)(*ex.arrs)


def all_gather(arrs, name):
    return exchange_call(Gather(arrs), name)


def _call(body, *, name, grid, in_specs, out_specs, out_shape, scratch_shapes=(), sem, args, side=None):
    if side is None:
        outs = pl.pallas_call(body, name=name, grid=grid, in_specs=list(in_specs), out_specs=list(out_specs),
                              out_shape=list(out_shape), scratch_shapes=list(scratch_shapes),
                              compiler_params=_cp(*sem))(*args)
        return outs, []
    ni, no, ns, na = len(in_specs), len(out_specs), len(scratch_shapes), side.na

    def wrapped(*refs):
        ins, s_in = refs[:ni], refs[ni:ni + na]
        outs, s_out = refs[ni + na:ni + na + no], refs[ni + na + no:ni + 2 * na + no]
        scr, sems = refs[ni + 2 * na + no:ni + 2 * na + no + ns], refs[ni + 2 * na + no + ns:]
        pids = [pl.program_id(i) for i in range(len(grid))]
        first = functools.reduce(lambda p, q: p & q, [p == 0 for p in pids])
        last = functools.reduce(lambda p, q: p & q, [p == g - 1 for p, g in zip(pids, grid)])

        @pl.when(first)
        def _():
            side.start(s_in, s_out, sems)

        body(*ins, *outs, *scr)

        @pl.when(last)
        def _():
            side.finish(s_in, s_out, sems)

    outs = pl.pallas_call(
        wrapped, name=name, grid=grid, in_specs=list(in_specs) + [ANY] * na, out_specs=list(out_specs) + [ANY] * na,
        out_shape=list(out_shape) + side.out_shape, scratch_shapes=list(scratch_shapes) + side.scratch,
        compiler_params=_cp(*["arbitrary"] * len(grid)))(*args, *side.arrs)
    return outs[:no], outs[no:]


WEIGHTS = ('w_ada', 'b_ada', 'pre_norm1', 'post_norm1', 'w_in', 'b_gate', 'lru_conv_w', 'lru_conv_b', 'lru_wa',
           'lru_ba', 'lru_wx', 'lru_bx', 'lru_lambda', 'w_pa', 'ssd_conv_w', 'ssd_conv_b', 'ssd_dt_bias', 'ssd_a_log',
           'ssd_d', 'ssd_norm_w', 'w_pb', 'w_out', 'pre_norm2', 'post_norm2', 'w_ff1', 'w_ff2')
BIG = ('w_in', 'w_pa', 'w_pb', 'w_out', 'w_ff1', 'w_ff2')
REPL = ('pre_norm1', 'post_norm1', 'b_gate', 'lru_conv_b', 'lru_wa', 'lru_ba', 'lru_wx', 'lru_bx', 'lru_lambda',
        'ssd_conv_b', 'ssd_dt_bias', 'ssd_a_log', 'ssd_d', 'ssd_norm_w', 'pre_norm2', 'post_norm2')
LANES = 1024


def _rows(n):
    return -(-n // LANES)


def _pack(vals, total_rows):
    parts = []
    for v in vals:
        f = v.reshape(-1).astype(F32)
        parts.append(jnp.pad(f, (0, _rows(f.shape[0]) * LANES - f.shape[0])))
    flat = jnp.concatenate(parts)
    return jnp.pad(flat.reshape(-1, LANES), ((0, total_rows - flat.shape[0] // LANES), (0, 0)))


def _unpack(slab, shapes):
    out, r = [], 0
    for s in shapes:
        n = int(np.prod(s))
        out.append(slab[r:r + _rows(n)].reshape(-1)[:n].reshape(s))
        r += _rows(n)
    return out


def _block_diag4(w):
    w4 = w.reshape(4, 4, 64, 64)
    eye = jnp.eye(4, dtype=w.dtype)
    return (w4[:, :, :, None, :] * eye[None, :, None, :, None]).reshape(4, LRU_BLOCK, LRU_BLOCK)


def _diag_blocks4(m):
    m5 = m.reshape(4, 4, 64, 4, 64)
    return jnp.stack([m5[:, a, :, a, :] for a in range(4)], axis=1).reshape(LRU_HEADS, 64, 64)


def kernel(x, c, w_ada, b_ada, pre_norm1, post_norm1, w_in, b_gate, lru_conv_w, lru_conv_b, lru_wa, lru_ba, lru_wx, lru_bx, lru_lambda, w_pa, ssd_conv_w, ssd_conv_b, ssd_dt_bias, ssd_a_log, ssd_d, ssd_norm_w, w_pb, w_out, pre_norm2, post_norm2, w_ff1, w_ff2, loss_target, m_w_ada, m_b_ada, m_pre_norm1, m_post_norm1, m_w_in, m_b_gate, m_lru_conv_w, m_lru_conv_b, m_lru_wa, m_lru_ba, m_lru_wx, m_lru_bx, m_lru_lambda, m_w_pa, m_ssd_conv_w, m_ssd_conv_b, m_ssd_dt_bias, m_ssd_a_log, m_ssd_d, m_ssd_norm_w, m_w_pb, m_w_out, m_pre_norm2, m_post_norm2, m_w_ff1, m_w_ff2, v_w_ada, v_b_ada, v_pre_norm1, v_post_norm1, v_w_in, v_b_gate, v_lru_conv_w, v_lru_conv_b, v_lru_wa, v_lru_ba, v_lru_wx, v_lru_bx, v_lru_lambda, v_w_pa, v_ssd_conv_w, v_ssd_conv_b, v_ssd_dt_bias, v_ssd_a_log, v_ssd_d, v_ssd_norm_w, v_w_pb, v_w_out, v_pre_norm2, v_post_norm2, v_w_ff1, v_w_ff2):
    given = dict(locals())
    w = {k: given[k] for k in WEIGHTS}
    mom = {k: given["m_" + k] for k in WEIGHTS}
    var = {k: given["v_" + k] for k in WEIGHTS}
    nb, seq, _ = x.shape
    assert nb == 2 and seq % 512 == 0, (nb, seq)
    t = nb * seq
    me = _dev_index(lax.axis_index("x"), lax.axis_index("y"), lax.axis_index("c"))
    x2 = x.reshape(t, D_MODEL)
    tgt2 = loss_target.reshape(t, D_MODEL)
    ada_cols = w_ada.shape[2]

    slab = jnp.zeros((16, LANES), F32)
    slab = slab.at[0:nb].set(c)
    slab = slab.at[2:6, 0:lru_conv_w.shape[2]].set(lru_conv_w[0])
    slab = slab.at[6:10, 0:ssd_conv_w.shape[2]].set(ssd_conv_w[0])
    (g1,) = all_gather([slab], "gather_cond")
    c_all = g1[:, 0:nb].reshape(NDEV * nb, D_MODEL)
    lru_cw = g1[:, 2:6, 0:lru_conv_w.shape[2]].transpose(1, 0, 2).reshape(4, D_MODEL)
    ssd_cw = g1[:, 6:10, 0:ssd_conv_w.shape[2]].transpose(1, 0, 2).reshape(4, 2 * SSD_INNER)
    b_cols = lax.dynamic_slice(b_ada, (0, me * ada_cols), (1, ada_cols))
    mod_cols = ada_fwd(c_all, w_ada[0], b_cols)
    (g2,) = all_gather([mod_cols], "gather_mod")
    mod_all = g2.transpose(1, 0, 2).reshape(NDEV * nb, N_MOD * D_MODEL)
    mod_mine = lax.dynamic_slice(mod_all, (me * nb, 0), (nb, N_MOD * D_MODEL)).reshape(nb, N_MOD, D_MODEL)
    mod8 = jnp.pad(mod_mine, ((0, 0), (0, 8 - N_MOD), (0, 0)))

    (gw_in,) = all_gather([w_in.astype(BF16)], "gather_w_in")
    w_nat = gw_in[:, 0].transpose(1, 0, 2).reshape(D_MODEL, IN_DIM)
    w_main = jnp.concatenate([w_nat[:, :DT_COL0], w_nat[:, DT_COL0 + SSD_HEADS:]], axis=1)
    w_dt = jnp.pad(w_nat[:, DT_COL0:DT_COL0 + SSD_HEADS], ((0, 0), (0, 128 - SSD_HEADS)))

    wa_bd = _block_diag4(lru_wa[0]).astype(BF16)
    wx_bd = _block_diag4(lru_wx[0]).astype(BF16)
    lam = lru_lambda[0]
    vec = _pack([lru_ba, lru_bx, jax.nn.softplus(-lam)], 8)
    tri, triu = ssd_consts()
    hp, lp = ssd_params(ssd_dt_bias[0], ssd_a_log[0], ssd_d[0], ssd_norm_w[0])

    rest = Gather([w[k].astype(BF16) for k in BIG[1:]])
    (proj, h1t, dtraw), gw = in_proj_fwd(x2, mod8, pre_norm1, w_main, w_dt, seq, side=rest)
    w_pa_f = gw[0].reshape(D_MODEL, D_MODEL)
    w_pb_f = gw[1].reshape(SSD_INNER, D_MODEL)
    w_out_f = gw[2].reshape(D_MODEL, D_MODEL)
    w_ff1_f = gw[3][:, 0].transpose(1, 0, 2).reshape(D_MODEL, D_FF)
    w_ff2_f = gw[4].reshape(D_FF, D_MODEL)
    xa = conv_fwd(proj, C_LRU_X, D_MODEL, lru_cw, lru_conv_b, nb, seq, False, "conv_lru_fwd")
    xbc = conv_fwd(proj, C_XBC, 2 * SSD_INNER, ssd_cw, ssd_conv_b, nb, seq, True, "conv_ssd_fwd")
    ya_in, hst = lru_fwd(xa, proj, wa_bd, wx_bd, vec, nb, seq)
    yb_in, states = ssd_fwd(xbc, proj, dtraw, hp, lp, tri, nb, seq)
    yab, out1, x1 = merge_fwd(ya_in, yb_in, proj, x2, mod8, b_gate, post_norm1, w_pa_f, w_pb_f, w_out_f, seq)

    dx1, h2, da1, act, dy2, loss8, vacc_mlp, dmod_mlp = mlp_fwd_bwd(
        x1, tgt2, mod8, pre_norm2, post_norm2, w_ff1_f, w_ff2_f, nb, seq)
    wg = dict(out_dtype=BF16, ta=True, tm=1024, tn=1024, tk=1024)
    dw_ff1 = matmul(h2, da1, name="wgrad_ff1", **wg)
    dw_ff2 = matmul(act, dy2, name="wgrad_ff2", **wg)
    dya_in, dyb_in, dgates, dyab, dout1, merged, vacc_mg, dmod_mg = merge_bwd(
        dx1, out1, yab, proj, mod8, b_gate, post_norm1, w_pa_f, w_pb_f, w_out_f, nb, seq)
    dw_out = matmul(merged, dout1, name="wgrad_out", **wg)
    dw_pa = matmul(ya_in, dyab, name="wgrad_pa", n=D_MODEL, b_off=0, **wg)
    dw_pb = matmul(yb_in, dyab, name="wgrad_pb", n=D_MODEL, b_off=1, **wg)
    by_rows = lambda g: g.reshape(NDEV, g.shape[0] // NDEV, g.shape[1])
    by_cols = lambda g: g.reshape(g.shape[0], NDEV, g.shape[1] // NDEV).transpose(1, 0, 2)
    (dxa, dlg, dwa_bd, dwx_bd, dvec), parts_ff = lru_bwd(
        dya_in, xa, proj, hst, wa_bd, wx_bd, vec, nb, seq, side=Scatter([by_cols(dw_ff1), by_rows(dw_ff2)]))
    dxbc, dz, ddt, hpg, lpg = ssd_bwd(xbc, proj, dtraw, hp, lp, tri, triu, states, dyb_in, nb, seq)
    (dlx, acc_l), _ = conv_bwd(proj, C_LRU_X, D_MODEL, lru_cw, lru_conv_b, dxa, nb, seq, False, "conv_lru_bwd")
    (dxr, acc_s), parts_mg = conv_bwd(proj, C_XBC, 2 * SSD_INNER, ssd_cw, ssd_conv_b, dxbc, nb, seq, True,
                                      "conv_ssd_bwd", side=Scatter([by_rows(dw_pa), by_rows(dw_pb), by_rows(dw_out)]))
    pieces = (dlx, dlg, dz, dxr, dgates)
    ddt_b = ddt.astype(BF16)
    dw_main, dw_dt = in_proj_wgrad(h1t, pieces, ddt_b)
    dw_nat = jnp.concatenate([dw_main[:, :DT_COL0], dw_dt[:, :SSD_HEADS], dw_main[:, DT_COL0:]], axis=1)
    (grad_x, vacc_in, dmod_in), parts_in = in_proj_bwd(pieces, ddt_b, dx1, x2, mod8, pre_norm1, w_main, w_dt, nb, seq,
                                                       side=Scatter([by_cols(dw_nat)]))
    parts = dict(zip(BIG, (parts_in[0], *parts_mg, *parts_ff)))

    dmod = (dmod_in + dmod_mg + dmod_mlp)[:, :N_MOD].reshape(nb, N_MOD * D_MODEL)
    (g3,) = all_gather([jnp.pad(dmod, ((0, 8 - nb), (0, 0)))], "gather_dmod")
    dmod_all = g3[:, :nb].reshape(NDEV * nb, N_MOD * D_MODEL)
    dmod_cols = lax.dynamic_slice(dmod_all, (0, me * ada_cols), (NDEV * nb, ada_cols))
    g_w_ada, g_b_ada = ada_bwd(c_all, dmod_cols, dmod_all)

    res = {}
    for k in BIG:
        res[k] = adamw(parts[k], w[k], mom[k], var[k], "adamw_" + k)
    res['w_ada'] = adamw(g_w_ada[None], w_ada, m_w_ada, v_w_ada, "adamw_w_ada")

    a_neg = -jnp.exp(ssd_a_log[0])
    small = {
        'pre_norm1': vacc_in[0], 'post_norm1': vacc_mg[1, :D_MODEL], 'b_gate': vacc_mg[0],
        'lru_conv_b': acc_l[4], 'lru_wa': _diag_blocks4(dwa_bd), 'lru_ba': dvec[0], 'lru_wx': _diag_blocks4(dwx_bd),
        'lru_bx': dvec[1], 'lru_lambda': dvec[2] * (-jax.nn.sigmoid(-lam)),
        'ssd_conv_b': acc_s[4],
        'ssd_dt_bias': hpg[0, :SSD_HEADS], 'ssd_a_log': hpg[1, :SSD_HEADS] * a_neg,
        'ssd_d': lpg[1].reshape(SSD_HEADS, SSD_P).sum(axis=-1), 'ssd_norm_w': lpg[0],
        'pre_norm2': vacc_mlp[0], 'post_norm2': vacc_mlp[1],
    }
    conv_full = [acc_l[:4], acc_s[:4]]
    nra = sum(_rows(int(np.prod(w[k].shape))) for k in REPL)
    nrc = sum(_rows(int(np.prod(v.shape))) for v in conv_full)
    rows_a = -(-(nra + nrc) // 8) * 8
    gslab = _pack([small[k] for k in REPL] + conv_full, rows_a)
    (g4,) = all_gather([gslab], "gather_small_grads")
    res_a = adamw(g4, _pack([w[k] for k in REPL], rows_a)[None], _pack([mom[k] for k in REPL], rows_a)[None],
                  _pack([var[k] for k in REPL], rows_a)[None], "adamw_small")
    for j, slab_j in enumerate(res_a):
        for k, val in zip(REPL, _unpack(slab_j[0], [w[k].shape for k in REPL])):
            res.setdefault(k, [None] * 4)[j] = val
    g_lru_cw, g_ssd_cw = _unpack(res_a[0][0, nra:], [v.shape for v in conv_full])
    lcw, scw = lru_conv_w.shape[2], ssd_conv_w.shape[2]
    sharded = {'b_ada': g_b_ada, 'lru_conv_w': lax.dynamic_slice(g_lru_cw, (0, me * lcw), (4, lcw)),
               'ssd_conv_w': lax.dynamic_slice(g_ssd_cw, (0, me * scw), (4, scw))}
    names_b = tuple(sharded)
    res_b = adamw(_pack([sharded[k] for k in names_b], 16)[None], _pack([w[k] for k in names_b], 16)[None],
                  _pack([mom[k] for k in names_b], 16)[None], _pack([var[k] for k in names_b], 16)[None],
                  "adamw_small_sharded")
    for j, slab_j in enumerate(res_b):
        for k, val in zip(names_b, _unpack(slab_j[0], [w[k].shape for k in names_b])):
            res.setdefault(k, [None] * 4)[j] = val

    loss = lax.psum(loss8[0, 0], ("x", "y", "c"))
    outs = [[res[k][j].reshape(w[k].shape) for k in WEIGHTS] for j in range(4)]
    return (loss, grad_x.reshape(x.shape), *outs[0], *outs[1], *outs[2], *outs[3])
```

```python
import functools

import numpy as np
import jax
import jax.numpy as jnp
from jax import lax
from jax.experimental import pallas as pl
from jax.experimental.pallas import tpu as pltpu

F32 = jnp.float32
BF16 = jnp.bfloat16

D_MODEL = 1024
LRU_HEADS = 16
LRU_BLOCK = 256
LRU_C = 8.0
SSD_INNER = 2048
SSD_HEADS = 32
SSD_P = 64
SSD_G = 8
SSD_N = 128
SSD_L = 128
SSD_GW = SSD_INNER // SSD_G
D_FF = 4096
N_MOD = 6
EPS = 1e-6
NDEV = 8

C_LRU_X, C_LRU_G, C_Z, C_XBC, C_GATES, PROJ_MAIN = 0, 1024, 2048, 4096, 8192, 10240
IN_DIM = 10272
DT_COL0 = 8192
HALO = 16

ADAM_LR, ADAM_B1, ADAM_B2, ADAM_EPS, ADAM_WD, ADAM_STEP = 0.001, 0.9, 0.999, 1e-08, 0.01, 10

VMEM_LIMIT = 60 * 1024 * 1024
MESH = pl.DeviceIdType.MESH
ANY = pl.BlockSpec(memory_space=pl.ANY)
VMEM_FULL = pl.BlockSpec(memory_space=pltpu.VMEM)


def _cp(*sem):
    return pltpu.CompilerParams(dimension_semantics=sem, vmem_limit_bytes=VMEM_LIMIT)


def _dot(a, b):
    return jnp.dot(a, b, preferred_element_type=F32)


def _dot_nt(a, b):
    return lax.dot_general(a, b, (((1,), (1,)), ((), ())), preferred_element_type=F32)


def _dot_tn(a, b):
    return lax.dot_general(a, b, (((0,), (0,)), ((), ())), preferred_element_type=F32)


def _dot_hi(a, b):
    return jnp.dot(a, b, precision=lax.Precision.HIGHEST, preferred_element_type=F32)


def _sigmoid(x):
    return 1.0 / (1.0 + jnp.exp(-x))


def _gelu_and_grad(x):
    k0, k1 = 0.7978845608028654, 0.044715
    t = jnp.tanh(k0 * (x + k1 * x * x * x))
    g = 0.5 * x * (1.0 + t)
    dg = 0.5 * (1.0 + t) + 0.5 * x * (1.0 - t * t) * k0 * (1.0 + 3.0 * k1 * x * x)
    return g, dg


def _neg_expm1(y):
    p = 1.0 + y * (1.0 / 7.0)
    p = 1.0 + y * (1.0 / 6.0) * p
    p = 1.0 + y * (1.0 / 5.0) * p
    p = 1.0 + y * (1.0 / 4.0) * p
    p = 1.0 + y * (1.0 / 3.0) * p
    p = 1.0 + y * 0.5 * p
    return jnp.where(y > -0.3, -y * p, 1.0 - jnp.exp(y))


def _colsum(v):
    return jnp.sum(v, axis=0, keepdims=True)


def _rowmean(v):
    return jnp.mean(v, axis=-1, keepdims=True)


def matmul(a, b, *, ta=False, tb=False, out_dtype=F32, tm, tn, tk, name, n=None, b_off=0):
    m = a.shape[1] if ta else a.shape[0]
    kdim = a.shape[0] if ta else a.shape[1]
    n = n or (b.shape[0] if tb else b.shape[1])
    tm, tn, tk = min(tm, m), min(tn, n), min(tk, kdim)
    nk = kdim // tk
    dn = (((0 if ta else 1,), (1 if tb else 0,)), ((), ()))

    def body(a_ref, b_ref, o_ref, acc_ref):
        k = pl.program_id(2)
        p = lax.dot_general(a_ref[...], b_ref[...], dn, preferred_element_type=F32)
        if nk == 1:
            o_ref[...] = p.astype(out_dtype)
        else:
            @pl.when(k == 0)
            def _():
                acc_ref[...] = p

            @pl.when(k > 0)
            def _():
                acc_ref[...] += p

            @pl.when(k == nk - 1)
            def _():
                o_ref[...] = acc_ref[...].astype(out_dtype)

    a_spec = pl.BlockSpec((tk, tm), lambda i, j, k: (k, i)) if ta else pl.BlockSpec((tm, tk), lambda i, j, k: (i, k))
    b_spec = (pl.BlockSpec((tn, tk), lambda i, j, k: (j, k)) if tb
              else pl.BlockSpec((tk, tn), lambda i, j, k: (k, j + b_off)))
    return pl.pallas_call(
        body, name=name, grid=(m // tm, n // tn, nk),
        in_specs=[a_spec, b_spec], out_specs=pl.BlockSpec((tm, tn), lambda i, j, k: (i, j)),
        out_shape=jax.ShapeDtypeStruct((m, n), out_dtype),
        scratch_shapes=[pltpu.VMEM((tm, tn), F32)],
        compiler_params=_cp("parallel", "parallel", "arbitrary"),
    )(a, b)


def in_proj_fwd(x2, mod8, pre1, w_main, w_dt, seq, side=None):
    t = x2.shape[0]
    tm = min(1024, seq)
    tn = 2048
    per_seq = seq // tm

    def body(x_ref, mod_ref, pre_ref, w_ref, wdt_ref, proj_ref, h_ref, dt_ref, h_scr):
        @pl.when(pl.program_id(1) == 0)
        def _():
            xv = x_ref[...]
            y = xv * lax.rsqrt(_rowmean(xv * xv) + EPS) * pre_ref[...]
            m = mod_ref[0]
            hf = y * (1.0 + m[1:2, :]) + m[0:1, :]
            h = hf.astype(BF16)
            h_scr[...] = h
            h_ref[...] = hf.T.astype(BF16)
            dt_ref[...] = _dot(h, wdt_ref[...])

        proj_ref[...] = _dot(h_scr[...], w_ref[...]).astype(BF16)

    return _call(
        body, name="in_proj_fwd", grid=(t // tm, PROJ_MAIN // tn), side=side, sem=("parallel", "arbitrary"),
        args=(x2, mod8, pre1, w_main, w_dt),
        in_specs=[pl.BlockSpec((tm, D_MODEL), lambda i, j: (i, 0)),
                  pl.BlockSpec((1, 8, D_MODEL), lambda i, j: (i // per_seq, 0, 0)),
                  pl.BlockSpec((1, D_MODEL), lambda i, j: (0, 0)),
                  pl.BlockSpec((D_MODEL, tn), lambda i, j: (0, j)),
                  pl.BlockSpec((D_MODEL, 128), lambda i, j: (0, 0))],
        out_specs=[pl.BlockSpec((tm, tn), lambda i, j: (i, j)),
                   pl.BlockSpec((None, D_MODEL, tm), lambda i, j: (i, 0, 0)),
                   pl.BlockSpec((tm, 128), lambda i, j: (i, 0))],
        out_shape=[jax.ShapeDtypeStruct((t, PROJ_MAIN), BF16), jax.ShapeDtypeStruct((t // tm, D_MODEL, tm), BF16),
                   jax.ShapeDtypeStruct((t, 128), F32)],
        scratch_shapes=[pltpu.VMEM((tm, D_MODEL), BF16)])


def conv_fwd(src, col0, width, w4, bias, nb, seq, act, name):
    t = src.shape[0]
    tt = min(512, seq)
    tc = 512
    ns = seq // tt
    cb0 = col0 // tc

    def body(cur_ref, prev_ref, w_ref, b_ref, o_ref):
        s = pl.program_id(1)
        cur = cur_ref[...].astype(F32)
        prev = jnp.where(s == 0, 0.0, prev_ref[...].astype(F32))
        xx = jnp.concatenate([prev, cur], axis=0)
        w = w_ref[...]
        acc = cur * w[3:4, :] + b_ref[...]
        for d in (1, 2, 3):
            acc = acc + pltpu.roll(xx, d, axis=0)[HALO:, :] * w[3 - d:4 - d, :]
        if act:
            acc = acc * _sigmoid(acc)
        o_ref[...] = acc.astype(BF16)

    return pl.pallas_call(
        body, name=name, grid=(nb, ns, width // tc),
        in_specs=[pl.BlockSpec((tt, tc), lambda b, s, j: (b * ns + s, cb0 + j)),
                  pl.BlockSpec((HALO, tc), lambda b, s, j: (jnp.maximum((b * seq + s * tt) // HALO - 1, 0), cb0 + j)),
                  pl.BlockSpec((4, tc), lambda b, s, j: (0, j)),
                  pl.BlockSpec((1, tc), lambda b, s, j: (0, j))],
        out_specs=pl.BlockSpec((tt, tc), lambda b, s, j: (b * ns + s, j)),
        out_shape=jax.ShapeDtypeStruct((t, width), BF16),
        compiler_params=_cp("parallel", "parallel", "parallel"),
    )(src, src, w4, bias)


def conv_bwd(src, col0, width, w4, bias, dout, nb, seq, act, name, side=None):
    t = src.shape[0]
    tt = min(512, seq)
    tc = 512
    ns = seq // tt
    cb0 = col0 // tc
    nh = t // HALO

    def body(cur_ref, prev_ref, next_ref, w_ref, b_ref, do_ref, don_ref, dx_ref, acc_ref):
        b, s = pl.program_id(1), pl.program_id(2)

        @pl.when((b == 0) & (s == 0))
        def _():
            acc_ref[...] = jnp.zeros_like(acc_ref)

        cur = cur_ref[...].astype(F32)
        prev = jnp.where(s == 0, 0.0, prev_ref[...].astype(F32))
        nxt = next_ref[...].astype(F32)
        xx = jnp.concatenate([prev, cur, nxt], axis=0)
        w = w_ref[...]
        do_ext = jnp.concatenate([do_ref[...].astype(F32),
                                  jnp.where(s == ns - 1, 0.0, don_ref[...].astype(F32))], axis=0)
        ne = tt + HALO
        xs = [xx[HALO:HALO + ne, :]] + [pltpu.roll(xx, d, axis=0)[HALO:HALO + ne, :] for d in (1, 2, 3)]
        if act:
            c = b_ref[...] + xs[0] * w[3:4, :] + xs[1] * w[2:3, :] + xs[2] * w[1:2, :] + xs[3] * w[0:1, :]
            sg = _sigmoid(c)
            dc = do_ext * (sg * (1.0 + c * (1.0 - sg)))
        else:
            dc = do_ext
        dx = dc[:tt, :] * w[3:4, :]
        for d in (1, 2, 3):
            dx = dx + pltpu.roll(dc, ne - d, axis=0)[:tt, :] * w[3 - d:4 - d, :]
        dx_ref[...] = dx.astype(BF16)
        dcc = dc[:tt, :]
        rows = [_colsum(dcc * xs[3 - r][:tt, :]) for r in range(4)] + [_colsum(dcc)]
        acc_ref[...] += jnp.concatenate(rows + [jnp.zeros((3, tc), F32)], axis=0)

    return _call(
        body, name=name, grid=(width // tc, nb, ns), side=side, sem=("parallel", "arbitrary", "arbitrary"),
        args=(src, src, src, w4, bias, dout, dout),
        in_specs=[pl.BlockSpec((tt, tc), lambda j, b, s: (b * ns + s, cb0 + j)),
                  pl.BlockSpec((HALO, tc), lambda j, b, s: (jnp.maximum((b * seq + s * tt) // HALO - 1, 0), cb0 + j)),
                  pl.BlockSpec((HALO, tc), lambda j, b, s: (jnp.minimum((b * seq + (s + 1) * tt) // HALO, nh - 1), cb0 + j)),
                  pl.BlockSpec((4, tc), lambda j, b, s: (0, j)),
                  pl.BlockSpec((1, tc), lambda j, b, s: (0, j)),
                  pl.BlockSpec((tt, tc), lambda j, b, s: (b * ns + s, j)),
                  pl.BlockSpec((HALO, tc), lambda j, b, s: (jnp.minimum((b * seq + (s + 1) * tt) // HALO, nh - 1), j))],
        out_specs=[pl.BlockSpec((tt, tc), lambda j, b, s: (b * ns + s, j)),
                   pl.BlockSpec((8, tc), lambda j, b, s: (0, j))],
        out_shape=[jax.ShapeDtypeStruct((t, width), BF16), jax.ShapeDtypeStruct((8, width), F32)])


def _lru_gates(xa, wa_ref, wx_ref, ba, bx, sp):
    nblk = D_MODEL // LRU_BLOCK
    pr = jnp.concatenate([_dot(xa[:, j * LRU_BLOCK:(j + 1) * LRU_BLOCK], wa_ref[j]) for j in range(nblk)], axis=1) + ba
    pi = jnp.concatenate([_dot(xa[:, j * LRU_BLOCK:(j + 1) * LRU_BLOCK], wx_ref[j]) for j in range(nblk)], axis=1) + bx
    r = _sigmoid(pr)
    i = _sigmoid(pi)
    log_a = (-LRU_C * r) * sp
    return r, i, jnp.exp(log_a), _neg_expm1(2.0 * log_a)


def lru_fwd(xa, proj, wa_bd, wx_bd, vec, nb, seq):
    t = xa.shape[0]
    tc = min(512, seq)
    nk = seq // tc
    gb = C_LRU_G // D_MODEL

    def body(xa_ref, g_ref, wa_ref, wx_ref, vec_ref, ya_ref, h_ref, a_scr, u_scr, hc_scr):
        @pl.when(pl.program_id(1) == 0)
        def _():
            hc_scr[...] = jnp.zeros_like(hc_scr)

        xa_v = xa_ref[...]
        v = vec_ref[...]
        r, i, a, e = _lru_gates(xa_v, wa_ref, wx_ref, v[0:1, :], v[1:2, :], v[2:3, :])
        a_scr[...] = a
        u_scr[...] = jnp.sqrt(e) * (i * xa_v.astype(F32))
        row = lax.broadcasted_iota(jnp.int32, (8, 1), 0)

        def tile(j, h):
            r0 = pl.multiple_of(j * 8, 8)
            av, uv = a_scr[pl.ds(r0, 8), :], u_scr[pl.ds(r0, 8), :]
            for d in (1, 2, 4):
                uv = uv + av * jnp.where(row >= d, pltpu.roll(uv, d, axis=0), 0.0)
                av = av * jnp.where(row >= d, pltpu.roll(av, d, axis=0), 1.0)
            hv = uv + av * h
            h_ref[pl.ds(r0, 8), :] = hv
            return hv[7:8, :]

        hc_scr[...] = lax.fori_loop(0, tc // 8, tile, hc_scr[...], unroll=2)
        gel, _ = _gelu_and_grad(g_ref[...].astype(F32))
        ya_ref[...] = (h_ref[...] * gel).astype(BF16)

    return pl.pallas_call(
        body, name="lru_fwd", grid=(nb, nk),
        in_specs=[pl.BlockSpec((tc, D_MODEL), lambda b, k: (b * nk + k, 0)),
                  pl.BlockSpec((tc, D_MODEL), lambda b, k: (b * nk + k, gb)),
                  VMEM_FULL, VMEM_FULL, VMEM_FULL],
        out_specs=[pl.BlockSpec((tc, D_MODEL), lambda b, k: (b * nk + k, 0)),
                   pl.BlockSpec((tc, D_MODEL), lambda b, k: (b * nk + k, 0))],
        out_shape=[jax.ShapeDtypeStruct((t, D_MODEL), BF16), jax.ShapeDtypeStruct((t, D_MODEL), F32)],
        scratch_shapes=[pltpu.VMEM((tc, D_MODEL), F32), pltpu.VMEM((tc, D_MODEL), F32), pltpu.VMEM((1, D_MODEL), F32)],
        compiler_params=_cp("arbitrary", "arbitrary"),
    )(xa, proj, wa_bd, wx_bd, vec)


def lru_bwd(dya, xa, proj, h, wa_bd, wx_bd, vec, nb, seq, side=None):
    t = xa.shape[0]
    tc = min(512, seq)
    nk = seq // tc
    gb = C_LRU_G // D_MODEL
    nblk = D_MODEL // LRU_BLOCK

    def chunk(b, k):
        return b * nk + (nk - 1 - k)

    def body(dya_ref, xa_ref, g_ref, h_ref, hp_ref, wa_ref, wx_ref, vec_ref,
             dxa_ref, dg_ref, dwa_ref, dwx_ref, dvec_ref, a_scr, dh_scr, c_scr):
        b, k = pl.program_id(0), pl.program_id(1)

        @pl.when((b == 0) & (k == 0))
        def _():
            dwa_ref[...] = jnp.zeros_like(dwa_ref)
            dwx_ref[...] = jnp.zeros_like(dwx_ref)
            dvec_ref[...] = jnp.zeros_like(dvec_ref)

        @pl.when(k == 0)
        def _():
            c_scr[...] = jnp.zeros_like(c_scr)

        xa_v = xa_ref[...]
        xaf = xa_v.astype(F32)
        v = vec_ref[...]
        sp = v[2:3, :]
        r, i, a, e = _lru_gates(xa_v, wa_ref, wx_ref, v[0:1, :], v[1:2, :], sp)
        gel, dgel = _gelu_and_grad(g_ref[...].astype(F32))
        hv = h_ref[...]
        dyv = dya_ref[...].astype(F32)
        dg_ref[...] = (dyv * hv * dgel).astype(BF16)
        a_scr[...] = a
        dh_scr[...] = dyv * gel

        row8 = lax.broadcasted_iota(jnp.int32, (8, 1), 0)

        def tile(j, c):
            r0 = pl.multiple_of((tc // 8 - 1 - j) * 8, 8)
            av, dout = a_scr[pl.ds(r0, 8), :], dh_scr[pl.ds(r0, 8), :]
            zv = av * dout
            for d in (1, 2, 4):
                zv = zv + av * jnp.where(row8 < 8 - d, pltpu.roll(zv, 8 - d, axis=0), 0.0)
                av = av * jnp.where(row8 < 8 - d, pltpu.roll(av, 8 - d, axis=0), 1.0)
            zv = zv + av * c
            dh_scr[pl.ds(r0, 8), :] = dout + jnp.where(row8 < 7, pltpu.roll(zv, 7, axis=0), c)
            return zv[0:1, :]

        c_scr[...] = lax.fori_loop(0, tc // 8, tile, c_scr[...], unroll=2)
        dh = dh_scr[...]
        h_last = jnp.where(k == nk - 1, 0.0, hp_ref[HALO // 2 - 1:HALO // 2, :])
        row = lax.broadcasted_iota(jnp.int32, (tc, 1), 0)
        h_prev = jnp.where(row == 0, h_last, pltpu.roll(hv, 1, axis=0))
        s = jnp.sqrt(e)
        da = dh * h_prev
        ix = i * xaf
        dlog_a = da * a - (dh * ix) * (a * a) * lax.rsqrt(jnp.maximum(e, 1e-30))
        di = dh * s * xaf
        dpr = (dlog_a * (-LRU_C * sp)) * (r * (1.0 - r))
        dpi = di * (i * (1.0 - i))
        dprb, dpib = dpr.astype(BF16), dpi.astype(BF16)
        dxa = dh * s * i
        dxa = dxa + jnp.concatenate(
            [_dot_nt(dprb[:, j * LRU_BLOCK:(j + 1) * LRU_BLOCK], wa_ref[j])
             + _dot_nt(dpib[:, j * LRU_BLOCK:(j + 1) * LRU_BLOCK], wx_ref[j]) for j in range(nblk)], axis=1)
        dxa_ref[...] = dxa.astype(BF16)
        for j in range(nblk):
            sl = slice(j * LRU_BLOCK, (j + 1) * LRU_BLOCK)
            dwa_ref[j] += _dot_tn(xa_v[:, sl], dprb[:, sl])
            dwx_ref[j] += _dot_tn(xa_v[:, sl], dpib[:, sl])
        dvec_ref[...] += jnp.concatenate(
            [_colsum(dpr), _colsum(dpi), _colsum(dlog_a * (-LRU_C * r)), jnp.zeros((5, D_MODEL), F32)], axis=0)

    hh = HALO // 2
    return _call(
        body, name="lru_bwd", grid=(nb, nk), side=side, sem=("arbitrary", "arbitrary"),
        args=(dya, xa, proj, h, h, wa_bd, wx_bd, vec),
        in_specs=[pl.BlockSpec((tc, D_MODEL), lambda b, k: (chunk(b, k), 0)),
                  pl.BlockSpec((tc, D_MODEL), lambda b, k: (chunk(b, k), 0)),
                  pl.BlockSpec((tc, D_MODEL), lambda b, k: (chunk(b, k), gb)),
                  pl.BlockSpec((tc, D_MODEL), lambda b, k: (chunk(b, k), 0)),
                  pl.BlockSpec((hh, D_MODEL), lambda b, k: (jnp.maximum(chunk(b, k) * (tc // hh) - 1, 0), 0)),
                  VMEM_FULL, VMEM_FULL, VMEM_FULL],
        out_specs=[pl.BlockSpec((tc, D_MODEL), lambda b, k: (chunk(b, k), 0)),
                   pl.BlockSpec((tc, D_MODEL), lambda b, k: (chunk(b, k), 0)),
                   pl.BlockSpec((nblk, LRU_BLOCK, LRU_BLOCK), lambda b, k: (0, 0, 0)),
                   pl.BlockSpec((nblk, LRU_BLOCK, LRU_BLOCK), lambda b, k: (0, 0, 0)),
                   pl.BlockSpec((8, D_MODEL), lambda b, k: (0, 0))],
        out_shape=[jax.ShapeDtypeStruct((t, D_MODEL), BF16), jax.ShapeDtypeStruct((t, D_MODEL), BF16),
                   jax.ShapeDtypeStruct((nblk, LRU_BLOCK, LRU_BLOCK), F32),
                   jax.ShapeDtypeStruct((nblk, LRU_BLOCK, LRU_BLOCK), F32),
                   jax.ShapeDtypeStruct((8, D_MODEL), F32)],
        scratch_shapes=[pltpu.VMEM((tc, D_MODEL), F32), pltpu.VMEM((tc, D_MODEL), F32), pltpu.VMEM((1, D_MODEL), F32)])


def merge_fwd(ya_in, yb_in, proj, x2, mod8, bgate, post1, w_pa, w_pb, w_out, seq):
    t = x2.shape[0]
    tm = min(512, seq)
    per_seq = seq // tm
    gcb = C_GATES // SSD_INNER

    def body(ya_ref, yb_ref, gt_ref, x_ref, mod_ref, bg_ref, post_ref, wpa_ref, wpb_ref, wo_ref,
             yab_ref, out1_ref, x1_ref):
        y_a = _dot(ya_ref[...], wpa_ref[...])
        y_b = _dot(yb_ref[...], wpb_ref[...])
        g = _sigmoid(gt_ref[...].astype(F32) + bg_ref[...])
        merged = g[:, :D_MODEL] * y_a + g[:, D_MODEL:] * y_b
        out1 = _dot(merged.astype(BF16), wo_ref[...])
        n = out1 * lax.rsqrt(_rowmean(out1 * out1) + EPS)
        yab_ref[...] = jnp.concatenate([y_a, y_b], axis=1).astype(BF16)
        out1_ref[...] = out1
        x1_ref[...] = x_ref[...] + mod_ref[0][2:3, :] * (n * post_ref[...])

    row = lambda w: pl.BlockSpec((tm, w), lambda i: (i, 0))
    return pl.pallas_call(
        body, name="merge_fwd", grid=(t // tm,),
        in_specs=[row(D_MODEL), row(SSD_INNER), pl.BlockSpec((tm, SSD_INNER), lambda i: (i, gcb)), row(D_MODEL),
                  pl.BlockSpec((1, 8, D_MODEL), lambda i: (i // per_seq, 0, 0)),
                  VMEM_FULL, VMEM_FULL, VMEM_FULL, VMEM_FULL, VMEM_FULL],
        out_specs=[row(SSD_INNER), row(D_MODEL), row(D_MODEL)],
        out_shape=[jax.ShapeDtypeStruct((t, SSD_INNER), BF16), jax.ShapeDtypeStruct((t, D_MODEL), F32),
                   jax.ShapeDtypeStruct((t, D_MODEL), F32)],
        compiler_params=_cp("parallel"),
    )(ya_in, yb_in, proj, x2, mod8, bgate, post1, w_pa, w_pb, w_out)


def merge_bwd(dx1, out1, yab, proj, mod8, bgate, post1, w_pa, w_pb, w_out, nb, seq):
    t = dx1.shape[0]
    tm = min(512, seq)
    per_seq = seq // tm
    gcb = C_GATES // SSD_INNER

    def body(dx1_ref, out1_ref, yab_ref, gt_ref, mod_ref, bg_ref, post_ref, wpa_ref, wpb_ref, wo_ref,
             dya_ref, dyb_ref, dgt_ref, dyab_ref, dout1_ref, mg_ref, vacc_ref, dmod_ref):
        b, s = pl.program_id(0), pl.program_id(1)

        @pl.when((b == 0) & (s == 0))
        def _():
            vacc_ref[...] = jnp.zeros_like(vacc_ref)

        @pl.when(s == 0)
        def _():
            dmod_ref[...] = jnp.zeros_like(dmod_ref)

        dx1v = dx1_ref[...]
        out1 = out1_ref[...]
        post = post_ref[...]
        rs = lax.rsqrt(_rowmean(out1 * out1) + EPS)
        n = out1 * rs
        do = dx1v * mod_ref[0][2:3, :]
        dn = do * post
        dout1 = rs * (dn - n * _rowmean(dn * n))
        dout1b = dout1.astype(BF16)
        dout1_ref[...] = dout1b
        dmerged = _dot_nt(dout1b, wo_ref[...])
        g = _sigmoid(gt_ref[...].astype(F32) + bg_ref[...])
        yab_v = yab_ref[...].astype(F32)
        gy = g * yab_v
        mg_ref[...] = (gy[:, :D_MODEL] + gy[:, D_MODEL:]).astype(BF16)
        dm2 = jnp.concatenate([dmerged, dmerged], axis=1)
        dyab = (dm2 * g).astype(BF16)
        dyab_ref[...] = dyab
        dgt = dm2 * gy * (1.0 - g)
        dgt_ref[...] = dgt.astype(BF16)
        dya_ref[...] = _dot_nt(dyab[:, :D_MODEL], wpa_ref[...]).astype(BF16)
        dyb_ref[...] = _dot_nt(dyab[:, D_MODEL:], wpb_ref[...]).astype(BF16)
        vacc_ref[...] += jnp.concatenate(
            [_colsum(dgt), jnp.concatenate([_colsum(do * n), jnp.zeros((1, D_MODEL), F32)], axis=1),
             jnp.zeros((6, SSD_INNER), F32)], axis=0)
        dmod_ref[0] += jnp.concatenate(
            [jnp.zeros((2, D_MODEL), F32), _colsum(dx1v * (n * post)), jnp.zeros((5, D_MODEL), F32)], axis=0)

    row = lambda w: pl.BlockSpec((tm, w), lambda b, s: (b * per_seq + s, 0))
    return pl.pallas_call(
        body, name="merge_bwd", grid=(nb, per_seq),
        in_specs=[row(D_MODEL), row(D_MODEL), row(SSD_INNER),
                  pl.BlockSpec((tm, SSD_INNER), lambda b, s: (b * per_seq + s, gcb)),
                  pl.BlockSpec((1, 8, D_MODEL), lambda b, s: (b, 0, 0)),
                  VMEM_FULL, VMEM_FULL, VMEM_FULL, VMEM_FULL, VMEM_FULL],
        out_specs=[row(D_MODEL), row(SSD_INNER), row(SSD_INNER), row(SSD_INNER), row(D_MODEL), row(D_MODEL),
                   pl.BlockSpec((8, SSD_INNER), lambda b, s: (0, 0)),
                   pl.BlockSpec((1, 8, D_MODEL), lambda b, s: (b, 0, 0))],
        out_shape=[jax.ShapeDtypeStruct((t, D_MODEL), BF16), jax.ShapeDtypeStruct((t, SSD_INNER), BF16),
                   jax.ShapeDtypeStruct((t, SSD_INNER), BF16), jax.ShapeDtypeStruct((t, SSD_INNER), BF16),
                   jax.ShapeDtypeStruct((t, D_MODEL), BF16), jax.ShapeDtypeStruct((t, D_MODEL), BF16),
                   jax.ShapeDtypeStruct((8, SSD_INNER), F32), jax.ShapeDtypeStruct((nb, 8, D_MODEL), F32)],
        compiler_params=_cp("arbitrary", "arbitrary"),
    )(dx1, out1, yab, proj, mod8, bgate, post1, w_pa, w_pb, w_out)


def mlp_fwd_bwd(x1, tgt, mod8, pre2, post2, w_ff1, w_ff2, nb, seq):
    t = x1.shape[0]
    tm = min(256, seq)
    per_seq = seq // tm
    fc = 1024
    nfc = D_FF // fc

    def body(x1_ref, tgt_ref, mod_ref, pre_ref, post_ref, w1_ref, w2_ref,
             dx1_ref, h2_ref, da1_ref, act_ref, dy2_ref, loss_ref, vacc_ref, dmod_ref, r_scr):
        b, s = pl.program_id(0), pl.program_id(1)

        @pl.when((b == 0) & (s == 0))
        def _():
            vacc_ref[...] = jnp.zeros_like(vacc_ref)
            loss_ref[...] = jnp.zeros_like(loss_ref)

        @pl.when(s == 0)
        def _():
            dmod_ref[...] = jnp.zeros_like(dmod_ref)

        m = mod_ref[0]
        sh2, sc2, g2 = m[3:4, :], m[4:5, :], m[5:6, :]
        pre, post = pre_ref[...], post_ref[...]
        x1v = x1_ref[...]
        rs1 = lax.rsqrt(_rowmean(x1v * x1v) + EPS)
        n1 = x1v * rs1
        y1 = n1 * pre
        h2b = (y1 * (1.0 + sc2) + sh2).astype(BF16)
        h2_ref[...] = h2b
        y2 = jnp.zeros((tm, D_MODEL), F32)
        for c in range(nfc):
            r = jnp.maximum(_dot(h2b, w1_ref[:, c * fc:(c + 1) * fc]), 0.0)
            r_scr[:, c * fc:(c + 1) * fc] = r
            a = (r * r).astype(BF16)
            act_ref[:, c * fc:(c + 1) * fc] = a
            y2 = y2 + _dot(a, w2_ref[c * fc:(c + 1) * fc, :])
        rs2 = lax.rsqrt(_rowmean(y2 * y2) + EPS)
        n2 = y2 * rs2
        o2 = n2 * post
        diff = x1v + g2 * o2 - tgt_ref[...]
        loss_ref[...] += 0.5 * jnp.sum(_rowmean(diff * diff))
        dx2 = diff * (1.0 / D_MODEL)
        do2 = dx2 * g2
        dn2 = do2 * post
        dy2b = (rs2 * (dn2 - n2 * _rowmean(dn2 * n2))).astype(BF16)
        dy2_ref[...] = dy2b
        dh2 = jnp.zeros((tm, D_MODEL), F32)
        for c in range(nfc):
            dact = _dot_nt(dy2b, w2_ref[c * fc:(c + 1) * fc, :])
            da = (dact * (2.0 * r_scr[:, c * fc:(c + 1) * fc])).astype(BF16)
            da1_ref[:, c * fc:(c + 1) * fc] = da
            dh2 = dh2 + _dot_nt(da, w1_ref[:, c * fc:(c + 1) * fc])
        dy1 = dh2 * (1.0 + sc2)
        dn1 = dy1 * pre
        dx1_ref[...] = dx2 + rs1 * (dn1 - n1 * _rowmean(dn1 * n1))
        vacc_ref[...] += jnp.concatenate([_colsum(dy1 * n1), _colsum(do2 * n2), jnp.zeros((6, D_MODEL), F32)], axis=0)
        dmod_ref[0] += jnp.concatenate(
            [jnp.zeros((3, D_MODEL), F32), _colsum(dh2), _colsum(dh2 * y1), _colsum(dx2 * o2),
             jnp.zeros((2, D_MODEL), F32)], axis=0)

    row = lambda w: pl.BlockSpec((tm, w), lambda b, s: (b * per_seq + s, 0))
    return pl.pallas_call(
        body, name="mlp_fwd_bwd", grid=(nb, per_seq),
        in_specs=[row(D_MODEL), row(D_MODEL), pl.BlockSpec((1, 8, D_MODEL), lambda b, s: (b, 0, 0)),
                  VMEM_FULL, VMEM_FULL, VMEM_FULL, VMEM_FULL],
        out_specs=[row(D_MODEL), row(D_MODEL), row(D_FF), row(D_FF), row(D_MODEL),
                   pl.BlockSpec((8, 128), lambda b, s: (0, 0)),
                   pl.BlockSpec((8, D_MODEL), lambda b, s: (0, 0)),
                   pl.BlockSpec((1, 8, D_MODEL), lambda b, s: (b, 0, 0))],
        out_shape=[jax.ShapeDtypeStruct((t, D_MODEL), F32), jax.ShapeDtypeStruct((t, D_MODEL), BF16),
                   jax.ShapeDtypeStruct((t, D_FF), BF16), jax.ShapeDtypeStruct((t, D_FF), BF16),
                   jax.ShapeDtypeStruct((t, D_MODEL), BF16), jax.ShapeDtypeStruct((8, 128), F32),
                   jax.ShapeDtypeStruct((8, D_MODEL), F32), jax.ShapeDtypeStruct((nb, 8, D_MODEL), F32)],
        scratch_shapes=[pltpu.VMEM((tm, D_FF), F32)],
        compiler_params=_cp("arbitrary", "arbitrary"),
    )(x1, tgt, mod8, pre2, post2, w_ff1, w_ff2)


_PIECES = ((C_LRU_X, 1024), (C_LRU_G, 1024), (C_Z, 2048), (C_XBC, 4096), (C_GATES, 2048))
_NP = len(_PIECES)


def _piece_of(k, tk):
    col = k * tk
    for p, (c0, w) in enumerate(_PIECES):
        if c0 <= col < c0 + w:
            return p, (col - c0) // tk
    raise ValueError(col)


def in_proj_bwd(pieces, ddt, dx1, x2, mod8, pre1, w_main, w_dt, nb, seq, side=None):
    t = x2.shape[0]
    tm = min(512, seq)
    per_seq = seq // tm
    tk = 1024
    nk = PROJ_MAIN // tk
    where = [_piece_of(k, tk) for k in range(nk)]

    def piece_spec(p):
        first = min(k for k in range(nk) if where[k][0] == p)
        nblk = _PIECES[p][1] // tk
        return pl.BlockSpec((tm, tk), lambda b, s, k: (b * per_seq + s, jnp.clip(k - first, 0, nblk - 1)))

    def body(*refs):
        prefs = refs[:_NP]
        ddt_ref, dx1_ref, x_ref, mod_ref, pre_ref, w_ref, wdt_ref, gx_ref, vacc_ref, dmod_ref, acc_ref = refs[_NP:]
        b, s, k = pl.program_id(0), pl.program_id(1), pl.program_id(2)

        @pl.when((b == 0) & (s == 0) & (k == 0))
        def _():
            vacc_ref[...] = jnp.zeros_like(vacc_ref)

        @pl.when((s == 0) & (k == 0))
        def _():
            dmod_ref[...] = jnp.zeros_like(dmod_ref)

        @pl.when(k == 0)
        def _():
            acc_ref[...] = _dot_nt(ddt_ref[...], wdt_ref[...])

        for kk in range(nk):
            @pl.when(k == kk)
            def _(kk=kk):
                acc_ref[...] += _dot_nt(prefs[where[kk][0]][...], w_ref[:, kk * tk:(kk + 1) * tk])

        @pl.when(k == nk - 1)
        def _():
            dh = acc_ref[...]
            m = mod_ref[0]
            pre = pre_ref[...]
            xv = x_ref[...]
            rs = lax.rsqrt(_rowmean(xv * xv) + EPS)
            n = xv * rs
            dy = dh * (1.0 + m[1:2, :])
            dn = dy * pre
            gx_ref[...] = dx1_ref[...] + rs * (dn - n * _rowmean(dn * n))
            vacc_ref[...] += jnp.concatenate([_colsum(dy * n), jnp.zeros((7, D_MODEL), F32)], axis=0)
            dmod_ref[0] += jnp.concatenate([_colsum(dh), _colsum(dh * (n * pre)), jnp.zeros((6, D_MODEL), F32)], axis=0)

    row = lambda w: pl.BlockSpec((tm, w), lambda b, s, k: (b * per_seq + s, 0))
    return _call(
        body, name="in_proj_bwd", grid=(nb, per_seq, nk), side=side, sem=("arbitrary", "arbitrary", "arbitrary"),
        args=(*pieces, ddt, dx1, x2, mod8, pre1, w_main, w_dt),
        in_specs=[piece_spec(p) for p in range(_NP)] + [
            row(128), row(D_MODEL), row(D_MODEL), pl.BlockSpec((1, 8, D_MODEL), lambda b, s, k: (b, 0, 0)),
            pl.BlockSpec((1, D_MODEL), lambda b, s, k: (0, 0)),
            VMEM_FULL,
            pl.BlockSpec((D_MODEL, 128), lambda b, s, k: (0, 0))],
        out_specs=[row(D_MODEL), pl.BlockSpec((8, D_MODEL), lambda b, s, k: (0, 0)),
                   pl.BlockSpec((1, 8, D_MODEL), lambda b, s, k: (b, 0, 0))],
        out_shape=[jax.ShapeDtypeStruct((t, D_MODEL), F32), jax.ShapeDtypeStruct((8, D_MODEL), F32),
                   jax.ShapeDtypeStruct((nb, 8, D_MODEL), F32)],
        scratch_shapes=[pltpu.VMEM((tm, D_MODEL), F32)])


def in_proj_wgrad(h1t, pieces, ddt, name="in_proj_wgrad"):
    nt, _, tt = h1t.shape
    tn = 1024
    nn = PROJ_MAIN // tn
    where = [_piece_of(n, tn) for n in range(nn)]

    def piece_spec(p):
        first = min(n for n in range(nn) if where[n][0] == p)
        nblk = _PIECES[p][1] // tn
        return pl.BlockSpec((tt, tn), lambda n, k: (jnp.where((n >= first) & (n < first + nblk), k, 0),
                                                    jnp.clip(n - first, 0, nblk - 1)))

    def body(h_ref, *refs):
        prefs = refs[:_NP]
        ddt_ref, dw_ref, dwdt_ref, acc_ref, accdt_ref = refs[_NP:]
        n, k = pl.program_id(0), pl.program_id(1)
        hv = h_ref[k]
        for nn_ in range(nn):
            @pl.when(n == nn_)
            def _(nn_=nn_):
                p = _dot(hv, prefs[where[nn_][0]][...])

                @pl.when(k == 0)
                def _():
                    acc_ref[...] = p

                @pl.when(k > 0)
                def _():
                    acc_ref[...] += p

        @pl.when(n == 0)
        def _():
            p = _dot(hv, ddt_ref[...])

            @pl.when(k == 0)
            def _():
                accdt_ref[...] = p

            @pl.when(k > 0)
            def _():
                accdt_ref[...] += p

        @pl.when(k == nt - 1)
        def _():
            dw_ref[...] = acc_ref[...].astype(BF16)

        @pl.when((n == 0) & (k == nt - 1))
        def _():
            dwdt_ref[...] = accdt_ref[...].astype(BF16)

    return pl.pallas_call(
        body, name=name, grid=(nn, nt),
        in_specs=[VMEM_FULL] + [piece_spec(p) for p in range(_NP)]
        + [pl.BlockSpec((tt, 128), lambda n, k: (k, 0))],
        out_specs=[pl.BlockSpec((D_MODEL, tn), lambda n, k: (0, n)), pl.BlockSpec((D_MODEL, 128), lambda n, k: (0, 0))],
        out_shape=[jax.ShapeDtypeStruct((D_MODEL, PROJ_MAIN), BF16), jax.ShapeDtypeStruct((D_MODEL, 128), BF16)],
        scratch_shapes=[pltpu.VMEM((D_MODEL, tn), F32), pltpu.VMEM((D_MODEL, 128), F32)],
        compiler_params=_cp("arbitrary", "arbitrary"),
    )(h1t, *pieces, ddt)


def _log1p(u):
    w = 1.0 + u
    return jnp.log(w) - ((w - 1.0) - u) / w


def _softplus(x):
    return jnp.maximum(x, 0.0) + _log1p(jnp.exp(-jnp.abs(x)))


def _head_mask(h):
    lane = lax.broadcasted_iota(jnp.int32, (1, SSD_GW), 1)
    return (lane >= SSD_P * h) & (lane < SSD_P * (h + 1))


def _pair(p):
    return slice(2 * SSD_P * p, 2 * SSD_P * (p + 1))


def _expand4(m, g):
    lane = lax.broadcasted_iota(jnp.int32, (1, SSD_GW), 1)
    col = lambda h: m[:, 4 * g + h:4 * g + h + 1]
    return jnp.where(lane < SSD_P, col(0), jnp.where(lane < 2 * SSD_P, col(1), jnp.where(lane < 3 * SSD_P, col(2), col(3))))


def _reduce4(v, g):
    lane = lax.broadcasted_iota(jnp.int32, (1, SSD_N), 1)
    out = jnp.zeros((v.shape[0], SSD_N), F32)
    for h in range(4):
        s = jnp.sum(jnp.where(_head_mask(h), v, 0.0), axis=1, keepdims=True)
        out = out + jnp.where(lane == 4 * g + h, s, 0.0)
    return out


def _ssd_heads(dtraw, hp, tri):
    xdt = dtraw + hp[0:1, :]
    dt = _softplus(xdt)
    cs = _dot_hi(tri, dt * hp[1:2, :])
    cs_last = cs[SSD_L - 1:SSD_L, :]
    return dict(xdt=xdt, dt=dt, cs=cs, cs_t=cs.T, e=jnp.exp(cs), w=jnp.exp(cs_last - cs), el=jnp.exp(cs_last))


def _ssd_group(g, hd, xs_b, bm_b, cm_b, d_x, st, paired=False):
    ll = SSD_L
    xs = xs_b.astype(F32)
    cs, cs_t = hd["cs"], hd["cs_t"]
    e_x, w_x, el_x, dt_x = _expand4(hd["e"], g), _expand4(hd["w"], g), _expand4(hd["el"], g), _expand4(hd["dt"], g)
    xd = xs * dt_x
    gcb = _dot_nt(cm_b, bm_b)
    ri = lax.broadcasted_iota(jnp.int32, (ll, ll), 0)
    ci = lax.broadcasted_iota(jnp.int32, (ll, ll), 1)
    dks, ms = [], []
    for h in range(4):
        k = 4 * g + h
        dk = jnp.exp(jnp.where(ri >= ci, cs[:, k:k + 1] - cs_t[k:k + 1, :], -1e30))
        dks.append(dk)
        ms.append((gcb * dk).astype(BF16))
    xdb = xd.astype(BF16)
    if paired:
        first = lax.broadcasted_iota(jnp.int32, (1, 2 * SSD_P), 1) < SSD_P
        ydiag = jnp.concatenate(
            [jnp.where(first, _dot(ms[2 * p], xdb[:, _pair(p)]), _dot(ms[2 * p + 1], xdb[:, _pair(p)]))
             for p in range(2)], axis=1)
    else:
        ydiag = jnp.zeros((ll, SSD_GW), F32)
        for h in range(4):
            ydiag = ydiag + _dot(ms[h], jnp.where(_head_mask(h), xd, 0.0).astype(BF16))
    yoff = _dot(cm_b, st.astype(BF16)) * e_x
    y = ydiag + yoff + d_x * xs
    st_new = st * el_x + _dot(bm_b.astype(F32).T.astype(BF16), (xd * w_x).astype(BF16))
    return dict(xs=xs, e_x=e_x, w_x=w_x, el_x=el_x, dt_x=dt_x, xd=xd, xdb=xdb, gcb=gcb, dks=dks, ms=ms, yoff=yoff, y=y,
                st_new=st_new)


def ssd_consts():
    hh = np.arange(SSD_N)
    tri = (hh[:, None] >= hh[None, :]).astype(np.float32)
    return jnp.asarray(tri), jnp.asarray(tri.T)


def ssd_params(dt_bias, a_log, d_skip, norm_w):
    padh = lambda v: jnp.pad(v.reshape(1, SSD_HEADS), ((0, 0), (0, SSD_N - SSD_HEADS)))
    hp = jnp.concatenate([padh(dt_bias), padh(-jnp.exp(a_log)), jnp.zeros((6, SSD_N), F32)], axis=0)
    lp = jnp.concatenate([norm_w.reshape(1, SSD_INNER), jnp.repeat(d_skip, SSD_P).reshape(1, SSD_INNER),
                          jnp.zeros((6, SSD_INNER), F32)], axis=0)
    return hp, lp


def _b_cols(g):
    return slice(SSD_INNER + g * SSD_N, SSD_INNER + (g + 1) * SSD_N)


def _c_cols(g):
    return slice(SSD_INNER + (SSD_G + g) * SSD_N, SSD_INNER + (SSD_G + g + 1) * SSD_N)


def _ssd_specs(nc, rc):
    return [pl.BlockSpec((SSD_L, 2 * SSD_INNER), lambda b, c: (b * nc + rc(c), 0)),
            pl.BlockSpec((SSD_L, SSD_INNER), lambda b, c: (b * nc + rc(c), C_Z // SSD_INNER)),
            pl.BlockSpec((SSD_L, SSD_N), lambda b, c: (b * nc + rc(c), 0))]


def ssd_fwd(xbc, proj, dtraw, hp, lp, tri, nb, seq):
    t = xbc.shape[0]
    nc = seq // SSD_L

    def body(xbc_ref, z_ref, dt_ref, hp_ref, lp_ref, tri_ref, y_ref, sts_ref, st_scr):
        @pl.when(pl.program_id(1) == 0)
        def _():
            st_scr[...] = jnp.zeros_like(st_scr)

        hd = _ssd_heads(dt_ref[...], hp_ref[...], tri_ref[...])
        for g in range(SSD_G):
            gs = slice(g * SSD_GW, (g + 1) * SSD_GW)
            st = st_scr[g]
            sts_ref[0, g] = st
            f = _ssd_group(g, hd, xbc_ref[:, gs], xbc_ref[:, _b_cols(g)], xbc_ref[:, _c_cols(g)], lp_ref[1:2, gs], st,
                           paired=True)
            st_scr[g] = f["st_new"]
            zf = z_ref[:, gs].astype(F32)
            yg = f["y"] * (zf * _sigmoid(zf))
            y_ref[:, gs] = (yg * lax.rsqrt(_rowmean(yg * yg) + EPS) * lp_ref[0:1, gs]).astype(BF16)

    return pl.pallas_call(
        body, name="ssd_fwd", grid=(nb, nc),
        in_specs=_ssd_specs(nc, lambda c: c) + [VMEM_FULL, VMEM_FULL, VMEM_FULL],
        out_specs=[pl.BlockSpec((SSD_L, SSD_INNER), lambda b, c: (b * nc + c, 0)),
                   pl.BlockSpec((1, SSD_G, SSD_N, SSD_GW), lambda b, c: (b * nc + c, 0, 0, 0))],
        out_shape=[jax.ShapeDtypeStruct((t, SSD_INNER), BF16),
                   jax.ShapeDtypeStruct((nb * nc, SSD_G, SSD_N, SSD_GW), F32)],
        scratch_shapes=[pltpu.VMEM((SSD_G, SSD_N, SSD_GW), F32)],
        compiler_params=_cp("arbitrary", "arbitrary"),
    )(xbc, proj, dtraw, hp, lp, tri)


def ssd_bwd(xbc, proj, dtraw, hp, lp, tri, triu, states, dyn, nb, seq):
    t = xbc.shape[0]
    nc = seq // SSD_L
    ll = SSD_L

    def body(xbc_ref, z_ref, dt_ref, sts_ref, dy_ref, hp_ref, lp_ref, tri_ref, triu_ref,
             dxbc_ref, dz_ref, ddt_ref, hpg_ref, lpg_ref, dst_scr):
        b, c_i = pl.program_id(0), pl.program_id(1)

        @pl.when((b == 0) & (c_i == 0))
        def _():
            hpg_ref[...] = jnp.zeros_like(hpg_ref)
            lpg_ref[...] = jnp.zeros_like(lpg_ref)

        @pl.when(c_i == 0)
        def _():
            dst_scr[...] = jnp.zeros_like(dst_scr)

        hp = hp_ref[...]
        hd = _ssd_heads(dt_ref[...], hp, tri_ref[...])
        lane = lax.broadcasted_iota(jnp.int32, (1, SSD_N), 1)
        subl = lax.broadcasted_iota(jnp.int32, (SSD_N, 1), 0)
        dcs = jnp.zeros((ll, SSD_N), F32)
        dcs_t = jnp.zeros((SSD_N, ll), F32)
        last = jnp.zeros((1, SSD_N), F32)
        dxx = jnp.zeros((ll, SSD_N), F32)
        for g in range(SSD_G):
            gs = slice(g * SSD_GW, (g + 1) * SSD_GW)
            st = sts_ref[0, g]
            dst = dst_scr[g]
            bm_b, cm_b = xbc_ref[:, _b_cols(g)], xbc_ref[:, _c_cols(g)]
            d_x = lp_ref[1:2, gs]
            f = _ssd_group(g, hd, xbc_ref[:, gs], bm_b, cm_b, d_x, st)
            xs, xd, gcb = f["xs"], f["xd"], f["gcb"]
            e_x, w_x, el_x, dt_x = f["e_x"], f["w_x"], f["el_x"], f["dt_x"]
            stb, dstb = st.astype(BF16), dst.astype(BF16)
            zf = z_ref[:, gs].astype(F32)
            sg = _sigmoid(zf)
            sz = zf * sg
            yg = f["y"] * sz
            rstd = lax.rsqrt(_rowmean(yg * yg) + EPS)
            n = yg * rstd
            dyn_v = dy_ref[:, gs].astype(F32)
            dn = dyn_v * lp_ref[0:1, gs]
            dyg = rstd * (dn - n * _rowmean(dn * n))
            dy = dyg * sz
            dz_ref[:, gs] = (dyg * f["y"] * (sg * (1.0 + zf * (1.0 - sg)))).astype(BF16)
            dyb = dy.astype(BF16)
            r_ = _dot(bm_b, dstb)
            dxd = w_x * r_
            dqb = (dy * e_x).astype(BF16)
            dcm = _dot_nt(dqb, stb)
            dst_scr[g] = dst * el_x + _dot_tn(cm_b, dqb)
            dbm = _dot_nt((xd * w_x).astype(BF16), dstb)
            xdb = f["xdb"]
            dgm = jnp.zeros((ll, ll), F32)
            for h in range(4):
                k = 4 * g + h
                hm = _head_mask(h)
                dxd = dxd + jnp.where(hm, _dot_tn(f["ms"][h], dyb), 0.0)
                dm = _dot_nt(jnp.where(hm, dy, 0.0).astype(BF16), xdb) * f["dks"][h]
                dgm = dgm + dm
                dseg = dm * gcb
                dcs = dcs + jnp.where(lane == k, jnp.sum(dseg, axis=1, keepdims=True), 0.0)
                dcs_t = dcs_t + jnp.where(subl == k, jnp.sum(dseg, axis=0, keepdims=True), 0.0)
            dgmb = dgm.astype(BF16)
            dxbc_ref[:, _c_cols(g)] = (dcm + _dot(dgmb, bm_b)).astype(BF16)
            dxbc_ref[:, _b_cols(g)] = (dbm + _dot_tn(dgmb, cm_b)).astype(BF16)
            v = _reduce4(r_ * xd * w_x, g)
            dcs = dcs + _reduce4(dy * f["yoff"], g) - v
            last = last + _colsum(v) + _reduce4(_colsum(dst * st) * el_x, g)
            dxx = dxx + _reduce4(dxd * xs, g)
            dxbc_ref[:, gs] = (d_x * dy + dxd * dt_x).astype(BF16)
            lpg_ref[0:1, gs] += _colsum(dyn_v * n)
            lpg_ref[1:2, gs] += _colsum(dy * xs)
        rowi = lax.broadcasted_iota(jnp.int32, (ll, 1), 0)
        da = _dot_hi(triu_ref[...], dcs - dcs_t.T + jnp.where(rowi == ll - 1, last, 0.0))
        ddt = (dxx + da * hp[1:2, :]) * _sigmoid(hd["xdt"])
        ddt_ref[...] = ddt
        hpg_ref[...] += jnp.concatenate([_colsum(ddt), _colsum(da * hd["dt"]), jnp.zeros((6, SSD_N), F32)], axis=0)

    rc = lambda c: nc - 1 - c
    return pl.pallas_call(
        body, name="ssd_bwd", grid=(nb, nc),
        in_specs=_ssd_specs(nc, rc) + [
            pl.BlockSpec((1, SSD_G, SSD_N, SSD_GW), lambda b, c: (b * nc + rc(c), 0, 0, 0)),
            pl.BlockSpec((SSD_L, SSD_INNER), lambda b, c: (b * nc + rc(c), 0)),
            VMEM_FULL, VMEM_FULL, VMEM_FULL, VMEM_FULL],
        out_specs=[pl.BlockSpec((SSD_L, 2 * SSD_INNER), lambda b, c: (b * nc + rc(c), 0)),
                   pl.BlockSpec((SSD_L, SSD_INNER), lambda b, c: (b * nc + rc(c), 0)),
                   pl.BlockSpec((SSD_L, SSD_N), lambda b, c: (b * nc + rc(c), 0)),
                   pl.BlockSpec((8, SSD_N), lambda b, c: (0, 0)),
                   pl.BlockSpec((8, SSD_INNER), lambda b, c: (0, 0))],
        out_shape=[jax.ShapeDtypeStruct((t, 2 * SSD_INNER), BF16), jax.ShapeDtypeStruct((t, SSD_INNER), BF16),
                   jax.ShapeDtypeStruct((t, SSD_N), F32), jax.ShapeDtypeStruct((8, SSD_N), F32),
                   jax.ShapeDtypeStruct((8, SSD_INNER), F32)],
        scratch_shapes=[pltpu.VMEM((SSD_G, SSD_N, SSD_GW), F32)],
        compiler_params=_cp("arbitrary", "arbitrary"),
    )(xbc, proj, dtraw, states, dyn, hp, lp, tri, triu)


def ada_fwd(c_all, w_cols, b_cols):
    def body(c_ref, w_ref, b_ref, o_ref):
        cv = c_ref[...]
        o_ref[...] = _dot_hi(cv * _sigmoid(cv), w_ref[...]) + b_ref[...]

    return pl.pallas_call(body, name="ada_fwd", out_shape=jax.ShapeDtypeStruct((c_all.shape[0], w_cols.shape[1]), F32),
                          compiler_params=pltpu.CompilerParams(vmem_limit_bytes=VMEM_LIMIT))(c_all, w_cols, b_cols)


def ada_bwd(c_all, dmod_cols, dmod_all):
    def body(c_ref, dc_ref, da_ref, gw_ref, gb_ref):
        cv = c_ref[...]
        gw_ref[...] = lax.dot_general(cv * _sigmoid(cv), dc_ref[...], (((0,), (0,)), ((), ())),
                                      precision=lax.Precision.HIGHEST, preferred_element_type=F32)
        gb_ref[...] = _colsum(da_ref[...])

    return pl.pallas_call(
        body, name="ada_bwd",
        out_shape=[jax.ShapeDtypeStruct((c_all.shape[1], dmod_cols.shape[1]), F32),
                   jax.ShapeDtypeStruct((1, dmod_all.shape[1]), F32)],
        compiler_params=pltpu.CompilerParams(vmem_limit_bytes=VMEM_LIMIT))(c_all, dmod_cols, dmod_all)


def _adam_update(g, w, m, v):
    m2 = ADAM_B1 * m + (1.0 - ADAM_B1) * g
    v2 = ADAM_B2 * v + (1.0 - ADAM_B2) * (g * g)
    m_hat = m2 / (1.0 - ADAM_B1 ** ADAM_STEP)
    v_hat = v2 / (1.0 - ADAM_B2 ** ADAM_STEP)
    return -ADAM_LR * (m_hat / (jnp.sqrt(v_hat) + ADAM_EPS) + ADAM_WD * w), m2, v2


def adamw(parts, w, m, v, name):
    n, r, c = parts.shape
    tr = r if r <= 256 else 128

    def body(p_ref, w_ref, m_ref, v_ref, g_ref, d_ref, nm_ref, nv_ref):
        g = p_ref[0].astype(F32)
        for s in range(1, n):
            g = g + p_ref[s].astype(F32)
        g_ref[0] = g
        d_ref[0], nm_ref[0], nv_ref[0] = _adam_update(g, w_ref[0], m_ref[0], v_ref[0])

    blk = pl.BlockSpec((1, tr, c), lambda i: (0, i, 0))
    return pl.pallas_call(
        body, name=name, grid=(r // tr,),
        in_specs=[pl.BlockSpec((n, tr, c), lambda i: (0, i, 0)), blk, blk, blk], out_specs=[blk] * 4,
        out_shape=[jax.ShapeDtypeStruct((1, r, c), F32)] * 4,
        compiler_params=_cp("parallel"),
    )(parts, w, m, v)


SMALL_SRC = {
    'pre_norm1': ('vin', 0, 1024), 'post_norm1': ('vmg', 1, 1024), 'b_gate': ('vmg', 0, 2048),
    'lru_conv_b': ('accl', 4, 1024), 'lru_wa': ('gwa', None, None), 'lru_ba': ('dvec', 0, 1024),
    'lru_wx': ('gwx', None, None), 'lru_bx': ('dvec', 1, 1024), 'lru_lambda': ('dvec', 2, 1024),
    'ssd_conv_b': ('accs', 4, 4096), 'ssd_dt_bias': ('hpg', 0, SSD_HEADS), 'ssd_a_log': ('hpg', 1, SSD_HEADS),
    'ssd_d': ('lpg', 1, SSD_INNER), 'ssd_norm_w': ('lpg', 0, SSD_INNER), 'pre_norm2': ('vmlp', 0, 1024),
    'post_norm2': ('vmlp', 1, 1024)}
SMALL_ACCS = ('vin', 'vmg', 'vmlp', 'dvec', 'accl', 'accs', 'hpg', 'lpg', 'gwa', 'gwx')


def adamw_small(gathered, params):
    names = tuple(params)
    na = len(SMALL_ACCS)

    def body(*refs):
        acc = {k: functools.reduce(lambda p, q: p + q, [refs[i][s] for s in range(NDEV)])
               for i, k in enumerate(SMALL_ACCS)}
        ins = refs[na:na + 3 * len(names)]
        outs = refs[na + 3 * len(names):]
        for j, k in enumerate(names):
            w_ref, m_ref, v_ref = ins[3 * j:3 * j + 3]
            src, row, width = SMALL_SRC[k]
            wv = w_ref[...]
            if row is None:
                g = acc[src]
            elif k == 'ssd_d':
                li = lax.broadcasted_iota(jnp.int32, (SSD_INNER, SSD_N), 0)
                hi = lax.broadcasted_iota(jnp.int32, (SSD_INNER, SSD_N), 1)
                g = _dot_hi(acc[src], jnp.where(jnp.right_shift(li, 6) == hi, 1.0, 0.0))[row:row + 1, :SSD_HEADS]
            else:
                g = acc[src][row:row + 1, :width]
            if k == 'lru_lambda':
                g = g * (-1.0 / (1.0 + jnp.exp(wv)))
            if k == 'ssd_a_log':
                g = g * (-jnp.exp(wv))
            o = outs[4 * j:4 * j + 4]
            o[0][...] = g
            o[1][...], o[2][...], o[3][...] = _adam_update(g, wv, m_ref[...], v_ref[...])
        outs[-2][...] = acc['accl'][0:4, :]
        outs[-1][...] = acc['accs'][0:4, :]

    flat = [a for k in names for a in params[k]]
    out_shape = [jax.ShapeDtypeStruct(params[k][0].shape, F32) for k in names for _ in range(4)]
    out_shape += [jax.ShapeDtypeStruct((4, D_MODEL), F32), jax.ShapeDtypeStruct((4, 2 * SSD_INNER), F32)]
    res = pl.pallas_call(body, name="adamw_small", out_shape=out_shape,
                         compiler_params=pltpu.CompilerParams(vmem_limit_bytes=VMEM_LIMIT))(
        *[gathered[k] for k in SMALL_ACCS], *flat)
    return {k: res[4 * j:4 * j + 4] for j, k in enumerate(names)}, res[-2], res[-1]


def _dev_index(px, py, pc):
    return 4 * px + 2 * py + pc


class _Exchange:
    def __init__(self, arrs):
        self.arrs = list(arrs)
        self.na = len(self.arrs)
        self.scratch = [pltpu.SemaphoreType.DMA((7 * self.na,)), pltpu.SemaphoreType.DMA((7 * self.na,)),
                        pltpu.SemaphoreType.DMA((self.na,))]


class Gather(_Exchange):
    def __init__(self, arrs):
        super().__init__(arrs)
        self.out_shape = [jax.ShapeDtypeStruct((NDEV,) + a.shape, a.dtype) for a in self.arrs]

    def _plan(self, ins, outs, sems):
        na = self.na
        send_sems, recv_sems, local_sems = sems
        x, y, c = lax.axis_index("x"), lax.axis_index("y"), lax.axis_index("c")
        me, sibling = (x, y, c), (x, y, 1 - c)
        chips = [(1 - x, y), (x, 1 - y), (1 - x, 1 - y)]

        def copy(a, k, block, to, src=None):
            dst = outs[a].at[_dev_index(*block)]
            return pltpu.make_async_remote_copy(
                src_ref=dst if src is None else src, dst_ref=dst, send_sem=send_sems.at[a * 7 + k],
                recv_sem=recv_sems.at[a * 7 + k], device_id=to, device_id_type=MESH)

        mine = [pltpu.make_async_copy(ins[a], outs[a].at[_dev_index(*me)], local_sems.at[a]) for a in range(na)]
        first = []
        for a in range(na):
            first.append(copy(a, 0, me, sibling, src=ins[a]))
            first += [copy(a, 1 + j, me, (*chip, c), src=ins[a]) for j, chip in enumerate(chips)]
        return copy, mine, first, me, sibling, chips, c

    def start(self, ins, outs, sems):
        _, mine, first, *_ = self._plan(ins, outs, sems)
        for cp in mine + first:
            cp.start()

    def finish(self, ins, outs, sems):
        copy, mine, first, me, sibling, chips, c = self._plan(ins, outs, sems)
        passed = []
        for j, chip in enumerate(chips):
            for a in range(self.na):
                copy(a, 1 + j, (*chip, c), me).wait_recv()
                cp = copy(a, 4 + j, (*chip, c), sibling)
                cp.start()
                passed.append(cp)
        for a in range(self.na):
            copy(a, 0, sibling, me).wait_recv()
            for j, chip in enumerate(chips):
                copy(a, 4 + j, (*chip, 1 - c), me).wait_recv()
        for cp in first + passed:
            cp.wait_send()
        for cp in mine:
            cp.wait()


class Scatter(_Exchange):
    def __init__(self, arrs):
        super().__init__(arrs)
        self.out_shape = [jax.ShapeDtypeStruct(a.shape, a.dtype) for a in self.arrs]

    def _plan(self, ins, outs, sems, arrivals):
        send_sems, recv_sems, local_sems = sems
        x, y, c = lax.axis_index("x"), lax.axis_index("y"), lax.axis_index("c")
        me = _dev_index(x, y, c)
        masks = [(mx, my, mc) for mx in (0, 1) for my in (0, 1) for mc in (0, 1)][1:]
        flip = lambda v, bit: 1 - v if bit else v
        mine = [pltpu.make_async_copy(ins[a].at[me], outs[a].at[me], local_sems.at[a]) for a in range(self.na)]
        sends, recvs = [], []
        for k, (mx, my, mc) in enumerate(masks):
            peer = (flip(x, mx), flip(y, my), flip(c, mc))
            pidx = _dev_index(*peer)
            for a in range(self.na):
                on = dict(send_sem=send_sems.at[a * 7 + k], recv_sem=recv_sems.at[a * 7 + k], device_id=peer,
                          device_id_type=MESH)
                sends.append(pltpu.make_async_remote_copy(src_ref=ins[a].at[pidx], dst_ref=outs[a].at[me], **on))
                if arrivals:
                    recvs.append(pltpu.make_async_remote_copy(src_ref=ins[a].at[pidx], dst_ref=outs[a].at[pidx], **on))
        return mine, sends, recvs

    def start(self, ins, outs, sems):
        mine, sends, _ = self._plan(ins, outs, sems, arrivals=False)
        for cp in mine + sends:
            cp.start()

    def finish(self, ins, outs, sems):
        mine, sends, recvs = self._plan(ins, outs, sems, arrivals=True)
        for cp in recvs:
            cp.wait_recv()
        for cp in sends:
            cp.wait_send()
        for cp in mine:
            cp.wait()


def exchange_call(ex, name):
    na = ex.na

    def body(*refs):
        ins, outs, sems = refs[:na], refs[na:2 * na], refs[2 * na:]
        ex.start(ins, outs, sems)
        ex.finish(ins, outs, sems)

    return pl.pallas_call(body, name=name, in_specs=[ANY] * na, out_specs=[ANY] * na, out_shape=ex.out_shape,
                          scratch_shapes=ex.scratch)(*ex.arrs)


def all_gather(arrs, name):
    return exchange_call(Gather(arrs), name)


def _call(body, *, name, grid, in_specs, out_specs, out_shape, scratch_shapes=(), sem, args, side=None):
    if side is None:
        outs = pl.pallas_call(body, name=name, grid=grid, in_specs=list(in_specs), out_specs=list(out_specs),
                              out_shape=list(out_shape), scratch_shapes=list(scratch_shapes),
                              compiler_params=_cp(*sem))(*args)
        return outs, []
    ni, no, ns, na = len(in_specs), len(out_specs), len(scratch_shapes), side.na

    def wrapped(*refs):
        ins, s_in = refs[:ni], refs[ni:ni + na]
        outs, s_out = refs[ni + na:ni + na + no], refs[ni + na + no:ni + 2 * na + no]
        scr, sems = refs[ni + 2 * na + no:ni + 2 * na + no + ns], refs[ni + 2 * na + no + ns:]
        pids = [pl.program_id(i) for i in range(len(grid))]
        first = functools.reduce(lambda p, q: p & q, [p == 0 for p in pids])
        last = functools.reduce(lambda p, q: p & q, [p == g - 1 for p, g in zip(pids, grid)])

        @pl.when(first)
        def _():
            side.start(s_in, s_out, sems)

        body(*ins, *outs, *scr)

        @pl.when(last)
        def _():
            side.finish(s_in, s_out, sems)

    outs = pl.pallas_call(
        wrapped, name=name, grid=grid, in_specs=list(in_specs) + [ANY] * na, out_specs=list(out_specs) + [ANY] * na,
        out_shape=list(out_shape) + side.out_shape, scratch_shapes=list(scratch_shapes) + side.scratch,
        compiler_params=_cp(*["arbitrary"] * len(grid)))(*args, *side.arrs)
    return outs[:no], outs[no:]


WEIGHTS = ('w_ada', 'b_ada', 'pre_norm1', 'post_norm1', 'w_in', 'b_gate', 'lru_conv_w', 'lru_conv_b', 'lru_wa',
           'lru_ba', 'lru_wx', 'lru_bx', 'lru_lambda', 'w_pa', 'ssd_conv_w', 'ssd_conv_b', 'ssd_dt_bias', 'ssd_a_log',
           'ssd_d', 'ssd_norm_w', 'w_pb', 'w_out', 'pre_norm2', 'post_norm2', 'w_ff1', 'w_ff2')
BIG = ('w_in', 'w_pa', 'w_pb', 'w_out', 'w_ff1', 'w_ff2')
REPL = ('pre_norm1', 'post_norm1', 'b_gate', 'lru_conv_b', 'lru_wa', 'lru_ba', 'lru_wx', 'lru_bx', 'lru_lambda',
        'ssd_conv_b', 'ssd_dt_bias', 'ssd_a_log', 'ssd_d', 'ssd_norm_w', 'pre_norm2', 'post_norm2')
LANES = 1024


def _rows(n):
    return -(-n // LANES)


def _pack(vals, total_rows):
    parts = []
    for v in vals:
        f = v.reshape(-1).astype(F32)
        parts.append(jnp.pad(f, (0, _rows(f.shape[0]) * LANES - f.shape[0])))
    flat = jnp.concatenate(parts)
    return jnp.pad(flat.reshape(-1, LANES), ((0, total_rows - flat.shape[0] // LANES), (0, 0)))


def _unpack(slab, shapes):
    out, r = [], 0
    for s in shapes:
        n = int(np.prod(s))
        out.append(slab[r:r + _rows(n)].reshape(-1)[:n].reshape(s))
        r += _rows(n)
    return out


def _block_diag4(w):
    w4 = w.reshape(4, 4, 64, 64)
    eye = jnp.eye(4, dtype=w.dtype)
    return (w4[:, :, :, None, :] * eye[None, :, None, :, None]).reshape(4, LRU_BLOCK, LRU_BLOCK)


def _diag_blocks4(m):
    m5 = m.reshape(4, 4, 64, 4, 64)
    return jnp.stack([m5[:, a, :, a, :] for a in range(4)], axis=1).reshape(LRU_HEADS, 64, 64)


def kernel(x, c, w_ada, b_ada, pre_norm1, post_norm1, w_in, b_gate, lru_conv_w, lru_conv_b, lru_wa, lru_ba, lru_wx, lru_bx, lru_lambda, w_pa, ssd_conv_w, ssd_conv_b, ssd_dt_bias, ssd_a_log, ssd_d, ssd_norm_w, w_pb, w_out, pre_norm2, post_norm2, w_ff1, w_ff2, loss_target, m_w_ada, m_b_ada, m_pre_norm1, m_post_norm1, m_w_in, m_b_gate, m_lru_conv_w, m_lru_conv_b, m_lru_wa, m_lru_ba, m_lru_wx, m_lru_bx, m_lru_lambda, m_w_pa, m_ssd_conv_w, m_ssd_conv_b, m_ssd_dt_bias, m_ssd_a_log, m_ssd_d, m_ssd_norm_w, m_w_pb, m_w_out, m_pre_norm2, m_post_norm2, m_w_ff1, m_w_ff2, v_w_ada, v_b_ada, v_pre_norm1, v_post_norm1, v_w_in, v_b_gate, v_lru_conv_w, v_lru_conv_b, v_lru_wa, v_lru_ba, v_lru_wx, v_lru_bx, v_lru_lambda, v_w_pa, v_ssd_conv_w, v_ssd_conv_b, v_ssd_dt_bias, v_ssd_a_log, v_ssd_d, v_ssd_norm_w, v_w_pb, v_w_out, v_pre_norm2, v_post_norm2, v_w_ff1, v_w_ff2):
    given = dict(locals())
    w = {k: given[k] for k in WEIGHTS}
    mom = {k: given["m_" + k] for k in WEIGHTS}
    var = {k: given["v_" + k] for k in WEIGHTS}
    nb, seq, _ = x.shape
    assert nb == 2 and seq % 512 == 0, (nb, seq)
    t = nb * seq
    me = _dev_index(lax.axis_index("x"), lax.axis_index("y"), lax.axis_index("c"))
    x2 = x.reshape(t, D_MODEL)
    tgt2 = loss_target.reshape(t, D_MODEL)
    ada_cols = w_ada.shape[2]

    slab = jnp.zeros((16, LANES), F32)
    slab = slab.at[0:nb].set(c)
    slab = slab.at[2:6, 0:lru_conv_w.shape[2]].set(lru_conv_w[0])
    slab = slab.at[6:10, 0:ssd_conv_w.shape[2]].set(ssd_conv_w[0])
    (g1,) = all_gather([slab], "gather_cond")
    c_all = g1[:, 0:nb].reshape(NDEV * nb, D_MODEL)
    lru_cw = g1[:, 2:6, 0:lru_conv_w.shape[2]].transpose(1, 0, 2).reshape(4, D_MODEL)
    ssd_cw = g1[:, 6:10, 0:ssd_conv_w.shape[2]].transpose(1, 0, 2).reshape(4, 2 * SSD_INNER)
    b_cols = lax.dynamic_slice(b_ada, (0, me * ada_cols), (1, ada_cols))
    mod_cols = ada_fwd(c_all, w_ada[0], b_cols)
    (g2,) = all_gather([mod_cols], "gather_mod")
    mod_all = g2.transpose(1, 0, 2).reshape(NDEV * nb, N_MOD * D_MODEL)
    mod_mine = lax.dynamic_slice(mod_all, (me * nb, 0), (nb, N_MOD * D_MODEL)).reshape(nb, N_MOD, D_MODEL)
    mod8 = jnp.pad(mod_mine, ((0, 0), (0, 8 - N_MOD), (0, 0)))

    (gw_in,) = all_gather([w_in.astype(BF16)], "gather_w_in")
    w_nat = gw_in[:, 0].transpose(1, 0, 2).reshape(D_MODEL, IN_DIM)
    w_main = jnp.concatenate([w_nat[:, :DT_COL0], w_nat[:, DT_COL0 + SSD_HEADS:]], axis=1)
    w_dt = jnp.pad(w_nat[:, DT_COL0:DT_COL0 + SSD_HEADS], ((0, 0), (0, 128 - SSD_HEADS)))

    wa_bd = _block_diag4(lru_wa[0]).astype(BF16)
    wx_bd = _block_diag4(lru_wx[0]).astype(BF16)
    lam = lru_lambda[0]
    vec = _pack([lru_ba, lru_bx, jax.nn.softplus(-lam)], 8)
    tri, triu = ssd_consts()
    hp, lp = ssd_params(ssd_dt_bias[0], ssd_a_log[0], ssd_d[0], ssd_norm_w[0])

    rest = Gather([w[k].astype(BF16) for k in BIG[1:]])
    (proj, h1t, dtraw), gw = in_proj_fwd(x2, mod8, pre_norm1, w_main, w_dt, seq, side=rest)
    w_pa_f = gw[0].reshape(D_MODEL, D_MODEL)
    w_pb_f = gw[1].reshape(SSD_INNER, D_MODEL)
    w_out_f = gw[2].reshape(D_MODEL, D_MODEL)
    w_ff1_f = gw[3][:, 0].transpose(1, 0, 2).reshape(D_MODEL, D_FF)
    w_ff2_f = gw[4].reshape(D_FF, D_MODEL)
    xa = conv_fwd(proj, C_LRU_X, D_MODEL, lru_cw, lru_conv_b, nb, seq, False, "conv_lru_fwd")
    xbc = conv_fwd(proj, C_XBC, 2 * SSD_INNER, ssd_cw, ssd_conv_b, nb, seq, True, "conv_ssd_fwd")
    ya_in, hst = lru_fwd(xa, proj, wa_bd, wx_bd, vec, nb, seq)
    yb_in, states = ssd_fwd(xbc, proj, dtraw, hp, lp, tri, nb, seq)
    yab, out1, x1 = merge_fwd(ya_in, yb_in, proj, x2, mod8, b_gate, post_norm1, w_pa_f, w_pb_f, w_out_f, seq)

    dx1, h2, da1, act, dy2, loss8, vacc_mlp, dmod_mlp = mlp_fwd_bwd(
        x1, tgt2, mod8, pre_norm2, post_norm2, w_ff1_f, w_ff2_f, nb, seq)
    wg = dict(out_dtype=BF16, ta=True, tm=1024, tn=1024, tk=1024)
    dw_ff1 = matmul(h2, da1, name="wgrad_ff1", **wg)
    dw_ff2 = matmul(act, dy2, name="wgrad_ff2", **wg)
    dya_in, dyb_in, dgates, dyab, dout1, merged, vacc_mg, dmod_mg = merge_bwd(
        dx1, out1, yab, proj, mod8, b_gate, post_norm1, w_pa_f, w_pb_f, w_out_f, nb, seq)
    dw_out = matmul(merged, dout1, name="wgrad_out", **wg)
    dw_pa = matmul(ya_in, dyab, name="wgrad_pa", n=D_MODEL, b_off=0, **wg)
    dw_pb = matmul(yb_in, dyab, name="wgrad_pb", n=D_MODEL, b_off=1, **wg)
    by_rows = lambda g: g.reshape(NDEV, g.shape[0] // NDEV, g.shape[1])
    by_cols = lambda g: g.reshape(g.shape[0], NDEV, g.shape[1] // NDEV).transpose(1, 0, 2)
    (dxa, dlg, dwa_bd, dwx_bd, dvec), parts_ff = lru_bwd(
        dya_in, xa, proj, hst, wa_bd, wx_bd, vec, nb, seq, side=Scatter([by_cols(dw_ff1), by_rows(dw_ff2)]))
    dxbc, dz, ddt, hpg, lpg = ssd_bwd(xbc, proj, dtraw, hp, lp, tri, triu, states, dyb_in, nb, seq)
    (dlx, acc_l), _ = conv_bwd(proj, C_LRU_X, D_MODEL, lru_cw, lru_conv_b, dxa, nb, seq, False, "conv_lru_bwd")
    (dxr, acc_s), parts_mg = conv_bwd(proj, C_XBC, 2 * SSD_INNER, ssd_cw, ssd_conv_b, dxbc, nb, seq, True,
                                      "conv_ssd_bwd", side=Scatter([by_rows(dw_pa), by_rows(dw_pb), by_rows(dw_out)]))
    pieces = (dlx, dlg, dz, dxr, dgates)
    ddt_b = ddt.astype(BF16)
    dw_main, dw_dt = in_proj_wgrad(h1t, pieces, ddt_b)
    dw_nat = jnp.concatenate([dw_main[:, :DT_COL0], dw_dt[:, :SSD_HEADS], dw_main[:, DT_COL0:]], axis=1)
    (grad_x, vacc_in, dmod_in), parts_in = in_proj_bwd(pieces, ddt_b, dx1, x2, mod8, pre_norm1, w_main, w_dt, nb, seq,
                                                       side=Scatter([by_cols(dw_nat)]))
    parts = dict(zip(BIG, (parts_in[0], *parts_mg, *parts_ff)))

    dmod = (dmod_in + dmod_mg + dmod_mlp)[:, :N_MOD].reshape(nb, N_MOD * D_MODEL)
    (g3,) = all_gather([jnp.pad(dmod, ((0, 8 - nb), (0, 0)))], "gather_dmod")
    dmod_all = g3[:, :nb].reshape(NDEV * nb, N_MOD * D_MODEL)
    dmod_cols = lax.dynamic_slice(dmod_all, (0, me * ada_cols), (NDEV * nb, ada_cols))
    g_w_ada, g_b_ada = ada_bwd(c_all, dmod_cols, dmod_all)

    res = {}
    for k in BIG:
        res[k] = adamw(parts[k], w[k], mom[k], var[k], "adamw_" + k)
    res['w_ada'] = adamw(g_w_ada[None], w_ada, m_w_ada, v_w_ada, "adamw_w_ada")

    accs = dict(vin=vacc_in, vmg=vacc_mg, vmlp=vacc_mlp, dvec=dvec, accl=acc_l, accs=acc_s, hpg=hpg, lpg=lpg,
                gwa=_diag_blocks4(dwa_bd).reshape(LRU_HEADS * 64, 64), gwx=_diag_blocks4(dwx_bd).reshape(LRU_HEADS * 64, 64))
    gathered = dict(zip(SMALL_ACCS, all_gather([accs[k] for k in SMALL_ACCS], "gather_small_grads")))
    view = lambda a: a.reshape(-1, a.shape[-1])
    res_a, g_lru_cw, g_ssd_cw = adamw_small(gathered, {k: (view(w[k]), view(mom[k]), view(var[k])) for k in REPL})
    res.update(res_a)
    lcw, scw = lru_conv_w.shape[2], ssd_conv_w.shape[2]
    sharded = {'b_ada': g_b_ada[None], 'lru_conv_w': lax.dynamic_slice(g_lru_cw, (0, me * lcw), (4, lcw))[None],
               'ssd_conv_w': lax.dynamic_slice(g_ssd_cw, (0, me * scw), (4, scw))[None]}
    for k, g in sharded.items():
        as3 = lambda a: a.reshape(g.shape)
        res[k] = adamw(g, as3(w[k]), as3(mom[k]), as3(var[k]), "adamw_" + k)

    loss = lax.psum(loss8[0, 0], ("x", "y", "c"))
    outs = [[res[k][j].reshape(w[k].shape) for k in WEIGHTS] for j in range(4)]
    return (loss, grad_x.reshape(x.shape), *outs[0], *outs[1], *outs[2], *outs[3])
```

```python
import functools

import numpy as np
import jax
import jax.numpy as jnp
from jax import lax
from jax.experimental import pallas as pl
from jax.experimental.pallas import tpu as pltpu

F32 = jnp.float32
BF16 = jnp.bfloat16

D_MODEL = 1024
LRU_HEADS = 16
LRU_BLOCK = 256
LRU_C = 8.0
SSD_INNER = 2048
SSD_HEADS = 32
SSD_P = 64
SSD_G = 8
SSD_N = 128
SSD_L = 128
SSD_GW = SSD_INNER // SSD_G
D_FF = 4096
N_MOD = 6
EPS = 1e-6
NDEV = 8

C_LRU_X, C_LRU_G, C_Z, C_XBC, C_GATES, PROJ_MAIN = 0, 1024, 2048, 4096, 8192, 10240
IN_DIM = 10272
DT_COL0 = 8192
HALO = 16

ADAM_LR, ADAM_B1, ADAM_B2, ADAM_EPS, ADAM_WD, ADAM_STEP = 0.001, 0.9, 0.999, 1e-08, 0.01, 10

VMEM_LIMIT = 60 * 1024 * 1024
MESH = pl.DeviceIdType.MESH
ANY = pl.BlockSpec(memory_space=pl.ANY)
VMEM_FULL = pl.BlockSpec(memory_space=pltpu.VMEM)


def _cp(*sem):
    return pltpu.CompilerParams(dimension_semantics=sem, vmem_limit_bytes=VMEM_LIMIT)


def _dot(a, b):
    return jnp.dot(a, b, preferred_element_type=F32)


def _dot_nt(a, b):
    return lax.dot_general(a, b, (((1,), (1,)), ((), ())), preferred_element_type=F32)


def _dot_tn(a, b):
    return lax.dot_general(a, b, (((0,), (0,)), ((), ())), preferred_element_type=F32)


def _dot_hi(a, b):
    return jnp.dot(a, b, precision=lax.Precision.HIGHEST, preferred_element_type=F32)


def _sigmoid(x):
    return 1.0 / (1.0 + jnp.exp(-x))


def _gelu_and_grad(x):
    k0, k1 = 0.7978845608028654, 0.044715
    t = jnp.tanh(k0 * (x + k1 * x * x * x))
    g = 0.5 * x * (1.0 + t)
    dg = 0.5 * (1.0 + t) + 0.5 * x * (1.0 - t * t) * k0 * (1.0 + 3.0 * k1 * x * x)
    return g, dg


def _neg_expm1(y):
    p = 1.0 + y * (1.0 / 7.0)
    p = 1.0 + y * (1.0 / 6.0) * p
    p = 1.0 + y * (1.0 / 5.0) * p
    p = 1.0 + y * (1.0 / 4.0) * p
    p = 1.0 + y * (1.0 / 3.0) * p
    p = 1.0 + y * 0.5 * p
    return jnp.where(y > -0.3, -y * p, 1.0 - jnp.exp(y))


def _colsum(v):
    return jnp.sum(v, axis=0, keepdims=True)


def _rowmean(v):
    return jnp.mean(v, axis=-1, keepdims=True)


def matmul(a, b, *, ta=False, tb=False, out_dtype=F32, tm, tn, tk, name, n=None, b_off=0):
    m = a.shape[1] if ta else a.shape[0]
    kdim = a.shape[0] if ta else a.shape[1]
    n = n or (b.shape[0] if tb else b.shape[1])
    tm, tn, tk = min(tm, m), min(tn, n), min(tk, kdim)
    nk = kdim // tk
    dn = (((0 if ta else 1,), (1 if tb else 0,)), ((), ()))

    def body(a_ref, b_ref, o_ref, acc_ref):
        k = pl.program_id(2)
        p = lax.dot_general(a_ref[...], b_ref[...], dn, preferred_element_type=F32)
        if nk == 1:
            o_ref[...] = p.astype(out_dtype)
        else:
            @pl.when(k == 0)
            def _():
                acc_ref[...] = p

            @pl.when(k > 0)
            def _():
                acc_ref[...] += p

            @pl.when(k == nk - 1)
            def _():
                o_ref[...] = acc_ref[...].astype(out_dtype)

    a_spec = pl.BlockSpec((tk, tm), lambda i, j, k: (k, i)) if ta else pl.BlockSpec((tm, tk), lambda i, j, k: (i, k))
    b_spec = (pl.BlockSpec((tn, tk), lambda i, j, k: (j, k)) if tb
              else pl.BlockSpec((tk, tn), lambda i, j, k: (k, j + b_off)))
    return pl.pallas_call(
        body, name=name, grid=(m // tm, n // tn, nk),
        in_specs=[a_spec, b_spec], out_specs=pl.BlockSpec((tm, tn), lambda i, j, k: (i, j)),
        out_shape=jax.ShapeDtypeStruct((m, n), out_dtype),
        scratch_shapes=[pltpu.VMEM((tm, tn), F32)],
        compiler_params=_cp("parallel", "parallel", "arbitrary"),
    )(a, b)


def _conv_tile(j, tn):
    return jnp.where(j == 0, 0, jnp.clip(j - C_XBC // tn + 1, 1, 2 * SSD_INNER // tn))


def in_proj_fwd(x2, mod8, pre1, w_main, w_dt, cw, cb, seq, side=None):
    t = x2.shape[0]
    tm = min(1024, seq)
    tn = 1024
    per_seq = seq // tm
    j_xbc = C_XBC // tn
    n_xbc = 2 * SSD_INNER // tn
    cs = 256

    def body(x_ref, mod_ref, pre_ref, w_ref, wdt_ref, cw_ref, cb_ref, proj_ref, h_ref, dt_ref, xa_ref, xbc_ref,
             h_scr, carry_scr):
        i, j = pl.program_id(0), pl.program_id(1)

        @pl.when(j == 0)
        def _():
            xv = x_ref[...]
            y = xv * lax.rsqrt(_rowmean(xv * xv) + EPS) * pre_ref[...]
            m = mod_ref[0]
            hf = y * (1.0 + m[1:2, :]) + m[0:1, :]
            h = hf.astype(BF16)
            h_scr[...] = h
            h_ref[...] = hf.T.astype(BF16)
            dt_ref[...] = _dot(h, wdt_ref[...])

        def project(c0=0, width=tn):
            pb = _dot(h_scr[...], w_ref[:, c0:c0 + width]).astype(BF16)
            proj_ref[:, c0:c0 + width] = pb
            return pb

        def conv(o_ref, slot, act):
            first = lax.rem(i, per_seq) == 0
            for c0 in range(0, tn, cs):
                cur = project(c0, cs).astype(F32)
                prev = jnp.where(first, 0.0, carry_scr[slot, :, c0:c0 + cs])
                carry_scr[slot, :, c0:c0 + cs] = cur[tm - HALO:, :]
                xx = jnp.concatenate([prev, cur], axis=0)
                w = cw_ref[:, c0:c0 + cs]
                acc = cur * w[3:4, :] + cb_ref[:, c0:c0 + cs]
                for d in (1, 2, 3):
                    acc = acc + pltpu.roll(xx, d, axis=0)[HALO:, :] * w[3 - d:4 - d, :]
                if act:
                    acc = acc * _sigmoid(acc)
                o_ref[:, c0:c0 + cs] = acc.astype(BF16)

        is_xbc = (j >= j_xbc) & (j < j_xbc + n_xbc)

        @pl.when(j == 0)
        def _():
            conv(xa_ref, 0, False)

        @pl.when(is_xbc)
        def _():
            conv(xbc_ref, j - j_xbc + 1, True)

        @pl.when((j > 0) & jnp.logical_not(is_xbc))
        def _():
            project()

    return _call(
        body, name="in_proj_fwd", grid=(t // tm, PROJ_MAIN // tn), side=side, sem=("arbitrary", "arbitrary"),
        args=(x2, mod8, pre1, w_main, w_dt, cw, cb),
        in_specs=[pl.BlockSpec((tm, D_MODEL), lambda i, j: (i, 0)),
                  pl.BlockSpec((1, 8, D_MODEL), lambda i, j: (i // per_seq, 0, 0)),
                  pl.BlockSpec((1, D_MODEL), lambda i, j: (0, 0)),
                  pl.BlockSpec((D_MODEL, tn), lambda i, j: (0, j)),
                  pl.BlockSpec((D_MODEL, 128), lambda i, j: (0, 0)),
                  pl.BlockSpec((4, tn), lambda i, j: (0, _conv_tile(j, tn))),
                  pl.BlockSpec((1, tn), lambda i, j: (0, _conv_tile(j, tn)))],
        out_specs=[pl.BlockSpec((tm, tn), lambda i, j: (i, j)),
                   pl.BlockSpec((None, D_MODEL, tm), lambda i, j: (i, 0, 0)),
                   pl.BlockSpec((tm, 128), lambda i, j: (i, 0)),
                   pl.BlockSpec((tm, tn), lambda i, j: (i, 0)),
                   pl.BlockSpec((tm, tn), lambda i, j: (i, jnp.clip(j - j_xbc, 0, n_xbc - 1)))],
        out_shape=[jax.ShapeDtypeStruct((t, PROJ_MAIN), BF16), jax.ShapeDtypeStruct((t // tm, D_MODEL, tm), BF16),
                   jax.ShapeDtypeStruct((t, 128), F32), jax.ShapeDtypeStruct((t, D_MODEL), BF16),
                   jax.ShapeDtypeStruct((t, 2 * SSD_INNER), BF16)],
        scratch_shapes=[pltpu.VMEM((tm, D_MODEL), BF16), pltpu.VMEM((1 + n_xbc, HALO, tn), F32)])


def conv_bwd(src, col0, width, w4, bias, dout, nb, seq, act, name, side=None):
    t = src.shape[0]
    tt = min(512, seq)
    tc = 512
    ns = seq // tt
    cb0 = col0 // tc
    nh = t // HALO

    def body(cur_ref, prev_ref, next_ref, w_ref, b_ref, do_ref, don_ref, dx_ref, acc_ref):
        b, s = pl.program_id(1), pl.program_id(2)

        @pl.when((b == 0) & (s == 0))
        def _():
            acc_ref[...] = jnp.zeros_like(acc_ref)

        cur = cur_ref[...].astype(F32)
        prev = jnp.where(s == 0, 0.0, prev_ref[...].astype(F32))
        nxt = next_ref[...].astype(F32)
        xx = jnp.concatenate([prev, cur, nxt], axis=0)
        w = w_ref[...]
        do_ext = jnp.concatenate([do_ref[...].astype(F32),
                                  jnp.where(s == ns - 1, 0.0, don_ref[...].astype(F32))], axis=0)
        ne = tt + HALO
        xs = [xx[HALO:HALO + ne, :]] + [pltpu.roll(xx, d, axis=0)[HALO:HALO + ne, :] for d in (1, 2, 3)]
        if act:
            c = b_ref[...] + xs[0] * w[3:4, :] + xs[1] * w[2:3, :] + xs[2] * w[1:2, :] + xs[3] * w[0:1, :]
            sg = _sigmoid(c)
            dc = do_ext * (sg * (1.0 + c * (1.0 - sg)))
        else:
            dc = do_ext
        dx = dc[:tt, :] * w[3:4, :]
        for d in (1, 2, 3):
            dx = dx + pltpu.roll(dc, ne - d, axis=0)[:tt, :] * w[3 - d:4 - d, :]
        dx_ref[...] = dx.astype(BF16)
        dcc = dc[:tt, :]
        rows = [_colsum(dcc * xs[3 - r][:tt, :]) for r in range(4)] + [_colsum(dcc)]
        acc_ref[...] += jnp.concatenate(rows + [jnp.zeros((3, tc), F32)], axis=0)

    return _call(
        body, name=name, grid=(width // tc, nb, ns), side=side, sem=("parallel", "arbitrary", "arbitrary"),
        args=(src, src, src, w4, bias, dout, dout),
        in_specs=[pl.BlockSpec((tt, tc), lambda j, b, s: (b * ns + s, cb0 + j)),
                  pl.BlockSpec((HALO, tc), lambda j, b, s: (jnp.maximum((b * seq + s * tt) // HALO - 1, 0), cb0 + j)),
                  pl.BlockSpec((HALO, tc), lambda j, b, s: (jnp.minimum((b * seq + (s + 1) * tt) // HALO, nh - 1), cb0 + j)),
                  pl.BlockSpec((4, tc), lambda j, b, s: (0, j)),
                  pl.BlockSpec((1, tc), lambda j, b, s: (0, j)),
                  pl.BlockSpec((tt, tc), lambda j, b, s: (b * ns + s, j)),
                  pl.BlockSpec((HALO, tc), lambda j, b, s: (jnp.minimum((b * seq + (s + 1) * tt) // HALO, nh - 1), j))],
        out_specs=[pl.BlockSpec((tt, tc), lambda j, b, s: (b * ns + s, j)),
                   pl.BlockSpec((8, tc), lambda j, b, s: (0, j))],
        out_shape=[jax.ShapeDtypeStruct((t, width), BF16), jax.ShapeDtypeStruct((8, width), F32)])


def _lru_gates(xa, wa_ref, wx_ref, ba, bx, sp):
    nblk = D_MODEL // LRU_BLOCK
    pr = jnp.concatenate([_dot(xa[:, j * LRU_BLOCK:(j + 1) * LRU_BLOCK], wa_ref[j]) for j in range(nblk)], axis=1) + ba
    pi = jnp.concatenate([_dot(xa[:, j * LRU_BLOCK:(j + 1) * LRU_BLOCK], wx_ref[j]) for j in range(nblk)], axis=1) + bx
    r = _sigmoid(pr)
    i = _sigmoid(pi)
    log_a = (-LRU_C * r) * sp
    return r, i, jnp.exp(log_a), _neg_expm1(2.0 * log_a)


def lru_fwd(xa, proj, wa_bd, wx_bd, vec, nb, seq):
    t = xa.shape[0]
    tc = min(512, seq)
    nk = seq // tc
    gb = C_LRU_G // D_MODEL

    def body(xa_ref, g_ref, wa_ref, wx_ref, vec_ref, ya_ref, h_ref, a_scr, u_scr, hc_scr):
        @pl.when(pl.program_id(1) == 0)
        def _():
            hc_scr[...] = jnp.zeros_like(hc_scr)

        xa_v = xa_ref[...]
        v = vec_ref[...]
        r, i, a, e = _lru_gates(xa_v, wa_ref, wx_ref, v[0:1, :], v[1:2, :], v[2:3, :])
        a_scr[...] = a
        u_scr[...] = jnp.sqrt(e) * (i * xa_v.astype(F32))
        row = lax.broadcasted_iota(jnp.int32, (8, 1), 0)

        def tile(j, h):
            r0 = pl.multiple_of(j * 8, 8)
            av, uv = a_scr[pl.ds(r0, 8), :], u_scr[pl.ds(r0, 8), :]
            for d in (1, 2, 4):
                uv = uv + av * jnp.where(row >= d, pltpu.roll(uv, d, axis=0), 0.0)
                av = av * jnp.where(row >= d, pltpu.roll(av, d, axis=0), 1.0)
            hv = uv + av * h
            h_ref[pl.ds(r0, 8), :] = hv
            return hv[7:8, :]

        hc_scr[...] = lax.fori_loop(0, tc // 8, tile, hc_scr[...], unroll=2)
        gel, _ = _gelu_and_grad(g_ref[...].astype(F32))
        ya_ref[...] = (h_ref[...] * gel).astype(BF16)

    return pl.pallas_call(
        body, name="lru_fwd", grid=(nb, nk),
        in_specs=[pl.BlockSpec((tc, D_MODEL), lambda b, k: (b * nk + k, 0)),
                  pl.BlockSpec((tc, D_MODEL), lambda b, k: (b * nk + k, gb)),
                  VMEM_FULL, VMEM_FULL, VMEM_FULL],
        out_specs=[pl.BlockSpec((tc, D_MODEL), lambda b, k: (b * nk + k, 0)),
                   pl.BlockSpec((tc, D_MODEL), lambda b, k: (b * nk + k, 0))],
        out_shape=[jax.ShapeDtypeStruct((t, D_MODEL), BF16), jax.ShapeDtypeStruct((t, D_MODEL), F32)],
        scratch_shapes=[pltpu.VMEM((tc, D_MODEL), F32), pltpu.VMEM((tc, D_MODEL), F32), pltpu.VMEM((1, D_MODEL), F32)],
        compiler_params=_cp("arbitrary", "arbitrary"),
    )(xa, proj, wa_bd, wx_bd, vec)


def lru_bwd(dya, xa, proj, h, wa_bd, wx_bd, vec, nb, seq, side=None):
    t = xa.shape[0]
    tc = min(512, seq)
    nk = seq // tc
    gb = C_LRU_G // D_MODEL
    nblk = D_MODEL // LRU_BLOCK

    def chunk(b, k):
        return b * nk + (nk - 1 - k)

    def body(dya_ref, xa_ref, g_ref, h_ref, hp_ref, wa_ref, wx_ref, vec_ref,
             dxa_ref, dg_ref, dwa_ref, dwx_ref, dvec_ref, a_scr, dh_scr, c_scr):
        b, k = pl.program_id(0), pl.program_id(1)

        @pl.when((b == 0) & (k == 0))
        def _():
            dwa_ref[...] = jnp.zeros_like(dwa_ref)
            dwx_ref[...] = jnp.zeros_like(dwx_ref)
            dvec_ref[...] = jnp.zeros_like(dvec_ref)

        @pl.when(k == 0)
        def _():
            c_scr[...] = jnp.zeros_like(c_scr)

        xa_v = xa_ref[...]
        xaf = xa_v.astype(F32)
        v = vec_ref[...]
        sp = v[2:3, :]
        r, i, a, e = _lru_gates(xa_v, wa_ref, wx_ref, v[0:1, :], v[1:2, :], sp)
        gel, dgel = _gelu_and_grad(g_ref[...].astype(F32))
        hv = h_ref[...]
        dyv = dya_ref[...].astype(F32)
        dg_ref[...] = (dyv * hv * dgel).astype(BF16)
        a_scr[...] = a
        dh_scr[...] = dyv * gel

        row8 = lax.broadcasted_iota(jnp.int32, (8, 1), 0)

        def tile(j, c):
            r0 = pl.multiple_of((tc // 8 - 1 - j) * 8, 8)
            av, dout = a_scr[pl.ds(r0, 8), :], dh_scr[pl.ds(r0, 8), :]
            zv = av * dout
            for d in (1, 2, 4):
                zv = zv + av * jnp.where(row8 < 8 - d, pltpu.roll(zv, 8 - d, axis=0), 0.0)
                av = av * jnp.where(row8 < 8 - d, pltpu.roll(av, 8 - d, axis=0), 1.0)
            zv = zv + av * c
            dh_scr[pl.ds(r0, 8), :] = dout + jnp.where(row8 < 7, pltpu.roll(zv, 7, axis=0), c)
            return zv[0:1, :]

        c_scr[...] = lax.fori_loop(0, tc // 8, tile, c_scr[...], unroll=2)
        dh = dh_scr[...]
        h_last = jnp.where(k == nk - 1, 0.0, hp_ref[HALO // 2 - 1:HALO // 2, :])
        row = lax.broadcasted_iota(jnp.int32, (tc, 1), 0)
        h_prev = jnp.where(row == 0, h_last, pltpu.roll(hv, 1, axis=0))
        s = jnp.sqrt(e)
        da = dh * h_prev
        ix = i * xaf
        dlog_a = da * a - (dh * ix) * (a * a) * lax.rsqrt(jnp.maximum(e, 1e-30))
        di = dh * s * xaf
        dpr = (dlog_a * (-LRU_C * sp)) * (r * (1.0 - r))
        dpi = di * (i * (1.0 - i))
        dprb, dpib = dpr.astype(BF16), dpi.astype(BF16)
        dxa = dh * s * i
        dxa = dxa + jnp.concatenate(
            [_dot_nt(dprb[:, j * LRU_BLOCK:(j + 1) * LRU_BLOCK], wa_ref[j])
             + _dot_nt(dpib[:, j * LRU_BLOCK:(j + 1) * LRU_BLOCK], wx_ref[j]) for j in range(nblk)], axis=1)
        dxa_ref[...] = dxa.astype(BF16)
        for j in range(nblk):
            sl = slice(j * LRU_BLOCK, (j + 1) * LRU_BLOCK)
            dwa_ref[j] += _dot_tn(xa_v[:, sl], dprb[:, sl])
            dwx_ref[j] += _dot_tn(xa_v[:, sl], dpib[:, sl])
        dvec_ref[...] += jnp.concatenate(
            [_colsum(dpr), _colsum(dpi), _colsum(dlog_a * (-LRU_C * r)), jnp.zeros((5, D_MODEL), F32)], axis=0)

    hh = HALO // 2
    return _call(
        body, name="lru_bwd", grid=(nb, nk), side=side, sem=("arbitrary", "arbitrary"),
        args=(dya, xa, proj, h, h, wa_bd, wx_bd, vec),
        in_specs=[pl.BlockSpec((tc, D_MODEL), lambda b, k: (chunk(b, k), 0)),
                  pl.BlockSpec((tc, D_MODEL), lambda b, k: (chunk(b, k), 0)),
                  pl.BlockSpec((tc, D_MODEL), lambda b, k: (chunk(b, k), gb)),
                  pl.BlockSpec((tc, D_MODEL), lambda b, k: (chunk(b, k), 0)),
                  pl.BlockSpec((hh, D_MODEL), lambda b, k: (jnp.maximum(chunk(b, k) * (tc // hh) - 1, 0), 0)),
                  VMEM_FULL, VMEM_FULL, VMEM_FULL],
        out_specs=[pl.BlockSpec((tc, D_MODEL), lambda b, k: (chunk(b, k), 0)),
                   pl.BlockSpec((tc, D_MODEL), lambda b, k: (chunk(b, k), 0)),
                   pl.BlockSpec((nblk, LRU_BLOCK, LRU_BLOCK), lambda b, k: (0, 0, 0)),
                   pl.BlockSpec((nblk, LRU_BLOCK, LRU_BLOCK), lambda b, k: (0, 0, 0)),
                   pl.BlockSpec((8, D_MODEL), lambda b, k: (0, 0))],
        out_shape=[jax.ShapeDtypeStruct((t, D_MODEL), BF16), jax.ShapeDtypeStruct((t, D_MODEL), BF16),
                   jax.ShapeDtypeStruct((nblk, LRU_BLOCK, LRU_BLOCK), F32),
                   jax.ShapeDtypeStruct((nblk, LRU_BLOCK, LRU_BLOCK), F32),
                   jax.ShapeDtypeStruct((8, D_MODEL), F32)],
        scratch_shapes=[pltpu.VMEM((tc, D_MODEL), F32), pltpu.VMEM((tc, D_MODEL), F32), pltpu.VMEM((1, D_MODEL), F32)])


def merge_fwd(ya_in, yb_in, proj, x2, mod8, bgate, post1, w_pa, w_pb, w_out, seq):
    t = x2.shape[0]
    tm = min(512, seq)
    per_seq = seq // tm
    gcb = C_GATES // SSD_INNER

    def body(ya_ref, yb_ref, gt_ref, x_ref, mod_ref, bg_ref, post_ref, wpa_ref, wpb_ref, wo_ref,
             yab_ref, out1_ref, x1_ref):
        y_a = _dot(ya_ref[...], wpa_ref[...])
        y_b = _dot(yb_ref[...], wpb_ref[...])
        g = _sigmoid(gt_ref[...].astype(F32) + bg_ref[...])
        merged = g[:, :D_MODEL] * y_a + g[:, D_MODEL:] * y_b
        out1 = _dot(merged.astype(BF16), wo_ref[...])
        n = out1 * lax.rsqrt(_rowmean(out1 * out1) + EPS)
        yab_ref[...] = jnp.concatenate([y_a, y_b], axis=1).astype(BF16)
        out1_ref[...] = out1
        x1_ref[...] = x_ref[...] + mod_ref[0][2:3, :] * (n * post_ref[...])

    row = lambda w: pl.BlockSpec((tm, w), lambda i: (i, 0))
    return pl.pallas_call(
        body, name="merge_fwd", grid=(t // tm,),
        in_specs=[row(D_MODEL), row(SSD_INNER), pl.BlockSpec((tm, SSD_INNER), lambda i: (i, gcb)), row(D_MODEL),
                  pl.BlockSpec((1, 8, D_MODEL), lambda i: (i // per_seq, 0, 0)),
                  VMEM_FULL, VMEM_FULL, VMEM_FULL, VMEM_FULL, VMEM_FULL],
        out_specs=[row(SSD_INNER), row(D_MODEL), row(D_MODEL)],
        out_shape=[jax.ShapeDtypeStruct((t, SSD_INNER), BF16), jax.ShapeDtypeStruct((t, D_MODEL), F32),
                   jax.ShapeDtypeStruct((t, D_MODEL), F32)],
        compiler_params=_cp("parallel"),
    )(ya_in, yb_in, proj, x2, mod8, bgate, post1, w_pa, w_pb, w_out)


def merge_bwd(dx1, out1, yab, proj, mod8, bgate, post1, w_pa, w_pb, w_out, nb, seq):
    t = dx1.shape[0]
    tm = min(512, seq)
    per_seq = seq // tm
    gcb = C_GATES // SSD_INNER

    def body(dx1_ref, out1_ref, yab_ref, gt_ref, mod_ref, bg_ref, post_ref, wpa_ref, wpb_ref, wo_ref,
             dya_ref, dyb_ref, dgt_ref, dyab_ref, dout1_ref, mg_ref, vacc_ref, dmod_ref):
        b, s = pl.program_id(0), pl.program_id(1)

        @pl.when((b == 0) & (s == 0))
        def _():
            vacc_ref[...] = jnp.zeros_like(vacc_ref)

        @pl.when(s == 0)
        def _():
            dmod_ref[...] = jnp.zeros_like(dmod_ref)

        dx1v = dx1_ref[...]
        out1 = out1_ref[...]
        post = post_ref[...]
        rs = lax.rsqrt(_rowmean(out1 * out1) + EPS)
        n = out1 * rs
        do = dx1v * mod_ref[0][2:3, :]
        dn = do * post
        dout1 = rs * (dn - n * _rowmean(dn * n))
        dout1b = dout1.astype(BF16)
        dout1_ref[...] = dout1b
        dmerged = _dot_nt(dout1b, wo_ref[...])
        g = _sigmoid(gt_ref[...].astype(F32) + bg_ref[...])
        yab_v = yab_ref[...].astype(F32)
        gy = g * yab_v
        mg_ref[...] = (gy[:, :D_MODEL] + gy[:, D_MODEL:]).astype(BF16)
        dm2 = jnp.concatenate([dmerged, dmerged], axis=1)
        dyab = (dm2 * g).astype(BF16)
        dyab_ref[...] = dyab
        dgt = dm2 * gy * (1.0 - g)
        dgt_ref[...] = dgt.astype(BF16)
        dya_ref[...] = _dot_nt(dyab[:, :D_MODEL], wpa_ref[...]).astype(BF16)
        dyb_ref[...] = _dot_nt(dyab[:, D_MODEL:], wpb_ref[...]).astype(BF16)
        vacc_ref[...] += jnp.concatenate(
            [_colsum(dgt), jnp.concatenate([_colsum(do * n), jnp.zeros((1, D_MODEL), F32)], axis=1),
             jnp.zeros((6, SSD_INNER), F32)], axis=0)
        dmod_ref[0] += jnp.concatenate(
            [jnp.zeros((2, D_MODEL), F32), _colsum(dx1v * (n * post)), jnp.zeros((5, D_MODEL), F32)], axis=0)

    row = lambda w: pl.BlockSpec((tm, w), lambda b, s: (b * per_seq + s, 0))
    return pl.pallas_call(
        body, name="merge_bwd", grid=(nb, per_seq),
        in_specs=[row(D_MODEL), row(D_MODEL), row(SSD_INNER),
                  pl.BlockSpec((tm, SSD_INNER), lambda b, s: (b * per_seq + s, gcb)),
                  pl.BlockSpec((1, 8, D_MODEL), lambda b, s: (b, 0, 0)),
                  VMEM_FULL, VMEM_FULL, VMEM_FULL, VMEM_FULL, VMEM_FULL],
        out_specs=[row(D_MODEL), row(SSD_INNER), row(SSD_INNER), row(SSD_INNER), row(D_MODEL), row(D_MODEL),
                   pl.BlockSpec((8, SSD_INNER), lambda b, s: (0, 0)),
                   pl.BlockSpec((1, 8, D_MODEL), lambda b, s: (b, 0, 0))],
        out_shape=[jax.ShapeDtypeStruct((t, D_MODEL), BF16), jax.ShapeDtypeStruct((t, SSD_INNER), BF16),
                   jax.ShapeDtypeStruct((t, SSD_INNER), BF16), jax.ShapeDtypeStruct((t, SSD_INNER), BF16),
                   jax.ShapeDtypeStruct((t, D_MODEL), BF16), jax.ShapeDtypeStruct((t, D_MODEL), BF16),
                   jax.ShapeDtypeStruct((8, SSD_INNER), F32), jax.ShapeDtypeStruct((nb, 8, D_MODEL), F32)],
        compiler_params=_cp("arbitrary", "arbitrary"),
    )(dx1, out1, yab, proj, mod8, bgate, post1, w_pa, w_pb, w_out)


def mlp_fwd_bwd(x1, tgt, mod8, pre2, post2, w_ff1, w_ff2, nb, seq):
    t = x1.shape[0]
    tm = min(256, seq)
    per_seq = seq // tm
    fc = 1024
    nfc = D_FF // fc

    def body(x1_ref, tgt_ref, mod_ref, pre_ref, post_ref, w1_ref, w2_ref,
             dx1_ref, h2_ref, da1_ref, act_ref, dy2_ref, loss_ref, vacc_ref, dmod_ref, r_scr):
        b, s = pl.program_id(0), pl.program_id(1)

        @pl.when((b == 0) & (s == 0))
        def _():
            vacc_ref[...] = jnp.zeros_like(vacc_ref)
            loss_ref[...] = jnp.zeros_like(loss_ref)

        @pl.when(s == 0)
        def _():
            dmod_ref[...] = jnp.zeros_like(dmod_ref)

        m = mod_ref[0]
        sh2, sc2, g2 = m[3:4, :], m[4:5, :], m[5:6, :]
        pre, post = pre_ref[...], post_ref[...]
        x1v = x1_ref[...]
        rs1 = lax.rsqrt(_rowmean(x1v * x1v) + EPS)
        n1 = x1v * rs1
        y1 = n1 * pre
        h2b = (y1 * (1.0 + sc2) + sh2).astype(BF16)
        h2_ref[...] = h2b
        y2 = jnp.zeros((tm, D_MODEL), F32)
        for c in range(nfc):
            r = jnp.maximum(_dot(h2b, w1_ref[:, c * fc:(c + 1) * fc]), 0.0)
            r_scr[:, c * fc:(c + 1) * fc] = r
            a = (r * r).astype(BF16)
            act_ref[:, c * fc:(c + 1) * fc] = a
            y2 = y2 + _dot(a, w2_ref[c * fc:(c + 1) * fc, :])
        rs2 = lax.rsqrt(_rowmean(y2 * y2) + EPS)
        n2 = y2 * rs2
        o2 = n2 * post
        diff = x1v + g2 * o2 - tgt_ref[...]
        loss_ref[...] += 0.5 * jnp.sum(_rowmean(diff * diff))
        dx2 = diff * (1.0 / D_MODEL)
        do2 = dx2 * g2
        dn2 = do2 * post
        dy2b = (rs2 * (dn2 - n2 * _rowmean(dn2 * n2))).astype(BF16)
        dy2_ref[...] = dy2b
        dh2 = jnp.zeros((tm, D_MODEL), F32)
        for c in range(nfc):
            dact = _dot_nt(dy2b, w2_ref[c * fc:(c + 1) * fc, :])
            da = (dact * (2.0 * r_scr[:, c * fc:(c + 1) * fc])).astype(BF16)
            da1_ref[:, c * fc:(c + 1) * fc] = da
            dh2 = dh2 + _dot_nt(da, w1_ref[:, c * fc:(c + 1) * fc])
        dy1 = dh2 * (1.0 + sc2)
        dn1 = dy1 * pre
        dx1_ref[...] = dx2 + rs1 * (dn1 - n1 * _rowmean(dn1 * n1))
        vacc_ref[...] += jnp.concatenate([_colsum(dy1 * n1), _colsum(do2 * n2), jnp.zeros((6, D_MODEL), F32)], axis=0)
        dmod_ref[0] += jnp.concatenate(
            [jnp.zeros((3, D_MODEL), F32), _colsum(dh2), _colsum(dh2 * y1), _colsum(dx2 * o2),
             jnp.zeros((2, D_MODEL), F32)], axis=0)

    row = lambda w: pl.BlockSpec((tm, w), lambda b, s: (b * per_seq + s, 0))
    return pl.pallas_call(
        body, name="mlp_fwd_bwd", grid=(nb, per_seq),
        in_specs=[row(D_MODEL), row(D_MODEL), pl.BlockSpec((1, 8, D_MODEL), lambda b, s: (b, 0, 0)),
                  VMEM_FULL, VMEM_FULL, VMEM_FULL, VMEM_FULL],
        out_specs=[row(D_MODEL), row(D_MODEL), row(D_FF), row(D_FF), row(D_MODEL),
                   pl.BlockSpec((8, 128), lambda b, s: (0, 0)),
                   pl.BlockSpec((8, D_MODEL), lambda b, s: (0, 0)),
                   pl.BlockSpec((1, 8, D_MODEL), lambda b, s: (b, 0, 0))],
        out_shape=[jax.ShapeDtypeStruct((t, D_MODEL), F32), jax.ShapeDtypeStruct((t, D_MODEL), BF16),
                   jax.ShapeDtypeStruct((t, D_FF), BF16), jax.ShapeDtypeStruct((t, D_FF), BF16),
                   jax.ShapeDtypeStruct((t, D_MODEL), BF16), jax.ShapeDtypeStruct((8, 128), F32),
                   jax.ShapeDtypeStruct((8, D_MODEL), F32), jax.ShapeDtypeStruct((nb, 8, D_MODEL), F32)],
        scratch_shapes=[pltpu.VMEM((tm, D_FF), F32)],
        compiler_params=_cp("arbitrary", "arbitrary"),
    )(x1, tgt, mod8, pre2, post2, w_ff1, w_ff2)


_PIECES = ((C_LRU_X, 1024), (C_LRU_G, 1024), (C_Z, 2048), (C_XBC, 4096), (C_GATES, 2048))
_NP = len(_PIECES)


def _piece_of(k, tk):
    col = k * tk
    for p, (c0, w) in enumerate(_PIECES):
        if c0 <= col < c0 + w:
            return p, (col - c0) // tk
    raise ValueError(col)


def in_proj_bwd(pieces, ddt, dx1, x2, mod8, pre1, w_main, w_dt, nb, seq, side=None):
    t = x2.shape[0]
    tm = min(512, seq)
    per_seq = seq // tm
    tk = 1024
    nk = PROJ_MAIN // tk
    where = [_piece_of(k, tk) for k in range(nk)]

    def piece_spec(p):
        first = min(k for k in range(nk) if where[k][0] == p)
        nblk = _PIECES[p][1] // tk
        return pl.BlockSpec((tm, tk), lambda b, s, k: (b * per_seq + s, jnp.clip(k - first, 0, nblk - 1)))

    def body(*refs):
        prefs = refs[:_NP]
        ddt_ref, dx1_ref, x_ref, mod_ref, pre_ref, w_ref, wdt_ref, gx_ref, vacc_ref, dmod_ref, acc_ref = refs[_NP:]
        b, s, k = pl.program_id(0), pl.program_id(1), pl.program_id(2)

        @pl.when((b == 0) & (s == 0) & (k == 0))
        def _():
            vacc_ref[...] = jnp.zeros_like(vacc_ref)

        @pl.when((s == 0) & (k == 0))
        def _():
            dmod_ref[...] = jnp.zeros_like(dmod_ref)

        @pl.when(k == 0)
        def _():
            acc_ref[...] = _dot_nt(ddt_ref[...], wdt_ref[...])

        for kk in range(nk):
            @pl.when(k == kk)
            def _(kk=kk):
                acc_ref[...] += _dot_nt(prefs[where[kk][0]][...], w_ref[:, kk * tk:(kk + 1) * tk])

        @pl.when(k == nk - 1)
        def _():
            dh = acc_ref[...]
            m = mod_ref[0]
            pre = pre_ref[...]
            xv = x_ref[...]
            rs = lax.rsqrt(_rowmean(xv * xv) + EPS)
            n = xv * rs
            dy = dh * (1.0 + m[1:2, :])
            dn = dy * pre
            gx_ref[...] = dx1_ref[...] + rs * (dn - n * _rowmean(dn * n))
            vacc_ref[...] += jnp.concatenate([_colsum(dy * n), jnp.zeros((7, D_MODEL), F32)], axis=0)
            dmod_ref[0] += jnp.concatenate([_colsum(dh), _colsum(dh * (n * pre)), jnp.zeros((6, D_MODEL), F32)], axis=0)

    row = lambda w: pl.BlockSpec((tm, w), lambda b, s, k: (b * per_seq + s, 0))
    return _call(
        body, name="in_proj_bwd", grid=(nb, per_seq, nk), side=side, sem=("arbitrary", "arbitrary", "arbitrary"),
        args=(*pieces, ddt, dx1, x2, mod8, pre1, w_main, w_dt),
        in_specs=[piece_spec(p) for p in range(_NP)] + [
            row(128), row(D_MODEL), row(D_MODEL), pl.BlockSpec((1, 8, D_MODEL), lambda b, s, k: (b, 0, 0)),
            pl.BlockSpec((1, D_MODEL), lambda b, s, k: (0, 0)),
            VMEM_FULL,
            pl.BlockSpec((D_MODEL, 128), lambda b, s, k: (0, 0))],
        out_specs=[row(D_MODEL), pl.BlockSpec((8, D_MODEL), lambda b, s, k: (0, 0)),
                   pl.BlockSpec((1, 8, D_MODEL), lambda b, s, k: (b, 0, 0))],
        out_shape=[jax.ShapeDtypeStruct((t, D_MODEL), F32), jax.ShapeDtypeStruct((8, D_MODEL), F32),
                   jax.ShapeDtypeStruct((nb, 8, D_MODEL), F32)],
        scratch_shapes=[pltpu.VMEM((tm, D_MODEL), F32)])


def in_proj_wgrad(h1t, pieces, ddt, name="in_proj_wgrad"):
    nt, _, tt = h1t.shape
    tn = 1024
    nn = PROJ_MAIN // tn
    where = [_piece_of(n, tn) for n in range(nn)]

    def piece_spec(p):
        first = min(n for n in range(nn) if where[n][0] == p)
        nblk = _PIECES[p][1] // tn
        return pl.BlockSpec((tt, tn), lambda n, k: (jnp.where((n >= first) & (n < first + nblk), k, 0),
                                                    jnp.clip(n - first, 0, nblk - 1)))

    def body(h_ref, *refs):
        prefs = refs[:_NP]
        ddt_ref, dw_ref, dwdt_ref, acc_ref, accdt_ref = refs[_NP:]
        n, k = pl.program_id(0), pl.program_id(1)
        hv = h_ref[k]
        for nn_ in range(nn):
            @pl.when(n == nn_)
            def _(nn_=nn_):
                p = _dot(hv, prefs[where[nn_][0]][...])

                @pl.when(k == 0)
                def _():
                    acc_ref[...] = p

                @pl.when(k > 0)
                def _():
                    acc_ref[...] += p

        @pl.when(n == 0)
        def _():
            p = _dot(hv, ddt_ref[...])

            @pl.when(k == 0)
            def _():
                accdt_ref[...] = p

            @pl.when(k > 0)
            def _():
                accdt_ref[...] += p

        @pl.when(k == nt - 1)
        def _():
            dw_ref[...] = acc_ref[...].astype(BF16)

        @pl.when((n == 0) & (k == nt - 1))
        def _():
            dwdt_ref[...] = accdt_ref[...].astype(BF16)

    return pl.pallas_call(
        body, name=name, grid=(nn, nt),
        in_specs=[VMEM_FULL] + [piece_spec(p) for p in range(_NP)]
        + [pl.BlockSpec((tt, 128), lambda n, k: (k, 0))],
        out_specs=[pl.BlockSpec((D_MODEL, tn), lambda n, k: (0, n)), pl.BlockSpec((D_MODEL, 128), lambda n, k: (0, 0))],
        out_shape=[jax.ShapeDtypeStruct((D_MODEL, PROJ_MAIN), BF16), jax.ShapeDtypeStruct((D_MODEL, 128), BF16)],
        scratch_shapes=[pltpu.VMEM((D_MODEL, tn), F32), pltpu.VMEM((D_MODEL, 128), F32)],
        compiler_params=_cp("arbitrary", "arbitrary"),
    )(h1t, *pieces, ddt)


def _log1p(u):
    w = 1.0 + u
    return jnp.log(w) - ((w - 1.0) - u) / w


def _softplus(x):
    return jnp.maximum(x, 0.0) + _log1p(jnp.exp(-jnp.abs(x)))


def _head_mask(h):
    lane = lax.broadcasted_iota(jnp.int32, (1, SSD_GW), 1)
    return (lane >= SSD_P * h) & (lane < SSD_P * (h + 1))


def _pair(p):
    return slice(2 * SSD_P * p, 2 * SSD_P * (p + 1))


def _expand4(m, g):
    lane = lax.broadcasted_iota(jnp.int32, (1, SSD_GW), 1)
    col = lambda h: m[:, 4 * g + h:4 * g + h + 1]
    return jnp.where(lane < SSD_P, col(0), jnp.where(lane < 2 * SSD_P, col(1), jnp.where(lane < 3 * SSD_P, col(2), col(3))))


def _reduce4(v, g):
    lane = lax.broadcasted_iota(jnp.int32, (1, SSD_N), 1)
    out = jnp.zeros((v.shape[0], SSD_N), F32)
    for h in range(4):
        s = jnp.sum(jnp.where(_head_mask(h), v, 0.0), axis=1, keepdims=True)
        out = out + jnp.where(lane == 4 * g + h, s, 0.0)
    return out


def _ssd_heads(dtraw, hp, tri):
    xdt = dtraw + hp[0:1, :]
    dt = _softplus(xdt)
    cs = _dot_hi(tri, dt * hp[1:2, :])
    cs_last = cs[SSD_L - 1:SSD_L, :]
    return dict(xdt=xdt, dt=dt, cs=cs, cs_t=cs.T, e=jnp.exp(cs), w=jnp.exp(cs_last - cs), el=jnp.exp(cs_last))


def _ssd_group(g, hd, xs_b, bm_b, cm_b, d_x, st, paired=False):
    ll = SSD_L
    xs = xs_b.astype(F32)
    cs, cs_t = hd["cs"], hd["cs_t"]
    e_x, w_x, el_x, dt_x = _expand4(hd["e"], g), _expand4(hd["w"], g), _expand4(hd["el"], g), _expand4(hd["dt"], g)
    xd = xs * dt_x
    gcb = _dot_nt(cm_b, bm_b)
    ri = lax.broadcasted_iota(jnp.int32, (ll, ll), 0)
    ci = lax.broadcasted_iota(jnp.int32, (ll, ll), 1)
    dks, ms = [], []
    for h in range(4):
        k = 4 * g + h
        dk = jnp.exp(jnp.where(ri >= ci, cs[:, k:k + 1] - cs_t[k:k + 1, :], -1e30))
        dks.append(dk)
        ms.append((gcb * dk).astype(BF16))
    xdb = xd.astype(BF16)
    if paired:
        first = lax.broadcasted_iota(jnp.int32, (1, 2 * SSD_P), 1) < SSD_P
        ydiag = jnp.concatenate(
            [jnp.where(first, _dot(ms[2 * p], xdb[:, _pair(p)]), _dot(ms[2 * p + 1], xdb[:, _pair(p)]))
             for p in range(2)], axis=1)
    else:
        ydiag = jnp.zeros((ll, SSD_GW), F32)
        for h in range(4):
            ydiag = ydiag + _dot(ms[h], jnp.where(_head_mask(h), xd, 0.0).astype(BF16))
    yoff = _dot(cm_b, st.astype(BF16)) * e_x
    y = ydiag + yoff + d_x * xs
    st_new = st * el_x + _dot(bm_b.astype(F32).T.astype(BF16), (xd * w_x).astype(BF16))
    return dict(xs=xs, e_x=e_x, w_x=w_x, el_x=el_x, dt_x=dt_x, xd=xd, xdb=xdb, gcb=gcb, dks=dks, ms=ms, yoff=yoff, y=y,
                st_new=st_new)


def ssd_consts():
    hh = np.arange(SSD_N)
    tri = (hh[:, None] >= hh[None, :]).astype(np.float32)
    return jnp.asarray(tri), jnp.asarray(tri.T)


def ssd_params(dt_bias, a_log, d_skip, norm_w):
    padh = lambda v: jnp.pad(v.reshape(1, SSD_HEADS), ((0, 0), (0, SSD_N - SSD_HEADS)))
    hp = jnp.concatenate([padh(dt_bias), padh(-jnp.exp(a_log)), jnp.zeros((6, SSD_N), F32)], axis=0)
    lp = jnp.concatenate([norm_w.reshape(1, SSD_INNER), jnp.repeat(d_skip, SSD_P).reshape(1, SSD_INNER),
                          jnp.zeros((6, SSD_INNER), F32)], axis=0)
    return hp, lp


def _b_cols(g):
    return slice(SSD_INNER + g * SSD_N, SSD_INNER + (g + 1) * SSD_N)


def _c_cols(g):
    return slice(SSD_INNER + (SSD_G + g) * SSD_N, SSD_INNER + (SSD_G + g + 1) * SSD_N)


def _ssd_specs(nc, rc):
    return [pl.BlockSpec((SSD_L, 2 * SSD_INNER), lambda b, c: (b * nc + rc(c), 0)),
            pl.BlockSpec((SSD_L, SSD_INNER), lambda b, c: (b * nc + rc(c), C_Z // SSD_INNER)),
            pl.BlockSpec((SSD_L, SSD_N), lambda b, c: (b * nc + rc(c), 0))]


def ssd_fwd(xbc, proj, dtraw, hp, lp, tri, nb, seq):
    t = xbc.shape[0]
    nc = seq // SSD_L

    def body(xbc_ref, z_ref, dt_ref, hp_ref, lp_ref, tri_ref, y_ref, sts_ref, st_scr):
        @pl.when(pl.program_id(1) == 0)
        def _():
            st_scr[...] = jnp.zeros_like(st_scr)

        hd = _ssd_heads(dt_ref[...], hp_ref[...], tri_ref[...])
        for g in range(SSD_G):
            gs = slice(g * SSD_GW, (g + 1) * SSD_GW)
            st = st_scr[g]
            sts_ref[0, g] = st
            f = _ssd_group(g, hd, xbc_ref[:, gs], xbc_ref[:, _b_cols(g)], xbc_ref[:, _c_cols(g)], lp_ref[1:2, gs], st,
                           paired=True)
            st_scr[g] = f["st_new"]
            zf = z_ref[:, gs].astype(F32)
            yg = f["y"] * (zf * _sigmoid(zf))
            y_ref[:, gs] = (yg * lax.rsqrt(_rowmean(yg * yg) + EPS) * lp_ref[0:1, gs]).astype(BF16)

    return pl.pallas_call(
        body, name="ssd_fwd", grid=(nb, nc),
        in_specs=_ssd_specs(nc, lambda c: c) + [VMEM_FULL, VMEM_FULL, VMEM_FULL],
        out_specs=[pl.BlockSpec((SSD_L, SSD_INNER), lambda b, c: (b * nc + c, 0)),
                   pl.BlockSpec((1, SSD_G, SSD_N, SSD_GW), lambda b, c: (b * nc + c, 0, 0, 0))],
        out_shape=[jax.ShapeDtypeStruct((t, SSD_INNER), BF16),
                   jax.ShapeDtypeStruct((nb * nc, SSD_G, SSD_N, SSD_GW), F32)],
        scratch_shapes=[pltpu.VMEM((SSD_G, SSD_N, SSD_GW), F32)],
        compiler_params=_cp("arbitrary", "arbitrary"),
    )(xbc, proj, dtraw, hp, lp, tri)


def ssd_bwd(xbc, proj, dtraw, hp, lp, tri, triu, states, dyn, nb, seq):
    t = xbc.shape[0]
    nc = seq // SSD_L
    ll = SSD_L

    def body(xbc_ref, z_ref, dt_ref, sts_ref, dy_ref, hp_ref, lp_ref, tri_ref, triu_ref,
             dxbc_ref, dz_ref, ddt_ref, hpg_ref, lpg_ref, dst_scr):
        b, c_i = pl.program_id(0), pl.program_id(1)

        @pl.when((b == 0) & (c_i == 0))
        def _():
            hpg_ref[...] = jnp.zeros_like(hpg_ref)
            lpg_ref[...] = jnp.zeros_like(lpg_ref)

        @pl.when(c_i == 0)
        def _():
            dst_scr[...] = jnp.zeros_like(dst_scr)

        hp = hp_ref[...]
        hd = _ssd_heads(dt_ref[...], hp, tri_ref[...])
        lane = lax.broadcasted_iota(jnp.int32, (1, SSD_N), 1)
        subl = lax.broadcasted_iota(jnp.int32, (SSD_N, 1), 0)
        dcs = jnp.zeros((ll, SSD_N), F32)
        dcs_t = jnp.zeros((SSD_N, ll), F32)
        last = jnp.zeros((1, SSD_N), F32)
        dxx = jnp.zeros((ll, SSD_N), F32)
        for g in range(SSD_G):
            gs = slice(g * SSD_GW, (g + 1) * SSD_GW)
            st = sts_ref[0, g]
            dst = dst_scr[g]
            bm_b, cm_b = xbc_ref[:, _b_cols(g)], xbc_ref[:, _c_cols(g)]
            d_x = lp_ref[1:2, gs]
            f = _ssd_group(g, hd, xbc_ref[:, gs], bm_b, cm_b, d_x, st)
            xs, xd, gcb = f["xs"], f["xd"], f["gcb"]
            e_x, w_x, el_x, dt_x = f["e_x"], f["w_x"], f["el_x"], f["dt_x"]
            stb, dstb = st.astype(BF16), dst.astype(BF16)
            zf = z_ref[:, gs].astype(F32)
            sg = _sigmoid(zf)
            sz = zf * sg
            yg = f["y"] * sz
            rstd = lax.rsqrt(_rowmean(yg * yg) + EPS)
            n = yg * rstd
            dyn_v = dy_ref[:, gs].astype(F32)
            dn = dyn_v * lp_ref[0:1, gs]
            dyg = rstd * (dn - n * _rowmean(dn * n))
            dy = dyg * sz
            dz_ref[:, gs] = (dyg * f["y"] * (sg * (1.0 + zf * (1.0 - sg)))).astype(BF16)
            dyb = dy.astype(BF16)
            r_ = _dot(bm_b, dstb)
            dxd = w_x * r_
            dqb = (dy * e_x).astype(BF16)
            dcm = _dot_nt(dqb, stb)
            dst_scr[g] = dst * el_x + _dot_tn(cm_b, dqb)
            dbm = _dot_nt((xd * w_x).astype(BF16), dstb)
            xdb = f["xdb"]
            dgm = jnp.zeros((ll, ll), F32)
            for h in range(4):
                k = 4 * g + h
                hm = _head_mask(h)
                dxd = dxd + jnp.where(hm, _dot_tn(f["ms"][h], dyb), 0.0)
                dm = _dot_nt(jnp.where(hm, dy, 0.0).astype(BF16), xdb) * f["dks"][h]
                dgm = dgm + dm
                dseg = dm * gcb
                dcs = dcs + jnp.where(lane == k, jnp.sum(dseg, axis=1, keepdims=True), 0.0)
                dcs_t = dcs_t + jnp.where(subl == k, jnp.sum(dseg, axis=0, keepdims=True), 0.0)
            dgmb = dgm.astype(BF16)
            dxbc_ref[:, _c_cols(g)] = (dcm + _dot(dgmb, bm_b)).astype(BF16)
            dxbc_ref[:, _b_cols(g)] = (dbm + _dot_tn(dgmb, cm_b)).astype(BF16)
            v = _reduce4(r_ * xd * w_x, g)
            dcs = dcs + _reduce4(dy * f["yoff"], g) - v
            last = last + _colsum(v) + _reduce4(_colsum(dst * st) * el_x, g)
            dxx = dxx + _reduce4(dxd * xs, g)
            dxbc_ref[:, gs] = (d_x * dy + dxd * dt_x).astype(BF16)
            lpg_ref[0:1, gs] += _colsum(dyn_v * n)
            lpg_ref[1:2, gs] += _colsum(dy * xs)
        rowi = lax.broadcasted_iota(jnp.int32, (ll, 1), 0)
        da = _dot_hi(triu_ref[...], dcs - dcs_t.T + jnp.where(rowi == ll - 1, last, 0.0))
        ddt = (dxx + da * hp[1:2, :]) * _sigmoid(hd["xdt"])
        ddt_ref[...] = ddt
        hpg_ref[...] += jnp.concatenate([_colsum(ddt), _colsum(da * hd["dt"]), jnp.zeros((6, SSD_N), F32)], axis=0)

    rc = lambda c: nc - 1 - c
    return pl.pallas_call(
        body, name="ssd_bwd", grid=(nb, nc),
        in_specs=_ssd_specs(nc, rc) + [
            pl.BlockSpec((1, SSD_G, SSD_N, SSD_GW), lambda b, c: (b * nc + rc(c), 0, 0, 0)),
            pl.BlockSpec((SSD_L, SSD_INNER), lambda b, c: (b * nc + rc(c), 0)),
            VMEM_FULL, VMEM_FULL, VMEM_FULL, VMEM_FULL],
        out_specs=[pl.BlockSpec((SSD_L, 2 * SSD_INNER), lambda b, c: (b * nc + rc(c), 0)),
                   pl.BlockSpec((SSD_L, SSD_INNER), lambda b, c: (b * nc + rc(c), 0)),
                   pl.BlockSpec((SSD_L, SSD_N), lambda b, c: (b * nc + rc(c), 0)),
                   pl.BlockSpec((8, SSD_N), lambda b, c: (0, 0)),
                   pl.BlockSpec((8, SSD_INNER), lambda b, c: (0, 0))],
        out_shape=[jax.ShapeDtypeStruct((t, 2 * SSD_INNER), BF16), jax.ShapeDtypeStruct((t, SSD_INNER), BF16),
                   jax.ShapeDtypeStruct((t, SSD_N), F32), jax.ShapeDtypeStruct((8, SSD_N), F32),
                   jax.ShapeDtypeStruct((8, SSD_INNER), F32)],
        scratch_shapes=[pltpu.VMEM((SSD_G, SSD_N, SSD_GW), F32)],
        compiler_params=_cp("arbitrary", "arbitrary"),
    )(xbc, proj, dtraw, states, dyn, hp, lp, tri, triu)


def ada_fwd(c_all, w_cols, b_cols):
    def body(c_ref, w_ref, b_ref, o_ref):
        cv = c_ref[...]
        o_ref[...] = _dot_hi(cv * _sigmoid(cv), w_ref[...]) + b_ref[...]

    return pl.pallas_call(body, name="ada_fwd", out_shape=jax.ShapeDtypeStruct((c_all.shape[0], w_cols.shape[1]), F32),
                          compiler_params=pltpu.CompilerParams(vmem_limit_bytes=VMEM_LIMIT))(c_all, w_cols, b_cols)


def ada_bwd(c_all, dmod_cols, dmod_all):
    def body(c_ref, dc_ref, da_ref, gw_ref, gb_ref):
        cv = c_ref[...]
        gw_ref[...] = lax.dot_general(cv * _sigmoid(cv), dc_ref[...], (((0,), (0,)), ((), ())),
                                      precision=lax.Precision.HIGHEST, preferred_element_type=F32)
        gb_ref[...] = _colsum(da_ref[...])

    return pl.pallas_call(
        body, name="ada_bwd",
        out_shape=[jax.ShapeDtypeStruct((c_all.shape[1], dmod_cols.shape[1]), F32),
                   jax.ShapeDtypeStruct((1, dmod_all.shape[1]), F32)],
        compiler_params=pltpu.CompilerParams(vmem_limit_bytes=VMEM_LIMIT))(c_all, dmod_cols, dmod_all)


def _adam_update(g, w, m, v):
    m2 = ADAM_B1 * m + (1.0 - ADAM_B1) * g
    v2 = ADAM_B2 * v + (1.0 - ADAM_B2) * (g * g)
    m_hat = m2 / (1.0 - ADAM_B1 ** ADAM_STEP)
    v_hat = v2 / (1.0 - ADAM_B2 ** ADAM_STEP)
    return -ADAM_LR * (m_hat / (jnp.sqrt(v_hat) + ADAM_EPS) + ADAM_WD * w), m2, v2


def adamw(parts, w, m, v, name):
    n, r, c = parts.shape
    tr = r if r <= 256 else 128

    def body(p_ref, w_ref, m_ref, v_ref, g_ref, d_ref, nm_ref, nv_ref):
        g = p_ref[0].astype(F32)
        for s in range(1, n):
            g = g + p_ref[s].astype(F32)
        g_ref[0] = g
        d_ref[0], nm_ref[0], nv_ref[0] = _adam_update(g, w_ref[0], m_ref[0], v_ref[0])

    blk = pl.BlockSpec((1, tr, c), lambda i: (0, i, 0))
    return pl.pallas_call(
        body, name=name, grid=(r // tr,),
        in_specs=[pl.BlockSpec((n, tr, c), lambda i: (0, i, 0)), blk, blk, blk], out_specs=[blk] * 4,
        out_shape=[jax.ShapeDtypeStruct((1, r, c), F32)] * 4,
        compiler_params=_cp("parallel"),
    )(parts, w, m, v)


SMALL_SRC = {
    'pre_norm1': ('vin', 0, 1024), 'post_norm1': ('vmg', 1, 1024), 'b_gate': ('vmg', 0, 2048),
    'lru_conv_b': ('accl', 4, 1024), 'lru_wa': ('gwa', None, None), 'lru_ba': ('dvec', 0, 1024),
    'lru_wx': ('gwx', None, None), 'lru_bx': ('dvec', 1, 1024), 'lru_lambda': ('dvec', 2, 1024),
    'ssd_conv_b': ('accs', 4, 4096), 'ssd_dt_bias': ('hpg', 0, SSD_HEADS), 'ssd_a_log': ('hpg', 1, SSD_HEADS),
    'ssd_d': ('lpg', 1, SSD_INNER), 'ssd_norm_w': ('lpg', 0, SSD_INNER), 'pre_norm2': ('vmlp', 0, 1024),
    'post_norm2': ('vmlp', 1, 1024)}
SMALL_ACCS = ('vin', 'vmg', 'vmlp', 'dvec', 'accl', 'accs', 'hpg', 'lpg', 'gwa', 'gwx')


def adamw_small(gathered, params):
    names = tuple(params)
    na = len(SMALL_ACCS)

    def body(*refs):
        acc = {k: functools.reduce(lambda p, q: p + q, [refs[i][s] for s in range(NDEV)])
               for i, k in enumerate(SMALL_ACCS)}
        ins = refs[na:na + 3 * len(names)]
        outs = refs[na + 3 * len(names):]
        for j, k in enumerate(names):
            w_ref, m_ref, v_ref = ins[3 * j:3 * j + 3]
            src, row, width = SMALL_SRC[k]
            wv = w_ref[...]
            if row is None:
                g = acc[src]
            elif k == 'ssd_d':
                li = lax.broadcasted_iota(jnp.int32, (SSD_INNER, SSD_N), 0)
                hi = lax.broadcasted_iota(jnp.int32, (SSD_INNER, SSD_N), 1)
                g = _dot_hi(acc[src], jnp.where(jnp.right_shift(li, 6) == hi, 1.0, 0.0))[row:row + 1, :SSD_HEADS]
            else:
                g = acc[src][row:row + 1, :width]
            if k == 'lru_lambda':
                g = g * (-1.0 / (1.0 + jnp.exp(wv)))
            if k == 'ssd_a_log':
                g = g * (-jnp.exp(wv))
            o = outs[4 * j:4 * j + 4]
            o[0][...] = g
            o[1][...], o[2][...], o[3][...] = _adam_update(g, wv, m_ref[...], v_ref[...])
        outs[-2][...] = acc['accl'][0:4, :]
        outs[-1][...] = acc['accs'][0:4, :]

    flat = [a for k in names for a in params[k]]
    out_shape = [jax.ShapeDtypeStruct(params[k][0].shape, F32) for k in names for _ in range(4)]
    out_shape += [jax.ShapeDtypeStruct((4, D_MODEL), F32), jax.ShapeDtypeStruct((4, 2 * SSD_INNER), F32)]
    res = pl.pallas_call(body, name="adamw_small", out_shape=out_shape,
                         compiler_params=pltpu.CompilerParams(vmem_limit_bytes=VMEM_LIMIT))(
        *[gathered[k] for k in SMALL_ACCS], *flat)
    return {k: res[4 * j:4 * j + 4] for j, k in enumerate(names)}, res[-2], res[-1]


def _dev_index(px, py, pc):
    return 4 * px + 2 * py + pc


class _Exchange:
    def __init__(self, arrs):
        self.arrs = list(arrs)
        self.na = len(self.arrs)
        self.scratch = [pltpu.SemaphoreType.DMA((7 * self.na,)), pltpu.SemaphoreType.DMA((7 * self.na,)),
                        pltpu.SemaphoreType.DMA((self.na,))]


class Gather(_Exchange):
    def __init__(self, arrs):
        super().__init__(arrs)
        self.out_shape = [jax.ShapeDtypeStruct((NDEV,) + a.shape, a.dtype) for a in self.arrs]

    def _plan(self, ins, outs, sems):
        na = self.na
        send_sems, recv_sems, local_sems = sems
        x, y, c = lax.axis_index("x"), lax.axis_index("y"), lax.axis_index("c")
        me, sibling = (x, y, c), (x, y, 1 - c)
        chips = [(1 - x, y), (x, 1 - y), (1 - x, 1 - y)]

        def copy(a, k, block, to, src=None):
            dst = outs[a].at[_dev_index(*block)]
            return pltpu.make_async_remote_copy(
                src_ref=dst if src is None else src, dst_ref=dst, send_sem=send_sems.at[a * 7 + k],
                recv_sem=recv_sems.at[a * 7 + k], device_id=to, device_id_type=MESH)

        mine = [pltpu.make_async_copy(ins[a], outs[a].at[_dev_index(*me)], local_sems.at[a]) for a in range(na)]
        first = []
        for a in range(na):
            first.append(copy(a, 0, me, sibling, src=ins[a]))
            first += [copy(a, 1 + j, me, (*chip, c), src=ins[a]) for j, chip in enumerate(chips)]
        return copy, mine, first, me, sibling, chips, c

    def start(self, ins, outs, sems):
        _, mine, first, *_ = self._plan(ins, outs, sems)
        for cp in mine + first:
            cp.start()

    def finish(self, ins, outs, sems):
        copy, mine, first, me, sibling, chips, c = self._plan(ins, outs, sems)
        passed = []
        for j, chip in enumerate(chips):
            for a in range(self.na):
                copy(a, 1 + j, (*chip, c), me).wait_recv()
                cp = copy(a, 4 + j, (*chip, c), sibling)
                cp.start()
                passed.append(cp)
        for a in range(self.na):
            copy(a, 0, sibling, me).wait_recv()
            for j, chip in enumerate(chips):
                copy(a, 4 + j, (*chip, 1 - c), me).wait_recv()
        for cp in first + passed:
            cp.wait_send()
        for cp in mine:
            cp.wait()


class Scatter(_Exchange):
    def __init__(self, arrs):
        super().__init__(arrs)
        self.out_shape = [jax.ShapeDtypeStruct(a.shape, a.dtype) for a in self.arrs]

    def _plan(self, ins, outs, sems, arrivals):
        send_sems, recv_sems, local_sems = sems
        x, y, c = lax.axis_index("x"), lax.axis_index("y"), lax.axis_index("c")
        me = _dev_index(x, y, c)
        masks = [(mx, my, mc) for mx in (0, 1) for my in (0, 1) for mc in (0, 1)][1:]
        flip = lambda v, bit: 1 - v if bit else v
        mine = [pltpu.make_async_copy(ins[a].at[me], outs[a].at[me], local_sems.at[a]) for a in range(self.na)]
        sends, recvs = [], []
        for k, (mx, my, mc) in enumerate(masks):
            peer = (flip(x, mx), flip(y, my), flip(c, mc))
            pidx = _dev_index(*peer)
            for a in range(self.na):
                on = dict(send_sem=send_sems.at[a * 7 + k], recv_sem=recv_sems.at[a * 7 + k], device_id=peer,
                          device_id_type=MESH)
                sends.append(pltpu.make_async_remote_copy(src_ref=ins[a].at[pidx], dst_ref=outs[a].at[me], **on))
                if arrivals:
                    recvs.append(pltpu.make_async_remote_copy(src_ref=ins[a].at[pidx], dst_ref=outs[a].at[pidx], **on))
        return mine, sends, recvs

    def start(self, ins, outs, sems):
        mine, sends, _ = self._plan(ins, outs, sems, arrivals=False)
        for cp in mine + sends:
            cp.start()

    def finish(self, ins, outs, sems):
        mine, sends, recvs = self._plan(ins, outs, sems, arrivals=True)
        for cp in recvs:
            cp.wait_recv()
        for cp in sends:
            cp.wait_send()
        for cp in mine:
            cp.wait()


def exchange_call(ex, name):
    na = ex.na

    def body(*refs):
        ins, outs, sems = refs[:na], refs[na:2 * na], refs[2 * na:]
        ex.start(ins, outs, sems)
        ex.finish(ins, outs, sems)

    return pl.pallas_call(body, name=name, in_specs=[ANY] * na, out_specs=[ANY] * na, out_shape=ex.out_shape,
                          scratch_shapes=ex.scratch)(*ex.arrs)


def all_gather(arrs, name):
    return exchange_call(Gather(arrs), name)


def _call(body, *, name, grid, in_specs, out_specs, out_shape, scratch_shapes=(), sem, args, side=None):
    if side is None:
        outs = pl.pallas_call(body, name=name, grid=grid, in_specs=list(in_specs), out_specs=list(out_specs),
                              out_shape=list(out_shape), scratch_shapes=list(scratch_shapes),
                              compiler_params=_cp(*sem))(*args)
        return outs, []
    ni, no, ns, na = len(in_specs), len(out_specs), len(scratch_shapes), side.na

    def wrapped(*refs):
        ins, s_in = refs[:ni], refs[ni:ni + na]
        outs, s_out = refs[ni + na:ni + na + no], refs[ni + na + no:ni + 2 * na + no]
        scr, sems = refs[ni + 2 * na + no:ni + 2 * na + no + ns], refs[ni + 2 * na + no + ns:]
        pids = [pl.program_id(i) for i in range(len(grid))]
        first = functools.reduce(lambda p, q: p & q, [p == 0 for p in pids])
        last = functools.reduce(lambda p, q: p & q, [p == g - 1 for p, g in zip(pids, grid)])

        @pl.when(first)
        def _():
            side.start(s_in, s_out, sems)

        body(*ins, *outs, *scr)

        @pl.when(last)
        def _():
            side.finish(s_in, s_out, sems)

    outs = pl.pallas_call(
        wrapped, name=name, grid=grid, in_specs=list(in_specs) + [ANY] * na, out_specs=list(out_specs) + [ANY] * na,
        out_shape=list(out_shape) + side.out_shape, scratch_shapes=list(scratch_shapes) + side.scratch,
        compiler_params=_cp(*["arbitrary"] * len(grid)))(*args, *side.arrs)
    return outs[:no], outs[no:]


WEIGHTS = ('w_ada', 'b_ada', 'pre_norm1', 'post_norm1', 'w_in', 'b_gate', 'lru_conv_w', 'lru_conv_b', 'lru_wa',
           'lru_ba', 'lru_wx', 'lru_bx', 'lru_lambda', 'w_pa', 'ssd_conv_w', 'ssd_conv_b', 'ssd_dt_bias', 'ssd_a_log',
           'ssd_d', 'ssd_norm_w', 'w_pb', 'w_out', 'pre_norm2', 'post_norm2', 'w_ff1', 'w_ff2')
BIG = ('w_in', 'w_pa', 'w_pb', 'w_out', 'w_ff1', 'w_ff2')
REPL = ('pre_norm1', 'post_norm1', 'b_gate', 'lru_conv_b', 'lru_wa', 'lru_ba', 'lru_wx', 'lru_bx', 'lru_lambda',
        'ssd_conv_b', 'ssd_dt_bias', 'ssd_a_log', 'ssd_d', 'ssd_norm_w', 'pre_norm2', 'post_norm2')
LANES = 1024


def _rows(n):
    return -(-n // LANES)


def _pack(vals, total_rows):
    parts = []
    for v in vals:
        f = v.reshape(-1).astype(F32)
        parts.append(jnp.pad(f, (0, _rows(f.shape[0]) * LANES - f.shape[0])))
    flat = jnp.concatenate(parts)
    return jnp.pad(flat.reshape(-1, LANES), ((0, total_rows - flat.shape[0] // LANES), (0, 0)))


def _unpack(slab, shapes):
    out, r = [], 0
    for s in shapes:
        n = int(np.prod(s))
        out.append(slab[r:r + _rows(n)].reshape(-1)[:n].reshape(s))
        r += _rows(n)
    return out


def _block_diag4(w):
    w4 = w.reshape(4, 4, 64, 64)
    eye = jnp.eye(4, dtype=w.dtype)
    return (w4[:, :, :, None, :] * eye[None, :, None, :, None]).reshape(4, LRU_BLOCK, LRU_BLOCK)


def _diag_blocks4(m):
    m5 = m.reshape(4, 4, 64, 4, 64)
    return jnp.stack([m5[:, a, :, a, :] for a in range(4)], axis=1).reshape(LRU_HEADS, 64, 64)


def kernel(x, c, w_ada, b_ada, pre_norm1, post_norm1, w_in, b_gate, lru_conv_w, lru_conv_b, lru_wa, lru_ba, lru_wx, lru_bx, lru_lambda, w_pa, ssd_conv_w, ssd_conv_b, ssd_dt_bias, ssd_a_log, ssd_d, ssd_norm_w, w_pb, w_out, pre_norm2, post_norm2, w_ff1, w_ff2, loss_target, m_w_ada, m_b_ada, m_pre_norm1, m_post_norm1, m_w_in, m_b_gate, m_lru_conv_w, m_lru_conv_b, m_lru_wa, m_lru_ba, m_lru_wx, m_lru_bx, m_lru_lambda, m_w_pa, m_ssd_conv_w, m_ssd_conv_b, m_ssd_dt_bias, m_ssd_a_log, m_ssd_d, m_ssd_norm_w, m_w_pb, m_w_out, m_pre_norm2, m_post_norm2, m_w_ff1, m_w_ff2, v_w_ada, v_b_ada, v_pre_norm1, v_post_norm1, v_w_in, v_b_gate, v_lru_conv_w, v_lru_conv_b, v_lru_wa, v_lru_ba, v_lru_wx, v_lru_bx, v_lru_lambda, v_w_pa, v_ssd_conv_w, v_ssd_conv_b, v_ssd_dt_bias, v_ssd_a_log, v_ssd_d, v_ssd_norm_w, v_w_pb, v_w_out, v_pre_norm2, v_post_norm2, v_w_ff1, v_w_ff2):
    given = dict(locals())
    w = {k: given[k] for k in WEIGHTS}
    mom = {k: given["m_" + k] for k in WEIGHTS}
    var = {k: given["v_" + k] for k in WEIGHTS}
    nb, seq, _ = x.shape
    assert nb == 2 and seq % 512 == 0, (nb, seq)
    t = nb * seq
    me = _dev_index(lax.axis_index("x"), lax.axis_index("y"), lax.axis_index("c"))
    x2 = x.reshape(t, D_MODEL)
    tgt2 = loss_target.reshape(t, D_MODEL)
    ada_cols = w_ada.shape[2]

    slab = jnp.zeros((16, LANES), F32)
    slab = slab.at[0:nb].set(c)
    slab = slab.at[2:6, 0:lru_conv_w.shape[2]].set(lru_conv_w[0])
    slab = slab.at[6:10, 0:ssd_conv_w.shape[2]].set(ssd_conv_w[0])
    (g1,) = all_gather([slab], "gather_cond")
    c_all = g1[:, 0:nb].reshape(NDEV * nb, D_MODEL)
    lru_cw = g1[:, 2:6, 0:lru_conv_w.shape[2]].transpose(1, 0, 2).reshape(4, D_MODEL)
    ssd_cw = g1[:, 6:10, 0:ssd_conv_w.shape[2]].transpose(1, 0, 2).reshape(4, 2 * SSD_INNER)
    b_cols = lax.dynamic_slice(b_ada, (0, me * ada_cols), (1, ada_cols))
    mod_cols = ada_fwd(c_all, w_ada[0], b_cols)
    (g2,) = all_gather([mod_cols], "gather_mod")
    mod_all = g2.transpose(1, 0, 2).reshape(NDEV * nb, N_MOD * D_MODEL)
    mod_mine = lax.dynamic_slice(mod_all, (me * nb, 0), (nb, N_MOD * D_MODEL)).reshape(nb, N_MOD, D_MODEL)
    mod8 = jnp.pad(mod_mine, ((0, 0), (0, 8 - N_MOD), (0, 0)))

    (gw_in,) = all_gather([w_in.astype(BF16)], "gather_w_in")
    w_nat = gw_in[:, 0].transpose(1, 0, 2).reshape(D_MODEL, IN_DIM)
    w_main = jnp.concatenate([w_nat[:, :DT_COL0], w_nat[:, DT_COL0 + SSD_HEADS:]], axis=1)
    w_dt = jnp.pad(w_nat[:, DT_COL0:DT_COL0 + SSD_HEADS], ((0, 0), (0, 128 - SSD_HEADS)))

    wa_bd = _block_diag4(lru_wa[0]).astype(BF16)
    wx_bd = _block_diag4(lru_wx[0]).astype(BF16)
    lam = lru_lambda[0]
    vec = _pack([lru_ba, lru_bx, jax.nn.softplus(-lam)], 8)
    tri, triu = ssd_consts()
    hp, lp = ssd_params(ssd_dt_bias[0], ssd_a_log[0], ssd_d[0], ssd_norm_w[0])

    rest = Gather([w[k].astype(BF16) for k in BIG[1:]])
    (proj, h1t, dtraw, xa, xbc), gw = in_proj_fwd(
        x2, mod8, pre_norm1, w_main, w_dt, jnp.concatenate([lru_cw, ssd_cw], axis=1),
        jnp.concatenate([lru_conv_b, ssd_conv_b], axis=1), seq, side=rest)
    w_pa_f = gw[0].reshape(D_MODEL, D_MODEL)
    w_pb_f = gw[1].reshape(SSD_INNER, D_MODEL)
    w_out_f = gw[2].reshape(D_MODEL, D_MODEL)
    w_ff1_f = gw[3][:, 0].transpose(1, 0, 2).reshape(D_MODEL, D_FF)
    w_ff2_f = gw[4].reshape(D_FF, D_MODEL)
    ya_in, hst = lru_fwd(xa, proj, wa_bd, wx_bd, vec, nb, seq)
    yb_in, states = ssd_fwd(xbc, proj, dtraw, hp, lp, tri, nb, seq)
    yab, out1, x1 = merge_fwd(ya_in, yb_in, proj, x2, mod8, b_gate, post_norm1, w_pa_f, w_pb_f, w_out_f, seq)

    dx1, h2, da1, act, dy2, loss8, vacc_mlp, dmod_mlp = mlp_fwd_bwd(
        x1, tgt2, mod8, pre_norm2, post_norm2, w_ff1_f, w_ff2_f, nb, seq)
    wg = dict(out_dtype=BF16, ta=True, tm=1024, tn=1024, tk=1024)
    dw_ff1 = matmul(h2, da1, name="wgrad_ff1", **wg)
    dw_ff2 = matmul(act, dy2, name="wgrad_ff2", **wg)
    dya_in, dyb_in, dgates, dyab, dout1, merged, vacc_mg, dmod_mg = merge_bwd(
        dx1, out1, yab, proj, mod8, b_gate, post_norm1, w_pa_f, w_pb_f, w_out_f, nb, seq)
    dw_out = matmul(merged, dout1, name="wgrad_out", **wg)
    dw_pa = matmul(ya_in, dyab, name="wgrad_pa", n=D_MODEL, b_off=0, **wg)
    dw_pb = matmul(yb_in, dyab, name="wgrad_pb", n=D_MODEL, b_off=1, **wg)
    by_rows = lambda g: g.reshape(NDEV, g.shape[0] // NDEV, g.shape[1])
    by_cols = lambda g: g.reshape(g.shape[0], NDEV, g.shape[1] // NDEV).transpose(1, 0, 2)
    (dxa, dlg, dwa_bd, dwx_bd, dvec), parts_ff = lru_bwd(
        dya_in, xa, proj, hst, wa_bd, wx_bd, vec, nb, seq, side=Scatter([by_cols(dw_ff1), by_rows(dw_ff2)]))
    dxbc, dz, ddt, hpg, lpg = ssd_bwd(xbc, proj, dtraw, hp, lp, tri, triu, states, dyb_in, nb, seq)
    (dlx, acc_l), _ = conv_bwd(proj, C_LRU_X, D_MODEL, lru_cw, lru_conv_b, dxa, nb, seq, False, "conv_lru_bwd")
    (dxr, acc_s), parts_mg = conv_bwd(proj, C_XBC, 2 * SSD_INNER, ssd_cw, ssd_conv_b, dxbc, nb, seq, True,
                                      "conv_ssd_bwd", side=Scatter([by_rows(dw_pa), by_rows(dw_pb), by_rows(dw_out)]))
    pieces = (dlx, dlg, dz, dxr, dgates)
    ddt_b = ddt.astype(BF16)
    dw_main, dw_dt = in_proj_wgrad(h1t, pieces, ddt_b)
    dw_nat = jnp.concatenate([dw_main[:, :DT_COL0], dw_dt[:, :SSD_HEADS], dw_main[:, DT_COL0:]], axis=1)
    (grad_x, vacc_in, dmod_in), parts_in = in_proj_bwd(pieces, ddt_b, dx1, x2, mod8, pre_norm1, w_main, w_dt, nb, seq,
                                                       side=Scatter([by_cols(dw_nat)]))
    parts = dict(zip(BIG, (parts_in[0], *parts_mg, *parts_ff)))

    dmod = (dmod_in + dmod_mg + dmod_mlp)[:, :N_MOD].reshape(nb, N_MOD * D_MODEL)
    (g3,) = all_gather([jnp.pad(dmod, ((0, 8 - nb), (0, 0)))], "gather_dmod")
    dmod_all = g3[:, :nb].reshape(NDEV * nb, N_MOD * D_MODEL)
    dmod_cols = lax.dynamic_slice(dmod_all, (0, me * ada_cols), (NDEV * nb, ada_cols))
    g_w_ada, g_b_ada = ada_bwd(c_all, dmod_cols, dmod_all)

    res = {}
    for k in BIG:
        res[k] = adamw(parts[k], w[k], mom[k], var[k], "adamw_" + k)
    res['w_ada'] = adamw(g_w_ada[None], w_ada, m_w_ada, v_w_ada, "adamw_w_ada")

    accs = dict(vin=vacc_in, vmg=vacc_mg, vmlp=vacc_mlp, dvec=dvec, accl=acc_l, accs=acc_s, hpg=hpg, lpg=lpg,
                gwa=_diag_blocks4(dwa_bd).reshape(LRU_HEADS * 64, 64), gwx=_diag_blocks4(dwx_bd).reshape(LRU_HEADS * 64, 64))
    gathered = dict(zip(SMALL_ACCS, all_gather([accs[k] for k in SMALL_ACCS], "gather_small_grads")))
    view = lambda a: a.reshape(-1, a.shape[-1])
    res_a, g_lru_cw, g_ssd_cw = adamw_small(gathered, {k: (view(w[k]), view(mom[k]), view(var[k])) for k in REPL})
    res.update(res_a)
    lcw, scw = lru_conv_w.shape[2], ssd_conv_w.shape[2]
    sharded = {'b_ada': g_b_ada[None], 'lru_conv_w': lax.dynamic_slice(g_lru_cw, (0, me * lcw), (4, lcw))[None],
               'ssd_conv_w': lax.dynamic_slice(g_ssd_cw, (0, me * scw), (4, scw))[None]}
    for k, g in sharded.items():
        as3 = lambda a: a.reshape(g.shape)
        res[k] = adamw(g, as3(w[k]), as3(mom[k]), as3(var[k]), "adamw_" + k)

    loss = lax.psum(loss8[0, 0], ("x", "y", "c"))
    outs = [[res[k][j].reshape(w[k].shape) for k in WEIGHTS] for j in range(4)]
    return (loss, grad_x.reshape(x.shape), *outs[0], *outs[1], *outs[2], *outs[3])
```

```python
import functools

import numpy as np
import jax
import jax.numpy as jnp
from jax import lax
from jax.experimental import pallas as pl
from jax.experimental.pallas import tpu as pltpu

F32 = jnp.float32
BF16 = jnp.bfloat16

D_MODEL = 1024
LRU_HEADS = 16
LRU_BLOCK = 256
LRU_C = 8.0
SSD_INNER = 2048
SSD_HEADS = 32
SSD_P = 64
SSD_G = 8
SSD_N = 128
SSD_L = 128
SSD_GW = SSD_INNER // SSD_G
D_FF = 4096
N_MOD = 6
EPS = 1e-6
NDEV = 8

C_LRU_X, C_LRU_G, C_Z, C_XBC, C_GATES, PROJ_MAIN = 0, 1024, 2048, 4096, 8192, 10240
IN_DIM = 10272
DT_COL0 = 8192
HALO = 16

ADAM_LR, ADAM_B1, ADAM_B2, ADAM_EPS, ADAM_WD, ADAM_STEP = 0.001, 0.9, 0.999, 1e-08, 0.01, 10

VMEM_LIMIT = 60 * 1024 * 1024
MESH = pl.DeviceIdType.MESH
ANY = pl.BlockSpec(memory_space=pl.ANY)
VMEM_FULL = pl.BlockSpec(memory_space=pltpu.VMEM)


def _cp(*sem):
    return pltpu.CompilerParams(dimension_semantics=sem, vmem_limit_bytes=VMEM_LIMIT)


def _dot(a, b):
    return jnp.dot(a, b, preferred_element_type=F32)


def _dot_nt(a, b):
    return lax.dot_general(a, b, (((1,), (1,)), ((), ())), preferred_element_type=F32)


def _dot_tn(a, b):
    return lax.dot_general(a, b, (((0,), (0,)), ((), ())), preferred_element_type=F32)


def _dot_hi(a, b):
    return jnp.dot(a, b, precision=lax.Precision.HIGHEST, preferred_element_type=F32)


def _sigmoid(x):
    return 1.0 / (1.0 + jnp.exp(-x))


def _gelu_and_grad(x):
    k0, k1 = 0.7978845608028654, 0.044715
    t = jnp.tanh(k0 * (x + k1 * x * x * x))
    g = 0.5 * x * (1.0 + t)
    dg = 0.5 * (1.0 + t) + 0.5 * x * (1.0 - t * t) * k0 * (1.0 + 3.0 * k1 * x * x)
    return g, dg


def _neg_expm1(y):
    p = 1.0 + y * (1.0 / 7.0)
    p = 1.0 + y * (1.0 / 6.0) * p
    p = 1.0 + y * (1.0 / 5.0) * p
    p = 1.0 + y * (1.0 / 4.0) * p
    p = 1.0 + y * (1.0 / 3.0) * p
    p = 1.0 + y * 0.5 * p
    return jnp.where(y > -0.3, -y * p, 1.0 - jnp.exp(y))


def _colsum(v):
    return jnp.sum(v, axis=0, keepdims=True)


def _rowmean(v):
    return jnp.mean(v, axis=-1, keepdims=True)


def matmul(a, b, *, ta=False, tb=False, out_dtype=F32, tm, tn, tk, name, n=None, b_off=0, blocked_out=False):
    m = a.shape[1] if ta else a.shape[0]
    kdim = a.shape[0] if ta else a.shape[1]
    n = n or (b.shape[0] if tb else b.shape[1])
    tm, tn, tk = min(tm, m), min(tn, n), min(tk, kdim)
    nk = kdim // tk
    dn = (((0 if ta else 1,), (1 if tb else 0,)), ((), ()))

    def body(a_ref, b_ref, o_ref, acc_ref):
        k = pl.program_id(2)
        p = lax.dot_general(a_ref[...], b_ref[...], dn, preferred_element_type=F32)
        if nk == 1:
            o_ref[...] = p.astype(out_dtype)
        else:
            @pl.when(k == 0)
            def _():
                acc_ref[...] = p

            @pl.when(k > 0)
            def _():
                acc_ref[...] += p

            @pl.when(k == nk - 1)
            def _():
                o_ref[...] = acc_ref[...].astype(out_dtype)

    a_spec = pl.BlockSpec((tk, tm), lambda i, j, k: (k, i)) if ta else pl.BlockSpec((tm, tk), lambda i, j, k: (i, k))
    b_spec = (pl.BlockSpec((tn, tk), lambda i, j, k: (j, k)) if tb
              else pl.BlockSpec((tk, tn), lambda i, j, k: (k, j + b_off)))
    if blocked_out:
        o_spec, o_shape = pl.BlockSpec((None, tm, tn), lambda i, j, k: (j, i, 0)), (n // tn, m, tn)
    else:
        o_spec, o_shape = pl.BlockSpec((tm, tn), lambda i, j, k: (i, j)), (m, n)
    return pl.pallas_call(
        body, name=name, grid=(m // tm, n // tn, nk),
        in_specs=[a_spec, b_spec], out_specs=o_spec,
        out_shape=jax.ShapeDtypeStruct(o_shape, out_dtype),
        scratch_shapes=[pltpu.VMEM((tm, tn), F32)],
        compiler_params=_cp("parallel", "parallel", "arbitrary"),
    )(a, b)


def _conv_tile(j, tn):
    return jnp.where(j == 0, 0, jnp.clip(j - C_XBC // tn + 1, 1, 2 * SSD_INNER // tn))


def in_proj_fwd(x2, mod8, pre1, w_main, w_dt, cw, cb, seq, side=None):
    t = x2.shape[0]
    tm = min(1024, seq)
    tn = 1024
    per_seq = seq // tm
    j_xbc = C_XBC // tn
    n_xbc = 2 * SSD_INNER // tn
    cs = 256

    def body(x_ref, mod_ref, pre_ref, w_ref, wdt_ref, cw_ref, cb_ref, proj_ref, h_ref, dt_ref, xa_ref, xbc_ref,
             h_scr, carry_scr):
        i, j = pl.program_id(0), pl.program_id(1)

        @pl.when(j == 0)
        def _():
            xv = x_ref[...]
            y = xv * lax.rsqrt(_rowmean(xv * xv) + EPS) * pre_ref[...]
            m = mod_ref[0]
            hf = y * (1.0 + m[1:2, :]) + m[0:1, :]
            h = hf.astype(BF16)
            h_scr[...] = h
            h_ref[...] = hf.T.astype(BF16)
            dt_ref[...] = _dot(h, wdt_ref[...])

        def project(c0=0, width=tn):
            pb = _dot(h_scr[...], w_ref[:, c0:c0 + width]).astype(BF16)
            proj_ref[:, c0:c0 + width] = pb
            return pb

        def conv(o_ref, slot, act):
            first = lax.rem(i, per_seq) == 0
            for c0 in range(0, tn, cs):
                cur = project(c0, cs).astype(F32)
                prev = jnp.where(first, 0.0, carry_scr[slot, :, c0:c0 + cs])
                carry_scr[slot, :, c0:c0 + cs] = cur[tm - HALO:, :]
                xx = jnp.concatenate([prev, cur], axis=0)
                w = cw_ref[:, c0:c0 + cs]
                acc = cur * w[3:4, :] + cb_ref[:, c0:c0 + cs]
                for d in (1, 2, 3):
                    acc = acc + pltpu.roll(xx, d, axis=0)[HALO:, :] * w[3 - d:4 - d, :]
                if act:
                    acc = acc * _sigmoid(acc)
                o_ref[:, c0:c0 + cs] = acc.astype(BF16)

        is_xbc = (j >= j_xbc) & (j < j_xbc + n_xbc)

        @pl.when(j == 0)
        def _():
            conv(xa_ref, 0, False)

        @pl.when(is_xbc)
        def _():
            conv(xbc_ref, j - j_xbc + 1, True)

        @pl.when((j > 0) & jnp.logical_not(is_xbc))
        def _():
            project()

    return _call(
        body, name="in_proj_fwd", grid=(t // tm, PROJ_MAIN // tn), side=side, sem=("arbitrary", "arbitrary"),
        args=(x2, mod8, pre1, w_main, w_dt, cw, cb),
        in_specs=[pl.BlockSpec((tm, D_MODEL), lambda i, j: (i, 0)),
                  pl.BlockSpec((1, 8, D_MODEL), lambda i, j: (i // per_seq, 0, 0)),
                  pl.BlockSpec((1, D_MODEL), lambda i, j: (0, 0)),
                  pl.BlockSpec((D_MODEL, tn), lambda i, j: (0, j)),
                  pl.BlockSpec((D_MODEL, 128), lambda i, j: (0, 0)),
                  pl.BlockSpec((4, tn), lambda i, j: (0, _conv_tile(j, tn))),
                  pl.BlockSpec((1, tn), lambda i, j: (0, _conv_tile(j, tn)))],
        out_specs=[pl.BlockSpec((tm, tn), lambda i, j: (i, j)),
                   pl.BlockSpec((None, D_MODEL, tm), lambda i, j: (i, 0, 0)),
                   pl.BlockSpec((tm, 128), lambda i, j: (i, 0)),
                   pl.BlockSpec((tm, tn), lambda i, j: (i, 0)),
                   pl.BlockSpec((tm, tn), lambda i, j: (i, jnp.clip(j - j_xbc, 0, n_xbc - 1)))],
        out_shape=[jax.ShapeDtypeStruct((t, PROJ_MAIN), BF16), jax.ShapeDtypeStruct((t // tm, D_MODEL, tm), BF16),
                   jax.ShapeDtypeStruct((t, 128), F32), jax.ShapeDtypeStruct((t, D_MODEL), BF16),
                   jax.ShapeDtypeStruct((t, 2 * SSD_INNER), BF16)],
        scratch_shapes=[pltpu.VMEM((tm, D_MODEL), BF16), pltpu.VMEM((1 + n_xbc, HALO, tn), F32)])


def conv_bwd(src, col0, width, w4, bias, dout, nb, seq, act, name, side=None):
    t = src.shape[0]
    tt = min(512, seq)
    tc = 512
    ns = seq // tt
    cb0 = col0 // tc
    nh = t // HALO

    def body(cur_ref, prev_ref, next_ref, w_ref, b_ref, do_ref, don_ref, dx_ref, acc_ref):
        b, s = pl.program_id(1), pl.program_id(2)

        @pl.when((b == 0) & (s == 0))
        def _():
            acc_ref[...] = jnp.zeros_like(acc_ref)

        cur = cur_ref[...].astype(F32)
        prev = jnp.where(s == 0, 0.0, prev_ref[...].astype(F32))
        nxt = next_ref[...].astype(F32)
        xx = jnp.concatenate([prev, cur, nxt], axis=0)
        w = w_ref[...]
        do_ext = jnp.concatenate([do_ref[...].astype(F32),
                                  jnp.where(s == ns - 1, 0.0, don_ref[...].astype(F32))], axis=0)
        ne = tt + HALO
        xs = [xx[HALO:HALO + ne, :]] + [pltpu.roll(xx, d, axis=0)[HALO:HALO + ne, :] for d in (1, 2, 3)]
        if act:
            c = b_ref[...] + xs[0] * w[3:4, :] + xs[1] * w[2:3, :] + xs[2] * w[1:2, :] + xs[3] * w[0:1, :]
            sg = _sigmoid(c)
            dc = do_ext * (sg * (1.0 + c * (1.0 - sg)))
        else:
            dc = do_ext
        dx = dc[:tt, :] * w[3:4, :]
        for d in (1, 2, 3):
            dx = dx + pltpu.roll(dc, ne - d, axis=0)[:tt, :] * w[3 - d:4 - d, :]
        dx_ref[...] = dx.astype(BF16)
        dcc = dc[:tt, :]
        rows = [_colsum(dcc * xs[3 - r][:tt, :]) for r in range(4)] + [_colsum(dcc)]
        acc_ref[...] += jnp.concatenate(rows + [jnp.zeros((3, tc), F32)], axis=0)

    return _call(
        body, name=name, grid=(width // tc, nb, ns), side=side, sem=("parallel", "arbitrary", "arbitrary"),
        args=(src, src, src, w4, bias, dout, dout),
        in_specs=[pl.BlockSpec((tt, tc), lambda j, b, s: (b * ns + s, cb0 + j)),
                  pl.BlockSpec((HALO, tc), lambda j, b, s: (jnp.maximum((b * seq + s * tt) // HALO - 1, 0), cb0 + j)),
                  pl.BlockSpec((HALO, tc), lambda j, b, s: (jnp.minimum((b * seq + (s + 1) * tt) // HALO, nh - 1), cb0 + j)),
                  pl.BlockSpec((4, tc), lambda j, b, s: (0, j)),
                  pl.BlockSpec((1, tc), lambda j, b, s: (0, j)),
                  pl.BlockSpec((tt, tc), lambda j, b, s: (b * ns + s, j)),
                  pl.BlockSpec((HALO, tc), lambda j, b, s: (jnp.minimum((b * seq + (s + 1) * tt) // HALO, nh - 1), j))],
        out_specs=[pl.BlockSpec((tt, tc), lambda j, b, s: (b * ns + s, j)),
                   pl.BlockSpec((8, tc), lambda j, b, s: (0, j))],
        out_shape=[jax.ShapeDtypeStruct((t, width), BF16), jax.ShapeDtypeStruct((8, width), F32)])


def _lru_gates(xa, wa_ref, wx_ref, ba, bx, sp):
    nblk = D_MODEL // LRU_BLOCK
    pr = jnp.concatenate([_dot(xa[:, j * LRU_BLOCK:(j + 1) * LRU_BLOCK], wa_ref[j]) for j in range(nblk)], axis=1) + ba
    pi = jnp.concatenate([_dot(xa[:, j * LRU_BLOCK:(j + 1) * LRU_BLOCK], wx_ref[j]) for j in range(nblk)], axis=1) + bx
    r = _sigmoid(pr)
    i = _sigmoid(pi)
    log_a = (-LRU_C * r) * sp
    return r, i, jnp.exp(log_a), _neg_expm1(2.0 * log_a)


def lru_fwd(xa, proj, wa_bd, wx_bd, vec, nb, seq):
    t = xa.shape[0]
    tc = min(512, seq)
    nk = seq // tc
    gb = C_LRU_G // D_MODEL

    def body(xa_ref, g_ref, wa_ref, wx_ref, vec_ref, ya_ref, h_ref, a_scr, u_scr, hc_scr):
        @pl.when(pl.program_id(1) == 0)
        def _():
            hc_scr[...] = jnp.zeros_like(hc_scr)

        xa_v = xa_ref[...]
        v = vec_ref[...]
        r, i, a, e = _lru_gates(xa_v, wa_ref, wx_ref, v[0:1, :], v[1:2, :], v[2:3, :])
        a_scr[...] = a
        u_scr[...] = jnp.sqrt(e) * (i * xa_v.astype(F32))
        row = lax.broadcasted_iota(jnp.int32, (8, 1), 0)

        def tile(j, h):
            r0 = pl.multiple_of(j * 8, 8)
            av, uv = a_scr[pl.ds(r0, 8), :], u_scr[pl.ds(r0, 8), :]
            for d in (1, 2, 4):
                uv = uv + av * jnp.where(row >= d, pltpu.roll(uv, d, axis=0), 0.0)
                av = av * jnp.where(row >= d, pltpu.roll(av, d, axis=0), 1.0)
            hv = uv + av * h
            h_ref[pl.ds(r0, 8), :] = hv
            return hv[7:8, :]

        hc_scr[...] = lax.fori_loop(0, tc // 8, tile, hc_scr[...], unroll=2)
        gel, _ = _gelu_and_grad(g_ref[...].astype(F32))
        ya_ref[...] = (h_ref[...] * gel).astype(BF16)

    return pl.pallas_call(
        body, name="lru_fwd", grid=(nb, nk),
        in_specs=[pl.BlockSpec((tc, D_MODEL), lambda b, k: (b * nk + k, 0)),
                  pl.BlockSpec((tc, D_MODEL), lambda b, k: (b * nk + k, gb)),
                  VMEM_FULL, VMEM_FULL, VMEM_FULL],
        out_specs=[pl.BlockSpec((tc, D_MODEL), lambda b, k: (b * nk + k, 0)),
                   pl.BlockSpec((tc, D_MODEL), lambda b, k: (b * nk + k, 0))],
        out_shape=[jax.ShapeDtypeStruct((t, D_MODEL), BF16), jax.ShapeDtypeStruct((t, D_MODEL), F32)],
        scratch_shapes=[pltpu.VMEM((tc, D_MODEL), F32), pltpu.VMEM((tc, D_MODEL), F32), pltpu.VMEM((1, D_MODEL), F32)],
        compiler_params=_cp("arbitrary", "arbitrary"),
    )(xa, proj, wa_bd, wx_bd, vec)


def lru_bwd(dya, xa, proj, h, wa_bd, wx_bd, vec, nb, seq, side=None):
    t = xa.shape[0]
    tc = min(512, seq)
    nk = seq // tc
    gb = C_LRU_G // D_MODEL
    nblk = D_MODEL // LRU_BLOCK

    def chunk(b, k):
        return b * nk + (nk - 1 - k)

    def body(dya_ref, xa_ref, g_ref, h_ref, hp_ref, wa_ref, wx_ref, vec_ref,
             dxa_ref, dg_ref, dwa_ref, dwx_ref, dvec_ref, a_scr, dh_scr, c_scr):
        b, k = pl.program_id(0), pl.program_id(1)

        @pl.when((b == 0) & (k == 0))
        def _():
            dwa_ref[...] = jnp.zeros_like(dwa_ref)
            dwx_ref[...] = jnp.zeros_like(dwx_ref)
            dvec_ref[...] = jnp.zeros_like(dvec_ref)

        @pl.when(k == 0)
        def _():
            c_scr[...] = jnp.zeros_like(c_scr)

        xa_v = xa_ref[...]
        xaf = xa_v.astype(F32)
        v = vec_ref[...]
        sp = v[2:3, :]
        r, i, a, e = _lru_gates(xa_v, wa_ref, wx_ref, v[0:1, :], v[1:2, :], sp)
        gel, dgel = _gelu_and_grad(g_ref[...].astype(F32))
        hv = h_ref[...]
        dyv = dya_ref[...].astype(F32)
        dg_ref[...] = (dyv * hv * dgel).astype(BF16)
        a_scr[...] = a
        dh_scr[...] = dyv * gel

        row8 = lax.broadcasted_iota(jnp.int32, (8, 1), 0)

        def tile(j, c):
            r0 = pl.multiple_of((tc // 8 - 1 - j) * 8, 8)
            av, dout = a_scr[pl.ds(r0, 8), :], dh_scr[pl.ds(r0, 8), :]
            zv = av * dout
            for d in (1, 2, 4):
                zv = zv + av * jnp.where(row8 < 8 - d, pltpu.roll(zv, 8 - d, axis=0), 0.0)
                av = av * jnp.where(row8 < 8 - d, pltpu.roll(av, 8 - d, axis=0), 1.0)
            zv = zv + av * c
            dh_scr[pl.ds(r0, 8), :] = dout + jnp.where(row8 < 7, pltpu.roll(zv, 7, axis=0), c)
            return zv[0:1, :]

        c_scr[...] = lax.fori_loop(0, tc // 8, tile, c_scr[...], unroll=2)
        dh = dh_scr[...]
        h_last = jnp.where(k == nk - 1, 0.0, hp_ref[HALO // 2 - 1:HALO // 2, :])
        row = lax.broadcasted_iota(jnp.int32, (tc, 1), 0)
        h_prev = jnp.where(row == 0, h_last, pltpu.roll(hv, 1, axis=0))
        s = jnp.sqrt(e)
        da = dh * h_prev
        ix = i * xaf
        dlog_a = da * a - (dh * ix) * (a * a) * lax.rsqrt(jnp.maximum(e, 1e-30))
        di = dh * s * xaf
        dpr = (dlog_a * (-LRU_C * sp)) * (r * (1.0 - r))
        dpi = di * (i * (1.0 - i))
        dprb, dpib = dpr.astype(BF16), dpi.astype(BF16)
        dxa = dh * s * i
        dxa = dxa + jnp.concatenate(
            [_dot_nt(dprb[:, j * LRU_BLOCK:(j + 1) * LRU_BLOCK], wa_ref[j])
             + _dot_nt(dpib[:, j * LRU_BLOCK:(j + 1) * LRU_BLOCK], wx_ref[j]) for j in range(nblk)], axis=1)
        dxa_ref[...] = dxa.astype(BF16)
        for j in range(nblk):
            sl = slice(j * LRU_BLOCK, (j + 1) * LRU_BLOCK)
            dwa_ref[j] += _dot_tn(xa_v[:, sl], dprb[:, sl])
            dwx_ref[j] += _dot_tn(xa_v[:, sl], dpib[:, sl])
        dvec_ref[...] += jnp.concatenate(
            [_colsum(dpr), _colsum(dpi), _colsum(dlog_a * (-LRU_C * r)), jnp.zeros((5, D_MODEL), F32)], axis=0)

    hh = HALO // 2
    return _call(
        body, name="lru_bwd", grid=(nb, nk), side=side, sem=("arbitrary", "arbitrary"),
        args=(dya, xa, proj, h, h, wa_bd, wx_bd, vec),
        in_specs=[pl.BlockSpec((tc, D_MODEL), lambda b, k: (chunk(b, k), 0)),
                  pl.BlockSpec((tc, D_MODEL), lambda b, k: (chunk(b, k), 0)),
                  pl.BlockSpec((tc, D_MODEL), lambda b, k: (chunk(b, k), gb)),
                  pl.BlockSpec((tc, D_MODEL), lambda b, k: (chunk(b, k), 0)),
                  pl.BlockSpec((hh, D_MODEL), lambda b, k: (jnp.maximum(chunk(b, k) * (tc // hh) - 1, 0), 0)),
                  VMEM_FULL, VMEM_FULL, VMEM_FULL],
        out_specs=[pl.BlockSpec((tc, D_MODEL), lambda b, k: (chunk(b, k), 0)),
                   pl.BlockSpec((tc, D_MODEL), lambda b, k: (chunk(b, k), 0)),
                   pl.BlockSpec((nblk, LRU_BLOCK, LRU_BLOCK), lambda b, k: (0, 0, 0)),
                   pl.BlockSpec((nblk, LRU_BLOCK, LRU_BLOCK), lambda b, k: (0, 0, 0)),
                   pl.BlockSpec((8, D_MODEL), lambda b, k: (0, 0))],
        out_shape=[jax.ShapeDtypeStruct((t, D_MODEL), BF16), jax.ShapeDtypeStruct((t, D_MODEL), BF16),
                   jax.ShapeDtypeStruct((nblk, LRU_BLOCK, LRU_BLOCK), F32),
                   jax.ShapeDtypeStruct((nblk, LRU_BLOCK, LRU_BLOCK), F32),
                   jax.ShapeDtypeStruct((8, D_MODEL), F32)],
        scratch_shapes=[pltpu.VMEM((tc, D_MODEL), F32), pltpu.VMEM((tc, D_MODEL), F32), pltpu.VMEM((1, D_MODEL), F32)])


def merge_fwd(ya_in, yb_in, proj, x2, mod8, bgate, post1, w_pa, w_pb, w_out, seq):
    t = x2.shape[0]
    tm = min(512, seq)
    per_seq = seq // tm
    gcb = C_GATES // SSD_INNER

    def body(ya_ref, yb_ref, gt_ref, x_ref, mod_ref, bg_ref, post_ref, wpa_ref, wpb_ref, wo_ref,
             yab_ref, out1_ref, x1_ref):
        y_a = _dot(ya_ref[...], wpa_ref[...])
        y_b = _dot(yb_ref[...], wpb_ref[...])
        g = _sigmoid(gt_ref[...].astype(F32) + bg_ref[...])
        merged = g[:, :D_MODEL] * y_a + g[:, D_MODEL:] * y_b
        out1 = _dot(merged.astype(BF16), wo_ref[...])
        n = out1 * lax.rsqrt(_rowmean(out1 * out1) + EPS)
        yab_ref[...] = jnp.concatenate([y_a, y_b], axis=1).astype(BF16)
        out1_ref[...] = out1
        x1_ref[...] = x_ref[...] + mod_ref[0][2:3, :] * (n * post_ref[...])

    row = lambda w: pl.BlockSpec((tm, w), lambda i: (i, 0))
    return pl.pallas_call(
        body, name="merge_fwd", grid=(t // tm,),
        in_specs=[row(D_MODEL), row(SSD_INNER), pl.BlockSpec((tm, SSD_INNER), lambda i: (i, gcb)), row(D_MODEL),
                  pl.BlockSpec((1, 8, D_MODEL), lambda i: (i // per_seq, 0, 0)),
                  VMEM_FULL, VMEM_FULL, VMEM_FULL, VMEM_FULL, VMEM_FULL],
        out_specs=[row(SSD_INNER), row(D_MODEL), row(D_MODEL)],
        out_shape=[jax.ShapeDtypeStruct((t, SSD_INNER), BF16), jax.ShapeDtypeStruct((t, D_MODEL), F32),
                   jax.ShapeDtypeStruct((t, D_MODEL), F32)],
        compiler_params=_cp("parallel"),
    )(ya_in, yb_in, proj, x2, mod8, bgate, post1, w_pa, w_pb, w_out)


def merge_bwd(dx1, out1, yab, proj, mod8, bgate, post1, w_pa, w_pb, w_out, nb, seq):
    t = dx1.shape[0]
    tm = min(512, seq)
    per_seq = seq // tm
    gcb = C_GATES // SSD_INNER

    def body(dx1_ref, out1_ref, yab_ref, gt_ref, mod_ref, bg_ref, post_ref, wpa_ref, wpb_ref, wo_ref,
             dya_ref, dyb_ref, dgt_ref, dyab_ref, dout1_ref, mg_ref, vacc_ref, dmod_ref):
        b, s = pl.program_id(0), pl.program_id(1)

        @pl.when((b == 0) & (s == 0))
        def _():
            vacc_ref[...] = jnp.zeros_like(vacc_ref)

        @pl.when(s == 0)
        def _():
            dmod_ref[...] = jnp.zeros_like(dmod_ref)

        dx1v = dx1_ref[...]
        out1 = out1_ref[...]
        post = post_ref[...]
        rs = lax.rsqrt(_rowmean(out1 * out1) + EPS)
        n = out1 * rs
        do = dx1v * mod_ref[0][2:3, :]
        dn = do * post
        dout1 = rs * (dn - n * _rowmean(dn * n))
        dout1b = dout1.astype(BF16)
        dout1_ref[...] = dout1b
        dmerged = _dot_nt(dout1b, wo_ref[...])
        g = _sigmoid(gt_ref[...].astype(F32) + bg_ref[...])
        yab_v = yab_ref[...].astype(F32)
        gy = g * yab_v
        mg_ref[...] = (gy[:, :D_MODEL] + gy[:, D_MODEL:]).astype(BF16)
        dm2 = jnp.concatenate([dmerged, dmerged], axis=1)
        dyab = (dm2 * g).astype(BF16)
        dyab_ref[...] = dyab
        dgt = dm2 * gy * (1.0 - g)
        dgt_ref[...] = dgt.astype(BF16)
        dya_ref[...] = _dot_nt(dyab[:, :D_MODEL], wpa_ref[...]).astype(BF16)
        dyb_ref[...] = _dot_nt(dyab[:, D_MODEL:], wpb_ref[...]).astype(BF16)
        vacc_ref[...] += jnp.concatenate(
            [_colsum(dgt), jnp.concatenate([_colsum(do * n), jnp.zeros((1, D_MODEL), F32)], axis=1),
             jnp.zeros((6, SSD_INNER), F32)], axis=0)
        dmod_ref[0] += jnp.concatenate(
            [jnp.zeros((2, D_MODEL), F32), _colsum(dx1v * (n * post)), jnp.zeros((5, D_MODEL), F32)], axis=0)

    row = lambda w: pl.BlockSpec((tm, w), lambda b, s: (b * per_seq + s, 0))
    return pl.pallas_call(
        body, name="merge_bwd", grid=(nb, per_seq),
        in_specs=[row(D_MODEL), row(D_MODEL), row(SSD_INNER),
                  pl.BlockSpec((tm, SSD_INNER), lambda b, s: (b * per_seq + s, gcb)),
                  pl.BlockSpec((1, 8, D_MODEL), lambda b, s: (b, 0, 0)),
                  VMEM_FULL, VMEM_FULL, VMEM_FULL, VMEM_FULL, VMEM_FULL],
        out_specs=[row(D_MODEL), row(SSD_INNER), row(SSD_INNER), row(SSD_INNER), row(D_MODEL), row(D_MODEL),
                   pl.BlockSpec((8, SSD_INNER), lambda b, s: (0, 0)),
                   pl.BlockSpec((1, 8, D_MODEL), lambda b, s: (b, 0, 0))],
        out_shape=[jax.ShapeDtypeStruct((t, D_MODEL), BF16), jax.ShapeDtypeStruct((t, SSD_INNER), BF16),
                   jax.ShapeDtypeStruct((t, SSD_INNER), BF16), jax.ShapeDtypeStruct((t, SSD_INNER), BF16),
                   jax.ShapeDtypeStruct((t, D_MODEL), BF16), jax.ShapeDtypeStruct((t, D_MODEL), BF16),
                   jax.ShapeDtypeStruct((8, SSD_INNER), F32), jax.ShapeDtypeStruct((nb, 8, D_MODEL), F32)],
        compiler_params=_cp("arbitrary", "arbitrary"),
    )(dx1, out1, yab, proj, mod8, bgate, post1, w_pa, w_pb, w_out)


def mlp_fwd_bwd(x1, tgt, mod8, pre2, post2, w_ff1, w_ff2, nb, seq):
    t = x1.shape[0]
    tm = min(256, seq)
    per_seq = seq // tm
    fc = 1024
    nfc = D_FF // fc

    def body(x1_ref, tgt_ref, mod_ref, pre_ref, post_ref, w1_ref, w2_ref,
             dx1_ref, h2_ref, da1_ref, act_ref, dy2_ref, loss_ref, vacc_ref, dmod_ref, r_scr):
        b, s = pl.program_id(0), pl.program_id(1)
        per = fc // w1_ref.shape[2]

        def w1_cols(c):
            return jnp.concatenate([w1_ref[per * c + q] for q in range(per)], axis=1)

        @pl.when((b == 0) & (s == 0))
        def _():
            vacc_ref[...] = jnp.zeros_like(vacc_ref)
            loss_ref[...] = jnp.zeros_like(loss_ref)

        @pl.when(s == 0)
        def _():
            dmod_ref[...] = jnp.zeros_like(dmod_ref)

        m = mod_ref[0]
        sh2, sc2, g2 = m[3:4, :], m[4:5, :], m[5:6, :]
        pre, post = pre_ref[...], post_ref[...]
        x1v = x1_ref[...]
        rs1 = lax.rsqrt(_rowmean(x1v * x1v) + EPS)
        n1 = x1v * rs1
        y1 = n1 * pre
        h2b = (y1 * (1.0 + sc2) + sh2).astype(BF16)
        h2_ref[...] = h2b
        y2 = jnp.zeros((tm, D_MODEL), F32)
        for c in range(nfc):
            r = jnp.maximum(_dot(h2b, w1_cols(c)), 0.0)
            r_scr[:, c * fc:(c + 1) * fc] = r
            a = (r * r).astype(BF16)
            act_ref[:, c * fc:(c + 1) * fc] = a
            y2 = y2 + _dot(a, w2_ref[c * fc:(c + 1) * fc, :])
        rs2 = lax.rsqrt(_rowmean(y2 * y2) + EPS)
        n2 = y2 * rs2
        o2 = n2 * post
        diff = x1v + g2 * o2 - tgt_ref[...]
        loss_ref[...] += 0.5 * jnp.sum(_rowmean(diff * diff))
        dx2 = diff * (1.0 / D_MODEL)
        do2 = dx2 * g2
        dn2 = do2 * post
        dy2b = (rs2 * (dn2 - n2 * _rowmean(dn2 * n2))).astype(BF16)
        dy2_ref[...] = dy2b
        dh2 = jnp.zeros((tm, D_MODEL), F32)
        for c in range(nfc):
            dact = _dot_nt(dy2b, w2_ref[c * fc:(c + 1) * fc, :])
            da = (dact * (2.0 * r_scr[:, c * fc:(c + 1) * fc])).astype(BF16)
            da1_ref[:, c * fc:(c + 1) * fc] = da
            dh2 = dh2 + _dot_nt(da, w1_cols(c))
        dy1 = dh2 * (1.0 + sc2)
        dn1 = dy1 * pre
        dx1_ref[...] = dx2 + rs1 * (dn1 - n1 * _rowmean(dn1 * n1))
        vacc_ref[...] += jnp.concatenate([_colsum(dy1 * n1), _colsum(do2 * n2), jnp.zeros((6, D_MODEL), F32)], axis=0)
        dmod_ref[0] += jnp.concatenate(
            [jnp.zeros((3, D_MODEL), F32), _colsum(dh2), _colsum(dh2 * y1), _colsum(dx2 * o2),
             jnp.zeros((2, D_MODEL), F32)], axis=0)

    row = lambda w: pl.BlockSpec((tm, w), lambda b, s: (b * per_seq + s, 0))
    return pl.pallas_call(
        body, name="mlp_fwd_bwd", grid=(nb, per_seq),
        in_specs=[row(D_MODEL), row(D_MODEL), pl.BlockSpec((1, 8, D_MODEL), lambda b, s: (b, 0, 0)),
                  VMEM_FULL, VMEM_FULL, VMEM_FULL, VMEM_FULL],
        out_specs=[row(D_MODEL), row(D_MODEL), row(D_FF), row(D_FF), row(D_MODEL),
                   pl.BlockSpec((8, 128), lambda b, s: (0, 0)),
                   pl.BlockSpec((8, D_MODEL), lambda b, s: (0, 0)),
                   pl.BlockSpec((1, 8, D_MODEL), lambda b, s: (b, 0, 0))],
        out_shape=[jax.ShapeDtypeStruct((t, D_MODEL), F32), jax.ShapeDtypeStruct((t, D_MODEL), BF16),
                   jax.ShapeDtypeStruct((t, D_FF), BF16), jax.ShapeDtypeStruct((t, D_FF), BF16),
                   jax.ShapeDtypeStruct((t, D_MODEL), BF16), jax.ShapeDtypeStruct((8, 128), F32),
                   jax.ShapeDtypeStruct((8, D_MODEL), F32), jax.ShapeDtypeStruct((nb, 8, D_MODEL), F32)],
        scratch_shapes=[pltpu.VMEM((tm, D_FF), F32)],
        compiler_params=_cp("arbitrary", "arbitrary"),
    )(x1, tgt, mod8, pre2, post2, w_ff1, w_ff2)


_PIECES = ((C_LRU_X, 1024), (C_LRU_G, 1024), (C_Z, 2048), (C_XBC, 4096), (C_GATES, 2048))
_NP = len(_PIECES)


def _piece_of(k, tk):
    col = k * tk
    for p, (c0, w) in enumerate(_PIECES):
        if c0 <= col < c0 + w:
            return p, (col - c0) // tk
    raise ValueError(col)


def in_proj_bwd(pieces, ddt, dx1, x2, mod8, pre1, w_main, w_dt, nb, seq, side=None):
    t = x2.shape[0]
    tm = min(512, seq)
    per_seq = seq // tm
    tk = 1024
    nk = PROJ_MAIN // tk
    where = [_piece_of(k, tk) for k in range(nk)]

    def piece_spec(p):
        first = min(k for k in range(nk) if where[k][0] == p)
        nblk = _PIECES[p][1] // tk
        return pl.BlockSpec((tm, tk), lambda b, s, k: (b * per_seq + s, jnp.clip(k - first, 0, nblk - 1)))

    def body(*refs):
        prefs = refs[:_NP]
        ddt_ref, dx1_ref, x_ref, mod_ref, pre_ref, w_ref, wdt_ref, gx_ref, vacc_ref, dmod_ref, acc_ref = refs[_NP:]
        b, s, k = pl.program_id(0), pl.program_id(1), pl.program_id(2)

        @pl.when((b == 0) & (s == 0) & (k == 0))
        def _():
            vacc_ref[...] = jnp.zeros_like(vacc_ref)

        @pl.when((s == 0) & (k == 0))
        def _():
            dmod_ref[...] = jnp.zeros_like(dmod_ref)

        @pl.when(k == 0)
        def _():
            acc_ref[...] = _dot_nt(ddt_ref[...], wdt_ref[...])

        for kk in range(nk):
            @pl.when(k == kk)
            def _(kk=kk):
                acc_ref[...] += _dot_nt(prefs[where[kk][0]][...], w_ref[:, kk * tk:(kk + 1) * tk])

        @pl.when(k == nk - 1)
        def _():
            dh = acc_ref[...]
            m = mod_ref[0]
            pre = pre_ref[...]
            xv = x_ref[...]
            rs = lax.rsqrt(_rowmean(xv * xv) + EPS)
            n = xv * rs
            dy = dh * (1.0 + m[1:2, :])
            dn = dy * pre
            gx_ref[...] = dx1_ref[...] + rs * (dn - n * _rowmean(dn * n))
            vacc_ref[...] += jnp.concatenate([_colsum(dy * n), jnp.zeros((7, D_MODEL), F32)], axis=0)
            dmod_ref[0] += jnp.concatenate([_colsum(dh), _colsum(dh * (n * pre)), jnp.zeros((6, D_MODEL), F32)], axis=0)

    row = lambda w: pl.BlockSpec((tm, w), lambda b, s, k: (b * per_seq + s, 0))
    return _call(
        body, name="in_proj_bwd", grid=(nb, per_seq, nk), side=side, sem=("arbitrary", "arbitrary", "arbitrary"),
        args=(*pieces, ddt, dx1, x2, mod8, pre1, w_main, w_dt),
        in_specs=[piece_spec(p) for p in range(_NP)] + [
            row(128), row(D_MODEL), row(D_MODEL), pl.BlockSpec((1, 8, D_MODEL), lambda b, s, k: (b, 0, 0)),
            pl.BlockSpec((1, D_MODEL), lambda b, s, k: (0, 0)),
            VMEM_FULL,
            pl.BlockSpec((D_MODEL, 128), lambda b, s, k: (0, 0))],
        out_specs=[row(D_MODEL), pl.BlockSpec((8, D_MODEL), lambda b, s, k: (0, 0)),
                   pl.BlockSpec((1, 8, D_MODEL), lambda b, s, k: (b, 0, 0))],
        out_shape=[jax.ShapeDtypeStruct((t, D_MODEL), F32), jax.ShapeDtypeStruct((8, D_MODEL), F32),
                   jax.ShapeDtypeStruct((nb, 8, D_MODEL), F32)],
        scratch_shapes=[pltpu.VMEM((tm, D_MODEL), F32)])


def in_proj_wgrad(h1t, pieces, ddt, name="in_proj_wgrad", side=None):
    nt, _, tt = h1t.shape
    tn = 1024
    nn = PROJ_MAIN // tn
    where = [_piece_of(n, tn) for n in range(nn)]

    def piece_spec(p):
        first = min(n for n in range(nn) if where[n][0] == p)
        nblk = _PIECES[p][1] // tn
        return pl.BlockSpec((tt, tn), lambda n, k: (jnp.where((n >= first) & (n < first + nblk), k, 0),
                                                    jnp.clip(n - first, 0, nblk - 1)))

    def body(h_ref, *refs):
        prefs = refs[:_NP]
        ddt_ref, dw_ref, dwdt_ref, acc_ref, accdt_ref = refs[_NP:]
        n, k = pl.program_id(0), pl.program_id(1)
        hv = h_ref[k]
        for nn_ in range(nn):
            @pl.when(n == nn_)
            def _(nn_=nn_):
                p = _dot(hv, prefs[where[nn_][0]][...])

                @pl.when(k == 0)
                def _():
                    acc_ref[...] = p

                @pl.when(k > 0)
                def _():
                    acc_ref[...] += p

        @pl.when(n == 0)
        def _():
            p = _dot(hv, ddt_ref[...])

            @pl.when(k == 0)
            def _():
                accdt_ref[...] = p

            @pl.when(k > 0)
            def _():
                accdt_ref[...] += p

        @pl.when(k == nt - 1)
        def _():
            dw_ref[...] = acc_ref[...].astype(BF16)

        @pl.when((n == 0) & (k == nt - 1))
        def _():
            dwdt_ref[...] = accdt_ref[...].astype(BF16)

    return _call(
        body, name=name, grid=(nn, nt), side=side, sem=("arbitrary", "arbitrary"), args=(h1t, *pieces, ddt),
        in_specs=[VMEM_FULL] + [piece_spec(p) for p in range(_NP)]
        + [pl.BlockSpec((tt, 128), lambda n, k: (k, 0))],
        out_specs=[pl.BlockSpec((D_MODEL, tn), lambda n, k: (0, n)), pl.BlockSpec((D_MODEL, 128), lambda n, k: (0, 0))],
        out_shape=[jax.ShapeDtypeStruct((D_MODEL, PROJ_MAIN), BF16), jax.ShapeDtypeStruct((D_MODEL, 128), BF16)],
        scratch_shapes=[pltpu.VMEM((D_MODEL, tn), F32), pltpu.VMEM((D_MODEL, 128), F32)])


def _log1p(u):
    w = 1.0 + u
    return jnp.log(w) - ((w - 1.0) - u) / w


def _softplus(x):
    return jnp.maximum(x, 0.0) + _log1p(jnp.exp(-jnp.abs(x)))


def _head_mask(h):
    lane = lax.broadcasted_iota(jnp.int32, (1, SSD_GW), 1)
    return (lane >= SSD_P * h) & (lane < SSD_P * (h + 1))


def _pair(p):
    return slice(2 * SSD_P * p, 2 * SSD_P * (p + 1))


def _expand4(m, g):
    lane = lax.broadcasted_iota(jnp.int32, (1, SSD_GW), 1)
    col = lambda h: m[:, 4 * g + h:4 * g + h + 1]
    return jnp.where(lane < SSD_P, col(0), jnp.where(lane < 2 * SSD_P, col(1), jnp.where(lane < 3 * SSD_P, col(2), col(3))))


def _reduce4(v, g):
    lane = lax.broadcasted_iota(jnp.int32, (1, SSD_N), 1)
    out = jnp.zeros((v.shape[0], SSD_N), F32)
    for h in range(4):
        s = jnp.sum(jnp.where(_head_mask(h), v, 0.0), axis=1, keepdims=True)
        out = out + jnp.where(lane == 4 * g + h, s, 0.0)
    return out


def _ssd_heads(dtraw, hp, tri):
    xdt = dtraw + hp[0:1, :]
    dt = _softplus(xdt)
    cs = _dot_hi(tri, dt * hp[1:2, :])
    cs_last = cs[SSD_L - 1:SSD_L, :]
    return dict(xdt=xdt, dt=dt, cs=cs, cs_t=cs.T, e=jnp.exp(cs), w=jnp.exp(cs_last - cs), el=jnp.exp(cs_last))


def _ssd_group(g, hd, xs_b, bm_b, cm_b, d_x, st, paired=False):
    ll = SSD_L
    xs = xs_b.astype(F32)
    cs, cs_t = hd["cs"], hd["cs_t"]
    e_x, w_x, el_x, dt_x = _expand4(hd["e"], g), _expand4(hd["w"], g), _expand4(hd["el"], g), _expand4(hd["dt"], g)
    xd = xs * dt_x
    gcb = _dot_nt(cm_b, bm_b)
    ri = lax.broadcasted_iota(jnp.int32, (ll, ll), 0)
    ci = lax.broadcasted_iota(jnp.int32, (ll, ll), 1)
    dks, ms = [], []
    for h in range(4):
        k = 4 * g + h
        dk = jnp.exp(jnp.where(ri >= ci, cs[:, k:k + 1] - cs_t[k:k + 1, :], -1e30))
        dks.append(dk)
        ms.append((gcb * dk).astype(BF16))
    xdb = xd.astype(BF16)
    if paired:
        first = lax.broadcasted_iota(jnp.int32, (1, 2 * SSD_P), 1) < SSD_P
        ydiag = jnp.concatenate(
            [jnp.where(first, _dot(ms[2 * p], xdb[:, _pair(p)]), _dot(ms[2 * p + 1], xdb[:, _pair(p)]))
             for p in range(2)], axis=1)
    else:
        ydiag = jnp.zeros((ll, SSD_GW), F32)
        for h in range(4):
            ydiag = ydiag + _dot(ms[h], jnp.where(_head_mask(h), xd, 0.0).astype(BF16))
    yoff = _dot(cm_b, st.astype(BF16)) * e_x
    y = ydiag + yoff + d_x * xs
    st_new = st * el_x + _dot(bm_b.astype(F32).T.astype(BF16), (xd * w_x).astype(BF16))
    return dict(xs=xs, e_x=e_x, w_x=w_x, el_x=el_x, dt_x=dt_x, xd=xd, xdb=xdb, gcb=gcb, dks=dks, ms=ms, yoff=yoff, y=y,
                st_new=st_new)


def ssd_consts():
    hh = np.arange(SSD_N)
    tri = (hh[:, None] >= hh[None, :]).astype(np.float32)
    return jnp.asarray(tri), jnp.asarray(tri.T)


def ssd_params(dt_bias, a_log, d_skip, norm_w):
    padh = lambda v: jnp.pad(v.reshape(1, SSD_HEADS), ((0, 0), (0, SSD_N - SSD_HEADS)))
    hp = jnp.concatenate([padh(dt_bias), padh(-jnp.exp(a_log)), jnp.zeros((6, SSD_N), F32)], axis=0)
    lp = jnp.concatenate([norm_w.reshape(1, SSD_INNER), jnp.repeat(d_skip, SSD_P).reshape(1, SSD_INNER),
                          jnp.zeros((6, SSD_INNER), F32)], axis=0)
    return hp, lp


def _b_cols(g):
    return slice(SSD_INNER + g * SSD_N, SSD_INNER + (g + 1) * SSD_N)


def _c_cols(g):
    return slice(SSD_INNER + (SSD_G + g) * SSD_N, SSD_INNER + (SSD_G + g + 1) * SSD_N)


def _ssd_specs(nc, rc):
    return [pl.BlockSpec((SSD_L, 2 * SSD_INNER), lambda b, c: (b * nc + rc(c), 0)),
            pl.BlockSpec((SSD_L, SSD_INNER), lambda b, c: (b * nc + rc(c), C_Z // SSD_INNER)),
            pl.BlockSpec((SSD_L, SSD_N), lambda b, c: (b * nc + rc(c), 0))]


def ssd_fwd(xbc, proj, dtraw, hp, lp, tri, nb, seq):
    t = xbc.shape[0]
    nc = seq // SSD_L

    def body(xbc_ref, z_ref, dt_ref, hp_ref, lp_ref, tri_ref, y_ref, sts_ref, st_scr):
        @pl.when(pl.program_id(1) == 0)
        def _():
            st_scr[...] = jnp.zeros_like(st_scr)

        hd = _ssd_heads(dt_ref[...], hp_ref[...], tri_ref[...])
        for g in range(SSD_G):
            gs = slice(g * SSD_GW, (g + 1) * SSD_GW)
            st = st_scr[g]
            sts_ref[0, g] = st
            f = _ssd_group(g, hd, xbc_ref[:, gs], xbc_ref[:, _b_cols(g)], xbc_ref[:, _c_cols(g)], lp_ref[1:2, gs], st,
                           paired=True)
            st_scr[g] = f["st_new"]
            zf = z_ref[:, gs].astype(F32)
            yg = f["y"] * (zf * _sigmoid(zf))
            y_ref[:, gs] = (yg * lax.rsqrt(_rowmean(yg * yg) + EPS) * lp_ref[0:1, gs]).astype(BF16)

    return pl.pallas_call(
        body, name="ssd_fwd", grid=(nb, nc),
        in_specs=_ssd_specs(nc, lambda c: c) + [VMEM_FULL, VMEM_FULL, VMEM_FULL],
        out_specs=[pl.BlockSpec((SSD_L, SSD_INNER), lambda b, c: (b * nc + c, 0)),
                   pl.BlockSpec((1, SSD_G, SSD_N, SSD_GW), lambda b, c: (b * nc + c, 0, 0, 0))],
        out_shape=[jax.ShapeDtypeStruct((t, SSD_INNER), BF16),
                   jax.ShapeDtypeStruct((nb * nc, SSD_G, SSD_N, SSD_GW), F32)],
        scratch_shapes=[pltpu.VMEM((SSD_G, SSD_N, SSD_GW), F32)],
        compiler_params=_cp("arbitrary", "arbitrary"),
    )(xbc, proj, dtraw, hp, lp, tri)


def ssd_bwd(xbc, proj, dtraw, hp, lp, tri, triu, states, dyn, nb, seq):
    t = xbc.shape[0]
    nc = seq // SSD_L
    ll = SSD_L

    def body(xbc_ref, z_ref, dt_ref, sts_ref, dy_ref, hp_ref, lp_ref, tri_ref, triu_ref,
             dxbc_ref, dz_ref, ddt_ref, hpg_ref, lpg_ref, dst_scr):
        b, c_i = pl.program_id(0), pl.program_id(1)

        @pl.when((b == 0) & (c_i == 0))
        def _():
            hpg_ref[...] = jnp.zeros_like(hpg_ref)
            lpg_ref[...] = jnp.zeros_like(lpg_ref)

        @pl.when(c_i == 0)
        def _():
            dst_scr[...] = jnp.zeros_like(dst_scr)

        hp = hp_ref[...]
        hd = _ssd_heads(dt_ref[...], hp, tri_ref[...])
        lane = lax.broadcasted_iota(jnp.int32, (1, SSD_N), 1)
        subl = lax.broadcasted_iota(jnp.int32, (SSD_N, 1), 0)
        dcs = jnp.zeros((ll, SSD_N), F32)
        dcs_t = jnp.zeros((SSD_N, ll), F32)
        last = jnp.zeros((1, SSD_N), F32)
        dxx = jnp.zeros((ll, SSD_N), F32)
        for g in range(SSD_G):
            gs = slice(g * SSD_GW, (g + 1) * SSD_GW)
            st = sts_ref[0, g]
            dst = dst_scr[g]
            bm_b, cm_b = xbc_ref[:, _b_cols(g)], xbc_ref[:, _c_cols(g)]
            d_x = lp_ref[1:2, gs]
            f = _ssd_group(g, hd, xbc_ref[:, gs], bm_b, cm_b, d_x, st)
            xs, xd, gcb = f["xs"], f["xd"], f["gcb"]
            e_x, w_x, el_x, dt_x = f["e_x"], f["w_x"], f["el_x"], f["dt_x"]
            stb, dstb = st.astype(BF16), dst.astype(BF16)
            zf = z_ref[:, gs].astype(F32)
            sg = _sigmoid(zf)
            sz = zf * sg
            yg = f["y"] * sz
            rstd = lax.rsqrt(_rowmean(yg * yg) + EPS)
            n = yg * rstd
            dyn_v = dy_ref[:, gs].astype(F32)
            dn = dyn_v * lp_ref[0:1, gs]
            dyg = rstd * (dn - n * _rowmean(dn * n))
            dy = dyg * sz
            dz_ref[:, gs] = (dyg * f["y"] * (sg * (1.0 + zf * (1.0 - sg)))).astype(BF16)
            dyb = dy.astype(BF16)
            r_ = _dot(bm_b, dstb)
            dxd = w_x * r_
            dqb = (dy * e_x).astype(BF16)
            dcm = _dot_nt(dqb, stb)
            dst_scr[g] = dst * el_x + _dot_tn(cm_b, dqb)
            dbm = _dot_nt((xd * w_x).astype(BF16), dstb)
            xdb = f["xdb"]
            dgm = jnp.zeros((ll, ll), F32)
            for h in range(4):
                k = 4 * g + h
                hm = _head_mask(h)
                dxd = dxd + jnp.where(hm, _dot_tn(f["ms"][h], dyb), 0.0)
                dm = _dot_nt(jnp.where(hm, dy, 0.0).astype(BF16), xdb) * f["dks"][h]
                dgm = dgm + dm
                dseg = dm * gcb
                dcs = dcs + jnp.where(lane == k, jnp.sum(dseg, axis=1, keepdims=True), 0.0)
                dcs_t = dcs_t + jnp.where(subl == k, jnp.sum(dseg, axis=0, keepdims=True), 0.0)
            dgmb = dgm.astype(BF16)
            dxbc_ref[:, _c_cols(g)] = (dcm + _dot(dgmb, bm_b)).astype(BF16)
            dxbc_ref[:, _b_cols(g)] = (dbm + _dot_tn(dgmb, cm_b)).astype(BF16)
            v = _reduce4(r_ * xd * w_x, g)
            dcs = dcs + _reduce4(dy * f["yoff"], g) - v
            last = last + _colsum(v) + _reduce4(_colsum(dst * st) * el_x, g)
            dxx = dxx + _reduce4(dxd * xs, g)
            dxbc_ref[:, gs] = (d_x * dy + dxd * dt_x).astype(BF16)
            lpg_ref[0:1, gs] += _colsum(dyn_v * n)
            lpg_ref[1:2, gs] += _colsum(dy * xs)
        rowi = lax.broadcasted_iota(jnp.int32, (ll, 1), 0)
        da = _dot_hi(triu_ref[...], dcs - dcs_t.T + jnp.where(rowi == ll - 1, last, 0.0))
        ddt = (dxx + da * hp[1:2, :]) * _sigmoid(hd["xdt"])
        ddt_ref[...] = ddt
        hpg_ref[...] += jnp.concatenate([_colsum(ddt), _colsum(da * hd["dt"]), jnp.zeros((6, SSD_N), F32)], axis=0)

    rc = lambda c: nc - 1 - c
    return pl.pallas_call(
        body, name="ssd_bwd", grid=(nb, nc),
        in_specs=_ssd_specs(nc, rc) + [
            pl.BlockSpec((1, SSD_G, SSD_N, SSD_GW), lambda b, c: (b * nc + rc(c), 0, 0, 0)),
            pl.BlockSpec((SSD_L, SSD_INNER), lambda b, c: (b * nc + rc(c), 0)),
            VMEM_FULL, VMEM_FULL, VMEM_FULL, VMEM_FULL],
        out_specs=[pl.BlockSpec((SSD_L, 2 * SSD_INNER), lambda b, c: (b * nc + rc(c), 0)),
                   pl.BlockSpec((SSD_L, SSD_INNER), lambda b, c: (b * nc + rc(c), 0)),
                   pl.BlockSpec((SSD_L, SSD_N), lambda b, c: (b * nc + rc(c), 0)),
                   pl.BlockSpec((8, SSD_N), lambda b, c: (0, 0)),
                   pl.BlockSpec((8, SSD_INNER), lambda b, c: (0, 0))],
        out_shape=[jax.ShapeDtypeStruct((t, 2 * SSD_INNER), BF16), jax.ShapeDtypeStruct((t, SSD_INNER), BF16),
                   jax.ShapeDtypeStruct((t, SSD_N), F32), jax.ShapeDtypeStruct((8, SSD_N), F32),
                   jax.ShapeDtypeStruct((8, SSD_INNER), F32)],
        scratch_shapes=[pltpu.VMEM((SSD_G, SSD_N, SSD_GW), F32)],
        compiler_params=_cp("arbitrary", "arbitrary"),
    )(xbc, proj, dtraw, states, dyn, hp, lp, tri, triu)


def ada_fwd(c_all, w_cols, b_cols):
    def body(c_ref, w_ref, b_ref, o_ref):
        cv = c_ref[...]
        o_ref[...] = _dot_hi(cv * _sigmoid(cv), w_ref[...]) + b_ref[...]

    return pl.pallas_call(body, name="ada_fwd", out_shape=jax.ShapeDtypeStruct((c_all.shape[0], w_cols.shape[1]), F32),
                          compiler_params=pltpu.CompilerParams(vmem_limit_bytes=VMEM_LIMIT))(c_all, w_cols, b_cols)


def ada_bwd(c_all, dmod_cols, dmod_all):
    def body(c_ref, dc_ref, da_ref, gw_ref, gb_ref):
        cv = c_ref[...]
        gw_ref[...] = lax.dot_general(cv * _sigmoid(cv), dc_ref[...], (((0,), (0,)), ((), ())),
                                      precision=lax.Precision.HIGHEST, preferred_element_type=F32)
        gb_ref[...] = _colsum(da_ref[...])

    return pl.pallas_call(
        body, name="ada_bwd",
        out_shape=[jax.ShapeDtypeStruct((c_all.shape[1], dmod_cols.shape[1]), F32),
                   jax.ShapeDtypeStruct((1, dmod_all.shape[1]), F32)],
        compiler_params=pltpu.CompilerParams(vmem_limit_bytes=VMEM_LIMIT))(c_all, dmod_cols, dmod_all)


def _adam_update(g, w, m, v):
    m2 = ADAM_B1 * m + (1.0 - ADAM_B1) * g
    v2 = ADAM_B2 * v + (1.0 - ADAM_B2) * (g * g)
    m_hat = m2 / (1.0 - ADAM_B1 ** ADAM_STEP)
    v_hat = v2 / (1.0 - ADAM_B2 ** ADAM_STEP)
    return -ADAM_LR * (m_hat / (jnp.sqrt(v_hat) + ADAM_EPS) + ADAM_WD * w), m2, v2


def adamw(parts, w, m, v, name):
    n, r, c = parts.shape
    tr = r if r <= 256 else 128

    def body(p_ref, w_ref, m_ref, v_ref, g_ref, d_ref, nm_ref, nv_ref):
        g = p_ref[0].astype(F32)
        for s in range(1, n):
            g = g + p_ref[s].astype(F32)
        g_ref[0] = g
        d_ref[0], nm_ref[0], nv_ref[0] = _adam_update(g, w_ref[0], m_ref[0], v_ref[0])

    blk = pl.BlockSpec((1, tr, c), lambda i: (0, i, 0))
    return pl.pallas_call(
        body, name=name, grid=(r // tr,),
        in_specs=[pl.BlockSpec((n, tr, c), lambda i: (0, i, 0)), blk, blk, blk], out_specs=[blk] * 4,
        out_shape=[jax.ShapeDtypeStruct((1, r, c), F32)] * 4,
        compiler_params=_cp("parallel"),
    )(parts, w, m, v)


SMALL_SRC = {
    'pre_norm1': ('vin', 0, 1024), 'post_norm1': ('vmg', 1, 1024), 'b_gate': ('vmg', 0, 2048),
    'lru_conv_b': ('accl', 4, 1024), 'lru_wa': ('gwa', None, None), 'lru_ba': ('dvec', 0, 1024),
    'lru_wx': ('gwx', None, None), 'lru_bx': ('dvec', 1, 1024), 'lru_lambda': ('dvec', 2, 1024),
    'ssd_conv_b': ('accs', 4, 4096), 'ssd_dt_bias': ('hpg', 0, SSD_HEADS), 'ssd_a_log': ('hpg', 1, SSD_HEADS),
    'ssd_d': ('lpg', 1, SSD_INNER), 'ssd_norm_w': ('lpg', 0, SSD_INNER), 'pre_norm2': ('vmlp', 0, 1024),
    'post_norm2': ('vmlp', 1, 1024)}
SMALL_ACCS = ('vin', 'vmg', 'vmlp', 'dvec', 'accl', 'accs', 'hpg', 'lpg', 'gwa', 'gwx')


def adamw_small(gathered, params):
    names = tuple(params)
    na = len(SMALL_ACCS)

    def body(*refs):
        acc = {k: functools.reduce(lambda p, q: p + q, [refs[i][s] for s in range(NDEV)])
               for i, k in enumerate(SMALL_ACCS)}
        ins = refs[na:na + 3 * len(names)]
        outs = refs[na + 3 * len(names):]
        for j, k in enumerate(names):
            w_ref, m_ref, v_ref = ins[3 * j:3 * j + 3]
            src, row, width = SMALL_SRC[k]
            wv = w_ref[...]
            if row is None:
                g = acc[src]
            elif k == 'ssd_d':
                li = lax.broadcasted_iota(jnp.int32, (SSD_INNER, SSD_N), 0)
                hi = lax.broadcasted_iota(jnp.int32, (SSD_INNER, SSD_N), 1)
                g = _dot_hi(acc[src], jnp.where(jnp.right_shift(li, 6) == hi, 1.0, 0.0))[row:row + 1, :SSD_HEADS]
            else:
                g = acc[src][row:row + 1, :width]
            if k == 'lru_lambda':
                g = g * (-1.0 / (1.0 + jnp.exp(wv)))
            if k == 'ssd_a_log':
                g = g * (-jnp.exp(wv))
            o = outs[4 * j:4 * j + 4]
            o[0][...] = g
            o[1][...], o[2][...], o[3][...] = _adam_update(g, wv, m_ref[...], v_ref[...])
        outs[-2][...] = acc['accl'][0:4, :]
        outs[-1][...] = acc['accs'][0:4, :]

    flat = [a for k in names for a in params[k]]
    out_shape = [jax.ShapeDtypeStruct(params[k][0].shape, F32) for k in names for _ in range(4)]
    out_shape += [jax.ShapeDtypeStruct((4, D_MODEL), F32), jax.ShapeDtypeStruct((4, 2 * SSD_INNER), F32)]
    res = pl.pallas_call(body, name="adamw_small", out_shape=out_shape,
                         compiler_params=pltpu.CompilerParams(vmem_limit_bytes=VMEM_LIMIT))(
        *[gathered[k] for k in SMALL_ACCS], *flat)
    return {k: res[4 * j:4 * j + 4] for j, k in enumerate(names)}, res[-2], res[-1]


def _dev_index(px, py, pc):
    return 4 * px + 2 * py + pc


class _Exchange:
    def __init__(self, arrs):
        self.arrs = list(arrs)
        self.na = len(self.arrs)
        self.scratch = [pltpu.SemaphoreType.DMA((7 * self.na,)), pltpu.SemaphoreType.DMA((7 * self.na,)),
                        pltpu.SemaphoreType.DMA((self.na,))]


class Gather(_Exchange):
    def __init__(self, arrs):
        super().__init__(arrs)
        self.out_shape = [jax.ShapeDtypeStruct((NDEV,) + a.shape, a.dtype) for a in self.arrs]

    def _plan(self, ins, outs, sems):
        na = self.na
        send_sems, recv_sems, local_sems = sems
        x, y, c = lax.axis_index("x"), lax.axis_index("y"), lax.axis_index("c")
        me, sibling = (x, y, c), (x, y, 1 - c)
        chips = [(1 - x, y), (x, 1 - y), (1 - x, 1 - y)]

        def copy(a, k, block, to, src=None):
            dst = outs[a].at[_dev_index(*block)]
            return pltpu.make_async_remote_copy(
                src_ref=dst if src is None else src, dst_ref=dst, send_sem=send_sems.at[a * 7 + k],
                recv_sem=recv_sems.at[a * 7 + k], device_id=to, device_id_type=MESH)

        mine = [pltpu.make_async_copy(ins[a], outs[a].at[_dev_index(*me)], local_sems.at[a]) for a in range(na)]
        first = []
        for a in range(na):
            first.append(copy(a, 0, me, sibling, src=ins[a]))
            first += [copy(a, 1 + j, me, (*chip, c), src=ins[a]) for j, chip in enumerate(chips)]
        return copy, mine, first, me, sibling, chips, c

    def start(self, ins, outs, sems):
        _, mine, first, *_ = self._plan(ins, outs, sems)
        for cp in mine + first:
            cp.start()

    def finish(self, ins, outs, sems):
        copy, mine, first, me, sibling, chips, c = self._plan(ins, outs, sems)
        passed = []
        for j, chip in enumerate(chips):
            for a in range(self.na):
                copy(a, 1 + j, (*chip, c), me).wait_recv()
                cp = copy(a, 4 + j, (*chip, c), sibling)
                cp.start()
                passed.append(cp)
        for a in range(self.na):
            copy(a, 0, sibling, me).wait_recv()
            for j, chip in enumerate(chips):
                copy(a, 4 + j, (*chip, 1 - c), me).wait_recv()
        for cp in first + passed:
            cp.wait_send()
        for cp in mine:
            cp.wait()


class Scatter(_Exchange):
    def __init__(self, arrs):
        super().__init__(arrs)
        self.out_shape = [jax.ShapeDtypeStruct(a.shape, a.dtype) for a in self.arrs]

    def _plan(self, ins, outs, sems, arrivals):
        send_sems, recv_sems, local_sems = sems
        x, y, c = lax.axis_index("x"), lax.axis_index("y"), lax.axis_index("c")
        me = _dev_index(x, y, c)
        masks = [(mx, my, mc) for mx in (0, 1) for my in (0, 1) for mc in (0, 1)][1:]
        flip = lambda v, bit: 1 - v if bit else v
        mine = [pltpu.make_async_copy(ins[a].at[me], outs[a].at[me], local_sems.at[a]) for a in range(self.na)]
        sends, recvs = [], []
        for k, (mx, my, mc) in enumerate(masks):
            peer = (flip(x, mx), flip(y, my), flip(c, mc))
            pidx = _dev_index(*peer)
            for a in range(self.na):
                on = dict(send_sem=send_sems.at[a * 7 + k], recv_sem=recv_sems.at[a * 7 + k], device_id=peer,
                          device_id_type=MESH)
                sends.append(pltpu.make_async_remote_copy(src_ref=ins[a].at[pidx], dst_ref=outs[a].at[me], **on))
                if arrivals:
                    recvs.append(pltpu.make_async_remote_copy(src_ref=ins[a].at[pidx], dst_ref=outs[a].at[pidx], **on))
        return mine, sends, recvs

    def start(self, ins, outs, sems):
        mine, sends, _ = self._plan(ins, outs, sems, arrivals=False)
        for cp in mine + sends:
            cp.start()

    def finish(self, ins, outs, sems):
        mine, sends, recvs = self._plan(ins, outs, sems, arrivals=True)
        for cp in recvs:
            cp.wait_recv()
        for cp in sends:
            cp.wait_send()
        for cp in mine:
            cp.wait()


def exchange_call(ex, name):
    na = ex.na

    def body(*refs):
        ins, outs, sems = refs[:na], refs[na:2 * na], refs[2 * na:]
        ex.start(ins, outs, sems)
        ex.finish(ins, outs, sems)

    return pl.pallas_call(body, name=name, in_specs=[ANY] * na, out_specs=[ANY] * na, out_shape=ex.out_shape,
                          scratch_shapes=ex.scratch)(*ex.arrs)


def all_gather(arrs, name):
    return exchange_call(Gather(arrs), name)


def _call(body, *, name, grid, in_specs, out_specs, out_shape, scratch_shapes=(), sem, args, side=None):
    if side is None:
        outs = pl.pallas_call(body, name=name, grid=grid, in_specs=list(in_specs), out_specs=list(out_specs),
                              out_shape=list(out_shape), scratch_shapes=list(scratch_shapes),
                              compiler_params=_cp(*sem))(*args)
        return outs, []
    ni, no, ns, na = len(in_specs), len(out_specs), len(scratch_shapes), side.na

    def wrapped(*refs):
        ins, s_in = refs[:ni], refs[ni:ni + na]
        outs, s_out = refs[ni + na:ni + na + no], refs[ni + na + no:ni + 2 * na + no]
        scr, sems = refs[ni + 2 * na + no:ni + 2 * na + no + ns], refs[ni + 2 * na + no + ns:]
        pids = [pl.program_id(i) for i in range(len(grid))]
        first = functools.reduce(lambda p, q: p & q, [p == 0 for p in pids])
        last = functools.reduce(lambda p, q: p & q, [p == g - 1 for p, g in zip(pids, grid)])

        @pl.when(first)
        def _():
            side.start(s_in, s_out, sems)

        body(*ins, *outs, *scr)

        @pl.when(last)
        def _():
            side.finish(s_in, s_out, sems)

    outs = pl.pallas_call(
        wrapped, name=name, grid=grid, in_specs=list(in_specs) + [ANY] * na, out_specs=list(out_specs) + [ANY] * na,
        out_shape=list(out_shape) + side.out_shape, scratch_shapes=list(scratch_shapes) + side.scratch,
        compiler_params=_cp(*["arbitrary"] * len(grid)))(*args, *side.arrs)
    return outs[:no], outs[no:]


WEIGHTS = ('w_ada', 'b_ada', 'pre_norm1', 'post_norm1', 'w_in', 'b_gate', 'lru_conv_w', 'lru_conv_b', 'lru_wa',
           'lru_ba', 'lru_wx', 'lru_bx', 'lru_lambda', 'w_pa', 'ssd_conv_w', 'ssd_conv_b', 'ssd_dt_bias', 'ssd_a_log',
           'ssd_d', 'ssd_norm_w', 'w_pb', 'w_out', 'pre_norm2', 'post_norm2', 'w_ff1', 'w_ff2')
BIG = ('w_in', 'w_pa', 'w_pb', 'w_out', 'w_ff1', 'w_ff2')
REPL = ('pre_norm1', 'post_norm1', 'b_gate', 'lru_conv_b', 'lru_wa', 'lru_ba', 'lru_wx', 'lru_bx', 'lru_lambda',
        'ssd_conv_b', 'ssd_dt_bias', 'ssd_a_log', 'ssd_d', 'ssd_norm_w', 'pre_norm2', 'post_norm2')
LANES = 1024


def _rows(n):
    return -(-n // LANES)


def _pack(vals, total_rows):
    parts = []
    for v in vals:
        f = v.reshape(-1).astype(F32)
        parts.append(jnp.pad(f, (0, _rows(f.shape[0]) * LANES - f.shape[0])))
    flat = jnp.concatenate(parts)
    return jnp.pad(flat.reshape(-1, LANES), ((0, total_rows - flat.shape[0] // LANES), (0, 0)))


def _unpack(slab, shapes):
    out, r = [], 0
    for s in shapes:
        n = int(np.prod(s))
        out.append(slab[r:r + _rows(n)].reshape(-1)[:n].reshape(s))
        r += _rows(n)
    return out


def _block_diag4(w):
    w4 = w.reshape(4, 4, 64, 64)
    eye = jnp.eye(4, dtype=w.dtype)
    return (w4[:, :, :, None, :] * eye[None, :, None, :, None]).reshape(4, LRU_BLOCK, LRU_BLOCK)


def _diag_blocks4(m):
    m5 = m.reshape(4, 4, 64, 4, 64)
    return jnp.stack([m5[:, a, :, a, :] for a in range(4)], axis=1).reshape(LRU_HEADS, 64, 64)


def kernel(x, c, w_ada, b_ada, pre_norm1, post_norm1, w_in, b_gate, lru_conv_w, lru_conv_b, lru_wa, lru_ba, lru_wx, lru_bx, lru_lambda, w_pa, ssd_conv_w, ssd_conv_b, ssd_dt_bias, ssd_a_log, ssd_d, ssd_norm_w, w_pb, w_out, pre_norm2, post_norm2, w_ff1, w_ff2, loss_target, m_w_ada, m_b_ada, m_pre_norm1, m_post_norm1, m_w_in, m_b_gate, m_lru_conv_w, m_lru_conv_b, m_lru_wa, m_lru_ba, m_lru_wx, m_lru_bx, m_lru_lambda, m_w_pa, m_ssd_conv_w, m_ssd_conv_b, m_ssd_dt_bias, m_ssd_a_log, m_ssd_d, m_ssd_norm_w, m_w_pb, m_w_out, m_pre_norm2, m_post_norm2, m_w_ff1, m_w_ff2, v_w_ada, v_b_ada, v_pre_norm1, v_post_norm1, v_w_in, v_b_gate, v_lru_conv_w, v_lru_conv_b, v_lru_wa, v_lru_ba, v_lru_wx, v_lru_bx, v_lru_lambda, v_w_pa, v_ssd_conv_w, v_ssd_conv_b, v_ssd_dt_bias, v_ssd_a_log, v_ssd_d, v_ssd_norm_w, v_w_pb, v_w_out, v_pre_norm2, v_post_norm2, v_w_ff1, v_w_ff2):
    given = dict(locals())
    w = {k: given[k] for k in WEIGHTS}
    mom = {k: given["m_" + k] for k in WEIGHTS}
    var = {k: given["v_" + k] for k in WEIGHTS}
    nb, seq, _ = x.shape
    assert nb == 2 and seq % 512 == 0, (nb, seq)
    t = nb * seq
    me = _dev_index(lax.axis_index("x"), lax.axis_index("y"), lax.axis_index("c"))
    x2 = x.reshape(t, D_MODEL)
    tgt2 = loss_target.reshape(t, D_MODEL)
    ada_cols = w_ada.shape[2]

    slab = jnp.zeros((16, LANES), F32)
    slab = slab.at[0:nb].set(c)
    slab = slab.at[2:6, 0:lru_conv_w.shape[2]].set(lru_conv_w[0])
    slab = slab.at[6:10, 0:ssd_conv_w.shape[2]].set(ssd_conv_w[0])
    (g1,) = all_gather([slab], "gather_cond")
    c_all = g1[:, 0:nb].reshape(NDEV * nb, D_MODEL)
    lru_cw = g1[:, 2:6, 0:lru_conv_w.shape[2]].transpose(1, 0, 2).reshape(4, D_MODEL)
    ssd_cw = g1[:, 6:10, 0:ssd_conv_w.shape[2]].transpose(1, 0, 2).reshape(4, 2 * SSD_INNER)
    b_cols = lax.dynamic_slice(b_ada, (0, me * ada_cols), (1, ada_cols))
    mod_cols = ada_fwd(c_all, w_ada[0], b_cols)
    (g2,) = all_gather([mod_cols], "gather_mod")
    mod_all = g2.transpose(1, 0, 2).reshape(NDEV * nb, N_MOD * D_MODEL)
    mod_mine = lax.dynamic_slice(mod_all, (me * nb, 0), (nb, N_MOD * D_MODEL)).reshape(nb, N_MOD, D_MODEL)
    mod8 = jnp.pad(mod_mine, ((0, 0), (0, 8 - N_MOD), (0, 0)))

    (gw_in,) = all_gather([w_in.astype(BF16)], "gather_w_in")
    shard = IN_DIM // NDEV
    kd, od = DT_COL0 // shard, DT_COL0 % shard
    assert od + SSD_HEADS <= shard
    gb = gw_in[:, 0]
    w_main = jnp.concatenate([gb[k] for k in range(kd)] + [gb[kd][:, :od], gb[kd][:, od + SSD_HEADS:]]
                             + [gb[k] for k in range(kd + 1, NDEV)], axis=1)
    w_dt = jnp.pad(gb[kd][:, od:od + SSD_HEADS], ((0, 0), (0, 128 - SSD_HEADS)))

    wa_bd = _block_diag4(lru_wa[0]).astype(BF16)
    wx_bd = _block_diag4(lru_wx[0]).astype(BF16)
    lam = lru_lambda[0]
    vec = _pack([lru_ba, lru_bx, jax.nn.softplus(-lam)], 8)
    tri, triu = ssd_consts()
    hp, lp = ssd_params(ssd_dt_bias[0], ssd_a_log[0], ssd_d[0], ssd_norm_w[0])

    rest = Gather([w[k].astype(BF16) for k in BIG[1:]])
    (proj, h1t, dtraw, xa, xbc), gw = in_proj_fwd(
        x2, mod8, pre_norm1, w_main, w_dt, jnp.concatenate([lru_cw, ssd_cw], axis=1),
        jnp.concatenate([lru_conv_b, ssd_conv_b], axis=1), seq, side=rest)
    w_pa_f = gw[0].reshape(D_MODEL, D_MODEL)
    w_pb_f = gw[1].reshape(SSD_INNER, D_MODEL)
    w_out_f = gw[2].reshape(D_MODEL, D_MODEL)
    w_ff1_f = gw[3][:, 0]
    w_ff2_f = gw[4].reshape(D_FF, D_MODEL)
    ya_in, hst = lru_fwd(xa, proj, wa_bd, wx_bd, vec, nb, seq)
    yb_in, states = ssd_fwd(xbc, proj, dtraw, hp, lp, tri, nb, seq)
    yab, out1, x1 = merge_fwd(ya_in, yb_in, proj, x2, mod8, b_gate, post_norm1, w_pa_f, w_pb_f, w_out_f, seq)

    dx1, h2, da1, act, dy2, loss8, vacc_mlp, dmod_mlp = mlp_fwd_bwd(
        x1, tgt2, mod8, pre_norm2, post_norm2, w_ff1_f, w_ff2_f, nb, seq)
    wg = dict(out_dtype=BF16, ta=True, tm=1024, tn=1024, tk=1024)
    dw_ff1 = matmul(h2, da1, name="wgrad_ff1", blocked_out=True, **{**wg, "tn": D_FF // NDEV})
    dw_ff2 = matmul(act, dy2, name="wgrad_ff2", **wg)
    dya_in, dyb_in, dgates, dyab, dout1, merged, vacc_mg, dmod_mg = merge_bwd(
        dx1, out1, yab, proj, mod8, b_gate, post_norm1, w_pa_f, w_pb_f, w_out_f, nb, seq)
    dw_out = matmul(merged, dout1, name="wgrad_out", **wg)
    dw_pa = matmul(ya_in, dyab, name="wgrad_pa", n=D_MODEL, b_off=0, **wg)
    dw_pb = matmul(yb_in, dyab, name="wgrad_pb", n=D_MODEL, b_off=1, **wg)
    by_rows = lambda g: g.reshape(NDEV, g.shape[0] // NDEV, g.shape[1])
    (dxa, dlg, dwa_bd, dwx_bd, dvec), parts_ff = lru_bwd(
        dya_in, xa, proj, hst, wa_bd, wx_bd, vec, nb, seq, side=Scatter([dw_ff1, by_rows(dw_ff2)]))
    dxbc, dz, ddt, hpg, lpg = ssd_bwd(xbc, proj, dtraw, hp, lp, tri, triu, states, dyb_in, nb, seq)
    (dlx, acc_l), _ = conv_bwd(proj, C_LRU_X, D_MODEL, lru_cw, lru_conv_b, dxa, nb, seq, False, "conv_lru_bwd")
    (dxr, acc_s), parts_mg = conv_bwd(proj, C_XBC, 2 * SSD_INNER, ssd_cw, ssd_conv_b, dxbc, nb, seq, True,
                                      "conv_ssd_bwd", side=Scatter([by_rows(dw_pa), by_rows(dw_pb), by_rows(dw_out)]))
    pieces = (dlx, dlg, dz, dxr, dgates)
    ddt_b = ddt.astype(BF16)
    accs = dict(vmg=vacc_mg, vmlp=vacc_mlp, dvec=dvec, accl=acc_l, accs=acc_s, hpg=hpg, lpg=lpg,
                gwa=_diag_blocks4(dwa_bd).reshape(LRU_HEADS * 64, 64), gwx=_diag_blocks4(dwx_bd).reshape(LRU_HEADS * 64, 64))
    (dw_main, dw_dt), g_small = in_proj_wgrad(h1t, pieces, ddt_b, side=Gather([accs[k] for k in SMALL_ACCS[1:]]))
    cut = lambda k: dw_main[:, k * shard - (SSD_HEADS if k > kd else 0):(k + 1) * shard - (SSD_HEADS if k >= kd else 0)]
    blk_dt = jnp.concatenate([dw_main[:, kd * shard:DT_COL0], dw_dt[:, :SSD_HEADS],
                              dw_main[:, DT_COL0:(kd + 1) * shard - SSD_HEADS]], axis=1)
    dw_blocks = jnp.stack([blk_dt if k == kd else cut(k) for k in range(NDEV)])
    (grad_x, vacc_in, dmod_in), parts_in = in_proj_bwd(pieces, ddt_b, dx1, x2, mod8, pre_norm1, w_main, w_dt, nb, seq,
                                                       side=Scatter([dw_blocks]))
    parts = dict(zip(BIG, (parts_in[0], *parts_mg, *parts_ff)))

    dmod = (dmod_in + dmod_mg + dmod_mlp)[:, :N_MOD].reshape(nb, N_MOD * D_MODEL)
    g3, g_vin = all_gather([jnp.pad(dmod, ((0, 8 - nb), (0, 0))), vacc_in], "gather_dmod")
    dmod_all = g3[:, :nb].reshape(NDEV * nb, N_MOD * D_MODEL)
    dmod_cols = lax.dynamic_slice(dmod_all, (0, me * ada_cols), (NDEV * nb, ada_cols))
    g_w_ada, g_b_ada = ada_bwd(c_all, dmod_cols, dmod_all)

    res = {}
    for k in BIG:
        res[k] = adamw(parts[k], w[k], mom[k], var[k], "adamw_" + k)
    res['w_ada'] = adamw(g_w_ada[None], w_ada, m_w_ada, v_w_ada, "adamw_w_ada")

    gathered = dict(zip(SMALL_ACCS, (g_vin, *g_small)))
    view = lambda a: a.reshape(-1, a.shape[-1])
    res_a, g_lru_cw, g_ssd_cw = adamw_small(gathered, {k: (view(w[k]), view(mom[k]), view(var[k])) for k in REPL})
    res.update(res_a)
    lcw, scw = lru_conv_w.shape[2], ssd_conv_w.shape[2]
    sharded = {'b_ada': g_b_ada[None], 'lru_conv_w': lax.dynamic_slice(g_lru_cw, (0, me * lcw), (4, lcw))[None],
               'ssd_conv_w': lax.dynamic_slice(g_ssd_cw, (0, me * scw), (4, scw))[None]}
    for k, g in sharded.items():
        as3 = lambda a: a.reshape(g.shape)
        res[k] = adamw(g, as3(w[k]), as3(mom[k]), as3(var[k]), "adamw_" + k)

    loss = lax.psum(loss8[0, 0], ("x", "y", "c"))
    outs = [[res[k][j].reshape(w[k].shape) for k in WEIGHTS] for j in range(4)]
    return (loss, grad_x.reshape(x.shape), *outs[0], *outs[1], *outs[2], *outs[3])
```

```python
import functools

import numpy as np
import jax
import jax.numpy as jnp
from jax import lax
from jax.experimental import pallas as pl
from jax.experimental.pallas import tpu as pltpu

F32 = jnp.float32
BF16 = jnp.bfloat16

D_MODEL = 1024
LRU_HEADS = 16
LRU_BLOCK = 256
LRU_C = 8.0
SSD_INNER = 2048
SSD_HEADS = 32
SSD_P = 64
SSD_G = 8
SSD_N = 128
SSD_L = 128
SSD_GW = SSD_INNER // SSD_G
D_FF = 4096
N_MOD = 6
EPS = 1e-6
NDEV = 8

C_LRU_X, C_LRU_G, C_Z, C_XBC, C_GATES, PROJ_MAIN = 0, 1024, 2048, 4096, 8192, 10240
IN_DIM = 10272
DT_COL0 = 8192
HALO = 16
HT_TOK = 512

ADAM_LR, ADAM_B1, ADAM_B2, ADAM_EPS, ADAM_WD, ADAM_STEP = 0.001, 0.9, 0.999, 1e-08, 0.01, 10

VMEM_LIMIT = 60 * 1024 * 1024
MESH = pl.DeviceIdType.MESH
ANY = pl.BlockSpec(memory_space=pl.ANY)
VMEM_FULL = pl.BlockSpec(memory_space=pltpu.VMEM)


def _cp(*sem):
    return pltpu.CompilerParams(dimension_semantics=sem, vmem_limit_bytes=VMEM_LIMIT)


def _dot(a, b):
    return jnp.dot(a, b, preferred_element_type=F32)


def _dot_nt(a, b):
    return lax.dot_general(a, b, (((1,), (1,)), ((), ())), preferred_element_type=F32)


def _dot_tn(a, b):
    return lax.dot_general(a, b, (((0,), (0,)), ((), ())), preferred_element_type=F32)


def _dot_hi(a, b):
    return jnp.dot(a, b, precision=lax.Precision.HIGHEST, preferred_element_type=F32)


def _sigmoid(x):
    return 1.0 / (1.0 + jnp.exp(-x))


def _gelu_and_grad(x):
    k0, k1 = 0.7978845608028654, 0.044715
    t = jnp.tanh(k0 * (x + k1 * x * x * x))
    g = 0.5 * x * (1.0 + t)
    dg = 0.5 * (1.0 + t) + 0.5 * x * (1.0 - t * t) * k0 * (1.0 + 3.0 * k1 * x * x)
    return g, dg


def _neg_expm1(y):
    p = 1.0 + y * (1.0 / 7.0)
    p = 1.0 + y * (1.0 / 6.0) * p
    p = 1.0 + y * (1.0 / 5.0) * p
    p = 1.0 + y * (1.0 / 4.0) * p
    p = 1.0 + y * (1.0 / 3.0) * p
    p = 1.0 + y * 0.5 * p
    return jnp.where(y > -0.3, -y * p, 1.0 - jnp.exp(y))


def _colsum(v):
    return jnp.sum(v, axis=0, keepdims=True)


def _rowmean(v):
    return jnp.mean(v, axis=-1, keepdims=True)


def matmul(a, b, *, ta=False, tb=False, out_dtype=F32, tm, tn, tk, name, n=None, b_off=0, blocked_out=False):
    m = a.shape[1] if ta else a.shape[0]
    kdim = a.shape[0] if ta else a.shape[1]
    n = n or (b.shape[0] if tb else b.shape[1])
    tm, tn, tk = min(tm, m), min(tn, n), min(tk, kdim)
    nk = kdim // tk
    dn = (((0 if ta else 1,), (1 if tb else 0,)), ((), ()))

    def body(a_ref, b_ref, o_ref, acc_ref):
        k = pl.program_id(2)
        p = lax.dot_general(a_ref[...], b_ref[...], dn, preferred_element_type=F32)
        if nk == 1:
            o_ref[...] = p.astype(out_dtype)
        else:
            @pl.when(k == 0)
            def _():
                acc_ref[...] = p

            @pl.when(k > 0)
            def _():
                acc_ref[...] += p

            @pl.when(k == nk - 1)
            def _():
                o_ref[...] = acc_ref[...].astype(out_dtype)

    a_spec = pl.BlockSpec((tk, tm), lambda i, j, k: (k, i)) if ta else pl.BlockSpec((tm, tk), lambda i, j, k: (i, k))
    b_spec = (pl.BlockSpec((tn, tk), lambda i, j, k: (j, k)) if tb
              else pl.BlockSpec((tk, tn), lambda i, j, k: (k, j + b_off)))
    if blocked_out:
        o_spec, o_shape = pl.BlockSpec((None, tm, tn), lambda i, j, k: (j, i, 0)), (n // tn, m, tn)
    else:
        o_spec, o_shape = pl.BlockSpec((tm, tn), lambda i, j, k: (i, j)), (m, n)
    return pl.pallas_call(
        body, name=name, grid=(m // tm, n // tn, nk),
        in_specs=[a_spec, b_spec], out_specs=o_spec,
        out_shape=jax.ShapeDtypeStruct(o_shape, out_dtype),
        scratch_shapes=[pltpu.VMEM((tm, tn), F32)],
        compiler_params=_cp("parallel", "parallel", "arbitrary"),
    )(a, b)


def _conv_tile(j, tn):
    return jnp.where(j == 0, 0, jnp.clip(j - C_XBC // tn + 1, 1, 2 * SSD_INNER // tn))


def in_proj_fwd(x2, mod8, pre1, w_main, w_dt, cw, cb, seq, side=None):
    t = x2.shape[0]
    tm = min(1024, seq)
    tn = 1024
    per_seq = seq // tm
    j_xbc = C_XBC // tn
    n_xbc = 2 * SSD_INNER // tn
    cs = 256

    def body(x_ref, mod_ref, pre_ref, w_ref, wdt_ref, cw_ref, cb_ref, proj_ref, h_ref, dt_ref, xa_ref, xbc_ref,
             h_scr, carry_scr):
        i, j = pl.program_id(0), pl.program_id(1)

        @pl.when(j == 0)
        def _():
            xv = x_ref[...]
            y = xv * lax.rsqrt(_rowmean(xv * xv) + EPS) * pre_ref[...]
            m = mod_ref[0]
            hf = y * (1.0 + m[1:2, :]) + m[0:1, :]
            h = hf.astype(BF16)
            h_scr[...] = h
            hft = hf.T.astype(BF16)
            for q in range(tm // HT_TOK):
                h_ref[q] = hft[:, q * HT_TOK:(q + 1) * HT_TOK]
            dt_ref[...] = _dot(h, wdt_ref[...])

        def project(c0=0, width=tn):
            pb = _dot(h_scr[...], w_ref[:, c0:c0 + width]).astype(BF16)
            proj_ref[:, c0:c0 + width] = pb
            return pb

        def conv(o_ref, slot, act):
            first = lax.rem(i, per_seq) == 0
            for c0 in range(0, tn, cs):
                cur = project(c0, cs).astype(F32)
                prev = jnp.where(first, 0.0, carry_scr[slot, :, c0:c0 + cs])
                carry_scr[slot, :, c0:c0 + cs] = cur[tm - HALO:, :]
                xx = jnp.concatenate([prev, cur], axis=0)
                w = cw_ref[:, c0:c0 + cs]
                acc = cur * w[3:4, :] + cb_ref[:, c0:c0 + cs]
                for d in (1, 2, 3):
                    acc = acc + pltpu.roll(xx, d, axis=0)[HALO:, :] * w[3 - d:4 - d, :]
                if act:
                    acc = acc * _sigmoid(acc)
                o_ref[:, c0:c0 + cs] = acc.astype(BF16)

        is_xbc = (j >= j_xbc) & (j < j_xbc + n_xbc)

        @pl.when(j == 0)
        def _():
            conv(xa_ref, 0, False)

        @pl.when(is_xbc)
        def _():
            conv(xbc_ref, j - j_xbc + 1, True)

        @pl.when((j > 0) & jnp.logical_not(is_xbc))
        def _():
            project()

    return _call(
        body, name="in_proj_fwd", grid=(t // tm, PROJ_MAIN // tn), side=side, sem=("arbitrary", "arbitrary"),
        args=(x2, mod8, pre1, w_main, w_dt, cw, cb),
        in_specs=[pl.BlockSpec((tm, D_MODEL), lambda i, j: (i, 0)),
                  pl.BlockSpec((1, 8, D_MODEL), lambda i, j: (i // per_seq, 0, 0)),
                  pl.BlockSpec((1, D_MODEL), lambda i, j: (0, 0)),
                  pl.BlockSpec((D_MODEL, tn), lambda i, j: (0, j)),
                  pl.BlockSpec((D_MODEL, 128), lambda i, j: (0, 0)),
                  pl.BlockSpec((4, tn), lambda i, j: (0, _conv_tile(j, tn))),
                  pl.BlockSpec((1, tn), lambda i, j: (0, _conv_tile(j, tn)))],
        out_specs=[pl.BlockSpec((tm, tn), lambda i, j: (i, j)),
                   pl.BlockSpec((tm // HT_TOK, D_MODEL, HT_TOK), lambda i, j: (i, 0, 0)),
                   pl.BlockSpec((tm, 128), lambda i, j: (i, 0)),
                   pl.BlockSpec((tm, tn), lambda i, j: (i, 0)),
                   pl.BlockSpec((tm, tn), lambda i, j: (i, jnp.clip(j - j_xbc, 0, n_xbc - 1)))],
        out_shape=[jax.ShapeDtypeStruct((t, PROJ_MAIN), BF16), jax.ShapeDtypeStruct((t // HT_TOK, D_MODEL, HT_TOK), BF16),
                   jax.ShapeDtypeStruct((t, 128), F32), jax.ShapeDtypeStruct((t, D_MODEL), BF16),
                   jax.ShapeDtypeStruct((t, 2 * SSD_INNER), BF16)],
        scratch_shapes=[pltpu.VMEM((tm, D_MODEL), BF16), pltpu.VMEM((1 + n_xbc, HALO, tn), F32)])


def _lru_gates(xa, wa_ref, wx_ref, ba, bx, sp):
    nblk = D_MODEL // LRU_BLOCK
    pr = jnp.concatenate([_dot(xa[:, j * LRU_BLOCK:(j + 1) * LRU_BLOCK], wa_ref[j]) for j in range(nblk)], axis=1) + ba
    pi = jnp.concatenate([_dot(xa[:, j * LRU_BLOCK:(j + 1) * LRU_BLOCK], wx_ref[j]) for j in range(nblk)], axis=1) + bx
    r = _sigmoid(pr)
    i = _sigmoid(pi)
    log_a = (-LRU_C * r) * sp
    return r, i, jnp.exp(log_a), _neg_expm1(2.0 * log_a)


def lru_fwd(xa, proj, wa_bd, wx_bd, vec, nb, seq):
    t = xa.shape[0]
    tc = min(512, seq)
    nk = seq // tc
    gb = C_LRU_G // D_MODEL

    def body(xa_ref, g_ref, wa_ref, wx_ref, vec_ref, ya_ref, h_ref, a_scr, u_scr, hc_scr):
        @pl.when(pl.program_id(1) == 0)
        def _():
            hc_scr[...] = jnp.zeros_like(hc_scr)

        xa_v = xa_ref[...]
        v = vec_ref[...]
        r, i, a, e = _lru_gates(xa_v, wa_ref, wx_ref, v[0:1, :], v[1:2, :], v[2:3, :])
        a_scr[...] = a
        u_scr[...] = jnp.sqrt(e) * (i * xa_v.astype(F32))
        row = lax.broadcasted_iota(jnp.int32, (8, 1), 0)

        def tile(j, h):
            r0 = pl.multiple_of(j * 8, 8)
            av, uv = a_scr[pl.ds(r0, 8), :], u_scr[pl.ds(r0, 8), :]
            for d in (1, 2, 4):
                uv = uv + av * jnp.where(row >= d, pltpu.roll(uv, d, axis=0), 0.0)
                av = av * jnp.where(row >= d, pltpu.roll(av, d, axis=0), 1.0)
            hv = uv + av * h
            h_ref[pl.ds(r0, 8), :] = hv
            return hv[7:8, :]

        hc_scr[...] = lax.fori_loop(0, tc // 8, tile, hc_scr[...], unroll=2)
        gel, _ = _gelu_and_grad(g_ref[...].astype(F32))
        ya_ref[...] = (h_ref[...] * gel).astype(BF16)

    return pl.pallas_call(
        body, name="lru_fwd", grid=(nb, nk),
        in_specs=[pl.BlockSpec((tc, D_MODEL), lambda b, k: (b * nk + k, 0)),
                  pl.BlockSpec((tc, D_MODEL), lambda b, k: (b * nk + k, gb)),
                  VMEM_FULL, VMEM_FULL, VMEM_FULL],
        out_specs=[pl.BlockSpec((tc, D_MODEL), lambda b, k: (b * nk + k, 0)),
                   pl.BlockSpec((tc, D_MODEL), lambda b, k: (b * nk + k, 0))],
        out_shape=[jax.ShapeDtypeStruct((t, D_MODEL), BF16), jax.ShapeDtypeStruct((t, D_MODEL), F32)],
        scratch_shapes=[pltpu.VMEM((tc, D_MODEL), F32), pltpu.VMEM((tc, D_MODEL), F32), pltpu.VMEM((1, D_MODEL), F32)],
        compiler_params=_cp("arbitrary", "arbitrary"),
    )(xa, proj, wa_bd, wx_bd, vec)


def lru_bwd(dya, xa, proj, h, wa_bd, wx_bd, vec, nb, seq, side=None):
    t = xa.shape[0]
    tc = min(512, seq)
    nk = seq // tc
    gb = C_LRU_G // D_MODEL
    nblk = D_MODEL // LRU_BLOCK

    def chunk(b, k):
        return b * nk + (nk - 1 - k)

    def body(dya_ref, xa_ref, g_ref, h_ref, hp_ref, wa_ref, wx_ref, vec_ref,
             dxa_ref, dg_ref, dwa_ref, dwx_ref, dvec_ref, a_scr, dh_scr, c_scr):
        b, k = pl.program_id(0), pl.program_id(1)

        @pl.when((b == 0) & (k == 0))
        def _():
            dwa_ref[...] = jnp.zeros_like(dwa_ref)
            dwx_ref[...] = jnp.zeros_like(dwx_ref)
            dvec_ref[...] = jnp.zeros_like(dvec_ref)

        @pl.when(k == 0)
        def _():
            c_scr[...] = jnp.zeros_like(c_scr)

        xa_v = xa_ref[...]
        xaf = xa_v.astype(F32)
        v = vec_ref[...]
        sp = v[2:3, :]
        r, i, a, e = _lru_gates(xa_v, wa_ref, wx_ref, v[0:1, :], v[1:2, :], sp)
        gel, dgel = _gelu_and_grad(g_ref[...].astype(F32))
        hv = h_ref[...]
        dyv = dya_ref[...].astype(F32)
        dg_ref[...] = (dyv * hv * dgel).astype(BF16)
        a_scr[...] = a
        dh_scr[...] = dyv * gel

        row8 = lax.broadcasted_iota(jnp.int32, (8, 1), 0)

        def tile(j, c):
            r0 = pl.multiple_of((tc // 8 - 1 - j) * 8, 8)
            av, dout = a_scr[pl.ds(r0, 8), :], dh_scr[pl.ds(r0, 8), :]
            zv = av * dout
            for d in (1, 2, 4):
                zv = zv + av * jnp.where(row8 < 8 - d, pltpu.roll(zv, 8 - d, axis=0), 0.0)
                av = av * jnp.where(row8 < 8 - d, pltpu.roll(av, 8 - d, axis=0), 1.0)
            zv = zv + av * c
            dh_scr[pl.ds(r0, 8), :] = dout + jnp.where(row8 < 7, pltpu.roll(zv, 7, axis=0), c)
            return zv[0:1, :]

        c_scr[...] = lax.fori_loop(0, tc // 8, tile, c_scr[...], unroll=2)
        dh = dh_scr[...]
        h_last = jnp.where(k == nk - 1, 0.0, hp_ref[HALO // 2 - 1:HALO // 2, :])
        row = lax.broadcasted_iota(jnp.int32, (tc, 1), 0)
        h_prev = jnp.where(row == 0, h_last, pltpu.roll(hv, 1, axis=0))
        s = jnp.sqrt(e)
        da = dh * h_prev
        ix = i * xaf
        dlog_a = da * a - (dh * ix) * (a * a) * lax.rsqrt(jnp.maximum(e, 1e-30))
        di = dh * s * xaf
        dpr = (dlog_a * (-LRU_C * sp)) * (r * (1.0 - r))
        dpi = di * (i * (1.0 - i))
        dprb, dpib = dpr.astype(BF16), dpi.astype(BF16)
        dxa = dh * s * i
        dxa = dxa + jnp.concatenate(
            [_dot_nt(dprb[:, j * LRU_BLOCK:(j + 1) * LRU_BLOCK], wa_ref[j])
             + _dot_nt(dpib[:, j * LRU_BLOCK:(j + 1) * LRU_BLOCK], wx_ref[j]) for j in range(nblk)], axis=1)
        dxa_ref[...] = dxa.astype(BF16)
        for j in range(nblk):
            sl = slice(j * LRU_BLOCK, (j + 1) * LRU_BLOCK)
            dwa_ref[j] += _dot_tn(xa_v[:, sl], dprb[:, sl])
            dwx_ref[j] += _dot_tn(xa_v[:, sl], dpib[:, sl])
        dvec_ref[...] += jnp.concatenate(
            [_colsum(dpr), _colsum(dpi), _colsum(dlog_a * (-LRU_C * r)), jnp.zeros((5, D_MODEL), F32)], axis=0)

    hh = HALO // 2
    return _call(
        body, name="lru_bwd", grid=(nb, nk), side=side, sem=("arbitrary", "arbitrary"),
        args=(dya, xa, proj, h, h, wa_bd, wx_bd, vec),
        in_specs=[pl.BlockSpec((tc, D_MODEL), lambda b, k: (chunk(b, k), 0)),
                  pl.BlockSpec((tc, D_MODEL), lambda b, k: (chunk(b, k), 0)),
                  pl.BlockSpec((tc, D_MODEL), lambda b, k: (chunk(b, k), gb)),
                  pl.BlockSpec((tc, D_MODEL), lambda b, k: (chunk(b, k), 0)),
                  pl.BlockSpec((hh, D_MODEL), lambda b, k: (jnp.maximum(chunk(b, k) * (tc // hh) - 1, 0), 0)),
                  VMEM_FULL, VMEM_FULL, VMEM_FULL],
        out_specs=[pl.BlockSpec((tc, D_MODEL), lambda b, k: (chunk(b, k), 0)),
                   pl.BlockSpec((tc, D_MODEL), lambda b, k: (chunk(b, k), 0)),
                   pl.BlockSpec((nblk, LRU_BLOCK, LRU_BLOCK), lambda b, k: (0, 0, 0)),
                   pl.BlockSpec((nblk, LRU_BLOCK, LRU_BLOCK), lambda b, k: (0, 0, 0)),
                   pl.BlockSpec((8, D_MODEL), lambda b, k: (0, 0))],
        out_shape=[jax.ShapeDtypeStruct((t, D_MODEL), BF16), jax.ShapeDtypeStruct((t, D_MODEL), BF16),
                   jax.ShapeDtypeStruct((nblk, LRU_BLOCK, LRU_BLOCK), F32),
                   jax.ShapeDtypeStruct((nblk, LRU_BLOCK, LRU_BLOCK), F32),
                   jax.ShapeDtypeStruct((8, D_MODEL), F32)],
        scratch_shapes=[pltpu.VMEM((tc, D_MODEL), F32), pltpu.VMEM((tc, D_MODEL), F32), pltpu.VMEM((1, D_MODEL), F32)])


def merge_fwd(ya_in, yb_in, proj, x2, mod8, bgate, post1, w_pa, w_pb, w_out, seq):
    t = x2.shape[0]
    tm = min(512, seq)
    per_seq = seq // tm
    gcb = C_GATES // SSD_INNER

    def body(ya_ref, yb_ref, gt_ref, x_ref, mod_ref, bg_ref, post_ref, wpa_ref, wpb_ref, wo_ref,
             yab_ref, out1_ref, x1_ref):
        y_a = _dot(ya_ref[...], wpa_ref[...])
        y_b = _dot(yb_ref[...], wpb_ref[...])
        g = _sigmoid(gt_ref[...].astype(F32) + bg_ref[...])
        merged = g[:, :D_MODEL] * y_a + g[:, D_MODEL:] * y_b
        out1 = _dot(merged.astype(BF16), wo_ref[...])
        n = out1 * lax.rsqrt(_rowmean(out1 * out1) + EPS)
        yab_ref[...] = jnp.concatenate([y_a, y_b], axis=1).astype(BF16)
        out1_ref[...] = out1
        x1_ref[...] = x_ref[...] + mod_ref[0][2:3, :] * (n * post_ref[...])

    row = lambda w: pl.BlockSpec((tm, w), lambda i: (i, 0))
    return pl.pallas_call(
        body, name="merge_fwd", grid=(t // tm,),
        in_specs=[row(D_MODEL), row(SSD_INNER), pl.BlockSpec((tm, SSD_INNER), lambda i: (i, gcb)), row(D_MODEL),
                  pl.BlockSpec((1, 8, D_MODEL), lambda i: (i // per_seq, 0, 0)),
                  VMEM_FULL, VMEM_FULL, VMEM_FULL, VMEM_FULL, VMEM_FULL],
        out_specs=[row(SSD_INNER), row(D_MODEL), row(D_MODEL)],
        out_shape=[jax.ShapeDtypeStruct((t, SSD_INNER), BF16), jax.ShapeDtypeStruct((t, D_MODEL), F32),
                   jax.ShapeDtypeStruct((t, D_MODEL), F32)],
        compiler_params=_cp("parallel"),
    )(ya_in, yb_in, proj, x2, mod8, bgate, post1, w_pa, w_pb, w_out)


def merge_bwd(dx1, out1, yab, proj, mod8, bgate, post1, w_pa, w_pb, w_out, nb, seq):
    t = dx1.shape[0]
    tm = min(512, seq)
    per_seq = seq // tm
    gcb = C_GATES // SSD_INNER

    def body(dx1_ref, out1_ref, yab_ref, gt_ref, mod_ref, bg_ref, post_ref, wpa_ref, wpb_ref, wo_ref,
             dya_ref, dyb_ref, dgt_ref, dyab_ref, dout1_ref, mg_ref, vacc_ref, dmod_ref):
        b, s = pl.program_id(0), pl.program_id(1)

        @pl.when((b == 0) & (s == 0))
        def _():
            vacc_ref[...] = jnp.zeros_like(vacc_ref)

        @pl.when(s == 0)
        def _():
            dmod_ref[...] = jnp.zeros_like(dmod_ref)

        dx1v = dx1_ref[...]
        out1 = out1_ref[...]
        post = post_ref[...]
        rs = lax.rsqrt(_rowmean(out1 * out1) + EPS)
        n = out1 * rs
        do = dx1v * mod_ref[0][2:3, :]
        dn = do * post
        dout1 = rs * (dn - n * _rowmean(dn * n))
        dout1b = dout1.astype(BF16)
        dout1_ref[...] = dout1b
        dmerged = _dot_nt(dout1b, wo_ref[...])
        g = _sigmoid(gt_ref[...].astype(F32) + bg_ref[...])
        yab_v = yab_ref[...].astype(F32)
        gy = g * yab_v
        mg_ref[...] = (gy[:, :D_MODEL] + gy[:, D_MODEL:]).astype(BF16)
        dm2 = jnp.concatenate([dmerged, dmerged], axis=1)
        dyab = (dm2 * g).astype(BF16)
        dyab_ref[...] = dyab
        dgt = dm2 * gy * (1.0 - g)
        dgt_ref[...] = dgt.astype(BF16)
        dya_ref[...] = _dot_nt(dyab[:, :D_MODEL], wpa_ref[...]).astype(BF16)
        dyb_ref[...] = _dot_nt(dyab[:, D_MODEL:], wpb_ref[...]).astype(BF16)
        vacc_ref[...] += jnp.concatenate(
            [_colsum(dgt), jnp.concatenate([_colsum(do * n), jnp.zeros((1, D_MODEL), F32)], axis=1),
             jnp.zeros((6, SSD_INNER), F32)], axis=0)
        dmod_ref[0] += jnp.concatenate(
            [jnp.zeros((2, D_MODEL), F32), _colsum(dx1v * (n * post)), jnp.zeros((5, D_MODEL), F32)], axis=0)

    row = lambda w: pl.BlockSpec((tm, w), lambda b, s: (b * per_seq + s, 0))
    return pl.pallas_call(
        body, name="merge_bwd", grid=(nb, per_seq),
        in_specs=[row(D_MODEL), row(D_MODEL), row(SSD_INNER),
                  pl.BlockSpec((tm, SSD_INNER), lambda b, s: (b * per_seq + s, gcb)),
                  pl.BlockSpec((1, 8, D_MODEL), lambda b, s: (b, 0, 0)),
                  VMEM_FULL, VMEM_FULL, VMEM_FULL, VMEM_FULL, VMEM_FULL],
        out_specs=[row(D_MODEL), row(SSD_INNER), row(SSD_INNER), row(SSD_INNER), row(D_MODEL), row(D_MODEL),
                   pl.BlockSpec((8, SSD_INNER), lambda b, s: (0, 0)),
                   pl.BlockSpec((1, 8, D_MODEL), lambda b, s: (b, 0, 0))],
        out_shape=[jax.ShapeDtypeStruct((t, D_MODEL), BF16), jax.ShapeDtypeStruct((t, SSD_INNER), BF16),
                   jax.ShapeDtypeStruct((t, SSD_INNER), BF16), jax.ShapeDtypeStruct((t, SSD_INNER), BF16),
                   jax.ShapeDtypeStruct((t, D_MODEL), BF16), jax.ShapeDtypeStruct((t, D_MODEL), BF16),
                   jax.ShapeDtypeStruct((8, SSD_INNER), F32), jax.ShapeDtypeStruct((nb, 8, D_MODEL), F32)],
        compiler_params=_cp("arbitrary", "arbitrary"),
    )(dx1, out1, yab, proj, mod8, bgate, post1, w_pa, w_pb, w_out)


def mlp_fwd_bwd(x1, tgt, mod8, pre2, post2, w_ff1, w_ff2, nb, seq):
    t = x1.shape[0]
    tm = min(256, seq)
    per_seq = seq // tm
    fc = 1024
    nfc = D_FF // fc

    def body(x1_ref, tgt_ref, mod_ref, pre_ref, post_ref, w1_ref, w2_ref,
             dx1_ref, h2_ref, da1_ref, act_ref, dy2_ref, loss_ref, vacc_ref, dmod_ref, r_scr):
        b, s = pl.program_id(0), pl.program_id(1)
        per = fc // w1_ref.shape[2]

        def w1_cols(c):
            return jnp.concatenate([w1_ref[per * c + q] for q in range(per)], axis=1)

        @pl.when((b == 0) & (s == 0))
        def _():
            vacc_ref[...] = jnp.zeros_like(vacc_ref)
            loss_ref[...] = jnp.zeros_like(loss_ref)

        @pl.when(s == 0)
        def _():
            dmod_ref[...] = jnp.zeros_like(dmod_ref)

        m = mod_ref[0]
        sh2, sc2, g2 = m[3:4, :], m[4:5, :], m[5:6, :]
        pre, post = pre_ref[...], post_ref[...]
        x1v = x1_ref[...]
        rs1 = lax.rsqrt(_rowmean(x1v * x1v) + EPS)
        n1 = x1v * rs1
        y1 = n1 * pre
        h2b = (y1 * (1.0 + sc2) + sh2).astype(BF16)
        h2_ref[...] = h2b
        y2 = jnp.zeros((tm, D_MODEL), F32)
        for c in range(nfc):
            r = jnp.maximum(_dot(h2b, w1_cols(c)), 0.0)
            r_scr[:, c * fc:(c + 1) * fc] = r
            a = (r * r).astype(BF16)
            act_ref[:, c * fc:(c + 1) * fc] = a
            y2 = y2 + _dot(a, w2_ref[c * fc:(c + 1) * fc, :])
        rs2 = lax.rsqrt(_rowmean(y2 * y2) + EPS)
        n2 = y2 * rs2
        o2 = n2 * post
        diff = x1v + g2 * o2 - tgt_ref[...]
        loss_ref[...] += 0.5 * jnp.sum(_rowmean(diff * diff))
        dx2 = diff * (1.0 / D_MODEL)
        do2 = dx2 * g2
        dn2 = do2 * post
        dy2b = (rs2 * (dn2 - n2 * _rowmean(dn2 * n2))).astype(BF16)
        dy2_ref[...] = dy2b
        dh2 = jnp.zeros((tm, D_MODEL), F32)
        for c in range(nfc):
            dact = _dot_nt(dy2b, w2_ref[c * fc:(c + 1) * fc, :])
            da = (dact * (2.0 * r_scr[:, c * fc:(c + 1) * fc])).astype(BF16)
            da1_ref[:, c * fc:(c + 1) * fc] = da
            dh2 = dh2 + _dot_nt(da, w1_cols(c))
        dy1 = dh2 * (1.0 + sc2)
        dn1 = dy1 * pre
        dx1_ref[...] = dx2 + rs1 * (dn1 - n1 * _rowmean(dn1 * n1))
        vacc_ref[...] += jnp.concatenate([_colsum(dy1 * n1), _colsum(do2 * n2), jnp.zeros((6, D_MODEL), F32)], axis=0)
        dmod_ref[0] += jnp.concatenate(
            [jnp.zeros((3, D_MODEL), F32), _colsum(dh2), _colsum(dh2 * y1), _colsum(dx2 * o2),
             jnp.zeros((2, D_MODEL), F32)], axis=0)

    row = lambda w: pl.BlockSpec((tm, w), lambda b, s: (b * per_seq + s, 0))
    return pl.pallas_call(
        body, name="mlp_fwd_bwd", grid=(nb, per_seq),
        in_specs=[row(D_MODEL), row(D_MODEL), pl.BlockSpec((1, 8, D_MODEL), lambda b, s: (b, 0, 0)),
                  VMEM_FULL, VMEM_FULL, VMEM_FULL, VMEM_FULL],
        out_specs=[row(D_MODEL), row(D_MODEL), row(D_FF), row(D_FF), row(D_MODEL),
                   pl.BlockSpec((8, 128), lambda b, s: (0, 0)),
                   pl.BlockSpec((8, D_MODEL), lambda b, s: (0, 0)),
                   pl.BlockSpec((1, 8, D_MODEL), lambda b, s: (b, 0, 0))],
        out_shape=[jax.ShapeDtypeStruct((t, D_MODEL), F32), jax.ShapeDtypeStruct((t, D_MODEL), BF16),
                   jax.ShapeDtypeStruct((t, D_FF), BF16), jax.ShapeDtypeStruct((t, D_FF), BF16),
                   jax.ShapeDtypeStruct((t, D_MODEL), BF16), jax.ShapeDtypeStruct((8, 128), F32),
                   jax.ShapeDtypeStruct((8, D_MODEL), F32), jax.ShapeDtypeStruct((nb, 8, D_MODEL), F32)],
        scratch_shapes=[pltpu.VMEM((tm, D_FF), F32)],
        compiler_params=_cp("arbitrary", "arbitrary"),
    )(x1, tgt, mod8, pre2, post2, w_ff1, w_ff2)


_PIECES = ((C_LRU_X, 1024), (C_LRU_G, 1024), (C_Z, 2048), (C_XBC, 4096), (C_GATES, 2048))
_NP = len(_PIECES)


def _piece_of(k, tk):
    col = k * tk
    for p, (c0, w) in enumerate(_PIECES):
        if c0 <= col < c0 + w:
            return p, (col - c0) // tk
    raise ValueError(col)


def in_proj_bwd(pieces, ddt, dx1, x2, mod8, pre1, w_main, w_dt, nb, seq, side=None):
    t = x2.shape[0]
    tm = min(512, seq)
    per_seq = seq // tm
    tk = 1024
    nk = PROJ_MAIN // tk
    where = [_piece_of(k, tk) for k in range(nk)]

    def piece_spec(p):
        first = min(k for k in range(nk) if where[k][0] == p)
        nblk = _PIECES[p][1] // tk
        return pl.BlockSpec((tm, tk), lambda b, s, k: (b * per_seq + s, jnp.clip(k - first, 0, nblk - 1)))

    def body(*refs):
        prefs = refs[:_NP]
        ddt_ref, dx1_ref, x_ref, mod_ref, pre_ref, w_ref, wdt_ref, gx_ref, vacc_ref, dmod_ref, acc_ref = refs[_NP:]
        b, s, k = pl.program_id(0), pl.program_id(1), pl.program_id(2)

        @pl.when((b == 0) & (s == 0) & (k == 0))
        def _():
            vacc_ref[...] = jnp.zeros_like(vacc_ref)

        @pl.when((s == 0) & (k == 0))
        def _():
            dmod_ref[...] = jnp.zeros_like(dmod_ref)

        @pl.when(k == 0)
        def _():
            acc_ref[...] = _dot_nt(ddt_ref[...], wdt_ref[...])

        for kk in range(nk):
            @pl.when(k == kk)
            def _(kk=kk):
                acc_ref[...] += _dot_nt(prefs[where[kk][0]][...], w_ref[:, kk * tk:(kk + 1) * tk])

        @pl.when(k == nk - 1)
        def _():
            dh = acc_ref[...]
            m = mod_ref[0]
            pre = pre_ref[...]
            xv = x_ref[...]
            rs = lax.rsqrt(_rowmean(xv * xv) + EPS)
            n = xv * rs
            dy = dh * (1.0 + m[1:2, :])
            dn = dy * pre
            gx_ref[...] = dx1_ref[...] + rs * (dn - n * _rowmean(dn * n))
            vacc_ref[...] += jnp.concatenate([_colsum(dy * n), jnp.zeros((7, D_MODEL), F32)], axis=0)
            dmod_ref[0] += jnp.concatenate([_colsum(dh), _colsum(dh * (n * pre)), jnp.zeros((6, D_MODEL), F32)], axis=0)

    row = lambda w: pl.BlockSpec((tm, w), lambda b, s, k: (b * per_seq + s, 0))
    return _call(
        body, name="in_proj_bwd", grid=(nb, per_seq, nk), side=side, sem=("arbitrary", "arbitrary", "arbitrary"),
        args=(*pieces, ddt, dx1, x2, mod8, pre1, w_main, w_dt),
        in_specs=[piece_spec(p) for p in range(_NP)] + [
            row(128), row(D_MODEL), row(D_MODEL), pl.BlockSpec((1, 8, D_MODEL), lambda b, s, k: (b, 0, 0)),
            pl.BlockSpec((1, D_MODEL), lambda b, s, k: (0, 0)),
            VMEM_FULL,
            pl.BlockSpec((D_MODEL, 128), lambda b, s, k: (0, 0))],
        out_specs=[row(D_MODEL), pl.BlockSpec((8, D_MODEL), lambda b, s, k: (0, 0)),
                   pl.BlockSpec((1, 8, D_MODEL), lambda b, s, k: (b, 0, 0))],
        out_shape=[jax.ShapeDtypeStruct((t, D_MODEL), F32), jax.ShapeDtypeStruct((8, D_MODEL), F32),
                   jax.ShapeDtypeStruct((nb, 8, D_MODEL), F32)],
        scratch_shapes=[pltpu.VMEM((tm, D_MODEL), F32)])


def in_proj_wgrad(h1t, proj, dxa, dxbc, dlg, dz, dgates, ddt, cw, cb, seq, side=None):
    nt, _, tt = h1t.shape
    t = nt * tt
    tn = 1024
    nn = PROJ_MAIN // tn
    ns = seq // tt
    nh = t // HALO
    j_g, j_z, j_x, j_gt = C_LRU_G // tn, C_Z // tn, C_XBC // tn, C_GATES // tn
    n_x = 2 * SSD_INNER // tn
    strip = 256
    ne = tt + HALO

    def body(h_ref, cur_ref, prev_ref, next_ref, dxa_ref, dxan_ref, dxb_ref, dxbn_ref, dlg_ref, dz_ref, dgt_ref, ddt_ref,
             cw_ref, cb_ref, dw_ref, dwdt_ref, dlx_ref, dxr_ref, accl_ref, accs_ref, acc_ref, accdt_ref):
        n, k = pl.program_id(0), pl.program_id(1)
        hv = h_ref[k]
        is_x = (n >= j_x) & (n < j_x + n_x)

        @pl.when(k == 0)
        def _():
            acc_ref[...] = jnp.zeros_like(acc_ref)

        @pl.when((n == 0) & (k == 0))
        def _():
            accdt_ref[...] = jnp.zeros_like(accdt_ref)
            accl_ref[...] = jnp.zeros_like(accl_ref)

        @pl.when(is_x & (k == 0))
        def _():
            accs_ref[...] = jnp.zeros_like(accs_ref)

        def conv_tile(do_ref, don_ref, out_ref, cacc_ref, act):
            first = lax.rem(k, ns) == 0
            last = lax.rem(k, ns) == ns - 1
            for c0 in range(0, tn, strip):
                cs = slice(c0, c0 + strip)
                xx = jnp.concatenate([jnp.where(first, 0.0, prev_ref[:, cs].astype(F32)), cur_ref[:, cs].astype(F32),
                                      next_ref[:, cs].astype(F32)], axis=0)
                do_ext = jnp.concatenate([do_ref[:, cs].astype(F32),
                                          jnp.where(last, 0.0, don_ref[:, cs].astype(F32))], axis=0)
                w = cw_ref[:, cs]
                xs = [xx[HALO:HALO + ne, :]] + [pltpu.roll(xx, d, axis=0)[HALO:HALO + ne, :] for d in (1, 2, 3)]
                if act:
                    c = cb_ref[:, cs] + xs[0] * w[3:4, :] + xs[1] * w[2:3, :] + xs[2] * w[1:2, :] + xs[3] * w[0:1, :]
                    sg = _sigmoid(c)
                    dc = do_ext * (sg * (1.0 + c * (1.0 - sg)))
                else:
                    dc = do_ext
                dx = dc[:tt, :] * w[3:4, :]
                for d in (1, 2, 3):
                    dx = dx + pltpu.roll(dc, ne - d, axis=0)[:tt, :] * w[3 - d:4 - d, :]
                dxb = dx.astype(BF16)
                out_ref[:, cs] = dxb
                acc_ref[:, cs] += _dot(hv, dxb)
                dcc = dc[:tt, :]
                rows = [_colsum(dcc * xs[3 - r][:tt, :]) for r in range(4)] + [_colsum(dcc)]
                cacc_ref[:, cs] += jnp.concatenate(rows + [jnp.zeros((3, strip), F32)], axis=0)

        @pl.when(n == 0)
        def _():
            conv_tile(dxa_ref, dxan_ref, dlx_ref, accl_ref, False)
            accdt_ref[...] += _dot(hv, ddt_ref[...])

        @pl.when(is_x)
        def _():
            conv_tile(dxb_ref, dxbn_ref, dxr_ref, accs_ref, True)

        @pl.when(n == j_g)
        def _():
            acc_ref[...] += _dot(hv, dlg_ref[...])

        @pl.when((n >= j_z) & (n < j_x))
        def _():
            acc_ref[...] += _dot(hv, dz_ref[...])

        @pl.when(n >= j_gt)
        def _():
            acc_ref[...] += _dot(hv, dgt_ref[...])

        @pl.when(k == nt - 1)
        def _():
            dw_ref[...] = acc_ref[...].astype(BF16)

        @pl.when((n == 0) & (k == nt - 1))
        def _():
            dwdt_ref[...] = accdt_ref[...].astype(BF16)

    conv_n = lambda n: (n == 0) | ((n >= j_x) & (n < j_x + n_x))
    src_col = lambda n: jnp.where(n == 0, 0, jnp.clip(n, j_x, j_x + n_x - 1))
    ctile = lambda n: jnp.where(n == 0, 0, jnp.clip(n - j_x + 1, 1, n_x))
    xcol = lambda n: jnp.clip(n - j_x, 0, n_x - 1)
    on = lambda cond, k: jnp.where(cond, k, 0)
    nxt = lambda k: jnp.minimum(((k + 1) * tt) // HALO, nh - 1)
    after = lambda cond_during, cond_after, k: jnp.where(cond_during, k, jnp.where(cond_after, nt - 1, 0))
    in_specs = [
        VMEM_FULL,
        pl.BlockSpec((tt, tn), lambda n, k: (on(conv_n(n), k), src_col(n))),
        pl.BlockSpec((HALO, tn), lambda n, k: (on(conv_n(n), jnp.maximum((k * tt) // HALO - 1, 0)), src_col(n))),
        pl.BlockSpec((HALO, tn), lambda n, k: (on(conv_n(n), nxt(k)), src_col(n))),
        pl.BlockSpec((tt, tn), lambda n, k: (on(n == 0, k), 0)),
        pl.BlockSpec((HALO, tn), lambda n, k: (on(n == 0, nxt(k)), 0)),
        pl.BlockSpec((tt, tn), lambda n, k: (on((n >= j_x) & (n < j_x + n_x), k), xcol(n))),
        pl.BlockSpec((HALO, tn), lambda n, k: (on((n >= j_x) & (n < j_x + n_x), nxt(k)), xcol(n))),
        pl.BlockSpec((tt, tn), lambda n, k: (on(n == j_g, k), 0)),
        pl.BlockSpec((tt, tn), lambda n, k: (on((n >= j_z) & (n < j_x), k), jnp.clip(n - j_z, 0, j_x - j_z - 1))),
        pl.BlockSpec((tt, tn), lambda n, k: (on(n >= j_gt, k), jnp.clip(n - j_gt, 0, nn - j_gt - 1))),
        pl.BlockSpec((tt, 128), lambda n, k: (on(n == 0, k), 0)),
        pl.BlockSpec((4, tn), lambda n, k: (0, ctile(n))),
        pl.BlockSpec((1, tn), lambda n, k: (0, ctile(n)))]
    out_specs = [
        pl.BlockSpec((D_MODEL, tn), lambda n, k: (0, n)),
        pl.BlockSpec((D_MODEL, 128), lambda n, k: (0, 0)),
        pl.BlockSpec((tt, tn), lambda n, k: (after(n == 0, n > 0, k), 0)),
        pl.BlockSpec((tt, tn), lambda n, k: (after((n >= j_x) & (n < j_x + n_x), n >= j_x + n_x, k), xcol(n))),
        pl.BlockSpec((8, tn), lambda n, k: (0, 0)),
        pl.BlockSpec((8, tn), lambda n, k: (0, xcol(n)))]
    return _call(
        body, name="in_proj_wgrad", grid=(nn, nt), side=side, sem=("arbitrary", "arbitrary"),
        args=(h1t, proj, proj, proj, dxa, dxa, dxbc, dxbc, dlg, dz, dgates, ddt, cw, cb),
        in_specs=in_specs, out_specs=out_specs,
        out_shape=[jax.ShapeDtypeStruct((D_MODEL, PROJ_MAIN), BF16), jax.ShapeDtypeStruct((D_MODEL, 128), BF16),
                   jax.ShapeDtypeStruct((t, D_MODEL), BF16), jax.ShapeDtypeStruct((t, 2 * SSD_INNER), BF16),
                   jax.ShapeDtypeStruct((8, D_MODEL), F32), jax.ShapeDtypeStruct((8, 2 * SSD_INNER), F32)],
        scratch_shapes=[pltpu.VMEM((D_MODEL, tn), F32), pltpu.VMEM((D_MODEL, 128), F32)])


def _log1p(u):
    w = 1.0 + u
    return jnp.log(w) - ((w - 1.0) - u) / w


def _softplus(x):
    return jnp.maximum(x, 0.0) + _log1p(jnp.exp(-jnp.abs(x)))


def _head_mask(h):
    lane = lax.broadcasted_iota(jnp.int32, (1, SSD_GW), 1)
    return (lane >= SSD_P * h) & (lane < SSD_P * (h + 1))


def _pair(p):
    return slice(2 * SSD_P * p, 2 * SSD_P * (p + 1))


def _expand4(m, g):
    lane = lax.broadcasted_iota(jnp.int32, (1, SSD_GW), 1)
    col = lambda h: m[:, 4 * g + h:4 * g + h + 1]
    return jnp.where(lane < SSD_P, col(0), jnp.where(lane < 2 * SSD_P, col(1), jnp.where(lane < 3 * SSD_P, col(2), col(3))))


def _reduce4(v, g):
    lane = lax.broadcasted_iota(jnp.int32, (1, SSD_N), 1)
    out = jnp.zeros((v.shape[0], SSD_N), F32)
    for h in range(4):
        s = jnp.sum(jnp.where(_head_mask(h), v, 0.0), axis=1, keepdims=True)
        out = out + jnp.where(lane == 4 * g + h, s, 0.0)
    return out


def _ssd_heads(dtraw, hp, tri):
    xdt = dtraw + hp[0:1, :]
    dt = _softplus(xdt)
    cs = _dot_hi(tri, dt * hp[1:2, :])
    cs_last = cs[SSD_L - 1:SSD_L, :]
    return dict(xdt=xdt, dt=dt, cs=cs, cs_t=cs.T, e=jnp.exp(cs), w=jnp.exp(cs_last - cs), el=jnp.exp(cs_last))


def _ssd_group(g, hd, xs_b, bm_b, cm_b, d_x, st, paired=False):
    ll = SSD_L
    xs = xs_b.astype(F32)
    cs, cs_t = hd["cs"], hd["cs_t"]
    e_x, w_x, el_x, dt_x = _expand4(hd["e"], g), _expand4(hd["w"], g), _expand4(hd["el"], g), _expand4(hd["dt"], g)
    xd = xs * dt_x
    gcb = _dot_nt(cm_b, bm_b)
    ri = lax.broadcasted_iota(jnp.int32, (ll, ll), 0)
    ci = lax.broadcasted_iota(jnp.int32, (ll, ll), 1)
    dks, ms = [], []
    for h in range(4):
        k = 4 * g + h
        dk = jnp.exp(jnp.where(ri >= ci, cs[:, k:k + 1] - cs_t[k:k + 1, :], -1e30))
        dks.append(dk)
        ms.append((gcb * dk).astype(BF16))
    xdb = xd.astype(BF16)
    if paired:
        first = lax.broadcasted_iota(jnp.int32, (1, 2 * SSD_P), 1) < SSD_P
        ydiag = jnp.concatenate(
            [jnp.where(first, _dot(ms[2 * p], xdb[:, _pair(p)]), _dot(ms[2 * p + 1], xdb[:, _pair(p)]))
             for p in range(2)], axis=1)
    else:
        ydiag = jnp.zeros((ll, SSD_GW), F32)
        for h in range(4):
            ydiag = ydiag + _dot(ms[h], jnp.where(_head_mask(h), xd, 0.0).astype(BF16))
    yoff = _dot(cm_b, st.astype(BF16)) * e_x
    y = ydiag + yoff + d_x * xs
    st_new = st * el_x + _dot(bm_b.astype(F32).T.astype(BF16), (xd * w_x).astype(BF16))
    return dict(xs=xs, e_x=e_x, w_x=w_x, el_x=el_x, dt_x=dt_x, xd=xd, xdb=xdb, gcb=gcb, dks=dks, ms=ms, yoff=yoff, y=y,
                st_new=st_new)


def ssd_consts():
    hh = np.arange(SSD_N)
    tri = (hh[:, None] >= hh[None, :]).astype(np.float32)
    return jnp.asarray(tri), jnp.asarray(tri.T)


def ssd_params(dt_bias, a_log, d_skip, norm_w):
    padh = lambda v: jnp.pad(v.reshape(1, SSD_HEADS), ((0, 0), (0, SSD_N - SSD_HEADS)))
    hp = jnp.concatenate([padh(dt_bias), padh(-jnp.exp(a_log)), jnp.zeros((6, SSD_N), F32)], axis=0)
    lp = jnp.concatenate([norm_w.reshape(1, SSD_INNER), jnp.repeat(d_skip, SSD_P).reshape(1, SSD_INNER),
                          jnp.zeros((6, SSD_INNER), F32)], axis=0)
    return hp, lp


def _b_cols(g):
    return slice(SSD_INNER + g * SSD_N, SSD_INNER + (g + 1) * SSD_N)


def _c_cols(g):
    return slice(SSD_INNER + (SSD_G + g) * SSD_N, SSD_INNER + (SSD_G + g + 1) * SSD_N)


def _ssd_specs(nc, rc):
    return [pl.BlockSpec((SSD_L, 2 * SSD_INNER), lambda b, c: (b * nc + rc(c), 0)),
            pl.BlockSpec((SSD_L, SSD_INNER), lambda b, c: (b * nc + rc(c), C_Z // SSD_INNER)),
            pl.BlockSpec((SSD_L, SSD_N), lambda b, c: (b * nc + rc(c), 0))]


def ssd_fwd(xbc, proj, dtraw, hp, lp, tri, nb, seq):
    t = xbc.shape[0]
    nc = seq // SSD_L

    def body(xbc_ref, z_ref, dt_ref, hp_ref, lp_ref, tri_ref, y_ref, sts_ref, st_scr):
        @pl.when(pl.program_id(1) == 0)
        def _():
            st_scr[...] = jnp.zeros_like(st_scr)

        hd = _ssd_heads(dt_ref[...], hp_ref[...], tri_ref[...])
        for g in range(SSD_G):
            gs = slice(g * SSD_GW, (g + 1) * SSD_GW)
            st = st_scr[g]
            sts_ref[0, g] = st
            f = _ssd_group(g, hd, xbc_ref[:, gs], xbc_ref[:, _b_cols(g)], xbc_ref[:, _c_cols(g)], lp_ref[1:2, gs], st,
                           paired=True)
            st_scr[g] = f["st_new"]
            zf = z_ref[:, gs].astype(F32)
            yg = f["y"] * (zf * _sigmoid(zf))
            y_ref[:, gs] = (yg * lax.rsqrt(_rowmean(yg * yg) + EPS) * lp_ref[0:1, gs]).astype(BF16)

    return pl.pallas_call(
        body, name="ssd_fwd", grid=(nb, nc),
        in_specs=_ssd_specs(nc, lambda c: c) + [VMEM_FULL, VMEM_FULL, VMEM_FULL],
        out_specs=[pl.BlockSpec((SSD_L, SSD_INNER), lambda b, c: (b * nc + c, 0)),
                   pl.BlockSpec((1, SSD_G, SSD_N, SSD_GW), lambda b, c: (b * nc + c, 0, 0, 0))],
        out_shape=[jax.ShapeDtypeStruct((t, SSD_INNER), BF16),
                   jax.ShapeDtypeStruct((nb * nc, SSD_G, SSD_N, SSD_GW), F32)],
        scratch_shapes=[pltpu.VMEM((SSD_G, SSD_N, SSD_GW), F32)],
        compiler_params=_cp("arbitrary", "arbitrary"),
    )(xbc, proj, dtraw, hp, lp, tri)


def ssd_bwd(xbc, proj, dtraw, hp, lp, tri, triu, states, dyn, nb, seq, side=None):
    t = xbc.shape[0]
    nc = seq // SSD_L
    ll = SSD_L

    def body(xbc_ref, z_ref, dt_ref, sts_ref, dy_ref, hp_ref, lp_ref, tri_ref, triu_ref,
             dxbc_ref, dz_ref, ddt_ref, hpg_ref, lpg_ref, dst_scr):
        b, c_i = pl.program_id(0), pl.program_id(1)

        @pl.when((b == 0) & (c_i == 0))
        def _():
            hpg_ref[...] = jnp.zeros_like(hpg_ref)
            lpg_ref[...] = jnp.zeros_like(lpg_ref)

        @pl.when(c_i == 0)
        def _():
            dst_scr[...] = jnp.zeros_like(dst_scr)

        hp = hp_ref[...]
        hd = _ssd_heads(dt_ref[...], hp, tri_ref[...])
        lane = lax.broadcasted_iota(jnp.int32, (1, SSD_N), 1)
        subl = lax.broadcasted_iota(jnp.int32, (SSD_N, 1), 0)
        dcs = jnp.zeros((ll, SSD_N), F32)
        dcs_t = jnp.zeros((SSD_N, ll), F32)
        last = jnp.zeros((1, SSD_N), F32)
        dxx = jnp.zeros((ll, SSD_N), F32)
        for g in range(SSD_G):
            gs = slice(g * SSD_GW, (g + 1) * SSD_GW)
            st = sts_ref[0, g]
            dst = dst_scr[g]
            bm_b, cm_b = xbc_ref[:, _b_cols(g)], xbc_ref[:, _c_cols(g)]
            d_x = lp_ref[1:2, gs]
            f = _ssd_group(g, hd, xbc_ref[:, gs], bm_b, cm_b, d_x, st)
            xs, xd, gcb = f["xs"], f["xd"], f["gcb"]
            e_x, w_x, el_x, dt_x = f["e_x"], f["w_x"], f["el_x"], f["dt_x"]
            stb, dstb = st.astype(BF16), dst.astype(BF16)
            zf = z_ref[:, gs].astype(F32)
            sg = _sigmoid(zf)
            sz = zf * sg
            yg = f["y"] * sz
            rstd = lax.rsqrt(_rowmean(yg * yg) + EPS)
            n = yg * rstd
            dyn_v = dy_ref[:, gs].astype(F32)
            dn = dyn_v * lp_ref[0:1, gs]
            dyg = rstd * (dn - n * _rowmean(dn * n))
            dy = dyg * sz
            dz_ref[:, gs] = (dyg * f["y"] * (sg * (1.0 + zf * (1.0 - sg)))).astype(BF16)
            dyb = dy.astype(BF16)
            r_ = _dot(bm_b, dstb)
            dxd = w_x * r_
            dqb = (dy * e_x).astype(BF16)
            dcm = _dot_nt(dqb, stb)
            dst_scr[g] = dst * el_x + _dot_tn(cm_b, dqb)
            dbm = _dot_nt((xd * w_x).astype(BF16), dstb)
            xdb = f["xdb"]
            dgm = jnp.zeros((ll, ll), F32)
            for h in range(4):
                k = 4 * g + h
                hm = _head_mask(h)
                dxd = dxd + jnp.where(hm, _dot_tn(f["ms"][h], dyb), 0.0)
                dm = _dot_nt(jnp.where(hm, dy, 0.0).astype(BF16), xdb) * f["dks"][h]
                dgm = dgm + dm
                dseg = dm * gcb
                dcs = dcs + jnp.where(lane == k, jnp.sum(dseg, axis=1, keepdims=True), 0.0)
                dcs_t = dcs_t + jnp.where(subl == k, jnp.sum(dseg, axis=0, keepdims=True), 0.0)
            dgmb = dgm.astype(BF16)
            dxbc_ref[:, _c_cols(g)] = (dcm + _dot(dgmb, bm_b)).astype(BF16)
            dxbc_ref[:, _b_cols(g)] = (dbm + _dot_tn(dgmb, cm_b)).astype(BF16)
            v = _reduce4(r_ * xd * w_x, g)
            dcs = dcs + _reduce4(dy * f["yoff"], g) - v
            last = last + _colsum(v) + _reduce4(_colsum(dst * st) * el_x, g)
            dxx = dxx + _reduce4(dxd * xs, g)
            dxbc_ref[:, gs] = (d_x * dy + dxd * dt_x).astype(BF16)
            lpg_ref[0:1, gs] += _colsum(dyn_v * n)
            lpg_ref[1:2, gs] += _colsum(dy * xs)
        rowi = lax.broadcasted_iota(jnp.int32, (ll, 1), 0)
        da = _dot_hi(triu_ref[...], dcs - dcs_t.T + jnp.where(rowi == ll - 1, last, 0.0))
        ddt = (dxx + da * hp[1:2, :]) * _sigmoid(hd["xdt"])
        ddt_ref[...] = ddt
        hpg_ref[...] += jnp.concatenate([_colsum(ddt), _colsum(da * hd["dt"]), jnp.zeros((6, SSD_N), F32)], axis=0)

    rc = lambda c: nc - 1 - c
    return _call(
        body, name="ssd_bwd", grid=(nb, nc), side=side, sem=("arbitrary", "arbitrary"),
        args=(xbc, proj, dtraw, states, dyn, hp, lp, tri, triu),
        in_specs=_ssd_specs(nc, rc) + [
            pl.BlockSpec((1, SSD_G, SSD_N, SSD_GW), lambda b, c: (b * nc + rc(c), 0, 0, 0)),
            pl.BlockSpec((SSD_L, SSD_INNER), lambda b, c: (b * nc + rc(c), 0)),
            VMEM_FULL, VMEM_FULL, VMEM_FULL, VMEM_FULL],
        out_specs=[pl.BlockSpec((SSD_L, 2 * SSD_INNER), lambda b, c: (b * nc + rc(c), 0)),
                   pl.BlockSpec((SSD_L, SSD_INNER), lambda b, c: (b * nc + rc(c), 0)),
                   pl.BlockSpec((SSD_L, SSD_N), lambda b, c: (b * nc + rc(c), 0)),
                   pl.BlockSpec((8, SSD_N), lambda b, c: (0, 0)),
                   pl.BlockSpec((8, SSD_INNER), lambda b, c: (0, 0))],
        out_shape=[jax.ShapeDtypeStruct((t, 2 * SSD_INNER), BF16), jax.ShapeDtypeStruct((t, SSD_INNER), BF16),
                   jax.ShapeDtypeStruct((t, SSD_N), F32), jax.ShapeDtypeStruct((8, SSD_N), F32),
                   jax.ShapeDtypeStruct((8, SSD_INNER), F32)],
        scratch_shapes=[pltpu.VMEM((SSD_G, SSD_N, SSD_GW), F32)])


def ada_fwd(c_all, w_cols, b_cols):
    def body(c_ref, w_ref, b_ref, o_ref):
        cv = c_ref[...]
        o_ref[...] = _dot_hi(cv * _sigmoid(cv), w_ref[...]) + b_ref[...]

    return pl.pallas_call(body, name="ada_fwd", out_shape=jax.ShapeDtypeStruct((c_all.shape[0], w_cols.shape[1]), F32),
                          compiler_params=pltpu.CompilerParams(vmem_limit_bytes=VMEM_LIMIT))(c_all, w_cols, b_cols)


def ada_bwd(c_all, dmod_cols, dmod_all):
    def body(c_ref, dc_ref, da_ref, gw_ref, gb_ref):
        cv = c_ref[...]
        gw_ref[...] = lax.dot_general(cv * _sigmoid(cv), dc_ref[...], (((0,), (0,)), ((), ())),
                                      precision=lax.Precision.HIGHEST, preferred_element_type=F32)
        gb_ref[...] = _colsum(da_ref[...])

    return pl.pallas_call(
        body, name="ada_bwd",
        out_shape=[jax.ShapeDtypeStruct((c_all.shape[1], dmod_cols.shape[1]), F32),
                   jax.ShapeDtypeStruct((1, dmod_all.shape[1]), F32)],
        compiler_params=pltpu.CompilerParams(vmem_limit_bytes=VMEM_LIMIT))(c_all, dmod_cols, dmod_all)


def _adam_update(g, w, m, v):
    m2 = ADAM_B1 * m + (1.0 - ADAM_B1) * g
    v2 = ADAM_B2 * v + (1.0 - ADAM_B2) * (g * g)
    m_hat = m2 / (1.0 - ADAM_B1 ** ADAM_STEP)
    v_hat = v2 / (1.0 - ADAM_B2 ** ADAM_STEP)
    return -ADAM_LR * (m_hat / (jnp.sqrt(v_hat) + ADAM_EPS) + ADAM_WD * w), m2, v2


def adamw(parts, w, m, v, name):
    n, r, c = parts.shape
    tr = r if r <= 256 else 128

    def body(p_ref, w_ref, m_ref, v_ref, g_ref, d_ref, nm_ref, nv_ref):
        g = p_ref[0].astype(F32)
        for s in range(1, n):
            g = g + p_ref[s].astype(F32)
        g_ref[0] = g
        d_ref[0], nm_ref[0], nv_ref[0] = _adam_update(g, w_ref[0], m_ref[0], v_ref[0])

    blk = pl.BlockSpec((1, tr, c), lambda i: (0, i, 0))
    return pl.pallas_call(
        body, name=name, grid=(r // tr,),
        in_specs=[pl.BlockSpec((n, tr, c), lambda i: (0, i, 0)), blk, blk, blk], out_specs=[blk] * 4,
        out_shape=[jax.ShapeDtypeStruct((1, r, c), F32)] * 4,
        compiler_params=_cp("parallel"),
    )(parts, w, m, v)


SMALL_SRC = {
    'pre_norm1': ('vin', 0, 1024), 'post_norm1': ('vmg', 1, 1024), 'b_gate': ('vmg', 0, 2048),
    'lru_conv_b': ('accl', 4, 1024), 'lru_wa': ('gwa', None, None), 'lru_ba': ('dvec', 0, 1024),
    'lru_wx': ('gwx', None, None), 'lru_bx': ('dvec', 1, 1024), 'lru_lambda': ('dvec', 2, 1024),
    'ssd_conv_b': ('accs', 4, 4096), 'ssd_dt_bias': ('hpg', 0, SSD_HEADS), 'ssd_a_log': ('hpg', 1, SSD_HEADS),
    'ssd_d': ('lpg', 1, SSD_INNER), 'ssd_norm_w': ('lpg', 0, SSD_INNER), 'pre_norm2': ('vmlp', 0, 1024),
    'post_norm2': ('vmlp', 1, 1024)}
SMALL_ACCS = ('vin', 'vmg', 'vmlp', 'dvec', 'accl', 'accs', 'hpg', 'lpg', 'gwa', 'gwx')
SMALL_RIDE = ('vmg', 'vmlp', 'dvec', 'hpg', 'lpg', 'gwa', 'gwx')


def adamw_small(gathered, params):
    names = tuple(params)
    na = len(SMALL_ACCS)

    def body(*refs):
        acc = {k: functools.reduce(lambda p, q: p + q, [refs[i][s] for s in range(NDEV)])
               for i, k in enumerate(SMALL_ACCS)}
        ins = refs[na:na + 3 * len(names)]
        outs = refs[na + 3 * len(names):]
        for j, k in enumerate(names):
            w_ref, m_ref, v_ref = ins[3 * j:3 * j + 3]
            src, row, width = SMALL_SRC[k]
            wv = w_ref[...]
            if row is None:
                g = acc[src]
            elif k == 'ssd_d':
                li = lax.broadcasted_iota(jnp.int32, (SSD_INNER, SSD_N), 0)
                hi = lax.broadcasted_iota(jnp.int32, (SSD_INNER, SSD_N), 1)
                g = _dot_hi(acc[src], jnp.where(jnp.right_shift(li, 6) == hi, 1.0, 0.0))[row:row + 1, :SSD_HEADS]
            else:
                g = acc[src][row:row + 1, :width]
            if k == 'lru_lambda':
                g = g * (-1.0 / (1.0 + jnp.exp(wv)))
            if k == 'ssd_a_log':
                g = g * (-jnp.exp(wv))
            o = outs[4 * j:4 * j + 4]
            o[0][...] = g
            o[1][...], o[2][...], o[3][...] = _adam_update(g, wv, m_ref[...], v_ref[...])
        outs[-2][...] = acc['accl'][0:4, :]
        outs[-1][...] = acc['accs'][0:4, :]

    flat = [a for k in names for a in params[k]]
    out_shape = [jax.ShapeDtypeStruct(params[k][0].shape, F32) for k in names for _ in range(4)]
    out_shape += [jax.ShapeDtypeStruct((4, D_MODEL), F32), jax.ShapeDtypeStruct((4, 2 * SSD_INNER), F32)]
    res = pl.pallas_call(body, name="adamw_small", out_shape=out_shape,
                         compiler_params=pltpu.CompilerParams(vmem_limit_bytes=VMEM_LIMIT))(
        *[gathered[k] for k in SMALL_ACCS], *flat)
    return {k: res[4 * j:4 * j + 4] for j, k in enumerate(names)}, res[-2], res[-1]


def _dev_index(px, py, pc):
    return 4 * px + 2 * py + pc


class _Exchange:
    def __init__(self, arrs):
        self.arrs = list(arrs)
        self.na = len(self.arrs)
        self.scratch = [pltpu.SemaphoreType.DMA((7 * self.na,)), pltpu.SemaphoreType.DMA((7 * self.na,)),
                        pltpu.SemaphoreType.DMA((self.na,))]


class Gather(_Exchange):
    def __init__(self, arrs):
        super().__init__(arrs)
        self.out_shape = [jax.ShapeDtypeStruct((NDEV,) + a.shape, a.dtype) for a in self.arrs]

    def _plan(self, ins, outs, sems):
        na = self.na
        send_sems, recv_sems, local_sems = sems
        x, y, c = lax.axis_index("x"), lax.axis_index("y"), lax.axis_index("c")
        me, sibling = (x, y, c), (x, y, 1 - c)
        chips = [(1 - x, y), (x, 1 - y), (1 - x, 1 - y)]

        def copy(a, k, block, to, src=None):
            dst = outs[a].at[_dev_index(*block)]
            return pltpu.make_async_remote_copy(
                src_ref=dst if src is None else src, dst_ref=dst, send_sem=send_sems.at[a * 7 + k],
                recv_sem=recv_sems.at[a * 7 + k], device_id=to, device_id_type=MESH)

        mine = [pltpu.make_async_copy(ins[a], outs[a].at[_dev_index(*me)], local_sems.at[a]) for a in range(na)]
        first = []
        for a in range(na):
            first.append(copy(a, 0, me, sibling, src=ins[a]))
            first += [copy(a, 1 + j, me, (*chip, c), src=ins[a]) for j, chip in enumerate(chips)]
        return copy, mine, first, me, sibling, chips, c

    def start(self, ins, outs, sems):
        _, mine, first, *_ = self._plan(ins, outs, sems)
        for cp in mine + first:
            cp.start()

    def finish(self, ins, outs, sems):
        copy, mine, first, me, sibling, chips, c = self._plan(ins, outs, sems)
        passed = []
        for j, chip in enumerate(chips):
            for a in range(self.na):
                copy(a, 1 + j, (*chip, c), me).wait_recv()
                cp = copy(a, 4 + j, (*chip, c), sibling)
                cp.start()
                passed.append(cp)
        for a in range(self.na):
            copy(a, 0, sibling, me).wait_recv()
            for j, chip in enumerate(chips):
                copy(a, 4 + j, (*chip, 1 - c), me).wait_recv()
        for cp in first + passed:
            cp.wait_send()
        for cp in mine:
            cp.wait()


class Scatter(_Exchange):
    def __init__(self, arrs):
        super().__init__(arrs)
        self.out_shape = [jax.ShapeDtypeStruct(a.shape, a.dtype) for a in self.arrs]

    def _plan(self, ins, outs, sems, arrivals):
        send_sems, recv_sems, local_sems = sems
        x, y, c = lax.axis_index("x"), lax.axis_index("y"), lax.axis_index("c")
        me = _dev_index(x, y, c)
        masks = [(mx, my, mc) for mx in (0, 1) for my in (0, 1) for mc in (0, 1)][1:]
        flip = lambda v, bit: 1 - v if bit else v
        mine = [pltpu.make_async_copy(ins[a].at[me], outs[a].at[me], local_sems.at[a]) for a in range(self.na)]
        sends, recvs = [], []
        for k, (mx, my, mc) in enumerate(masks):
            peer = (flip(x, mx), flip(y, my), flip(c, mc))
            pidx = _dev_index(*peer)
            for a in range(self.na):
                on = dict(send_sem=send_sems.at[a * 7 + k], recv_sem=recv_sems.at[a * 7 + k], device_id=peer,
                          device_id_type=MESH)
                sends.append(pltpu.make_async_remote_copy(src_ref=ins[a].at[pidx], dst_ref=outs[a].at[me], **on))
                if arrivals:
                    recvs.append(pltpu.make_async_remote_copy(src_ref=ins[a].at[pidx], dst_ref=outs[a].at[pidx], **on))
        return mine, sends, recvs

    def start(self, ins, outs, sems):
        mine, sends, _ = self._plan(ins, outs, sems, arrivals=False)
        for cp in mine + sends:
            cp.start()

    def finish(self, ins, outs, sems):
        mine, sends, recvs = self._plan(ins, outs, sems, arrivals=True)
        for cp in recvs:
            cp.wait_recv()
        for cp in sends:
            cp.wait_send()
        for cp in mine:
            cp.wait()


def exchange_call(ex, name):
    na = ex.na

    def body(*refs):
        ins, outs, sems = refs[:na], refs[na:2 * na], refs[2 * na:]
        ex.start(ins, outs, sems)
        ex.finish(ins, outs, sems)

    return pl.pallas_call(body, name=name, in_specs=[ANY] * na, out_specs=[ANY] * na, out_shape=ex.out_shape,
                          scratch_shapes=ex.scratch)(*ex.arrs)


def all_gather(arrs, name):
    return exchange_call(Gather(arrs), name)


def _call(body, *, name, grid, in_specs, out_specs, out_shape, scratch_shapes=(), sem, args, side=None):
    if side is None:
        outs = pl.pallas_call(body, name=name, grid=grid, in_specs=list(in_specs), out_specs=list(out_specs),
                              out_shape=list(out_shape), scratch_shapes=list(scratch_shapes),
                              compiler_params=_cp(*sem))(*args)
        return outs, []
    ni, no, ns, na = len(in_specs), len(out_specs), len(scratch_shapes), side.na

    def wrapped(*refs):
        ins, s_in = refs[:ni], refs[ni:ni + na]
        outs, s_out = refs[ni + na:ni + na + no], refs[ni + na + no:ni + 2 * na + no]
        scr, sems = refs[ni + 2 * na + no:ni + 2 * na + no + ns], refs[ni + 2 * na + no + ns:]
        pids = [pl.program_id(i) for i in range(len(grid))]
        first = functools.reduce(lambda p, q: p & q, [p == 0 for p in pids])
        last = functools.reduce(lambda p, q: p & q, [p == g - 1 for p, g in zip(pids, grid)])

        @pl.when(first)
        def _():
            side.start(s_in, s_out, sems)

        body(*ins, *outs, *scr)

        @pl.when(last)
        def _():
            side.finish(s_in, s_out, sems)

    outs = pl.pallas_call(
        wrapped, name=name, grid=grid, in_specs=list(in_specs) + [ANY] * na, out_specs=list(out_specs) + [ANY] * na,
        out_shape=list(out_shape) + side.out_shape, scratch_shapes=list(scratch_shapes) + side.scratch,
        compiler_params=_cp(*["arbitrary"] * len(grid)))(*args, *side.arrs)
    return outs[:no], outs[no:]


WEIGHTS = ('w_ada', 'b_ada', 'pre_norm1', 'post_norm1', 'w_in', 'b_gate', 'lru_conv_w', 'lru_conv_b', 'lru_wa',
           'lru_ba', 'lru_wx', 'lru_bx', 'lru_lambda', 'w_pa', 'ssd_conv_w', 'ssd_conv_b', 'ssd_dt_bias', 'ssd_a_log',
           'ssd_d', 'ssd_norm_w', 'w_pb', 'w_out', 'pre_norm2', 'post_norm2', 'w_ff1', 'w_ff2')
BIG = ('w_in', 'w_pa', 'w_pb', 'w_out', 'w_ff1', 'w_ff2')
REPL = ('pre_norm1', 'post_norm1', 'b_gate', 'lru_conv_b', 'lru_wa', 'lru_ba', 'lru_wx', 'lru_bx', 'lru_lambda',
        'ssd_conv_b', 'ssd_dt_bias', 'ssd_a_log', 'ssd_d', 'ssd_norm_w', 'pre_norm2', 'post_norm2')
LANES = 1024


def _rows(n):
    return -(-n // LANES)


def _pack(vals, total_rows):
    parts = []
    for v in vals:
        f = v.reshape(-1).astype(F32)
        parts.append(jnp.pad(f, (0, _rows(f.shape[0]) * LANES - f.shape[0])))
    flat = jnp.concatenate(parts)
    return jnp.pad(flat.reshape(-1, LANES), ((0, total_rows - flat.shape[0] // LANES), (0, 0)))


def _unpack(slab, shapes):
    out, r = [], 0
    for s in shapes:
        n = int(np.prod(s))
        out.append(slab[r:r + _rows(n)].reshape(-1)[:n].reshape(s))
        r += _rows(n)
    return out


def _block_diag4(w):
    w4 = w.reshape(4, 4, 64, 64)
    eye = jnp.eye(4, dtype=w.dtype)
    return (w4[:, :, :, None, :] * eye[None, :, None, :, None]).reshape(4, LRU_BLOCK, LRU_BLOCK)


def _diag_blocks4(m):
    m5 = m.reshape(4, 4, 64, 4, 64)
    return jnp.stack([m5[:, a, :, a, :] for a in range(4)], axis=1).reshape(LRU_HEADS, 64, 64)


def kernel(x, c, w_ada, b_ada, pre_norm1, post_norm1, w_in, b_gate, lru_conv_w, lru_conv_b, lru_wa, lru_ba, lru_wx, lru_bx, lru_lambda, w_pa, ssd_conv_w, ssd_conv_b, ssd_dt_bias, ssd_a_log, ssd_d, ssd_norm_w, w_pb, w_out, pre_norm2, post_norm2, w_ff1, w_ff2, loss_target, m_w_ada, m_b_ada, m_pre_norm1, m_post_norm1, m_w_in, m_b_gate, m_lru_conv_w, m_lru_conv_b, m_lru_wa, m_lru_ba, m_lru_wx, m_lru_bx, m_lru_lambda, m_w_pa, m_ssd_conv_w, m_ssd_conv_b, m_ssd_dt_bias, m_ssd_a_log, m_ssd_d, m_ssd_norm_w, m_w_pb, m_w_out, m_pre_norm2, m_post_norm2, m_w_ff1, m_w_ff2, v_w_ada, v_b_ada, v_pre_norm1, v_post_norm1, v_w_in, v_b_gate, v_lru_conv_w, v_lru_conv_b, v_lru_wa, v_lru_ba, v_lru_wx, v_lru_bx, v_lru_lambda, v_w_pa, v_ssd_conv_w, v_ssd_conv_b, v_ssd_dt_bias, v_ssd_a_log, v_ssd_d, v_ssd_norm_w, v_w_pb, v_w_out, v_pre_norm2, v_post_norm2, v_w_ff1, v_w_ff2):
    given = dict(locals())
    w = {k: given[k] for k in WEIGHTS}
    mom = {k: given["m_" + k] for k in WEIGHTS}
    var = {k: given["v_" + k] for k in WEIGHTS}
    nb, seq, _ = x.shape
    assert nb == 2 and seq % 512 == 0, (nb, seq)
    t = nb * seq
    me = _dev_index(lax.axis_index("x"), lax.axis_index("y"), lax.axis_index("c"))
    x2 = x.reshape(t, D_MODEL)
    tgt2 = loss_target.reshape(t, D_MODEL)
    ada_cols = w_ada.shape[2]

    slab = jnp.zeros((16, LANES), F32)
    slab = slab.at[0:nb].set(c)
    slab = slab.at[2:6, 0:lru_conv_w.shape[2]].set(lru_conv_w[0])
    slab = slab.at[6:10, 0:ssd_conv_w.shape[2]].set(ssd_conv_w[0])
    (g1,) = all_gather([slab], "gather_cond")
    c_all = g1[:, 0:nb].reshape(NDEV * nb, D_MODEL)
    lru_cw = g1[:, 2:6, 0:lru_conv_w.shape[2]].transpose(1, 0, 2).reshape(4, D_MODEL)
    ssd_cw = g1[:, 6:10, 0:ssd_conv_w.shape[2]].transpose(1, 0, 2).reshape(4, 2 * SSD_INNER)
    b_cols = lax.dynamic_slice(b_ada, (0, me * ada_cols), (1, ada_cols))
    mod_cols = ada_fwd(c_all, w_ada[0], b_cols)
    (g2,) = all_gather([mod_cols], "gather_mod")
    mod_all = g2.transpose(1, 0, 2).reshape(NDEV * nb, N_MOD * D_MODEL)
    mod_mine = lax.dynamic_slice(mod_all, (me * nb, 0), (nb, N_MOD * D_MODEL)).reshape(nb, N_MOD, D_MODEL)
    mod8 = jnp.pad(mod_mine, ((0, 0), (0, 8 - N_MOD), (0, 0)))

    (gw_in,) = all_gather([w_in.astype(BF16)], "gather_w_in")
    shard = IN_DIM // NDEV
    kd, od = DT_COL0 // shard, DT_COL0 % shard
    assert od + SSD_HEADS <= shard
    gb = gw_in[:, 0]
    w_main = jnp.concatenate([gb[k] for k in range(kd)] + [gb[kd][:, :od], gb[kd][:, od + SSD_HEADS:]]
                             + [gb[k] for k in range(kd + 1, NDEV)], axis=1)
    w_dt = jnp.pad(gb[kd][:, od:od + SSD_HEADS], ((0, 0), (0, 128 - SSD_HEADS)))

    wa_bd = _block_diag4(lru_wa[0]).astype(BF16)
    wx_bd = _block_diag4(lru_wx[0]).astype(BF16)
    lam = lru_lambda[0]
    vec = _pack([lru_ba, lru_bx, jax.nn.softplus(-lam)], 8)
    tri, triu = ssd_consts()
    hp, lp = ssd_params(ssd_dt_bias[0], ssd_a_log[0], ssd_d[0], ssd_norm_w[0])

    rest = Gather([w[k].astype(BF16) for k in BIG[1:]])
    cw_all = jnp.concatenate([lru_cw, ssd_cw], axis=1)
    cb_all = jnp.concatenate([lru_conv_b, ssd_conv_b], axis=1)
    (proj, h1t, dtraw, xa, xbc), gw = in_proj_fwd(x2, mod8, pre_norm1, w_main, w_dt, cw_all, cb_all, seq, side=rest)
    w_pa_f = gw[0].reshape(D_MODEL, D_MODEL)
    w_pb_f = gw[1].reshape(SSD_INNER, D_MODEL)
    w_out_f = gw[2].reshape(D_MODEL, D_MODEL)
    w_ff1_f = gw[3][:, 0]
    w_ff2_f = gw[4].reshape(D_FF, D_MODEL)
    ya_in, hst = lru_fwd(xa, proj, wa_bd, wx_bd, vec, nb, seq)
    yb_in, states = ssd_fwd(xbc, proj, dtraw, hp, lp, tri, nb, seq)
    yab, out1, x1 = merge_fwd(ya_in, yb_in, proj, x2, mod8, b_gate, post_norm1, w_pa_f, w_pb_f, w_out_f, seq)

    dx1, h2, da1, act, dy2, loss8, vacc_mlp, dmod_mlp = mlp_fwd_bwd(
        x1, tgt2, mod8, pre_norm2, post_norm2, w_ff1_f, w_ff2_f, nb, seq)
    wg = dict(out_dtype=BF16, ta=True, tm=1024, tn=1024, tk=1024)
    dw_ff1 = matmul(h2, da1, name="wgrad_ff1", blocked_out=True, **{**wg, "tn": D_FF // NDEV})
    dw_ff2 = matmul(act, dy2, name="wgrad_ff2", **wg)
    dya_in, dyb_in, dgates, dyab, dout1, merged, vacc_mg, dmod_mg = merge_bwd(
        dx1, out1, yab, proj, mod8, b_gate, post_norm1, w_pa_f, w_pb_f, w_out_f, nb, seq)
    dw_out = matmul(merged, dout1, name="wgrad_out", **wg)
    dw_pa = matmul(ya_in, dyab, name="wgrad_pa", n=D_MODEL, b_off=0, **wg)
    dw_pb = matmul(yb_in, dyab, name="wgrad_pb", n=D_MODEL, b_off=1, **wg)
    by_rows = lambda g: g.reshape(NDEV, g.shape[0] // NDEV, g.shape[1])
    (dxa, dlg, dwa_bd, dwx_bd, dvec), parts_ff = lru_bwd(
        dya_in, xa, proj, hst, wa_bd, wx_bd, vec, nb, seq, side=Scatter([dw_ff1, by_rows(dw_ff2)]))
    (dxbc, dz, ddt, hpg, lpg), parts_mg = ssd_bwd(xbc, proj, dtraw, hp, lp, tri, triu, states, dyb_in, nb, seq,
                                                  side=Scatter([by_rows(dw_pa), by_rows(dw_pb), by_rows(dw_out)]))
    ddt_b = ddt.astype(BF16)
    accs = dict(vmg=vacc_mg, vmlp=vacc_mlp, dvec=dvec, hpg=hpg, lpg=lpg,
                gwa=_diag_blocks4(dwa_bd).reshape(LRU_HEADS * 64, 64), gwx=_diag_blocks4(dwx_bd).reshape(LRU_HEADS * 64, 64))
    (dw_main, dw_dt, dlx, dxr, acc_l, acc_s), g_small = in_proj_wgrad(
        h1t, proj, dxa, dxbc, dlg, dz, dgates, ddt_b, cw_all, cb_all, seq, side=Gather([accs[k] for k in SMALL_RIDE]))
    pieces = (dlx, dlg, dz, dxr, dgates)
    cut = lambda k: dw_main[:, k * shard - (SSD_HEADS if k > kd else 0):(k + 1) * shard - (SSD_HEADS if k >= kd else 0)]
    blk_dt = jnp.concatenate([dw_main[:, kd * shard:DT_COL0], dw_dt[:, :SSD_HEADS],
                              dw_main[:, DT_COL0:(kd + 1) * shard - SSD_HEADS]], axis=1)
    dw_blocks = jnp.stack([blk_dt if k == kd else cut(k) for k in range(NDEV)])
    (grad_x, vacc_in, dmod_in), parts_in = in_proj_bwd(pieces, ddt_b, dx1, x2, mod8, pre_norm1, w_main, w_dt, nb, seq,
                                                       side=Scatter([dw_blocks]))
    parts = dict(zip(BIG, (parts_in[0], *parts_mg, *parts_ff)))

    dmod = (dmod_in + dmod_mg + dmod_mlp)[:, :N_MOD].reshape(nb, N_MOD * D_MODEL)
    g3, g_vin, g_accl, g_accs = all_gather([jnp.pad(dmod, ((0, 8 - nb), (0, 0))), vacc_in, acc_l, acc_s], "gather_dmod")
    dmod_all = g3[:, :nb].reshape(NDEV * nb, N_MOD * D_MODEL)
    dmod_cols = lax.dynamic_slice(dmod_all, (0, me * ada_cols), (NDEV * nb, ada_cols))
    g_w_ada, g_b_ada = ada_bwd(c_all, dmod_cols, dmod_all)

    res = {}
    for k in BIG:
        res[k] = adamw(parts[k], w[k], mom[k], var[k], "adamw_" + k)
    res['w_ada'] = adamw(g_w_ada[None], w_ada, m_w_ada, v_w_ada, "adamw_w_ada")

    gathered = dict(zip(SMALL_RIDE, g_small), vin=g_vin, accl=g_accl, accs=g_accs)
    view = lambda a: a.reshape(-1, a.shape[-1])
    res_a, g_lru_cw, g_ssd_cw = adamw_small(gathered, {k: (view(w[k]), view(mom[k]), view(var[k])) for k in REPL})
    res.update(res_a)
    lcw, scw = lru_conv_w.shape[2], ssd_conv_w.shape[2]
    sharded = {'b_ada': g_b_ada[None], 'lru_conv_w': lax.dynamic_slice(g_lru_cw, (0, me * lcw), (4, lcw))[None],
               'ssd_conv_w': lax.dynamic_slice(g_ssd_cw, (0, me * scw), (4, scw))[None]}
    for k, g in sharded.items():
        as3 = lambda a: a.reshape(g.shape)
        res[k] = adamw(g, as3(w[k]), as3(mom[k]), as3(var[k]), "adamw_" + k)

    loss = lax.psum(loss8[0, 0], ("x", "y", "c"))
    outs = [[res[k][j].reshape(w[k].shape) for k in WEIGHTS] for j in range(4)]
    return (loss, grad_x.reshape(x.shape), *outs[0], *outs[1], *outs[2], *outs[3])
```

```python
import functools

import numpy as np
import jax
import jax.numpy as jnp
from jax import lax
from jax.experimental import pallas as pl
from jax.experimental.pallas import tpu as pltpu

F32 = jnp.float32
BF16 = jnp.bfloat16

D_MODEL = 1024
LRU_HEADS = 16
LRU_BLOCK = 256
LRU_C = 8.0
SSD_INNER = 2048
SSD_HEADS = 32
SSD_P = 64
SSD_G = 8
SSD_N = 128
SSD_L = 128
SSD_GW = SSD_INNER // SSD_G
D_FF = 4096
N_MOD = 6
EPS = 1e-6
NDEV = 8

C_LRU_X, C_LRU_G, C_Z, C_XBC, C_GATES, PROJ_MAIN = 0, 1024, 2048, 4096, 8192, 10240
IN_DIM = 10272
DT_COL0 = 8192
HALO = 16
HT_TOK = 512

ADAM_LR, ADAM_B1, ADAM_B2, ADAM_EPS, ADAM_WD, ADAM_STEP = 0.001, 0.9, 0.999, 1e-08, 0.01, 10

VMEM_LIMIT = 60 * 1024 * 1024
MESH = pl.DeviceIdType.MESH
ANY = pl.BlockSpec(memory_space=pl.ANY)
VMEM_FULL = pl.BlockSpec(memory_space=pltpu.VMEM)


def _cp(*sem):
    return pltpu.CompilerParams(dimension_semantics=sem, vmem_limit_bytes=VMEM_LIMIT)


def _dot(a, b):
    return jnp.dot(a, b, preferred_element_type=F32)


def _dot_nt(a, b):
    return lax.dot_general(a, b, (((1,), (1,)), ((), ())), preferred_element_type=F32)


def _dot_tn(a, b):
    return lax.dot_general(a, b, (((0,), (0,)), ((), ())), preferred_element_type=F32)


def _dot_hi(a, b):
    return jnp.dot(a, b, precision=lax.Precision.HIGHEST, preferred_element_type=F32)


def _sigmoid(x):
    return 1.0 / (1.0 + jnp.exp(-x))


def _gelu_and_grad(x):
    k0, k1 = 0.7978845608028654, 0.044715
    t = jnp.tanh(k0 * (x + k1 * x * x * x))
    g = 0.5 * x * (1.0 + t)
    dg = 0.5 * (1.0 + t) + 0.5 * x * (1.0 - t * t) * k0 * (1.0 + 3.0 * k1 * x * x)
    return g, dg


def _neg_expm1(y):
    p = 1.0 + y * (1.0 / 7.0)
    p = 1.0 + y * (1.0 / 6.0) * p
    p = 1.0 + y * (1.0 / 5.0) * p
    p = 1.0 + y * (1.0 / 4.0) * p
    p = 1.0 + y * (1.0 / 3.0) * p
    p = 1.0 + y * 0.5 * p
    return jnp.where(y > -0.3, -y * p, 1.0 - jnp.exp(y))


def _colsum(v):
    return jnp.sum(v, axis=0, keepdims=True)


def _rowmean(v):
    return jnp.mean(v, axis=-1, keepdims=True)


def matmul(a, b, *, ta=False, tb=False, out_dtype=F32, tm, tn, tk, name, n=None, b_off=0, blocked_out=False):
    m = a.shape[1] if ta else a.shape[0]
    kdim = a.shape[0] if ta else a.shape[1]
    n = n or (b.shape[0] if tb else b.shape[1])
    tm, tn, tk = min(tm, m), min(tn, n), min(tk, kdim)
    nk = kdim // tk
    dn = (((0 if ta else 1,), (1 if tb else 0,)), ((), ()))
    bw = blocked_out or tn

    def body(a_ref, b_ref, o_ref, acc_ref):
        k = pl.program_id(2)
        p = lax.dot_general(a_ref[...], b_ref[...], dn, preferred_element_type=F32)

        def emit(v):
            if blocked_out:
                for q in range(tn // bw):
                    o_ref[q] = v[:, q * bw:(q + 1) * bw].astype(out_dtype)
            else:
                o_ref[...] = v.astype(out_dtype)

        if nk == 1:
            emit(p)
        else:
            @pl.when(k == 0)
            def _():
                acc_ref[...] = p

            @pl.when(k > 0)
            def _():
                acc_ref[...] += p

            @pl.when(k == nk - 1)
            def _():
                emit(acc_ref[...])

    a_spec = pl.BlockSpec((tk, tm), lambda i, j, k: (k, i)) if ta else pl.BlockSpec((tm, tk), lambda i, j, k: (i, k))
    b_spec = (pl.BlockSpec((tn, tk), lambda i, j, k: (j, k)) if tb
              else pl.BlockSpec((tk, tn), lambda i, j, k: (k, j + b_off)))
    if blocked_out:
        o_spec, o_shape = pl.BlockSpec((tn // bw, tm, bw), lambda i, j, k: (j, i, 0)), (n // bw, m, bw)
    else:
        o_spec, o_shape = pl.BlockSpec((tm, tn), lambda i, j, k: (i, j)), (m, n)
    return pl.pallas_call(
        body, name=name, grid=(m // tm, n // tn, nk),
        in_specs=[a_spec, b_spec], out_specs=o_spec,
        out_shape=jax.ShapeDtypeStruct(o_shape, out_dtype),
        scratch_shapes=[pltpu.VMEM((tm, tn), F32)],
        compiler_params=_cp("parallel", "parallel", "arbitrary"),
    )(a, b)


def _conv_tile(j, tn):
    return jnp.where(j == 0, 0, jnp.clip(j - C_XBC // tn + 1, 1, 2 * SSD_INNER // tn))


def in_proj_fwd(x2, mod8, pre1, w_main, w_dt, cw, cb, seq, side=None):
    t = x2.shape[0]
    tm = min(1024, seq)
    tn = 1024
    per_seq = seq // tm
    j_xbc = C_XBC // tn
    n_xbc = 2 * SSD_INNER // tn
    cs = 256

    def body(x_ref, mod_ref, pre_ref, w_ref, wdt_ref, cw_ref, cb_ref, proj_ref, h_ref, dt_ref, xa_ref, xbc_ref,
             h_scr, carry_scr):
        i, j = pl.program_id(0), pl.program_id(1)

        @pl.when(j == 0)
        def _():
            xv = x_ref[...]
            y = xv * lax.rsqrt(_rowmean(xv * xv) + EPS) * pre_ref[...]
            m = mod_ref[0]
            hf = y * (1.0 + m[1:2, :]) + m[0:1, :]
            h = hf.astype(BF16)
            h_scr[...] = h
            hft = hf.T.astype(BF16)
            for q in range(tm // HT_TOK):
                h_ref[q] = hft[:, q * HT_TOK:(q + 1) * HT_TOK]
            dt_ref[...] = _dot(h, wdt_ref[...])

        def project(c0=0, width=tn):
            pb = _dot(h_scr[...], w_ref[:, c0:c0 + width]).astype(BF16)
            proj_ref[:, c0:c0 + width] = pb
            return pb

        def conv(o_ref, slot, act):
            first = lax.rem(i, per_seq) == 0
            for c0 in range(0, tn, cs):
                cur = project(c0, cs).astype(F32)
                prev = jnp.where(first, 0.0, carry_scr[slot, :, c0:c0 + cs])
                carry_scr[slot, :, c0:c0 + cs] = cur[tm - HALO:, :]
                xx = jnp.concatenate([prev, cur], axis=0)
                w = cw_ref[:, c0:c0 + cs]
                acc = cur * w[3:4, :] + cb_ref[:, c0:c0 + cs]
                for d in (1, 2, 3):
                    acc = acc + pltpu.roll(xx, d, axis=0)[HALO:, :] * w[3 - d:4 - d, :]
                if act:
                    acc = acc * _sigmoid(acc)
                o_ref[:, c0:c0 + cs] = acc.astype(BF16)

        is_xbc = (j >= j_xbc) & (j < j_xbc + n_xbc)

        @pl.when(j == 0)
        def _():
            conv(xa_ref, 0, False)

        @pl.when(is_xbc)
        def _():
            conv(xbc_ref, j - j_xbc + 1, True)

        @pl.when((j > 0) & jnp.logical_not(is_xbc))
        def _():
            project()

    return _call(
        body, name="in_proj_fwd", grid=(t // tm, PROJ_MAIN // tn), side=side, sem=("arbitrary", "arbitrary"),
        args=(x2, mod8, pre1, w_main, w_dt, cw, cb),
        in_specs=[pl.BlockSpec((tm, D_MODEL), lambda i, j: (i, 0)),
                  pl.BlockSpec((1, 8, D_MODEL), lambda i, j: (i // per_seq, 0, 0)),
                  pl.BlockSpec((1, D_MODEL), lambda i, j: (0, 0)),
                  pl.BlockSpec((D_MODEL, tn), lambda i, j: (0, j)),
                  pl.BlockSpec((D_MODEL, 128), lambda i, j: (0, 0)),
                  pl.BlockSpec((4, tn), lambda i, j: (0, _conv_tile(j, tn))),
                  pl.BlockSpec((1, tn), lambda i, j: (0, _conv_tile(j, tn)))],
        out_specs=[pl.BlockSpec((tm, tn), lambda i, j: (i, j)),
                   pl.BlockSpec((tm // HT_TOK, D_MODEL, HT_TOK), lambda i, j: (i, 0, 0)),
                   pl.BlockSpec((tm, 128), lambda i, j: (i, 0)),
                   pl.BlockSpec((tm, tn), lambda i, j: (i, 0)),
                   pl.BlockSpec((tm, tn), lambda i, j: (i, jnp.clip(j - j_xbc, 0, n_xbc - 1)))],
        out_shape=[jax.ShapeDtypeStruct((t, PROJ_MAIN), BF16), jax.ShapeDtypeStruct((t // HT_TOK, D_MODEL, HT_TOK), BF16),
                   jax.ShapeDtypeStruct((t, 128), F32), jax.ShapeDtypeStruct((t, D_MODEL), BF16),
                   jax.ShapeDtypeStruct((t, 2 * SSD_INNER), BF16)],
        scratch_shapes=[pltpu.VMEM((tm, D_MODEL), BF16), pltpu.VMEM((1 + n_xbc, HALO, tn), F32)])


def _lru_gates(xa, wa_ref, wx_ref, ba, bx, sp):
    nblk = D_MODEL // LRU_BLOCK
    pr = jnp.concatenate([_dot(xa[:, j * LRU_BLOCK:(j + 1) * LRU_BLOCK], wa_ref[j]) for j in range(nblk)], axis=1) + ba
    pi = jnp.concatenate([_dot(xa[:, j * LRU_BLOCK:(j + 1) * LRU_BLOCK], wx_ref[j]) for j in range(nblk)], axis=1) + bx
    r = _sigmoid(pr)
    i = _sigmoid(pi)
    log_a = (-LRU_C * r) * sp
    return r, i, jnp.exp(log_a), _neg_expm1(2.0 * log_a)


def lru_fwd(xa, proj, wa_bd, wx_bd, vec, nb, seq):
    t = xa.shape[0]
    tc = min(512, seq)
    nk = seq // tc
    gb = C_LRU_G // D_MODEL

    def body(xa_ref, g_ref, wa_ref, wx_ref, vec_ref, ya_ref, h_ref, a_scr, u_scr, hc_scr):
        @pl.when(pl.program_id(1) == 0)
        def _():
            hc_scr[...] = jnp.zeros_like(hc_scr)

        xa_v = xa_ref[...]
        v = vec_ref[...]
        r, i, a, e = _lru_gates(xa_v, wa_ref, wx_ref, v[0:1, :], v[1:2, :], v[2:3, :])
        a_scr[...] = a
        u_scr[...] = jnp.sqrt(e) * (i * xa_v.astype(F32))
        row = lax.broadcasted_iota(jnp.int32, (8, 1), 0)

        def tile(j, h):
            r0 = pl.multiple_of(j * 8, 8)
            av, uv = a_scr[pl.ds(r0, 8), :], u_scr[pl.ds(r0, 8), :]
            for d in (1, 2, 4):
                uv = uv + av * jnp.where(row >= d, pltpu.roll(uv, d, axis=0), 0.0)
                av = av * jnp.where(row >= d, pltpu.roll(av, d, axis=0), 1.0)
            hv = uv + av * h
            h_ref[pl.ds(r0, 8), :] = hv
            return hv[7:8, :]

        hc_scr[...] = lax.fori_loop(0, tc // 8, tile, hc_scr[...], unroll=2)
        gel, _ = _gelu_and_grad(g_ref[...].astype(F32))
        ya_ref[...] = (h_ref[...] * gel).astype(BF16)

    return pl.pallas_call(
        body, name="lru_fwd", grid=(nb, nk),
        in_specs=[pl.BlockSpec((tc, D_MODEL), lambda b, k: (b * nk + k, 0)),
                  pl.BlockSpec((tc, D_MODEL), lambda b, k: (b * nk + k, gb)),
                  VMEM_FULL, VMEM_FULL, VMEM_FULL],
        out_specs=[pl.BlockSpec((tc, D_MODEL), lambda b, k: (b * nk + k, 0)),
                   pl.BlockSpec((tc, D_MODEL), lambda b, k: (b * nk + k, 0))],
        out_shape=[jax.ShapeDtypeStruct((t, D_MODEL), BF16), jax.ShapeDtypeStruct((t, D_MODEL), F32)],
        scratch_shapes=[pltpu.VMEM((tc, D_MODEL), F32), pltpu.VMEM((tc, D_MODEL), F32), pltpu.VMEM((1, D_MODEL), F32)],
        compiler_params=_cp("arbitrary", "arbitrary"),
    )(xa, proj, wa_bd, wx_bd, vec)


def lru_bwd(dya, xa, proj, h, wa_bd, wx_bd, vec, nb, seq, side=None):
    t = xa.shape[0]
    tc = min(512, seq)
    nk = seq // tc
    gb = C_LRU_G // D_MODEL
    nblk = D_MODEL // LRU_BLOCK

    def chunk(b, k):
        return b * nk + (nk - 1 - k)

    def body(dya_ref, xa_ref, g_ref, h_ref, hp_ref, wa_ref, wx_ref, vec_ref,
             dxa_ref, dg_ref, dwa_ref, dwx_ref, dvec_ref, a_scr, dh_scr, c_scr):
        b, k = pl.program_id(0), pl.program_id(1)

        @pl.when((b == 0) & (k == 0))
        def _():
            dwa_ref[...] = jnp.zeros_like(dwa_ref)
            dwx_ref[...] = jnp.zeros_like(dwx_ref)
            dvec_ref[...] = jnp.zeros_like(dvec_ref)

        @pl.when(k == 0)
        def _():
            c_scr[...] = jnp.zeros_like(c_scr)

        xa_v = xa_ref[...]
        xaf = xa_v.astype(F32)
        v = vec_ref[...]
        sp = v[2:3, :]
        r, i, a, e = _lru_gates(xa_v, wa_ref, wx_ref, v[0:1, :], v[1:2, :], sp)
        gel, dgel = _gelu_and_grad(g_ref[...].astype(F32))
        hv = h_ref[...]
        dyv = dya_ref[...].astype(F32)
        dg_ref[...] = (dyv * hv * dgel).astype(BF16)
        a_scr[...] = a
        dh_scr[...] = dyv * gel

        row8 = lax.broadcasted_iota(jnp.int32, (8, 1), 0)

        def tile(j, c):
            r0 = pl.multiple_of((tc // 8 - 1 - j) * 8, 8)
            av, dout = a_scr[pl.ds(r0, 8), :], dh_scr[pl.ds(r0, 8), :]
            zv = av * dout
            for d in (1, 2, 4):
                zv = zv + av * jnp.where(row8 < 8 - d, pltpu.roll(zv, 8 - d, axis=0), 0.0)
                av = av * jnp.where(row8 < 8 - d, pltpu.roll(av, 8 - d, axis=0), 1.0)
            zv = zv + av * c
            dh_scr[pl.ds(r0, 8), :] = dout + jnp.where(row8 < 7, pltpu.roll(zv, 7, axis=0), c)
            return zv[0:1, :]

        c_scr[...] = lax.fori_loop(0, tc // 8, tile, c_scr[...], unroll=2)
        dh = dh_scr[...]
        h_last = jnp.where(k == nk - 1, 0.0, hp_ref[HALO // 2 - 1:HALO // 2, :])
        row = lax.broadcasted_iota(jnp.int32, (tc, 1), 0)
        h_prev = jnp.where(row == 0, h_last, pltpu.roll(hv, 1, axis=0))
        s = jnp.sqrt(e)
        da = dh * h_prev
        ix = i * xaf
        dlog_a = da * a - (dh * ix) * (a * a) * lax.rsqrt(jnp.maximum(e, 1e-30))
        di = dh * s * xaf
        dpr = (dlog_a * (-LRU_C * sp)) * (r * (1.0 - r))
        dpi = di * (i * (1.0 - i))
        dprb, dpib = dpr.astype(BF16), dpi.astype(BF16)
        dxa = dh * s * i
        dxa = dxa + jnp.concatenate(
            [_dot_nt(dprb[:, j * LRU_BLOCK:(j + 1) * LRU_BLOCK], wa_ref[j])
             + _dot_nt(dpib[:, j * LRU_BLOCK:(j + 1) * LRU_BLOCK], wx_ref[j]) for j in range(nblk)], axis=1)
        dxa_ref[...] = dxa.astype(BF16)
        for j in range(nblk):
            sl = slice(j * LRU_BLOCK, (j + 1) * LRU_BLOCK)
            dwa_ref[j] += _dot_tn(xa_v[:, sl], dprb[:, sl])
            dwx_ref[j] += _dot_tn(xa_v[:, sl], dpib[:, sl])
        dvec_ref[...] += jnp.concatenate(
            [_colsum(dpr), _colsum(dpi), _colsum(dlog_a * (-LRU_C * r)), jnp.zeros((5, D_MODEL), F32)], axis=0)

    hh = HALO // 2
    return _call(
        body, name="lru_bwd", grid=(nb, nk), side=side, sem=("arbitrary", "arbitrary"),
        args=(dya, xa, proj, h, h, wa_bd, wx_bd, vec),
        in_specs=[pl.BlockSpec((tc, D_MODEL), lambda b, k: (chunk(b, k), 0)),
                  pl.BlockSpec((tc, D_MODEL), lambda b, k: (chunk(b, k), 0)),
                  pl.BlockSpec((tc, D_MODEL), lambda b, k: (chunk(b, k), gb)),
                  pl.BlockSpec((tc, D_MODEL), lambda b, k: (chunk(b, k), 0)),
                  pl.BlockSpec((hh, D_MODEL), lambda b, k: (jnp.maximum(chunk(b, k) * (tc // hh) - 1, 0), 0)),
                  VMEM_FULL, VMEM_FULL, VMEM_FULL],
        out_specs=[pl.BlockSpec((tc, D_MODEL), lambda b, k: (chunk(b, k), 0)),
                   pl.BlockSpec((tc, D_MODEL), lambda b, k: (chunk(b, k), 0)),
                   pl.BlockSpec((nblk, LRU_BLOCK, LRU_BLOCK), lambda b, k: (0, 0, 0)),
                   pl.BlockSpec((nblk, LRU_BLOCK, LRU_BLOCK), lambda b, k: (0, 0, 0)),
                   pl.BlockSpec((8, D_MODEL), lambda b, k: (0, 0))],
        out_shape=[jax.ShapeDtypeStruct((t, D_MODEL), BF16), jax.ShapeDtypeStruct((t, D_MODEL), BF16),
                   jax.ShapeDtypeStruct((nblk, LRU_BLOCK, LRU_BLOCK), F32),
                   jax.ShapeDtypeStruct((nblk, LRU_BLOCK, LRU_BLOCK), F32),
                   jax.ShapeDtypeStruct((8, D_MODEL), F32)],
        scratch_shapes=[pltpu.VMEM((tc, D_MODEL), F32), pltpu.VMEM((tc, D_MODEL), F32), pltpu.VMEM((1, D_MODEL), F32)])


def merge_fwd(ya_in, yb_in, proj, x2, mod8, bgate, post1, w_pa, w_pb, w_out, seq):
    t = x2.shape[0]
    tm = min(512, seq)
    per_seq = seq // tm
    gcb = C_GATES // SSD_INNER

    def body(ya_ref, yb_ref, gt_ref, x_ref, mod_ref, bg_ref, post_ref, wpa_ref, wpb_ref, wo_ref,
             yab_ref, out1_ref, x1_ref):
        y_a = _dot(ya_ref[...], wpa_ref[...])
        y_b = _dot(yb_ref[...], wpb_ref[...])
        g = _sigmoid(gt_ref[...].astype(F32) + bg_ref[...])
        merged = g[:, :D_MODEL] * y_a + g[:, D_MODEL:] * y_b
        out1 = _dot(merged.astype(BF16), wo_ref[...])
        n = out1 * lax.rsqrt(_rowmean(out1 * out1) + EPS)
        yab_ref[...] = jnp.concatenate([y_a, y_b], axis=1).astype(BF16)
        out1_ref[...] = out1
        x1_ref[...] = x_ref[...] + mod_ref[0][2:3, :] * (n * post_ref[...])

    row = lambda w: pl.BlockSpec((tm, w), lambda i: (i, 0))
    return pl.pallas_call(
        body, name="merge_fwd", grid=(t // tm,),
        in_specs=[row(D_MODEL), row(SSD_INNER), pl.BlockSpec((tm, SSD_INNER), lambda i: (i, gcb)), row(D_MODEL),
                  pl.BlockSpec((1, 8, D_MODEL), lambda i: (i // per_seq, 0, 0)),
                  VMEM_FULL, VMEM_FULL, VMEM_FULL, VMEM_FULL, VMEM_FULL],
        out_specs=[row(SSD_INNER), row(D_MODEL), row(D_MODEL)],
        out_shape=[jax.ShapeDtypeStruct((t, SSD_INNER), BF16), jax.ShapeDtypeStruct((t, D_MODEL), F32),
                   jax.ShapeDtypeStruct((t, D_MODEL), F32)],
        compiler_params=_cp("parallel"),
    )(ya_in, yb_in, proj, x2, mod8, bgate, post1, w_pa, w_pb, w_out)


def merge_bwd(dx1, out1, yab, proj, mod8, bgate, post1, w_pa, w_pb, w_out, nb, seq):
    t = dx1.shape[0]
    tm = min(512, seq)
    per_seq = seq // tm
    gcb = C_GATES // SSD_INNER

    def body(dx1_ref, out1_ref, yab_ref, gt_ref, mod_ref, bg_ref, post_ref, wpa_ref, wpb_ref, wo_ref,
             dya_ref, dyb_ref, dgt_ref, dyab_ref, dout1_ref, mg_ref, vacc_ref, dmod_ref):
        b, s = pl.program_id(0), pl.program_id(1)

        @pl.when((b == 0) & (s == 0))
        def _():
            vacc_ref[...] = jnp.zeros_like(vacc_ref)

        @pl.when(s == 0)
        def _():
            dmod_ref[...] = jnp.zeros_like(dmod_ref)

        dx1v = dx1_ref[...]
        out1 = out1_ref[...]
        post = post_ref[...]
        rs = lax.rsqrt(_rowmean(out1 * out1) + EPS)
        n = out1 * rs
        do = dx1v * mod_ref[0][2:3, :]
        dn = do * post
        dout1 = rs * (dn - n * _rowmean(dn * n))
        dout1b = dout1.astype(BF16)
        dout1_ref[...] = dout1b
        dmerged = _dot_nt(dout1b, wo_ref[...])
        g = _sigmoid(gt_ref[...].astype(F32) + bg_ref[...])
        yab_v = yab_ref[...].astype(F32)
        gy = g * yab_v
        mg_ref[...] = (gy[:, :D_MODEL] + gy[:, D_MODEL:]).astype(BF16)
        dm2 = jnp.concatenate([dmerged, dmerged], axis=1)
        dyab = (dm2 * g).astype(BF16)
        dyab_ref[...] = dyab
        dgt = dm2 * gy * (1.0 - g)
        dgt_ref[...] = dgt.astype(BF16)
        dya_ref[...] = _dot_nt(dyab[:, :D_MODEL], wpa_ref[...]).astype(BF16)
        dyb_ref[...] = _dot_nt(dyab[:, D_MODEL:], wpb_ref[...]).astype(BF16)
        vacc_ref[...] += jnp.concatenate(
            [_colsum(dgt), jnp.concatenate([_colsum(do * n), jnp.zeros((1, D_MODEL), F32)], axis=1),
             jnp.zeros((6, SSD_INNER), F32)], axis=0)
        dmod_ref[0] += jnp.concatenate(
            [jnp.zeros((2, D_MODEL), F32), _colsum(dx1v * (n * post)), jnp.zeros((5, D_MODEL), F32)], axis=0)

    row = lambda w: pl.BlockSpec((tm, w), lambda b, s: (b * per_seq + s, 0))
    return pl.pallas_call(
        body, name="merge_bwd", grid=(nb, per_seq),
        in_specs=[row(D_MODEL), row(D_MODEL), row(SSD_INNER),
                  pl.BlockSpec((tm, SSD_INNER), lambda b, s: (b * per_seq + s, gcb)),
                  pl.BlockSpec((1, 8, D_MODEL), lambda b, s: (b, 0, 0)),
                  VMEM_FULL, VMEM_FULL, VMEM_FULL, VMEM_FULL, VMEM_FULL],
        out_specs=[row(D_MODEL), row(SSD_INNER), row(SSD_INNER), row(SSD_INNER), row(D_MODEL), row(D_MODEL),
                   pl.BlockSpec((8, SSD_INNER), lambda b, s: (0, 0)),
                   pl.BlockSpec((1, 8, D_MODEL), lambda b, s: (b, 0, 0))],
        out_shape=[jax.ShapeDtypeStruct((t, D_MODEL), BF16), jax.ShapeDtypeStruct((t, SSD_INNER), BF16),
                   jax.ShapeDtypeStruct((t, SSD_INNER), BF16), jax.ShapeDtypeStruct((t, SSD_INNER), BF16),
                   jax.ShapeDtypeStruct((t, D_MODEL), BF16), jax.ShapeDtypeStruct((t, D_MODEL), BF16),
                   jax.ShapeDtypeStruct((8, SSD_INNER), F32), jax.ShapeDtypeStruct((nb, 8, D_MODEL), F32)],
        compiler_params=_cp("arbitrary", "arbitrary"),
    )(dx1, out1, yab, proj, mod8, bgate, post1, w_pa, w_pb, w_out)


def mlp_fwd_bwd(x1, tgt, mod8, pre2, post2, w_ff1, w_ff2, nb, seq):
    t = x1.shape[0]
    tm = min(256, seq)
    per_seq = seq // tm
    fc = 1024
    nfc = D_FF // fc

    def body(x1_ref, tgt_ref, mod_ref, pre_ref, post_ref, w1_ref, w2_ref,
             dx1_ref, h2_ref, da1_ref, act_ref, dy2_ref, loss_ref, vacc_ref, dmod_ref, r_scr):
        b, s = pl.program_id(0), pl.program_id(1)
        per = fc // w1_ref.shape[2]

        def w1_cols(c):
            return jnp.concatenate([w1_ref[per * c + q] for q in range(per)], axis=1)

        @pl.when((b == 0) & (s == 0))
        def _():
            vacc_ref[...] = jnp.zeros_like(vacc_ref)
            loss_ref[...] = jnp.zeros_like(loss_ref)

        @pl.when(s == 0)
        def _():
            dmod_ref[...] = jnp.zeros_like(dmod_ref)

        m = mod_ref[0]
        sh2, sc2, g2 = m[3:4, :], m[4:5, :], m[5:6, :]
        pre, post = pre_ref[...], post_ref[...]
        x1v = x1_ref[...]
        rs1 = lax.rsqrt(_rowmean(x1v * x1v) + EPS)
        n1 = x1v * rs1
        y1 = n1 * pre
        h2b = (y1 * (1.0 + sc2) + sh2).astype(BF16)
        h2_ref[...] = h2b
        y2 = jnp.zeros((tm, D_MODEL), F32)
        for c in range(nfc):
            r = jnp.maximum(_dot(h2b, w1_cols(c)), 0.0)
            r_scr[:, c * fc:(c + 1) * fc] = r
            a = (r * r).astype(BF16)
            act_ref[:, c * fc:(c + 1) * fc] = a
            y2 = y2 + _dot(a, w2_ref[c * fc:(c + 1) * fc, :])
        rs2 = lax.rsqrt(_rowmean(y2 * y2) + EPS)
        n2 = y2 * rs2
        o2 = n2 * post
        diff = x1v + g2 * o2 - tgt_ref[...]
        loss_ref[...] += 0.5 * jnp.sum(_rowmean(diff * diff))
        dx2 = diff * (1.0 / D_MODEL)
        do2 = dx2 * g2
        dn2 = do2 * post
        dy2b = (rs2 * (dn2 - n2 * _rowmean(dn2 * n2))).astype(BF16)
        dy2_ref[...] = dy2b
        dh2 = jnp.zeros((tm, D_MODEL), F32)
        for c in range(nfc):
            dact = _dot_nt(dy2b, w2_ref[c * fc:(c + 1) * fc, :])
            da = (dact * (2.0 * r_scr[:, c * fc:(c + 1) * fc])).astype(BF16)
            da1_ref[:, c * fc:(c + 1) * fc] = da
            dh2 = dh2 + _dot_nt(da, w1_cols(c))
        dy1 = dh2 * (1.0 + sc2)
        dn1 = dy1 * pre
        dx1_ref[...] = dx2 + rs1 * (dn1 - n1 * _rowmean(dn1 * n1))
        vacc_ref[...] += jnp.concatenate([_colsum(dy1 * n1), _colsum(do2 * n2), jnp.zeros((6, D_MODEL), F32)], axis=0)
        dmod_ref[0] += jnp.concatenate(
            [jnp.zeros((3, D_MODEL), F32), _colsum(dh2), _colsum(dh2 * y1), _colsum(dx2 * o2),
             jnp.zeros((2, D_MODEL), F32)], axis=0)

    row = lambda w: pl.BlockSpec((tm, w), lambda b, s: (b * per_seq + s, 0))
    return pl.pallas_call(
        body, name="mlp_fwd_bwd", grid=(nb, per_seq),
        in_specs=[row(D_MODEL), row(D_MODEL), pl.BlockSpec((1, 8, D_MODEL), lambda b, s: (b, 0, 0)),
                  VMEM_FULL, VMEM_FULL, VMEM_FULL, VMEM_FULL],
        out_specs=[row(D_MODEL), row(D_MODEL), row(D_FF), row(D_FF), row(D_MODEL),
                   pl.BlockSpec((8, 128), lambda b, s: (0, 0)),
                   pl.BlockSpec((8, D_MODEL), lambda b, s: (0, 0)),
                   pl.BlockSpec((1, 8, D_MODEL), lambda b, s: (b, 0, 0))],
        out_shape=[jax.ShapeDtypeStruct((t, D_MODEL), F32), jax.ShapeDtypeStruct((t, D_MODEL), BF16),
                   jax.ShapeDtypeStruct((t, D_FF), BF16), jax.ShapeDtypeStruct((t, D_FF), BF16),
                   jax.ShapeDtypeStruct((t, D_MODEL), BF16), jax.ShapeDtypeStruct((8, 128), F32),
                   jax.ShapeDtypeStruct((8, D_MODEL), F32), jax.ShapeDtypeStruct((nb, 8, D_MODEL), F32)],
        scratch_shapes=[pltpu.VMEM((tm, D_FF), F32)],
        compiler_params=_cp("arbitrary", "arbitrary"),
    )(x1, tgt, mod8, pre2, post2, w_ff1, w_ff2)


_PIECES = ((C_LRU_X, 1024), (C_LRU_G, 1024), (C_Z, 2048), (C_XBC, 4096), (C_GATES, 2048))
_NP = len(_PIECES)


def in_proj_bwd(pieces, ddt, dx1, x2, mod8, pre1, w_main, w_dt, nb, seq, side=None):
    t = x2.shape[0]
    tm = min(512, seq)
    per_seq = seq // tm
    widths = [min(w, 2048) for _, w in _PIECES]
    steps = [(p, q) for p, (_, w) in enumerate(_PIECES) for q in range(w // widths[p])]
    nk = len(steps)

    def piece_spec(p):
        first = min(k for k in range(nk) if steps[k][0] == p)
        nblk = _PIECES[p][1] // widths[p]
        return pl.BlockSpec((tm, widths[p]), lambda b, s, k: (b * per_seq + s, jnp.clip(k - first, 0, nblk - 1)))

    def body(*refs):
        prefs = refs[:_NP]
        ddt_ref, dx1_ref, x_ref, mod_ref, pre_ref, w_ref, wdt_ref, gx_ref, vacc_ref, dmod_ref, acc_ref = refs[_NP:]
        b, s, k = pl.program_id(0), pl.program_id(1), pl.program_id(2)

        @pl.when((b == 0) & (s == 0) & (k == 0))
        def _():
            vacc_ref[...] = jnp.zeros_like(vacc_ref)

        @pl.when((s == 0) & (k == 0))
        def _():
            dmod_ref[...] = jnp.zeros_like(dmod_ref)

        @pl.when(k == 0)
        def _():
            acc_ref[...] = _dot_nt(ddt_ref[...], wdt_ref[...])

        for kk, (p, q) in enumerate(steps):
            @pl.when(k == kk)
            def _(p=p, q=q):
                c0 = _PIECES[p][0] + q * widths[p]
                acc_ref[...] += _dot_nt(prefs[p][...], w_ref[:, c0:c0 + widths[p]])

        @pl.when(k == nk - 1)
        def _():
            dh = acc_ref[...]
            m = mod_ref[0]
            pre = pre_ref[...]
            xv = x_ref[...]
            rs = lax.rsqrt(_rowmean(xv * xv) + EPS)
            n = xv * rs
            dy = dh * (1.0 + m[1:2, :])
            dn = dy * pre
            gx_ref[...] = dx1_ref[...] + rs * (dn - n * _rowmean(dn * n))
            vacc_ref[...] += jnp.concatenate([_colsum(dy * n), jnp.zeros((7, D_MODEL), F32)], axis=0)
            dmod_ref[0] += jnp.concatenate([_colsum(dh), _colsum(dh * (n * pre)), jnp.zeros((6, D_MODEL), F32)], axis=0)

    row = lambda w: pl.BlockSpec((tm, w), lambda b, s, k: (b * per_seq + s, 0))
    return _call(
        body, name="in_proj_bwd", grid=(nb, per_seq, nk), side=side, sem=("arbitrary", "arbitrary", "arbitrary"),
        args=(*pieces, ddt, dx1, x2, mod8, pre1, w_main, w_dt),
        in_specs=[piece_spec(p) for p in range(_NP)] + [
            row(128), row(D_MODEL), row(D_MODEL), pl.BlockSpec((1, 8, D_MODEL), lambda b, s, k: (b, 0, 0)),
            pl.BlockSpec((1, D_MODEL), lambda b, s, k: (0, 0)),
            VMEM_FULL,
            pl.BlockSpec((D_MODEL, 128), lambda b, s, k: (0, 0))],
        out_specs=[row(D_MODEL), pl.BlockSpec((8, D_MODEL), lambda b, s, k: (0, 0)),
                   pl.BlockSpec((1, 8, D_MODEL), lambda b, s, k: (b, 0, 0))],
        out_shape=[jax.ShapeDtypeStruct((t, D_MODEL), F32), jax.ShapeDtypeStruct((8, D_MODEL), F32),
                   jax.ShapeDtypeStruct((nb, 8, D_MODEL), F32)],
        scratch_shapes=[pltpu.VMEM((tm, D_MODEL), F32)])


def in_proj_wgrad(h1t, proj, dxa, dxbc, dlg, dz, dgates, ddt, cw, cb, seq, side=None):
    nt, _, tt = h1t.shape
    t = nt * tt
    tn = 1024
    nn = PROJ_MAIN // tn
    ns = seq // tt
    nh = t // HALO
    j_g, j_z, j_x, j_gt = C_LRU_G // tn, C_Z // tn, C_XBC // tn, C_GATES // tn
    n_x = 2 * SSD_INNER // tn
    strip = 256
    ne = tt + HALO

    def body(h_ref, cur_ref, prev_ref, next_ref, dxa_ref, dxan_ref, dxb_ref, dxbn_ref, dlg_ref, dz_ref, dgt_ref, ddt_ref,
             cw_ref, cb_ref, dw_ref, dwdt_ref, dlx_ref, dxr_ref, accl_ref, accs_ref, acc_ref, accdt_ref):
        n, k = pl.program_id(0), pl.program_id(1)
        hv = h_ref[k]
        is_x = (n >= j_x) & (n < j_x + n_x)

        @pl.when(k == 0)
        def _():
            acc_ref[...] = jnp.zeros_like(acc_ref)

        @pl.when((n == 0) & (k == 0))
        def _():
            accdt_ref[...] = jnp.zeros_like(accdt_ref)
            accl_ref[...] = jnp.zeros_like(accl_ref)

        @pl.when(is_x & (k == 0))
        def _():
            accs_ref[...] = jnp.zeros_like(accs_ref)

        def conv_tile(do_ref, don_ref, out_ref, cacc_ref, act):
            first = lax.rem(k, ns) == 0
            last = lax.rem(k, ns) == ns - 1
            for c0 in range(0, tn, strip):
                cs = slice(c0, c0 + strip)
                xx = jnp.concatenate([jnp.where(first, 0.0, prev_ref[:, cs].astype(F32)), cur_ref[:, cs].astype(F32),
                                      next_ref[:, cs].astype(F32)], axis=0)
                do_ext = jnp.concatenate([do_ref[:, cs].astype(F32),
                                          jnp.where(last, 0.0, don_ref[:, cs].astype(F32))], axis=0)
                w = cw_ref[:, cs]
                xs = [xx[HALO:HALO + ne, :]] + [pltpu.roll(xx, d, axis=0)[HALO:HALO + ne, :] for d in (1, 2, 3)]
                if act:
                    c = cb_ref[:, cs] + xs[0] * w[3:4, :] + xs[1] * w[2:3, :] + xs[2] * w[1:2, :] + xs[3] * w[0:1, :]
                    sg = _sigmoid(c)
                    dc = do_ext * (sg * (1.0 + c * (1.0 - sg)))
                else:
                    dc = do_ext
                dx = dc[:tt, :] * w[3:4, :]
                for d in (1, 2, 3):
                    dx = dx + pltpu.roll(dc, ne - d, axis=0)[:tt, :] * w[3 - d:4 - d, :]
                dxb = dx.astype(BF16)
                out_ref[:, cs] = dxb
                acc_ref[:, cs] += _dot(hv, dxb)
                dcc = dc[:tt, :]
                rows = [_colsum(dcc * xs[3 - r][:tt, :]) for r in range(4)] + [_colsum(dcc)]
                cacc_ref[:, cs] += jnp.concatenate(rows + [jnp.zeros((3, strip), F32)], axis=0)

        @pl.when(n == 0)
        def _():
            conv_tile(dxa_ref, dxan_ref, dlx_ref, accl_ref, False)
            accdt_ref[...] += _dot(hv, ddt_ref[...])

        @pl.when(is_x)
        def _():
            conv_tile(dxb_ref, dxbn_ref, dxr_ref, accs_ref, True)

        @pl.when(n == j_g)
        def _():
            acc_ref[...] += _dot(hv, dlg_ref[...])

        @pl.when((n >= j_z) & (n < j_x))
        def _():
            acc_ref[...] += _dot(hv, dz_ref[...])

        @pl.when(n >= j_gt)
        def _():
            acc_ref[...] += _dot(hv, dgt_ref[...])

        @pl.when(k == nt - 1)
        def _():
            dw_ref[...] = acc_ref[...].astype(BF16)

        @pl.when((n == 0) & (k == nt - 1))
        def _():
            dwdt_ref[...] = accdt_ref[...].astype(BF16)

    conv_n = lambda n: (n == 0) | ((n >= j_x) & (n < j_x + n_x))
    src_col = lambda n: jnp.where(n == 0, 0, jnp.clip(n, j_x, j_x + n_x - 1))
    ctile = lambda n: jnp.where(n == 0, 0, jnp.clip(n - j_x + 1, 1, n_x))
    xcol = lambda n: jnp.clip(n - j_x, 0, n_x - 1)
    on = lambda cond, k: jnp.where(cond, k, 0)
    nxt = lambda k: jnp.minimum(((k + 1) * tt) // HALO, nh - 1)
    after = lambda cond_during, cond_after, k: jnp.where(cond_during, k, jnp.where(cond_after, nt - 1, 0))
    in_specs = [
        VMEM_FULL,
        pl.BlockSpec((tt, tn), lambda n, k: (on(conv_n(n), k), src_col(n))),
        pl.BlockSpec((HALO, tn), lambda n, k: (on(conv_n(n), jnp.maximum((k * tt) // HALO - 1, 0)), src_col(n))),
        pl.BlockSpec((HALO, tn), lambda n, k: (on(conv_n(n), nxt(k)), src_col(n))),
        pl.BlockSpec((tt, tn), lambda n, k: (on(n == 0, k), 0)),
        pl.BlockSpec((HALO, tn), lambda n, k: (on(n == 0, nxt(k)), 0)),
        pl.BlockSpec((tt, tn), lambda n, k: (on((n >= j_x) & (n < j_x + n_x), k), xcol(n))),
        pl.BlockSpec((HALO, tn), lambda n, k: (on((n >= j_x) & (n < j_x + n_x), nxt(k)), xcol(n))),
        pl.BlockSpec((tt, tn), lambda n, k: (on(n == j_g, k), 0)),
        pl.BlockSpec((tt, tn), lambda n, k: (on((n >= j_z) & (n < j_x), k), jnp.clip(n - j_z, 0, j_x - j_z - 1))),
        pl.BlockSpec((tt, tn), lambda n, k: (on(n >= j_gt, k), jnp.clip(n - j_gt, 0, nn - j_gt - 1))),
        pl.BlockSpec((tt, 128), lambda n, k: (on(n == 0, k), 0)),
        pl.BlockSpec((4, tn), lambda n, k: (0, ctile(n))),
        pl.BlockSpec((1, tn), lambda n, k: (0, ctile(n)))]
    out_specs = [
        pl.BlockSpec((D_MODEL, tn), lambda n, k: (0, n)),
        pl.BlockSpec((D_MODEL, 128), lambda n, k: (0, 0)),
        pl.BlockSpec((tt, tn), lambda n, k: (after(n == 0, n > 0, k), 0)),
        pl.BlockSpec((tt, tn), lambda n, k: (after((n >= j_x) & (n < j_x + n_x), n >= j_x + n_x, k), xcol(n))),
        pl.BlockSpec((8, tn), lambda n, k: (0, 0)),
        pl.BlockSpec((8, tn), lambda n, k: (0, xcol(n)))]
    return _call(
        body, name="in_proj_wgrad", grid=(nn, nt), side=side, sem=("arbitrary", "arbitrary"),
        args=(h1t, proj, proj, proj, dxa, dxa, dxbc, dxbc, dlg, dz, dgates, ddt, cw, cb),
        in_specs=in_specs, out_specs=out_specs,
        out_shape=[jax.ShapeDtypeStruct((D_MODEL, PROJ_MAIN), BF16), jax.ShapeDtypeStruct((D_MODEL, 128), BF16),
                   jax.ShapeDtypeStruct((t, D_MODEL), BF16), jax.ShapeDtypeStruct((t, 2 * SSD_INNER), BF16),
                   jax.ShapeDtypeStruct((8, D_MODEL), F32), jax.ShapeDtypeStruct((8, 2 * SSD_INNER), F32)],
        scratch_shapes=[pltpu.VMEM((D_MODEL, tn), F32), pltpu.VMEM((D_MODEL, 128), F32)])


def _log1p(u):
    w = 1.0 + u
    return jnp.log(w) - ((w - 1.0) - u) / w


def _softplus(x):
    return jnp.maximum(x, 0.0) + _log1p(jnp.exp(-jnp.abs(x)))


def _head_mask(h):
    lane = lax.broadcasted_iota(jnp.int32, (1, SSD_GW), 1)
    return (lane >= SSD_P * h) & (lane < SSD_P * (h + 1))


def _pair(p):
    return slice(2 * SSD_P * p, 2 * SSD_P * (p + 1))


def _expand4(m, g):
    lane = lax.broadcasted_iota(jnp.int32, (1, SSD_GW), 1)
    col = lambda h: m[:, 4 * g + h:4 * g + h + 1]
    return jnp.where(lane < SSD_P, col(0), jnp.where(lane < 2 * SSD_P, col(1), jnp.where(lane < 3 * SSD_P, col(2), col(3))))


def _reduce4(v, g):
    lane = lax.broadcasted_iota(jnp.int32, (1, SSD_N), 1)
    out = jnp.zeros((v.shape[0], SSD_N), F32)
    for h in range(4):
        s = jnp.sum(jnp.where(_head_mask(h), v, 0.0), axis=1, keepdims=True)
        out = out + jnp.where(lane == 4 * g + h, s, 0.0)
    return out


def _ssd_heads(dtraw, hp, tri):
    xdt = dtraw + hp[0:1, :]
    dt = _softplus(xdt)
    cs = _dot_hi(tri, dt * hp[1:2, :])
    cs_last = cs[SSD_L - 1:SSD_L, :]
    return dict(xdt=xdt, dt=dt, cs=cs, cs_t=cs.T, e=jnp.exp(cs), w=jnp.exp(cs_last - cs), el=jnp.exp(cs_last))


def _ssd_group(g, hd, xs_b, bm_b, cm_b, d_x, st, paired=False):
    ll = SSD_L
    xs = xs_b.astype(F32)
    cs, cs_t = hd["cs"], hd["cs_t"]
    e_x, w_x, el_x, dt_x = _expand4(hd["e"], g), _expand4(hd["w"], g), _expand4(hd["el"], g), _expand4(hd["dt"], g)
    xd = xs * dt_x
    gcb = _dot_nt(cm_b, bm_b)
    ri = lax.broadcasted_iota(jnp.int32, (ll, ll), 0)
    ci = lax.broadcasted_iota(jnp.int32, (ll, ll), 1)
    dks, ms = [], []
    for h in range(4):
        k = 4 * g + h
        dk = jnp.exp(jnp.where(ri >= ci, cs[:, k:k + 1] - cs_t[k:k + 1, :], -1e30))
        dks.append(dk)
        ms.append((gcb * dk).astype(BF16))
    xdb = xd.astype(BF16)
    if paired:
        first = lax.broadcasted_iota(jnp.int32, (1, 2 * SSD_P), 1) < SSD_P
        ydiag = jnp.concatenate(
            [jnp.where(first, _dot(ms[2 * p], xdb[:, _pair(p)]), _dot(ms[2 * p + 1], xdb[:, _pair(p)]))
             for p in range(2)], axis=1)
    else:
        ydiag = jnp.zeros((ll, SSD_GW), F32)
        for h in range(4):
            ydiag = ydiag + _dot(ms[h], jnp.where(_head_mask(h), xd, 0.0).astype(BF16))
    yoff = _dot(cm_b, st.astype(BF16)) * e_x
    y = ydiag + yoff + d_x * xs
    st_new = st * el_x + _dot(bm_b.astype(F32).T.astype(BF16), (xd * w_x).astype(BF16))
    return dict(xs=xs, e_x=e_x, w_x=w_x, el_x=el_x, dt_x=dt_x, xd=xd, xdb=xdb, gcb=gcb, dks=dks, ms=ms, yoff=yoff, y=y,
                st_new=st_new)


def ssd_consts():
    hh = np.arange(SSD_N)
    tri = (hh[:, None] >= hh[None, :]).astype(np.float32)
    return jnp.asarray(tri), jnp.asarray(tri.T)


def ssd_params(dt_bias, a_log, d_skip, norm_w):
    padh = lambda v: jnp.pad(v.reshape(1, SSD_HEADS), ((0, 0), (0, SSD_N - SSD_HEADS)))
    hp = jnp.concatenate([padh(dt_bias), padh(-jnp.exp(a_log)), jnp.zeros((6, SSD_N), F32)], axis=0)
    lp = jnp.concatenate([norm_w.reshape(1, SSD_INNER), jnp.repeat(d_skip, SSD_P).reshape(1, SSD_INNER),
                          jnp.zeros((6, SSD_INNER), F32)], axis=0)
    return hp, lp


def _b_cols(g):
    return slice(SSD_INNER + g * SSD_N, SSD_INNER + (g + 1) * SSD_N)


def _c_cols(g):
    return slice(SSD_INNER + (SSD_G + g) * SSD_N, SSD_INNER + (SSD_G + g + 1) * SSD_N)


def _ssd_specs(nc, rc):
    return [pl.BlockSpec((SSD_L, 2 * SSD_INNER), lambda b, c: (b * nc + rc(c), 0)),
            pl.BlockSpec((SSD_L, SSD_INNER), lambda b, c: (b * nc + rc(c), C_Z // SSD_INNER)),
            pl.BlockSpec((SSD_L, SSD_N), lambda b, c: (b * nc + rc(c), 0))]


def ssd_fwd(xbc, proj, dtraw, hp, lp, tri, nb, seq):
    t = xbc.shape[0]
    nc = seq // SSD_L

    def body(xbc_ref, z_ref, dt_ref, hp_ref, lp_ref, tri_ref, y_ref, sts_ref, st_scr):
        @pl.when(pl.program_id(1) == 0)
        def _():
            st_scr[...] = jnp.zeros_like(st_scr)

        hd = _ssd_heads(dt_ref[...], hp_ref[...], tri_ref[...])
        for g in range(SSD_G):
            gs = slice(g * SSD_GW, (g + 1) * SSD_GW)
            st = st_scr[g]
            sts_ref[0, g] = st
            f = _ssd_group(g, hd, xbc_ref[:, gs], xbc_ref[:, _b_cols(g)], xbc_ref[:, _c_cols(g)], lp_ref[1:2, gs], st,
                           paired=True)
            st_scr[g] = f["st_new"]
            zf = z_ref[:, gs].astype(F32)
            yg = f["y"] * (zf * _sigmoid(zf))
            y_ref[:, gs] = (yg * lax.rsqrt(_rowmean(yg * yg) + EPS) * lp_ref[0:1, gs]).astype(BF16)

    return pl.pallas_call(
        body, name="ssd_fwd", grid=(nb, nc),
        in_specs=_ssd_specs(nc, lambda c: c) + [VMEM_FULL, VMEM_FULL, VMEM_FULL],
        out_specs=[pl.BlockSpec((SSD_L, SSD_INNER), lambda b, c: (b * nc + c, 0)),
                   pl.BlockSpec((1, SSD_G, SSD_N, SSD_GW), lambda b, c: (b * nc + c, 0, 0, 0))],
        out_shape=[jax.ShapeDtypeStruct((t, SSD_INNER), BF16),
                   jax.ShapeDtypeStruct((nb * nc, SSD_G, SSD_N, SSD_GW), F32)],
        scratch_shapes=[pltpu.VMEM((SSD_G, SSD_N, SSD_GW), F32)],
        compiler_params=_cp("arbitrary", "arbitrary"),
    )(xbc, proj, dtraw, hp, lp, tri)


def ssd_bwd(xbc, proj, dtraw, hp, lp, tri, triu, states, dyn, nb, seq, side=None):
    t = xbc.shape[0]
    nc = seq // SSD_L
    ll = SSD_L

    def body(xbc_ref, z_ref, dt_ref, sts_ref, dy_ref, hp_ref, lp_ref, tri_ref, triu_ref,
             dxbc_ref, dz_ref, ddt_ref, hpg_ref, lpg_ref, dst_scr):
        b, c_i = pl.program_id(0), pl.program_id(1)

        @pl.when((b == 0) & (c_i == 0))
        def _():
            hpg_ref[...] = jnp.zeros_like(hpg_ref)
            lpg_ref[...] = jnp.zeros_like(lpg_ref)

        @pl.when(c_i == 0)
        def _():
            dst_scr[...] = jnp.zeros_like(dst_scr)

        hp = hp_ref[...]
        hd = _ssd_heads(dt_ref[...], hp, tri_ref[...])
        lane = lax.broadcasted_iota(jnp.int32, (1, SSD_N), 1)
        subl = lax.broadcasted_iota(jnp.int32, (SSD_N, 1), 0)
        dcs = jnp.zeros((ll, SSD_N), F32)
        dcs_t = jnp.zeros((SSD_N, ll), F32)
        last = jnp.zeros((1, SSD_N), F32)
        dxx = jnp.zeros((ll, SSD_N), F32)
        for g in range(SSD_G):
            gs = slice(g * SSD_GW, (g + 1) * SSD_GW)
            st = sts_ref[0, g]
            dst = dst_scr[g]
            bm_b, cm_b = xbc_ref[:, _b_cols(g)], xbc_ref[:, _c_cols(g)]
            d_x = lp_ref[1:2, gs]
            f = _ssd_group(g, hd, xbc_ref[:, gs], bm_b, cm_b, d_x, st)
            xs, xd, gcb = f["xs"], f["xd"], f["gcb"]
            e_x, w_x, el_x, dt_x = f["e_x"], f["w_x"], f["el_x"], f["dt_x"]
            stb, dstb = st.astype(BF16), dst.astype(BF16)
            zf = z_ref[:, gs].astype(F32)
            sg = _sigmoid(zf)
            sz = zf * sg
            yg = f["y"] * sz
            rstd = lax.rsqrt(_rowmean(yg * yg) + EPS)
            n = yg * rstd
            dyn_v = dy_ref[:, gs].astype(F32)
            dn = dyn_v * lp_ref[0:1, gs]
            dyg = rstd * (dn - n * _rowmean(dn * n))
            dy = dyg * sz
            dz_ref[:, gs] = (dyg * f["y"] * (sg * (1.0 + zf * (1.0 - sg)))).astype(BF16)
            dyb = dy.astype(BF16)
            r_ = _dot(bm_b, dstb)
            dxd = w_x * r_
            dqb = (dy * e_x).astype(BF16)
            dcm = _dot_nt(dqb, stb)
            dst_scr[g] = dst * el_x + _dot_tn(cm_b, dqb)
            dbm = _dot_nt((xd * w_x).astype(BF16), dstb)
            xdb = f["xdb"]
            dgm = jnp.zeros((ll, ll), F32)
            for h in range(4):
                k = 4 * g + h
                hm = _head_mask(h)
                dxd = dxd + jnp.where(hm, _dot_tn(f["ms"][h], dyb), 0.0)
                dm = _dot_nt(jnp.where(hm, dy, 0.0).astype(BF16), xdb) * f["dks"][h]
                dgm = dgm + dm
                dseg = dm * gcb
                dcs = dcs + jnp.where(lane == k, jnp.sum(dseg, axis=1, keepdims=True), 0.0)
                dcs_t = dcs_t + jnp.where(subl == k, jnp.sum(dseg, axis=0, keepdims=True), 0.0)
            dgmb = dgm.astype(BF16)
            dxbc_ref[:, _c_cols(g)] = (dcm + _dot(dgmb, bm_b)).astype(BF16)
            dxbc_ref[:, _b_cols(g)] = (dbm + _dot_tn(dgmb, cm_b)).astype(BF16)
            v = _reduce4(r_ * xd * w_x, g)
            dcs = dcs + _reduce4(dy * f["yoff"], g) - v
            last = last + _colsum(v) + _reduce4(_colsum(dst * st) * el_x, g)
            dxx = dxx + _reduce4(dxd * xs, g)
            dxbc_ref[:, gs] = (d_x * dy + dxd * dt_x).astype(BF16)
            lpg_ref[0:1, gs] += _colsum(dyn_v * n)
            lpg_ref[1:2, gs] += _colsum(dy * xs)
        rowi = lax.broadcasted_iota(jnp.int32, (ll, 1), 0)
        da = _dot_hi(triu_ref[...], dcs - dcs_t.T + jnp.where(rowi == ll - 1, last, 0.0))
        ddt = (dxx + da * hp[1:2, :]) * _sigmoid(hd["xdt"])
        ddt_ref[...] = ddt
        hpg_ref[...] += jnp.concatenate([_colsum(ddt), _colsum(da * hd["dt"]), jnp.zeros((6, SSD_N), F32)], axis=0)

    rc = lambda c: nc - 1 - c
    return _call(
        body, name="ssd_bwd", grid=(nb, nc), side=side, sem=("arbitrary", "arbitrary"),
        args=(xbc, proj, dtraw, states, dyn, hp, lp, tri, triu),
        in_specs=_ssd_specs(nc, rc) + [
            pl.BlockSpec((1, SSD_G, SSD_N, SSD_GW), lambda b, c: (b * nc + rc(c), 0, 0, 0)),
            pl.BlockSpec((SSD_L, SSD_INNER), lambda b, c: (b * nc + rc(c), 0)),
            VMEM_FULL, VMEM_FULL, VMEM_FULL, VMEM_FULL],
        out_specs=[pl.BlockSpec((SSD_L, 2 * SSD_INNER), lambda b, c: (b * nc + rc(c), 0)),
                   pl.BlockSpec((SSD_L, SSD_INNER), lambda b, c: (b * nc + rc(c), 0)),
                   pl.BlockSpec((SSD_L, SSD_N), lambda b, c: (b * nc + rc(c), 0)),
                   pl.BlockSpec((8, SSD_N), lambda b, c: (0, 0)),
                   pl.BlockSpec((8, SSD_INNER), lambda b, c: (0, 0))],
        out_shape=[jax.ShapeDtypeStruct((t, 2 * SSD_INNER), BF16), jax.ShapeDtypeStruct((t, SSD_INNER), BF16),
                   jax.ShapeDtypeStruct((t, SSD_N), F32), jax.ShapeDtypeStruct((8, SSD_N), F32),
                   jax.ShapeDtypeStruct((8, SSD_INNER), F32)],
        scratch_shapes=[pltpu.VMEM((SSD_G, SSD_N, SSD_GW), F32)])


def ada_fwd(c_all, w_cols, b_cols):
    def body(c_ref, w_ref, b_ref, o_ref):
        cv = c_ref[...]
        o_ref[...] = _dot_hi(cv * _sigmoid(cv), w_ref[...]) + b_ref[...]

    return pl.pallas_call(body, name="ada_fwd", out_shape=jax.ShapeDtypeStruct((c_all.shape[0], w_cols.shape[1]), F32),
                          compiler_params=pltpu.CompilerParams(vmem_limit_bytes=VMEM_LIMIT))(c_all, w_cols, b_cols)


def ada_bwd(c_all, dmod_cols, dmod_all):
    def body(c_ref, dc_ref, da_ref, gw_ref, gb_ref):
        cv = c_ref[...]
        gw_ref[...] = lax.dot_general(cv * _sigmoid(cv), dc_ref[...], (((0,), (0,)), ((), ())),
                                      precision=lax.Precision.HIGHEST, preferred_element_type=F32)
        gb_ref[...] = _colsum(da_ref[...])

    return pl.pallas_call(
        body, name="ada_bwd",
        out_shape=[jax.ShapeDtypeStruct((c_all.shape[1], dmod_cols.shape[1]), F32),
                   jax.ShapeDtypeStruct((1, dmod_all.shape[1]), F32)],
        compiler_params=pltpu.CompilerParams(vmem_limit_bytes=VMEM_LIMIT))(c_all, dmod_cols, dmod_all)


def _adam_update(g, w, m, v):
    m2 = ADAM_B1 * m + (1.0 - ADAM_B1) * g
    v2 = ADAM_B2 * v + (1.0 - ADAM_B2) * (g * g)
    m_hat = m2 / (1.0 - ADAM_B1 ** ADAM_STEP)
    v_hat = v2 / (1.0 - ADAM_B2 ** ADAM_STEP)
    return -ADAM_LR * (m_hat / (jnp.sqrt(v_hat) + ADAM_EPS) + ADAM_WD * w), m2, v2


def adamw(parts, w, m, v, name):
    n, r, c = parts.shape
    tr = r if r <= 256 else 128

    def body(p_ref, w_ref, m_ref, v_ref, g_ref, d_ref, nm_ref, nv_ref):
        g = p_ref[0].astype(F32)
        for s in range(1, n):
            g = g + p_ref[s].astype(F32)
        g_ref[0] = g
        d_ref[0], nm_ref[0], nv_ref[0] = _adam_update(g, w_ref[0], m_ref[0], v_ref[0])

    blk = pl.BlockSpec((1, tr, c), lambda i: (0, i, 0))
    return pl.pallas_call(
        body, name=name, grid=(r // tr,),
        in_specs=[pl.BlockSpec((n, tr, c), lambda i: (0, i, 0)), blk, blk, blk], out_specs=[blk] * 4,
        out_shape=[jax.ShapeDtypeStruct((1, r, c), F32)] * 4,
        compiler_params=_cp("parallel"),
    )(parts, w, m, v)


SMALL_SRC = {
    'pre_norm1': ('vin', 0, 1024), 'post_norm1': ('vmg', 1, 1024), 'b_gate': ('vmg', 0, 2048),
    'lru_conv_b': ('accl', 4, 1024), 'lru_wa': ('gwa', None, None), 'lru_ba': ('dvec', 0, 1024),
    'lru_wx': ('gwx', None, None), 'lru_bx': ('dvec', 1, 1024), 'lru_lambda': ('dvec', 2, 1024),
    'ssd_conv_b': ('accs', 4, 4096), 'ssd_dt_bias': ('hpg', 0, SSD_HEADS), 'ssd_a_log': ('hpg', 1, SSD_HEADS),
    'ssd_d': ('lpg', 1, SSD_INNER), 'ssd_norm_w': ('lpg', 0, SSD_INNER), 'pre_norm2': ('vmlp', 0, 1024),
    'post_norm2': ('vmlp', 1, 1024)}
SMALL_ACCS = ('vin', 'vmg', 'vmlp', 'dvec', 'accl', 'accs', 'hpg', 'lpg', 'gwa', 'gwx')
SMALL_RIDE = ('vmg', 'vmlp', 'dvec', 'hpg', 'lpg', 'gwa', 'gwx')


def adamw_small(gathered, params):
    names = tuple(params)
    na = len(SMALL_ACCS)

    def body(*refs):
        acc = {k: functools.reduce(lambda p, q: p + q, [refs[i][s] for s in range(NDEV)])
               for i, k in enumerate(SMALL_ACCS)}
        ins = refs[na:na + 3 * len(names)]
        outs = refs[na + 3 * len(names):]
        for j, k in enumerate(names):
            w_ref, m_ref, v_ref = ins[3 * j:3 * j + 3]
            src, row, width = SMALL_SRC[k]
            wv = w_ref[...]
            if row is None:
                g = acc[src]
            elif k == 'ssd_d':
                li = lax.broadcasted_iota(jnp.int32, (SSD_INNER, SSD_N), 0)
                hi = lax.broadcasted_iota(jnp.int32, (SSD_INNER, SSD_N), 1)
                g = _dot_hi(acc[src], jnp.where(jnp.right_shift(li, 6) == hi, 1.0, 0.0))[row:row + 1, :SSD_HEADS]
            else:
                g = acc[src][row:row + 1, :width]
            if k == 'lru_lambda':
                g = g * (-1.0 / (1.0 + jnp.exp(wv)))
            if k == 'ssd_a_log':
                g = g * (-jnp.exp(wv))
            o = outs[4 * j:4 * j + 4]
            o[0][...] = g
            o[1][...], o[2][...], o[3][...] = _adam_update(g, wv, m_ref[...], v_ref[...])
        outs[-2][...] = acc['accl'][0:4, :]
        outs[-1][...] = acc['accs'][0:4, :]

    flat = [a for k in names for a in params[k]]
    out_shape = [jax.ShapeDtypeStruct(params[k][0].shape, F32) for k in names for _ in range(4)]
    out_shape += [jax.ShapeDtypeStruct((4, D_MODEL), F32), jax.ShapeDtypeStruct((4, 2 * SSD_INNER), F32)]
    res = pl.pallas_call(body, name="adamw_small", out_shape=out_shape,
                         compiler_params=pltpu.CompilerParams(vmem_limit_bytes=VMEM_LIMIT))(
        *[gathered[k] for k in SMALL_ACCS], *flat)
    return {k: res[4 * j:4 * j + 4] for j, k in enumerate(names)}, res[-2], res[-1]


def _dev_index(px, py, pc):
    return 4 * px + 2 * py + pc


class _Exchange:
    def __init__(self, arrs):
        self.arrs = list(arrs)
        self.na = len(self.arrs)
        self.scratch = [pltpu.SemaphoreType.DMA((7 * self.na,)), pltpu.SemaphoreType.DMA((7 * self.na,)),
                        pltpu.SemaphoreType.DMA((self.na,))]


class Gather(_Exchange):
    def __init__(self, arrs):
        super().__init__(arrs)
        self.out_shape = [jax.ShapeDtypeStruct((NDEV,) + a.shape, a.dtype) for a in self.arrs]

    def _plan(self, ins, outs, sems):
        na = self.na
        send_sems, recv_sems, local_sems = sems
        x, y, c = lax.axis_index("x"), lax.axis_index("y"), lax.axis_index("c")
        me, sibling = (x, y, c), (x, y, 1 - c)
        chips = [(1 - x, y), (x, 1 - y), (1 - x, 1 - y)]

        def copy(a, k, block, to, src=None):
            dst = outs[a].at[_dev_index(*block)]
            return pltpu.make_async_remote_copy(
                src_ref=dst if src is None else src, dst_ref=dst, send_sem=send_sems.at[a * 7 + k],
                recv_sem=recv_sems.at[a * 7 + k], device_id=to, device_id_type=MESH)

        mine = [pltpu.make_async_copy(ins[a], outs[a].at[_dev_index(*me)], local_sems.at[a]) for a in range(na)]
        first = []
        for a in range(na):
            first.append(copy(a, 0, me, sibling, src=ins[a]))
            first += [copy(a, 1 + j, me, (*chip, c), src=ins[a]) for j, chip in enumerate(chips)]
        return copy, mine, first, me, sibling, chips, c

    def start(self, ins, outs, sems):
        _, mine, first, *_ = self._plan(ins, outs, sems)
        for cp in mine + first:
            cp.start()

    def finish(self, ins, outs, sems):
        copy, mine, first, me, sibling, chips, c = self._plan(ins, outs, sems)
        passed = []
        for j, chip in enumerate(chips):
            for a in range(self.na):
                copy(a, 1 + j, (*chip, c), me).wait_recv()
                cp = copy(a, 4 + j, (*chip, c), sibling)
                cp.start()
                passed.append(cp)
        for a in range(self.na):
            copy(a, 0, sibling, me).wait_recv()
            for j, chip in enumerate(chips):
                copy(a, 4 + j, (*chip, 1 - c), me).wait_recv()
        for cp in first + passed:
            cp.wait_send()
        for cp in mine:
            cp.wait()


class Scatter(_Exchange):
    def __init__(self, arrs):
        super().__init__(arrs)
        self.out_shape = [jax.ShapeDtypeStruct(a.shape, a.dtype) for a in self.arrs]

    def _plan(self, ins, outs, sems, arrivals):
        send_sems, recv_sems, local_sems = sems
        x, y, c = lax.axis_index("x"), lax.axis_index("y"), lax.axis_index("c")
        me = _dev_index(x, y, c)
        masks = [(mx, my, mc) for mx in (0, 1) for my in (0, 1) for mc in (0, 1)][1:]
        flip = lambda v, bit: 1 - v if bit else v
        mine = [pltpu.make_async_copy(ins[a].at[me], outs[a].at[me], local_sems.at[a]) for a in range(self.na)]
        sends, recvs = [], []
        for k, (mx, my, mc) in enumerate(masks):
            peer = (flip(x, mx), flip(y, my), flip(c, mc))
            pidx = _dev_index(*peer)
            for a in range(self.na):
                on = dict(send_sem=send_sems.at[a * 7 + k], recv_sem=recv_sems.at[a * 7 + k], device_id=peer,
                          device_id_type=MESH)
                sends.append(pltpu.make_async_remote_copy(src_ref=ins[a].at[pidx], dst_ref=outs[a].at[me], **on))
                if arrivals:
                    recvs.append(pltpu.make_async_remote_copy(src_ref=ins[a].at[pidx], dst_ref=outs[a].at[pidx], **on))
        return mine, sends, recvs

    def start(self, ins, outs, sems):
        mine, sends, _ = self._plan(ins, outs, sems, arrivals=False)
        for cp in mine + sends:
            cp.start()

    def finish(self, ins, outs, sems):
        mine, sends, recvs = self._plan(ins, outs, sems, arrivals=True)
        for cp in recvs:
            cp.wait_recv()
        for cp in sends:
            cp.wait_send()
        for cp in mine:
            cp.wait()


def exchange_call(ex, name):
    na = ex.na

    def body(*refs):
        ins, outs, sems = refs[:na], refs[na:2 * na], refs[2 * na:]
        ex.start(ins, outs, sems)
        ex.finish(ins, outs, sems)

    return pl.pallas_call(body, name=name, in_specs=[ANY] * na, out_specs=[ANY] * na, out_shape=ex.out_shape,
                          scratch_shapes=ex.scratch)(*ex.arrs)


def all_gather(arrs, name):
    return exchange_call(Gather(arrs), name)


def _call(body, *, name, grid, in_specs, out_specs, out_shape, scratch_shapes=(), sem, args, side=None):
    if side is None:
        outs = pl.pallas_call(body, name=name, grid=grid, in_specs=list(in_specs), out_specs=list(out_specs),
                              out_shape=list(out_shape), scratch_shapes=list(scratch_shapes),
                              compiler_params=_cp(*sem))(*args)
        return outs, []
    ni, no, ns, na = len(in_specs), len(out_specs), len(scratch_shapes), side.na

    def wrapped(*refs):
        ins, s_in = refs[:ni], refs[ni:ni + na]
        outs, s_out = refs[ni + na:ni + na + no], refs[ni + na + no:ni + 2 * na + no]
        scr, sems = refs[ni + 2 * na + no:ni + 2 * na + no + ns], refs[ni + 2 * na + no + ns:]
        pids = [pl.program_id(i) for i in range(len(grid))]
        first = functools.reduce(lambda p, q: p & q, [p == 0 for p in pids])
        last = functools.reduce(lambda p, q: p & q, [p == g - 1 for p, g in zip(pids, grid)])

        @pl.when(first)
        def _():
            side.start(s_in, s_out, sems)

        body(*ins, *outs, *scr)

        @pl.when(last)
        def _():
            side.finish(s_in, s_out, sems)

    outs = pl.pallas_call(
        wrapped, name=name, grid=grid, in_specs=list(in_specs) + [ANY] * na, out_specs=list(out_specs) + [ANY] * na,
        out_shape=list(out_shape) + side.out_shape, scratch_shapes=list(scratch_shapes) + side.scratch,
        compiler_params=_cp(*["arbitrary"] * len(grid)))(*args, *side.arrs)
    return outs[:no], outs[no:]


WEIGHTS = ('w_ada', 'b_ada', 'pre_norm1', 'post_norm1', 'w_in', 'b_gate', 'lru_conv_w', 'lru_conv_b', 'lru_wa',
           'lru_ba', 'lru_wx', 'lru_bx', 'lru_lambda', 'w_pa', 'ssd_conv_w', 'ssd_conv_b', 'ssd_dt_bias', 'ssd_a_log',
           'ssd_d', 'ssd_norm_w', 'w_pb', 'w_out', 'pre_norm2', 'post_norm2', 'w_ff1', 'w_ff2')
BIG = ('w_in', 'w_pa', 'w_pb', 'w_out', 'w_ff1', 'w_ff2')
REPL = ('pre_norm1', 'post_norm1', 'b_gate', 'lru_conv_b', 'lru_wa', 'lru_ba', 'lru_wx', 'lru_bx', 'lru_lambda',
        'ssd_conv_b', 'ssd_dt_bias', 'ssd_a_log', 'ssd_d', 'ssd_norm_w', 'pre_norm2', 'post_norm2')
LANES = 1024


def _rows(n):
    return -(-n // LANES)


def _pack(vals, total_rows):
    parts = []
    for v in vals:
        f = v.reshape(-1).astype(F32)
        parts.append(jnp.pad(f, (0, _rows(f.shape[0]) * LANES - f.shape[0])))
    flat = jnp.concatenate(parts)
    return jnp.pad(flat.reshape(-1, LANES), ((0, total_rows - flat.shape[0] // LANES), (0, 0)))


def _unpack(slab, shapes):
    out, r = [], 0
    for s in shapes:
        n = int(np.prod(s))
        out.append(slab[r:r + _rows(n)].reshape(-1)[:n].reshape(s))
        r += _rows(n)
    return out


def _block_diag4(w):
    w4 = w.reshape(4, 4, 64, 64)
    eye = jnp.eye(4, dtype=w.dtype)
    return (w4[:, :, :, None, :] * eye[None, :, None, :, None]).reshape(4, LRU_BLOCK, LRU_BLOCK)


def _diag_blocks4(m):
    m5 = m.reshape(4, 4, 64, 4, 64)
    return jnp.stack([m5[:, a, :, a, :] for a in range(4)], axis=1).reshape(LRU_HEADS, 64, 64)


def kernel(x, c, w_ada, b_ada, pre_norm1, post_norm1, w_in, b_gate, lru_conv_w, lru_conv_b, lru_wa, lru_ba, lru_wx, lru_bx, lru_lambda, w_pa, ssd_conv_w, ssd_conv_b, ssd_dt_bias, ssd_a_log, ssd_d, ssd_norm_w, w_pb, w_out, pre_norm2, post_norm2, w_ff1, w_ff2, loss_target, m_w_ada, m_b_ada, m_pre_norm1, m_post_norm1, m_w_in, m_b_gate, m_lru_conv_w, m_lru_conv_b, m_lru_wa, m_lru_ba, m_lru_wx, m_lru_bx, m_lru_lambda, m_w_pa, m_ssd_conv_w, m_ssd_conv_b, m_ssd_dt_bias, m_ssd_a_log, m_ssd_d, m_ssd_norm_w, m_w_pb, m_w_out, m_pre_norm2, m_post_norm2, m_w_ff1, m_w_ff2, v_w_ada, v_b_ada, v_pre_norm1, v_post_norm1, v_w_in, v_b_gate, v_lru_conv_w, v_lru_conv_b, v_lru_wa, v_lru_ba, v_lru_wx, v_lru_bx, v_lru_lambda, v_w_pa, v_ssd_conv_w, v_ssd_conv_b, v_ssd_dt_bias, v_ssd_a_log, v_ssd_d, v_ssd_norm_w, v_w_pb, v_w_out, v_pre_norm2, v_post_norm2, v_w_ff1, v_w_ff2):
    given = dict(locals())
    w = {k: given[k] for k in WEIGHTS}
    mom = {k: given["m_" + k] for k in WEIGHTS}
    var = {k: given["v_" + k] for k in WEIGHTS}
    nb, seq, _ = x.shape
    assert nb == 2 and seq % 512 == 0, (nb, seq)
    t = nb * seq
    me = _dev_index(lax.axis_index("x"), lax.axis_index("y"), lax.axis_index("c"))
    x2 = x.reshape(t, D_MODEL)
    tgt2 = loss_target.reshape(t, D_MODEL)
    ada_cols = w_ada.shape[2]

    slab = jnp.zeros((16, LANES), F32)
    slab = slab.at[0:nb].set(c)
    slab = slab.at[2:6, 0:lru_conv_w.shape[2]].set(lru_conv_w[0])
    slab = slab.at[6:10, 0:ssd_conv_w.shape[2]].set(ssd_conv_w[0])
    (g1,) = all_gather([slab], "gather_cond")
    c_all = g1[:, 0:nb].reshape(NDEV * nb, D_MODEL)
    lru_cw = g1[:, 2:6, 0:lru_conv_w.shape[2]].transpose(1, 0, 2).reshape(4, D_MODEL)
    ssd_cw = g1[:, 6:10, 0:ssd_conv_w.shape[2]].transpose(1, 0, 2).reshape(4, 2 * SSD_INNER)
    b_cols = lax.dynamic_slice(b_ada, (0, me * ada_cols), (1, ada_cols))
    mod_cols = ada_fwd(c_all, w_ada[0], b_cols)
    (g2,) = all_gather([mod_cols], "gather_mod")
    mod_all = g2.transpose(1, 0, 2).reshape(NDEV * nb, N_MOD * D_MODEL)
    mod_mine = lax.dynamic_slice(mod_all, (me * nb, 0), (nb, N_MOD * D_MODEL)).reshape(nb, N_MOD, D_MODEL)
    mod8 = jnp.pad(mod_mine, ((0, 0), (0, 8 - N_MOD), (0, 0)))

    (gw_in,) = all_gather([w_in.astype(BF16)], "gather_w_in")
    shard = IN_DIM // NDEV
    kd, od = DT_COL0 // shard, DT_COL0 % shard
    assert od + SSD_HEADS <= shard
    gb = gw_in[:, 0]
    w_main = jnp.concatenate([gb[k] for k in range(kd)] + [gb[kd][:, :od], gb[kd][:, od + SSD_HEADS:]]
                             + [gb[k] for k in range(kd + 1, NDEV)], axis=1)
    w_dt = jnp.pad(gb[kd][:, od:od + SSD_HEADS], ((0, 0), (0, 128 - SSD_HEADS)))

    wa_bd = _block_diag4(lru_wa[0]).astype(BF16)
    wx_bd = _block_diag4(lru_wx[0]).astype(BF16)
    lam = lru_lambda[0]
    vec = _pack([lru_ba, lru_bx, jax.nn.softplus(-lam)], 8)
    tri, triu = ssd_consts()
    hp, lp = ssd_params(ssd_dt_bias[0], ssd_a_log[0], ssd_d[0], ssd_norm_w[0])

    rest = Gather([w[k].astype(BF16) for k in BIG[1:]])
    cw_all = jnp.concatenate([lru_cw, ssd_cw], axis=1)
    cb_all = jnp.concatenate([lru_conv_b, ssd_conv_b], axis=1)
    (proj, h1t, dtraw, xa, xbc), gw = in_proj_fwd(x2, mod8, pre_norm1, w_main, w_dt, cw_all, cb_all, seq, side=rest)
    w_pa_f = gw[0].reshape(D_MODEL, D_MODEL)
    w_pb_f = gw[1].reshape(SSD_INNER, D_MODEL)
    w_out_f = gw[2].reshape(D_MODEL, D_MODEL)
    w_ff1_f = gw[3][:, 0]
    w_ff2_f = gw[4].reshape(D_FF, D_MODEL)
    ya_in, hst = lru_fwd(xa, proj, wa_bd, wx_bd, vec, nb, seq)
    yb_in, states = ssd_fwd(xbc, proj, dtraw, hp, lp, tri, nb, seq)
    yab, out1, x1 = merge_fwd(ya_in, yb_in, proj, x2, mod8, b_gate, post_norm1, w_pa_f, w_pb_f, w_out_f, seq)

    dx1, h2, da1, act, dy2, loss8, vacc_mlp, dmod_mlp = mlp_fwd_bwd(
        x1, tgt2, mod8, pre_norm2, post_norm2, w_ff1_f, w_ff2_f, nb, seq)
    wg = dict(out_dtype=BF16, ta=True, tm=1024, tn=1024, tk=1024)
    dw_ff1 = matmul(h2, da1, name="wgrad_ff1", blocked_out=D_FF // NDEV, **wg)
    dw_ff2 = matmul(act, dy2, name="wgrad_ff2", **wg)
    dya_in, dyb_in, dgates, dyab, dout1, merged, vacc_mg, dmod_mg = merge_bwd(
        dx1, out1, yab, proj, mod8, b_gate, post_norm1, w_pa_f, w_pb_f, w_out_f, nb, seq)
    dw_out = matmul(merged, dout1, name="wgrad_out", **wg)
    dw_pa = matmul(ya_in, dyab, name="wgrad_pa", n=D_MODEL, b_off=0, **wg)
    dw_pb = matmul(yb_in, dyab, name="wgrad_pb", n=D_MODEL, b_off=1, **wg)
    by_rows = lambda g: g.reshape(NDEV, g.shape[0] // NDEV, g.shape[1])
    (dxa, dlg, dwa_bd, dwx_bd, dvec), parts_ff = lru_bwd(
        dya_in, xa, proj, hst, wa_bd, wx_bd, vec, nb, seq, side=Scatter([dw_ff1, by_rows(dw_ff2)]))
    (dxbc, dz, ddt, hpg, lpg), parts_mg = ssd_bwd(xbc, proj, dtraw, hp, lp, tri, triu, states, dyb_in, nb, seq,
                                                  side=Scatter([by_rows(dw_pa), by_rows(dw_pb), by_rows(dw_out)]))
    ddt_b = ddt.astype(BF16)
    accs = dict(vmg=vacc_mg, vmlp=vacc_mlp, dvec=dvec, hpg=hpg, lpg=lpg,
                gwa=_diag_blocks4(dwa_bd).reshape(LRU_HEADS * 64, 64), gwx=_diag_blocks4(dwx_bd).reshape(LRU_HEADS * 64, 64))
    (dw_main, dw_dt, dlx, dxr, acc_l, acc_s), g_small = in_proj_wgrad(
        h1t, proj, dxa, dxbc, dlg, dz, dgates, ddt_b, cw_all, cb_all, seq, side=Gather([accs[k] for k in SMALL_RIDE]))
    pieces = (dlx, dlg, dz, dxr, dgates)
    cut = lambda k: dw_main[:, k * shard - (SSD_HEADS if k > kd else 0):(k + 1) * shard - (SSD_HEADS if k >= kd else 0)]
    blk_dt = jnp.concatenate([dw_main[:, kd * shard:DT_COL0], dw_dt[:, :SSD_HEADS],
                              dw_main[:, DT_COL0:(kd + 1) * shard - SSD_HEADS]], axis=1)
    dw_blocks = jnp.stack([blk_dt if k == kd else cut(k) for k in range(NDEV)])
    (grad_x, vacc_in, dmod_in), parts_in = in_proj_bwd(pieces, ddt_b, dx1, x2, mod8, pre_norm1, w_main, w_dt, nb, seq,
                                                       side=Scatter([dw_blocks]))
    parts = dict(zip(BIG, (parts_in[0], *parts_mg, *parts_ff)))

    dmod = (dmod_in + dmod_mg + dmod_mlp)[:, :N_MOD].reshape(nb, N_MOD * D_MODEL)
    g3, g_vin, g_accl, g_accs = all_gather([jnp.pad(dmod, ((0, 8 - nb), (0, 0))), vacc_in, acc_l, acc_s], "gather_dmod")
    dmod_all = g3[:, :nb].reshape(NDEV * nb, N_MOD * D_MODEL)
    dmod_cols = lax.dynamic_slice(dmod_all, (0, me * ada_cols), (NDEV * nb, ada_cols))
    g_w_ada, g_b_ada = ada_bwd(c_all, dmod_cols, dmod_all)

    res = {}
    for k in BIG:
        res[k] = adamw(parts[k], w[k], mom[k], var[k], "adamw_" + k)
    res['w_ada'] = adamw(g_w_ada[None], w_ada, m_w_ada, v_w_ada, "adamw_w_ada")

    gathered = dict(zip(SMALL_RIDE, g_small), vin=g_vin, accl=g_accl, accs=g_accs)
    view = lambda a: a.reshape(-1, a.shape[-1])
    res_a, g_lru_cw, g_ssd_cw = adamw_small(gathered, {k: (view(w[k]), view(mom[k]), view(var[k])) for k in REPL})
    res.update(res_a)
    lcw, scw = lru_conv_w.shape[2], ssd_conv_w.shape[2]
    sharded = {'b_ada': g_b_ada[None], 'lru_conv_w': lax.dynamic_slice(g_lru_cw, (0, me * lcw), (4, lcw))[None],
               'ssd_conv_w': lax.dynamic_slice(g_ssd_cw, (0, me * scw), (4, scw))[None]}
    for k, g in sharded.items():
        as3 = lambda a: a.reshape(g.shape)
        res[k] = adamw(g, as3(w[k]), as3(mom[k]), as3(var[k]), "adamw_" + k)

    loss = lax.psum(loss8[0, 0], ("x", "y", "c"))
    outs = [[res[k][j].reshape(w[k].shape) for k in WEIGHTS] for j in range(4)]
    return (loss, grad_x.reshape(x.shape), *outs[0], *outs[1], *outs[2], *outs[3])
```

```python
import functools

import numpy as np
import jax
import jax.numpy as jnp
from jax import lax
from jax.experimental import pallas as pl
from jax.experimental.pallas import tpu as pltpu

F32 = jnp.float32
BF16 = jnp.bfloat16

D_MODEL = 1024
LRU_HEADS = 16
LRU_BLOCK = 256
LRU_C = 8.0
SSD_INNER = 2048
SSD_HEADS = 32
SSD_P = 64
SSD_G = 8
SSD_N = 128
SSD_L = 128
SSD_GW = SSD_INNER // SSD_G
D_FF = 4096
N_MOD = 6
EPS = 1e-6
NDEV = 8

C_LRU_X, C_LRU_G, C_Z, C_XBC, C_GATES, PROJ_MAIN = 0, 1024, 2048, 4096, 8192, 10240
IN_DIM = 10272
DT_COL0 = 8192
HALO = 16
SSD_FWD_CPS = 1
HT_TOK = 512

ADAM_LR, ADAM_B1, ADAM_B2, ADAM_EPS, ADAM_WD, ADAM_STEP = 0.001, 0.9, 0.999, 1e-08, 0.01, 10

VMEM_LIMIT = 60 * 1024 * 1024
MESH = pl.DeviceIdType.MESH
ANY = pl.BlockSpec(memory_space=pl.ANY)
VMEM_FULL = pl.BlockSpec(memory_space=pltpu.VMEM)


def _cp(*sem):
    return pltpu.CompilerParams(dimension_semantics=sem, vmem_limit_bytes=VMEM_LIMIT)


def _dot(a, b):
    return jnp.dot(a, b, preferred_element_type=F32)


def _dot_nt(a, b):
    return lax.dot_general(a, b, (((1,), (1,)), ((), ())), preferred_element_type=F32)


def _dot_tn(a, b):
    return lax.dot_general(a, b, (((0,), (0,)), ((), ())), preferred_element_type=F32)


def _dot_hi(a, b):
    return jnp.dot(a, b, precision=lax.Precision.HIGHEST, preferred_element_type=F32)


def _sigmoid(x):
    return 1.0 / (1.0 + jnp.exp(-x))


def _gelu_and_grad(x):
    k0, k1 = 0.7978845608028654, 0.044715
    t = jnp.tanh(k0 * (x + k1 * x * x * x))
    g = 0.5 * x * (1.0 + t)
    dg = 0.5 * (1.0 + t) + 0.5 * x * (1.0 - t * t) * k0 * (1.0 + 3.0 * k1 * x * x)
    return g, dg


def _neg_expm1(y):
    p = 1.0 + y * (1.0 / 7.0)
    p = 1.0 + y * (1.0 / 6.0) * p
    p = 1.0 + y * (1.0 / 5.0) * p
    p = 1.0 + y * (1.0 / 4.0) * p
    p = 1.0 + y * (1.0 / 3.0) * p
    p = 1.0 + y * 0.5 * p
    return jnp.where(y > -0.3, -y * p, 1.0 - jnp.exp(y))


def _colsum(v):
    return jnp.sum(v, axis=0, keepdims=True)


def _rowmean(v):
    return jnp.mean(v, axis=-1, keepdims=True)


def matmul(a, b, *, ta=False, tb=False, out_dtype=F32, tm, tn, tk, name, n=None, b_off=0, blocked_out=False):
    m = a.shape[1] if ta else a.shape[0]
    kdim = a.shape[0] if ta else a.shape[1]
    n = n or (b.shape[0] if tb else b.shape[1])
    tm, tn, tk = min(tm, m), min(tn, n), min(tk, kdim)
    nk = kdim // tk
    dn = (((0 if ta else 1,), (1 if tb else 0,)), ((), ()))
    bw = blocked_out or tn

    def body(a_ref, b_ref, o_ref, acc_ref):
        k = pl.program_id(2)
        p = lax.dot_general(a_ref[...], b_ref[...], dn, preferred_element_type=F32)

        def emit(v):
            if blocked_out:
                for q in range(tn // bw):
                    o_ref[q] = v[:, q * bw:(q + 1) * bw].astype(out_dtype)
            else:
                o_ref[...] = v.astype(out_dtype)

        if nk == 1:
            emit(p)
        else:
            @pl.when(k == 0)
            def _():
                acc_ref[...] = p

            @pl.when(k > 0)
            def _():
                acc_ref[...] += p

            @pl.when(k == nk - 1)
            def _():
                emit(acc_ref[...])

    a_spec = pl.BlockSpec((tk, tm), lambda i, j, k: (k, i)) if ta else pl.BlockSpec((tm, tk), lambda i, j, k: (i, k))
    b_spec = (pl.BlockSpec((tn, tk), lambda i, j, k: (j, k)) if tb
              else pl.BlockSpec((tk, tn), lambda i, j, k: (k, j + b_off)))
    if blocked_out:
        o_spec, o_shape = pl.BlockSpec((tn // bw, tm, bw), lambda i, j, k: (j, i, 0)), (n // bw, m, bw)
    else:
        o_spec, o_shape = pl.BlockSpec((tm, tn), lambda i, j, k: (i, j)), (m, n)
    return pl.pallas_call(
        body, name=name, grid=(m // tm, n // tn, nk),
        in_specs=[a_spec, b_spec], out_specs=o_spec,
        out_shape=jax.ShapeDtypeStruct(o_shape, out_dtype),
        scratch_shapes=[pltpu.VMEM((tm, tn), F32)],
        compiler_params=_cp("parallel", "parallel", "arbitrary"),
    )(a, b)


def _conv_tile(j, tn):
    return jnp.where(j == 0, 0, jnp.clip(j - C_XBC // tn + 1, 1, 2 * SSD_INNER // tn))


def in_proj_fwd(x2, mod8, pre1, w_main, w_dt, cw, cb, seq, side=None):
    t = x2.shape[0]
    tm = min(1024, seq)
    tn = 1024
    per_seq = seq // tm
    j_xbc = C_XBC // tn
    n_xbc = 2 * SSD_INNER // tn
    cs = 256

    def body(x_ref, mod_ref, pre_ref, w_ref, wdt_ref, cw_ref, cb_ref, proj_ref, h_ref, dt_ref, xa_ref, xbc_ref,
             h_scr, carry_scr):
        i, j = pl.program_id(0), pl.program_id(1)

        @pl.when(j == 0)
        def _():
            xv = x_ref[...]
            y = xv * lax.rsqrt(_rowmean(xv * xv) + EPS) * pre_ref[...]
            m = mod_ref[0]
            hf = y * (1.0 + m[1:2, :]) + m[0:1, :]
            h = hf.astype(BF16)
            h_scr[...] = h
            hft = hf.T.astype(BF16)
            for q in range(tm // HT_TOK):
                h_ref[q] = hft[:, q * HT_TOK:(q + 1) * HT_TOK]
            dt_ref[...] = _dot(h, wdt_ref[...])

        def project(c0=0, width=tn):
            pb = _dot(h_scr[...], w_ref[:, c0:c0 + width]).astype(BF16)
            proj_ref[:, c0:c0 + width] = pb
            return pb

        def conv(o_ref, slot, act):
            first = lax.rem(i, per_seq) == 0
            for c0 in range(0, tn, cs):
                cur = project(c0, cs).astype(F32)
                prev = jnp.where(first, 0.0, carry_scr[slot, :, c0:c0 + cs])
                carry_scr[slot, :, c0:c0 + cs] = cur[tm - HALO:, :]
                xx = jnp.concatenate([prev, cur], axis=0)
                w = cw_ref[:, c0:c0 + cs]
                acc = cur * w[3:4, :] + cb_ref[:, c0:c0 + cs]
                for d in (1, 2, 3):
                    acc = acc + pltpu.roll(xx, d, axis=0)[HALO:, :] * w[3 - d:4 - d, :]
                if act:
                    acc = acc * _sigmoid(acc)
                o_ref[:, c0:c0 + cs] = acc.astype(BF16)

        is_xbc = (j >= j_xbc) & (j < j_xbc + n_xbc)

        @pl.when(j == 0)
        def _():
            conv(xa_ref, 0, False)

        @pl.when(is_xbc)
        def _():
            conv(xbc_ref, j - j_xbc + 1, True)

        @pl.when((j > 0) & jnp.logical_not(is_xbc))
        def _():
            project()

    return _call(
        body, name="in_proj_fwd", grid=(t // tm, PROJ_MAIN // tn), side=side, sem=("arbitrary", "arbitrary"),
        args=(x2, mod8, pre1, w_main, w_dt, cw, cb),
        in_specs=[pl.BlockSpec((tm, D_MODEL), lambda i, j: (i, 0)),
                  pl.BlockSpec((1, 8, D_MODEL), lambda i, j: (i // per_seq, 0, 0)),
                  pl.BlockSpec((1, D_MODEL), lambda i, j: (0, 0)),
                  pl.BlockSpec((D_MODEL, tn), lambda i, j: (0, j)),
                  pl.BlockSpec((D_MODEL, 128), lambda i, j: (0, 0)),
                  pl.BlockSpec((4, tn), lambda i, j: (0, _conv_tile(j, tn))),
                  pl.BlockSpec((1, tn), lambda i, j: (0, _conv_tile(j, tn)))],
        out_specs=[pl.BlockSpec((tm, tn), lambda i, j: (i, j)),
                   pl.BlockSpec((tm // HT_TOK, D_MODEL, HT_TOK), lambda i, j: (i, 0, 0)),
                   pl.BlockSpec((tm, 128), lambda i, j: (i, 0)),
                   pl.BlockSpec((tm, tn), lambda i, j: (i, 0)),
                   pl.BlockSpec((tm, tn), lambda i, j: (i, jnp.clip(j - j_xbc, 0, n_xbc - 1)))],
        out_shape=[jax.ShapeDtypeStruct((t, PROJ_MAIN), BF16), jax.ShapeDtypeStruct((t // HT_TOK, D_MODEL, HT_TOK), BF16),
                   jax.ShapeDtypeStruct((t, 128), F32), jax.ShapeDtypeStruct((t, D_MODEL), BF16),
                   jax.ShapeDtypeStruct((t, 2 * SSD_INNER), BF16)],
        scratch_shapes=[pltpu.VMEM((tm, D_MODEL), BF16), pltpu.VMEM((1 + n_xbc, HALO, tn), F32)])


def _lru_gates(xa, wa_ref, wx_ref, ba, bx, sp):
    nblk = D_MODEL // LRU_BLOCK
    pr = jnp.concatenate([_dot(xa[:, j * LRU_BLOCK:(j + 1) * LRU_BLOCK], wa_ref[j]) for j in range(nblk)], axis=1) + ba
    pi = jnp.concatenate([_dot(xa[:, j * LRU_BLOCK:(j + 1) * LRU_BLOCK], wx_ref[j]) for j in range(nblk)], axis=1) + bx
    r = _sigmoid(pr)
    i = _sigmoid(pi)
    log_a = (-LRU_C * r) * sp
    return r, i, jnp.exp(log_a), _neg_expm1(2.0 * log_a)


def lru_fwd(xa, proj, wa_bd, wx_bd, vec, nb, seq):
    t = xa.shape[0]
    tc = min(512, seq)
    nk = seq // tc
    gb = C_LRU_G // D_MODEL

    def body(xa_ref, g_ref, wa_ref, wx_ref, vec_ref, ya_ref, h_ref, a_scr, u_scr, hc_scr):
        @pl.when(pl.program_id(1) == 0)
        def _():
            hc_scr[...] = jnp.zeros_like(hc_scr)

        xa_v = xa_ref[...]
        v = vec_ref[...]
        r, i, a, e = _lru_gates(xa_v, wa_ref, wx_ref, v[0:1, :], v[1:2, :], v[2:3, :])
        a_scr[...] = a
        u_scr[...] = jnp.sqrt(e) * (i * xa_v.astype(F32))
        row = lax.broadcasted_iota(jnp.int32, (8, 1), 0)

        def tile(j, h):
            r0 = pl.multiple_of(j * 8, 8)
            av, uv = a_scr[pl.ds(r0, 8), :], u_scr[pl.ds(r0, 8), :]
            for d in (1, 2, 4):
                uv = uv + av * jnp.where(row >= d, pltpu.roll(uv, d, axis=0), 0.0)
                av = av * jnp.where(row >= d, pltpu.roll(av, d, axis=0), 1.0)
            hv = uv + av * h
            h_ref[pl.ds(r0, 8), :] = hv
            return hv[7:8, :]

        hc_scr[...] = lax.fori_loop(0, tc // 8, tile, hc_scr[...], unroll=2)
        gel, _ = _gelu_and_grad(g_ref[...].astype(F32))
        ya_ref[...] = (h_ref[...] * gel).astype(BF16)

    return pl.pallas_call(
        body, name="lru_fwd", grid=(nb, nk),
        in_specs=[pl.BlockSpec((tc, D_MODEL), lambda b, k: (b * nk + k, 0)),
                  pl.BlockSpec((tc, D_MODEL), lambda b, k: (b * nk + k, gb)),
                  VMEM_FULL, VMEM_FULL, VMEM_FULL],
        out_specs=[pl.BlockSpec((tc, D_MODEL), lambda b, k: (b * nk + k, 0)),
                   pl.BlockSpec((tc, D_MODEL), lambda b, k: (b * nk + k, 0))],
        out_shape=[jax.ShapeDtypeStruct((t, D_MODEL), BF16), jax.ShapeDtypeStruct((t, D_MODEL), F32)],
        scratch_shapes=[pltpu.VMEM((tc, D_MODEL), F32), pltpu.VMEM((tc, D_MODEL), F32), pltpu.VMEM((1, D_MODEL), F32)],
        compiler_params=_cp("arbitrary", "arbitrary"),
    )(xa, proj, wa_bd, wx_bd, vec)


def lru_bwd(dya, xa, proj, h, wa_bd, wx_bd, vec, nb, seq, side=None):
    t = xa.shape[0]
    tc = min(512, seq)
    nk = seq // tc
    gb = C_LRU_G // D_MODEL
    nblk = D_MODEL // LRU_BLOCK

    def chunk(b, k):
        return b * nk + (nk - 1 - k)

    def body(dya_ref, xa_ref, g_ref, h_ref, hp_ref, wa_ref, wx_ref, vec_ref,
             dxa_ref, dg_ref, dwa_ref, dwx_ref, dvec_ref, a_scr, dh_scr, c_scr):
        b, k = pl.program_id(0), pl.program_id(1)

        @pl.when((b == 0) & (k == 0))
        def _():
            dwa_ref[...] = jnp.zeros_like(dwa_ref)
            dwx_ref[...] = jnp.zeros_like(dwx_ref)
            dvec_ref[...] = jnp.zeros_like(dvec_ref)

        @pl.when(k == 0)
        def _():
            c_scr[...] = jnp.zeros_like(c_scr)

        xa_v = xa_ref[...]
        xaf = xa_v.astype(F32)
        v = vec_ref[...]
        sp = v[2:3, :]
        r, i, a, e = _lru_gates(xa_v, wa_ref, wx_ref, v[0:1, :], v[1:2, :], sp)
        gel, dgel = _gelu_and_grad(g_ref[...].astype(F32))
        hv = h_ref[...]
        dyv = dya_ref[...].astype(F32)
        dg_ref[...] = (dyv * hv * dgel).astype(BF16)
        a_scr[...] = a
        dh_scr[...] = dyv * gel

        row8 = lax.broadcasted_iota(jnp.int32, (8, 1), 0)

        def tile(j, c):
            r0 = pl.multiple_of((tc // 8 - 1 - j) * 8, 8)
            av, dout = a_scr[pl.ds(r0, 8), :], dh_scr[pl.ds(r0, 8), :]
            zv = av * dout
            for d in (1, 2, 4):
                zv = zv + av * jnp.where(row8 < 8 - d, pltpu.roll(zv, 8 - d, axis=0), 0.0)
                av = av * jnp.where(row8 < 8 - d, pltpu.roll(av, 8 - d, axis=0), 1.0)
            zv = zv + av * c
            dh_scr[pl.ds(r0, 8), :] = dout + jnp.where(row8 < 7, pltpu.roll(zv, 7, axis=0), c)
            return zv[0:1, :]

        c_scr[...] = lax.fori_loop(0, tc // 8, tile, c_scr[...], unroll=2)
        dh = dh_scr[...]
        h_last = jnp.where(k == nk - 1, 0.0, hp_ref[HALO // 2 - 1:HALO // 2, :])
        row = lax.broadcasted_iota(jnp.int32, (tc, 1), 0)
        h_prev = jnp.where(row == 0, h_last, pltpu.roll(hv, 1, axis=0))
        s = jnp.sqrt(e)
        da = dh * h_prev
        ix = i * xaf
        dlog_a = da * a - (dh * ix) * (a * a) * lax.rsqrt(jnp.maximum(e, 1e-30))
        di = dh * s * xaf
        dpr = (dlog_a * (-LRU_C * sp)) * (r * (1.0 - r))
        dpi = di * (i * (1.0 - i))
        dprb, dpib = dpr.astype(BF16), dpi.astype(BF16)
        dxa = dh * s * i
        dxa = dxa + jnp.concatenate(
            [_dot_nt(dprb[:, j * LRU_BLOCK:(j + 1) * LRU_BLOCK], wa_ref[j])
             + _dot_nt(dpib[:, j * LRU_BLOCK:(j + 1) * LRU_BLOCK], wx_ref[j]) for j in range(nblk)], axis=1)
        dxa_ref[...] = dxa.astype(BF16)
        for j in range(nblk):
            sl = slice(j * LRU_BLOCK, (j + 1) * LRU_BLOCK)
            dwa_ref[j] += _dot_tn(xa_v[:, sl], dprb[:, sl])
            dwx_ref[j] += _dot_tn(xa_v[:, sl], dpib[:, sl])
        dvec_ref[...] += jnp.concatenate(
            [_colsum(dpr), _colsum(dpi), _colsum(dlog_a * (-LRU_C * r)), jnp.zeros((5, D_MODEL), F32)], axis=0)

    hh = HALO // 2
    return _call(
        body, name="lru_bwd", grid=(nb, nk), side=side, sem=("arbitrary", "arbitrary"),
        args=(dya, xa, proj, h, h, wa_bd, wx_bd, vec),
        in_specs=[pl.BlockSpec((tc, D_MODEL), lambda b, k: (chunk(b, k), 0)),
                  pl.BlockSpec((tc, D_MODEL), lambda b, k: (chunk(b, k), 0)),
                  pl.BlockSpec((tc, D_MODEL), lambda b, k: (chunk(b, k), gb)),
                  pl.BlockSpec((tc, D_MODEL), lambda b, k: (chunk(b, k), 0)),
                  pl.BlockSpec((hh, D_MODEL), lambda b, k: (jnp.maximum(chunk(b, k) * (tc // hh) - 1, 0), 0)),
                  VMEM_FULL, VMEM_FULL, VMEM_FULL],
        out_specs=[pl.BlockSpec((tc, D_MODEL), lambda b, k: (chunk(b, k), 0)),
                   pl.BlockSpec((tc, D_MODEL), lambda b, k: (chunk(b, k), 0)),
                   pl.BlockSpec((nblk, LRU_BLOCK, LRU_BLOCK), lambda b, k: (0, 0, 0)),
                   pl.BlockSpec((nblk, LRU_BLOCK, LRU_BLOCK), lambda b, k: (0, 0, 0)),
                   pl.BlockSpec((8, D_MODEL), lambda b, k: (0, 0))],
        out_shape=[jax.ShapeDtypeStruct((t, D_MODEL), BF16), jax.ShapeDtypeStruct((t, D_MODEL), BF16),
                   jax.ShapeDtypeStruct((nblk, LRU_BLOCK, LRU_BLOCK), F32),
                   jax.ShapeDtypeStruct((nblk, LRU_BLOCK, LRU_BLOCK), F32),
                   jax.ShapeDtypeStruct((8, D_MODEL), F32)],
        scratch_shapes=[pltpu.VMEM((tc, D_MODEL), F32), pltpu.VMEM((tc, D_MODEL), F32), pltpu.VMEM((1, D_MODEL), F32)])


def merge_fwd(ya_in, yb_in, proj, x2, mod8, bgate, post1, w_pa, w_pb, w_out, seq):
    t = x2.shape[0]
    tm = min(512, seq)
    per_seq = seq // tm
    gcb = C_GATES // SSD_INNER

    def body(ya_ref, yb_ref, gt_ref, x_ref, mod_ref, bg_ref, post_ref, wpa_ref, wpb_ref, wo_ref,
             yab_ref, out1_ref, x1_ref):
        y_a = _dot(ya_ref[...], wpa_ref[...])
        y_b = _dot(yb_ref[...], wpb_ref[...])
        g = _sigmoid(gt_ref[...].astype(F32) + bg_ref[...])
        merged = g[:, :D_MODEL] * y_a + g[:, D_MODEL:] * y_b
        out1 = _dot(merged.astype(BF16), wo_ref[...])
        n = out1 * lax.rsqrt(_rowmean(out1 * out1) + EPS)
        yab_ref[...] = jnp.concatenate([y_a, y_b], axis=1).astype(BF16)
        out1_ref[...] = out1
        x1_ref[...] = x_ref[...] + mod_ref[0][2:3, :] * (n * post_ref[...])

    row = lambda w: pl.BlockSpec((tm, w), lambda i: (i, 0))
    return pl.pallas_call(
        body, name="merge_fwd", grid=(t // tm,),
        in_specs=[row(D_MODEL), row(SSD_INNER), pl.BlockSpec((tm, SSD_INNER), lambda i: (i, gcb)), row(D_MODEL),
                  pl.BlockSpec((1, 8, D_MODEL), lambda i: (i // per_seq, 0, 0)),
                  VMEM_FULL, VMEM_FULL, VMEM_FULL, VMEM_FULL, VMEM_FULL],
        out_specs=[row(SSD_INNER), row(D_MODEL), row(D_MODEL)],
        out_shape=[jax.ShapeDtypeStruct((t, SSD_INNER), BF16), jax.ShapeDtypeStruct((t, D_MODEL), F32),
                   jax.ShapeDtypeStruct((t, D_MODEL), F32)],
        compiler_params=_cp("parallel"),
    )(ya_in, yb_in, proj, x2, mod8, bgate, post1, w_pa, w_pb, w_out)


def merge_bwd(dx1, out1, yab, proj, mod8, bgate, post1, w_pa, w_pb, w_out, nb, seq):
    t = dx1.shape[0]
    tm = min(512, seq)
    per_seq = seq // tm
    gcb = C_GATES // SSD_INNER

    def body(dx1_ref, out1_ref, yab_ref, gt_ref, mod_ref, bg_ref, post_ref, wpa_ref, wpb_ref, wo_ref,
             dya_ref, dyb_ref, dgt_ref, dyab_ref, dout1_ref, mg_ref, vacc_ref, dmod_ref):
        b, s = pl.program_id(0), pl.program_id(1)

        @pl.when((b == 0) & (s == 0))
        def _():
            vacc_ref[...] = jnp.zeros_like(vacc_ref)

        @pl.when(s == 0)
        def _():
            dmod_ref[...] = jnp.zeros_like(dmod_ref)

        dx1v = dx1_ref[...]
        out1 = out1_ref[...]
        post = post_ref[...]
        rs = lax.rsqrt(_rowmean(out1 * out1) + EPS)
        n = out1 * rs
        do = dx1v * mod_ref[0][2:3, :]
        dn = do * post
        dout1 = rs * (dn - n * _rowmean(dn * n))
        dout1b = dout1.astype(BF16)
        dout1_ref[...] = dout1b
        dmerged = _dot_nt(dout1b, wo_ref[...])
        g = _sigmoid(gt_ref[...].astype(F32) + bg_ref[...])
        yab_v = yab_ref[...].astype(F32)
        gy = g * yab_v
        mg_ref[...] = (gy[:, :D_MODEL] + gy[:, D_MODEL:]).astype(BF16)
        dm2 = jnp.concatenate([dmerged, dmerged], axis=1)
        dyab = (dm2 * g).astype(BF16)
        dyab_ref[...] = dyab
        dgt = dm2 * gy * (1.0 - g)
        dgt_ref[...] = dgt.astype(BF16)
        dya_ref[...] = _dot_nt(dyab[:, :D_MODEL], wpa_ref[...]).astype(BF16)
        dyb_ref[...] = _dot_nt(dyab[:, D_MODEL:], wpb_ref[...]).astype(BF16)
        vacc_ref[...] += jnp.concatenate(
            [_colsum(dgt), jnp.concatenate([_colsum(do * n), jnp.zeros((1, D_MODEL), F32)], axis=1),
             jnp.zeros((6, SSD_INNER), F32)], axis=0)
        dmod_ref[0] += jnp.concatenate(
            [jnp.zeros((2, D_MODEL), F32), _colsum(dx1v * (n * post)), jnp.zeros((5, D_MODEL), F32)], axis=0)

    row = lambda w: pl.BlockSpec((tm, w), lambda b, s: (b * per_seq + s, 0))
    return pl.pallas_call(
        body, name="merge_bwd", grid=(nb, per_seq),
        in_specs=[row(D_MODEL), row(D_MODEL), row(SSD_INNER),
                  pl.BlockSpec((tm, SSD_INNER), lambda b, s: (b * per_seq + s, gcb)),
                  pl.BlockSpec((1, 8, D_MODEL), lambda b, s: (b, 0, 0)),
                  VMEM_FULL, VMEM_FULL, VMEM_FULL, VMEM_FULL, VMEM_FULL],
        out_specs=[row(D_MODEL), row(SSD_INNER), row(SSD_INNER), row(SSD_INNER), row(D_MODEL), row(D_MODEL),
                   pl.BlockSpec((8, SSD_INNER), lambda b, s: (0, 0)),
                   pl.BlockSpec((1, 8, D_MODEL), lambda b, s: (b, 0, 0))],
        out_shape=[jax.ShapeDtypeStruct((t, D_MODEL), BF16), jax.ShapeDtypeStruct((t, SSD_INNER), BF16),
                   jax.ShapeDtypeStruct((t, SSD_INNER), BF16), jax.ShapeDtypeStruct((t, SSD_INNER), BF16),
                   jax.ShapeDtypeStruct((t, D_MODEL), BF16), jax.ShapeDtypeStruct((t, D_MODEL), BF16),
                   jax.ShapeDtypeStruct((8, SSD_INNER), F32), jax.ShapeDtypeStruct((nb, 8, D_MODEL), F32)],
        compiler_params=_cp("arbitrary", "arbitrary"),
    )(dx1, out1, yab, proj, mod8, bgate, post1, w_pa, w_pb, w_out)


def mlp_fwd_bwd(x1, tgt, mod8, pre2, post2, w_ff1, w_ff2, nb, seq):
    t = x1.shape[0]
    tm = min(256, seq)
    per_seq = seq // tm
    fc = 1024
    nfc = D_FF // fc

    def body(x1_ref, tgt_ref, mod_ref, pre_ref, post_ref, w1_ref, w2_ref,
             dx1_ref, h2_ref, da1_ref, act_ref, dy2_ref, loss_ref, vacc_ref, dmod_ref, r_scr):
        b, s = pl.program_id(0), pl.program_id(1)
        per = fc // w1_ref.shape[2]

        def w1_cols(c):
            return jnp.concatenate([w1_ref[per * c + q] for q in range(per)], axis=1)

        @pl.when((b == 0) & (s == 0))
        def _():
            vacc_ref[...] = jnp.zeros_like(vacc_ref)
            loss_ref[...] = jnp.zeros_like(loss_ref)

        @pl.when(s == 0)
        def _():
            dmod_ref[...] = jnp.zeros_like(dmod_ref)

        m = mod_ref[0]
        sh2, sc2, g2 = m[3:4, :], m[4:5, :], m[5:6, :]
        pre, post = pre_ref[...], post_ref[...]
        x1v = x1_ref[...]
        rs1 = lax.rsqrt(_rowmean(x1v * x1v) + EPS)
        n1 = x1v * rs1
        y1 = n1 * pre
        h2b = (y1 * (1.0 + sc2) + sh2).astype(BF16)
        h2_ref[...] = h2b
        y2 = jnp.zeros((tm, D_MODEL), F32)
        for c in range(nfc):
            r = jnp.maximum(_dot(h2b, w1_cols(c)), 0.0)
            r_scr[:, c * fc:(c + 1) * fc] = r
            a = (r * r).astype(BF16)
            act_ref[:, c * fc:(c + 1) * fc] = a
            y2 = y2 + _dot(a, w2_ref[c * fc:(c + 1) * fc, :])
        rs2 = lax.rsqrt(_rowmean(y2 * y2) + EPS)
        n2 = y2 * rs2
        o2 = n2 * post
        diff = x1v + g2 * o2 - tgt_ref[...]
        loss_ref[...] += 0.5 * jnp.sum(_rowmean(diff * diff))
        dx2 = diff * (1.0 / D_MODEL)
        do2 = dx2 * g2
        dn2 = do2 * post
        dy2b = (rs2 * (dn2 - n2 * _rowmean(dn2 * n2))).astype(BF16)
        dy2_ref[...] = dy2b
        dh2 = jnp.zeros((tm, D_MODEL), F32)
        for c in range(nfc):
            dact = _dot_nt(dy2b, w2_ref[c * fc:(c + 1) * fc, :])
            da = (dact * (2.0 * r_scr[:, c * fc:(c + 1) * fc])).astype(BF16)
            da1_ref[:, c * fc:(c + 1) * fc] = da
            dh2 = dh2 + _dot_nt(da, w1_cols(c))
        dy1 = dh2 * (1.0 + sc2)
        dn1 = dy1 * pre
        dx1_ref[...] = dx2 + rs1 * (dn1 - n1 * _rowmean(dn1 * n1))
        vacc_ref[...] += jnp.concatenate([_colsum(dy1 * n1), _colsum(do2 * n2), jnp.zeros((6, D_MODEL), F32)], axis=0)
        dmod_ref[0] += jnp.concatenate(
            [jnp.zeros((3, D_MODEL), F32), _colsum(dh2), _colsum(dh2 * y1), _colsum(dx2 * o2),
             jnp.zeros((2, D_MODEL), F32)], axis=0)

    row = lambda w: pl.BlockSpec((tm, w), lambda b, s: (b * per_seq + s, 0))
    return pl.pallas_call(
        body, name="mlp_fwd_bwd", grid=(nb, per_seq),
        in_specs=[row(D_MODEL), row(D_MODEL), pl.BlockSpec((1, 8, D_MODEL), lambda b, s: (b, 0, 0)),
                  VMEM_FULL, VMEM_FULL, VMEM_FULL, VMEM_FULL],
        out_specs=[row(D_MODEL), row(D_MODEL), row(D_FF), row(D_FF), row(D_MODEL),
                   pl.BlockSpec((8, 128), lambda b, s: (0, 0)),
                   pl.BlockSpec((8, D_MODEL), lambda b, s: (0, 0)),
                   pl.BlockSpec((1, 8, D_MODEL), lambda b, s: (b, 0, 0))],
        out_shape=[jax.ShapeDtypeStruct((t, D_MODEL), F32), jax.ShapeDtypeStruct((t, D_MODEL), BF16),
                   jax.ShapeDtypeStruct((t, D_FF), BF16), jax.ShapeDtypeStruct((t, D_FF), BF16),
                   jax.ShapeDtypeStruct((t, D_MODEL), BF16), jax.ShapeDtypeStruct((8, 128), F32),
                   jax.ShapeDtypeStruct((8, D_MODEL), F32), jax.ShapeDtypeStruct((nb, 8, D_MODEL), F32)],
        scratch_shapes=[pltpu.VMEM((tm, D_FF), F32)],
        compiler_params=_cp("arbitrary", "arbitrary"),
    )(x1, tgt, mod8, pre2, post2, w_ff1, w_ff2)


_PIECES = ((C_LRU_X, 1024), (C_LRU_G, 1024), (C_Z, 2048), (C_XBC, 4096), (C_GATES, 2048))
_NP = len(_PIECES)


def in_proj_bwd(pieces, ddt, dx1, x2, mod8, pre1, w_main, w_dt, nb, seq, side=None):
    t = x2.shape[0]
    tm = min(512, seq)
    per_seq = seq // tm
    widths = [min(w, 2048) for _, w in _PIECES]
    steps = [(p, q) for p, (_, w) in enumerate(_PIECES) for q in range(w // widths[p])]
    nk = len(steps)

    def piece_spec(p):
        first = min(k for k in range(nk) if steps[k][0] == p)
        nblk = _PIECES[p][1] // widths[p]
        return pl.BlockSpec((tm, widths[p]), lambda b, s, k: (b * per_seq + s, jnp.clip(k - first, 0, nblk - 1)))

    def body(*refs):
        prefs = refs[:_NP]
        ddt_ref, dx1_ref, x_ref, mod_ref, pre_ref, w_ref, wdt_ref, gx_ref, vacc_ref, dmod_ref, acc_ref = refs[_NP:]
        b, s, k = pl.program_id(0), pl.program_id(1), pl.program_id(2)

        @pl.when((b == 0) & (s == 0) & (k == 0))
        def _():
            vacc_ref[...] = jnp.zeros_like(vacc_ref)

        @pl.when((s == 0) & (k == 0))
        def _():
            dmod_ref[...] = jnp.zeros_like(dmod_ref)

        @pl.when(k == 0)
        def _():
            acc_ref[...] = _dot_nt(ddt_ref[...], wdt_ref[...])

        for kk, (p, q) in enumerate(steps):
            @pl.when(k == kk)
            def _(p=p, q=q):
                c0 = _PIECES[p][0] + q * widths[p]
                acc_ref[...] += _dot_nt(prefs[p][...], w_ref[:, c0:c0 + widths[p]])

        @pl.when(k == nk - 1)
        def _():
            dh = acc_ref[...]
            m = mod_ref[0]
            pre = pre_ref[...]
            xv = x_ref[...]
            rs = lax.rsqrt(_rowmean(xv * xv) + EPS)
            n = xv * rs
            dy = dh * (1.0 + m[1:2, :])
            dn = dy * pre
            gx_ref[...] = dx1_ref[...] + rs * (dn - n * _rowmean(dn * n))
            vacc_ref[...] += jnp.concatenate([_colsum(dy * n), jnp.zeros((7, D_MODEL), F32)], axis=0)
            dmod_ref[0] += jnp.concatenate([_colsum(dh), _colsum(dh * (n * pre)), jnp.zeros((6, D_MODEL), F32)], axis=0)

    row = lambda w: pl.BlockSpec((tm, w), lambda b, s, k: (b * per_seq + s, 0))
    return _call(
        body, name="in_proj_bwd", grid=(nb, per_seq, nk), side=side, sem=("arbitrary", "arbitrary", "arbitrary"),
        args=(*pieces, ddt, dx1, x2, mod8, pre1, w_main, w_dt),
        in_specs=[piece_spec(p) for p in range(_NP)] + [
            row(128), row(D_MODEL), row(D_MODEL), pl.BlockSpec((1, 8, D_MODEL), lambda b, s, k: (b, 0, 0)),
            pl.BlockSpec((1, D_MODEL), lambda b, s, k: (0, 0)),
            VMEM_FULL,
            pl.BlockSpec((D_MODEL, 128), lambda b, s, k: (0, 0))],
        out_specs=[row(D_MODEL), pl.BlockSpec((8, D_MODEL), lambda b, s, k: (0, 0)),
                   pl.BlockSpec((1, 8, D_MODEL), lambda b, s, k: (b, 0, 0))],
        out_shape=[jax.ShapeDtypeStruct((t, D_MODEL), F32), jax.ShapeDtypeStruct((8, D_MODEL), F32),
                   jax.ShapeDtypeStruct((nb, 8, D_MODEL), F32)],
        scratch_shapes=[pltpu.VMEM((tm, D_MODEL), F32)])


def in_proj_wgrad(h1t, proj, dxa, dxbc, dlg, dz, dgates, ddt, cw, cb, seq, side=None):
    nt, _, tt = h1t.shape
    t = nt * tt
    tn = 1024
    nn = PROJ_MAIN // tn
    ns = seq // tt
    nh = t // HALO
    j_g, j_z, j_x, j_gt = C_LRU_G // tn, C_Z // tn, C_XBC // tn, C_GATES // tn
    n_x = 2 * SSD_INNER // tn
    strip = 256
    ne = tt + HALO

    def body(h_ref, cur_ref, prev_ref, next_ref, dxa_ref, dxan_ref, dxb_ref, dxbn_ref, dlg_ref, dz_ref, dgt_ref, ddt_ref,
             cw_ref, cb_ref, dw_ref, dwdt_ref, dlx_ref, dxr_ref, accl_ref, accs_ref, acc_ref, accdt_ref):
        n, k = pl.program_id(0), pl.program_id(1)
        hv = h_ref[k]
        is_x = (n >= j_x) & (n < j_x + n_x)

        @pl.when(k == 0)
        def _():
            acc_ref[...] = jnp.zeros_like(acc_ref)

        @pl.when((n == 0) & (k == 0))
        def _():
            accdt_ref[...] = jnp.zeros_like(accdt_ref)
            accl_ref[...] = jnp.zeros_like(accl_ref)

        @pl.when(is_x & (k == 0))
        def _():
            accs_ref[...] = jnp.zeros_like(accs_ref)

        def conv_tile(do_ref, don_ref, out_ref, cacc_ref, act):
            first = lax.rem(k, ns) == 0
            last = lax.rem(k, ns) == ns - 1
            for c0 in range(0, tn, strip):
                cs = slice(c0, c0 + strip)
                xx = jnp.concatenate([jnp.where(first, 0.0, prev_ref[:, cs].astype(F32)), cur_ref[:, cs].astype(F32),
                                      next_ref[:, cs].astype(F32)], axis=0)
                do_ext = jnp.concatenate([do_ref[:, cs].astype(F32),
                                          jnp.where(last, 0.0, don_ref[:, cs].astype(F32))], axis=0)
                w = cw_ref[:, cs]
                xs = [xx[HALO:HALO + ne, :]] + [pltpu.roll(xx, d, axis=0)[HALO:HALO + ne, :] for d in (1, 2, 3)]
                if act:
                    c = cb_ref[:, cs] + xs[0] * w[3:4, :] + xs[1] * w[2:3, :] + xs[2] * w[1:2, :] + xs[3] * w[0:1, :]
                    sg = _sigmoid(c)
                    dc = do_ext * (sg * (1.0 + c * (1.0 - sg)))
                else:
                    dc = do_ext
                dx = dc[:tt, :] * w[3:4, :]
                for d in (1, 2, 3):
                    dx = dx + pltpu.roll(dc, ne - d, axis=0)[:tt, :] * w[3 - d:4 - d, :]
                dxb = dx.astype(BF16)
                out_ref[:, cs] = dxb
                acc_ref[:, cs] += _dot(hv, dxb)
                dcc = dc[:tt, :]
                rows = [_colsum(dcc * xs[3 - r][:tt, :]) for r in range(4)] + [_colsum(dcc)]
                cacc_ref[:, cs] += jnp.concatenate(rows + [jnp.zeros((3, strip), F32)], axis=0)

        @pl.when(n == 0)
        def _():
            conv_tile(dxa_ref, dxan_ref, dlx_ref, accl_ref, False)
            accdt_ref[...] += _dot(hv, ddt_ref[...])

        @pl.when(is_x)
        def _():
            conv_tile(dxb_ref, dxbn_ref, dxr_ref, accs_ref, True)

        @pl.when(n == j_g)
        def _():
            acc_ref[...] += _dot(hv, dlg_ref[...])

        @pl.when((n >= j_z) & (n < j_x))
        def _():
            acc_ref[...] += _dot(hv, dz_ref[...])

        @pl.when(n >= j_gt)
        def _():
            acc_ref[...] += _dot(hv, dgt_ref[...])

        @pl.when(k == nt - 1)
        def _():
            dw_ref[...] = acc_ref[...].astype(BF16)

        @pl.when((n == 0) & (k == nt - 1))
        def _():
            dwdt_ref[...] = accdt_ref[...].astype(BF16)

    conv_n = lambda n: (n == 0) | ((n >= j_x) & (n < j_x + n_x))
    src_col = lambda n: jnp.where(n == 0, 0, jnp.clip(n, j_x, j_x + n_x - 1))
    ctile = lambda n: jnp.where(n == 0, 0, jnp.clip(n - j_x + 1, 1, n_x))
    xcol = lambda n: jnp.clip(n - j_x, 0, n_x - 1)
    on = lambda cond, k: jnp.where(cond, k, 0)
    nxt = lambda k: jnp.minimum(((k + 1) * tt) // HALO, nh - 1)
    after = lambda cond_during, cond_after, k: jnp.where(cond_during, k, jnp.where(cond_after, nt - 1, 0))
    in_specs = [
        VMEM_FULL,
        pl.BlockSpec((tt, tn), lambda n, k: (on(conv_n(n), k), src_col(n))),
        pl.BlockSpec((HALO, tn), lambda n, k: (on(conv_n(n), jnp.maximum((k * tt) // HALO - 1, 0)), src_col(n))),
        pl.BlockSpec((HALO, tn), lambda n, k: (on(conv_n(n), nxt(k)), src_col(n))),
        pl.BlockSpec((tt, tn), lambda n, k: (on(n == 0, k), 0)),
        pl.BlockSpec((HALO, tn), lambda n, k: (on(n == 0, nxt(k)), 0)),
        pl.BlockSpec((tt, tn), lambda n, k: (on((n >= j_x) & (n < j_x + n_x), k), xcol(n))),
        pl.BlockSpec((HALO, tn), lambda n, k: (on((n >= j_x) & (n < j_x + n_x), nxt(k)), xcol(n))),
        pl.BlockSpec((tt, tn), lambda n, k: (on(n == j_g, k), 0)),
        pl.BlockSpec((tt, tn), lambda n, k: (on((n >= j_z) & (n < j_x), k), jnp.clip(n - j_z, 0, j_x - j_z - 1))),
        pl.BlockSpec((tt, tn), lambda n, k: (on(n >= j_gt, k), jnp.clip(n - j_gt, 0, nn - j_gt - 1))),
        pl.BlockSpec((tt, 128), lambda n, k: (on(n == 0, k), 0)),
        pl.BlockSpec((4, tn), lambda n, k: (0, ctile(n))),
        pl.BlockSpec((1, tn), lambda n, k: (0, ctile(n)))]
    out_specs = [
        pl.BlockSpec((D_MODEL, tn), lambda n, k: (0, n)),
        pl.BlockSpec((D_MODEL, 128), lambda n, k: (0, 0)),
        pl.BlockSpec((tt, tn), lambda n, k: (after(n == 0, n > 0, k), 0)),
        pl.BlockSpec((tt, tn), lambda n, k: (after((n >= j_x) & (n < j_x + n_x), n >= j_x + n_x, k), xcol(n))),
        pl.BlockSpec((8, tn), lambda n, k: (0, 0)),
        pl.BlockSpec((8, tn), lambda n, k: (0, xcol(n)))]
    return _call(
        body, name="in_proj_wgrad", grid=(nn, nt), side=side, sem=("arbitrary", "arbitrary"),
        args=(h1t, proj, proj, proj, dxa, dxa, dxbc, dxbc, dlg, dz, dgates, ddt, cw, cb),
        in_specs=in_specs, out_specs=out_specs,
        out_shape=[jax.ShapeDtypeStruct((D_MODEL, PROJ_MAIN), BF16), jax.ShapeDtypeStruct((D_MODEL, 128), BF16),
                   jax.ShapeDtypeStruct((t, D_MODEL), BF16), jax.ShapeDtypeStruct((t, 2 * SSD_INNER), BF16),
                   jax.ShapeDtypeStruct((8, D_MODEL), F32), jax.ShapeDtypeStruct((8, 2 * SSD_INNER), F32)],
        scratch_shapes=[pltpu.VMEM((D_MODEL, tn), F32), pltpu.VMEM((D_MODEL, 128), F32)])


def _log1p(u):
    w = 1.0 + u
    return jnp.log(w) - ((w - 1.0) - u) / w


def _softplus(x):
    return jnp.maximum(x, 0.0) + _log1p(jnp.exp(-jnp.abs(x)))


def _head_mask(h):
    lane = lax.broadcasted_iota(jnp.int32, (1, SSD_GW), 1)
    return (lane >= SSD_P * h) & (lane < SSD_P * (h + 1))


def _pair(p):
    return slice(2 * SSD_P * p, 2 * SSD_P * (p + 1))


def _expand4(m, g):
    lane = lax.broadcasted_iota(jnp.int32, (1, SSD_GW), 1)
    col = lambda h: m[:, 4 * g + h:4 * g + h + 1]
    return jnp.where(lane < SSD_P, col(0), jnp.where(lane < 2 * SSD_P, col(1), jnp.where(lane < 3 * SSD_P, col(2), col(3))))


def _reduce4(v, g):
    lane = lax.broadcasted_iota(jnp.int32, (1, SSD_N), 1)
    out = jnp.zeros((v.shape[0], SSD_N), F32)
    for h in range(4):
        s = jnp.sum(jnp.where(_head_mask(h), v, 0.0), axis=1, keepdims=True)
        out = out + jnp.where(lane == 4 * g + h, s, 0.0)
    return out


def _ssd_heads(dtraw, hp, tri):
    xdt = dtraw + hp[0:1, :]
    dt = _softplus(xdt)
    cs = _dot_hi(tri, dt * hp[1:2, :])
    cs_last = cs[SSD_L - 1:SSD_L, :]
    return dict(xdt=xdt, dt=dt, cs=cs, cs_t=cs.T, e=jnp.exp(cs), w=jnp.exp(cs_last - cs), el=jnp.exp(cs_last))


def _ssd_group(g, hd, xs_b, bm_b, cm_b, d_x, st, paired=False):
    ll = SSD_L
    xs = xs_b.astype(F32)
    cs, cs_t = hd["cs"], hd["cs_t"]
    e_x, w_x, el_x, dt_x = _expand4(hd["e"], g), _expand4(hd["w"], g), _expand4(hd["el"], g), _expand4(hd["dt"], g)
    xd = xs * dt_x
    gcb = _dot_nt(cm_b, bm_b)
    ri = lax.broadcasted_iota(jnp.int32, (ll, ll), 0)
    ci = lax.broadcasted_iota(jnp.int32, (ll, ll), 1)
    dks, ms = [], []
    for h in range(4):
        k = 4 * g + h
        dk = jnp.exp(jnp.where(ri >= ci, cs[:, k:k + 1] - cs_t[k:k + 1, :], -1e30))
        dks.append(dk)
        ms.append((gcb * dk).astype(BF16))
    xdb = xd.astype(BF16)
    if paired:
        first = lax.broadcasted_iota(jnp.int32, (1, 2 * SSD_P), 1) < SSD_P
        ydiag = jnp.concatenate(
            [jnp.where(first, _dot(ms[2 * p], xdb[:, _pair(p)]), _dot(ms[2 * p + 1], xdb[:, _pair(p)]))
             for p in range(2)], axis=1)
    else:
        ydiag = jnp.zeros((ll, SSD_GW), F32)
        for h in range(4):
            ydiag = ydiag + _dot(ms[h], jnp.where(_head_mask(h), xd, 0.0).astype(BF16))
    yoff = _dot(cm_b, st.astype(BF16)) * e_x
    y = ydiag + yoff + d_x * xs
    st_new = st * el_x + _dot(bm_b.astype(F32).T.astype(BF16), (xd * w_x).astype(BF16))
    return dict(xs=xs, e_x=e_x, w_x=w_x, el_x=el_x, dt_x=dt_x, xd=xd, xdb=xdb, gcb=gcb, dks=dks, ms=ms, yoff=yoff, y=y,
                st_new=st_new)


def ssd_consts():
    hh = np.arange(SSD_N)
    tri = (hh[:, None] >= hh[None, :]).astype(np.float32)
    return jnp.asarray(tri), jnp.asarray(tri.T)


def ssd_params(dt_bias, a_log, d_skip, norm_w):
    padh = lambda v: jnp.pad(v.reshape(1, SSD_HEADS), ((0, 0), (0, SSD_N - SSD_HEADS)))
    hp = jnp.concatenate([padh(dt_bias), padh(-jnp.exp(a_log)), jnp.zeros((6, SSD_N), F32)], axis=0)
    lp = jnp.concatenate([norm_w.reshape(1, SSD_INNER), jnp.repeat(d_skip, SSD_P).reshape(1, SSD_INNER),
                          jnp.zeros((6, SSD_INNER), F32)], axis=0)
    return hp, lp


def _b_cols(g):
    return slice(SSD_INNER + g * SSD_N, SSD_INNER + (g + 1) * SSD_N)


def _c_cols(g):
    return slice(SSD_INNER + (SSD_G + g) * SSD_N, SSD_INNER + (SSD_G + g + 1) * SSD_N)


def _ssd_specs(nc, rc, cps=1):
    rows = cps * SSD_L
    return [pl.BlockSpec((rows, 2 * SSD_INNER), lambda b, c: (b * nc + rc(c), 0)),
            pl.BlockSpec((rows, SSD_INNER), lambda b, c: (b * nc + rc(c), C_Z // SSD_INNER)),
            pl.BlockSpec((rows, SSD_N), lambda b, c: (b * nc + rc(c), 0))]


def ssd_fwd(xbc, proj, dtraw, hp, lp, tri, nb, seq):
    t = xbc.shape[0]
    cps = SSD_FWD_CPS
    nc = seq // (cps * SSD_L)

    def body(xbc_ref, z_ref, dt_ref, hp_ref, lp_ref, tri_ref, y_ref, sts_ref, st_scr):
        @pl.when(pl.program_id(1) == 0)
        def _():
            st_scr[...] = jnp.zeros_like(st_scr)

        for cc in range(cps):
            rs = slice(cc * SSD_L, (cc + 1) * SSD_L)
            hd = _ssd_heads(dt_ref[rs, :], hp_ref[...], tri_ref[...])
            for g in range(SSD_G):
                gs = slice(g * SSD_GW, (g + 1) * SSD_GW)
                st = st_scr[g]
                sts_ref[cc, g] = st
                f = _ssd_group(g, hd, xbc_ref[rs, gs], xbc_ref[rs, _b_cols(g)], xbc_ref[rs, _c_cols(g)],
                               lp_ref[1:2, gs], st, paired=True)
                st_scr[g] = f["st_new"]
                zf = z_ref[rs, gs].astype(F32)
                yg = f["y"] * (zf * _sigmoid(zf))
                y_ref[rs, gs] = (yg * lax.rsqrt(_rowmean(yg * yg) + EPS) * lp_ref[0:1, gs]).astype(BF16)

    return pl.pallas_call(
        body, name="ssd_fwd", grid=(nb, nc),
        in_specs=_ssd_specs(nc, lambda c: c, cps) + [VMEM_FULL, VMEM_FULL, VMEM_FULL],
        out_specs=[pl.BlockSpec((cps * SSD_L, SSD_INNER), lambda b, c: (b * nc + c, 0)),
                   pl.BlockSpec((cps, SSD_G, SSD_N, SSD_GW), lambda b, c: (b * nc + c, 0, 0, 0))],
        out_shape=[jax.ShapeDtypeStruct((t, SSD_INNER), BF16),
                   jax.ShapeDtypeStruct((nb * nc * cps, SSD_G, SSD_N, SSD_GW), F32)],
        scratch_shapes=[pltpu.VMEM((SSD_G, SSD_N, SSD_GW), F32)],
        compiler_params=_cp("arbitrary", "arbitrary"),
    )(xbc, proj, dtraw, hp, lp, tri)


def ssd_bwd(xbc, proj, dtraw, hp, lp, tri, triu, states, dyn, nb, seq, side=None):
    t = xbc.shape[0]
    nc = seq // SSD_L
    ll = SSD_L

    def body(xbc_ref, z_ref, dt_ref, sts_ref, dy_ref, hp_ref, lp_ref, tri_ref, triu_ref,
             dxbc_ref, dz_ref, ddt_ref, hpg_ref, lpg_ref, dst_scr):
        b, c_i = pl.program_id(0), pl.program_id(1)

        @pl.when((b == 0) & (c_i == 0))
        def _():
            hpg_ref[...] = jnp.zeros_like(hpg_ref)
            lpg_ref[...] = jnp.zeros_like(lpg_ref)

        @pl.when(c_i == 0)
        def _():
            dst_scr[...] = jnp.zeros_like(dst_scr)

        hp = hp_ref[...]
        hd = _ssd_heads(dt_ref[...], hp, tri_ref[...])
        lane = lax.broadcasted_iota(jnp.int32, (1, SSD_N), 1)
        subl = lax.broadcasted_iota(jnp.int32, (SSD_N, 1), 0)
        dcs = jnp.zeros((ll, SSD_N), F32)
        dcs_t = jnp.zeros((SSD_N, ll), F32)
        last = jnp.zeros((1, SSD_N), F32)
        dxx = jnp.zeros((ll, SSD_N), F32)
        for g in range(SSD_G):
            gs = slice(g * SSD_GW, (g + 1) * SSD_GW)
            st = sts_ref[0, g]
            dst = dst_scr[g]
            bm_b, cm_b = xbc_ref[:, _b_cols(g)], xbc_ref[:, _c_cols(g)]
            d_x = lp_ref[1:2, gs]
            f = _ssd_group(g, hd, xbc_ref[:, gs], bm_b, cm_b, d_x, st)
            xs, xd, gcb = f["xs"], f["xd"], f["gcb"]
            e_x, w_x, el_x, dt_x = f["e_x"], f["w_x"], f["el_x"], f["dt_x"]
            stb, dstb = st.astype(BF16), dst.astype(BF16)
            zf = z_ref[:, gs].astype(F32)
            sg = _sigmoid(zf)
            sz = zf * sg
            yv = f["y"]
            yg = yv * sz
            rstd = lax.rsqrt(_rowmean(yg * yg) + EPS)
            n = yg * rstd
            dyn_v = dy_ref[:, gs].astype(F32)
            dn = dyn_v * lp_ref[0:1, gs]
            dyg = rstd * (dn - n * _rowmean(dn * n))
            dy = dyg * sz
            dz_ref[:, gs] = (dyg * yv * (sg * (1.0 + zf * (1.0 - sg)))).astype(BF16)
            dyb = dy.astype(BF16)
            r_ = _dot(bm_b, dstb)
            dxd = w_x * r_
            dqb = (dy * e_x).astype(BF16)
            dcm = _dot_nt(dqb, stb)
            dst_scr[g] = dst * el_x + _dot_tn(cm_b, dqb)
            dbm = _dot_nt((xd * w_x).astype(BF16), dstb)
            xdb = f["xdb"]
            dgm = jnp.zeros((ll, ll), F32)
            for h in range(4):
                k = 4 * g + h
                hm = _head_mask(h)
                dxd = dxd + jnp.where(hm, _dot_tn(f["ms"][h], dyb), 0.0)
                dm = _dot_nt(jnp.where(hm, dy, 0.0).astype(BF16), xdb) * f["dks"][h]
                dgm = dgm + dm
                dseg = dm * gcb
                dcs = dcs + jnp.where(lane == k, jnp.sum(dseg, axis=1, keepdims=True), 0.0)
                dcs_t = dcs_t + jnp.where(subl == k, jnp.sum(dseg, axis=0, keepdims=True), 0.0)
            dgmb = dgm.astype(BF16)
            dxbc_ref[:, _c_cols(g)] = (dcm + _dot(dgmb, bm_b)).astype(BF16)
            dxbc_ref[:, _b_cols(g)] = (dbm + _dot_tn(dgmb, cm_b)).astype(BF16)
            v = _reduce4(r_ * xd * w_x, g)
            dcs = dcs + _reduce4(dy * f["yoff"], g) - v
            last = last + _colsum(v) + _reduce4(_colsum(dst * st) * el_x, g)
            dxx = dxx + _reduce4(dxd * xs, g)
            dxbc_ref[:, gs] = (d_x * dy + dxd * dt_x).astype(BF16)
            lpg_ref[0:1, gs] += _colsum(dyn_v * n)
            lpg_ref[1:2, gs] += _colsum(dy * xs)
        rowi = lax.broadcasted_iota(jnp.int32, (ll, 1), 0)
        da = _dot_hi(triu_ref[...], dcs - dcs_t.T + jnp.where(rowi == ll - 1, last, 0.0))
        ddt = (dxx + da * hp[1:2, :]) * _sigmoid(hd["xdt"])
        ddt_ref[...] = ddt
        hpg_ref[...] += jnp.concatenate([_colsum(ddt), _colsum(da * hd["dt"]), jnp.zeros((6, SSD_N), F32)], axis=0)

    rc = lambda c: nc - 1 - c
    return _call(
        body, name="ssd_bwd", grid=(nb, nc), side=side, sem=("arbitrary", "arbitrary"),
        args=(xbc, proj, dtraw, states, dyn, hp, lp, tri, triu),
        in_specs=_ssd_specs(nc, rc) + [
            pl.BlockSpec((1, SSD_G, SSD_N, SSD_GW), lambda b, c: (b * nc + rc(c), 0, 0, 0)),
            pl.BlockSpec((SSD_L, SSD_INNER), lambda b, c: (b * nc + rc(c), 0)),
            VMEM_FULL, VMEM_FULL, VMEM_FULL, VMEM_FULL],
        out_specs=[pl.BlockSpec((SSD_L, 2 * SSD_INNER), lambda b, c: (b * nc + rc(c), 0)),
                   pl.BlockSpec((SSD_L, SSD_INNER), lambda b, c: (b * nc + rc(c), 0)),
                   pl.BlockSpec((SSD_L, SSD_N), lambda b, c: (b * nc + rc(c), 0)),
                   pl.BlockSpec((8, SSD_N), lambda b, c: (0, 0)),
                   pl.BlockSpec((8, SSD_INNER), lambda b, c: (0, 0))],
        out_shape=[jax.ShapeDtypeStruct((t, 2 * SSD_INNER), BF16), jax.ShapeDtypeStruct((t, SSD_INNER), BF16),
                   jax.ShapeDtypeStruct((t, SSD_N), F32), jax.ShapeDtypeStruct((8, SSD_N), F32),
                   jax.ShapeDtypeStruct((8, SSD_INNER), F32)],
        scratch_shapes=[pltpu.VMEM((SSD_G, SSD_N, SSD_GW), F32)])


def ada_fwd(c_all, w_cols, b_cols):
    def body(c_ref, w_ref, b_ref, o_ref):
        cv = c_ref[...]
        o_ref[...] = _dot_hi(cv * _sigmoid(cv), w_ref[...]) + b_ref[...]

    return pl.pallas_call(body, name="ada_fwd", out_shape=jax.ShapeDtypeStruct((c_all.shape[0], w_cols.shape[1]), F32),
                          compiler_params=pltpu.CompilerParams(vmem_limit_bytes=VMEM_LIMIT))(c_all, w_cols, b_cols)


def ada_bwd(c_all, dmod_cols, dmod_all):
    def body(c_ref, dc_ref, da_ref, gw_ref, gb_ref):
        cv = c_ref[...]
        gw_ref[...] = lax.dot_general(cv * _sigmoid(cv), dc_ref[...], (((0,), (0,)), ((), ())),
                                      precision=lax.Precision.HIGHEST, preferred_element_type=F32)
        gb_ref[...] = _colsum(da_ref[...])

    return pl.pallas_call(
        body, name="ada_bwd",
        out_shape=[jax.ShapeDtypeStruct((c_all.shape[1], dmod_cols.shape[1]), F32),
                   jax.ShapeDtypeStruct((1, dmod_all.shape[1]), F32)],
        compiler_params=pltpu.CompilerParams(vmem_limit_bytes=VMEM_LIMIT))(c_all, dmod_cols, dmod_all)


def _adam_update(g, w, m, v):
    m2 = ADAM_B1 * m + (1.0 - ADAM_B1) * g
    v2 = ADAM_B2 * v + (1.0 - ADAM_B2) * (g * g)
    m_hat = m2 / (1.0 - ADAM_B1 ** ADAM_STEP)
    v_hat = v2 / (1.0 - ADAM_B2 ** ADAM_STEP)
    return -ADAM_LR * (m_hat / (jnp.sqrt(v_hat) + ADAM_EPS) + ADAM_WD * w), m2, v2


def adamw(parts, w, m, v, name):
    n, r, c = parts.shape
    tr = r if r <= 256 else 128

    def body(p_ref, w_ref, m_ref, v_ref, g_ref, d_ref, nm_ref, nv_ref):
        g = p_ref[0].astype(F32)
        for s in range(1, n):
            g = g + p_ref[s].astype(F32)
        g_ref[0] = g
        d_ref[0], nm_ref[0], nv_ref[0] = _adam_update(g, w_ref[0], m_ref[0], v_ref[0])

    blk = pl.BlockSpec((1, tr, c), lambda i: (0, i, 0))
    return pl.pallas_call(
        body, name=name, grid=(r // tr,),
        in_specs=[pl.BlockSpec((n, tr, c), lambda i: (0, i, 0)), blk, blk, blk], out_specs=[blk] * 4,
        out_shape=[jax.ShapeDtypeStruct((1, r, c), F32)] * 4,
        compiler_params=_cp("parallel"),
    )(parts, w, m, v)


SMALL_SRC = {
    'pre_norm1': ('vin', 0, 1024), 'post_norm1': ('vmg', 1, 1024), 'b_gate': ('vmg', 0, 2048),
    'lru_conv_b': ('accl', 4, 1024), 'lru_wa': ('gwa', None, None), 'lru_ba': ('dvec', 0, 1024),
    'lru_wx': ('gwx', None, None), 'lru_bx': ('dvec', 1, 1024), 'lru_lambda': ('dvec', 2, 1024),
    'ssd_conv_b': ('accs', 4, 4096), 'ssd_dt_bias': ('hpg', 0, SSD_HEADS), 'ssd_a_log': ('hpg', 1, SSD_HEADS),
    'ssd_d': ('lpg', 1, SSD_INNER), 'ssd_norm_w': ('lpg', 0, SSD_INNER), 'pre_norm2': ('vmlp', 0, 1024),
    'post_norm2': ('vmlp', 1, 1024)}
SMALL_ACCS = ('vin', 'vmg', 'vmlp', 'dvec', 'accl', 'accs', 'hpg', 'lpg', 'gwa', 'gwx')
SMALL_RIDE = ('vmg', 'vmlp', 'dvec', 'hpg', 'lpg', 'gwa', 'gwx')


def adamw_small(gathered, params):
    names = tuple(params)
    na = len(SMALL_ACCS)

    def body(*refs):
        acc = {k: functools.reduce(lambda p, q: p + q, [refs[i][s] for s in range(NDEV)])
               for i, k in enumerate(SMALL_ACCS)}
        ins = refs[na:na + 3 * len(names)]
        outs = refs[na + 3 * len(names):]
        for j, k in enumerate(names):
            w_ref, m_ref, v_ref = ins[3 * j:3 * j + 3]
            src, row, width = SMALL_SRC[k]
            wv = w_ref[...]
            if row is None:
                g = acc[src]
            elif k == 'ssd_d':
                li = lax.broadcasted_iota(jnp.int32, (SSD_INNER, SSD_N), 0)
                hi = lax.broadcasted_iota(jnp.int32, (SSD_INNER, SSD_N), 1)
                g = _dot_hi(acc[src], jnp.where(jnp.right_shift(li, 6) == hi, 1.0, 0.0))[row:row + 1, :SSD_HEADS]
            else:
                g = acc[src][row:row + 1, :width]
            if k == 'lru_lambda':
                g = g * (-1.0 / (1.0 + jnp.exp(wv)))
            if k == 'ssd_a_log':
                g = g * (-jnp.exp(wv))
            o = outs[4 * j:4 * j + 4]
            o[0][...] = g
            o[1][...], o[2][...], o[3][...] = _adam_update(g, wv, m_ref[...], v_ref[...])
        outs[-2][...] = acc['accl'][0:4, :]
        outs[-1][...] = acc['accs'][0:4, :]

    flat = [a for k in names for a in params[k]]
    out_shape = [jax.ShapeDtypeStruct(params[k][0].shape, F32) for k in names for _ in range(4)]
    out_shape += [jax.ShapeDtypeStruct((4, D_MODEL), F32), jax.ShapeDtypeStruct((4, 2 * SSD_INNER), F32)]
    res = pl.pallas_call(body, name="adamw_small", out_shape=out_shape,
                         compiler_params=pltpu.CompilerParams(vmem_limit_bytes=VMEM_LIMIT))(
        *[gathered[k] for k in SMALL_ACCS], *flat)
    return {k: res[4 * j:4 * j + 4] for j, k in enumerate(names)}, res[-2], res[-1]


def _dev_index(px, py, pc):
    return 4 * px + 2 * py + pc


class _Exchange:
    def __init__(self, arrs):
        self.arrs = list(arrs)
        self.na = len(self.arrs)
        self.scratch = [pltpu.SemaphoreType.DMA((7 * self.na,)), pltpu.SemaphoreType.DMA((7 * self.na,)),
                        pltpu.SemaphoreType.DMA((self.na,))]


class Gather(_Exchange):
    def __init__(self, arrs):
        super().__init__(arrs)
        self.out_shape = [jax.ShapeDtypeStruct((NDEV,) + a.shape, a.dtype) for a in self.arrs]

    def _plan(self, ins, outs, sems):
        na = self.na
        send_sems, recv_sems, local_sems = sems
        x, y, c = lax.axis_index("x"), lax.axis_index("y"), lax.axis_index("c")
        me, sibling = (x, y, c), (x, y, 1 - c)
        chips = [(1 - x, y), (x, 1 - y), (1 - x, 1 - y)]

        def copy(a, k, block, to, src=None):
            dst = outs[a].at[_dev_index(*block)]
            return pltpu.make_async_remote_copy(
                src_ref=dst if src is None else src, dst_ref=dst, send_sem=send_sems.at[a * 7 + k],
                recv_sem=recv_sems.at[a * 7 + k], device_id=to, device_id_type=MESH)

        mine = [pltpu.make_async_copy(ins[a], outs[a].at[_dev_index(*me)], local_sems.at[a]) for a in range(na)]
        first = []
        for a in range(na):
            first.append(copy(a, 0, me, sibling, src=ins[a]))
            first += [copy(a, 1 + j, me, (*chip, c), src=ins[a]) for j, chip in enumerate(chips)]
        return copy, mine, first, me, sibling, chips, c

    def start(self, ins, outs, sems):
        _, mine, first, *_ = self._plan(ins, outs, sems)
        for cp in mine + first:
            cp.start()

    def finish(self, ins, outs, sems):
        copy, mine, first, me, sibling, chips, c = self._plan(ins, outs, sems)
        passed = []
        for j, chip in enumerate(chips):
            for a in range(self.na):
                copy(a, 1 + j, (*chip, c), me).wait_recv()
                cp = copy(a, 4 + j, (*chip, c), sibling)
                cp.start()
                passed.append(cp)
        for a in range(self.na):
            copy(a, 0, sibling, me).wait_recv()
            for j, chip in enumerate(chips):
                copy(a, 4 + j, (*chip, 1 - c), me).wait_recv()
        for cp in first + passed:
            cp.wait_send()
        for cp in mine:
            cp.wait()


class Scatter(_Exchange):
    def __init__(self, arrs):
        super().__init__(arrs)
        self.out_shape = [jax.ShapeDtypeStruct(a.shape, a.dtype) for a in self.arrs]

    def _plan(self, ins, outs, sems, arrivals):
        send_sems, recv_sems, local_sems = sems
        x, y, c = lax.axis_index("x"), lax.axis_index("y"), lax.axis_index("c")
        me = _dev_index(x, y, c)
        masks = [(mx, my, mc) for mx in (0, 1) for my in (0, 1) for mc in (0, 1)][1:]
        flip = lambda v, bit: 1 - v if bit else v
        mine = [pltpu.make_async_copy(ins[a].at[me], outs[a].at[me], local_sems.at[a]) for a in range(self.na)]
        sends, recvs = [], []
        for k, (mx, my, mc) in enumerate(masks):
            peer = (flip(x, mx), flip(y, my), flip(c, mc))
            pidx = _dev_index(*peer)
            for a in range(self.na):
                on = dict(send_sem=send_sems.at[a * 7 + k], recv_sem=recv_sems.at[a * 7 + k], device_id=peer,
                          device_id_type=MESH)
                sends.append(pltpu.make_async_remote_copy(src_ref=ins[a].at[pidx], dst_ref=outs[a].at[me], **on))
                if arrivals:
                    recvs.append(pltpu.make_async_remote_copy(src_ref=ins[a].at[pidx], dst_ref=outs[a].at[pidx], **on))
        return mine, sends, recvs

    def start(self, ins, outs, sems):
        mine, sends, _ = self._plan(ins, outs, sems, arrivals=False)
        for cp in mine + sends:
            cp.start()

    def finish(self, ins, outs, sems):
        mine, sends, recvs = self._plan(ins, outs, sems, arrivals=True)
        for cp in recvs:
            cp.wait_recv()
        for cp in sends:
            cp.wait_send()
        for cp in mine:
            cp.wait()


def exchange_call(ex, name):
    na = ex.na

    def body(*refs):
        ins, outs, sems = refs[:na], refs[na:2 * na], refs[2 * na:]
        ex.start(ins, outs, sems)
        ex.finish(ins, outs, sems)

    return pl.pallas_call(body, name=name, in_specs=[ANY] * na, out_specs=[ANY] * na, out_shape=ex.out_shape,
                          scratch_shapes=ex.scratch)(*ex.arrs)


def all_gather(arrs, name):
    return exchange_call(Gather(arrs), name)


def _call(body, *, name, grid, in_specs, out_specs, out_shape, scratch_shapes=(), sem, args, side=None):
    if side is None:
        outs = pl.pallas_call(body, name=name, grid=grid, in_specs=list(in_specs), out_specs=list(out_specs),
                              out_shape=list(out_shape), scratch_shapes=list(scratch_shapes),
                              compiler_params=_cp(*sem))(*args)
        return outs, []
    ni, no, ns, na = len(in_specs), len(out_specs), len(scratch_shapes), side.na

    def wrapped(*refs):
        ins, s_in = refs[:ni], refs[ni:ni + na]
        outs, s_out = refs[ni + na:ni + na + no], refs[ni + na + no:ni + 2 * na + no]
        scr, sems = refs[ni + 2 * na + no:ni + 2 * na + no + ns], refs[ni + 2 * na + no + ns:]
        pids = [pl.program_id(i) for i in range(len(grid))]
        first = functools.reduce(lambda p, q: p & q, [p == 0 for p in pids])
        last = functools.reduce(lambda p, q: p & q, [p == g - 1 for p, g in zip(pids, grid)])

        @pl.when(first)
        def _():
            side.start(s_in, s_out, sems)

        body(*ins, *outs, *scr)

        @pl.when(last)
        def _():
            side.finish(s_in, s_out, sems)

    outs = pl.pallas_call(
        wrapped, name=name, grid=grid, in_specs=list(in_specs) + [ANY] * na, out_specs=list(out_specs) + [ANY] * na,
        out_shape=list(out_shape) + side.out_shape, scratch_shapes=list(scratch_shapes) + side.scratch,
        compiler_params=_cp(*["arbitrary"] * len(grid)))(*args, *side.arrs)
    return outs[:no], outs[no:]


WEIGHTS = ('w_ada', 'b_ada', 'pre_norm1', 'post_norm1', 'w_in', 'b_gate', 'lru_conv_w', 'lru_conv_b', 'lru_wa',
           'lru_ba', 'lru_wx', 'lru_bx', 'lru_lambda', 'w_pa', 'ssd_conv_w', 'ssd_conv_b', 'ssd_dt_bias', 'ssd_a_log',
           'ssd_d', 'ssd_norm_w', 'w_pb', 'w_out', 'pre_norm2', 'post_norm2', 'w_ff1', 'w_ff2')
BIG = ('w_in', 'w_pa', 'w_pb', 'w_out', 'w_ff1', 'w_ff2')
REPL = ('pre_norm1', 'post_norm1', 'b_gate', 'lru_conv_b', 'lru_wa', 'lru_ba', 'lru_wx', 'lru_bx', 'lru_lambda',
        'ssd_conv_b', 'ssd_dt_bias', 'ssd_a_log', 'ssd_d', 'ssd_norm_w', 'pre_norm2', 'post_norm2')
LANES = 1024


def _rows(n):
    return -(-n // LANES)


def _pack(vals, total_rows):
    parts = []
    for v in vals:
        f = v.reshape(-1).astype(F32)
        parts.append(jnp.pad(f, (0, _rows(f.shape[0]) * LANES - f.shape[0])))
    flat = jnp.concatenate(parts)
    return jnp.pad(flat.reshape(-1, LANES), ((0, total_rows - flat.shape[0] // LANES), (0, 0)))


def _unpack(slab, shapes):
    out, r = [], 0
    for s in shapes:
        n = int(np.prod(s))
        out.append(slab[r:r + _rows(n)].reshape(-1)[:n].reshape(s))
        r += _rows(n)
    return out


def _block_diag4(w):
    w4 = w.reshape(4, 4, 64, 64)
    eye = jnp.eye(4, dtype=w.dtype)
    return (w4[:, :, :, None, :] * eye[None, :, None, :, None]).reshape(4, LRU_BLOCK, LRU_BLOCK)


def _diag_blocks4(m):
    m5 = m.reshape(4, 4, 64, 4, 64)
    return jnp.stack([m5[:, a, :, a, :] for a in range(4)], axis=1).reshape(LRU_HEADS, 64, 64)


def kernel(x, c, w_ada, b_ada, pre_norm1, post_norm1, w_in, b_gate, lru_conv_w, lru_conv_b, lru_wa, lru_ba, lru_wx, lru_bx, lru_lambda, w_pa, ssd_conv_w, ssd_conv_b, ssd_dt_bias, ssd_a_log, ssd_d, ssd_norm_w, w_pb, w_out, pre_norm2, post_norm2, w_ff1, w_ff2, loss_target, m_w_ada, m_b_ada, m_pre_norm1, m_post_norm1, m_w_in, m_b_gate, m_lru_conv_w, m_lru_conv_b, m_lru_wa, m_lru_ba, m_lru_wx, m_lru_bx, m_lru_lambda, m_w_pa, m_ssd_conv_w, m_ssd_conv_b, m_ssd_dt_bias, m_ssd_a_log, m_ssd_d, m_ssd_norm_w, m_w_pb, m_w_out, m_pre_norm2, m_post_norm2, m_w_ff1, m_w_ff2, v_w_ada, v_b_ada, v_pre_norm1, v_post_norm1, v_w_in, v_b_gate, v_lru_conv_w, v_lru_conv_b, v_lru_wa, v_lru_ba, v_lru_wx, v_lru_bx, v_lru_lambda, v_w_pa, v_ssd_conv_w, v_ssd_conv_b, v_ssd_dt_bias, v_ssd_a_log, v_ssd_d, v_ssd_norm_w, v_w_pb, v_w_out, v_pre_norm2, v_post_norm2, v_w_ff1, v_w_ff2):
    given = dict(locals())
    w = {k: given[k] for k in WEIGHTS}
    mom = {k: given["m_" + k] for k in WEIGHTS}
    var = {k: given["v_" + k] for k in WEIGHTS}
    nb, seq, _ = x.shape
    assert nb == 2 and seq % 512 == 0, (nb, seq)
    t = nb * seq
    me = _dev_index(lax.axis_index("x"), lax.axis_index("y"), lax.axis_index("c"))
    x2 = x.reshape(t, D_MODEL)
    tgt2 = loss_target.reshape(t, D_MODEL)
    ada_cols = w_ada.shape[2]

    slab = jnp.zeros((16, LANES), F32)
    slab = slab.at[0:nb].set(c)
    slab = slab.at[2:6, 0:lru_conv_w.shape[2]].set(lru_conv_w[0])
    slab = slab.at[6:10, 0:ssd_conv_w.shape[2]].set(ssd_conv_w[0])
    g1, gw_in = all_gather([slab, w_in.astype(BF16)], "gather_cond_w_in")
    c_all = g1[:, 0:nb].reshape(NDEV * nb, D_MODEL)
    lru_cw = g1[:, 2:6, 0:lru_conv_w.shape[2]].transpose(1, 0, 2).reshape(4, D_MODEL)
    ssd_cw = g1[:, 6:10, 0:ssd_conv_w.shape[2]].transpose(1, 0, 2).reshape(4, 2 * SSD_INNER)
    b_cols = lax.dynamic_slice(b_ada, (0, me * ada_cols), (1, ada_cols))
    mod_cols = ada_fwd(c_all, w_ada[0], b_cols)
    (g2,) = all_gather([mod_cols], "gather_mod")
    mod_all = g2.transpose(1, 0, 2).reshape(NDEV * nb, N_MOD * D_MODEL)
    mod_mine = lax.dynamic_slice(mod_all, (me * nb, 0), (nb, N_MOD * D_MODEL)).reshape(nb, N_MOD, D_MODEL)
    mod8 = jnp.pad(mod_mine, ((0, 0), (0, 8 - N_MOD), (0, 0)))

    shard = IN_DIM // NDEV
    kd, od = DT_COL0 // shard, DT_COL0 % shard
    assert od + SSD_HEADS <= shard
    gb = gw_in[:, 0]
    w_main = jnp.concatenate([gb[k] for k in range(kd)] + [gb[kd][:, :od], gb[kd][:, od + SSD_HEADS:]]
                             + [gb[k] for k in range(kd + 1, NDEV)], axis=1)
    w_dt = jnp.pad(gb[kd][:, od:od + SSD_HEADS], ((0, 0), (0, 128 - SSD_HEADS)))

    wa_bd = _block_diag4(lru_wa[0]).astype(BF16)
    wx_bd = _block_diag4(lru_wx[0]).astype(BF16)
    lam = lru_lambda[0]
    vec = _pack([lru_ba, lru_bx, jax.nn.softplus(-lam)], 8)
    tri, triu = ssd_consts()
    hp, lp = ssd_params(ssd_dt_bias[0], ssd_a_log[0], ssd_d[0], ssd_norm_w[0])

    rest = Gather([w[k].astype(BF16) for k in BIG[1:]])
    cw_all = jnp.concatenate([lru_cw, ssd_cw], axis=1)
    cb_all = jnp.concatenate([lru_conv_b, ssd_conv_b], axis=1)
    (proj, h1t, dtraw, xa, xbc), gw = in_proj_fwd(x2, mod8, pre_norm1, w_main, w_dt, cw_all, cb_all, seq, side=rest)
    w_pa_f = gw[0].reshape(D_MODEL, D_MODEL)
    w_pb_f = gw[1].reshape(SSD_INNER, D_MODEL)
    w_out_f = gw[2].reshape(D_MODEL, D_MODEL)
    w_ff1_f = gw[3][:, 0]
    w_ff2_f = gw[4].reshape(D_FF, D_MODEL)
    ya_in, hst = lru_fwd(xa, proj, wa_bd, wx_bd, vec, nb, seq)
    yb_in, states = ssd_fwd(xbc, proj, dtraw, hp, lp, tri, nb, seq)
    yab, out1, x1 = merge_fwd(ya_in, yb_in, proj, x2, mod8, b_gate, post_norm1, w_pa_f, w_pb_f, w_out_f, seq)

    dx1, h2, da1, act, dy2, loss8, vacc_mlp, dmod_mlp = mlp_fwd_bwd(
        x1, tgt2, mod8, pre_norm2, post_norm2, w_ff1_f, w_ff2_f, nb, seq)
    wg = dict(out_dtype=BF16, ta=True, tm=1024, tn=1024, tk=1024)
    dw_ff1 = matmul(h2, da1, name="wgrad_ff1", blocked_out=D_FF // NDEV, **wg)
    dw_ff2 = matmul(act, dy2, name="wgrad_ff2", **wg)
    dya_in, dyb_in, dgates, dyab, dout1, merged, vacc_mg, dmod_mg = merge_bwd(
        dx1, out1, yab, proj, mod8, b_gate, post_norm1, w_pa_f, w_pb_f, w_out_f, nb, seq)
    dw_out = matmul(merged, dout1, name="wgrad_out", **wg)
    dw_pa = matmul(ya_in, dyab, name="wgrad_pa", n=D_MODEL, b_off=0, **wg)
    dw_pb = matmul(yb_in, dyab, name="wgrad_pb", n=D_MODEL, b_off=1, **wg)
    by_rows = lambda g: g.reshape(NDEV, g.shape[0] // NDEV, g.shape[1])
    (dxa, dlg, dwa_bd, dwx_bd, dvec), parts_ff = lru_bwd(
        dya_in, xa, proj, hst, wa_bd, wx_bd, vec, nb, seq, side=Scatter([dw_ff1, by_rows(dw_ff2)]))
    (dxbc, dz, ddt, hpg, lpg), parts_mg = ssd_bwd(xbc, proj, dtraw, hp, lp, tri, triu, states, dyb_in, nb, seq,
                                                  side=Scatter([by_rows(dw_pa), by_rows(dw_pb), by_rows(dw_out)]))
    ddt_b = ddt.astype(BF16)
    accs = dict(vmg=vacc_mg, vmlp=vacc_mlp, dvec=dvec, hpg=hpg, lpg=lpg,
                gwa=_diag_blocks4(dwa_bd).reshape(LRU_HEADS * 64, 64), gwx=_diag_blocks4(dwx_bd).reshape(LRU_HEADS * 64, 64))
    (dw_main, dw_dt, dlx, dxr, acc_l, acc_s), g_small = in_proj_wgrad(
        h1t, proj, dxa, dxbc, dlg, dz, dgates, ddt_b, cw_all, cb_all, seq, side=Gather([accs[k] for k in SMALL_RIDE]))
    pieces = (dlx, dlg, dz, dxr, dgates)
    cut = lambda k: dw_main[:, k * shard - (SSD_HEADS if k > kd else 0):(k + 1) * shard - (SSD_HEADS if k >= kd else 0)]
    blk_dt = jnp.concatenate([dw_main[:, kd * shard:DT_COL0], dw_dt[:, :SSD_HEADS],
                              dw_main[:, DT_COL0:(kd + 1) * shard - SSD_HEADS]], axis=1)
    dw_blocks = jnp.stack([blk_dt if k == kd else cut(k) for k in range(NDEV)])
    (grad_x, vacc_in, dmod_in), parts_in = in_proj_bwd(pieces, ddt_b, dx1, x2, mod8, pre_norm1, w_main, w_dt, nb, seq,
                                                       side=Scatter([dw_blocks]))
    parts = dict(zip(BIG, (parts_in[0], *parts_mg, *parts_ff)))

    dmod = (dmod_in + dmod_mg + dmod_mlp)[:, :N_MOD].reshape(nb, N_MOD * D_MODEL)
    g3, g_vin, g_accl, g_accs = all_gather([jnp.pad(dmod, ((0, 8 - nb), (0, 0))), vacc_in, acc_l, acc_s], "gather_dmod")
    dmod_all = g3[:, :nb].reshape(NDEV * nb, N_MOD * D_MODEL)
    dmod_cols = lax.dynamic_slice(dmod_all, (0, me * ada_cols), (NDEV * nb, ada_cols))
    g_w_ada, g_b_ada = ada_bwd(c_all, dmod_cols, dmod_all)

    res = {}
    for k in BIG:
        res[k] = adamw(parts[k], w[k], mom[k], var[k], "adamw_" + k)
    res['w_ada'] = adamw(g_w_ada[None], w_ada, m_w_ada, v_w_ada, "adamw_w_ada")

    gathered = dict(zip(SMALL_RIDE, g_small), vin=g_vin, accl=g_accl, accs=g_accs)
    view = lambda a: a.reshape(-1, a.shape[-1])
    res_a, g_lru_cw, g_ssd_cw = adamw_small(gathered, {k: (view(w[k]), view(mom[k]), view(var[k])) for k in REPL})
    res.update(res_a)
    lcw, scw = lru_conv_w.shape[2], ssd_conv_w.shape[2]
    sharded = {'b_ada': g_b_ada[None], 'lru_conv_w': lax.dynamic_slice(g_lru_cw, (0, me * lcw), (4, lcw))[None],
               'ssd_conv_w': lax.dynamic_slice(g_ssd_cw, (0, me * scw), (4, scw))[None]}
    for k, g in sharded.items():
        as3 = lambda a: a.reshape(g.shape)
        res[k] = adamw(g, as3(w[k]), as3(mom[k]), as3(var[k]), "adamw_" + k)

    loss = lax.psum(loss8[0, 0], ("x", "y", "c"))
    outs = [[res[k][j].reshape(w[k].shape) for k in WEIGHTS] for j in range(4)]
    return (loss, grad_x.reshape(x.shape), *outs[0], *outs[1], *outs[2], *outs[3])
```

```python
import functools

import numpy as np
import jax
import jax.numpy as jnp
from jax import lax
from jax.experimental import pallas as pl
from jax.experimental.pallas import tpu as pltpu

F32 = jnp.float32
BF16 = jnp.bfloat16

D_MODEL = 1024
LRU_HEADS = 16
LRU_BLOCK = 256
LRU_C = 8.0
SSD_INNER = 2048
SSD_HEADS = 32
SSD_P = 64
SSD_G = 8
SSD_N = 128
SSD_L = 128
SSD_GW = SSD_INNER // SSD_G
D_FF = 4096
N_MOD = 6
EPS = 1e-6
NDEV = 8

C_LRU_X, C_LRU_G, C_Z, C_XBC, C_GATES, PROJ_MAIN = 0, 1024, 2048, 4096, 8192, 10240
IN_DIM = 10272
DT_COL0 = 8192
HALO = 16
SSD_FWD_CPS = 1
HT_TOK = 512

ADAM_LR, ADAM_B1, ADAM_B2, ADAM_EPS, ADAM_WD, ADAM_STEP = 0.001, 0.9, 0.999, 1e-08, 0.01, 10

VMEM_LIMIT = 60 * 1024 * 1024
MESH = pl.DeviceIdType.MESH
ANY = pl.BlockSpec(memory_space=pl.ANY)
VMEM_FULL = pl.BlockSpec(memory_space=pltpu.VMEM)


def _cp(*sem):
    return pltpu.CompilerParams(dimension_semantics=sem, vmem_limit_bytes=VMEM_LIMIT)


def _dot(a, b):
    return jnp.dot(a, b, preferred_element_type=F32)


def _dot_nt(a, b):
    return lax.dot_general(a, b, (((1,), (1,)), ((), ())), preferred_element_type=F32)


def _dot_tn(a, b):
    return lax.dot_general(a, b, (((0,), (0,)), ((), ())), preferred_element_type=F32)


def _dot_hi(a, b):
    return jnp.dot(a, b, precision=lax.Precision.HIGHEST, preferred_element_type=F32)


def _sigmoid(x):
    return 1.0 / (1.0 + jnp.exp(-x))


def _gelu_and_grad(x):
    k0, k1 = 0.7978845608028654, 0.044715
    t = jnp.tanh(k0 * (x + k1 * x * x * x))
    g = 0.5 * x * (1.0 + t)
    dg = 0.5 * (1.0 + t) + 0.5 * x * (1.0 - t * t) * k0 * (1.0 + 3.0 * k1 * x * x)
    return g, dg


def _neg_expm1(y):
    p = 1.0 + y * (1.0 / 7.0)
    p = 1.0 + y * (1.0 / 6.0) * p
    p = 1.0 + y * (1.0 / 5.0) * p
    p = 1.0 + y * (1.0 / 4.0) * p
    p = 1.0 + y * (1.0 / 3.0) * p
    p = 1.0 + y * 0.5 * p
    return jnp.where(y > -0.3, -y * p, 1.0 - jnp.exp(y))


def _colsum(v):
    return jnp.sum(v, axis=0, keepdims=True)


def _rowmean(v):
    return jnp.mean(v, axis=-1, keepdims=True)


def matmul(a, b, *, ta=False, tb=False, out_dtype=F32, tm, tn, tk, name, n=None, b_off=0, blocked_out=False):
    m = a.shape[1] if ta else a.shape[0]
    kdim = a.shape[0] if ta else a.shape[1]
    n = n or (b.shape[0] if tb else b.shape[1])
    tm, tn, tk = min(tm, m), min(tn, n), min(tk, kdim)
    nk = kdim // tk
    dn = (((0 if ta else 1,), (1 if tb else 0,)), ((), ()))
    bw = blocked_out or tn

    def body(a_ref, b_ref, o_ref, acc_ref):
        k = pl.program_id(2)
        p = lax.dot_general(a_ref[...], b_ref[...], dn, preferred_element_type=F32)

        def emit(v):
            if blocked_out:
                for q in range(tn // bw):
                    o_ref[q] = v[:, q * bw:(q + 1) * bw].astype(out_dtype)
            else:
                o_ref[...] = v.astype(out_dtype)

        if nk == 1:
            emit(p)
        else:
            @pl.when(k == 0)
            def _():
                acc_ref[...] = p

            @pl.when(k > 0)
            def _():
                acc_ref[...] += p

            @pl.when(k == nk - 1)
            def _():
                emit(acc_ref[...])

    a_spec = pl.BlockSpec((tk, tm), lambda i, j, k: (k, i)) if ta else pl.BlockSpec((tm, tk), lambda i, j, k: (i, k))
    b_spec = (pl.BlockSpec((tn, tk), lambda i, j, k: (j, k)) if tb
              else pl.BlockSpec((tk, tn), lambda i, j, k: (k, j + b_off)))
    if blocked_out:
        o_spec, o_shape = pl.BlockSpec((tn // bw, tm, bw), lambda i, j, k: (j, i, 0)), (n // bw, m, bw)
    else:
        o_spec, o_shape = pl.BlockSpec((tm, tn), lambda i, j, k: (i, j)), (m, n)
    return pl.pallas_call(
        body, name=name, grid=(m // tm, n // tn, nk),
        in_specs=[a_spec, b_spec], out_specs=o_spec,
        out_shape=jax.ShapeDtypeStruct(o_shape, out_dtype),
        scratch_shapes=[pltpu.VMEM((tm, tn), F32)],
        compiler_params=_cp("parallel", "parallel", "arbitrary"),
    )(a, b)


def _conv_tile(j, tn):
    return jnp.where(j == 0, 0, jnp.clip(j - C_XBC // tn + 1, 1, 2 * SSD_INNER // tn))


def in_proj_fwd(x2, mod8, pre1, w_main, w_dt, cw, cb, seq, side=None):
    t = x2.shape[0]
    tm = min(1024, seq)
    tn = 1024
    per_seq = seq // tm
    j_xbc = C_XBC // tn
    n_xbc = 2 * SSD_INNER // tn
    cs = 256

    def body(x_ref, mod_ref, pre_ref, w_ref, wdt_ref, cw_ref, cb_ref, proj_ref, h_ref, dt_ref, xa_ref, xbc_ref,
             h_scr, carry_scr):
        i, j = pl.program_id(0), pl.program_id(1)

        @pl.when(j == 0)
        def _():
            xv = x_ref[...]
            y = xv * lax.rsqrt(_rowmean(xv * xv) + EPS) * pre_ref[...]
            m = mod_ref[0]
            hf = y * (1.0 + m[1:2, :]) + m[0:1, :]
            h = hf.astype(BF16)
            h_scr[...] = h
            hft = hf.T.astype(BF16)
            for q in range(tm // HT_TOK):
                h_ref[q] = hft[:, q * HT_TOK:(q + 1) * HT_TOK]
            dt_ref[...] = _dot(h, wdt_ref[...])

        def project(c0=0, width=tn):
            pb = _dot(h_scr[...], w_ref[:, c0:c0 + width]).astype(BF16)
            proj_ref[:, c0:c0 + width] = pb
            return pb

        def conv(o_ref, slot, act):
            first = lax.rem(i, per_seq) == 0
            for c0 in range(0, tn, cs):
                cur = project(c0, cs).astype(F32)
                prev = jnp.where(first, 0.0, carry_scr[slot, :, c0:c0 + cs])
                carry_scr[slot, :, c0:c0 + cs] = cur[tm - HALO:, :]
                xx = jnp.concatenate([prev, cur], axis=0)
                w = cw_ref[:, c0:c0 + cs]
                acc = cur * w[3:4, :] + cb_ref[:, c0:c0 + cs]
                for d in (1, 2, 3):
                    acc = acc + pltpu.roll(xx, d, axis=0)[HALO:, :] * w[3 - d:4 - d, :]
                if act:
                    acc = acc * _sigmoid(acc)
                o_ref[:, c0:c0 + cs] = acc.astype(BF16)

        is_xbc = (j >= j_xbc) & (j < j_xbc + n_xbc)

        @pl.when(j == 0)
        def _():
            conv(xa_ref, 0, False)

        @pl.when(is_xbc)
        def _():
            conv(xbc_ref, j - j_xbc + 1, True)

        @pl.when((j > 0) & jnp.logical_not(is_xbc))
        def _():
            project()

    return _call(
        body, name="in_proj_fwd", grid=(t // tm, PROJ_MAIN // tn), side=side, sem=("arbitrary", "arbitrary"),
        args=(x2, mod8, pre1, w_main, w_dt, cw, cb),
        in_specs=[pl.BlockSpec((tm, D_MODEL), lambda i, j: (i, 0)),
                  pl.BlockSpec((1, 8, D_MODEL), lambda i, j: (i // per_seq, 0, 0)),
                  pl.BlockSpec((1, D_MODEL), lambda i, j: (0, 0)),
                  pl.BlockSpec((D_MODEL, tn), lambda i, j: (0, j)),
                  pl.BlockSpec((D_MODEL, 128), lambda i, j: (0, 0)),
                  pl.BlockSpec((4, tn), lambda i, j: (0, _conv_tile(j, tn))),
                  pl.BlockSpec((1, tn), lambda i, j: (0, _conv_tile(j, tn)))],
        out_specs=[pl.BlockSpec((tm, tn), lambda i, j: (i, j)),
                   pl.BlockSpec((tm // HT_TOK, D_MODEL, HT_TOK), lambda i, j: (i, 0, 0)),
                   pl.BlockSpec((tm, 128), lambda i, j: (i, 0)),
                   pl.BlockSpec((tm, tn), lambda i, j: (i, 0)),
                   pl.BlockSpec((tm, tn), lambda i, j: (i, jnp.clip(j - j_xbc, 0, n_xbc - 1)))],
        out_shape=[jax.ShapeDtypeStruct((t, PROJ_MAIN), BF16), jax.ShapeDtypeStruct((t // HT_TOK, D_MODEL, HT_TOK), BF16),
                   jax.ShapeDtypeStruct((t, 128), F32), jax.ShapeDtypeStruct((t, D_MODEL), BF16),
                   jax.ShapeDtypeStruct((t, 2 * SSD_INNER), BF16)],
        scratch_shapes=[pltpu.VMEM((tm, D_MODEL), BF16), pltpu.VMEM((1 + n_xbc, HALO, tn), F32)])


def _lru_gates(xa, wa_ref, wx_ref, ba, bx, sp):
    nblk = D_MODEL // LRU_BLOCK
    pr = jnp.concatenate([_dot(xa[:, j * LRU_BLOCK:(j + 1) * LRU_BLOCK], wa_ref[j]) for j in range(nblk)], axis=1) + ba
    pi = jnp.concatenate([_dot(xa[:, j * LRU_BLOCK:(j + 1) * LRU_BLOCK], wx_ref[j]) for j in range(nblk)], axis=1) + bx
    r = _sigmoid(pr)
    i = _sigmoid(pi)
    log_a = (-LRU_C * r) * sp
    return r, i, jnp.exp(log_a), _neg_expm1(2.0 * log_a)


def lru_fwd(xa, proj, wa_bd, wx_bd, vec, nb, seq):
    t = xa.shape[0]
    tc = min(512, seq)
    nk = seq // tc
    gb = C_LRU_G // D_MODEL

    def body(xa_ref, g_ref, wa_ref, wx_ref, vec_ref, ya_ref, h_ref, a_scr, u_scr, hc_scr):
        @pl.when(pl.program_id(1) == 0)
        def _():
            hc_scr[...] = jnp.zeros_like(hc_scr)

        xa_v = xa_ref[...]
        v = vec_ref[...]
        r, i, a, e = _lru_gates(xa_v, wa_ref, wx_ref, v[0:1, :], v[1:2, :], v[2:3, :])
        a_scr[...] = a
        u_scr[...] = jnp.sqrt(e) * (i * xa_v.astype(F32))
        row = lax.broadcasted_iota(jnp.int32, (8, 1), 0)

        def tile(j, h):
            r0 = pl.multiple_of(j * 8, 8)
            av, uv = a_scr[pl.ds(r0, 8), :], u_scr[pl.ds(r0, 8), :]
            for d in (1, 2, 4):
                uv = uv + av * jnp.where(row >= d, pltpu.roll(uv, d, axis=0), 0.0)
                av = av * jnp.where(row >= d, pltpu.roll(av, d, axis=0), 1.0)
            hv = uv + av * h
            h_ref[pl.ds(r0, 8), :] = hv
            return hv[7:8, :]

        hc_scr[...] = lax.fori_loop(0, tc // 8, tile, hc_scr[...], unroll=2)
        gel, _ = _gelu_and_grad(g_ref[...].astype(F32))
        ya_ref[...] = (h_ref[...] * gel).astype(BF16)

    return pl.pallas_call(
        body, name="lru_fwd", grid=(nb, nk),
        in_specs=[pl.BlockSpec((tc, D_MODEL), lambda b, k: (b * nk + k, 0)),
                  pl.BlockSpec((tc, D_MODEL), lambda b, k: (b * nk + k, gb)),
                  VMEM_FULL, VMEM_FULL, VMEM_FULL],
        out_specs=[pl.BlockSpec((tc, D_MODEL), lambda b, k: (b * nk + k, 0)),
                   pl.BlockSpec((tc, D_MODEL), lambda b, k: (b * nk + k, 0))],
        out_shape=[jax.ShapeDtypeStruct((t, D_MODEL), BF16), jax.ShapeDtypeStruct((t, D_MODEL), F32)],
        scratch_shapes=[pltpu.VMEM((tc, D_MODEL), F32), pltpu.VMEM((tc, D_MODEL), F32), pltpu.VMEM((1, D_MODEL), F32)],
        compiler_params=_cp("arbitrary", "arbitrary"),
    )(xa, proj, wa_bd, wx_bd, vec)


def lru_bwd(dya, xa, proj, h, wa_bd, wx_bd, vec, nb, seq, side=None):
    t = xa.shape[0]
    tc = min(512, seq)
    nk = seq // tc
    gb = C_LRU_G // D_MODEL
    nblk = D_MODEL // LRU_BLOCK

    def chunk(b, k):
        return b * nk + (nk - 1 - k)

    def body(dya_ref, xa_ref, g_ref, h_ref, hp_ref, wa_ref, wx_ref, vec_ref,
             dxa_ref, dg_ref, dwa_ref, dwx_ref, dvec_ref, a_scr, dh_scr, c_scr):
        b, k = pl.program_id(0), pl.program_id(1)

        @pl.when((b == 0) & (k == 0))
        def _():
            dwa_ref[...] = jnp.zeros_like(dwa_ref)
            dwx_ref[...] = jnp.zeros_like(dwx_ref)
            dvec_ref[...] = jnp.zeros_like(dvec_ref)

        @pl.when(k == 0)
        def _():
            c_scr[...] = jnp.zeros_like(c_scr)

        xa_v = xa_ref[...]
        xaf = xa_v.astype(F32)
        v = vec_ref[...]
        sp = v[2:3, :]
        r, i, a, e = _lru_gates(xa_v, wa_ref, wx_ref, v[0:1, :], v[1:2, :], sp)
        gel, dgel = _gelu_and_grad(g_ref[...].astype(F32))
        hv = h_ref[...]
        dyv = dya_ref[...].astype(F32)
        dg_ref[...] = (dyv * hv * dgel).astype(BF16)
        a_scr[...] = a
        dh_scr[...] = dyv * gel

        row8 = lax.broadcasted_iota(jnp.int32, (8, 1), 0)

        def tile(j, c):
            r0 = pl.multiple_of((tc // 8 - 1 - j) * 8, 8)
            av, dout = a_scr[pl.ds(r0, 8), :], dh_scr[pl.ds(r0, 8), :]
            zv = av * dout
            for d in (1, 2, 4):
                zv = zv + av * jnp.where(row8 < 8 - d, pltpu.roll(zv, 8 - d, axis=0), 0.0)
                av = av * jnp.where(row8 < 8 - d, pltpu.roll(av, 8 - d, axis=0), 1.0)
            zv = zv + av * c
            dh_scr[pl.ds(r0, 8), :] = dout + jnp.where(row8 < 7, pltpu.roll(zv, 7, axis=0), c)
            return zv[0:1, :]

        c_scr[...] = lax.fori_loop(0, tc // 8, tile, c_scr[...], unroll=2)
        dh = dh_scr[...]
        h_last = jnp.where(k == nk - 1, 0.0, hp_ref[HALO // 2 - 1:HALO // 2, :])
        row = lax.broadcasted_iota(jnp.int32, (tc, 1), 0)
        h_prev = jnp.where(row == 0, h_last, pltpu.roll(hv, 1, axis=0))
        s = jnp.sqrt(e)
        da = dh * h_prev
        ix = i * xaf
        dlog_a = da * a - (dh * ix) * (a * a) * lax.rsqrt(jnp.maximum(e, 1e-30))
        di = dh * s * xaf
        dpr = (dlog_a * (-LRU_C * sp)) * (r * (1.0 - r))
        dpi = di * (i * (1.0 - i))
        dprb, dpib = dpr.astype(BF16), dpi.astype(BF16)
        dxa = dh * s * i
        dxa = dxa + jnp.concatenate(
            [_dot_nt(dprb[:, j * LRU_BLOCK:(j + 1) * LRU_BLOCK], wa_ref[j])
             + _dot_nt(dpib[:, j * LRU_BLOCK:(j + 1) * LRU_BLOCK], wx_ref[j]) for j in range(nblk)], axis=1)
        dxa_ref[...] = dxa.astype(BF16)
        for j in range(nblk):
            sl = slice(j * LRU_BLOCK, (j + 1) * LRU_BLOCK)
            dwa_ref[j] += _dot_tn(xa_v[:, sl], dprb[:, sl])
            dwx_ref[j] += _dot_tn(xa_v[:, sl], dpib[:, sl])
        dvec_ref[...] += jnp.concatenate(
            [_colsum(dpr), _colsum(dpi), _colsum(dlog_a * (-LRU_C * r)), jnp.zeros((5, D_MODEL), F32)], axis=0)

    hh = HALO // 2
    return _call(
        body, name="lru_bwd", grid=(nb, nk), side=side, sem=("arbitrary", "arbitrary"),
        args=(dya, xa, proj, h, h, wa_bd, wx_bd, vec),
        in_specs=[pl.BlockSpec((tc, D_MODEL), lambda b, k: (chunk(b, k), 0)),
                  pl.BlockSpec((tc, D_MODEL), lambda b, k: (chunk(b, k), 0)),
                  pl.BlockSpec((tc, D_MODEL), lambda b, k: (chunk(b, k), gb)),
                  pl.BlockSpec((tc, D_MODEL), lambda b, k: (chunk(b, k), 0)),
                  pl.BlockSpec((hh, D_MODEL), lambda b, k: (jnp.maximum(chunk(b, k) * (tc // hh) - 1, 0), 0)),
                  VMEM_FULL, VMEM_FULL, VMEM_FULL],
        out_specs=[pl.BlockSpec((tc, D_MODEL), lambda b, k: (chunk(b, k), 0)),
                   pl.BlockSpec((tc, D_MODEL), lambda b, k: (chunk(b, k), 0)),
                   pl.BlockSpec((nblk, LRU_BLOCK, LRU_BLOCK), lambda b, k: (0, 0, 0)),
                   pl.BlockSpec((nblk, LRU_BLOCK, LRU_BLOCK), lambda b, k: (0, 0, 0)),
                   pl.BlockSpec((8, D_MODEL), lambda b, k: (0, 0))],
        out_shape=[jax.ShapeDtypeStruct((t, D_MODEL), BF16), jax.ShapeDtypeStruct((t, D_MODEL), BF16),
                   jax.ShapeDtypeStruct((nblk, LRU_BLOCK, LRU_BLOCK), F32),
                   jax.ShapeDtypeStruct((nblk, LRU_BLOCK, LRU_BLOCK), F32),
                   jax.ShapeDtypeStruct((8, D_MODEL), F32)],
        scratch_shapes=[pltpu.VMEM((tc, D_MODEL), F32), pltpu.VMEM((tc, D_MODEL), F32), pltpu.VMEM((1, D_MODEL), F32)])


def merge_fwd(ya_in, yb_in, proj, x2, mod8, bgate, post1, w_pa, w_pb, w_out, seq):
    t = x2.shape[0]
    tm = min(512, seq)
    per_seq = seq // tm
    gcb = C_GATES // SSD_INNER

    def body(ya_ref, yb_ref, gt_ref, x_ref, mod_ref, bg_ref, post_ref, wpa_ref, wpb_ref, wo_ref,
             yab_ref, out1_ref, x1_ref):
        y_a = _dot(ya_ref[...], wpa_ref[...])
        y_b = _dot(yb_ref[...], wpb_ref[...])
        g = _sigmoid(gt_ref[...].astype(F32) + bg_ref[...])
        merged = g[:, :D_MODEL] * y_a + g[:, D_MODEL:] * y_b
        out1 = _dot(merged.astype(BF16), wo_ref[...])
        n = out1 * lax.rsqrt(_rowmean(out1 * out1) + EPS)
        yab_ref[...] = jnp.concatenate([y_a, y_b], axis=1).astype(BF16)
        out1_ref[...] = out1
        x1_ref[...] = x_ref[...] + mod_ref[0][2:3, :] * (n * post_ref[...])

    row = lambda w: pl.BlockSpec((tm, w), lambda i: (i, 0))
    return pl.pallas_call(
        body, name="merge_fwd", grid=(t // tm,),
        in_specs=[row(D_MODEL), row(SSD_INNER), pl.BlockSpec((tm, SSD_INNER), lambda i: (i, gcb)), row(D_MODEL),
                  pl.BlockSpec((1, 8, D_MODEL), lambda i: (i // per_seq, 0, 0)),
                  VMEM_FULL, VMEM_FULL, VMEM_FULL, VMEM_FULL, VMEM_FULL],
        out_specs=[row(SSD_INNER), row(D_MODEL), row(D_MODEL)],
        out_shape=[jax.ShapeDtypeStruct((t, SSD_INNER), BF16), jax.ShapeDtypeStruct((t, D_MODEL), F32),
                   jax.ShapeDtypeStruct((t, D_MODEL), F32)],
        compiler_params=_cp("parallel"),
    )(ya_in, yb_in, proj, x2, mod8, bgate, post1, w_pa, w_pb, w_out)


def merge_bwd(dx1, out1, yab, proj, mod8, bgate, post1, w_pa, w_pb, w_out, nb, seq):
    t = dx1.shape[0]
    tm = min(512, seq)
    per_seq = seq // tm
    gcb = C_GATES // SSD_INNER

    def body(dx1_ref, out1_ref, yab_ref, gt_ref, mod_ref, bg_ref, post_ref, wpa_ref, wpb_ref, wo_ref,
             dya_ref, dyb_ref, dgt_ref, dyab_ref, dout1_ref, mg_ref, vacc_ref, dmod_ref):
        b, s = pl.program_id(0), pl.program_id(1)

        @pl.when((b == 0) & (s == 0))
        def _():
            vacc_ref[...] = jnp.zeros_like(vacc_ref)

        @pl.when(s == 0)
        def _():
            dmod_ref[...] = jnp.zeros_like(dmod_ref)

        dx1v = dx1_ref[...]
        out1 = out1_ref[...]
        post = post_ref[...]
        rs = lax.rsqrt(_rowmean(out1 * out1) + EPS)
        n = out1 * rs
        do = dx1v * mod_ref[0][2:3, :]
        dn = do * post
        dout1 = rs * (dn - n * _rowmean(dn * n))
        dout1b = dout1.astype(BF16)
        dout1_ref[...] = dout1b
        dmerged = _dot_nt(dout1b, wo_ref[...])
        g = _sigmoid(gt_ref[...].astype(F32) + bg_ref[...])
        yab_v = yab_ref[...].astype(F32)
        gy = g * yab_v
        mg_ref[...] = (gy[:, :D_MODEL] + gy[:, D_MODEL:]).astype(BF16)
        dm2 = jnp.concatenate([dmerged, dmerged], axis=1)
        dyab = (dm2 * g).astype(BF16)
        dyab_ref[...] = dyab
        dgt = dm2 * gy * (1.0 - g)
        dgt_ref[...] = dgt.astype(BF16)
        dya_ref[...] = _dot_nt(dyab[:, :D_MODEL], wpa_ref[...]).astype(BF16)
        dyb_ref[...] = _dot_nt(dyab[:, D_MODEL:], wpb_ref[...]).astype(BF16)
        vacc_ref[...] += jnp.concatenate(
            [_colsum(dgt), jnp.concatenate([_colsum(do * n), jnp.zeros((1, D_MODEL), F32)], axis=1),
             jnp.zeros((6, SSD_INNER), F32)], axis=0)
        dmod_ref[0] += jnp.concatenate(
            [jnp.zeros((2, D_MODEL), F32), _colsum(dx1v * (n * post)), jnp.zeros((5, D_MODEL), F32)], axis=0)

    row = lambda w: pl.BlockSpec((tm, w), lambda b, s: (b * per_seq + s, 0))
    return pl.pallas_call(
        body, name="merge_bwd", grid=(nb, per_seq),
        in_specs=[row(D_MODEL), row(D_MODEL), row(SSD_INNER),
                  pl.BlockSpec((tm, SSD_INNER), lambda b, s: (b * per_seq + s, gcb)),
                  pl.BlockSpec((1, 8, D_MODEL), lambda b, s: (b, 0, 0)),
                  VMEM_FULL, VMEM_FULL, VMEM_FULL, VMEM_FULL, VMEM_FULL],
        out_specs=[row(D_MODEL), row(SSD_INNER), row(SSD_INNER), row(SSD_INNER), row(D_MODEL), row(D_MODEL),
                   pl.BlockSpec((8, SSD_INNER), lambda b, s: (0, 0)),
                   pl.BlockSpec((1, 8, D_MODEL), lambda b, s: (b, 0, 0))],
        out_shape=[jax.ShapeDtypeStruct((t, D_MODEL), BF16), jax.ShapeDtypeStruct((t, SSD_INNER), BF16),
                   jax.ShapeDtypeStruct((t, SSD_INNER), BF16), jax.ShapeDtypeStruct((t, SSD_INNER), BF16),
                   jax.ShapeDtypeStruct((t, D_MODEL), BF16), jax.ShapeDtypeStruct((t, D_MODEL), BF16),
                   jax.ShapeDtypeStruct((8, SSD_INNER), F32), jax.ShapeDtypeStruct((nb, 8, D_MODEL), F32)],
        compiler_params=_cp("arbitrary", "arbitrary"),
    )(dx1, out1, yab, proj, mod8, bgate, post1, w_pa, w_pb, w_out)


def mlp_fwd_bwd(x1, tgt, mod8, pre2, post2, w_ff1, w_ff2, nb, seq):
    t = x1.shape[0]
    tm = min(256, seq)
    per_seq = seq // tm
    fc = 1024
    nfc = D_FF // fc

    def body(x1_ref, tgt_ref, mod_ref, pre_ref, post_ref, w1_ref, w2_ref,
             dx1_ref, h2_ref, da1_ref, act_ref, dy2_ref, loss_ref, vacc_ref, dmod_ref, r_scr):
        b, s = pl.program_id(0), pl.program_id(1)
        per = fc // w1_ref.shape[2]

        def w1_cols(c):
            return jnp.concatenate([w1_ref[per * c + q] for q in range(per)], axis=1)

        @pl.when((b == 0) & (s == 0))
        def _():
            vacc_ref[...] = jnp.zeros_like(vacc_ref)
            loss_ref[...] = jnp.zeros_like(loss_ref)

        @pl.when(s == 0)
        def _():
            dmod_ref[...] = jnp.zeros_like(dmod_ref)

        m = mod_ref[0]
        sh2, sc2, g2 = m[3:4, :], m[4:5, :], m[5:6, :]
        pre, post = pre_ref[...], post_ref[...]
        x1v = x1_ref[...]
        rs1 = lax.rsqrt(_rowmean(x1v * x1v) + EPS)
        n1 = x1v * rs1
        y1 = n1 * pre
        h2b = (y1 * (1.0 + sc2) + sh2).astype(BF16)
        h2_ref[...] = h2b
        y2 = jnp.zeros((tm, D_MODEL), F32)
        for c in range(nfc):
            r = jnp.maximum(_dot(h2b, w1_cols(c)), 0.0)
            r_scr[:, c * fc:(c + 1) * fc] = r
            a = (r * r).astype(BF16)
            act_ref[:, c * fc:(c + 1) * fc] = a
            y2 = y2 + _dot(a, w2_ref[c * fc:(c + 1) * fc, :])
        rs2 = lax.rsqrt(_rowmean(y2 * y2) + EPS)
        n2 = y2 * rs2
        o2 = n2 * post
        diff = x1v + g2 * o2 - tgt_ref[...]
        loss_ref[...] += 0.5 * jnp.sum(_rowmean(diff * diff))
        dx2 = diff * (1.0 / D_MODEL)
        do2 = dx2 * g2
        dn2 = do2 * post
        dy2b = (rs2 * (dn2 - n2 * _rowmean(dn2 * n2))).astype(BF16)
        dy2_ref[...] = dy2b
        dh2 = jnp.zeros((tm, D_MODEL), F32)
        for c in range(nfc):
            dact = _dot_nt(dy2b, w2_ref[c * fc:(c + 1) * fc, :])
            da = (dact * (2.0 * r_scr[:, c * fc:(c + 1) * fc])).astype(BF16)
            da1_ref[:, c * fc:(c + 1) * fc] = da
            dh2 = dh2 + _dot_nt(da, w1_cols(c))
        dy1 = dh2 * (1.0 + sc2)
        dn1 = dy1 * pre
        dx1_ref[...] = dx2 + rs1 * (dn1 - n1 * _rowmean(dn1 * n1))
        vacc_ref[...] += jnp.concatenate([_colsum(dy1 * n1), _colsum(do2 * n2), jnp.zeros((6, D_MODEL), F32)], axis=0)
        dmod_ref[0] += jnp.concatenate(
            [jnp.zeros((3, D_MODEL), F32), _colsum(dh2), _colsum(dh2 * y1), _colsum(dx2 * o2),
             jnp.zeros((2, D_MODEL), F32)], axis=0)

    row = lambda w: pl.BlockSpec((tm, w), lambda b, s: (b * per_seq + s, 0))
    return pl.pallas_call(
        body, name="mlp_fwd_bwd", grid=(nb, per_seq),
        in_specs=[row(D_MODEL), row(D_MODEL), pl.BlockSpec((1, 8, D_MODEL), lambda b, s: (b, 0, 0)),
                  VMEM_FULL, VMEM_FULL, VMEM_FULL, VMEM_FULL],
        out_specs=[row(D_MODEL), row(D_MODEL), row(D_FF), row(D_FF), row(D_MODEL),
                   pl.BlockSpec((8, 128), lambda b, s: (0, 0)),
                   pl.BlockSpec((8, D_MODEL), lambda b, s: (0, 0)),
                   pl.BlockSpec((1, 8, D_MODEL), lambda b, s: (b, 0, 0))],
        out_shape=[jax.ShapeDtypeStruct((t, D_MODEL), F32), jax.ShapeDtypeStruct((t, D_MODEL), BF16),
                   jax.ShapeDtypeStruct((t, D_FF), BF16), jax.ShapeDtypeStruct((t, D_FF), BF16),
                   jax.ShapeDtypeStruct((t, D_MODEL), BF16), jax.ShapeDtypeStruct((8, 128), F32),
                   jax.ShapeDtypeStruct((8, D_MODEL), F32), jax.ShapeDtypeStruct((nb, 8, D_MODEL), F32)],
        scratch_shapes=[pltpu.VMEM((tm, D_FF), F32)],
        compiler_params=_cp("arbitrary", "arbitrary"),
    )(x1, tgt, mod8, pre2, post2, w_ff1, w_ff2)


_PIECES = ((C_LRU_X, 1024), (C_LRU_G, 1024), (C_Z, 2048), (C_XBC, 4096), (C_GATES, 2048))
_NP = len(_PIECES)


def in_proj_bwd(pieces, ddt, dx1, x2, mod8, pre1, w_main, w_dt, nb, seq, side=None):
    t = x2.shape[0]
    tm = min(512, seq)
    per_seq = seq // tm
    widths = [min(w, 2048) for _, w in _PIECES]
    steps = [(p, q) for p, (_, w) in enumerate(_PIECES) for q in range(w // widths[p])]
    nk = len(steps)

    def piece_spec(p):
        first = min(k for k in range(nk) if steps[k][0] == p)
        nblk = _PIECES[p][1] // widths[p]
        return pl.BlockSpec((tm, widths[p]), lambda b, s, k: (b * per_seq + s, jnp.clip(k - first, 0, nblk - 1)))

    def body(*refs):
        prefs = refs[:_NP]
        ddt_ref, dx1_ref, x_ref, mod_ref, pre_ref, w_ref, wdt_ref, gx_ref, vacc_ref, dmod_ref, acc_ref = refs[_NP:]
        b, s, k = pl.program_id(0), pl.program_id(1), pl.program_id(2)

        @pl.when((b == 0) & (s == 0) & (k == 0))
        def _():
            vacc_ref[...] = jnp.zeros_like(vacc_ref)

        @pl.when((s == 0) & (k == 0))
        def _():
            dmod_ref[...] = jnp.zeros_like(dmod_ref)

        @pl.when(k == 0)
        def _():
            acc_ref[...] = _dot_nt(ddt_ref[...], wdt_ref[...])

        for kk, (p, q) in enumerate(steps):
            @pl.when(k == kk)
            def _(p=p, q=q):
                c0 = _PIECES[p][0] + q * widths[p]
                acc_ref[...] += _dot_nt(prefs[p][...], w_ref[:, c0:c0 + widths[p]])

        @pl.when(k == nk - 1)
        def _():
            dh = acc_ref[...]
            m = mod_ref[0]
            pre = pre_ref[...]
            xv = x_ref[...]
            rs = lax.rsqrt(_rowmean(xv * xv) + EPS)
            n = xv * rs
            dy = dh * (1.0 + m[1:2, :])
            dn = dy * pre
            gx_ref[...] = dx1_ref[...] + rs * (dn - n * _rowmean(dn * n))
            vacc_ref[...] += jnp.concatenate([_colsum(dy * n), jnp.zeros((7, D_MODEL), F32)], axis=0)
            dmod_ref[0] += jnp.concatenate([_colsum(dh), _colsum(dh * (n * pre)), jnp.zeros((6, D_MODEL), F32)], axis=0)

    row = lambda w: pl.BlockSpec((tm, w), lambda b, s, k: (b * per_seq + s, 0))
    return _call(
        body, name="in_proj_bwd", grid=(nb, per_seq, nk), side=side, sem=("arbitrary", "arbitrary", "arbitrary"),
        args=(*pieces, ddt, dx1, x2, mod8, pre1, w_main, w_dt),
        in_specs=[piece_spec(p) for p in range(_NP)] + [
            row(128), row(D_MODEL), row(D_MODEL), pl.BlockSpec((1, 8, D_MODEL), lambda b, s, k: (b, 0, 0)),
            pl.BlockSpec((1, D_MODEL), lambda b, s, k: (0, 0)),
            VMEM_FULL,
            pl.BlockSpec((D_MODEL, 128), lambda b, s, k: (0, 0))],
        out_specs=[row(D_MODEL), pl.BlockSpec((8, D_MODEL), lambda b, s, k: (0, 0)),
                   pl.BlockSpec((1, 8, D_MODEL), lambda b, s, k: (b, 0, 0))],
        out_shape=[jax.ShapeDtypeStruct((t, D_MODEL), F32), jax.ShapeDtypeStruct((8, D_MODEL), F32),
                   jax.ShapeDtypeStruct((nb, 8, D_MODEL), F32)],
        scratch_shapes=[pltpu.VMEM((tm, D_MODEL), F32)])


def in_proj_wgrad(h1t, proj, dxa, dxbc, dlg, dz, dgates, ddt, cw, cb, seq, side=None):
    nt, _, tt = h1t.shape
    t = nt * tt
    tn = 1024
    nn = PROJ_MAIN // tn
    ns = seq // tt
    nh = t // HALO
    j_g, j_z, j_x, j_gt = C_LRU_G // tn, C_Z // tn, C_XBC // tn, C_GATES // tn
    n_x = 2 * SSD_INNER // tn
    strip = 256
    ne = tt + HALO

    def body(h_ref, cur_ref, prev_ref, next_ref, dxa_ref, dxan_ref, dxb_ref, dxbn_ref, dlg_ref, dz_ref, dgt_ref, ddt_ref,
             cw_ref, cb_ref, dw_ref, dwdt_ref, dlx_ref, dxr_ref, accl_ref, accs_ref, acc_ref, accdt_ref):
        n, k = pl.program_id(0), pl.program_id(1)
        hv = h_ref[k]
        is_x = (n >= j_x) & (n < j_x + n_x)

        @pl.when(k == 0)
        def _():
            acc_ref[...] = jnp.zeros_like(acc_ref)

        @pl.when((n == 0) & (k == 0))
        def _():
            accdt_ref[...] = jnp.zeros_like(accdt_ref)
            accl_ref[...] = jnp.zeros_like(accl_ref)

        @pl.when(is_x & (k == 0))
        def _():
            accs_ref[...] = jnp.zeros_like(accs_ref)

        def conv_tile(do_ref, don_ref, out_ref, cacc_ref, act):
            first = lax.rem(k, ns) == 0
            last = lax.rem(k, ns) == ns - 1
            for c0 in range(0, tn, strip):
                cs = slice(c0, c0 + strip)
                xx = jnp.concatenate([jnp.where(first, 0.0, prev_ref[:, cs].astype(F32)), cur_ref[:, cs].astype(F32),
                                      next_ref[:, cs].astype(F32)], axis=0)
                do_ext = jnp.concatenate([do_ref[:, cs].astype(F32),
                                          jnp.where(last, 0.0, don_ref[:, cs].astype(F32))], axis=0)
                w = cw_ref[:, cs]
                xs = [xx[HALO:HALO + ne, :]] + [pltpu.roll(xx, d, axis=0)[HALO:HALO + ne, :] for d in (1, 2, 3)]
                if act:
                    c = cb_ref[:, cs] + xs[0] * w[3:4, :] + xs[1] * w[2:3, :] + xs[2] * w[1:2, :] + xs[3] * w[0:1, :]
                    sg = _sigmoid(c)
                    dc = do_ext * (sg * (1.0 + c * (1.0 - sg)))
                else:
                    dc = do_ext
                dx = dc[:tt, :] * w[3:4, :]
                for d in (1, 2, 3):
                    dx = dx + pltpu.roll(dc, ne - d, axis=0)[:tt, :] * w[3 - d:4 - d, :]
                dxb = dx.astype(BF16)
                out_ref[:, cs] = dxb
                acc_ref[:, cs] += _dot(hv, dxb)
                dcc = dc[:tt, :]
                rows = [_colsum(dcc * xs[3 - r][:tt, :]) for r in range(4)] + [_colsum(dcc)]
                cacc_ref[:, cs] += jnp.concatenate(rows + [jnp.zeros((3, strip), F32)], axis=0)

        @pl.when(n == 0)
        def _():
            conv_tile(dxa_ref, dxan_ref, dlx_ref, accl_ref, False)
            accdt_ref[...] += _dot(hv, ddt_ref[...])

        @pl.when(is_x)
        def _():
            conv_tile(dxb_ref, dxbn_ref, dxr_ref, accs_ref, True)

        @pl.when(n == j_g)
        def _():
            acc_ref[...] += _dot(hv, dlg_ref[...])

        @pl.when((n >= j_z) & (n < j_x))
        def _():
            acc_ref[...] += _dot(hv, dz_ref[...])

        @pl.when(n >= j_gt)
        def _():
            acc_ref[...] += _dot(hv, dgt_ref[...])

        @pl.when(k == nt - 1)
        def _():
            dw_ref[...] = acc_ref[...].astype(BF16)

        @pl.when((n == 0) & (k == nt - 1))
        def _():
            dwdt_ref[...] = accdt_ref[...].astype(BF16)

    conv_n = lambda n: (n == 0) | ((n >= j_x) & (n < j_x + n_x))
    src_col = lambda n: jnp.where(n == 0, 0, jnp.clip(n, j_x, j_x + n_x - 1))
    ctile = lambda n: jnp.where(n == 0, 0, jnp.clip(n - j_x + 1, 1, n_x))
    xcol = lambda n: jnp.clip(n - j_x, 0, n_x - 1)
    on = lambda cond, k: jnp.where(cond, k, 0)
    nxt = lambda k: jnp.minimum(((k + 1) * tt) // HALO, nh - 1)
    after = lambda cond_during, cond_after, k: jnp.where(cond_during, k, jnp.where(cond_after, nt - 1, 0))
    in_specs = [
        VMEM_FULL,
        pl.BlockSpec((tt, tn), lambda n, k: (on(conv_n(n), k), src_col(n))),
        pl.BlockSpec((HALO, tn), lambda n, k: (on(conv_n(n), jnp.maximum((k * tt) // HALO - 1, 0)), src_col(n))),
        pl.BlockSpec((HALO, tn), lambda n, k: (on(conv_n(n), nxt(k)), src_col(n))),
        pl.BlockSpec((tt, tn), lambda n, k: (on(n == 0, k), 0)),
        pl.BlockSpec((HALO, tn), lambda n, k: (on(n == 0, nxt(k)), 0)),
        pl.BlockSpec((tt, tn), lambda n, k: (on((n >= j_x) & (n < j_x + n_x), k), xcol(n))),
        pl.BlockSpec((HALO, tn), lambda n, k: (on((n >= j_x) & (n < j_x + n_x), nxt(k)), xcol(n))),
        pl.BlockSpec((tt, tn), lambda n, k: (on(n == j_g, k), 0)),
        pl.BlockSpec((tt, tn), lambda n, k: (on((n >= j_z) & (n < j_x), k), jnp.clip(n - j_z, 0, j_x - j_z - 1))),
        pl.BlockSpec((tt, tn), lambda n, k: (on(n >= j_gt, k), jnp.clip(n - j_gt, 0, nn - j_gt - 1))),
        pl.BlockSpec((tt, 128), lambda n, k: (on(n == 0, k), 0)),
        pl.BlockSpec((4, tn), lambda n, k: (0, ctile(n))),
        pl.BlockSpec((1, tn), lambda n, k: (0, ctile(n)))]
    out_specs = [
        pl.BlockSpec((D_MODEL, tn), lambda n, k: (0, n)),
        pl.BlockSpec((D_MODEL, 128), lambda n, k: (0, 0)),
        pl.BlockSpec((tt, tn), lambda n, k: (after(n == 0, n > 0, k), 0)),
        pl.BlockSpec((tt, tn), lambda n, k: (after((n >= j_x) & (n < j_x + n_x), n >= j_x + n_x, k), xcol(n))),
        pl.BlockSpec((8, tn), lambda n, k: (0, 0)),
        pl.BlockSpec((8, tn), lambda n, k: (0, xcol(n)))]
    return _call(
        body, name="in_proj_wgrad", grid=(nn, nt), side=side, sem=("arbitrary", "arbitrary"),
        args=(h1t, proj, proj, proj, dxa, dxa, dxbc, dxbc, dlg, dz, dgates, ddt, cw, cb),
        in_specs=in_specs, out_specs=out_specs,
        out_shape=[jax.ShapeDtypeStruct((D_MODEL, PROJ_MAIN), BF16), jax.ShapeDtypeStruct((D_MODEL, 128), BF16),
                   jax.ShapeDtypeStruct((t, D_MODEL), BF16), jax.ShapeDtypeStruct((t, 2 * SSD_INNER), BF16),
                   jax.ShapeDtypeStruct((8, D_MODEL), F32), jax.ShapeDtypeStruct((8, 2 * SSD_INNER), F32)],
        scratch_shapes=[pltpu.VMEM((D_MODEL, tn), F32), pltpu.VMEM((D_MODEL, 128), F32)])


def _log1p(u):
    w = 1.0 + u
    return jnp.log(w) - ((w - 1.0) - u) / w


def _softplus(x):
    return jnp.maximum(x, 0.0) + _log1p(jnp.exp(-jnp.abs(x)))


def _head_mask(h):
    lane = lax.broadcasted_iota(jnp.int32, (1, SSD_GW), 1)
    return (lane >= SSD_P * h) & (lane < SSD_P * (h + 1))


def _pair(p):
    return slice(2 * SSD_P * p, 2 * SSD_P * (p + 1))


def _expand4(m, g):
    lane = lax.broadcasted_iota(jnp.int32, (1, SSD_GW), 1)
    col = lambda h: m[:, 4 * g + h:4 * g + h + 1]
    return jnp.where(lane < SSD_P, col(0), jnp.where(lane < 2 * SSD_P, col(1), jnp.where(lane < 3 * SSD_P, col(2), col(3))))


def _reduce4(v, g):
    lane = lax.broadcasted_iota(jnp.int32, (1, SSD_N), 1)
    out = jnp.zeros((v.shape[0], SSD_N), F32)
    for h in range(4):
        s = jnp.sum(jnp.where(_head_mask(h), v, 0.0), axis=1, keepdims=True)
        out = out + jnp.where(lane == 4 * g + h, s, 0.0)
    return out


def _ssd_heads(dtraw, hp, tri):
    xdt = dtraw + hp[0:1, :]
    dt = _softplus(xdt)
    cs = _dot_hi(tri, dt * hp[1:2, :])
    cs_last = cs[SSD_L - 1:SSD_L, :]
    return dict(xdt=xdt, dt=dt, cs=cs, cs_t=cs.T, e=jnp.exp(cs), w=jnp.exp(cs_last - cs), el=jnp.exp(cs_last))


def _ssd_group(g, hd, xs_b, bm_b, cm_b, d_x, st, paired=False):
    ll = SSD_L
    xs = xs_b.astype(F32)
    cs, cs_t = hd["cs"], hd["cs_t"]
    e_x, w_x, el_x, dt_x = _expand4(hd["e"], g), _expand4(hd["w"], g), _expand4(hd["el"], g), _expand4(hd["dt"], g)
    xd = xs * dt_x
    gcb = _dot_nt(cm_b, bm_b)
    ri = lax.broadcasted_iota(jnp.int32, (ll, ll), 0)
    ci = lax.broadcasted_iota(jnp.int32, (ll, ll), 1)
    dks, ms = [], []
    for h in range(4):
        k = 4 * g + h
        dk = jnp.exp(jnp.where(ri >= ci, cs[:, k:k + 1] - cs_t[k:k + 1, :], -1e30))
        dks.append(dk)
        ms.append((gcb * dk).astype(BF16))
    xdb = xd.astype(BF16)
    if paired:
        first = lax.broadcasted_iota(jnp.int32, (1, 2 * SSD_P), 1) < SSD_P
        ydiag = jnp.concatenate(
            [jnp.where(first, _dot(ms[2 * p], xdb[:, _pair(p)]), _dot(ms[2 * p + 1], xdb[:, _pair(p)]))
             for p in range(2)], axis=1)
    else:
        ydiag = jnp.zeros((ll, SSD_GW), F32)
        for h in range(4):
            ydiag = ydiag + _dot(ms[h], jnp.where(_head_mask(h), xd, 0.0).astype(BF16))
    yoff = _dot(cm_b, st.astype(BF16)) * e_x
    y = ydiag + yoff + d_x * xs
    st_new = st * el_x + _dot(bm_b.astype(F32).T.astype(BF16), (xd * w_x).astype(BF16))
    return dict(xs=xs, e_x=e_x, w_x=w_x, el_x=el_x, dt_x=dt_x, xd=xd, xdb=xdb, gcb=gcb, dks=dks, ms=ms, yoff=yoff, y=y,
                st_new=st_new)


def ssd_consts():
    hh = np.arange(SSD_N)
    tri = (hh[:, None] >= hh[None, :]).astype(np.float32)
    return jnp.asarray(tri), jnp.asarray(tri.T)


def ssd_params(dt_bias, a_log, d_skip, norm_w):
    padh = lambda v: jnp.pad(v.reshape(1, SSD_HEADS), ((0, 0), (0, SSD_N - SSD_HEADS)))
    hp = jnp.concatenate([padh(dt_bias), padh(-jnp.exp(a_log)), jnp.zeros((6, SSD_N), F32)], axis=0)
    lp = jnp.concatenate([norm_w.reshape(1, SSD_INNER), jnp.repeat(d_skip, SSD_P).reshape(1, SSD_INNER),
                          jnp.zeros((6, SSD_INNER), F32)], axis=0)
    return hp, lp


def _b_cols(g):
    return slice(SSD_INNER + g * SSD_N, SSD_INNER + (g + 1) * SSD_N)


def _c_cols(g):
    return slice(SSD_INNER + (SSD_G + g) * SSD_N, SSD_INNER + (SSD_G + g + 1) * SSD_N)


def _ssd_specs(nc, rc, cps=1):
    rows = cps * SSD_L
    return [pl.BlockSpec((rows, 2 * SSD_INNER), lambda b, c: (b * nc + rc(c), 0)),
            pl.BlockSpec((rows, SSD_INNER), lambda b, c: (b * nc + rc(c), C_Z // SSD_INNER)),
            pl.BlockSpec((rows, SSD_N), lambda b, c: (b * nc + rc(c), 0))]


def ssd_fwd(xbc, proj, dtraw, hp, lp, tri, nb, seq):
    t = xbc.shape[0]
    cps = SSD_FWD_CPS
    nc = seq // (cps * SSD_L)

    def body(xbc_ref, z_ref, dt_ref, hp_ref, lp_ref, tri_ref, y_ref, sts_ref, st_scr):
        @pl.when(pl.program_id(1) == 0)
        def _():
            st_scr[...] = jnp.zeros_like(st_scr)

        for cc in range(cps):
            rs = slice(cc * SSD_L, (cc + 1) * SSD_L)
            hd = _ssd_heads(dt_ref[rs, :], hp_ref[...], tri_ref[...])
            for g in range(SSD_G):
                gs = slice(g * SSD_GW, (g + 1) * SSD_GW)
                st = st_scr[g]
                sts_ref[cc, g] = st
                f = _ssd_group(g, hd, xbc_ref[rs, gs], xbc_ref[rs, _b_cols(g)], xbc_ref[rs, _c_cols(g)],
                               lp_ref[1:2, gs], st, paired=True)
                st_scr[g] = f["st_new"]
                zf = z_ref[rs, gs].astype(F32)
                yg = f["y"] * (zf * _sigmoid(zf))
                y_ref[rs, gs] = (yg * lax.rsqrt(_rowmean(yg * yg) + EPS) * lp_ref[0:1, gs]).astype(BF16)

    return pl.pallas_call(
        body, name="ssd_fwd", grid=(nb, nc),
        in_specs=_ssd_specs(nc, lambda c: c, cps) + [VMEM_FULL, VMEM_FULL, VMEM_FULL],
        out_specs=[pl.BlockSpec((cps * SSD_L, SSD_INNER), lambda b, c: (b * nc + c, 0)),
                   pl.BlockSpec((cps, SSD_G, SSD_N, SSD_GW), lambda b, c: (b * nc + c, 0, 0, 0))],
        out_shape=[jax.ShapeDtypeStruct((t, SSD_INNER), BF16),
                   jax.ShapeDtypeStruct((nb * nc * cps, SSD_G, SSD_N, SSD_GW), F32)],
        scratch_shapes=[pltpu.VMEM((SSD_G, SSD_N, SSD_GW), F32)],
        compiler_params=_cp("arbitrary", "arbitrary"),
    )(xbc, proj, dtraw, hp, lp, tri)


def ssd_bwd(xbc, proj, dtraw, hp, lp, tri, triu, states, dyn, nb, seq, side=None):
    t = xbc.shape[0]
    nc = seq // SSD_L
    ll = SSD_L

    def body(xbc_ref, z_ref, dt_ref, sts_ref, dy_ref, hp_ref, lp_ref, tri_ref, triu_ref,
             dxbc_ref, dz_ref, ddt_ref, hpg_ref, lpg_ref, dst_scr):
        b, c_i = pl.program_id(0), pl.program_id(1)

        @pl.when((b == 0) & (c_i == 0))
        def _():
            hpg_ref[...] = jnp.zeros_like(hpg_ref)
            lpg_ref[...] = jnp.zeros_like(lpg_ref)

        @pl.when(c_i == 0)
        def _():
            dst_scr[...] = jnp.zeros_like(dst_scr)

        hp = hp_ref[...]
        hd = _ssd_heads(dt_ref[...], hp, tri_ref[...])
        lane = lax.broadcasted_iota(jnp.int32, (1, SSD_N), 1)
        subl = lax.broadcasted_iota(jnp.int32, (SSD_N, 1), 0)
        dcs = jnp.zeros((ll, SSD_N), F32)
        dcs_t = jnp.zeros((SSD_N, ll), F32)
        last = jnp.zeros((1, SSD_N), F32)
        dxx = jnp.zeros((ll, SSD_N), F32)
        for g in range(SSD_G):
            gs = slice(g * SSD_GW, (g + 1) * SSD_GW)
            st = sts_ref[0, g]
            dst = dst_scr[g]
            bm_b, cm_b = xbc_ref[:, _b_cols(g)], xbc_ref[:, _c_cols(g)]
            d_x = lp_ref[1:2, gs]
            f = _ssd_group(g, hd, xbc_ref[:, gs], bm_b, cm_b, d_x, st)
            xs, xd, gcb = f["xs"], f["xd"], f["gcb"]
            e_x, w_x, el_x, dt_x = f["e_x"], f["w_x"], f["el_x"], f["dt_x"]
            stb, dstb = st.astype(BF16), dst.astype(BF16)
            zf = z_ref[:, gs].astype(F32)
            sg = _sigmoid(zf)
            sz = zf * sg
            yv = f["y"]
            yg = yv * sz
            rstd = lax.rsqrt(_rowmean(yg * yg) + EPS)
            n = yg * rstd
            dyn_v = dy_ref[:, gs].astype(F32)
            dn = dyn_v * lp_ref[0:1, gs]
            dyg = rstd * (dn - n * _rowmean(dn * n))
            dy = dyg * sz
            dz_ref[:, gs] = (dyg * yv * (sg * (1.0 + zf * (1.0 - sg)))).astype(BF16)
            dyb = dy.astype(BF16)
            r_ = _dot(bm_b, dstb)
            dxd = w_x * r_
            dqb = (dy * e_x).astype(BF16)
            dcm = _dot_nt(dqb, stb)
            dst_scr[g] = dst * el_x + _dot_tn(cm_b, dqb)
            dbm = _dot_nt((xd * w_x).astype(BF16), dstb)
            xdb = f["xdb"]
            dgm = jnp.zeros((ll, ll), F32)
            for h in range(4):
                k = 4 * g + h
                hm = _head_mask(h)
                dxd = dxd + jnp.where(hm, _dot_tn(f["ms"][h], dyb), 0.0)
                dm = _dot_nt(jnp.where(hm, dy, 0.0).astype(BF16), xdb) * f["dks"][h]
                dgm = dgm + dm
                dseg = dm * gcb
                dcs = dcs + jnp.where(lane == k, jnp.sum(dseg, axis=1, keepdims=True), 0.0)
                dcs_t = dcs_t + jnp.where(subl == k, jnp.sum(dseg, axis=0, keepdims=True), 0.0)
            dgmb = dgm.astype(BF16)
            dxbc_ref[:, _c_cols(g)] = (dcm + _dot(dgmb, bm_b)).astype(BF16)
            dxbc_ref[:, _b_cols(g)] = (dbm + _dot_tn(dgmb, cm_b)).astype(BF16)
            v = _reduce4(r_ * xd * w_x, g)
            dcs = dcs + _reduce4(dy * f["yoff"], g) - v
            last = last + _colsum(v) + _reduce4(_colsum(dst * st) * el_x, g)
            dxx = dxx + _reduce4(dxd * xs, g)
            dxbc_ref[:, gs] = (d_x * dy + dxd * dt_x).astype(BF16)
            lpg_ref[0:1, gs] += _colsum(dyn_v * n)
            lpg_ref[1:2, gs] += _colsum(dy * xs)
        rowi = lax.broadcasted_iota(jnp.int32, (ll, 1), 0)
        da = _dot_hi(triu_ref[...], dcs - dcs_t.T + jnp.where(rowi == ll - 1, last, 0.0))
        ddt = (dxx + da * hp[1:2, :]) * _sigmoid(hd["xdt"])
        ddt_ref[...] = ddt
        hpg_ref[...] += jnp.concatenate([_colsum(ddt), _colsum(da * hd["dt"]), jnp.zeros((6, SSD_N), F32)], axis=0)

    rc = lambda c: nc - 1 - c
    return _call(
        body, name="ssd_bwd", grid=(nb, nc), side=side, sem=("arbitrary", "arbitrary"),
        args=(xbc, proj, dtraw, states, dyn, hp, lp, tri, triu),
        in_specs=_ssd_specs(nc, rc) + [
            pl.BlockSpec((1, SSD_G, SSD_N, SSD_GW), lambda b, c: (b * nc + rc(c), 0, 0, 0)),
            pl.BlockSpec((SSD_L, SSD_INNER), lambda b, c: (b * nc + rc(c), 0)),
            VMEM_FULL, VMEM_FULL, VMEM_FULL, VMEM_FULL],
        out_specs=[pl.BlockSpec((SSD_L, 2 * SSD_INNER), lambda b, c: (b * nc + rc(c), 0)),
                   pl.BlockSpec((SSD_L, SSD_INNER), lambda b, c: (b * nc + rc(c), 0)),
                   pl.BlockSpec((SSD_L, SSD_N), lambda b, c: (b * nc + rc(c), 0)),
                   pl.BlockSpec((8, SSD_N), lambda b, c: (0, 0)),
                   pl.BlockSpec((8, SSD_INNER), lambda b, c: (0, 0))],
        out_shape=[jax.ShapeDtypeStruct((t, 2 * SSD_INNER), BF16), jax.ShapeDtypeStruct((t, SSD_INNER), BF16),
                   jax.ShapeDtypeStruct((t, SSD_N), F32), jax.ShapeDtypeStruct((8, SSD_N), F32),
                   jax.ShapeDtypeStruct((8, SSD_INNER), F32)],
        scratch_shapes=[pltpu.VMEM((SSD_G, SSD_N, SSD_GW), F32)])


def ada_fwd(c_all, w_cols, b_cols):
    def body(c_ref, w_ref, b_ref, o_ref):
        cv = c_ref[...]
        o_ref[...] = _dot_hi(cv * _sigmoid(cv), w_ref[...]) + b_ref[...]

    return pl.pallas_call(body, name="ada_fwd", out_shape=jax.ShapeDtypeStruct((c_all.shape[0], w_cols.shape[1]), F32),
                          compiler_params=pltpu.CompilerParams(vmem_limit_bytes=VMEM_LIMIT))(c_all, w_cols, b_cols)


def ada_bwd(c_all, dmod_cols, dmod_all):
    def body(c_ref, dc_ref, da_ref, gw_ref, gb_ref):
        cv = c_ref[...]
        gw_ref[...] = lax.dot_general(cv * _sigmoid(cv), dc_ref[...], (((0,), (0,)), ((), ())),
                                      precision=lax.Precision.HIGHEST, preferred_element_type=F32)
        gb_ref[...] = _colsum(da_ref[...])

    return pl.pallas_call(
        body, name="ada_bwd",
        out_shape=[jax.ShapeDtypeStruct((c_all.shape[1], dmod_cols.shape[1]), F32),
                   jax.ShapeDtypeStruct((1, dmod_all.shape[1]), F32)],
        compiler_params=pltpu.CompilerParams(vmem_limit_bytes=VMEM_LIMIT))(c_all, dmod_cols, dmod_all)


def _adam_update(g, w, m, v):
    m2 = ADAM_B1 * m + (1.0 - ADAM_B1) * g
    v2 = ADAM_B2 * v + (1.0 - ADAM_B2) * (g * g)
    m_hat = m2 / (1.0 - ADAM_B1 ** ADAM_STEP)
    v_hat = v2 / (1.0 - ADAM_B2 ** ADAM_STEP)
    return -ADAM_LR * (m_hat / (jnp.sqrt(v_hat) + ADAM_EPS) + ADAM_WD * w), m2, v2


def adamw(parts, w, m, v, name):
    n, r, c = parts.shape
    tr = r if r <= 256 else 128

    def body(p_ref, w_ref, m_ref, v_ref, g_ref, d_ref, nm_ref, nv_ref):
        g = p_ref[0].astype(F32)
        for s in range(1, n):
            g = g + p_ref[s].astype(F32)
        g_ref[0] = g
        d_ref[0], nm_ref[0], nv_ref[0] = _adam_update(g, w_ref[0], m_ref[0], v_ref[0])

    blk = pl.BlockSpec((1, tr, c), lambda i: (0, i, 0))
    return pl.pallas_call(
        body, name=name, grid=(r // tr,),
        in_specs=[pl.BlockSpec((n, tr, c), lambda i: (0, i, 0)), blk, blk, blk], out_specs=[blk] * 4,
        out_shape=[jax.ShapeDtypeStruct((1, r, c), F32)] * 4,
        compiler_params=_cp("parallel"),
    )(parts, w, m, v)


SMALL_SRC = {
    'pre_norm1': ('vin', 0, 1024), 'post_norm1': ('vmg', 1, 1024), 'b_gate': ('vmg', 0, 2048),
    'lru_conv_b': ('accl', 4, 1024), 'lru_wa': ('gwa', None, None), 'lru_ba': ('dvec', 0, 1024),
    'lru_wx': ('gwx', None, None), 'lru_bx': ('dvec', 1, 1024), 'lru_lambda': ('dvec', 2, 1024),
    'ssd_conv_b': ('accs', 4, 4096), 'ssd_dt_bias': ('hpg', 0, SSD_HEADS), 'ssd_a_log': ('hpg', 1, SSD_HEADS),
    'ssd_d': ('lpg', 1, SSD_INNER), 'ssd_norm_w': ('lpg', 0, SSD_INNER), 'pre_norm2': ('vmlp', 0, 1024),
    'post_norm2': ('vmlp', 1, 1024)}
SMALL_ACCS = ('vin', 'vmg', 'vmlp', 'dvec', 'accl', 'accs', 'hpg', 'lpg', 'gwa', 'gwx')
SMALL_RIDE = ('vmg', 'vmlp', 'dvec', 'hpg', 'lpg', 'gwa', 'gwx')


def adamw_small(gathered, params):
    names = tuple(params)
    na = len(SMALL_ACCS)

    def body(*refs):
        acc = {k: functools.reduce(lambda p, q: p + q, [refs[i][s] for s in range(NDEV)])
               for i, k in enumerate(SMALL_ACCS)}
        ins = refs[na:na + 3 * len(names)]
        outs = refs[na + 3 * len(names):]
        for j, k in enumerate(names):
            w_ref, m_ref, v_ref = ins[3 * j:3 * j + 3]
            src, row, width = SMALL_SRC[k]
            wv = w_ref[...]
            if row is None:
                g = acc[src]
            elif k == 'ssd_d':
                li = lax.broadcasted_iota(jnp.int32, (SSD_INNER, SSD_N), 0)
                hi = lax.broadcasted_iota(jnp.int32, (SSD_INNER, SSD_N), 1)
                g = _dot_hi(acc[src], jnp.where(jnp.right_shift(li, 6) == hi, 1.0, 0.0))[row:row + 1, :SSD_HEADS]
            else:
                g = acc[src][row:row + 1, :width]
            if k == 'lru_lambda':
                g = g * (-1.0 / (1.0 + jnp.exp(wv)))
            if k == 'ssd_a_log':
                g = g * (-jnp.exp(wv))
            o = outs[4 * j:4 * j + 4]
            o[0][...] = g
            o[1][...], o[2][...], o[3][...] = _adam_update(g, wv, m_ref[...], v_ref[...])
        outs[-2][...] = acc['accl'][0:4, :]
        outs[-1][...] = acc['accs'][0:4, :]

    flat = [a for k in names for a in params[k]]
    out_shape = [jax.ShapeDtypeStruct(params[k][0].shape, F32) for k in names for _ in range(4)]
    out_shape += [jax.ShapeDtypeStruct((4, D_MODEL), F32), jax.ShapeDtypeStruct((4, 2 * SSD_INNER), F32)]
    res = pl.pallas_call(body, name="adamw_small", out_shape=out_shape,
                         compiler_params=pltpu.CompilerParams(vmem_limit_bytes=VMEM_LIMIT))(
        *[gathered[k] for k in SMALL_ACCS], *flat)
    return {k: res[4 * j:4 * j + 4] for j, k in enumerate(names)}, res[-2], res[-1]


def _dev_index(px, py, pc):
    return 4 * px + 2 * py + pc


class _Exchange:
    def __init__(self, arrs):
        self.arrs = list(arrs)
        self.na = len(self.arrs)
        self.scratch = [pltpu.SemaphoreType.DMA((7 * self.na,)), pltpu.SemaphoreType.DMA((7 * self.na,)),
                        pltpu.SemaphoreType.DMA((self.na,))]


class Gather(_Exchange):
    def __init__(self, arrs):
        super().__init__(arrs)
        self.out_shape = [jax.ShapeDtypeStruct((NDEV,) + a.shape, a.dtype) for a in self.arrs]

    def _plan(self, ins, outs, sems):
        na = self.na
        send_sems, recv_sems, local_sems = sems
        x, y, c = lax.axis_index("x"), lax.axis_index("y"), lax.axis_index("c")
        me, sibling = (x, y, c), (x, y, 1 - c)
        chips = [(1 - x, y), (x, 1 - y), (1 - x, 1 - y)]

        def copy(a, k, block, to, src=None):
            dst = outs[a].at[_dev_index(*block)]
            return pltpu.make_async_remote_copy(
                src_ref=dst if src is None else src, dst_ref=dst, send_sem=send_sems.at[a * 7 + k],
                recv_sem=recv_sems.at[a * 7 + k], device_id=to, device_id_type=MESH)

        mine = [pltpu.make_async_copy(ins[a], outs[a].at[_dev_index(*me)], local_sems.at[a]) for a in range(na)]
        first = []
        for a in range(na):
            first.append(copy(a, 0, me, sibling, src=ins[a]))
            first += [copy(a, 1 + j, me, (*chip, c), src=ins[a]) for j, chip in enumerate(chips)]
        return copy, mine, first, me, sibling, chips, c

    def start(self, ins, outs, sems):
        _, mine, first, *_ = self._plan(ins, outs, sems)
        for cp in mine + first:
            cp.start()

    def finish(self, ins, outs, sems):
        copy, mine, first, me, sibling, chips, c = self._plan(ins, outs, sems)
        passed = []
        for j, chip in enumerate(chips):
            for a in range(self.na):
                copy(a, 1 + j, (*chip, c), me).wait_recv()
                cp = copy(a, 4 + j, (*chip, c), sibling)
                cp.start()
                passed.append(cp)
        for a in range(self.na):
            copy(a, 0, sibling, me).wait_recv()
            for j, chip in enumerate(chips):
                copy(a, 4 + j, (*chip, 1 - c), me).wait_recv()
        for cp in first + passed:
            cp.wait_send()
        for cp in mine:
            cp.wait()


class GatherRelay(Gather):
    def _plan(self, ins, outs, sems):
        na = self.na
        send_sems, recv_sems, local_sems = sems
        x, y, c = lax.axis_index("x"), lax.axis_index("y"), lax.axis_index("c")
        me, sibling = (x, y, c), (x, y, 1 - c)
        xn, yn, dg = (1 - x, y), (x, 1 - y), (1 - x, 1 - y)
        south = c == 0
        pick = lambda a, b: tuple(jnp.where(south, p, q) for p, q in zip(a, b))
        relay_to = pick(yn, xn)
        relay_of = pick(xn, yn)

        def copy(a, k, block, to, src=None):
            dst = outs[a].at[_dev_index(*block)]
            return pltpu.make_async_remote_copy(
                src_ref=dst if src is None else src, dst_ref=dst, send_sem=send_sems.at[a * 7 + k],
                recv_sem=recv_sems.at[a * 7 + k], device_id=to, device_id_type=MESH)

        mine = [pltpu.make_async_copy(ins[a], outs[a].at[_dev_index(*me)], local_sems.at[a]) for a in range(na)]
        first = []
        for a in range(na):
            first += [copy(a, 0, me, sibling, src=ins[a]), copy(a, 1, me, (*xn, c), src=ins[a]),
                      copy(a, 2, me, (*yn, c), src=ins[a])]
        return copy, mine, first, me, sibling, (xn, yn, dg), c, relay_to, relay_of

    def start(self, ins, outs, sems):
        _, mine, first, *_ = self._plan(ins, outs, sems)
        for cp in mine + first:
            cp.start()

    def finish(self, ins, outs, sems):
        copy, mine, first, me, sibling, (xn, yn, dg), c, relay_to, relay_of = self._plan(ins, outs, sems)
        later = []
        for a in range(self.na):
            copy(a, 1, (*xn, c), me).wait_recv()
            copy(a, 2, (*yn, c), me).wait_recv()
            later.append(copy(a, 3, (*relay_of, c), (*relay_to, c)))
            later += [copy(a, 4, (*xn, c), sibling), copy(a, 5, (*yn, c), sibling)]
            for cp in later[-3:]:
                cp.start()
        for a in range(self.na):
            copy(a, 3, (*dg, c), me).wait_recv()
            cp = copy(a, 6, (*dg, c), sibling)
            cp.start()
            later.append(cp)
        for a in range(self.na):
            copy(a, 0, sibling, me).wait_recv()
            for k, chip in ((4, xn), (5, yn), (6, dg)):
                copy(a, k, (*chip, 1 - c), me).wait_recv()
        for cp in first + later:
            cp.wait_send()
        for cp in mine:
            cp.wait()


class Scatter(_Exchange):
    def __init__(self, arrs):
        super().__init__(arrs)
        self.out_shape = [jax.ShapeDtypeStruct(a.shape, a.dtype) for a in self.arrs]

    def _plan(self, ins, outs, sems, arrivals):
        send_sems, recv_sems, local_sems = sems
        x, y, c = lax.axis_index("x"), lax.axis_index("y"), lax.axis_index("c")
        me = _dev_index(x, y, c)
        masks = [(mx, my, mc) for mx in (0, 1) for my in (0, 1) for mc in (0, 1)][1:]
        flip = lambda v, bit: 1 - v if bit else v
        mine = [pltpu.make_async_copy(ins[a].at[me], outs[a].at[me], local_sems.at[a]) for a in range(self.na)]
        sends, recvs = [], []
        for k, (mx, my, mc) in enumerate(masks):
            peer = (flip(x, mx), flip(y, my), flip(c, mc))
            pidx = _dev_index(*peer)
            for a in range(self.na):
                on = dict(send_sem=send_sems.at[a * 7 + k], recv_sem=recv_sems.at[a * 7 + k], device_id=peer,
                          device_id_type=MESH)
                sends.append(pltpu.make_async_remote_copy(src_ref=ins[a].at[pidx], dst_ref=outs[a].at[me], **on))
                if arrivals:
                    recvs.append(pltpu.make_async_remote_copy(src_ref=ins[a].at[pidx], dst_ref=outs[a].at[pidx], **on))
        return mine, sends, recvs

    def start(self, ins, outs, sems):
        mine, sends, _ = self._plan(ins, outs, sems, arrivals=False)
        for cp in mine + sends:
            cp.start()

    def finish(self, ins, outs, sems):
        mine, sends, recvs = self._plan(ins, outs, sems, arrivals=True)
        for cp in recvs:
            cp.wait_recv()
        for cp in sends:
            cp.wait_send()
        for cp in mine:
            cp.wait()


def exchange_call(ex, name):
    na = ex.na

    def body(*refs):
        ins, outs, sems = refs[:na], refs[na:2 * na], refs[2 * na:]
        ex.start(ins, outs, sems)
        ex.finish(ins, outs, sems)

    return pl.pallas_call(body, name=name, in_specs=[ANY] * na, out_specs=[ANY] * na, out_shape=ex.out_shape,
                          scratch_shapes=ex.scratch)(*ex.arrs)


def all_gather(arrs, name, relay=False):
    return exchange_call((GatherRelay if relay else Gather)(arrs), name)


def _call(body, *, name, grid, in_specs, out_specs, out_shape, scratch_shapes=(), sem, args, side=None):
    if side is None:
        outs = pl.pallas_call(body, name=name, grid=grid, in_specs=list(in_specs), out_specs=list(out_specs),
                              out_shape=list(out_shape), scratch_shapes=list(scratch_shapes),
                              compiler_params=_cp(*sem))(*args)
        return outs, []
    ni, no, ns, na = len(in_specs), len(out_specs), len(scratch_shapes), side.na

    def wrapped(*refs):
        ins, s_in = refs[:ni], refs[ni:ni + na]
        outs, s_out = refs[ni + na:ni + na + no], refs[ni + na + no:ni + 2 * na + no]
        scr, sems = refs[ni + 2 * na + no:ni + 2 * na + no + ns], refs[ni + 2 * na + no + ns:]
        pids = [pl.program_id(i) for i in range(len(grid))]
        first = functools.reduce(lambda p, q: p & q, [p == 0 for p in pids])
        last = functools.reduce(lambda p, q: p & q, [p == g - 1 for p, g in zip(pids, grid)])

        @pl.when(first)
        def _():
            side.start(s_in, s_out, sems)

        body(*ins, *outs, *scr)

        @pl.when(last)
        def _():
            side.finish(s_in, s_out, sems)

    outs = pl.pallas_call(
        wrapped, name=name, grid=grid, in_specs=list(in_specs) + [ANY] * na, out_specs=list(out_specs) + [ANY] * na,
        out_shape=list(out_shape) + side.out_shape, scratch_shapes=list(scratch_shapes) + side.scratch,
        compiler_params=_cp(*["arbitrary"] * len(grid)))(*args, *side.arrs)
    return outs[:no], outs[no:]


WEIGHTS = ('w_ada', 'b_ada', 'pre_norm1', 'post_norm1', 'w_in', 'b_gate', 'lru_conv_w', 'lru_conv_b', 'lru_wa',
           'lru_ba', 'lru_wx', 'lru_bx', 'lru_lambda', 'w_pa', 'ssd_conv_w', 'ssd_conv_b', 'ssd_dt_bias', 'ssd_a_log',
           'ssd_d', 'ssd_norm_w', 'w_pb', 'w_out', 'pre_norm2', 'post_norm2', 'w_ff1', 'w_ff2')
BIG = ('w_in', 'w_pa', 'w_pb', 'w_out', 'w_ff1', 'w_ff2')
REPL = ('pre_norm1', 'post_norm1', 'b_gate', 'lru_conv_b', 'lru_wa', 'lru_ba', 'lru_wx', 'lru_bx', 'lru_lambda',
        'ssd_conv_b', 'ssd_dt_bias', 'ssd_a_log', 'ssd_d', 'ssd_norm_w', 'pre_norm2', 'post_norm2')
LANES = 1024


def _rows(n):
    return -(-n // LANES)


def _pack(vals, total_rows):
    parts = []
    for v in vals:
        f = v.reshape(-1).astype(F32)
        parts.append(jnp.pad(f, (0, _rows(f.shape[0]) * LANES - f.shape[0])))
    flat = jnp.concatenate(parts)
    return jnp.pad(flat.reshape(-1, LANES), ((0, total_rows - flat.shape[0] // LANES), (0, 0)))


def _unpack(slab, shapes):
    out, r = [], 0
    for s in shapes:
        n = int(np.prod(s))
        out.append(slab[r:r + _rows(n)].reshape(-1)[:n].reshape(s))
        r += _rows(n)
    return out


def _block_diag4(w):
    w4 = w.reshape(4, 4, 64, 64)
    eye = jnp.eye(4, dtype=w.dtype)
    return (w4[:, :, :, None, :] * eye[None, :, None, :, None]).reshape(4, LRU_BLOCK, LRU_BLOCK)


def _diag_blocks4(m):
    m5 = m.reshape(4, 4, 64, 4, 64)
    return jnp.stack([m5[:, a, :, a, :] for a in range(4)], axis=1).reshape(LRU_HEADS, 64, 64)


def kernel(x, c, w_ada, b_ada, pre_norm1, post_norm1, w_in, b_gate, lru_conv_w, lru_conv_b, lru_wa, lru_ba, lru_wx, lru_bx, lru_lambda, w_pa, ssd_conv_w, ssd_conv_b, ssd_dt_bias, ssd_a_log, ssd_d, ssd_norm_w, w_pb, w_out, pre_norm2, post_norm2, w_ff1, w_ff2, loss_target, m_w_ada, m_b_ada, m_pre_norm1, m_post_norm1, m_w_in, m_b_gate, m_lru_conv_w, m_lru_conv_b, m_lru_wa, m_lru_ba, m_lru_wx, m_lru_bx, m_lru_lambda, m_w_pa, m_ssd_conv_w, m_ssd_conv_b, m_ssd_dt_bias, m_ssd_a_log, m_ssd_d, m_ssd_norm_w, m_w_pb, m_w_out, m_pre_norm2, m_post_norm2, m_w_ff1, m_w_ff2, v_w_ada, v_b_ada, v_pre_norm1, v_post_norm1, v_w_in, v_b_gate, v_lru_conv_w, v_lru_conv_b, v_lru_wa, v_lru_ba, v_lru_wx, v_lru_bx, v_lru_lambda, v_w_pa, v_ssd_conv_w, v_ssd_conv_b, v_ssd_dt_bias, v_ssd_a_log, v_ssd_d, v_ssd_norm_w, v_w_pb, v_w_out, v_pre_norm2, v_post_norm2, v_w_ff1, v_w_ff2):
    given = dict(locals())
    w = {k: given[k] for k in WEIGHTS}
    mom = {k: given["m_" + k] for k in WEIGHTS}
    var = {k: given["v_" + k] for k in WEIGHTS}
    nb, seq, _ = x.shape
    assert nb == 2 and seq % 512 == 0, (nb, seq)
    t = nb * seq
    me = _dev_index(lax.axis_index("x"), lax.axis_index("y"), lax.axis_index("c"))
    x2 = x.reshape(t, D_MODEL)
    tgt2 = loss_target.reshape(t, D_MODEL)
    ada_cols = w_ada.shape[2]

    slab = jnp.zeros((16, LANES), F32)
    slab = slab.at[0:nb].set(c)
    slab = slab.at[2:6, 0:lru_conv_w.shape[2]].set(lru_conv_w[0])
    slab = slab.at[6:10, 0:ssd_conv_w.shape[2]].set(ssd_conv_w[0])
    g1, gw_in = all_gather([slab, w_in.astype(BF16)], "gather_cond_w_in", relay=True)
    c_all = g1[:, 0:nb].reshape(NDEV * nb, D_MODEL)
    lru_cw = g1[:, 2:6, 0:lru_conv_w.shape[2]].transpose(1, 0, 2).reshape(4, D_MODEL)
    ssd_cw = g1[:, 6:10, 0:ssd_conv_w.shape[2]].transpose(1, 0, 2).reshape(4, 2 * SSD_INNER)
    b_cols = lax.dynamic_slice(b_ada, (0, me * ada_cols), (1, ada_cols))
    mod_cols = ada_fwd(c_all, w_ada[0], b_cols)
    (g2,) = all_gather([mod_cols], "gather_mod")
    mod_all = g2.transpose(1, 0, 2).reshape(NDEV * nb, N_MOD * D_MODEL)
    mod_mine = lax.dynamic_slice(mod_all, (me * nb, 0), (nb, N_MOD * D_MODEL)).reshape(nb, N_MOD, D_MODEL)
    mod8 = jnp.pad(mod_mine, ((0, 0), (0, 8 - N_MOD), (0, 0)))

    shard = IN_DIM // NDEV
    kd, od = DT_COL0 // shard, DT_COL0 % shard
    assert od + SSD_HEADS <= shard
    gb = gw_in[:, 0]
    w_main = jnp.concatenate([gb[k] for k in range(kd)] + [gb[kd][:, :od], gb[kd][:, od + SSD_HEADS:]]
                             + [gb[k] for k in range(kd + 1, NDEV)], axis=1)
    w_dt = jnp.pad(gb[kd][:, od:od + SSD_HEADS], ((0, 0), (0, 128 - SSD_HEADS)))

    wa_bd = _block_diag4(lru_wa[0]).astype(BF16)
    wx_bd = _block_diag4(lru_wx[0]).astype(BF16)
    lam = lru_lambda[0]
    vec = _pack([lru_ba, lru_bx, jax.nn.softplus(-lam)], 8)
    tri, triu = ssd_consts()
    hp, lp = ssd_params(ssd_dt_bias[0], ssd_a_log[0], ssd_d[0], ssd_norm_w[0])

    rest = Gather([w[k].astype(BF16) for k in BIG[1:]])
    cw_all = jnp.concatenate([lru_cw, ssd_cw], axis=1)
    cb_all = jnp.concatenate([lru_conv_b, ssd_conv_b], axis=1)
    (proj, h1t, dtraw, xa, xbc), gw = in_proj_fwd(x2, mod8, pre_norm1, w_main, w_dt, cw_all, cb_all, seq, side=rest)
    w_pa_f = gw[0].reshape(D_MODEL, D_MODEL)
    w_pb_f = gw[1].reshape(SSD_INNER, D_MODEL)
    w_out_f = gw[2].reshape(D_MODEL, D_MODEL)
    w_ff1_f = gw[3][:, 0]
    w_ff2_f = gw[4].reshape(D_FF, D_MODEL)
    ya_in, hst = lru_fwd(xa, proj, wa_bd, wx_bd, vec, nb, seq)
    yb_in, states = ssd_fwd(xbc, proj, dtraw, hp, lp, tri, nb, seq)
    yab, out1, x1 = merge_fwd(ya_in, yb_in, proj, x2, mod8, b_gate, post_norm1, w_pa_f, w_pb_f, w_out_f, seq)

    dx1, h2, da1, act, dy2, loss8, vacc_mlp, dmod_mlp = mlp_fwd_bwd(
        x1, tgt2, mod8, pre_norm2, post_norm2, w_ff1_f, w_ff2_f, nb, seq)
    wg = dict(out_dtype=BF16, ta=True, tm=1024, tn=1024, tk=1024)
    dw_ff1 = matmul(h2, da1, name="wgrad_ff1", blocked_out=D_FF // NDEV, **wg)
    dw_ff2 = matmul(act, dy2, name="wgrad_ff2", **wg)
    dya_in, dyb_in, dgates, dyab, dout1, merged, vacc_mg, dmod_mg = merge_bwd(
        dx1, out1, yab, proj, mod8, b_gate, post_norm1, w_pa_f, w_pb_f, w_out_f, nb, seq)
    dw_out = matmul(merged, dout1, name="wgrad_out", **wg)
    dw_pa = matmul(ya_in, dyab, name="wgrad_pa", n=D_MODEL, b_off=0, **wg)
    dw_pb = matmul(yb_in, dyab, name="wgrad_pb", n=D_MODEL, b_off=1, **wg)
    by_rows = lambda g: g.reshape(NDEV, g.shape[0] // NDEV, g.shape[1])
    (dxa, dlg, dwa_bd, dwx_bd, dvec), parts_ff = lru_bwd(
        dya_in, xa, proj, hst, wa_bd, wx_bd, vec, nb, seq, side=Scatter([dw_ff1, by_rows(dw_ff2)]))
    (dxbc, dz, ddt, hpg, lpg), parts_mg = ssd_bwd(xbc, proj, dtraw, hp, lp, tri, triu, states, dyb_in, nb, seq,
                                                  side=Scatter([by_rows(dw_pa), by_rows(dw_pb), by_rows(dw_out)]))
    ddt_b = ddt.astype(BF16)
    accs = dict(vmg=vacc_mg, vmlp=vacc_mlp, dvec=dvec, hpg=hpg, lpg=lpg,
                gwa=_diag_blocks4(dwa_bd).reshape(LRU_HEADS * 64, 64), gwx=_diag_blocks4(dwx_bd).reshape(LRU_HEADS * 64, 64))
    (dw_main, dw_dt, dlx, dxr, acc_l, acc_s), g_small = in_proj_wgrad(
        h1t, proj, dxa, dxbc, dlg, dz, dgates, ddt_b, cw_all, cb_all, seq, side=Gather([accs[k] for k in SMALL_RIDE]))
    pieces = (dlx, dlg, dz, dxr, dgates)
    cut = lambda k: dw_main[:, k * shard - (SSD_HEADS if k > kd else 0):(k + 1) * shard - (SSD_HEADS if k >= kd else 0)]
    blk_dt = jnp.concatenate([dw_main[:, kd * shard:DT_COL0], dw_dt[:, :SSD_HEADS],
                              dw_main[:, DT_COL0:(kd + 1) * shard - SSD_HEADS]], axis=1)
    dw_blocks = jnp.stack([blk_dt if k == kd else cut(k) for k in range(NDEV)])
    (grad_x, vacc_in, dmod_in), parts_in = in_proj_bwd(pieces, ddt_b, dx1, x2, mod8, pre_norm1, w_main, w_dt, nb, seq,
                                                       side=Scatter([dw_blocks]))
    parts = dict(zip(BIG, (parts_in[0], *parts_mg, *parts_ff)))

    dmod = (dmod_in + dmod_mg + dmod_mlp)[:, :N_MOD].reshape(nb, N_MOD * D_MODEL)
    g3, g_vin, g_accl, g_accs = all_gather([jnp.pad(dmod, ((0, 8 - nb), (0, 0))), vacc_in, acc_l, acc_s], "gather_dmod")
    dmod_all = g3[:, :nb].reshape(NDEV * nb, N_MOD * D_MODEL)
    dmod_cols = lax.dynamic_slice(dmod_all, (0, me * ada_cols), (NDEV * nb, ada_cols))
    g_w_ada, g_b_ada = ada_bwd(c_all, dmod_cols, dmod_all)

    res = {}
    for k in BIG:
        res[k] = adamw(parts[k], w[k], mom[k], var[k], "adamw_" + k)
    res['w_ada'] = adamw(g_w_ada[None], w_ada, m_w_ada, v_w_ada, "adamw_w_ada")

    gathered = dict(zip(SMALL_RIDE, g_small), vin=g_vin, accl=g_accl, accs=g_accs)
    view = lambda a: a.reshape(-1, a.shape[-1])
    res_a, g_lru_cw, g_ssd_cw = adamw_small(gathered, {k: (view(w[k]), view(mom[k]), view(var[k])) for k in REPL})
    res.update(res_a)
    lcw, scw = lru_conv_w.shape[2], ssd_conv_w.shape[2]
    sharded = {'b_ada': g_b_ada[None], 'lru_conv_w': lax.dynamic_slice(g_lru_cw, (0, me * lcw), (4, lcw))[None],
               'ssd_conv_w': lax.dynamic_slice(g_ssd_cw, (0, me * scw), (4, scw))[None]}
    for k, g in sharded.items():
        as3 = lambda a: a.reshape(g.shape)
        res[k] = adamw(g, as3(w[k]), as3(mom[k]), as3(var[k]), "adamw_" + k)

    loss = lax.psum(loss8[0, 0], ("x", "y", "c"))
    outs = [[res[k][j].reshape(w[k].shape) for k in WEIGHTS] for j in range(4)]
    return (loss, grad_x.reshape(x.shape), *outs[0], *outs[1], *outs[2], *outs[3])
```

```python
import functools

import numpy as np
import jax
import jax.numpy as jnp
from jax import lax
from jax.experimental import pallas as pl
from jax.experimental.pallas import tpu as pltpu

F32 = jnp.float32
BF16 = jnp.bfloat16

D_MODEL = 1024
LRU_HEADS = 16
LRU_BLOCK = 256
LRU_C = 8.0
SSD_INNER = 2048
SSD_HEADS = 32
SSD_P = 64
SSD_G = 8
SSD_N = 128
SSD_L = 128
SSD_GW = SSD_INNER // SSD_G
D_FF = 4096
N_MOD = 6
EPS = 1e-6
NDEV = 8

C_LRU_X, C_LRU_G, C_Z, C_XBC, C_GATES, PROJ_MAIN = 0, 1024, 2048, 4096, 8192, 10240
IN_DIM = 10272
DT_COL0 = 8192
HALO = 16
SSD_FWD_CPS = 1
HT_TOK = 512

ADAM_LR, ADAM_B1, ADAM_B2, ADAM_EPS, ADAM_WD, ADAM_STEP = 0.001, 0.9, 0.999, 1e-08, 0.01, 10

VMEM_LIMIT = 60 * 1024 * 1024
MESH = pl.DeviceIdType.MESH
ANY = pl.BlockSpec(memory_space=pl.ANY)
VMEM_FULL = pl.BlockSpec(memory_space=pltpu.VMEM)


def _cp(*sem):
    return pltpu.CompilerParams(dimension_semantics=sem, vmem_limit_bytes=VMEM_LIMIT)


def _dot(a, b):
    return jnp.dot(a, b, preferred_element_type=F32)


def _dot_nt(a, b):
    return lax.dot_general(a, b, (((1,), (1,)), ((), ())), preferred_element_type=F32)


def _dot_tn(a, b):
    return lax.dot_general(a, b, (((0,), (0,)), ((), ())), preferred_element_type=F32)


def _dot_hi(a, b):
    return jnp.dot(a, b, precision=lax.Precision.HIGHEST, preferred_element_type=F32)


def _sigmoid(x):
    return 1.0 / (1.0 + jnp.exp(-x))


def _gelu_and_grad(x):
    k0, k1 = 0.7978845608028654, 0.044715
    t = jnp.tanh(k0 * (x + k1 * x * x * x))
    g = 0.5 * x * (1.0 + t)
    dg = 0.5 * (1.0 + t) + 0.5 * x * (1.0 - t * t) * k0 * (1.0 + 3.0 * k1 * x * x)
    return g, dg


def _neg_expm1(y):
    p = 1.0 + y * (1.0 / 7.0)
    p = 1.0 + y * (1.0 / 6.0) * p
    p = 1.0 + y * (1.0 / 5.0) * p
    p = 1.0 + y * (1.0 / 4.0) * p
    p = 1.0 + y * (1.0 / 3.0) * p
    p = 1.0 + y * 0.5 * p
    return jnp.where(y > -0.3, -y * p, 1.0 - jnp.exp(y))


def _colsum(v):
    return jnp.sum(v, axis=0, keepdims=True)


def _rowmean(v):
    return jnp.mean(v, axis=-1, keepdims=True)


def matmul(a, b, *, ta=False, tb=False, out_dtype=F32, tm, tn, tk, name, n=None, b_off=0, blocked_out=False):
    m = a.shape[1] if ta else a.shape[0]
    kdim = a.shape[0] if ta else a.shape[1]
    n = n or (b.shape[0] if tb else b.shape[1])
    tm, tn, tk = min(tm, m), min(tn, n), min(tk, kdim)
    nk = kdim // tk
    dn = (((0 if ta else 1,), (1 if tb else 0,)), ((), ()))
    bw = blocked_out or tn

    def body(a_ref, b_ref, o_ref, acc_ref):
        k = pl.program_id(2)
        p = lax.dot_general(a_ref[...], b_ref[...], dn, preferred_element_type=F32)

        def emit(v):
            if blocked_out:
                for q in range(tn // bw):
                    o_ref[q] = v[:, q * bw:(q + 1) * bw].astype(out_dtype)
            else:
                o_ref[...] = v.astype(out_dtype)

        if nk == 1:
            emit(p)
        else:
            @pl.when(k == 0)
            def _():
                acc_ref[...] = p

            @pl.when(k > 0)
            def _():
                acc_ref[...] += p

            @pl.when(k == nk - 1)
            def _():
                emit(acc_ref[...])

    a_spec = pl.BlockSpec((tk, tm), lambda i, j, k: (k, i)) if ta else pl.BlockSpec((tm, tk), lambda i, j, k: (i, k))
    b_spec = (pl.BlockSpec((tn, tk), lambda i, j, k: (j, k)) if tb
              else pl.BlockSpec((tk, tn), lambda i, j, k: (k, j + b_off)))
    if blocked_out:
        o_spec, o_shape = pl.BlockSpec((tn // bw, tm, bw), lambda i, j, k: (j, i, 0)), (n // bw, m, bw)
    else:
        o_spec, o_shape = pl.BlockSpec((tm, tn), lambda i, j, k: (i, j)), (m, n)
    return pl.pallas_call(
        body, name=name, grid=(m // tm, n // tn, nk),
        in_specs=[a_spec, b_spec], out_specs=o_spec,
        out_shape=jax.ShapeDtypeStruct(o_shape, out_dtype),
        scratch_shapes=[pltpu.VMEM((tm, tn), F32)],
        compiler_params=_cp("parallel", "parallel", "arbitrary"),
    )(a, b)


def _conv_tile(j, tn):
    return jnp.where(j == 0, 0, jnp.clip(j - C_XBC // tn + 1, 1, 2 * SSD_INNER // tn))


def in_proj_fwd(x2, mod8, pre1, w_main, w_dt, cw, cb, seq, side=None):
    t = x2.shape[0]
    tm = min(1024, seq)
    tn = 1024
    per_seq = seq // tm
    j_xbc = C_XBC // tn
    n_xbc = 2 * SSD_INNER // tn
    cs = 256

    def body(x_ref, mod_ref, pre_ref, w_ref, wdt_ref, cw_ref, cb_ref, proj_ref, h_ref, dt_ref, xa_ref, xbc_ref, ds_ref,
             h_scr, carry_scr):
        i, j = pl.program_id(0), pl.program_id(1)

        @pl.when(j == 0)
        def _():
            xv = x_ref[...]
            y = xv * lax.rsqrt(_rowmean(xv * xv) + EPS) * pre_ref[...]
            m = mod_ref[0]
            hf = y * (1.0 + m[1:2, :]) + m[0:1, :]
            h = hf.astype(BF16)
            h_scr[...] = h
            hft = hf.T.astype(BF16)
            for q in range(tm // HT_TOK):
                h_ref[q] = hft[:, q * HT_TOK:(q + 1) * HT_TOK]
            dt_ref[...] = _dot(h, wdt_ref[...])

        def project(c0=0, width=tn):
            pb = _dot(h_scr[...], w_ref[:, c0:c0 + width]).astype(BF16)
            proj_ref[:, c0:c0 + width] = pb
            return pb

        def conv(o_ref, slot, act):
            first = lax.rem(i, per_seq) == 0
            for c0 in range(0, tn, cs):
                cur = project(c0, cs).astype(F32)
                prev = jnp.where(first, 0.0, carry_scr[slot, :, c0:c0 + cs])
                carry_scr[slot, :, c0:c0 + cs] = cur[tm - HALO:, :]
                xx = jnp.concatenate([prev, cur], axis=0)
                w = cw_ref[:, c0:c0 + cs]
                acc = cur * w[3:4, :] + cb_ref[:, c0:c0 + cs]
                for d in (1, 2, 3):
                    acc = acc + pltpu.roll(xx, d, axis=0)[HALO:, :] * w[3 - d:4 - d, :]
                if act:
                    sg = _sigmoid(acc)
                    ds_ref[:, c0:c0 + cs] = (sg * (1.0 + acc * (1.0 - sg))).astype(BF16)
                    acc = acc * sg
                o_ref[:, c0:c0 + cs] = acc.astype(BF16)

        is_xbc = (j >= j_xbc) & (j < j_xbc + n_xbc)

        @pl.when(j == 0)
        def _():
            conv(xa_ref, 0, False)

        @pl.when(is_xbc)
        def _():
            conv(xbc_ref, j - j_xbc + 1, True)

        @pl.when((j > 0) & jnp.logical_not(is_xbc))
        def _():
            project()

    return _call(
        body, name="in_proj_fwd", grid=(t // tm, PROJ_MAIN // tn), side=side, sem=("arbitrary", "arbitrary"),
        args=(x2, mod8, pre1, w_main, w_dt, cw, cb),
        in_specs=[pl.BlockSpec((tm, D_MODEL), lambda i, j: (i, 0)),
                  pl.BlockSpec((1, 8, D_MODEL), lambda i, j: (i // per_seq, 0, 0)),
                  pl.BlockSpec((1, D_MODEL), lambda i, j: (0, 0)),
                  pl.BlockSpec((D_MODEL, tn), lambda i, j: (0, j)),
                  pl.BlockSpec((D_MODEL, 128), lambda i, j: (0, 0)),
                  pl.BlockSpec((4, tn), lambda i, j: (0, _conv_tile(j, tn))),
                  pl.BlockSpec((1, tn), lambda i, j: (0, _conv_tile(j, tn)))],
        out_specs=[pl.BlockSpec((tm, tn), lambda i, j: (i, j)),
                   pl.BlockSpec((tm // HT_TOK, D_MODEL, HT_TOK), lambda i, j: (i, 0, 0)),
                   pl.BlockSpec((tm, 128), lambda i, j: (i, 0)),
                   pl.BlockSpec((tm, tn), lambda i, j: (i, 0)),
                   pl.BlockSpec((tm, tn), lambda i, j: (i, jnp.clip(j - j_xbc, 0, n_xbc - 1))),
                   pl.BlockSpec((tm, tn), lambda i, j: (i, jnp.clip(j - j_xbc, 0, n_xbc - 1)))],
        out_shape=[jax.ShapeDtypeStruct((t, PROJ_MAIN), BF16), jax.ShapeDtypeStruct((t // HT_TOK, D_MODEL, HT_TOK), BF16),
                   jax.ShapeDtypeStruct((t, 128), F32), jax.ShapeDtypeStruct((t, D_MODEL), BF16),
                   jax.ShapeDtypeStruct((t, 2 * SSD_INNER), BF16), jax.ShapeDtypeStruct((t, 2 * SSD_INNER), BF16)],
        scratch_shapes=[pltpu.VMEM((tm, D_MODEL), BF16), pltpu.VMEM((1 + n_xbc, HALO, tn), F32)])


def _lru_gates(xa, wa_ref, wx_ref, ba, bx, sp):
    nblk = D_MODEL // LRU_BLOCK
    pr = jnp.concatenate([_dot(xa[:, j * LRU_BLOCK:(j + 1) * LRU_BLOCK], wa_ref[j]) for j in range(nblk)], axis=1) + ba
    pi = jnp.concatenate([_dot(xa[:, j * LRU_BLOCK:(j + 1) * LRU_BLOCK], wx_ref[j]) for j in range(nblk)], axis=1) + bx
    r = _sigmoid(pr)
    i = _sigmoid(pi)
    log_a = (-LRU_C * r) * sp
    return r, i, jnp.exp(log_a), _neg_expm1(2.0 * log_a)


def lru_fwd(xa, proj, wa_bd, wx_bd, vec, nb, seq):
    t = xa.shape[0]
    tc = min(512, seq)
    nk = seq // tc
    gb = C_LRU_G // D_MODEL

    def body(xa_ref, g_ref, wa_ref, wx_ref, vec_ref, ya_ref, h_ref, a_scr, u_scr, hc_scr):
        @pl.when(pl.program_id(1) == 0)
        def _():
            hc_scr[...] = jnp.zeros_like(hc_scr)

        xa_v = xa_ref[...]
        v = vec_ref[...]
        r, i, a, e = _lru_gates(xa_v, wa_ref, wx_ref, v[0:1, :], v[1:2, :], v[2:3, :])
        a_scr[...] = a
        u_scr[...] = jnp.sqrt(e) * (i * xa_v.astype(F32))
        row = lax.broadcasted_iota(jnp.int32, (8, 1), 0)

        def tile(j, h):
            r0 = pl.multiple_of(j * 8, 8)
            av, uv = a_scr[pl.ds(r0, 8), :], u_scr[pl.ds(r0, 8), :]
            for d in (1, 2, 4):
                uv = uv + av * jnp.where(row >= d, pltpu.roll(uv, d, axis=0), 0.0)
                av = av * jnp.where(row >= d, pltpu.roll(av, d, axis=0), 1.0)
            hv = uv + av * h
            h_ref[pl.ds(r0, 8), :] = hv
            return hv[7:8, :]

        hc_scr[...] = lax.fori_loop(0, tc // 8, tile, hc_scr[...], unroll=2)
        gel, _ = _gelu_and_grad(g_ref[...].astype(F32))
        ya_ref[...] = (h_ref[...] * gel).astype(BF16)

    return pl.pallas_call(
        body, name="lru_fwd", grid=(nb, nk),
        in_specs=[pl.BlockSpec((tc, D_MODEL), lambda b, k: (b * nk + k, 0)),
                  pl.BlockSpec((tc, D_MODEL), lambda b, k: (b * nk + k, gb)),
                  VMEM_FULL, VMEM_FULL, VMEM_FULL],
        out_specs=[pl.BlockSpec((tc, D_MODEL), lambda b, k: (b * nk + k, 0)),
                   pl.BlockSpec((tc, D_MODEL), lambda b, k: (b * nk + k, 0))],
        out_shape=[jax.ShapeDtypeStruct((t, D_MODEL), BF16), jax.ShapeDtypeStruct((t, D_MODEL), F32)],
        scratch_shapes=[pltpu.VMEM((tc, D_MODEL), F32), pltpu.VMEM((tc, D_MODEL), F32), pltpu.VMEM((1, D_MODEL), F32)],
        compiler_params=_cp("arbitrary", "arbitrary"),
    )(xa, proj, wa_bd, wx_bd, vec)


def lru_bwd(dya, xa, proj, h, wa_bd, wx_bd, vec, nb, seq, side=None):
    t = xa.shape[0]
    tc = min(512, seq)
    nk = seq // tc
    gb = C_LRU_G // D_MODEL
    nblk = D_MODEL // LRU_BLOCK

    def chunk(b, k):
        return b * nk + (nk - 1 - k)

    def body(dya_ref, xa_ref, g_ref, h_ref, hp_ref, wa_ref, wx_ref, vec_ref,
             dxa_ref, dg_ref, dwa_ref, dwx_ref, dvec_ref, a_scr, dh_scr, c_scr):
        b, k = pl.program_id(0), pl.program_id(1)

        @pl.when((b == 0) & (k == 0))
        def _():
            dwa_ref[...] = jnp.zeros_like(dwa_ref)
            dwx_ref[...] = jnp.zeros_like(dwx_ref)
            dvec_ref[...] = jnp.zeros_like(dvec_ref)

        @pl.when(k == 0)
        def _():
            c_scr[...] = jnp.zeros_like(c_scr)

        xa_v = xa_ref[...]
        xaf = xa_v.astype(F32)
        v = vec_ref[...]
        sp = v[2:3, :]
        r, i, a, e = _lru_gates(xa_v, wa_ref, wx_ref, v[0:1, :], v[1:2, :], sp)
        gel, dgel = _gelu_and_grad(g_ref[...].astype(F32))
        hv = h_ref[...]
        dyv = dya_ref[...].astype(F32)
        dg_ref[...] = (dyv * hv * dgel).astype(BF16)
        a_scr[...] = a
        dh_scr[...] = dyv * gel

        row8 = lax.broadcasted_iota(jnp.int32, (8, 1), 0)

        def tile(j, c):
            r0 = pl.multiple_of((tc // 8 - 1 - j) * 8, 8)
            av, dout = a_scr[pl.ds(r0, 8), :], dh_scr[pl.ds(r0, 8), :]
            zv = av * dout
            for d in (1, 2, 4):
                zv = zv + av * jnp.where(row8 < 8 - d, pltpu.roll(zv, 8 - d, axis=0), 0.0)
                av = av * jnp.where(row8 < 8 - d, pltpu.roll(av, 8 - d, axis=0), 1.0)
            zv = zv + av * c
            dh_scr[pl.ds(r0, 8), :] = dout + jnp.where(row8 < 7, pltpu.roll(zv, 7, axis=0), c)
            return zv[0:1, :]

        c_scr[...] = lax.fori_loop(0, tc // 8, tile, c_scr[...], unroll=2)
        dh = dh_scr[...]
        h_last = jnp.where(k == nk - 1, 0.0, hp_ref[HALO // 2 - 1:HALO // 2, :])
        row = lax.broadcasted_iota(jnp.int32, (tc, 1), 0)
        h_prev = jnp.where(row == 0, h_last, pltpu.roll(hv, 1, axis=0))
        s = jnp.sqrt(e)
        da = dh * h_prev
        ix = i * xaf
        dlog_a = da * a - (dh * ix) * (a * a) * lax.rsqrt(jnp.maximum(e, 1e-30))
        di = dh * s * xaf
        dpr = (dlog_a * (-LRU_C * sp)) * (r * (1.0 - r))
        dpi = di * (i * (1.0 - i))
        dprb, dpib = dpr.astype(BF16), dpi.astype(BF16)
        dxa = dh * s * i
        dxa = dxa + jnp.concatenate(
            [_dot_nt(dprb[:, j * LRU_BLOCK:(j + 1) * LRU_BLOCK], wa_ref[j])
             + _dot_nt(dpib[:, j * LRU_BLOCK:(j + 1) * LRU_BLOCK], wx_ref[j]) for j in range(nblk)], axis=1)
        dxa_ref[...] = dxa.astype(BF16)
        for j in range(nblk):
            sl = slice(j * LRU_BLOCK, (j + 1) * LRU_BLOCK)
            dwa_ref[j] += _dot_tn(xa_v[:, sl], dprb[:, sl])
            dwx_ref[j] += _dot_tn(xa_v[:, sl], dpib[:, sl])
        dvec_ref[...] += jnp.concatenate(
            [_colsum(dpr), _colsum(dpi), _colsum(dlog_a * (-LRU_C * r)), jnp.zeros((5, D_MODEL), F32)], axis=0)

    hh = HALO // 2
    return _call(
        body, name="lru_bwd", grid=(nb, nk), side=side, sem=("arbitrary", "arbitrary"),
        args=(dya, xa, proj, h, h, wa_bd, wx_bd, vec),
        in_specs=[pl.BlockSpec((tc, D_MODEL), lambda b, k: (chunk(b, k), 0)),
                  pl.BlockSpec((tc, D_MODEL), lambda b, k: (chunk(b, k), 0)),
                  pl.BlockSpec((tc, D_MODEL), lambda b, k: (chunk(b, k), gb)),
                  pl.BlockSpec((tc, D_MODEL), lambda b, k: (chunk(b, k), 0)),
                  pl.BlockSpec((hh, D_MODEL), lambda b, k: (jnp.maximum(chunk(b, k) * (tc // hh) - 1, 0), 0)),
                  VMEM_FULL, VMEM_FULL, VMEM_FULL],
        out_specs=[pl.BlockSpec((tc, D_MODEL), lambda b, k: (chunk(b, k), 0)),
                   pl.BlockSpec((tc, D_MODEL), lambda b, k: (chunk(b, k), 0)),
                   pl.BlockSpec((nblk, LRU_BLOCK, LRU_BLOCK), lambda b, k: (0, 0, 0)),
                   pl.BlockSpec((nblk, LRU_BLOCK, LRU_BLOCK), lambda b, k: (0, 0, 0)),
                   pl.BlockSpec((8, D_MODEL), lambda b, k: (0, 0))],
        out_shape=[jax.ShapeDtypeStruct((t, D_MODEL), BF16), jax.ShapeDtypeStruct((t, D_MODEL), BF16),
                   jax.ShapeDtypeStruct((nblk, LRU_BLOCK, LRU_BLOCK), F32),
                   jax.ShapeDtypeStruct((nblk, LRU_BLOCK, LRU_BLOCK), F32),
                   jax.ShapeDtypeStruct((8, D_MODEL), F32)],
        scratch_shapes=[pltpu.VMEM((tc, D_MODEL), F32), pltpu.VMEM((tc, D_MODEL), F32), pltpu.VMEM((1, D_MODEL), F32)])


def merge_fwd(ya_in, yb_in, proj, x2, mod8, bgate, post1, w_pa, w_pb, w_out, seq):
    t = x2.shape[0]
    tm = min(512, seq)
    per_seq = seq // tm
    gcb = C_GATES // SSD_INNER

    def body(ya_ref, yb_ref, gt_ref, x_ref, mod_ref, bg_ref, post_ref, wpa_ref, wpb_ref, wo_ref,
             yab_ref, out1_ref, x1_ref):
        y_a = _dot(ya_ref[...], wpa_ref[...])
        y_b = _dot(yb_ref[...], wpb_ref[...])
        g = _sigmoid(gt_ref[...].astype(F32) + bg_ref[...])
        merged = g[:, :D_MODEL] * y_a + g[:, D_MODEL:] * y_b
        out1 = _dot(merged.astype(BF16), wo_ref[...])
        n = out1 * lax.rsqrt(_rowmean(out1 * out1) + EPS)
        yab_ref[...] = jnp.concatenate([y_a, y_b], axis=1).astype(BF16)
        out1_ref[...] = out1
        x1_ref[...] = x_ref[...] + mod_ref[0][2:3, :] * (n * post_ref[...])

    row = lambda w: pl.BlockSpec((tm, w), lambda i: (i, 0))
    return pl.pallas_call(
        body, name="merge_fwd", grid=(t // tm,),
        in_specs=[row(D_MODEL), row(SSD_INNER), pl.BlockSpec((tm, SSD_INNER), lambda i: (i, gcb)), row(D_MODEL),
                  pl.BlockSpec((1, 8, D_MODEL), lambda i: (i // per_seq, 0, 0)),
                  VMEM_FULL, VMEM_FULL, VMEM_FULL, VMEM_FULL, VMEM_FULL],
        out_specs=[row(SSD_INNER), row(D_MODEL), row(D_MODEL)],
        out_shape=[jax.ShapeDtypeStruct((t, SSD_INNER), BF16), jax.ShapeDtypeStruct((t, D_MODEL), F32),
                   jax.ShapeDtypeStruct((t, D_MODEL), F32)],
        compiler_params=_cp("parallel"),
    )(ya_in, yb_in, proj, x2, mod8, bgate, post1, w_pa, w_pb, w_out)


def merge_bwd(dx1, out1, yab, proj, mod8, bgate, post1, w_pa, w_pb, w_out, nb, seq):
    t = dx1.shape[0]
    tm = min(512, seq)
    per_seq = seq // tm
    gcb = C_GATES // SSD_INNER

    def body(dx1_ref, out1_ref, yab_ref, gt_ref, mod_ref, bg_ref, post_ref, wpa_ref, wpb_ref, wo_ref,
             dya_ref, dyb_ref, dgt_ref, dyab_ref, dout1_ref, mg_ref, vacc_ref, dmod_ref):
        b, s = pl.program_id(0), pl.program_id(1)

        @pl.when((b == 0) & (s == 0))
        def _():
            vacc_ref[...] = jnp.zeros_like(vacc_ref)

        @pl.when(s == 0)
        def _():
            dmod_ref[...] = jnp.zeros_like(dmod_ref)

        dx1v = dx1_ref[...]
        out1 = out1_ref[...]
        post = post_ref[...]
        rs = lax.rsqrt(_rowmean(out1 * out1) + EPS)
        n = out1 * rs
        do = dx1v * mod_ref[0][2:3, :]
        dn = do * post
        dout1 = rs * (dn - n * _rowmean(dn * n))
        dout1b = dout1.astype(BF16)
        dout1_ref[...] = dout1b
        dmerged = _dot_nt(dout1b, wo_ref[...])
        g = _sigmoid(gt_ref[...].astype(F32) + bg_ref[...])
        yab_v = yab_ref[...].astype(F32)
        gy = g * yab_v
        mg_ref[...] = (gy[:, :D_MODEL] + gy[:, D_MODEL:]).astype(BF16)
        dm2 = jnp.concatenate([dmerged, dmerged], axis=1)
        dyab = (dm2 * g).astype(BF16)
        dyab_ref[...] = dyab
        dgt = dm2 * gy * (1.0 - g)
        dgt_ref[...] = dgt.astype(BF16)
        dya_ref[...] = _dot_nt(dyab[:, :D_MODEL], wpa_ref[...]).astype(BF16)
        dyb_ref[...] = _dot_nt(dyab[:, D_MODEL:], wpb_ref[...]).astype(BF16)
        vacc_ref[...] += jnp.concatenate(
            [_colsum(dgt), jnp.concatenate([_colsum(do * n), jnp.zeros((1, D_MODEL), F32)], axis=1),
             jnp.zeros((6, SSD_INNER), F32)], axis=0)
        dmod_ref[0] += jnp.concatenate(
            [jnp.zeros((2, D_MODEL), F32), _colsum(dx1v * (n * post)), jnp.zeros((5, D_MODEL), F32)], axis=0)

    row = lambda w: pl.BlockSpec((tm, w), lambda b, s: (b * per_seq + s, 0))
    return pl.pallas_call(
        body, name="merge_bwd", grid=(nb, per_seq),
        in_specs=[row(D_MODEL), row(D_MODEL), row(SSD_INNER),
                  pl.BlockSpec((tm, SSD_INNER), lambda b, s: (b * per_seq + s, gcb)),
                  pl.BlockSpec((1, 8, D_MODEL), lambda b, s: (b, 0, 0)),
                  VMEM_FULL, VMEM_FULL, VMEM_FULL, VMEM_FULL, VMEM_FULL],
        out_specs=[row(D_MODEL), row(SSD_INNER), row(SSD_INNER), row(SSD_INNER), row(D_MODEL), row(D_MODEL),
                   pl.BlockSpec((8, SSD_INNER), lambda b, s: (0, 0)),
                   pl.BlockSpec((1, 8, D_MODEL), lambda b, s: (b, 0, 0))],
        out_shape=[jax.ShapeDtypeStruct((t, D_MODEL), BF16), jax.ShapeDtypeStruct((t, SSD_INNER), BF16),
                   jax.ShapeDtypeStruct((t, SSD_INNER), BF16), jax.ShapeDtypeStruct((t, SSD_INNER), BF16),
                   jax.ShapeDtypeStruct((t, D_MODEL), BF16), jax.ShapeDtypeStruct((t, D_MODEL), BF16),
                   jax.ShapeDtypeStruct((8, SSD_INNER), F32), jax.ShapeDtypeStruct((nb, 8, D_MODEL), F32)],
        compiler_params=_cp("arbitrary", "arbitrary"),
    )(dx1, out1, yab, proj, mod8, bgate, post1, w_pa, w_pb, w_out)


def mlp_fwd_bwd(x1, tgt, mod8, pre2, post2, w_ff1, w_ff2, nb, seq):
    t = x1.shape[0]
    tm = min(256, seq)
    per_seq = seq // tm
    fc = 1024
    nfc = D_FF // fc

    def body(x1_ref, tgt_ref, mod_ref, pre_ref, post_ref, w1_ref, w2_ref,
             dx1_ref, h2_ref, da1_ref, act_ref, dy2_ref, loss_ref, vacc_ref, dmod_ref, r_scr):
        b, s = pl.program_id(0), pl.program_id(1)
        per = fc // w1_ref.shape[2]

        def w1_cols(c):
            return jnp.concatenate([w1_ref[per * c + q] for q in range(per)], axis=1)

        @pl.when((b == 0) & (s == 0))
        def _():
            vacc_ref[...] = jnp.zeros_like(vacc_ref)
            loss_ref[...] = jnp.zeros_like(loss_ref)

        @pl.when(s == 0)
        def _():
            dmod_ref[...] = jnp.zeros_like(dmod_ref)

        m = mod_ref[0]
        sh2, sc2, g2 = m[3:4, :], m[4:5, :], m[5:6, :]
        pre, post = pre_ref[...], post_ref[...]
        x1v = x1_ref[...]
        rs1 = lax.rsqrt(_rowmean(x1v * x1v) + EPS)
        n1 = x1v * rs1
        y1 = n1 * pre
        h2b = (y1 * (1.0 + sc2) + sh2).astype(BF16)
        h2_ref[...] = h2b
        y2 = jnp.zeros((tm, D_MODEL), F32)
        for c in range(nfc):
            r = jnp.maximum(_dot(h2b, w1_cols(c)), 0.0)
            r_scr[:, c * fc:(c + 1) * fc] = r
            a = (r * r).astype(BF16)
            act_ref[:, c * fc:(c + 1) * fc] = a
            y2 = y2 + _dot(a, w2_ref[c * fc:(c + 1) * fc, :])
        rs2 = lax.rsqrt(_rowmean(y2 * y2) + EPS)
        n2 = y2 * rs2
        o2 = n2 * post
        diff = x1v + g2 * o2 - tgt_ref[...]
        loss_ref[...] += 0.5 * jnp.sum(_rowmean(diff * diff))
        dx2 = diff * (1.0 / D_MODEL)
        do2 = dx2 * g2
        dn2 = do2 * post
        dy2b = (rs2 * (dn2 - n2 * _rowmean(dn2 * n2))).astype(BF16)
        dy2_ref[...] = dy2b
        dh2 = jnp.zeros((tm, D_MODEL), F32)
        for c in range(nfc):
            dact = _dot_nt(dy2b, w2_ref[c * fc:(c + 1) * fc, :])
            da = (dact * (2.0 * r_scr[:, c * fc:(c + 1) * fc])).astype(BF16)
            da1_ref[:, c * fc:(c + 1) * fc] = da
            dh2 = dh2 + _dot_nt(da, w1_cols(c))
        dy1 = dh2 * (1.0 + sc2)
        dn1 = dy1 * pre
        dx1_ref[...] = dx2 + rs1 * (dn1 - n1 * _rowmean(dn1 * n1))
        vacc_ref[...] += jnp.concatenate([_colsum(dy1 * n1), _colsum(do2 * n2), jnp.zeros((6, D_MODEL), F32)], axis=0)
        dmod_ref[0] += jnp.concatenate(
            [jnp.zeros((3, D_MODEL), F32), _colsum(dh2), _colsum(dh2 * y1), _colsum(dx2 * o2),
             jnp.zeros((2, D_MODEL), F32)], axis=0)

    row = lambda w: pl.BlockSpec((tm, w), lambda b, s: (b * per_seq + s, 0))
    return pl.pallas_call(
        body, name="mlp_fwd_bwd", grid=(nb, per_seq),
        in_specs=[row(D_MODEL), row(D_MODEL), pl.BlockSpec((1, 8, D_MODEL), lambda b, s: (b, 0, 0)),
                  VMEM_FULL, VMEM_FULL, VMEM_FULL, VMEM_FULL],
        out_specs=[row(D_MODEL), row(D_MODEL), row(D_FF), row(D_FF), row(D_MODEL),
                   pl.BlockSpec((8, 128), lambda b, s: (0, 0)),
                   pl.BlockSpec((8, D_MODEL), lambda b, s: (0, 0)),
                   pl.BlockSpec((1, 8, D_MODEL), lambda b, s: (b, 0, 0))],
        out_shape=[jax.ShapeDtypeStruct((t, D_MODEL), F32), jax.ShapeDtypeStruct((t, D_MODEL), BF16),
                   jax.ShapeDtypeStruct((t, D_FF), BF16), jax.ShapeDtypeStruct((t, D_FF), BF16),
                   jax.ShapeDtypeStruct((t, D_MODEL), BF16), jax.ShapeDtypeStruct((8, 128), F32),
                   jax.ShapeDtypeStruct((8, D_MODEL), F32), jax.ShapeDtypeStruct((nb, 8, D_MODEL), F32)],
        scratch_shapes=[pltpu.VMEM((tm, D_FF), F32)],
        compiler_params=_cp("arbitrary", "arbitrary"),
    )(x1, tgt, mod8, pre2, post2, w_ff1, w_ff2)


_PIECES = ((C_LRU_X, 1024), (C_LRU_G, 1024), (C_Z, 2048), (C_XBC, 4096), (C_GATES, 2048))
_NP = len(_PIECES)


def in_proj_bwd(pieces, ddt, dx1, x2, mod8, pre1, w_main, w_dt, nb, seq, side=None):
    t = x2.shape[0]
    tm = min(512, seq)
    per_seq = seq // tm
    widths = [min(w, 2048) for _, w in _PIECES]
    steps = [(p, q) for p, (_, w) in enumerate(_PIECES) for q in range(w // widths[p])]
    nk = len(steps)

    def piece_spec(p):
        first = min(k for k in range(nk) if steps[k][0] == p)
        nblk = _PIECES[p][1] // widths[p]
        return pl.BlockSpec((tm, widths[p]), lambda b, s, k: (b * per_seq + s, jnp.clip(k - first, 0, nblk - 1)))

    def body(*refs):
        prefs = refs[:_NP]
        ddt_ref, dx1_ref, x_ref, mod_ref, pre_ref, w_ref, wdt_ref, gx_ref, vacc_ref, dmod_ref, acc_ref = refs[_NP:]
        b, s, k = pl.program_id(0), pl.program_id(1), pl.program_id(2)

        @pl.when((b == 0) & (s == 0) & (k == 0))
        def _():
            vacc_ref[...] = jnp.zeros_like(vacc_ref)

        @pl.when((s == 0) & (k == 0))
        def _():
            dmod_ref[...] = jnp.zeros_like(dmod_ref)

        @pl.when(k == 0)
        def _():
            acc_ref[...] = _dot_nt(ddt_ref[...], wdt_ref[...])

        for kk, (p, q) in enumerate(steps):
            @pl.when(k == kk)
            def _(p=p, q=q):
                c0 = _PIECES[p][0] + q * widths[p]
                acc_ref[...] += _dot_nt(prefs[p][...], w_ref[:, c0:c0 + widths[p]])

        @pl.when(k == nk - 1)
        def _():
            dh = acc_ref[...]
            m = mod_ref[0]
            pre = pre_ref[...]
            xv = x_ref[...]
            rs = lax.rsqrt(_rowmean(xv * xv) + EPS)
            n = xv * rs
            dy = dh * (1.0 + m[1:2, :])
            dn = dy * pre
            gx_ref[...] = dx1_ref[...] + rs * (dn - n * _rowmean(dn * n))
            vacc_ref[...] += jnp.concatenate([_colsum(dy * n), jnp.zeros((7, D_MODEL), F32)], axis=0)
            dmod_ref[0] += jnp.concatenate([_colsum(dh), _colsum(dh * (n * pre)), jnp.zeros((6, D_MODEL), F32)], axis=0)

    row = lambda w: pl.BlockSpec((tm, w), lambda b, s, k: (b * per_seq + s, 0))
    return _call(
        body, name="in_proj_bwd", grid=(nb, per_seq, nk), side=side, sem=("arbitrary", "arbitrary", "arbitrary"),
        args=(*pieces, ddt, dx1, x2, mod8, pre1, w_main, w_dt),
        in_specs=[piece_spec(p) for p in range(_NP)] + [
            row(128), row(D_MODEL), row(D_MODEL), pl.BlockSpec((1, 8, D_MODEL), lambda b, s, k: (b, 0, 0)),
            pl.BlockSpec((1, D_MODEL), lambda b, s, k: (0, 0)),
            VMEM_FULL,
            pl.BlockSpec((D_MODEL, 128), lambda b, s, k: (0, 0))],
        out_specs=[row(D_MODEL), pl.BlockSpec((8, D_MODEL), lambda b, s, k: (0, 0)),
                   pl.BlockSpec((1, 8, D_MODEL), lambda b, s, k: (b, 0, 0))],
        out_shape=[jax.ShapeDtypeStruct((t, D_MODEL), F32), jax.ShapeDtypeStruct((8, D_MODEL), F32),
                   jax.ShapeDtypeStruct((nb, 8, D_MODEL), F32)],
        scratch_shapes=[pltpu.VMEM((tm, D_MODEL), F32)])


def in_proj_wgrad(h1t, proj, dxa, dxbc, dsilu, dlg, dz, dgates, ddt, cw, seq, side=None):
    nt, _, tt = h1t.shape
    t = nt * tt
    tn = 1024
    nn = PROJ_MAIN // tn
    ns = seq // tt
    nh = t // HALO
    j_g, j_z, j_x, j_gt = C_LRU_G // tn, C_Z // tn, C_XBC // tn, C_GATES // tn
    n_x = 2 * SSD_INNER // tn
    strip = 256
    ne = tt + HALO

    def body(h_ref, cur_ref, prev_ref, next_ref, dxa_ref, dxan_ref, dxb_ref, dxbn_ref, ds_ref, dsn_ref, dlg_ref, dz_ref,
             dgt_ref, ddt_ref, cw_ref, dw_ref, dwdt_ref, dlx_ref, dxr_ref, accl_ref, accs_ref, acc_ref, accdt_ref):
        n, k = pl.program_id(0), pl.program_id(1)
        hv = h_ref[k]
        is_x = (n >= j_x) & (n < j_x + n_x)

        @pl.when(k == 0)
        def _():
            acc_ref[...] = jnp.zeros_like(acc_ref)

        @pl.when((n == 0) & (k == 0))
        def _():
            accdt_ref[...] = jnp.zeros_like(accdt_ref)
            accl_ref[...] = jnp.zeros_like(accl_ref)

        @pl.when(is_x & (k == 0))
        def _():
            accs_ref[...] = jnp.zeros_like(accs_ref)

        def conv_tile(do_ref, don_ref, out_ref, cacc_ref, act):
            first = lax.rem(k, ns) == 0
            last = lax.rem(k, ns) == ns - 1
            for c0 in range(0, tn, strip):
                cs = slice(c0, c0 + strip)
                xx = jnp.concatenate([jnp.where(first, 0.0, prev_ref[:, cs].astype(F32)), cur_ref[:, cs].astype(F32),
                                      next_ref[:, cs].astype(F32)], axis=0)
                do_ext = jnp.concatenate([do_ref[:, cs].astype(F32),
                                          jnp.where(last, 0.0, don_ref[:, cs].astype(F32))], axis=0)
                w = cw_ref[:, cs]
                xs = [xx[HALO:HALO + ne, :]] + [pltpu.roll(xx, d, axis=0)[HALO:HALO + ne, :] for d in (1, 2, 3)]
                if act:
                    dc = do_ext * jnp.concatenate([ds_ref[:, cs].astype(F32), dsn_ref[:, cs].astype(F32)], axis=0)
                else:
                    dc = do_ext
                dx = dc[:tt, :] * w[3:4, :]
                for d in (1, 2, 3):
                    dx = dx + pltpu.roll(dc, ne - d, axis=0)[:tt, :] * w[3 - d:4 - d, :]
                dxb = dx.astype(BF16)
                out_ref[:, cs] = dxb
                acc_ref[:, cs] += _dot(hv, dxb)
                dcc = dc[:tt, :]
                rows = [_colsum(dcc * xs[3 - r][:tt, :]) for r in range(4)] + [_colsum(dcc)]
                cacc_ref[:, cs] += jnp.concatenate(rows + [jnp.zeros((3, strip), F32)], axis=0)

        @pl.when(n == 0)
        def _():
            conv_tile(dxa_ref, dxan_ref, dlx_ref, accl_ref, False)
            accdt_ref[...] += _dot(hv, ddt_ref[...])

        @pl.when(is_x)
        def _():
            conv_tile(dxb_ref, dxbn_ref, dxr_ref, accs_ref, True)

        @pl.when(n == j_g)
        def _():
            acc_ref[...] += _dot(hv, dlg_ref[...])

        @pl.when((n >= j_z) & (n < j_x))
        def _():
            acc_ref[...] += _dot(hv, dz_ref[...])

        @pl.when(n >= j_gt)
        def _():
            acc_ref[...] += _dot(hv, dgt_ref[...])

        @pl.when(k == nt - 1)
        def _():
            dw_ref[...] = acc_ref[...].astype(BF16)

        @pl.when((n == 0) & (k == nt - 1))
        def _():
            dwdt_ref[...] = accdt_ref[...].astype(BF16)

    conv_n = lambda n: (n == 0) | ((n >= j_x) & (n < j_x + n_x))
    src_col = lambda n: jnp.where(n == 0, 0, jnp.clip(n, j_x, j_x + n_x - 1))
    ctile = lambda n: jnp.where(n == 0, 0, jnp.clip(n - j_x + 1, 1, n_x))
    xcol = lambda n: jnp.clip(n - j_x, 0, n_x - 1)
    on = lambda cond, k: jnp.where(cond, k, 0)
    nxt = lambda k: jnp.minimum(((k + 1) * tt) // HALO, nh - 1)
    after = lambda cond_during, cond_after, k: jnp.where(cond_during, k, jnp.where(cond_after, nt - 1, 0))
    in_specs = [
        VMEM_FULL,
        pl.BlockSpec((tt, tn), lambda n, k: (on(conv_n(n), k), src_col(n))),
        pl.BlockSpec((HALO, tn), lambda n, k: (on(conv_n(n), jnp.maximum((k * tt) // HALO - 1, 0)), src_col(n))),
        pl.BlockSpec((HALO, tn), lambda n, k: (on(conv_n(n), nxt(k)), src_col(n))),
        pl.BlockSpec((tt, tn), lambda n, k: (on(n == 0, k), 0)),
        pl.BlockSpec((HALO, tn), lambda n, k: (on(n == 0, nxt(k)), 0)),
        pl.BlockSpec((tt, tn), lambda n, k: (on((n >= j_x) & (n < j_x + n_x), k), xcol(n))),
        pl.BlockSpec((HALO, tn), lambda n, k: (on((n >= j_x) & (n < j_x + n_x), nxt(k)), xcol(n))),
        pl.BlockSpec((tt, tn), lambda n, k: (on((n >= j_x) & (n < j_x + n_x), k), xcol(n))),
        pl.BlockSpec((HALO, tn), lambda n, k: (on((n >= j_x) & (n < j_x + n_x), nxt(k)), xcol(n))),
        pl.BlockSpec((tt, tn), lambda n, k: (on(n == j_g, k), 0)),
        pl.BlockSpec((tt, tn), lambda n, k: (on((n >= j_z) & (n < j_x), k), jnp.clip(n - j_z, 0, j_x - j_z - 1))),
        pl.BlockSpec((tt, tn), lambda n, k: (on(n >= j_gt, k), jnp.clip(n - j_gt, 0, nn - j_gt - 1))),
        pl.BlockSpec((tt, 128), lambda n, k: (on(n == 0, k), 0)),
        pl.BlockSpec((4, tn), lambda n, k: (0, ctile(n)))]
    out_specs = [
        pl.BlockSpec((D_MODEL, tn), lambda n, k: (0, n)),
        pl.BlockSpec((D_MODEL, 128), lambda n, k: (0, 0)),
        pl.BlockSpec((tt, tn), lambda n, k: (after(n == 0, n > 0, k), 0)),
        pl.BlockSpec((tt, tn), lambda n, k: (after((n >= j_x) & (n < j_x + n_x), n >= j_x + n_x, k), xcol(n))),
        pl.BlockSpec((8, tn), lambda n, k: (0, 0)),
        pl.BlockSpec((8, tn), lambda n, k: (0, xcol(n)))]
    return _call(
        body, name="in_proj_wgrad", grid=(nn, nt), side=side, sem=("arbitrary", "arbitrary"),
        args=(h1t, proj, proj, proj, dxa, dxa, dxbc, dxbc, dsilu, dsilu, dlg, dz, dgates, ddt, cw),
        in_specs=in_specs, out_specs=out_specs,
        out_shape=[jax.ShapeDtypeStruct((D_MODEL, PROJ_MAIN), BF16), jax.ShapeDtypeStruct((D_MODEL, 128), BF16),
                   jax.ShapeDtypeStruct((t, D_MODEL), BF16), jax.ShapeDtypeStruct((t, 2 * SSD_INNER), BF16),
                   jax.ShapeDtypeStruct((8, D_MODEL), F32), jax.ShapeDtypeStruct((8, 2 * SSD_INNER), F32)],
        scratch_shapes=[pltpu.VMEM((D_MODEL, tn), F32), pltpu.VMEM((D_MODEL, 128), F32)])


def _log1p(u):
    w = 1.0 + u
    return jnp.log(w) - ((w - 1.0) - u) / w


def _softplus(x):
    return jnp.maximum(x, 0.0) + _log1p(jnp.exp(-jnp.abs(x)))


def _head_mask(h):
    lane = lax.broadcasted_iota(jnp.int32, (1, SSD_GW), 1)
    return (lane >= SSD_P * h) & (lane < SSD_P * (h + 1))


def _pair(p):
    return slice(2 * SSD_P * p, 2 * SSD_P * (p + 1))


def _expand4(m, g):
    lane = lax.broadcasted_iota(jnp.int32, (1, SSD_GW), 1)
    col = lambda h: m[:, 4 * g + h:4 * g + h + 1]
    return jnp.where(lane < SSD_P, col(0), jnp.where(lane < 2 * SSD_P, col(1), jnp.where(lane < 3 * SSD_P, col(2), col(3))))


def _reduce4(v, g):
    lane = lax.broadcasted_iota(jnp.int32, (1, SSD_N), 1)
    out = jnp.zeros((v.shape[0], SSD_N), F32)
    for h in range(4):
        s = jnp.sum(jnp.where(_head_mask(h), v, 0.0), axis=1, keepdims=True)
        out = out + jnp.where(lane == 4 * g + h, s, 0.0)
    return out


def _ssd_heads(dtraw, hp, tri):
    xdt = dtraw + hp[0:1, :]
    dt = _softplus(xdt)
    cs = _dot_hi(tri, dt * hp[1:2, :])
    cs_last = cs[SSD_L - 1:SSD_L, :]
    return dict(xdt=xdt, dt=dt, cs=cs, cs_t=cs.T, e=jnp.exp(cs), w=jnp.exp(cs_last - cs), el=jnp.exp(cs_last))


def _ssd_group(g, hd, xs_b, bm_b, cm_b, d_x, st, paired=False):
    ll = SSD_L
    xs = xs_b.astype(F32)
    cs, cs_t = hd["cs"], hd["cs_t"]
    e_x, w_x, el_x, dt_x = _expand4(hd["e"], g), _expand4(hd["w"], g), _expand4(hd["el"], g), _expand4(hd["dt"], g)
    xd = xs * dt_x
    gcb = _dot_nt(cm_b, bm_b)
    ri = lax.broadcasted_iota(jnp.int32, (ll, ll), 0)
    ci = lax.broadcasted_iota(jnp.int32, (ll, ll), 1)
    dks, ms = [], []
    for h in range(4):
        k = 4 * g + h
        dk = jnp.exp(jnp.where(ri >= ci, cs[:, k:k + 1] - cs_t[k:k + 1, :], -1e30))
        dks.append(dk)
        ms.append((gcb * dk).astype(BF16))
    xdb = xd.astype(BF16)
    if paired:
        first = lax.broadcasted_iota(jnp.int32, (1, 2 * SSD_P), 1) < SSD_P
        ydiag = jnp.concatenate(
            [jnp.where(first, _dot(ms[2 * p], xdb[:, _pair(p)]), _dot(ms[2 * p + 1], xdb[:, _pair(p)]))
             for p in range(2)], axis=1)
    else:
        ydiag = jnp.zeros((ll, SSD_GW), F32)
        for h in range(4):
            ydiag = ydiag + _dot(ms[h], jnp.where(_head_mask(h), xd, 0.0).astype(BF16))
    yoff = _dot(cm_b, st.astype(BF16)) * e_x
    y = ydiag + yoff + d_x * xs
    st_new = st * el_x + _dot(bm_b.astype(F32).T.astype(BF16), (xd * w_x).astype(BF16))
    return dict(xs=xs, e_x=e_x, w_x=w_x, el_x=el_x, dt_x=dt_x, xd=xd, xdb=xdb, gcb=gcb, dks=dks, ms=ms, yoff=yoff, y=y,
                st_new=st_new)


def ssd_consts():
    hh = np.arange(SSD_N)
    tri = (hh[:, None] >= hh[None, :]).astype(np.float32)
    return jnp.asarray(tri), jnp.asarray(tri.T)


def ssd_params(dt_bias, a_log, d_skip, norm_w):
    padh = lambda v: jnp.pad(v.reshape(1, SSD_HEADS), ((0, 0), (0, SSD_N - SSD_HEADS)))
    hp = jnp.concatenate([padh(dt_bias), padh(-jnp.exp(a_log)), jnp.zeros((6, SSD_N), F32)], axis=0)
    lp = jnp.concatenate([norm_w.reshape(1, SSD_INNER), jnp.repeat(d_skip, SSD_P).reshape(1, SSD_INNER),
                          jnp.zeros((6, SSD_INNER), F32)], axis=0)
    return hp, lp


def _b_cols(g):
    return slice(SSD_INNER + g * SSD_N, SSD_INNER + (g + 1) * SSD_N)


def _c_cols(g):
    return slice(SSD_INNER + (SSD_G + g) * SSD_N, SSD_INNER + (SSD_G + g + 1) * SSD_N)


def _ssd_specs(nc, rc, cps=1):
    rows = cps * SSD_L
    return [pl.BlockSpec((rows, 2 * SSD_INNER), lambda b, c: (b * nc + rc(c), 0)),
            pl.BlockSpec((rows, SSD_INNER), lambda b, c: (b * nc + rc(c), C_Z // SSD_INNER)),
            pl.BlockSpec((rows, SSD_N), lambda b, c: (b * nc + rc(c), 0))]


def ssd_fwd(xbc, proj, dtraw, hp, lp, tri, nb, seq):
    t = xbc.shape[0]
    cps = SSD_FWD_CPS
    nc = seq // (cps * SSD_L)

    def body(xbc_ref, z_ref, dt_ref, hp_ref, lp_ref, tri_ref, y_ref, sts_ref, st_scr):
        @pl.when(pl.program_id(1) == 0)
        def _():
            st_scr[...] = jnp.zeros_like(st_scr)

        for cc in range(cps):
            rs = slice(cc * SSD_L, (cc + 1) * SSD_L)
            hd = _ssd_heads(dt_ref[rs, :], hp_ref[...], tri_ref[...])
            for g in range(SSD_G):
                gs = slice(g * SSD_GW, (g + 1) * SSD_GW)
                st = st_scr[g]
                sts_ref[cc, g] = st
                f = _ssd_group(g, hd, xbc_ref[rs, gs], xbc_ref[rs, _b_cols(g)], xbc_ref[rs, _c_cols(g)],
                               lp_ref[1:2, gs], st, paired=True)
                st_scr[g] = f["st_new"]
                zf = z_ref[rs, gs].astype(F32)
                yg = f["y"] * (zf * _sigmoid(zf))
                y_ref[rs, gs] = (yg * lax.rsqrt(_rowmean(yg * yg) + EPS) * lp_ref[0:1, gs]).astype(BF16)

    return pl.pallas_call(
        body, name="ssd_fwd", grid=(nb, nc),
        in_specs=_ssd_specs(nc, lambda c: c, cps) + [VMEM_FULL, VMEM_FULL, VMEM_FULL],
        out_specs=[pl.BlockSpec((cps * SSD_L, SSD_INNER), lambda b, c: (b * nc + c, 0)),
                   pl.BlockSpec((cps, SSD_G, SSD_N, SSD_GW), lambda b, c: (b * nc + c, 0, 0, 0))],
        out_shape=[jax.ShapeDtypeStruct((t, SSD_INNER), BF16),
                   jax.ShapeDtypeStruct((nb * nc * cps, SSD_G, SSD_N, SSD_GW), F32)],
        scratch_shapes=[pltpu.VMEM((SSD_G, SSD_N, SSD_GW), F32)],
        compiler_params=_cp("arbitrary", "arbitrary"),
    )(xbc, proj, dtraw, hp, lp, tri)


def ssd_bwd(xbc, proj, dtraw, hp, lp, tri, triu, states, dyn, nb, seq, side=None):
    t = xbc.shape[0]
    nc = seq // SSD_L
    ll = SSD_L

    def body(xbc_ref, z_ref, dt_ref, sts_ref, dy_ref, hp_ref, lp_ref, tri_ref, triu_ref,
             dxbc_ref, dz_ref, ddt_ref, hpg_ref, lpg_ref, dst_scr):
        b, c_i = pl.program_id(0), pl.program_id(1)

        @pl.when((b == 0) & (c_i == 0))
        def _():
            hpg_ref[...] = jnp.zeros_like(hpg_ref)
            lpg_ref[...] = jnp.zeros_like(lpg_ref)

        @pl.when(c_i == 0)
        def _():
            dst_scr[...] = jnp.zeros_like(dst_scr)

        hp = hp_ref[...]
        hd = _ssd_heads(dt_ref[...], hp, tri_ref[...])
        lane = lax.broadcasted_iota(jnp.int32, (1, SSD_N), 1)
        subl = lax.broadcasted_iota(jnp.int32, (SSD_N, 1), 0)
        dcs = jnp.zeros((ll, SSD_N), F32)
        dcs_t = jnp.zeros((SSD_N, ll), F32)
        last = jnp.zeros((1, SSD_N), F32)
        dxx = jnp.zeros((ll, SSD_N), F32)
        for g in range(SSD_G):
            gs = slice(g * SSD_GW, (g + 1) * SSD_GW)
            st = sts_ref[0, g]
            dst = dst_scr[g]
            bm_b, cm_b = xbc_ref[:, _b_cols(g)], xbc_ref[:, _c_cols(g)]
            d_x = lp_ref[1:2, gs]
            f = _ssd_group(g, hd, xbc_ref[:, gs], bm_b, cm_b, d_x, st)
            xs, xd, gcb = f["xs"], f["xd"], f["gcb"]
            e_x, w_x, el_x, dt_x = f["e_x"], f["w_x"], f["el_x"], f["dt_x"]
            stb, dstb = st.astype(BF16), dst.astype(BF16)
            zf = z_ref[:, gs].astype(F32)
            sg = _sigmoid(zf)
            sz = zf * sg
            yv = f["y"]
            yg = yv * sz
            rstd = lax.rsqrt(_rowmean(yg * yg) + EPS)
            n = yg * rstd
            dyn_v = dy_ref[:, gs].astype(F32)
            dn = dyn_v * lp_ref[0:1, gs]
            dyg = rstd * (dn - n * _rowmean(dn * n))
            dy = dyg * sz
            dz_ref[:, gs] = (dyg * yv * (sg * (1.0 + zf * (1.0 - sg)))).astype(BF16)
            dyb = dy.astype(BF16)
            r_ = _dot(bm_b, dstb)
            dxd = w_x * r_
            dqb = (dy * e_x).astype(BF16)
            dcm = _dot_nt(dqb, stb)
            dst_scr[g] = dst * el_x + _dot_tn(cm_b, dqb)
            dbm = _dot_nt((xd * w_x).astype(BF16), dstb)
            xdb = f["xdb"]
            dgm = jnp.zeros((ll, ll), F32)
            for h in range(4):
                k = 4 * g + h
                hm = _head_mask(h)
                dxd = dxd + jnp.where(hm, _dot_tn(f["ms"][h], dyb), 0.0)
                dm = _dot_nt(jnp.where(hm, dy, 0.0).astype(BF16), xdb) * f["dks"][h]
                dgm = dgm + dm
                dseg = dm * gcb
                dcs = dcs + jnp.where(lane == k, jnp.sum(dseg, axis=1, keepdims=True), 0.0)
                dcs_t = dcs_t + jnp.where(subl == k, jnp.sum(dseg, axis=0, keepdims=True), 0.0)
            dgmb = dgm.astype(BF16)
            dxbc_ref[:, _c_cols(g)] = (dcm + _dot(dgmb, bm_b)).astype(BF16)
            dxbc_ref[:, _b_cols(g)] = (dbm + _dot_tn(dgmb, cm_b)).astype(BF16)
            v = _reduce4(r_ * xd * w_x, g)
            dcs = dcs + _reduce4(dy * f["yoff"], g) - v
            last = last + _colsum(v) + _reduce4(_colsum(dst * st) * el_x, g)
            dxx = dxx + _reduce4(dxd * xs, g)
            dxbc_ref[:, gs] = (d_x * dy + dxd * dt_x).astype(BF16)
            lpg_ref[0:1, gs] += _colsum(dyn_v * n)
            lpg_ref[1:2, gs] += _colsum(dy * xs)
        rowi = lax.broadcasted_iota(jnp.int32, (ll, 1), 0)
        da = _dot_hi(triu_ref[...], dcs - dcs_t.T + jnp.where(rowi == ll - 1, last, 0.0))
        ddt = (dxx + da * hp[1:2, :]) * _sigmoid(hd["xdt"])
        ddt_ref[...] = ddt
        hpg_ref[...] += jnp.concatenate([_colsum(ddt), _colsum(da * hd["dt"]), jnp.zeros((6, SSD_N), F32)], axis=0)

    rc = lambda c: nc - 1 - c
    return _call(
        body, name="ssd_bwd", grid=(nb, nc), side=side, sem=("arbitrary", "arbitrary"),
        args=(xbc, proj, dtraw, states, dyn, hp, lp, tri, triu),
        in_specs=_ssd_specs(nc, rc) + [
            pl.BlockSpec((1, SSD_G, SSD_N, SSD_GW), lambda b, c: (b * nc + rc(c), 0, 0, 0)),
            pl.BlockSpec((SSD_L, SSD_INNER), lambda b, c: (b * nc + rc(c), 0)),
            VMEM_FULL, VMEM_FULL, VMEM_FULL, VMEM_FULL],
        out_specs=[pl.BlockSpec((SSD_L, 2 * SSD_INNER), lambda b, c: (b * nc + rc(c), 0)),
                   pl.BlockSpec((SSD_L, SSD_INNER), lambda b, c: (b * nc + rc(c), 0)),
                   pl.BlockSpec((SSD_L, SSD_N), lambda b, c: (b * nc + rc(c), 0)),
                   pl.BlockSpec((8, SSD_N), lambda b, c: (0, 0)),
                   pl.BlockSpec((8, SSD_INNER), lambda b, c: (0, 0))],
        out_shape=[jax.ShapeDtypeStruct((t, 2 * SSD_INNER), BF16), jax.ShapeDtypeStruct((t, SSD_INNER), BF16),
                   jax.ShapeDtypeStruct((t, SSD_N), F32), jax.ShapeDtypeStruct((8, SSD_N), F32),
                   jax.ShapeDtypeStruct((8, SSD_INNER), F32)],
        scratch_shapes=[pltpu.VMEM((SSD_G, SSD_N, SSD_GW), F32)])


def ada_fwd(c_all, w_cols, b_cols):
    def body(c_ref, w_ref, b_ref, o_ref):
        cv = c_ref[...]
        o_ref[...] = _dot_hi(cv * _sigmoid(cv), w_ref[...]) + b_ref[...]

    return pl.pallas_call(body, name="ada_fwd", out_shape=jax.ShapeDtypeStruct((c_all.shape[0], w_cols.shape[1]), F32),
                          compiler_params=pltpu.CompilerParams(vmem_limit_bytes=VMEM_LIMIT))(c_all, w_cols, b_cols)


def ada_bwd(c_all, dmod_cols, dmod_all):
    def body(c_ref, dc_ref, da_ref, gw_ref, gb_ref):
        cv = c_ref[...]
        gw_ref[...] = lax.dot_general(cv * _sigmoid(cv), dc_ref[...], (((0,), (0,)), ((), ())),
                                      precision=lax.Precision.HIGHEST, preferred_element_type=F32)
        gb_ref[...] = _colsum(da_ref[...])

    return pl.pallas_call(
        body, name="ada_bwd",
        out_shape=[jax.ShapeDtypeStruct((c_all.shape[1], dmod_cols.shape[1]), F32),
                   jax.ShapeDtypeStruct((1, dmod_all.shape[1]), F32)],
        compiler_params=pltpu.CompilerParams(vmem_limit_bytes=VMEM_LIMIT))(c_all, dmod_cols, dmod_all)


def _adam_update(g, w, m, v):
    m2 = ADAM_B1 * m + (1.0 - ADAM_B1) * g
    v2 = ADAM_B2 * v + (1.0 - ADAM_B2) * (g * g)
    m_hat = m2 / (1.0 - ADAM_B1 ** ADAM_STEP)
    v_hat = v2 / (1.0 - ADAM_B2 ** ADAM_STEP)
    return -ADAM_LR * (m_hat / (jnp.sqrt(v_hat) + ADAM_EPS) + ADAM_WD * w), m2, v2


def adamw(parts, w, m, v, name):
    n, r, c = parts.shape
    tr = r if r <= 256 else 128

    def body(p_ref, w_ref, m_ref, v_ref, g_ref, d_ref, nm_ref, nv_ref):
        g = p_ref[0].astype(F32)
        for s in range(1, n):
            g = g + p_ref[s].astype(F32)
        g_ref[0] = g
        d_ref[0], nm_ref[0], nv_ref[0] = _adam_update(g, w_ref[0], m_ref[0], v_ref[0])

    blk = pl.BlockSpec((1, tr, c), lambda i: (0, i, 0))
    return pl.pallas_call(
        body, name=name, grid=(r // tr,),
        in_specs=[pl.BlockSpec((n, tr, c), lambda i: (0, i, 0)), blk, blk, blk], out_specs=[blk] * 4,
        out_shape=[jax.ShapeDtypeStruct((1, r, c), F32)] * 4,
        compiler_params=_cp("parallel"),
    )(parts, w, m, v)


SMALL_SRC = {
    'pre_norm1': ('vin', 0, 1024), 'post_norm1': ('vmg', 1, 1024), 'b_gate': ('vmg', 0, 2048),
    'lru_conv_b': ('accl', 4, 1024), 'lru_wa': ('gwa', None, None), 'lru_ba': ('dvec', 0, 1024),
    'lru_wx': ('gwx', None, None), 'lru_bx': ('dvec', 1, 1024), 'lru_lambda': ('dvec', 2, 1024),
    'ssd_conv_b': ('accs', 4, 4096), 'ssd_dt_bias': ('hpg', 0, SSD_HEADS), 'ssd_a_log': ('hpg', 1, SSD_HEADS),
    'ssd_d': ('lpg', 1, SSD_INNER), 'ssd_norm_w': ('lpg', 0, SSD_INNER), 'pre_norm2': ('vmlp', 0, 1024),
    'post_norm2': ('vmlp', 1, 1024)}
SMALL_ACCS = ('vin', 'vmg', 'vmlp', 'dvec', 'accl', 'accs', 'hpg', 'lpg', 'gwa', 'gwx')
SMALL_RIDE = ('vmg', 'vmlp', 'dvec', 'hpg', 'lpg', 'gwa', 'gwx')


def adamw_small(gathered, params):
    names = tuple(params)
    na = len(SMALL_ACCS)

    def body(*refs):
        acc = {k: functools.reduce(lambda p, q: p + q, [refs[i][s] for s in range(NDEV)])
               for i, k in enumerate(SMALL_ACCS)}
        ins = refs[na:na + 3 * len(names)]
        outs = refs[na + 3 * len(names):]
        for j, k in enumerate(names):
            w_ref, m_ref, v_ref = ins[3 * j:3 * j + 3]
            src, row, width = SMALL_SRC[k]
            wv = w_ref[...]
            if row is None:
                g = acc[src]
            elif k == 'ssd_d':
                li = lax.broadcasted_iota(jnp.int32, (SSD_INNER, SSD_N), 0)
                hi = lax.broadcasted_iota(jnp.int32, (SSD_INNER, SSD_N), 1)
                g = _dot_hi(acc[src], jnp.where(jnp.right_shift(li, 6) == hi, 1.0, 0.0))[row:row + 1, :SSD_HEADS]
            else:
                g = acc[src][row:row + 1, :width]
            if k == 'lru_lambda':
                g = g * (-1.0 / (1.0 + jnp.exp(wv)))
            if k == 'ssd_a_log':
                g = g * (-jnp.exp(wv))
            o = outs[4 * j:4 * j + 4]
            o[0][...] = g
            o[1][...], o[2][...], o[3][...] = _adam_update(g, wv, m_ref[...], v_ref[...])
        outs[-2][...] = acc['accl'][0:4, :]
        outs[-1][...] = acc['accs'][0:4, :]

    flat = [a for k in names for a in params[k]]
    out_shape = [jax.ShapeDtypeStruct(params[k][0].shape, F32) for k in names for _ in range(4)]
    out_shape += [jax.ShapeDtypeStruct((4, D_MODEL), F32), jax.ShapeDtypeStruct((4, 2 * SSD_INNER), F32)]
    res = pl.pallas_call(body, name="adamw_small", out_shape=out_shape,
                         compiler_params=pltpu.CompilerParams(vmem_limit_bytes=VMEM_LIMIT))(
        *[gathered[k] for k in SMALL_ACCS], *flat)
    return {k: res[4 * j:4 * j + 4] for j, k in enumerate(names)}, res[-2], res[-1]


def _dev_index(px, py, pc):
    return 4 * px + 2 * py + pc


class _Exchange:
    def __init__(self, arrs):
        self.arrs = list(arrs)
        self.na = len(self.arrs)
        self.scratch = [pltpu.SemaphoreType.DMA((7 * self.na,)), pltpu.SemaphoreType.DMA((7 * self.na,)),
                        pltpu.SemaphoreType.DMA((self.na,))]


class Gather(_Exchange):
    def __init__(self, arrs):
        super().__init__(arrs)
        self.out_shape = [jax.ShapeDtypeStruct((NDEV,) + a.shape, a.dtype) for a in self.arrs]

    def _plan(self, ins, outs, sems):
        na = self.na
        send_sems, recv_sems, local_sems = sems
        x, y, c = lax.axis_index("x"), lax.axis_index("y"), lax.axis_index("c")
        me, sibling = (x, y, c), (x, y, 1 - c)
        chips = [(1 - x, y), (x, 1 - y), (1 - x, 1 - y)]

        def copy(a, k, block, to, src=None):
            dst = outs[a].at[_dev_index(*block)]
            return pltpu.make_async_remote_copy(
                src_ref=dst if src is None else src, dst_ref=dst, send_sem=send_sems.at[a * 7 + k],
                recv_sem=recv_sems.at[a * 7 + k], device_id=to, device_id_type=MESH)

        mine = [pltpu.make_async_copy(ins[a], outs[a].at[_dev_index(*me)], local_sems.at[a]) for a in range(na)]
        first = []
        for a in range(na):
            first.append(copy(a, 0, me, sibling, src=ins[a]))
            first += [copy(a, 1 + j, me, (*chip, c), src=ins[a]) for j, chip in enumerate(chips)]
        return copy, mine, first, me, sibling, chips, c

    def start(self, ins, outs, sems):
        _, mine, first, *_ = self._plan(ins, outs, sems)
        for cp in mine + first:
            cp.start()

    def finish(self, ins, outs, sems):
        copy, mine, first, me, sibling, chips, c = self._plan(ins, outs, sems)
        passed = []
        for j, chip in enumerate(chips):
            for a in range(self.na):
                copy(a, 1 + j, (*chip, c), me).wait_recv()
                cp = copy(a, 4 + j, (*chip, c), sibling)
                cp.start()
                passed.append(cp)
        for a in range(self.na):
            copy(a, 0, sibling, me).wait_recv()
            for j, chip in enumerate(chips):
                copy(a, 4 + j, (*chip, 1 - c), me).wait_recv()
        for cp in first + passed:
            cp.wait_send()
        for cp in mine:
            cp.wait()


class GatherRelay(Gather):
    def _plan(self, ins, outs, sems):
        na = self.na
        send_sems, recv_sems, local_sems = sems
        x, y, c = lax.axis_index("x"), lax.axis_index("y"), lax.axis_index("c")
        me, sibling = (x, y, c), (x, y, 1 - c)
        xn, yn, dg = (1 - x, y), (x, 1 - y), (1 - x, 1 - y)
        south = c == 0
        pick = lambda a, b: tuple(jnp.where(south, p, q) for p, q in zip(a, b))
        relay_to = pick(yn, xn)
        relay_of = pick(xn, yn)

        def copy(a, k, block, to, src=None):
            dst = outs[a].at[_dev_index(*block)]
            return pltpu.make_async_remote_copy(
                src_ref=dst if src is None else src, dst_ref=dst, send_sem=send_sems.at[a * 7 + k],
                recv_sem=recv_sems.at[a * 7 + k], device_id=to, device_id_type=MESH)

        mine = [pltpu.make_async_copy(ins[a], outs[a].at[_dev_index(*me)], local_sems.at[a]) for a in range(na)]
        first = []
        for a in range(na):
            first += [copy(a, 0, me, sibling, src=ins[a]), copy(a, 1, me, (*xn, c), src=ins[a]),
                      copy(a, 2, me, (*yn, c), src=ins[a])]
        return copy, mine, first, me, sibling, (xn, yn, dg), c, relay_to, relay_of

    def start(self, ins, outs, sems):
        _, mine, first, *_ = self._plan(ins, outs, sems)
        for cp in mine + first:
            cp.start()

    def finish(self, ins, outs, sems):
        copy, mine, first, me, sibling, (xn, yn, dg), c, relay_to, relay_of = self._plan(ins, outs, sems)
        later = []
        for a in range(self.na):
            copy(a, 1, (*xn, c), me).wait_recv()
            copy(a, 2, (*yn, c), me).wait_recv()
            later.append(copy(a, 3, (*relay_of, c), (*relay_to, c)))
            later += [copy(a, 4, (*xn, c), sibling), copy(a, 5, (*yn, c), sibling)]
            for cp in later[-3:]:
                cp.start()
        for a in range(self.na):
            copy(a, 3, (*dg, c), me).wait_recv()
            cp = copy(a, 6, (*dg, c), sibling)
            cp.start()
            later.append(cp)
        for a in range(self.na):
            copy(a, 0, sibling, me).wait_recv()
            for k, chip in ((4, xn), (5, yn), (6, dg)):
                copy(a, k, (*chip, 1 - c), me).wait_recv()
        for cp in first + later:
            cp.wait_send()
        for cp in mine:
            cp.wait()


class Scatter(_Exchange):
    def __init__(self, arrs):
        super().__init__(arrs)
        self.out_shape = [jax.ShapeDtypeStruct(a.shape, a.dtype) for a in self.arrs]

    def _plan(self, ins, outs, sems, arrivals):
        send_sems, recv_sems, local_sems = sems
        x, y, c = lax.axis_index("x"), lax.axis_index("y"), lax.axis_index("c")
        me = _dev_index(x, y, c)
        masks = [(mx, my, mc) for mx in (0, 1) for my in (0, 1) for mc in (0, 1)][1:]
        flip = lambda v, bit: 1 - v if bit else v
        mine = [pltpu.make_async_copy(ins[a].at[me], outs[a].at[me], local_sems.at[a]) for a in range(self.na)]
        sends, recvs = [], []
        for k, (mx, my, mc) in enumerate(masks):
            peer = (flip(x, mx), flip(y, my), flip(c, mc))
            pidx = _dev_index(*peer)
            for a in range(self.na):
                on = dict(send_sem=send_sems.at[a * 7 + k], recv_sem=recv_sems.at[a * 7 + k], device_id=peer,
                          device_id_type=MESH)
                sends.append(pltpu.make_async_remote_copy(src_ref=ins[a].at[pidx], dst_ref=outs[a].at[me], **on))
                if arrivals:
                    recvs.append(pltpu.make_async_remote_copy(src_ref=ins[a].at[pidx], dst_ref=outs[a].at[pidx], **on))
        return mine, sends, recvs

    def start(self, ins, outs, sems):
        mine, sends, _ = self._plan(ins, outs, sems, arrivals=False)
        for cp in mine + sends:
            cp.start()

    def finish(self, ins, outs, sems):
        mine, sends, recvs = self._plan(ins, outs, sems, arrivals=True)
        for cp in recvs:
            cp.wait_recv()
        for cp in sends:
            cp.wait_send()
        for cp in mine:
            cp.wait()


def exchange_call(ex, name):
    na = ex.na

    def body(*refs):
        ins, outs, sems = refs[:na], refs[na:2 * na], refs[2 * na:]
        ex.start(ins, outs, sems)
        ex.finish(ins, outs, sems)

    return pl.pallas_call(body, name=name, in_specs=[ANY] * na, out_specs=[ANY] * na, out_shape=ex.out_shape,
                          scratch_shapes=ex.scratch)(*ex.arrs)


def all_gather(arrs, name, relay=False):
    return exchange_call((GatherRelay if relay else Gather)(arrs), name)


def _call(body, *, name, grid, in_specs, out_specs, out_shape, scratch_shapes=(), sem, args, side=None):
    if side is None:
        outs = pl.pallas_call(body, name=name, grid=grid, in_specs=list(in_specs), out_specs=list(out_specs),
                              out_shape=list(out_shape), scratch_shapes=list(scratch_shapes),
                              compiler_params=_cp(*sem))(*args)
        return outs, []
    ni, no, ns, na = len(in_specs), len(out_specs), len(scratch_shapes), side.na

    def wrapped(*refs):
        ins, s_in = refs[:ni], refs[ni:ni + na]
        outs, s_out = refs[ni + na:ni + na + no], refs[ni + na + no:ni + 2 * na + no]
        scr, sems = refs[ni + 2 * na + no:ni + 2 * na + no + ns], refs[ni + 2 * na + no + ns:]
        pids = [pl.program_id(i) for i in range(len(grid))]
        first = functools.reduce(lambda p, q: p & q, [p == 0 for p in pids])
        last = functools.reduce(lambda p, q: p & q, [p == g - 1 for p, g in zip(pids, grid)])

        @pl.when(first)
        def _():
            side.start(s_in, s_out, sems)

        body(*ins, *outs, *scr)

        @pl.when(last)
        def _():
            side.finish(s_in, s_out, sems)

    outs = pl.pallas_call(
        wrapped, name=name, grid=grid, in_specs=list(in_specs) + [ANY] * na, out_specs=list(out_specs) + [ANY] * na,
        out_shape=list(out_shape) + side.out_shape, scratch_shapes=list(scratch_shapes) + side.scratch,
        compiler_params=_cp(*["arbitrary"] * len(grid)))(*args, *side.arrs)
    return outs[:no], outs[no:]


WEIGHTS = ('w_ada', 'b_ada', 'pre_norm1', 'post_norm1', 'w_in', 'b_gate', 'lru_conv_w', 'lru_conv_b', 'lru_wa',
           'lru_ba', 'lru_wx', 'lru_bx', 'lru_lambda', 'w_pa', 'ssd_conv_w', 'ssd_conv_b', 'ssd_dt_bias', 'ssd_a_log',
           'ssd_d', 'ssd_norm_w', 'w_pb', 'w_out', 'pre_norm2', 'post_norm2', 'w_ff1', 'w_ff2')
BIG = ('w_in', 'w_pa', 'w_pb', 'w_out', 'w_ff1', 'w_ff2')
REPL = ('pre_norm1', 'post_norm1', 'b_gate', 'lru_conv_b', 'lru_wa', 'lru_ba', 'lru_wx', 'lru_bx', 'lru_lambda',
        'ssd_conv_b', 'ssd_dt_bias', 'ssd_a_log', 'ssd_d', 'ssd_norm_w', 'pre_norm2', 'post_norm2')
LANES = 1024


def _rows(n):
    return -(-n // LANES)


def _pack(vals, total_rows):
    parts = []
    for v in vals:
        f = v.reshape(-1).astype(F32)
        parts.append(jnp.pad(f, (0, _rows(f.shape[0]) * LANES - f.shape[0])))
    flat = jnp.concatenate(parts)
    return jnp.pad(flat.reshape(-1, LANES), ((0, total_rows - flat.shape[0] // LANES), (0, 0)))


def _unpack(slab, shapes):
    out, r = [], 0
    for s in shapes:
        n = int(np.prod(s))
        out.append(slab[r:r + _rows(n)].reshape(-1)[:n].reshape(s))
        r += _rows(n)
    return out


def _block_diag4(w):
    w4 = w.reshape(4, 4, 64, 64)
    eye = jnp.eye(4, dtype=w.dtype)
    return (w4[:, :, :, None, :] * eye[None, :, None, :, None]).reshape(4, LRU_BLOCK, LRU_BLOCK)


def _diag_blocks4(m):
    m5 = m.reshape(4, 4, 64, 4, 64)
    return jnp.stack([m5[:, a, :, a, :] for a in range(4)], axis=1).reshape(LRU_HEADS, 64, 64)


def kernel(x, c, w_ada, b_ada, pre_norm1, post_norm1, w_in, b_gate, lru_conv_w, lru_conv_b, lru_wa, lru_ba, lru_wx, lru_bx, lru_lambda, w_pa, ssd_conv_w, ssd_conv_b, ssd_dt_bias, ssd_a_log, ssd_d, ssd_norm_w, w_pb, w_out, pre_norm2, post_norm2, w_ff1, w_ff2, loss_target, m_w_ada, m_b_ada, m_pre_norm1, m_post_norm1, m_w_in, m_b_gate, m_lru_conv_w, m_lru_conv_b, m_lru_wa, m_lru_ba, m_lru_wx, m_lru_bx, m_lru_lambda, m_w_pa, m_ssd_conv_w, m_ssd_conv_b, m_ssd_dt_bias, m_ssd_a_log, m_ssd_d, m_ssd_norm_w, m_w_pb, m_w_out, m_pre_norm2, m_post_norm2, m_w_ff1, m_w_ff2, v_w_ada, v_b_ada, v_pre_norm1, v_post_norm1, v_w_in, v_b_gate, v_lru_conv_w, v_lru_conv_b, v_lru_wa, v_lru_ba, v_lru_wx, v_lru_bx, v_lru_lambda, v_w_pa, v_ssd_conv_w, v_ssd_conv_b, v_ssd_dt_bias, v_ssd_a_log, v_ssd_d, v_ssd_norm_w, v_w_pb, v_w_out, v_pre_norm2, v_post_norm2, v_w_ff1, v_w_ff2):
    given = dict(locals())
    w = {k: given[k] for k in WEIGHTS}
    mom = {k: given["m_" + k] for k in WEIGHTS}
    var = {k: given["v_" + k] for k in WEIGHTS}
    nb, seq, _ = x.shape
    assert nb == 2 and seq % 512 == 0, (nb, seq)
    t = nb * seq
    me = _dev_index(lax.axis_index("x"), lax.axis_index("y"), lax.axis_index("c"))
    x2 = x.reshape(t, D_MODEL)
    tgt2 = loss_target.reshape(t, D_MODEL)
    ada_cols = w_ada.shape[2]

    slab = jnp.zeros((16, LANES), F32)
    slab = slab.at[0:nb].set(c)
    slab = slab.at[2:6, 0:lru_conv_w.shape[2]].set(lru_conv_w[0])
    slab = slab.at[6:10, 0:ssd_conv_w.shape[2]].set(ssd_conv_w[0])
    g1, gw_in = all_gather([slab, w_in.astype(BF16)], "gather_cond_w_in", relay=True)
    c_all = g1[:, 0:nb].reshape(NDEV * nb, D_MODEL)
    lru_cw = g1[:, 2:6, 0:lru_conv_w.shape[2]].transpose(1, 0, 2).reshape(4, D_MODEL)
    ssd_cw = g1[:, 6:10, 0:ssd_conv_w.shape[2]].transpose(1, 0, 2).reshape(4, 2 * SSD_INNER)
    b_cols = lax.dynamic_slice(b_ada, (0, me * ada_cols), (1, ada_cols))
    mod_cols = ada_fwd(c_all, w_ada[0], b_cols)
    (g2,) = all_gather([mod_cols], "gather_mod")
    mod_all = g2.transpose(1, 0, 2).reshape(NDEV * nb, N_MOD * D_MODEL)
    mod_mine = lax.dynamic_slice(mod_all, (me * nb, 0), (nb, N_MOD * D_MODEL)).reshape(nb, N_MOD, D_MODEL)
    mod8 = jnp.pad(mod_mine, ((0, 0), (0, 8 - N_MOD), (0, 0)))

    shard = IN_DIM // NDEV
    kd, od = DT_COL0 // shard, DT_COL0 % shard
    assert od + SSD_HEADS <= shard
    gb = gw_in[:, 0]
    w_main = jnp.concatenate([gb[k] for k in range(kd)] + [gb[kd][:, :od], gb[kd][:, od + SSD_HEADS:]]
                             + [gb[k] for k in range(kd + 1, NDEV)], axis=1)
    w_dt = jnp.pad(gb[kd][:, od:od + SSD_HEADS], ((0, 0), (0, 128 - SSD_HEADS)))

    wa_bd = _block_diag4(lru_wa[0]).astype(BF16)
    wx_bd = _block_diag4(lru_wx[0]).astype(BF16)
    lam = lru_lambda[0]
    vec = _pack([lru_ba, lru_bx, jax.nn.softplus(-lam)], 8)
    tri, triu = ssd_consts()
    hp, lp = ssd_params(ssd_dt_bias[0], ssd_a_log[0], ssd_d[0], ssd_norm_w[0])

    rest = Gather([w[k].astype(BF16) for k in BIG[1:]])
    cw_all = jnp.concatenate([lru_cw, ssd_cw], axis=1)
    cb_all = jnp.concatenate([lru_conv_b, ssd_conv_b], axis=1)
    (proj, h1t, dtraw, xa, xbc, dsilu), gw = in_proj_fwd(x2, mod8, pre_norm1, w_main, w_dt, cw_all, cb_all, seq, side=rest)
    w_pa_f = gw[0].reshape(D_MODEL, D_MODEL)
    w_pb_f = gw[1].reshape(SSD_INNER, D_MODEL)
    w_out_f = gw[2].reshape(D_MODEL, D_MODEL)
    w_ff1_f = gw[3][:, 0]
    w_ff2_f = gw[4].reshape(D_FF, D_MODEL)
    ya_in, hst = lru_fwd(xa, proj, wa_bd, wx_bd, vec, nb, seq)
    yb_in, states = ssd_fwd(xbc, proj, dtraw, hp, lp, tri, nb, seq)
    yab, out1, x1 = merge_fwd(ya_in, yb_in, proj, x2, mod8, b_gate, post_norm1, w_pa_f, w_pb_f, w_out_f, seq)

    dx1, h2, da1, act, dy2, loss8, vacc_mlp, dmod_mlp = mlp_fwd_bwd(
        x1, tgt2, mod8, pre_norm2, post_norm2, w_ff1_f, w_ff2_f, nb, seq)
    wg = dict(out_dtype=BF16, ta=True, tm=1024, tn=1024, tk=1024)
    dw_ff1 = matmul(h2, da1, name="wgrad_ff1", blocked_out=D_FF // NDEV, **wg)
    dw_ff2 = matmul(act, dy2, name="wgrad_ff2", **wg)
    dya_in, dyb_in, dgates, dyab, dout1, merged, vacc_mg, dmod_mg = merge_bwd(
        dx1, out1, yab, proj, mod8, b_gate, post_norm1, w_pa_f, w_pb_f, w_out_f, nb, seq)
    dw_out = matmul(merged, dout1, name="wgrad_out", **wg)
    dw_pa = matmul(ya_in, dyab, name="wgrad_pa", n=D_MODEL, b_off=0, **wg)
    dw_pb = matmul(yb_in, dyab, name="wgrad_pb", n=D_MODEL, b_off=1, **wg)
    by_rows = lambda g: g.reshape(NDEV, g.shape[0] // NDEV, g.shape[1])
    (dxa, dlg, dwa_bd, dwx_bd, dvec), parts_ff = lru_bwd(
        dya_in, xa, proj, hst, wa_bd, wx_bd, vec, nb, seq, side=Scatter([dw_ff1, by_rows(dw_ff2)]))
    (dxbc, dz, ddt, hpg, lpg), parts_mg = ssd_bwd(xbc, proj, dtraw, hp, lp, tri, triu, states, dyb_in, nb, seq,
                                                  side=Scatter([by_rows(dw_pa), by_rows(dw_pb), by_rows(dw_out)]))
    ddt_b = ddt.astype(BF16)
    accs = dict(vmg=vacc_mg, vmlp=vacc_mlp, dvec=dvec, hpg=hpg, lpg=lpg,
                gwa=_diag_blocks4(dwa_bd).reshape(LRU_HEADS * 64, 64), gwx=_diag_blocks4(dwx_bd).reshape(LRU_HEADS * 64, 64))
    (dw_main, dw_dt, dlx, dxr, acc_l, acc_s), g_small = in_proj_wgrad(
        h1t, proj, dxa, dxbc, dsilu, dlg, dz, dgates, ddt_b, cw_all, seq, side=Gather([accs[k] for k in SMALL_RIDE]))
    pieces = (dlx, dlg, dz, dxr, dgates)
    cut = lambda k: dw_main[:, k * shard - (SSD_HEADS if k > kd else 0):(k + 1) * shard - (SSD_HEADS if k >= kd else 0)]
    blk_dt = jnp.concatenate([dw_main[:, kd * shard:DT_COL0], dw_dt[:, :SSD_HEADS],
                              dw_main[:, DT_COL0:(kd + 1) * shard - SSD_HEADS]], axis=1)
    dw_blocks = jnp.stack([blk_dt if k == kd else cut(k) for k in range(NDEV)])
    (grad_x, vacc_in, dmod_in), parts_in = in_proj_bwd(pieces, ddt_b, dx1, x2, mod8, pre_norm1, w_main, w_dt, nb, seq,
                                                       side=Scatter([dw_blocks]))
    parts = dict(zip(BIG, (parts_in[0], *parts_mg, *parts_ff)))

    dmod = (dmod_in + dmod_mg + dmod_mlp)[:, :N_MOD].reshape(nb, N_MOD * D_MODEL)
    g3, g_vin, g_accl, g_accs = all_gather([jnp.pad(dmod, ((0, 8 - nb), (0, 0))), vacc_in, acc_l, acc_s], "gather_dmod")
    dmod_all = g3[:, :nb].reshape(NDEV * nb, N_MOD * D_MODEL)
    dmod_cols = lax.dynamic_slice(dmod_all, (0, me * ada_cols), (NDEV * nb, ada_cols))
    g_w_ada, g_b_ada = ada_bwd(c_all, dmod_cols, dmod_all)

    res = {}
    for k in BIG:
        res[k] = adamw(parts[k], w[k], mom[k], var[k], "adamw_" + k)
    res['w_ada'] = adamw(g_w_ada[None], w_ada, m_w_ada, v_w_ada, "adamw_w_ada")

    gathered = dict(zip(SMALL_RIDE, g_small), vin=g_vin, accl=g_accl, accs=g_accs)
    view = lambda a: a.reshape(-1, a.shape[-1])
    res_a, g_lru_cw, g_ssd_cw = adamw_small(gathered, {k: (view(w[k]), view(mom[k]), view(var[k])) for k in REPL})
    res.update(res_a)
    lcw, scw = lru_conv_w.shape[2], ssd_conv_w.shape[2]
    sharded = {'b_ada': g_b_ada[None], 'lru_conv_w': lax.dynamic_slice(g_lru_cw, (0, me * lcw), (4, lcw))[None],
               'ssd_conv_w': lax.dynamic_slice(g_ssd_cw, (0, me * scw), (4, scw))[None]}
    for k, g in sharded.items():
        as3 = lambda a: a.reshape(g.shape)
        res[k] = adamw(g, as3(w[k]), as3(mom[k]), as3(var[k]), "adamw_" + k)

    loss = lax.psum(loss8[0, 0], ("x", "y", "c"))
    outs = [[res[k][j].reshape(w[k].shape) for k in WEIGHTS] for j in range(4)]
    return (loss, grad_x.reshape(x.shape), *outs[0], *outs[1], *outs[2], *outs[3])
```

```python
import functools

import numpy as np
import jax
import jax.numpy as jnp
from jax import lax
from jax.experimental import pallas as pl
from jax.experimental.pallas import tpu as pltpu

F32 = jnp.float32
BF16 = jnp.bfloat16

D_MODEL = 1024
LRU_HEADS = 16
LRU_BLOCK = 256
LRU_C = 8.0
SSD_INNER = 2048
SSD_HEADS = 32
SSD_P = 64
SSD_G = 8
SSD_N = 128
SSD_L = 128
SSD_GW = SSD_INNER // SSD_G
D_FF = 4096
N_MOD = 6
EPS = 1e-6
NDEV = 8

C_LRU_X, C_LRU_G, C_Z, C_XBC, C_GATES, PROJ_MAIN = 0, 1024, 2048, 4096, 8192, 10240
IN_DIM = 10272
DT_COL0 = 8192
HALO = 16
SSD_FWD_CPS = 1
HT_TOK = 512

ADAM_LR, ADAM_B1, ADAM_B2, ADAM_EPS, ADAM_WD, ADAM_STEP = 0.001, 0.9, 0.999, 1e-08, 0.01, 10

VMEM_LIMIT = 60 * 1024 * 1024
MESH = pl.DeviceIdType.MESH
ANY = pl.BlockSpec(memory_space=pl.ANY)
VMEM_FULL = pl.BlockSpec(memory_space=pltpu.VMEM)


def _cp(*sem):
    return pltpu.CompilerParams(dimension_semantics=sem, vmem_limit_bytes=VMEM_LIMIT)


def _dot(a, b):
    return jnp.dot(a, b, preferred_element_type=F32)


def _dot_nt(a, b):
    return lax.dot_general(a, b, (((1,), (1,)), ((), ())), preferred_element_type=F32)


def _dot_tn(a, b):
    return lax.dot_general(a, b, (((0,), (0,)), ((), ())), preferred_element_type=F32)


def _dot_hi(a, b):
    return jnp.dot(a, b, precision=lax.Precision.HIGHEST, preferred_element_type=F32)


def _sigmoid(x):
    return 1.0 / (1.0 + jnp.exp(-x))


def _gelu_and_grad(x):
    k0, k1 = 0.7978845608028654, 0.044715
    t = jnp.tanh(k0 * (x + k1 * x * x * x))
    g = 0.5 * x * (1.0 + t)
    dg = 0.5 * (1.0 + t) + 0.5 * x * (1.0 - t * t) * k0 * (1.0 + 3.0 * k1 * x * x)
    return g, dg


def _neg_expm1(y):
    p = 1.0 + y * (1.0 / 7.0)
    p = 1.0 + y * (1.0 / 6.0) * p
    p = 1.0 + y * (1.0 / 5.0) * p
    p = 1.0 + y * (1.0 / 4.0) * p
    p = 1.0 + y * (1.0 / 3.0) * p
    p = 1.0 + y * 0.5 * p
    return jnp.where(y > -0.3, -y * p, 1.0 - jnp.exp(y))


def _colsum(v):
    return jnp.sum(v, axis=0, keepdims=True)


def _rowmean(v):
    return jnp.mean(v, axis=-1, keepdims=True)


def matmul(a, b, *, ta=False, tb=False, out_dtype=F32, tm, tn, tk, name, n=None, b_off=0, blocked_out=False):
    m = a.shape[1] if ta else a.shape[0]
    kdim = a.shape[0] if ta else a.shape[1]
    n = n or (b.shape[0] if tb else b.shape[1])
    tm, tn, tk = min(tm, m), min(tn, n), min(tk, kdim)
    nk = kdim // tk
    dn = (((0 if ta else 1,), (1 if tb else 0,)), ((), ()))
    bw = blocked_out or tn

    def body(a_ref, b_ref, o_ref, acc_ref):
        k = pl.program_id(2)
        p = lax.dot_general(a_ref[...], b_ref[...], dn, preferred_element_type=F32)

        def emit(v):
            if blocked_out:
                for q in range(tn // bw):
                    o_ref[q] = v[:, q * bw:(q + 1) * bw].astype(out_dtype)
            else:
                o_ref[...] = v.astype(out_dtype)

        if nk == 1:
            emit(p)
        else:
            @pl.when(k == 0)
            def _():
                acc_ref[...] = p

            @pl.when(k > 0)
            def _():
                acc_ref[...] += p

            @pl.when(k == nk - 1)
            def _():
                emit(acc_ref[...])

    a_spec = pl.BlockSpec((tk, tm), lambda i, j, k: (k, i)) if ta else pl.BlockSpec((tm, tk), lambda i, j, k: (i, k))
    b_spec = (pl.BlockSpec((tn, tk), lambda i, j, k: (j, k)) if tb
              else pl.BlockSpec((tk, tn), lambda i, j, k: (k, j + b_off)))
    if blocked_out:
        o_spec, o_shape = pl.BlockSpec((tn // bw, tm, bw), lambda i, j, k: (j, i, 0)), (n // bw, m, bw)
    else:
        o_spec, o_shape = pl.BlockSpec((tm, tn), lambda i, j, k: (i, j)), (m, n)
    return pl.pallas_call(
        body, name=name, grid=(m // tm, n // tn, nk),
        in_specs=[a_spec, b_spec], out_specs=o_spec,
        out_shape=jax.ShapeDtypeStruct(o_shape, out_dtype),
        scratch_shapes=[pltpu.VMEM((tm, tn), F32)],
        compiler_params=_cp("parallel", "parallel", "arbitrary"),
    )(a, b)


def _conv_tile(j, tn):
    return jnp.where(j == 0, 0, jnp.clip(j - C_XBC // tn + 1, 1, 2 * SSD_INNER // tn))


def in_proj_fwd(x2, mod8, pre1, w_main, w_dt, cw, cb, seq, side=None):
    t = x2.shape[0]
    tm = min(1024, seq)
    tn = 1024
    per_seq = seq // tm
    j_xbc = C_XBC // tn
    n_xbc = 2 * SSD_INNER // tn
    cs = 256

    def body(x_ref, mod_ref, pre_ref, w_ref, wdt_ref, cw_ref, cb_ref, proj_ref, h_ref, dt_ref, xa_ref, xbc_ref, ds_ref,
             h_scr, carry_scr):
        i, j = pl.program_id(0), pl.program_id(1)

        @pl.when(j == 0)
        def _():
            xv = x_ref[...]
            y = xv * lax.rsqrt(_rowmean(xv * xv) + EPS) * pre_ref[...]
            m = mod_ref[0]
            hf = y * (1.0 + m[1:2, :]) + m[0:1, :]
            h = hf.astype(BF16)
            h_scr[...] = h
            hft = hf.T.astype(BF16)
            for q in range(tm // HT_TOK):
                h_ref[q] = hft[:, q * HT_TOK:(q + 1) * HT_TOK]
            dt_ref[...] = _dot(h, wdt_ref[...])

        def project(c0=0, width=tn):
            pb = _dot(h_scr[...], w_ref[:, c0:c0 + width]).astype(BF16)
            proj_ref[:, c0:c0 + width] = pb
            return pb

        def conv(o_ref, slot, act):
            first = lax.rem(i, per_seq) == 0
            for c0 in range(0, tn, cs):
                cur = project(c0, cs).astype(F32)
                prev = jnp.where(first, 0.0, carry_scr[slot, :, c0:c0 + cs])
                carry_scr[slot, :, c0:c0 + cs] = cur[tm - HALO:, :]
                xx = jnp.concatenate([prev, cur], axis=0)
                w = cw_ref[:, c0:c0 + cs]
                acc = cur * w[3:4, :] + cb_ref[:, c0:c0 + cs]
                for d in (1, 2, 3):
                    acc = acc + pltpu.roll(xx, d, axis=0)[HALO:, :] * w[3 - d:4 - d, :]
                if act:
                    sg = _sigmoid(acc)
                    ds_ref[:, c0:c0 + cs] = (sg * (1.0 + acc * (1.0 - sg))).astype(BF16)
                    acc = acc * sg
                o_ref[:, c0:c0 + cs] = acc.astype(BF16)

        is_xbc = (j >= j_xbc) & (j < j_xbc + n_xbc)

        @pl.when(j == 0)
        def _():
            conv(xa_ref, 0, False)

        @pl.when(is_xbc)
        def _():
            conv(xbc_ref, j - j_xbc + 1, True)

        @pl.when((j > 0) & jnp.logical_not(is_xbc))
        def _():
            project()

    return _call(
        body, name="in_proj_fwd", grid=(t // tm, PROJ_MAIN // tn), side=side, sem=("arbitrary", "arbitrary"),
        args=(x2, mod8, pre1, w_main, w_dt, cw, cb),
        in_specs=[pl.BlockSpec((tm, D_MODEL), lambda i, j: (i, 0)),
                  pl.BlockSpec((1, 8, D_MODEL), lambda i, j: (i // per_seq, 0, 0)),
                  pl.BlockSpec((1, D_MODEL), lambda i, j: (0, 0)),
                  pl.BlockSpec((D_MODEL, tn), lambda i, j: (0, j)),
                  pl.BlockSpec((D_MODEL, 128), lambda i, j: (0, 0)),
                  pl.BlockSpec((4, tn), lambda i, j: (0, _conv_tile(j, tn))),
                  pl.BlockSpec((1, tn), lambda i, j: (0, _conv_tile(j, tn)))],
        out_specs=[pl.BlockSpec((tm, tn), lambda i, j: (i, j)),
                   pl.BlockSpec((tm // HT_TOK, D_MODEL, HT_TOK), lambda i, j: (i, 0, 0)),
                   pl.BlockSpec((tm, 128), lambda i, j: (i, 0)),
                   pl.BlockSpec((tm, tn), lambda i, j: (i, 0)),
                   pl.BlockSpec((tm, tn), lambda i, j: (i, jnp.clip(j - j_xbc, 0, n_xbc - 1))),
                   pl.BlockSpec((tm, tn), lambda i, j: (i, jnp.clip(j - j_xbc, 0, n_xbc - 1)))],
        out_shape=[jax.ShapeDtypeStruct((t, PROJ_MAIN), BF16), jax.ShapeDtypeStruct((t // HT_TOK, D_MODEL, HT_TOK), BF16),
                   jax.ShapeDtypeStruct((t, 128), F32), jax.ShapeDtypeStruct((t, D_MODEL), BF16),
                   jax.ShapeDtypeStruct((t, 2 * SSD_INNER), BF16), jax.ShapeDtypeStruct((t, 2 * SSD_INNER), BF16)],
        scratch_shapes=[pltpu.VMEM((tm, D_MODEL), BF16), pltpu.VMEM((1 + n_xbc, HALO, tn), F32)])


def _lru_gates(xa, wa_ref, wx_ref, ba, bx, sp):
    nblk = D_MODEL // LRU_BLOCK
    pr = jnp.concatenate([_dot(xa[:, j * LRU_BLOCK:(j + 1) * LRU_BLOCK], wa_ref[j]) for j in range(nblk)], axis=1) + ba
    pi = jnp.concatenate([_dot(xa[:, j * LRU_BLOCK:(j + 1) * LRU_BLOCK], wx_ref[j]) for j in range(nblk)], axis=1) + bx
    r = _sigmoid(pr)
    i = _sigmoid(pi)
    log_a = (-LRU_C * r) * sp
    return r, i, jnp.exp(log_a), _neg_expm1(2.0 * log_a)


def lru_fwd(xa, proj, wa_bd, wx_bd, vec, nb, seq):
    t = xa.shape[0]
    tc = min(512, seq)
    nk = seq // tc
    gb = C_LRU_G // D_MODEL

    def body(xa_ref, g_ref, wa_ref, wx_ref, vec_ref, ya_ref, h_ref, gates_ref, a_scr, u_scr, hc_scr):
        @pl.when(pl.program_id(1) == 0)
        def _():
            hc_scr[...] = jnp.zeros_like(hc_scr)

        xa_v = xa_ref[...]
        v = vec_ref[...]
        r, i, a, e = _lru_gates(xa_v, wa_ref, wx_ref, v[0:1, :], v[1:2, :], v[2:3, :])
        s = jnp.sqrt(e)
        gates_ref[...] = jnp.concatenate([r, i, a, s], axis=1)
        a_scr[...] = a
        u_scr[...] = s * (i * xa_v.astype(F32))
        row = lax.broadcasted_iota(jnp.int32, (8, 1), 0)

        def tile(j, h):
            r0 = pl.multiple_of(j * 8, 8)
            av, uv = a_scr[pl.ds(r0, 8), :], u_scr[pl.ds(r0, 8), :]
            for d in (1, 2, 4):
                uv = uv + av * jnp.where(row >= d, pltpu.roll(uv, d, axis=0), 0.0)
                av = av * jnp.where(row >= d, pltpu.roll(av, d, axis=0), 1.0)
            hv = uv + av * h
            h_ref[pl.ds(r0, 8), :] = hv
            return hv[7:8, :]

        hc_scr[...] = lax.fori_loop(0, tc // 8, tile, hc_scr[...], unroll=2)
        gel, _ = _gelu_and_grad(g_ref[...].astype(F32))
        ya_ref[...] = (h_ref[...] * gel).astype(BF16)

    return pl.pallas_call(
        body, name="lru_fwd", grid=(nb, nk),
        in_specs=[pl.BlockSpec((tc, D_MODEL), lambda b, k: (b * nk + k, 0)),
                  pl.BlockSpec((tc, D_MODEL), lambda b, k: (b * nk + k, gb)),
                  VMEM_FULL, VMEM_FULL, VMEM_FULL],
        out_specs=[pl.BlockSpec((tc, D_MODEL), lambda b, k: (b * nk + k, 0)),
                   pl.BlockSpec((tc, D_MODEL), lambda b, k: (b * nk + k, 0)),
                   pl.BlockSpec((tc, 4 * D_MODEL), lambda b, k: (b * nk + k, 0))],
        out_shape=[jax.ShapeDtypeStruct((t, D_MODEL), BF16), jax.ShapeDtypeStruct((t, D_MODEL), F32),
                   jax.ShapeDtypeStruct((t, 4 * D_MODEL), F32)],
        scratch_shapes=[pltpu.VMEM((tc, D_MODEL), F32), pltpu.VMEM((tc, D_MODEL), F32), pltpu.VMEM((1, D_MODEL), F32)],
        compiler_params=_cp("arbitrary", "arbitrary"),
    )(xa, proj, wa_bd, wx_bd, vec)


def lru_bwd(dya, xa, proj, h, gates, wa_bd, wx_bd, vec, nb, seq, side=None):
    t = xa.shape[0]
    tc = min(512, seq)
    nk = seq // tc
    gb = C_LRU_G // D_MODEL
    nblk = D_MODEL // LRU_BLOCK

    def chunk(b, k):
        return b * nk + (nk - 1 - k)

    def body(dya_ref, xa_ref, g_ref, h_ref, hp_ref, gates_ref, wa_ref, wx_ref, vec_ref,
             dxa_ref, dg_ref, dwa_ref, dwx_ref, dvec_ref, a_scr, dh_scr, c_scr):
        b, k = pl.program_id(0), pl.program_id(1)

        @pl.when((b == 0) & (k == 0))
        def _():
            dwa_ref[...] = jnp.zeros_like(dwa_ref)
            dwx_ref[...] = jnp.zeros_like(dwx_ref)
            dvec_ref[...] = jnp.zeros_like(dvec_ref)

        @pl.when(k == 0)
        def _():
            c_scr[...] = jnp.zeros_like(c_scr)

        xa_v = xa_ref[...]
        xaf = xa_v.astype(F32)
        v = vec_ref[...]
        sp = v[2:3, :]
        r, i = gates_ref[:, 0:D_MODEL], gates_ref[:, D_MODEL:2 * D_MODEL]
        a, s = gates_ref[:, 2 * D_MODEL:3 * D_MODEL], gates_ref[:, 3 * D_MODEL:]
        gel, dgel = _gelu_and_grad(g_ref[...].astype(F32))
        hv = h_ref[...]
        dyv = dya_ref[...].astype(F32)
        dg_ref[...] = (dyv * hv * dgel).astype(BF16)
        a_scr[...] = a
        dh_scr[...] = dyv * gel

        row8 = lax.broadcasted_iota(jnp.int32, (8, 1), 0)

        def tile(j, c):
            r0 = pl.multiple_of((tc // 8 - 1 - j) * 8, 8)
            av, dout = a_scr[pl.ds(r0, 8), :], dh_scr[pl.ds(r0, 8), :]
            zv = av * dout
            for d in (1, 2, 4):
                zv = zv + av * jnp.where(row8 < 8 - d, pltpu.roll(zv, 8 - d, axis=0), 0.0)
                av = av * jnp.where(row8 < 8 - d, pltpu.roll(av, 8 - d, axis=0), 1.0)
            zv = zv + av * c
            dh_scr[pl.ds(r0, 8), :] = dout + jnp.where(row8 < 7, pltpu.roll(zv, 7, axis=0), c)
            return zv[0:1, :]

        c_scr[...] = lax.fori_loop(0, tc // 8, tile, c_scr[...], unroll=2)
        dh = dh_scr[...]
        h_last = jnp.where(k == nk - 1, 0.0, hp_ref[HALO // 2 - 1:HALO // 2, :])
        row = lax.broadcasted_iota(jnp.int32, (tc, 1), 0)
        h_prev = jnp.where(row == 0, h_last, pltpu.roll(hv, 1, axis=0))
        da = dh * h_prev
        ix = i * xaf
        dlog_a = da * a - (dh * ix) * (a * a) / jnp.maximum(s, 1e-15)
        di = dh * s * xaf
        dpr = (dlog_a * (-LRU_C * sp)) * (r * (1.0 - r))
        dpi = di * (i * (1.0 - i))
        dprb, dpib = dpr.astype(BF16), dpi.astype(BF16)
        dxa = dh * s * i
        dxa = dxa + jnp.concatenate(
            [_dot_nt(dprb[:, j * LRU_BLOCK:(j + 1) * LRU_BLOCK], wa_ref[j])
             + _dot_nt(dpib[:, j * LRU_BLOCK:(j + 1) * LRU_BLOCK], wx_ref[j]) for j in range(nblk)], axis=1)
        dxa_ref[...] = dxa.astype(BF16)
        for j in range(nblk):
            sl = slice(j * LRU_BLOCK, (j + 1) * LRU_BLOCK)
            dwa_ref[j] += _dot_tn(xa_v[:, sl], dprb[:, sl])
            dwx_ref[j] += _dot_tn(xa_v[:, sl], dpib[:, sl])
        dvec_ref[...] += jnp.concatenate(
            [_colsum(dpr), _colsum(dpi), _colsum(dlog_a * (-LRU_C * r)), jnp.zeros((5, D_MODEL), F32)], axis=0)

    hh = HALO // 2
    return _call(
        body, name="lru_bwd", grid=(nb, nk), side=side, sem=("arbitrary", "arbitrary"),
        args=(dya, xa, proj, h, h, gates, wa_bd, wx_bd, vec),
        in_specs=[pl.BlockSpec((tc, D_MODEL), lambda b, k: (chunk(b, k), 0)),
                  pl.BlockSpec((tc, D_MODEL), lambda b, k: (chunk(b, k), 0)),
                  pl.BlockSpec((tc, D_MODEL), lambda b, k: (chunk(b, k), gb)),
                  pl.BlockSpec((tc, D_MODEL), lambda b, k: (chunk(b, k), 0)),
                  pl.BlockSpec((hh, D_MODEL), lambda b, k: (jnp.maximum(chunk(b, k) * (tc // hh) - 1, 0), 0)),
                  pl.BlockSpec((tc, 4 * D_MODEL), lambda b, k: (chunk(b, k), 0)),
                  VMEM_FULL, VMEM_FULL, VMEM_FULL],
        out_specs=[pl.BlockSpec((tc, D_MODEL), lambda b, k: (chunk(b, k), 0)),
                   pl.BlockSpec((tc, D_MODEL), lambda b, k: (chunk(b, k), 0)),
                   pl.BlockSpec((nblk, LRU_BLOCK, LRU_BLOCK), lambda b, k: (0, 0, 0)),
                   pl.BlockSpec((nblk, LRU_BLOCK, LRU_BLOCK), lambda b, k: (0, 0, 0)),
                   pl.BlockSpec((8, D_MODEL), lambda b, k: (0, 0))],
        out_shape=[jax.ShapeDtypeStruct((t, D_MODEL), BF16), jax.ShapeDtypeStruct((t, D_MODEL), BF16),
                   jax.ShapeDtypeStruct((nblk, LRU_BLOCK, LRU_BLOCK), F32),
                   jax.ShapeDtypeStruct((nblk, LRU_BLOCK, LRU_BLOCK), F32),
                   jax.ShapeDtypeStruct((8, D_MODEL), F32)],
        scratch_shapes=[pltpu.VMEM((tc, D_MODEL), F32), pltpu.VMEM((tc, D_MODEL), F32), pltpu.VMEM((1, D_MODEL), F32)])


def merge_fwd(ya_in, yb_in, proj, x2, mod8, bgate, post1, w_pa, w_pb, w_out, seq):
    t = x2.shape[0]
    tm = min(512, seq)
    per_seq = seq // tm
    gcb = C_GATES // SSD_INNER

    def body(ya_ref, yb_ref, gt_ref, x_ref, mod_ref, bg_ref, post_ref, wpa_ref, wpb_ref, wo_ref,
             yab_ref, out1_ref, x1_ref):
        y_a = _dot(ya_ref[...], wpa_ref[...])
        y_b = _dot(yb_ref[...], wpb_ref[...])
        g = _sigmoid(gt_ref[...].astype(F32) + bg_ref[...])
        merged = g[:, :D_MODEL] * y_a + g[:, D_MODEL:] * y_b
        out1 = _dot(merged.astype(BF16), wo_ref[...])
        n = out1 * lax.rsqrt(_rowmean(out1 * out1) + EPS)
        yab_ref[...] = jnp.concatenate([y_a, y_b], axis=1).astype(BF16)
        out1_ref[...] = out1
        x1_ref[...] = x_ref[...] + mod_ref[0][2:3, :] * (n * post_ref[...])

    row = lambda w: pl.BlockSpec((tm, w), lambda i: (i, 0))
    return pl.pallas_call(
        body, name="merge_fwd", grid=(t // tm,),
        in_specs=[row(D_MODEL), row(SSD_INNER), pl.BlockSpec((tm, SSD_INNER), lambda i: (i, gcb)), row(D_MODEL),
                  pl.BlockSpec((1, 8, D_MODEL), lambda i: (i // per_seq, 0, 0)),
                  VMEM_FULL, VMEM_FULL, VMEM_FULL, VMEM_FULL, VMEM_FULL],
        out_specs=[row(SSD_INNER), row(D_MODEL), row(D_MODEL)],
        out_shape=[jax.ShapeDtypeStruct((t, SSD_INNER), BF16), jax.ShapeDtypeStruct((t, D_MODEL), F32),
                   jax.ShapeDtypeStruct((t, D_MODEL), F32)],
        compiler_params=_cp("parallel"),
    )(ya_in, yb_in, proj, x2, mod8, bgate, post1, w_pa, w_pb, w_out)


def merge_bwd(dx1, out1, yab, proj, mod8, bgate, post1, w_pa, w_pb, w_out, nb, seq):
    t = dx1.shape[0]
    tm = min(512, seq)
    per_seq = seq // tm
    gcb = C_GATES // SSD_INNER

    def body(dx1_ref, out1_ref, yab_ref, gt_ref, mod_ref, bg_ref, post_ref, wpa_ref, wpb_ref, wo_ref,
             dya_ref, dyb_ref, dgt_ref, dyab_ref, dout1_ref, mg_ref, vacc_ref, dmod_ref):
        b, s = pl.program_id(0), pl.program_id(1)

        @pl.when((b == 0) & (s == 0))
        def _():
            vacc_ref[...] = jnp.zeros_like(vacc_ref)

        @pl.when(s == 0)
        def _():
            dmod_ref[...] = jnp.zeros_like(dmod_ref)

        dx1v = dx1_ref[...]
        out1 = out1_ref[...]
        post = post_ref[...]
        rs = lax.rsqrt(_rowmean(out1 * out1) + EPS)
        n = out1 * rs
        do = dx1v * mod_ref[0][2:3, :]
        dn = do * post
        dout1 = rs * (dn - n * _rowmean(dn * n))
        dout1b = dout1.astype(BF16)
        dout1_ref[...] = dout1b
        dmerged = _dot_nt(dout1b, wo_ref[...])
        g = _sigmoid(gt_ref[...].astype(F32) + bg_ref[...])
        yab_v = yab_ref[...].astype(F32)
        gy = g * yab_v
        mg_ref[...] = (gy[:, :D_MODEL] + gy[:, D_MODEL:]).astype(BF16)
        dm2 = jnp.concatenate([dmerged, dmerged], axis=1)
        dyab = (dm2 * g).astype(BF16)
        dyab_ref[...] = dyab
        dgt = dm2 * gy * (1.0 - g)
        dgt_ref[...] = dgt.astype(BF16)
        dya_ref[...] = _dot_nt(dyab[:, :D_MODEL], wpa_ref[...]).astype(BF16)
        dyb_ref[...] = _dot_nt(dyab[:, D_MODEL:], wpb_ref[...]).astype(BF16)
        vacc_ref[...] += jnp.concatenate(
            [_colsum(dgt), jnp.concatenate([_colsum(do * n), jnp.zeros((1, D_MODEL), F32)], axis=1),
             jnp.zeros((6, SSD_INNER), F32)], axis=0)
        dmod_ref[0] += jnp.concatenate(
            [jnp.zeros((2, D_MODEL), F32), _colsum(dx1v * (n * post)), jnp.zeros((5, D_MODEL), F32)], axis=0)

    row = lambda w: pl.BlockSpec((tm, w), lambda b, s: (b * per_seq + s, 0))
    return pl.pallas_call(
        body, name="merge_bwd", grid=(nb, per_seq),
        in_specs=[row(D_MODEL), row(D_MODEL), row(SSD_INNER),
                  pl.BlockSpec((tm, SSD_INNER), lambda b, s: (b * per_seq + s, gcb)),
                  pl.BlockSpec((1, 8, D_MODEL), lambda b, s: (b, 0, 0)),
                  VMEM_FULL, VMEM_FULL, VMEM_FULL, VMEM_FULL, VMEM_FULL],
        out_specs=[row(D_MODEL), row(SSD_INNER), row(SSD_INNER), row(SSD_INNER), row(D_MODEL), row(D_MODEL),
                   pl.BlockSpec((8, SSD_INNER), lambda b, s: (0, 0)),
                   pl.BlockSpec((1, 8, D_MODEL), lambda b, s: (b, 0, 0))],
        out_shape=[jax.ShapeDtypeStruct((t, D_MODEL), BF16), jax.ShapeDtypeStruct((t, SSD_INNER), BF16),
                   jax.ShapeDtypeStruct((t, SSD_INNER), BF16), jax.ShapeDtypeStruct((t, SSD_INNER), BF16),
                   jax.ShapeDtypeStruct((t, D_MODEL), BF16), jax.ShapeDtypeStruct((t, D_MODEL), BF16),
                   jax.ShapeDtypeStruct((8, SSD_INNER), F32), jax.ShapeDtypeStruct((nb, 8, D_MODEL), F32)],
        compiler_params=_cp("arbitrary", "arbitrary"),
    )(dx1, out1, yab, proj, mod8, bgate, post1, w_pa, w_pb, w_out)


def mlp_fwd_bwd(x1, tgt, mod8, pre2, post2, w_ff1, w_ff2, nb, seq):
    t = x1.shape[0]
    tm = min(256, seq)
    per_seq = seq // tm
    fc = 1024
    nfc = D_FF // fc

    def body(x1_ref, tgt_ref, mod_ref, pre_ref, post_ref, w1_ref, w2_ref,
             dx1_ref, h2_ref, da1_ref, act_ref, dy2_ref, loss_ref, vacc_ref, dmod_ref, r_scr):
        b, s = pl.program_id(0), pl.program_id(1)
        per = fc // w1_ref.shape[2]

        def w1_cols(c):
            return jnp.concatenate([w1_ref[per * c + q] for q in range(per)], axis=1)

        @pl.when((b == 0) & (s == 0))
        def _():
            vacc_ref[...] = jnp.zeros_like(vacc_ref)
            loss_ref[...] = jnp.zeros_like(loss_ref)

        @pl.when(s == 0)
        def _():
            dmod_ref[...] = jnp.zeros_like(dmod_ref)

        m = mod_ref[0]
        sh2, sc2, g2 = m[3:4, :], m[4:5, :], m[5:6, :]
        pre, post = pre_ref[...], post_ref[...]
        x1v = x1_ref[...]
        rs1 = lax.rsqrt(_rowmean(x1v * x1v) + EPS)
        n1 = x1v * rs1
        y1 = n1 * pre
        h2b = (y1 * (1.0 + sc2) + sh2).astype(BF16)
        h2_ref[...] = h2b
        y2 = jnp.zeros((tm, D_MODEL), F32)
        for c in range(nfc):
            r = jnp.maximum(_dot(h2b, w1_cols(c)), 0.0)
            r_scr[:, c * fc:(c + 1) * fc] = r
            a = (r * r).astype(BF16)
            act_ref[:, c * fc:(c + 1) * fc] = a
            y2 = y2 + _dot(a, w2_ref[c * fc:(c + 1) * fc, :])
        rs2 = lax.rsqrt(_rowmean(y2 * y2) + EPS)
        n2 = y2 * rs2
        o2 = n2 * post
        diff = x1v + g2 * o2 - tgt_ref[...]
        loss_ref[...] += 0.5 * jnp.sum(_rowmean(diff * diff))
        dx2 = diff * (1.0 / D_MODEL)
        do2 = dx2 * g2
        dn2 = do2 * post
        dy2b = (rs2 * (dn2 - n2 * _rowmean(dn2 * n2))).astype(BF16)
        dy2_ref[...] = dy2b
        dh2 = jnp.zeros((tm, D_MODEL), F32)
        for c in range(nfc):
            dact = _dot_nt(dy2b, w2_ref[c * fc:(c + 1) * fc, :])
            da = (dact * (2.0 * r_scr[:, c * fc:(c + 1) * fc])).astype(BF16)
            da1_ref[:, c * fc:(c + 1) * fc] = da
            dh2 = dh2 + _dot_nt(da, w1_cols(c))
        dy1 = dh2 * (1.0 + sc2)
        dn1 = dy1 * pre
        dx1_ref[...] = dx2 + rs1 * (dn1 - n1 * _rowmean(dn1 * n1))
        vacc_ref[...] += jnp.concatenate([_colsum(dy1 * n1), _colsum(do2 * n2), jnp.zeros((6, D_MODEL), F32)], axis=0)
        dmod_ref[0] += jnp.concatenate(
            [jnp.zeros((3, D_MODEL), F32), _colsum(dh2), _colsum(dh2 * y1), _colsum(dx2 * o2),
             jnp.zeros((2, D_MODEL), F32)], axis=0)

    row = lambda w: pl.BlockSpec((tm, w), lambda b, s: (b * per_seq + s, 0))
    return pl.pallas_call(
        body, name="mlp_fwd_bwd", grid=(nb, per_seq),
        in_specs=[row(D_MODEL), row(D_MODEL), pl.BlockSpec((1, 8, D_MODEL), lambda b, s: (b, 0, 0)),
                  VMEM_FULL, VMEM_FULL, VMEM_FULL, VMEM_FULL],
        out_specs=[row(D_MODEL), row(D_MODEL), row(D_FF), row(D_FF), row(D_MODEL),
                   pl.BlockSpec((8, 128), lambda b, s: (0, 0)),
                   pl.BlockSpec((8, D_MODEL), lambda b, s: (0, 0)),
                   pl.BlockSpec((1, 8, D_MODEL), lambda b, s: (b, 0, 0))],
        out_shape=[jax.ShapeDtypeStruct((t, D_MODEL), F32), jax.ShapeDtypeStruct((t, D_MODEL), BF16),
                   jax.ShapeDtypeStruct((t, D_FF), BF16), jax.ShapeDtypeStruct((t, D_FF), BF16),
                   jax.ShapeDtypeStruct((t, D_MODEL), BF16), jax.ShapeDtypeStruct((8, 128), F32),
                   jax.ShapeDtypeStruct((8, D_MODEL), F32), jax.ShapeDtypeStruct((nb, 8, D_MODEL), F32)],
        scratch_shapes=[pltpu.VMEM((tm, D_FF), F32)],
        compiler_params=_cp("arbitrary", "arbitrary"),
    )(x1, tgt, mod8, pre2, post2, w_ff1, w_ff2)


_PIECES = ((C_LRU_X, 1024), (C_LRU_G, 1024), (C_Z, 2048), (C_XBC, 4096), (C_GATES, 2048))
_NP = len(_PIECES)


def in_proj_bwd(pieces, ddt, dx1, x2, mod8, pre1, w_main, w_dt, nb, seq, side=None):
    t = x2.shape[0]
    tm = min(512, seq)
    per_seq = seq // tm
    widths = [min(w, 2048) for _, w in _PIECES]
    steps = [(p, q) for p, (_, w) in enumerate(_PIECES) for q in range(w // widths[p])]
    nk = len(steps)

    def piece_spec(p):
        first = min(k for k in range(nk) if steps[k][0] == p)
        nblk = _PIECES[p][1] // widths[p]
        return pl.BlockSpec((tm, widths[p]), lambda b, s, k: (b * per_seq + s, jnp.clip(k - first, 0, nblk - 1)))

    def body(*refs):
        prefs = refs[:_NP]
        ddt_ref, dx1_ref, x_ref, mod_ref, pre_ref, w_ref, wdt_ref, gx_ref, vacc_ref, dmod_ref, acc_ref = refs[_NP:]
        b, s, k = pl.program_id(0), pl.program_id(1), pl.program_id(2)

        @pl.when((b == 0) & (s == 0) & (k == 0))
        def _():
            vacc_ref[...] = jnp.zeros_like(vacc_ref)

        @pl.when((s == 0) & (k == 0))
        def _():
            dmod_ref[...] = jnp.zeros_like(dmod_ref)

        @pl.when(k == 0)
        def _():
            acc_ref[...] = _dot_nt(ddt_ref[...], wdt_ref[...])

        for kk, (p, q) in enumerate(steps):
            @pl.when(k == kk)
            def _(p=p, q=q):
                c0 = _PIECES[p][0] + q * widths[p]
                acc_ref[...] += _dot_nt(prefs[p][...], w_ref[:, c0:c0 + widths[p]])

        @pl.when(k == nk - 1)
        def _():
            dh = acc_ref[...]
            m = mod_ref[0]
            pre = pre_ref[...]
            xv = x_ref[...]
            rs = lax.rsqrt(_rowmean(xv * xv) + EPS)
            n = xv * rs
            dy = dh * (1.0 + m[1:2, :])
            dn = dy * pre
            gx_ref[...] = dx1_ref[...] + rs * (dn - n * _rowmean(dn * n))
            vacc_ref[...] += jnp.concatenate([_colsum(dy * n), jnp.zeros((7, D_MODEL), F32)], axis=0)
            dmod_ref[0] += jnp.concatenate([_colsum(dh), _colsum(dh * (n * pre)), jnp.zeros((6, D_MODEL), F32)], axis=0)

    row = lambda w: pl.BlockSpec((tm, w), lambda b, s, k: (b * per_seq + s, 0))
    return _call(
        body, name="in_proj_bwd", grid=(nb, per_seq, nk), side=side, sem=("arbitrary", "arbitrary", "arbitrary"),
        args=(*pieces, ddt, dx1, x2, mod8, pre1, w_main, w_dt),
        in_specs=[piece_spec(p) for p in range(_NP)] + [
            row(128), row(D_MODEL), row(D_MODEL), pl.BlockSpec((1, 8, D_MODEL), lambda b, s, k: (b, 0, 0)),
            pl.BlockSpec((1, D_MODEL), lambda b, s, k: (0, 0)),
            VMEM_FULL,
            pl.BlockSpec((D_MODEL, 128), lambda b, s, k: (0, 0))],
        out_specs=[row(D_MODEL), pl.BlockSpec((8, D_MODEL), lambda b, s, k: (0, 0)),
                   pl.BlockSpec((1, 8, D_MODEL), lambda b, s, k: (b, 0, 0))],
        out_shape=[jax.ShapeDtypeStruct((t, D_MODEL), F32), jax.ShapeDtypeStruct((8, D_MODEL), F32),
                   jax.ShapeDtypeStruct((nb, 8, D_MODEL), F32)],
        scratch_shapes=[pltpu.VMEM((tm, D_MODEL), F32)])


def in_proj_wgrad(h1t, proj, dxa, dxbc, dsilu, dlg, dz, dgates, ddt, cw, seq, side=None):
    nt, _, tt = h1t.shape
    t = nt * tt
    tn = 1024
    nn = PROJ_MAIN // tn
    ns = seq // tt
    nh = t // HALO
    j_g, j_z, j_x, j_gt = C_LRU_G // tn, C_Z // tn, C_XBC // tn, C_GATES // tn
    n_x = 2 * SSD_INNER // tn
    strip = 256
    ne = tt + HALO

    def body(h_ref, cur_ref, prev_ref, next_ref, dxa_ref, dxan_ref, dxb_ref, dxbn_ref, ds_ref, dsn_ref, dlg_ref, dz_ref,
             dgt_ref, ddt_ref, cw_ref, dw_ref, dwdt_ref, dlx_ref, dxr_ref, accl_ref, accs_ref, acc_ref, accdt_ref):
        n, k = pl.program_id(0), pl.program_id(1)
        hv = h_ref[k]
        is_x = (n >= j_x) & (n < j_x + n_x)

        @pl.when(k == 0)
        def _():
            acc_ref[...] = jnp.zeros_like(acc_ref)

        @pl.when((n == 0) & (k == 0))
        def _():
            accdt_ref[...] = jnp.zeros_like(accdt_ref)
            accl_ref[...] = jnp.zeros_like(accl_ref)

        @pl.when(is_x & (k == 0))
        def _():
            accs_ref[...] = jnp.zeros_like(accs_ref)

        def conv_tile(do_ref, don_ref, out_ref, cacc_ref, act):
            first = lax.rem(k, ns) == 0
            last = lax.rem(k, ns) == ns - 1
            for c0 in range(0, tn, strip):
                cs = slice(c0, c0 + strip)
                xx = jnp.concatenate([jnp.where(first, 0.0, prev_ref[:, cs].astype(F32)), cur_ref[:, cs].astype(F32),
                                      next_ref[:, cs].astype(F32)], axis=0)
                do_ext = jnp.concatenate([do_ref[:, cs].astype(F32),
                                          jnp.where(last, 0.0, don_ref[:, cs].astype(F32))], axis=0)
                w = cw_ref[:, cs]
                xs = [xx[HALO:HALO + ne, :]] + [pltpu.roll(xx, d, axis=0)[HALO:HALO + ne, :] for d in (1, 2, 3)]
                if act:
                    dc = do_ext * jnp.concatenate([ds_ref[:, cs].astype(F32), dsn_ref[:, cs].astype(F32)], axis=0)
                else:
                    dc = do_ext
                dx = dc[:tt, :] * w[3:4, :]
                for d in (1, 2, 3):
                    dx = dx + pltpu.roll(dc, ne - d, axis=0)[:tt, :] * w[3 - d:4 - d, :]
                dxb = dx.astype(BF16)
                out_ref[:, cs] = dxb
                acc_ref[:, cs] += _dot(hv, dxb)
                dcc = dc[:tt, :]
                rows = [_colsum(dcc * xs[3 - r][:tt, :]) for r in range(4)] + [_colsum(dcc)]
                cacc_ref[:, cs] += jnp.concatenate(rows + [jnp.zeros((3, strip), F32)], axis=0)

        @pl.when(n == 0)
        def _():
            conv_tile(dxa_ref, dxan_ref, dlx_ref, accl_ref, False)
            accdt_ref[...] += _dot(hv, ddt_ref[...])

        @pl.when(is_x)
        def _():
            conv_tile(dxb_ref, dxbn_ref, dxr_ref, accs_ref, True)

        @pl.when(n == j_g)
        def _():
            acc_ref[...] += _dot(hv, dlg_ref[...])

        @pl.when((n >= j_z) & (n < j_x))
        def _():
            acc_ref[...] += _dot(hv, dz_ref[...])

        @pl.when(n >= j_gt)
        def _():
            acc_ref[...] += _dot(hv, dgt_ref[...])

        @pl.when(k == nt - 1)
        def _():
            dw_ref[...] = acc_ref[...].astype(BF16)

        @pl.when((n == 0) & (k == nt - 1))
        def _():
            dwdt_ref[...] = accdt_ref[...].astype(BF16)

    conv_n = lambda n: (n == 0) | ((n >= j_x) & (n < j_x + n_x))
    src_col = lambda n: jnp.where(n == 0, 0, jnp.clip(n, j_x, j_x + n_x - 1))
    ctile = lambda n: jnp.where(n == 0, 0, jnp.clip(n - j_x + 1, 1, n_x))
    xcol = lambda n: jnp.clip(n - j_x, 0, n_x - 1)
    on = lambda cond, k: jnp.where(cond, k, 0)
    nxt = lambda k: jnp.minimum(((k + 1) * tt) // HALO, nh - 1)
    after = lambda cond_during, cond_after, k: jnp.where(cond_during, k, jnp.where(cond_after, nt - 1, 0))
    in_specs = [
        VMEM_FULL,
        pl.BlockSpec((tt, tn), lambda n, k: (on(conv_n(n), k), src_col(n))),
        pl.BlockSpec((HALO, tn), lambda n, k: (on(conv_n(n), jnp.maximum((k * tt) // HALO - 1, 0)), src_col(n))),
        pl.BlockSpec((HALO, tn), lambda n, k: (on(conv_n(n), nxt(k)), src_col(n))),
        pl.BlockSpec((tt, tn), lambda n, k: (on(n == 0, k), 0)),
        pl.BlockSpec((HALO, tn), lambda n, k: (on(n == 0, nxt(k)), 0)),
        pl.BlockSpec((tt, tn), lambda n, k: (on((n >= j_x) & (n < j_x + n_x), k), xcol(n))),
        pl.BlockSpec((HALO, tn), lambda n, k: (on((n >= j_x) & (n < j_x + n_x), nxt(k)), xcol(n))),
        pl.BlockSpec((tt, tn), lambda n, k: (on((n >= j_x) & (n < j_x + n_x), k), xcol(n))),
        pl.BlockSpec((HALO, tn), lambda n, k: (on((n >= j_x) & (n < j_x + n_x), nxt(k)), xcol(n))),
        pl.BlockSpec((tt, tn), lambda n, k: (on(n == j_g, k), 0)),
        pl.BlockSpec((tt, tn), lambda n, k: (on((n >= j_z) & (n < j_x), k), jnp.clip(n - j_z, 0, j_x - j_z - 1))),
        pl.BlockSpec((tt, tn), lambda n, k: (on(n >= j_gt, k), jnp.clip(n - j_gt, 0, nn - j_gt - 1))),
        pl.BlockSpec((tt, 128), lambda n, k: (on(n == 0, k), 0)),
        pl.BlockSpec((4, tn), lambda n, k: (0, ctile(n)))]
    out_specs = [
        pl.BlockSpec((D_MODEL, tn), lambda n, k: (0, n)),
        pl.BlockSpec((D_MODEL, 128), lambda n, k: (0, 0)),
        pl.BlockSpec((tt, tn), lambda n, k: (after(n == 0, n > 0, k), 0)),
        pl.BlockSpec((tt, tn), lambda n, k: (after((n >= j_x) & (n < j_x + n_x), n >= j_x + n_x, k), xcol(n))),
        pl.BlockSpec((8, tn), lambda n, k: (0, 0)),
        pl.BlockSpec((8, tn), lambda n, k: (0, xcol(n)))]
    return _call(
        body, name="in_proj_wgrad", grid=(nn, nt), side=side, sem=("arbitrary", "arbitrary"),
        args=(h1t, proj, proj, proj, dxa, dxa, dxbc, dxbc, dsilu, dsilu, dlg, dz, dgates, ddt, cw),
        in_specs=in_specs, out_specs=out_specs,
        out_shape=[jax.ShapeDtypeStruct((D_MODEL, PROJ_MAIN), BF16), jax.ShapeDtypeStruct((D_MODEL, 128), BF16),
                   jax.ShapeDtypeStruct((t, D_MODEL), BF16), jax.ShapeDtypeStruct((t, 2 * SSD_INNER), BF16),
                   jax.ShapeDtypeStruct((8, D_MODEL), F32), jax.ShapeDtypeStruct((8, 2 * SSD_INNER), F32)],
        scratch_shapes=[pltpu.VMEM((D_MODEL, tn), F32), pltpu.VMEM((D_MODEL, 128), F32)])


def _log1p(u):
    w = 1.0 + u
    return jnp.log(w) - ((w - 1.0) - u) / w


def _softplus(x):
    return jnp.maximum(x, 0.0) + _log1p(jnp.exp(-jnp.abs(x)))


def _head_mask(h):
    lane = lax.broadcasted_iota(jnp.int32, (1, SSD_GW), 1)
    return (lane >= SSD_P * h) & (lane < SSD_P * (h + 1))


def _pair(p):
    return slice(2 * SSD_P * p, 2 * SSD_P * (p + 1))


def _expand4(m, g):
    lane = lax.broadcasted_iota(jnp.int32, (1, SSD_GW), 1)
    col = lambda h: m[:, 4 * g + h:4 * g + h + 1]
    return jnp.where(lane < SSD_P, col(0), jnp.where(lane < 2 * SSD_P, col(1), jnp.where(lane < 3 * SSD_P, col(2), col(3))))


def _reduce4(v, g):
    lane = lax.broadcasted_iota(jnp.int32, (1, SSD_N), 1)
    out = jnp.zeros((v.shape[0], SSD_N), F32)
    for h in range(4):
        s = jnp.sum(jnp.where(_head_mask(h), v, 0.0), axis=1, keepdims=True)
        out = out + jnp.where(lane == 4 * g + h, s, 0.0)
    return out


def _ssd_heads(dtraw, hp, tri):
    xdt = dtraw + hp[0:1, :]
    dt = _softplus(xdt)
    cs = _dot_hi(tri, dt * hp[1:2, :])
    cs_last = cs[SSD_L - 1:SSD_L, :]
    return dict(xdt=xdt, dt=dt, cs=cs, cs_t=cs.T, e=jnp.exp(cs), w=jnp.exp(cs_last - cs), el=jnp.exp(cs_last))


def _ssd_group(g, hd, xs_b, bm_b, cm_b, d_x, st, paired=False):
    ll = SSD_L
    xs = xs_b.astype(F32)
    cs, cs_t = hd["cs"], hd["cs_t"]
    e_x, w_x, el_x, dt_x = _expand4(hd["e"], g), _expand4(hd["w"], g), _expand4(hd["el"], g), _expand4(hd["dt"], g)
    xd = xs * dt_x
    gcb = _dot_nt(cm_b, bm_b)
    ri = lax.broadcasted_iota(jnp.int32, (ll, ll), 0)
    ci = lax.broadcasted_iota(jnp.int32, (ll, ll), 1)
    dks, ms = [], []
    for h in range(4):
        k = 4 * g + h
        dk = jnp.exp(jnp.where(ri >= ci, cs[:, k:k + 1] - cs_t[k:k + 1, :], -1e30))
        dks.append(dk)
        ms.append((gcb * dk).astype(BF16))
    xdb = xd.astype(BF16)
    if paired:
        first = lax.broadcasted_iota(jnp.int32, (1, 2 * SSD_P), 1) < SSD_P
        ydiag = jnp.concatenate(
            [jnp.where(first, _dot(ms[2 * p], xdb[:, _pair(p)]), _dot(ms[2 * p + 1], xdb[:, _pair(p)]))
             for p in range(2)], axis=1)
    else:
        ydiag = jnp.zeros((ll, SSD_GW), F32)
        for h in range(4):
            ydiag = ydiag + _dot(ms[h], jnp.where(_head_mask(h), xd, 0.0).astype(BF16))
    yoff = _dot(cm_b, st.astype(BF16)) * e_x
    y = ydiag + yoff + d_x * xs
    st_new = st * el_x + _dot(bm_b.astype(F32).T.astype(BF16), (xd * w_x).astype(BF16))
    return dict(xs=xs, e_x=e_x, w_x=w_x, el_x=el_x, dt_x=dt_x, xd=xd, xdb=xdb, gcb=gcb, dks=dks, ms=ms, yoff=yoff, y=y,
                st_new=st_new)


def ssd_consts():
    hh = np.arange(SSD_N)
    tri = (hh[:, None] >= hh[None, :]).astype(np.float32)
    return jnp.asarray(tri), jnp.asarray(tri.T)


def ssd_params(dt_bias, a_log, d_skip, norm_w):
    padh = lambda v: jnp.pad(v.reshape(1, SSD_HEADS), ((0, 0), (0, SSD_N - SSD_HEADS)))
    hp = jnp.concatenate([padh(dt_bias), padh(-jnp.exp(a_log)), jnp.zeros((6, SSD_N), F32)], axis=0)
    lp = jnp.concatenate([norm_w.reshape(1, SSD_INNER), jnp.repeat(d_skip, SSD_P).reshape(1, SSD_INNER),
                          jnp.zeros((6, SSD_INNER), F32)], axis=0)
    return hp, lp


def _b_cols(g):
    return slice(SSD_INNER + g * SSD_N, SSD_INNER + (g + 1) * SSD_N)


def _c_cols(g):
    return slice(SSD_INNER + (SSD_G + g) * SSD_N, SSD_INNER + (SSD_G + g + 1) * SSD_N)


def _ssd_specs(nc, rc, cps=1):
    rows = cps * SSD_L
    return [pl.BlockSpec((rows, 2 * SSD_INNER), lambda b, c: (b * nc + rc(c), 0)),
            pl.BlockSpec((rows, SSD_INNER), lambda b, c: (b * nc + rc(c), C_Z // SSD_INNER)),
            pl.BlockSpec((rows, SSD_N), lambda b, c: (b * nc + rc(c), 0))]


def ssd_fwd(xbc, proj, dtraw, hp, lp, tri, nb, seq):
    t = xbc.shape[0]
    cps = SSD_FWD_CPS
    nc = seq // (cps * SSD_L)

    def body(xbc_ref, z_ref, dt_ref, hp_ref, lp_ref, tri_ref, y_ref, sts_ref, st_scr):
        @pl.when(pl.program_id(1) == 0)
        def _():
            st_scr[...] = jnp.zeros_like(st_scr)

        for cc in range(cps):
            rs = slice(cc * SSD_L, (cc + 1) * SSD_L)
            hd = _ssd_heads(dt_ref[rs, :], hp_ref[...], tri_ref[...])
            for g in range(SSD_G):
                gs = slice(g * SSD_GW, (g + 1) * SSD_GW)
                st = st_scr[g]
                sts_ref[cc, g] = st
                f = _ssd_group(g, hd, xbc_ref[rs, gs], xbc_ref[rs, _b_cols(g)], xbc_ref[rs, _c_cols(g)],
                               lp_ref[1:2, gs], st, paired=True)
                st_scr[g] = f["st_new"]
                zf = z_ref[rs, gs].astype(F32)
                yg = f["y"] * (zf * _sigmoid(zf))
                y_ref[rs, gs] = (yg * lax.rsqrt(_rowmean(yg * yg) + EPS) * lp_ref[0:1, gs]).astype(BF16)

    return pl.pallas_call(
        body, name="ssd_fwd", grid=(nb, nc),
        in_specs=_ssd_specs(nc, lambda c: c, cps) + [VMEM_FULL, VMEM_FULL, VMEM_FULL],
        out_specs=[pl.BlockSpec((cps * SSD_L, SSD_INNER), lambda b, c: (b * nc + c, 0)),
                   pl.BlockSpec((cps, SSD_G, SSD_N, SSD_GW), lambda b, c: (b * nc + c, 0, 0, 0))],
        out_shape=[jax.ShapeDtypeStruct((t, SSD_INNER), BF16),
                   jax.ShapeDtypeStruct((nb * nc * cps, SSD_G, SSD_N, SSD_GW), F32)],
        scratch_shapes=[pltpu.VMEM((SSD_G, SSD_N, SSD_GW), F32)],
        compiler_params=_cp("arbitrary", "arbitrary"),
    )(xbc, proj, dtraw, hp, lp, tri)


def ssd_bwd(xbc, proj, dtraw, hp, lp, tri, triu, states, dyn, nb, seq, side=None):
    t = xbc.shape[0]
    nc = seq // SSD_L
    ll = SSD_L

    def body(xbc_ref, z_ref, dt_ref, sts_ref, dy_ref, hp_ref, lp_ref, tri_ref, triu_ref,
             dxbc_ref, dz_ref, ddt_ref, hpg_ref, lpg_ref, dst_scr):
        b, c_i = pl.program_id(0), pl.program_id(1)

        @pl.when((b == 0) & (c_i == 0))
        def _():
            hpg_ref[...] = jnp.zeros_like(hpg_ref)
            lpg_ref[...] = jnp.zeros_like(lpg_ref)

        @pl.when(c_i == 0)
        def _():
            dst_scr[...] = jnp.zeros_like(dst_scr)

        hp = hp_ref[...]
        hd = _ssd_heads(dt_ref[...], hp, tri_ref[...])
        lane = lax.broadcasted_iota(jnp.int32, (1, SSD_N), 1)
        subl = lax.broadcasted_iota(jnp.int32, (SSD_N, 1), 0)
        dcs = jnp.zeros((ll, SSD_N), F32)
        dcs_t = jnp.zeros((SSD_N, ll), F32)
        last = jnp.zeros((1, SSD_N), F32)
        dxx = jnp.zeros((ll, SSD_N), F32)
        for g in range(SSD_G):
            gs = slice(g * SSD_GW, (g + 1) * SSD_GW)
            st = sts_ref[0, g]
            dst = dst_scr[g]
            bm_b, cm_b = xbc_ref[:, _b_cols(g)], xbc_ref[:, _c_cols(g)]
            d_x = lp_ref[1:2, gs]
            f = _ssd_group(g, hd, xbc_ref[:, gs], bm_b, cm_b, d_x, st)
            xs, xd, gcb = f["xs"], f["xd"], f["gcb"]
            e_x, w_x, el_x, dt_x = f["e_x"], f["w_x"], f["el_x"], f["dt_x"]
            stb, dstb = st.astype(BF16), dst.astype(BF16)
            zf = z_ref[:, gs].astype(F32)
            sg = _sigmoid(zf)
            sz = zf * sg
            yv = f["y"]
            yg = yv * sz
            rstd = lax.rsqrt(_rowmean(yg * yg) + EPS)
            n = yg * rstd
            dyn_v = dy_ref[:, gs].astype(F32)
            dn = dyn_v * lp_ref[0:1, gs]
            dyg = rstd * (dn - n * _rowmean(dn * n))
            dy = dyg * sz
            dz_ref[:, gs] = (dyg * yv * (sg * (1.0 + zf * (1.0 - sg)))).astype(BF16)
            dyb = dy.astype(BF16)
            r_ = _dot(bm_b, dstb)
            dxd = w_x * r_
            dqb = (dy * e_x).astype(BF16)
            dcm = _dot_nt(dqb, stb)
            dst_scr[g] = dst * el_x + _dot_tn(cm_b, dqb)
            dbm = _dot_nt((xd * w_x).astype(BF16), dstb)
            xdb = f["xdb"]
            dgm = jnp.zeros((ll, ll), F32)
            for h in range(4):
                k = 4 * g + h
                hm = _head_mask(h)
                dxd = dxd + jnp.where(hm, _dot_tn(f["ms"][h], dyb), 0.0)
                dm = _dot_nt(jnp.where(hm, dy, 0.0).astype(BF16), xdb) * f["dks"][h]
                dgm = dgm + dm
                dseg = dm * gcb
                dcs = dcs + jnp.where(lane == k, jnp.sum(dseg, axis=1, keepdims=True), 0.0)
                dcs_t = dcs_t + jnp.where(subl == k, jnp.sum(dseg, axis=0, keepdims=True), 0.0)
            dgmb = dgm.astype(BF16)
            dxbc_ref[:, _c_cols(g)] = (dcm + _dot(dgmb, bm_b)).astype(BF16)
            dxbc_ref[:, _b_cols(g)] = (dbm + _dot_tn(dgmb, cm_b)).astype(BF16)
            v = _reduce4(r_ * xd * w_x, g)
            dcs = dcs + _reduce4(dy * f["yoff"], g) - v
            last = last + _colsum(v) + _reduce4(_colsum(dst * st) * el_x, g)
            dxx = dxx + _reduce4(dxd * xs, g)
            dxbc_ref[:, gs] = (d_x * dy + dxd * dt_x).astype(BF16)
            lpg_ref[0:1, gs] += _colsum(dyn_v * n)
            lpg_ref[1:2, gs] += _colsum(dy * xs)
        rowi = lax.broadcasted_iota(jnp.int32, (ll, 1), 0)
        da = _dot_hi(triu_ref[...], dcs - dcs_t.T + jnp.where(rowi == ll - 1, last, 0.0))
        ddt = (dxx + da * hp[1:2, :]) * _sigmoid(hd["xdt"])
        ddt_ref[...] = ddt
        hpg_ref[...] += jnp.concatenate([_colsum(ddt), _colsum(da * hd["dt"]), jnp.zeros((6, SSD_N), F32)], axis=0)

    rc = lambda c: nc - 1 - c
    return _call(
        body, name="ssd_bwd", grid=(nb, nc), side=side, sem=("arbitrary", "arbitrary"),
        args=(xbc, proj, dtraw, states, dyn, hp, lp, tri, triu),
        in_specs=_ssd_specs(nc, rc) + [
            pl.BlockSpec((1, SSD_G, SSD_N, SSD_GW), lambda b, c: (b * nc + rc(c), 0, 0, 0)),
            pl.BlockSpec((SSD_L, SSD_INNER), lambda b, c: (b * nc + rc(c), 0)),
            VMEM_FULL, VMEM_FULL, VMEM_FULL, VMEM_FULL],
        out_specs=[pl.BlockSpec((SSD_L, 2 * SSD_INNER), lambda b, c: (b * nc + rc(c), 0)),
                   pl.BlockSpec((SSD_L, SSD_INNER), lambda b, c: (b * nc + rc(c), 0)),
                   pl.BlockSpec((SSD_L, SSD_N), lambda b, c: (b * nc + rc(c), 0)),
                   pl.BlockSpec((8, SSD_N), lambda b, c: (0, 0)),
                   pl.BlockSpec((8, SSD_INNER), lambda b, c: (0, 0))],
        out_shape=[jax.ShapeDtypeStruct((t, 2 * SSD_INNER), BF16), jax.ShapeDtypeStruct((t, SSD_INNER), BF16),
                   jax.ShapeDtypeStruct((t, SSD_N), F32), jax.ShapeDtypeStruct((8, SSD_N), F32),
                   jax.ShapeDtypeStruct((8, SSD_INNER), F32)],
        scratch_shapes=[pltpu.VMEM((SSD_G, SSD_N, SSD_GW), F32)])


def ada_fwd(c_all, w_cols, b_cols):
    def body(c_ref, w_ref, b_ref, o_ref):
        cv = c_ref[...]
        o_ref[...] = _dot_hi(cv * _sigmoid(cv), w_ref[...]) + b_ref[...]

    return pl.pallas_call(body, name="ada_fwd", out_shape=jax.ShapeDtypeStruct((c_all.shape[0], w_cols.shape[1]), F32),
                          compiler_params=pltpu.CompilerParams(vmem_limit_bytes=VMEM_LIMIT))(c_all, w_cols, b_cols)


def ada_bwd(c_all, dmod_cols, dmod_all):
    def body(c_ref, dc_ref, da_ref, gw_ref, gb_ref):
        cv = c_ref[...]
        gw_ref[...] = lax.dot_general(cv * _sigmoid(cv), dc_ref[...], (((0,), (0,)), ((), ())),
                                      precision=lax.Precision.HIGHEST, preferred_element_type=F32)
        gb_ref[...] = _colsum(da_ref[...])

    return pl.pallas_call(
        body, name="ada_bwd",
        out_shape=[jax.ShapeDtypeStruct((c_all.shape[1], dmod_cols.shape[1]), F32),
                   jax.ShapeDtypeStruct((1, dmod_all.shape[1]), F32)],
        compiler_params=pltpu.CompilerParams(vmem_limit_bytes=VMEM_LIMIT))(c_all, dmod_cols, dmod_all)


def _adam_update(g, w, m, v):
    m2 = ADAM_B1 * m + (1.0 - ADAM_B1) * g
    v2 = ADAM_B2 * v + (1.0 - ADAM_B2) * (g * g)
    m_hat = m2 / (1.0 - ADAM_B1 ** ADAM_STEP)
    v_hat = v2 / (1.0 - ADAM_B2 ** ADAM_STEP)
    return -ADAM_LR * (m_hat / (jnp.sqrt(v_hat) + ADAM_EPS) + ADAM_WD * w), m2, v2


def adamw(parts, w, m, v, name):
    n, r, c = parts.shape
    tr = r if r <= 256 else 128

    def body(p_ref, w_ref, m_ref, v_ref, g_ref, d_ref, nm_ref, nv_ref):
        g = p_ref[0].astype(F32)
        for s in range(1, n):
            g = g + p_ref[s].astype(F32)
        g_ref[0] = g
        d_ref[0], nm_ref[0], nv_ref[0] = _adam_update(g, w_ref[0], m_ref[0], v_ref[0])

    blk = pl.BlockSpec((1, tr, c), lambda i: (0, i, 0))
    return pl.pallas_call(
        body, name=name, grid=(r // tr,),
        in_specs=[pl.BlockSpec((n, tr, c), lambda i: (0, i, 0)), blk, blk, blk], out_specs=[blk] * 4,
        out_shape=[jax.ShapeDtypeStruct((1, r, c), F32)] * 4,
        compiler_params=_cp("parallel"),
    )(parts, w, m, v)


SMALL_SRC = {
    'pre_norm1': ('vin', 0, 1024), 'post_norm1': ('vmg', 1, 1024), 'b_gate': ('vmg', 0, 2048),
    'lru_conv_b': ('accl', 4, 1024), 'lru_wa': ('gwa', None, None), 'lru_ba': ('dvec', 0, 1024),
    'lru_wx': ('gwx', None, None), 'lru_bx': ('dvec', 1, 1024), 'lru_lambda': ('dvec', 2, 1024),
    'ssd_conv_b': ('accs', 4, 4096), 'ssd_dt_bias': ('hpg', 0, SSD_HEADS), 'ssd_a_log': ('hpg', 1, SSD_HEADS),
    'ssd_d': ('lpg', 1, SSD_INNER), 'ssd_norm_w': ('lpg', 0, SSD_INNER), 'pre_norm2': ('vmlp', 0, 1024),
    'post_norm2': ('vmlp', 1, 1024)}
SMALL_ACCS = ('vin', 'vmg', 'vmlp', 'dvec', 'accl', 'accs', 'hpg', 'lpg', 'gwa', 'gwx')
SMALL_RIDE = ('vmg', 'vmlp', 'dvec', 'hpg', 'lpg', 'gwa', 'gwx')


def adamw_small(gathered, params):
    names = tuple(params)
    na = len(SMALL_ACCS)

    def body(*refs):
        acc = {k: functools.reduce(lambda p, q: p + q, [refs[i][s] for s in range(NDEV)])
               for i, k in enumerate(SMALL_ACCS)}
        ins = refs[na:na + 3 * len(names)]
        outs = refs[na + 3 * len(names):]
        for j, k in enumerate(names):
            w_ref, m_ref, v_ref = ins[3 * j:3 * j + 3]
            src, row, width = SMALL_SRC[k]
            wv = w_ref[...]
            if row is None:
                g = acc[src]
            elif k == 'ssd_d':
                li = lax.broadcasted_iota(jnp.int32, (SSD_INNER, SSD_N), 0)
                hi = lax.broadcasted_iota(jnp.int32, (SSD_INNER, SSD_N), 1)
                g = _dot_hi(acc[src], jnp.where(jnp.right_shift(li, 6) == hi, 1.0, 0.0))[row:row + 1, :SSD_HEADS]
            else:
                g = acc[src][row:row + 1, :width]
            if k == 'lru_lambda':
                g = g * (-1.0 / (1.0 + jnp.exp(wv)))
            if k == 'ssd_a_log':
                g = g * (-jnp.exp(wv))
            o = outs[4 * j:4 * j + 4]
            o[0][...] = g
            o[1][...], o[2][...], o[3][...] = _adam_update(g, wv, m_ref[...], v_ref[...])
        outs[-2][...] = acc['accl'][0:4, :]
        outs[-1][...] = acc['accs'][0:4, :]

    flat = [a for k in names for a in params[k]]
    out_shape = [jax.ShapeDtypeStruct(params[k][0].shape, F32) for k in names for _ in range(4)]
    out_shape += [jax.ShapeDtypeStruct((4, D_MODEL), F32), jax.ShapeDtypeStruct((4, 2 * SSD_INNER), F32)]
    res = pl.pallas_call(body, name="adamw_small", out_shape=out_shape,
                         compiler_params=pltpu.CompilerParams(vmem_limit_bytes=VMEM_LIMIT))(
        *[gathered[k] for k in SMALL_ACCS], *flat)
    return {k: res[4 * j:4 * j + 4] for j, k in enumerate(names)}, res[-2], res[-1]


def _dev_index(px, py, pc):
    return 4 * px + 2 * py + pc


class _Exchange:
    def __init__(self, arrs):
        self.arrs = list(arrs)
        self.na = len(self.arrs)
        self.scratch = [pltpu.SemaphoreType.DMA((7 * self.na,)), pltpu.SemaphoreType.DMA((7 * self.na,)),
                        pltpu.SemaphoreType.DMA((self.na,))]


class Gather(_Exchange):
    def __init__(self, arrs):
        super().__init__(arrs)
        self.out_shape = [jax.ShapeDtypeStruct((NDEV,) + a.shape, a.dtype) for a in self.arrs]

    def _plan(self, ins, outs, sems):
        na = self.na
        send_sems, recv_sems, local_sems = sems
        x, y, c = lax.axis_index("x"), lax.axis_index("y"), lax.axis_index("c")
        me, sibling = (x, y, c), (x, y, 1 - c)
        chips = [(1 - x, y), (x, 1 - y), (1 - x, 1 - y)]

        def copy(a, k, block, to, src=None):
            dst = outs[a].at[_dev_index(*block)]
            return pltpu.make_async_remote_copy(
                src_ref=dst if src is None else src, dst_ref=dst, send_sem=send_sems.at[a * 7 + k],
                recv_sem=recv_sems.at[a * 7 + k], device_id=to, device_id_type=MESH)

        mine = [pltpu.make_async_copy(ins[a], outs[a].at[_dev_index(*me)], local_sems.at[a]) for a in range(na)]
        first = []
        for a in range(na):
            first.append(copy(a, 0, me, sibling, src=ins[a]))
            first += [copy(a, 1 + j, me, (*chip, c), src=ins[a]) for j, chip in enumerate(chips)]
        return copy, mine, first, me, sibling, chips, c

    def start(self, ins, outs, sems):
        _, mine, first, *_ = self._plan(ins, outs, sems)
        for cp in mine + first:
            cp.start()

    def finish(self, ins, outs, sems):
        copy, mine, first, me, sibling, chips, c = self._plan(ins, outs, sems)
        passed = []
        for j, chip in enumerate(chips):
            for a in range(self.na):
                copy(a, 1 + j, (*chip, c), me).wait_recv()
                cp = copy(a, 4 + j, (*chip, c), sibling)
                cp.start()
                passed.append(cp)
        for a in range(self.na):
            copy(a, 0, sibling, me).wait_recv()
            for j, chip in enumerate(chips):
                copy(a, 4 + j, (*chip, 1 - c), me).wait_recv()
        for cp in first + passed:
            cp.wait_send()
        for cp in mine:
            cp.wait()


class GatherRelay(Gather):
    def _plan(self, ins, outs, sems):
        na = self.na
        send_sems, recv_sems, local_sems = sems
        x, y, c = lax.axis_index("x"), lax.axis_index("y"), lax.axis_index("c")
        me, sibling = (x, y, c), (x, y, 1 - c)
        xn, yn, dg = (1 - x, y), (x, 1 - y), (1 - x, 1 - y)
        south = c == 0
        pick = lambda a, b: tuple(jnp.where(south, p, q) for p, q in zip(a, b))
        relay_to = pick(yn, xn)
        relay_of = pick(xn, yn)

        def copy(a, k, block, to, src=None):
            dst = outs[a].at[_dev_index(*block)]
            return pltpu.make_async_remote_copy(
                src_ref=dst if src is None else src, dst_ref=dst, send_sem=send_sems.at[a * 7 + k],
                recv_sem=recv_sems.at[a * 7 + k], device_id=to, device_id_type=MESH)

        mine = [pltpu.make_async_copy(ins[a], outs[a].at[_dev_index(*me)], local_sems.at[a]) for a in range(na)]
        first = []
        for a in range(na):
            first += [copy(a, 0, me, sibling, src=ins[a]), copy(a, 1, me, (*xn, c), src=ins[a]),
                      copy(a, 2, me, (*yn, c), src=ins[a])]
        return copy, mine, first, me, sibling, (xn, yn, dg), c, relay_to, relay_of

    def start(self, ins, outs, sems):
        _, mine, first, *_ = self._plan(ins, outs, sems)
        for cp in mine + first:
            cp.start()

    def finish(self, ins, outs, sems):
        copy, mine, first, me, sibling, (xn, yn, dg), c, relay_to, relay_of = self._plan(ins, outs, sems)
        later = []
        for a in range(self.na):
            copy(a, 1, (*xn, c), me).wait_recv()
            copy(a, 2, (*yn, c), me).wait_recv()
            later.append(copy(a, 3, (*relay_of, c), (*relay_to, c)))
            later += [copy(a, 4, (*xn, c), sibling), copy(a, 5, (*yn, c), sibling)]
            for cp in later[-3:]:
                cp.start()
        for a in range(self.na):
            copy(a, 3, (*dg, c), me).wait_recv()
            cp = copy(a, 6, (*dg, c), sibling)
            cp.start()
            later.append(cp)
        for a in range(self.na):
            copy(a, 0, sibling, me).wait_recv()
            for k, chip in ((4, xn), (5, yn), (6, dg)):
                copy(a, k, (*chip, 1 - c), me).wait_recv()
        for cp in first + later:
            cp.wait_send()
        for cp in mine:
            cp.wait()


class Scatter(_Exchange):
    def __init__(self, arrs):
        super().__init__(arrs)
        self.out_shape = [jax.ShapeDtypeStruct(a.shape, a.dtype) for a in self.arrs]

    def _plan(self, ins, outs, sems, arrivals):
        send_sems, recv_sems, local_sems = sems
        x, y, c = lax.axis_index("x"), lax.axis_index("y"), lax.axis_index("c")
        me = _dev_index(x, y, c)
        masks = [(mx, my, mc) for mx in (0, 1) for my in (0, 1) for mc in (0, 1)][1:]
        flip = lambda v, bit: 1 - v if bit else v
        mine = [pltpu.make_async_copy(ins[a].at[me], outs[a].at[me], local_sems.at[a]) for a in range(self.na)]
        sends, recvs = [], []
        for k, (mx, my, mc) in enumerate(masks):
            peer = (flip(x, mx), flip(y, my), flip(c, mc))
            pidx = _dev_index(*peer)
            for a in range(self.na):
                on = dict(send_sem=send_sems.at[a * 7 + k], recv_sem=recv_sems.at[a * 7 + k], device_id=peer,
                          device_id_type=MESH)
                sends.append(pltpu.make_async_remote_copy(src_ref=ins[a].at[pidx], dst_ref=outs[a].at[me], **on))
                if arrivals:
                    recvs.append(pltpu.make_async_remote_copy(src_ref=ins[a].at[pidx], dst_ref=outs[a].at[pidx], **on))
        return mine, sends, recvs

    def start(self, ins, outs, sems):
        mine, sends, _ = self._plan(ins, outs, sems, arrivals=False)
        for cp in mine + sends:
            cp.start()

    def finish(self, ins, outs, sems):
        mine, sends, recvs = self._plan(ins, outs, sems, arrivals=True)
        for cp in recvs:
            cp.wait_recv()
        for cp in sends:
            cp.wait_send()
        for cp in mine:
            cp.wait()


def exchange_call(ex, name):
    na = ex.na

    def body(*refs):
        ins, outs, sems = refs[:na], refs[na:2 * na], refs[2 * na:]
        ex.start(ins, outs, sems)
        ex.finish(ins, outs, sems)

    return pl.pallas_call(body, name=name, in_specs=[ANY] * na, out_specs=[ANY] * na, out_shape=ex.out_shape,
                          scratch_shapes=ex.scratch)(*ex.arrs)


def all_gather(arrs, name, relay=False):
    return exchange_call((GatherRelay if relay else Gather)(arrs), name)


def _call(body, *, name, grid, in_specs, out_specs, out_shape, scratch_shapes=(), sem, args, side=None):
    if side is None:
        outs = pl.pallas_call(body, name=name, grid=grid, in_specs=list(in_specs), out_specs=list(out_specs),
                              out_shape=list(out_shape), scratch_shapes=list(scratch_shapes),
                              compiler_params=_cp(*sem))(*args)
        return outs, []
    ni, no, ns, na = len(in_specs), len(out_specs), len(scratch_shapes), side.na

    def wrapped(*refs):
        ins, s_in = refs[:ni], refs[ni:ni + na]
        outs, s_out = refs[ni + na:ni + na + no], refs[ni + na + no:ni + 2 * na + no]
        scr, sems = refs[ni + 2 * na + no:ni + 2 * na + no + ns], refs[ni + 2 * na + no + ns:]
        pids = [pl.program_id(i) for i in range(len(grid))]
        first = functools.reduce(lambda p, q: p & q, [p == 0 for p in pids])
        last = functools.reduce(lambda p, q: p & q, [p == g - 1 for p, g in zip(pids, grid)])

        @pl.when(first)
        def _():
            side.start(s_in, s_out, sems)

        body(*ins, *outs, *scr)

        @pl.when(last)
        def _():
            side.finish(s_in, s_out, sems)

    outs = pl.pallas_call(
        wrapped, name=name, grid=grid, in_specs=list(in_specs) + [ANY] * na, out_specs=list(out_specs) + [ANY] * na,
        out_shape=list(out_shape) + side.out_shape, scratch_shapes=list(scratch_shapes) + side.scratch,
        compiler_params=_cp(*["arbitrary"] * len(grid)))(*args, *side.arrs)
    return outs[:no], outs[no:]


WEIGHTS = ('w_ada', 'b_ada', 'pre_norm1', 'post_norm1', 'w_in', 'b_gate', 'lru_conv_w', 'lru_conv_b', 'lru_wa',
           'lru_ba', 'lru_wx', 'lru_bx', 'lru_lambda', 'w_pa', 'ssd_conv_w', 'ssd_conv_b', 'ssd_dt_bias', 'ssd_a_log',
           'ssd_d', 'ssd_norm_w', 'w_pb', 'w_out', 'pre_norm2', 'post_norm2', 'w_ff1', 'w_ff2')
BIG = ('w_in', 'w_pa', 'w_pb', 'w_out', 'w_ff1', 'w_ff2')
REPL = ('pre_norm1', 'post_norm1', 'b_gate', 'lru_conv_b', 'lru_wa', 'lru_ba', 'lru_wx', 'lru_bx', 'lru_lambda',
        'ssd_conv_b', 'ssd_dt_bias', 'ssd_a_log', 'ssd_d', 'ssd_norm_w', 'pre_norm2', 'post_norm2')
LANES = 1024


def _rows(n):
    return -(-n // LANES)


def _pack(vals, total_rows):
    parts = []
    for v in vals:
        f = v.reshape(-1).astype(F32)
        parts.append(jnp.pad(f, (0, _rows(f.shape[0]) * LANES - f.shape[0])))
    flat = jnp.concatenate(parts)
    return jnp.pad(flat.reshape(-1, LANES), ((0, total_rows - flat.shape[0] // LANES), (0, 0)))


def _unpack(slab, shapes):
    out, r = [], 0
    for s in shapes:
        n = int(np.prod(s))
        out.append(slab[r:r + _rows(n)].reshape(-1)[:n].reshape(s))
        r += _rows(n)
    return out


def _block_diag4(w):
    w4 = w.reshape(4, 4, 64, 64)
    eye = jnp.eye(4, dtype=w.dtype)
    return (w4[:, :, :, None, :] * eye[None, :, None, :, None]).reshape(4, LRU_BLOCK, LRU_BLOCK)


def _diag_blocks4(m):
    m5 = m.reshape(4, 4, 64, 4, 64)
    return jnp.stack([m5[:, a, :, a, :] for a in range(4)], axis=1).reshape(LRU_HEADS, 64, 64)


def kernel(x, c, w_ada, b_ada, pre_norm1, post_norm1, w_in, b_gate, lru_conv_w, lru_conv_b, lru_wa, lru_ba, lru_wx, lru_bx, lru_lambda, w_pa, ssd_conv_w, ssd_conv_b, ssd_dt_bias, ssd_a_log, ssd_d, ssd_norm_w, w_pb, w_out, pre_norm2, post_norm2, w_ff1, w_ff2, loss_target, m_w_ada, m_b_ada, m_pre_norm1, m_post_norm1, m_w_in, m_b_gate, m_lru_conv_w, m_lru_conv_b, m_lru_wa, m_lru_ba, m_lru_wx, m_lru_bx, m_lru_lambda, m_w_pa, m_ssd_conv_w, m_ssd_conv_b, m_ssd_dt_bias, m_ssd_a_log, m_ssd_d, m_ssd_norm_w, m_w_pb, m_w_out, m_pre_norm2, m_post_norm2, m_w_ff1, m_w_ff2, v_w_ada, v_b_ada, v_pre_norm1, v_post_norm1, v_w_in, v_b_gate, v_lru_conv_w, v_lru_conv_b, v_lru_wa, v_lru_ba, v_lru_wx, v_lru_bx, v_lru_lambda, v_w_pa, v_ssd_conv_w, v_ssd_conv_b, v_ssd_dt_bias, v_ssd_a_log, v_ssd_d, v_ssd_norm_w, v_w_pb, v_w_out, v_pre_norm2, v_post_norm2, v_w_ff1, v_w_ff2):
    given = dict(locals())
    w = {k: given[k] for k in WEIGHTS}
    mom = {k: given["m_" + k] for k in WEIGHTS}
    var = {k: given["v_" + k] for k in WEIGHTS}
    nb, seq, _ = x.shape
    assert nb == 2 and seq % 512 == 0, (nb, seq)
    t = nb * seq
    me = _dev_index(lax.axis_index("x"), lax.axis_index("y"), lax.axis_index("c"))
    x2 = x.reshape(t, D_MODEL)
    tgt2 = loss_target.reshape(t, D_MODEL)
    ada_cols = w_ada.shape[2]

    slab = jnp.zeros((16, LANES), F32)
    slab = slab.at[0:nb].set(c)
    slab = slab.at[2:6, 0:lru_conv_w.shape[2]].set(lru_conv_w[0])
    slab = slab.at[6:10, 0:ssd_conv_w.shape[2]].set(ssd_conv_w[0])
    g1, gw_in = all_gather([slab, w_in.astype(BF16)], "gather_cond_w_in", relay=True)
    c_all = g1[:, 0:nb].reshape(NDEV * nb, D_MODEL)
    lru_cw = g1[:, 2:6, 0:lru_conv_w.shape[2]].transpose(1, 0, 2).reshape(4, D_MODEL)
    ssd_cw = g1[:, 6:10, 0:ssd_conv_w.shape[2]].transpose(1, 0, 2).reshape(4, 2 * SSD_INNER)
    b_cols = lax.dynamic_slice(b_ada, (0, me * ada_cols), (1, ada_cols))
    mod_cols = ada_fwd(c_all, w_ada[0], b_cols)
    (g2,) = all_gather([mod_cols], "gather_mod")
    mod_all = g2.transpose(1, 0, 2).reshape(NDEV * nb, N_MOD * D_MODEL)
    mod_mine = lax.dynamic_slice(mod_all, (me * nb, 0), (nb, N_MOD * D_MODEL)).reshape(nb, N_MOD, D_MODEL)
    mod8 = jnp.pad(mod_mine, ((0, 0), (0, 8 - N_MOD), (0, 0)))

    shard = IN_DIM // NDEV
    kd, od = DT_COL0 // shard, DT_COL0 % shard
    assert od + SSD_HEADS <= shard
    gb = gw_in[:, 0]
    w_main = jnp.concatenate([gb[k] for k in range(kd)] + [gb[kd][:, :od], gb[kd][:, od + SSD_HEADS:]]
                             + [gb[k] for k in range(kd + 1, NDEV)], axis=1)
    w_dt = jnp.pad(gb[kd][:, od:od + SSD_HEADS], ((0, 0), (0, 128 - SSD_HEADS)))

    wa_bd = _block_diag4(lru_wa[0]).astype(BF16)
    wx_bd = _block_diag4(lru_wx[0]).astype(BF16)
    lam = lru_lambda[0]
    vec = _pack([lru_ba, lru_bx, jax.nn.softplus(-lam)], 8)
    tri, triu = ssd_consts()
    hp, lp = ssd_params(ssd_dt_bias[0], ssd_a_log[0], ssd_d[0], ssd_norm_w[0])

    rest = Gather([w[k].astype(BF16) for k in BIG[1:]])
    cw_all = jnp.concatenate([lru_cw, ssd_cw], axis=1)
    cb_all = jnp.concatenate([lru_conv_b, ssd_conv_b], axis=1)
    (proj, h1t, dtraw, xa, xbc, dsilu), gw = in_proj_fwd(x2, mod8, pre_norm1, w_main, w_dt, cw_all, cb_all, seq, side=rest)
    w_pa_f = gw[0].reshape(D_MODEL, D_MODEL)
    w_pb_f = gw[1].reshape(SSD_INNER, D_MODEL)
    w_out_f = gw[2].reshape(D_MODEL, D_MODEL)
    w_ff1_f = gw[3][:, 0]
    w_ff2_f = gw[4].reshape(D_FF, D_MODEL)
    ya_in, hst, lru_gates = lru_fwd(xa, proj, wa_bd, wx_bd, vec, nb, seq)
    yb_in, states = ssd_fwd(xbc, proj, dtraw, hp, lp, tri, nb, seq)
    yab, out1, x1 = merge_fwd(ya_in, yb_in, proj, x2, mod8, b_gate, post_norm1, w_pa_f, w_pb_f, w_out_f, seq)

    dx1, h2, da1, act, dy2, loss8, vacc_mlp, dmod_mlp = mlp_fwd_bwd(
        x1, tgt2, mod8, pre_norm2, post_norm2, w_ff1_f, w_ff2_f, nb, seq)
    wg = dict(out_dtype=BF16, ta=True, tm=1024, tn=1024, tk=1024)
    dw_ff1 = matmul(h2, da1, name="wgrad_ff1", blocked_out=D_FF // NDEV, **wg)
    dw_ff2 = matmul(act, dy2, name="wgrad_ff2", **wg)
    dya_in, dyb_in, dgates, dyab, dout1, merged, vacc_mg, dmod_mg = merge_bwd(
        dx1, out1, yab, proj, mod8, b_gate, post_norm1, w_pa_f, w_pb_f, w_out_f, nb, seq)
    dw_out = matmul(merged, dout1, name="wgrad_out", **wg)
    dw_pa = matmul(ya_in, dyab, name="wgrad_pa", n=D_MODEL, b_off=0, **wg)
    dw_pb = matmul(yb_in, dyab, name="wgrad_pb", n=D_MODEL, b_off=1, **wg)
    by_rows = lambda g: g.reshape(NDEV, g.shape[0] // NDEV, g.shape[1])
    (dxa, dlg, dwa_bd, dwx_bd, dvec), parts_ff = lru_bwd(
        dya_in, xa, proj, hst, lru_gates, wa_bd, wx_bd, vec, nb, seq, side=Scatter([dw_ff1, by_rows(dw_ff2)]))
    (dxbc, dz, ddt, hpg, lpg), parts_mg = ssd_bwd(xbc, proj, dtraw, hp, lp, tri, triu, states, dyb_in, nb, seq,
                                                  side=Scatter([by_rows(dw_pa), by_rows(dw_pb), by_rows(dw_out)]))
    ddt_b = ddt.astype(BF16)
    accs = dict(vmg=vacc_mg, vmlp=vacc_mlp, dvec=dvec, hpg=hpg, lpg=lpg,
                gwa=_diag_blocks4(dwa_bd).reshape(LRU_HEADS * 64, 64), gwx=_diag_blocks4(dwx_bd).reshape(LRU_HEADS * 64, 64))
    (dw_main, dw_dt, dlx, dxr, acc_l, acc_s), g_small = in_proj_wgrad(
        h1t, proj, dxa, dxbc, dsilu, dlg, dz, dgates, ddt_b, cw_all, seq, side=Gather([accs[k] for k in SMALL_RIDE]))
    pieces = (dlx, dlg, dz, dxr, dgates)
    cut = lambda k: dw_main[:, k * shard - (SSD_HEADS if k > kd else 0):(k + 1) * shard - (SSD_HEADS if k >= kd else 0)]
    blk_dt = jnp.concatenate([dw_main[:, kd * shard:DT_COL0], dw_dt[:, :SSD_HEADS],
                              dw_main[:, DT_COL0:(kd + 1) * shard - SSD_HEADS]], axis=1)
    dw_blocks = jnp.stack([blk_dt if k == kd else cut(k) for k in range(NDEV)])
    (grad_x, vacc_in, dmod_in), parts_in = in_proj_bwd(pieces, ddt_b, dx1, x2, mod8, pre_norm1, w_main, w_dt, nb, seq,
                                                       side=Scatter([dw_blocks]))
    parts = dict(zip(BIG, (parts_in[0], *parts_mg, *parts_ff)))

    dmod = (dmod_in + dmod_mg + dmod_mlp)[:, :N_MOD].reshape(nb, N_MOD * D_MODEL)
    g3, g_vin, g_accl, g_accs = all_gather([jnp.pad(dmod, ((0, 8 - nb), (0, 0))), vacc_in, acc_l, acc_s], "gather_dmod")
    dmod_all = g3[:, :nb].reshape(NDEV * nb, N_MOD * D_MODEL)
    dmod_cols = lax.dynamic_slice(dmod_all, (0, me * ada_cols), (NDEV * nb, ada_cols))
    g_w_ada, g_b_ada = ada_bwd(c_all, dmod_cols, dmod_all)

    res = {}
    for k in BIG:
        res[k] = adamw(parts[k], w[k], mom[k], var[k], "adamw_" + k)
    res['w_ada'] = adamw(g_w_ada[None], w_ada, m_w_ada, v_w_ada, "adamw_w_ada")

    gathered = dict(zip(SMALL_RIDE, g_small), vin=g_vin, accl=g_accl, accs=g_accs)
    view = lambda a: a.reshape(-1, a.shape[-1])
    res_a, g_lru_cw, g_ssd_cw = adamw_small(gathered, {k: (view(w[k]), view(mom[k]), view(var[k])) for k in REPL})
    res.update(res_a)
    lcw, scw = lru_conv_w.shape[2], ssd_conv_w.shape[2]
    sharded = {'b_ada': g_b_ada[None], 'lru_conv_w': lax.dynamic_slice(g_lru_cw, (0, me * lcw), (4, lcw))[None],
               'ssd_conv_w': lax.dynamic_slice(g_ssd_cw, (0, me * scw), (4, scw))[None]}
    for k, g in sharded.items():
        as3 = lambda a: a.reshape(g.shape)
        res[k] = adamw(g, as3(w[k]), as3(mom[k]), as3(var[k]), "adamw_" + k)

    loss = lax.psum(loss8[0, 0], ("x", "y", "c"))
    outs = [[res[k][j].reshape(w[k].shape) for k in WEIGHTS] for j in range(4)]
    return (loss, grad_x.reshape(x.shape), *outs[0], *outs[1], *outs[2], *outs[3])
```

```python
import functools

import numpy as np
import jax
import jax.numpy as jnp
from jax import lax
from jax.experimental import pallas as pl
from jax.experimental.pallas import tpu as pltpu

F32 = jnp.float32
BF16 = jnp.bfloat16

D_MODEL = 1024
LRU_HEADS = 16
LRU_BLOCK = 256
LRU_C = 8.0
SSD_INNER = 2048
SSD_HEADS = 32
SSD_P = 64
SSD_G = 8
SSD_N = 128
SSD_L = 128
SSD_GW = SSD_INNER // SSD_G
D_FF = 4096
N_MOD = 6
EPS = 1e-6
NDEV = 8

C_LRU_X, C_LRU_G, C_Z, C_XBC, C_GATES, PROJ_MAIN = 0, 1024, 2048, 4096, 8192, 10240
IN_DIM = 10272
DT_COL0 = 8192
HALO = 16
SSD_FWD_CPS = 1
HT_TOK = 512

ADAM_LR, ADAM_B1, ADAM_B2, ADAM_EPS, ADAM_WD, ADAM_STEP = 0.001, 0.9, 0.999, 1e-08, 0.01, 10

VMEM_LIMIT = 60 * 1024 * 1024
MESH = pl.DeviceIdType.MESH
ANY = pl.BlockSpec(memory_space=pl.ANY)
VMEM_FULL = pl.BlockSpec(memory_space=pltpu.VMEM)


def _cp(*sem):
    return pltpu.CompilerParams(dimension_semantics=sem, vmem_limit_bytes=VMEM_LIMIT)


def _dot(a, b):
    return jnp.dot(a, b, preferred_element_type=F32)


def _dot_nt(a, b):
    return lax.dot_general(a, b, (((1,), (1,)), ((), ())), preferred_element_type=F32)


def _dot_tn(a, b):
    return lax.dot_general(a, b, (((0,), (0,)), ((), ())), preferred_element_type=F32)


def _dot_hi(a, b):
    return jnp.dot(a, b, precision=lax.Precision.HIGHEST, preferred_element_type=F32)


def _sigmoid(x):
    return 1.0 / (1.0 + jnp.exp(-x))


def _gelu_and_grad(x):
    k0, k1 = 0.7978845608028654, 0.044715
    t = jnp.tanh(k0 * (x + k1 * x * x * x))
    g = 0.5 * x * (1.0 + t)
    dg = 0.5 * (1.0 + t) + 0.5 * x * (1.0 - t * t) * k0 * (1.0 + 3.0 * k1 * x * x)
    return g, dg


def _neg_expm1(y):
    p = 1.0 + y * (1.0 / 7.0)
    p = 1.0 + y * (1.0 / 6.0) * p
    p = 1.0 + y * (1.0 / 5.0) * p
    p = 1.0 + y * (1.0 / 4.0) * p
    p = 1.0 + y * (1.0 / 3.0) * p
    p = 1.0 + y * 0.5 * p
    return jnp.where(y > -0.3, -y * p, 1.0 - jnp.exp(y))


def _colsum(v):
    return jnp.sum(v, axis=0, keepdims=True)


def _rowmean(v):
    return jnp.mean(v, axis=-1, keepdims=True)


def matmul(a, b, *, ta=False, tb=False, out_dtype=F32, tm, tn, tk, name, n=None, b_off=0, blocked_out=False):
    m = a.shape[1] if ta else a.shape[0]
    kdim = a.shape[0] if ta else a.shape[1]
    n = n or (b.shape[0] if tb else b.shape[1])
    tm, tn, tk = min(tm, m), min(tn, n), min(tk, kdim)
    nk = kdim // tk
    dn = (((0 if ta else 1,), (1 if tb else 0,)), ((), ()))
    bw = blocked_out or tn

    def body(a_ref, b_ref, o_ref, acc_ref):
        k = pl.program_id(2)
        p = lax.dot_general(a_ref[...], b_ref[...], dn, preferred_element_type=F32)

        def emit(v):
            if blocked_out:
                for q in range(tn // bw):
                    o_ref[q] = v[:, q * bw:(q + 1) * bw].astype(out_dtype)
            else:
                o_ref[...] = v.astype(out_dtype)

        if nk == 1:
            emit(p)
        else:
            @pl.when(k == 0)
            def _():
                acc_ref[...] = p

            @pl.when(k > 0)
            def _():
                acc_ref[...] += p

            @pl.when(k == nk - 1)
            def _():
                emit(acc_ref[...])

    a_spec = pl.BlockSpec((tk, tm), lambda i, j, k: (k, i)) if ta else pl.BlockSpec((tm, tk), lambda i, j, k: (i, k))
    b_spec = (pl.BlockSpec((tn, tk), lambda i, j, k: (j, k)) if tb
              else pl.BlockSpec((tk, tn), lambda i, j, k: (k, j + b_off)))
    if blocked_out:
        o_spec, o_shape = pl.BlockSpec((tn // bw, tm, bw), lambda i, j, k: (j, i, 0)), (n // bw, m, bw)
    else:
        o_spec, o_shape = pl.BlockSpec((tm, tn), lambda i, j, k: (i, j)), (m, n)
    return pl.pallas_call(
        body, name=name, grid=(m // tm, n // tn, nk),
        in_specs=[a_spec, b_spec], out_specs=o_spec,
        out_shape=jax.ShapeDtypeStruct(o_shape, out_dtype),
        scratch_shapes=[pltpu.VMEM((tm, tn), F32)],
        compiler_params=_cp("parallel", "parallel", "arbitrary"),
    )(a, b)


def _conv_tile(j, tn):
    return jnp.where(j == 0, 0, jnp.clip(j - C_XBC // tn + 1, 1, 2 * SSD_INNER // tn))


def in_proj_fwd(x2, mod8, pre1, w_main, w_dt, cw, cb, seq, side=None):
    t = x2.shape[0]
    tm = min(1024, seq)
    tn = 1024
    per_seq = seq // tm
    j_xbc = C_XBC // tn
    n_xbc = 2 * SSD_INNER // tn
    cs = 256

    def body(x_ref, mod_ref, pre_ref, w_ref, wdt_ref, cw_ref, cb_ref, proj_ref, h_ref, dt_ref, xa_ref, xbc_ref, ds_ref,
             h_scr, carry_scr):
        i, j = pl.program_id(0), pl.program_id(1)

        @pl.when(j == 0)
        def _():
            xv = x_ref[...]
            y = xv * lax.rsqrt(_rowmean(xv * xv) + EPS) * pre_ref[...]
            m = mod_ref[0]
            hf = y * (1.0 + m[1:2, :]) + m[0:1, :]
            h = hf.astype(BF16)
            h_scr[...] = h
            hft = hf.T.astype(BF16)
            for q in range(tm // HT_TOK):
                h_ref[q] = hft[:, q * HT_TOK:(q + 1) * HT_TOK]
            dt_ref[...] = _dot(h, wdt_ref[...])

        def project(c0=0, width=tn):
            pb = _dot(h_scr[...], w_ref[:, c0:c0 + width]).astype(BF16)
            proj_ref[:, c0:c0 + width] = pb
            return pb

        def conv(o_ref, slot, act):
            first = lax.rem(i, per_seq) == 0
            for c0 in range(0, tn, cs):
                cur = project(c0, cs).astype(F32)
                prev = jnp.where(first, 0.0, carry_scr[slot, :, c0:c0 + cs])
                carry_scr[slot, :, c0:c0 + cs] = cur[tm - HALO:, :]
                xx = jnp.concatenate([prev, cur], axis=0)
                w = cw_ref[:, c0:c0 + cs]
                acc = cur * w[3:4, :] + cb_ref[:, c0:c0 + cs]
                for d in (1, 2, 3):
                    acc = acc + pltpu.roll(xx, d, axis=0)[HALO:, :] * w[3 - d:4 - d, :]
                if act:
                    sg = _sigmoid(acc)
                    ds_ref[:, c0:c0 + cs] = (sg * (1.0 + acc * (1.0 - sg))).astype(BF16)
                    acc = acc * sg
                o_ref[:, c0:c0 + cs] = acc.astype(BF16)

        is_xbc = (j >= j_xbc) & (j < j_xbc + n_xbc)

        @pl.when(j == 0)
        def _():
            conv(xa_ref, 0, False)

        @pl.when(is_xbc)
        def _():
            conv(xbc_ref, j - j_xbc + 1, True)

        @pl.when((j > 0) & jnp.logical_not(is_xbc))
        def _():
            project()

    return _call(
        body, name="in_proj_fwd", grid=(t // tm, PROJ_MAIN // tn), side=side, sem=("arbitrary", "arbitrary"),
        args=(x2, mod8, pre1, w_main, w_dt, cw, cb),
        in_specs=[pl.BlockSpec((tm, D_MODEL), lambda i, j: (i, 0)),
                  pl.BlockSpec((1, 8, D_MODEL), lambda i, j: (i // per_seq, 0, 0)),
                  pl.BlockSpec((1, D_MODEL), lambda i, j: (0, 0)),
                  pl.BlockSpec((D_MODEL, tn), lambda i, j: (0, j)),
                  pl.BlockSpec((D_MODEL, 128), lambda i, j: (0, 0)),
                  pl.BlockSpec((4, tn), lambda i, j: (0, _conv_tile(j, tn))),
                  pl.BlockSpec((1, tn), lambda i, j: (0, _conv_tile(j, tn)))],
        out_specs=[pl.BlockSpec((tm, tn), lambda i, j: (i, j)),
                   pl.BlockSpec((tm // HT_TOK, D_MODEL, HT_TOK), lambda i, j: (i, 0, 0)),
                   pl.BlockSpec((tm, 128), lambda i, j: (i, 0)),
                   pl.BlockSpec((tm, tn), lambda i, j: (i, 0)),
                   pl.BlockSpec((tm, tn), lambda i, j: (i, jnp.clip(j - j_xbc, 0, n_xbc - 1))),
                   pl.BlockSpec((tm, tn), lambda i, j: (i, jnp.clip(j - j_xbc, 0, n_xbc - 1)))],
        out_shape=[jax.ShapeDtypeStruct((t, PROJ_MAIN), BF16), jax.ShapeDtypeStruct((t // HT_TOK, D_MODEL, HT_TOK), BF16),
                   jax.ShapeDtypeStruct((t, 128), F32), jax.ShapeDtypeStruct((t, D_MODEL), BF16),
                   jax.ShapeDtypeStruct((t, 2 * SSD_INNER), BF16), jax.ShapeDtypeStruct((t, 2 * SSD_INNER), BF16)],
        scratch_shapes=[pltpu.VMEM((tm, D_MODEL), BF16), pltpu.VMEM((1 + n_xbc, HALO, tn), F32)])


def _lru_gates(xa, wa_ref, wx_ref, ba, bx, sp):
    nblk = D_MODEL // LRU_BLOCK
    pr = jnp.concatenate([_dot(xa[:, j * LRU_BLOCK:(j + 1) * LRU_BLOCK], wa_ref[j]) for j in range(nblk)], axis=1) + ba
    pi = jnp.concatenate([_dot(xa[:, j * LRU_BLOCK:(j + 1) * LRU_BLOCK], wx_ref[j]) for j in range(nblk)], axis=1) + bx
    r = _sigmoid(pr)
    i = _sigmoid(pi)
    log_a = (-LRU_C * r) * sp
    return r, i, jnp.exp(log_a), _neg_expm1(2.0 * log_a)


def lru_fwd(xa, proj, wa_bd, wx_bd, vec, nb, seq):
    t = xa.shape[0]
    tc = min(512, seq)
    nk = seq // tc
    gb = C_LRU_G // D_MODEL

    def body(xa_ref, g_ref, wa_ref, wx_ref, vec_ref, ya_ref, h_ref, gates_ref, a_scr, u_scr, hc_scr):
        @pl.when(pl.program_id(1) == 0)
        def _():
            hc_scr[...] = jnp.zeros_like(hc_scr)

        xa_v = xa_ref[...]
        v = vec_ref[...]
        r, i, a, e = _lru_gates(xa_v, wa_ref, wx_ref, v[0:1, :], v[1:2, :], v[2:3, :])
        s = jnp.sqrt(e)
        gates_ref[...] = jnp.concatenate([r, i, a, s], axis=1)
        a_scr[...] = a
        u_scr[...] = s * (i * xa_v.astype(F32))
        row = lax.broadcasted_iota(jnp.int32, (8, 1), 0)

        def tile(j, h):
            r0 = pl.multiple_of(j * 8, 8)
            av, uv = a_scr[pl.ds(r0, 8), :], u_scr[pl.ds(r0, 8), :]
            for d in (1, 2, 4):
                uv = uv + av * jnp.where(row >= d, pltpu.roll(uv, d, axis=0), 0.0)
                av = av * jnp.where(row >= d, pltpu.roll(av, d, axis=0), 1.0)
            hv = uv + av * h
            h_ref[pl.ds(r0, 8), :] = hv
            return hv[7:8, :]

        hc_scr[...] = lax.fori_loop(0, tc // 8, tile, hc_scr[...], unroll=2)
        gel, _ = _gelu_and_grad(g_ref[...].astype(F32))
        ya_ref[...] = (h_ref[...] * gel).astype(BF16)

    return pl.pallas_call(
        body, name="lru_fwd", grid=(nb, nk),
        in_specs=[pl.BlockSpec((tc, D_MODEL), lambda b, k: (b * nk + k, 0)),
                  pl.BlockSpec((tc, D_MODEL), lambda b, k: (b * nk + k, gb)),
                  VMEM_FULL, VMEM_FULL, VMEM_FULL],
        out_specs=[pl.BlockSpec((tc, D_MODEL), lambda b, k: (b * nk + k, 0)),
                   pl.BlockSpec((tc, D_MODEL), lambda b, k: (b * nk + k, 0)),
                   pl.BlockSpec((tc, 4 * D_MODEL), lambda b, k: (b * nk + k, 0))],
        out_shape=[jax.ShapeDtypeStruct((t, D_MODEL), BF16), jax.ShapeDtypeStruct((t, D_MODEL), F32),
                   jax.ShapeDtypeStruct((t, 4 * D_MODEL), F32)],
        scratch_shapes=[pltpu.VMEM((tc, D_MODEL), F32), pltpu.VMEM((tc, D_MODEL), F32), pltpu.VMEM((1, D_MODEL), F32)],
        compiler_params=_cp("arbitrary", "arbitrary"),
    )(xa, proj, wa_bd, wx_bd, vec)


def lru_bwd(dya, xa, proj, h, gates, wa_bd, wx_bd, vec, nb, seq, side=None):
    t = xa.shape[0]
    tc = min(512, seq)
    nk = seq // tc
    gb = C_LRU_G // D_MODEL
    nblk = D_MODEL // LRU_BLOCK

    def chunk(b, k):
        return b * nk + (nk - 1 - k)

    def body(dya_ref, xa_ref, g_ref, h_ref, hp_ref, gates_ref, wa_ref, wx_ref, vec_ref,
             dxa_ref, dg_ref, dwa_ref, dwx_ref, dvec_ref, a_scr, dh_scr, c_scr):
        b, k = pl.program_id(0), pl.program_id(1)

        @pl.when((b == 0) & (k == 0))
        def _():
            dwa_ref[...] = jnp.zeros_like(dwa_ref)
            dwx_ref[...] = jnp.zeros_like(dwx_ref)
            dvec_ref[...] = jnp.zeros_like(dvec_ref)

        @pl.when(k == 0)
        def _():
            c_scr[...] = jnp.zeros_like(c_scr)

        xa_v = xa_ref[...]
        xaf = xa_v.astype(F32)
        v = vec_ref[...]
        sp = v[2:3, :]
        r, i = gates_ref[:, 0:D_MODEL], gates_ref[:, D_MODEL:2 * D_MODEL]
        a, s = gates_ref[:, 2 * D_MODEL:3 * D_MODEL], gates_ref[:, 3 * D_MODEL:]
        gel, dgel = _gelu_and_grad(g_ref[...].astype(F32))
        hv = h_ref[...]
        dyv = dya_ref[...].astype(F32)
        dg_ref[...] = (dyv * hv * dgel).astype(BF16)
        a_scr[...] = a
        dh_scr[...] = dyv * gel

        row8 = lax.broadcasted_iota(jnp.int32, (8, 1), 0)

        def tile(j, c):
            r0 = pl.multiple_of((tc // 8 - 1 - j) * 8, 8)
            av, dout = a_scr[pl.ds(r0, 8), :], dh_scr[pl.ds(r0, 8), :]
            zv = av * dout
            for d in (1, 2, 4):
                zv = zv + av * jnp.where(row8 < 8 - d, pltpu.roll(zv, 8 - d, axis=0), 0.0)
                av = av * jnp.where(row8 < 8 - d, pltpu.roll(av, 8 - d, axis=0), 1.0)
            zv = zv + av * c
            dh_scr[pl.ds(r0, 8), :] = dout + jnp.where(row8 < 7, pltpu.roll(zv, 7, axis=0), c)
            return zv[0:1, :]

        c_scr[...] = lax.fori_loop(0, tc // 8, tile, c_scr[...], unroll=2)
        dh = dh_scr[...]
        h_last = jnp.where(k == nk - 1, 0.0, hp_ref[HALO // 2 - 1:HALO // 2, :])
        row = lax.broadcasted_iota(jnp.int32, (tc, 1), 0)
        h_prev = jnp.where(row == 0, h_last, pltpu.roll(hv, 1, axis=0))
        da = dh * h_prev
        ix = i * xaf
        dlog_a = da * a - (dh * ix) * (a * a) / jnp.maximum(s, 1e-15)
        di = dh * s * xaf
        dpr = (dlog_a * (-LRU_C * sp)) * (r * (1.0 - r))
        dpi = di * (i * (1.0 - i))
        dprb, dpib = dpr.astype(BF16), dpi.astype(BF16)
        dxa = dh * s * i
        dxa = dxa + jnp.concatenate(
            [_dot_nt(dprb[:, j * LRU_BLOCK:(j + 1) * LRU_BLOCK], wa_ref[j])
             + _dot_nt(dpib[:, j * LRU_BLOCK:(j + 1) * LRU_BLOCK], wx_ref[j]) for j in range(nblk)], axis=1)
        dxa_ref[...] = dxa.astype(BF16)
        for j in range(nblk):
            sl = slice(j * LRU_BLOCK, (j + 1) * LRU_BLOCK)
            dwa_ref[j] += _dot_tn(xa_v[:, sl], dprb[:, sl])
            dwx_ref[j] += _dot_tn(xa_v[:, sl], dpib[:, sl])
        dvec_ref[...] += jnp.concatenate(
            [_colsum(dpr), _colsum(dpi), _colsum(dlog_a * (-LRU_C * r)), jnp.zeros((5, D_MODEL), F32)], axis=0)

    hh = HALO // 2
    return _call(
        body, name="lru_bwd", grid=(nb, nk), side=side, sem=("arbitrary", "arbitrary"),
        args=(dya, xa, proj, h, h, gates, wa_bd, wx_bd, vec),
        in_specs=[pl.BlockSpec((tc, D_MODEL), lambda b, k: (chunk(b, k), 0)),
                  pl.BlockSpec((tc, D_MODEL), lambda b, k: (chunk(b, k), 0)),
                  pl.BlockSpec((tc, D_MODEL), lambda b, k: (chunk(b, k), gb)),
                  pl.BlockSpec((tc, D_MODEL), lambda b, k: (chunk(b, k), 0)),
                  pl.BlockSpec((hh, D_MODEL), lambda b, k: (jnp.maximum(chunk(b, k) * (tc // hh) - 1, 0), 0)),
                  pl.BlockSpec((tc, 4 * D_MODEL), lambda b, k: (chunk(b, k), 0)),
                  VMEM_FULL, VMEM_FULL, VMEM_FULL],
        out_specs=[pl.BlockSpec((tc, D_MODEL), lambda b, k: (chunk(b, k), 0)),
                   pl.BlockSpec((tc, D_MODEL), lambda b, k: (chunk(b, k), 0)),
                   pl.BlockSpec((nblk, LRU_BLOCK, LRU_BLOCK), lambda b, k: (0, 0, 0)),
                   pl.BlockSpec((nblk, LRU_BLOCK, LRU_BLOCK), lambda b, k: (0, 0, 0)),
                   pl.BlockSpec((8, D_MODEL), lambda b, k: (0, 0))],
        out_shape=[jax.ShapeDtypeStruct((t, D_MODEL), BF16), jax.ShapeDtypeStruct((t, D_MODEL), BF16),
                   jax.ShapeDtypeStruct((nblk, LRU_BLOCK, LRU_BLOCK), F32),
                   jax.ShapeDtypeStruct((nblk, LRU_BLOCK, LRU_BLOCK), F32),
                   jax.ShapeDtypeStruct((8, D_MODEL), F32)],
        scratch_shapes=[pltpu.VMEM((tc, D_MODEL), F32), pltpu.VMEM((tc, D_MODEL), F32), pltpu.VMEM((1, D_MODEL), F32)])


def merge_fwd(ya_in, yb_in, proj, x2, mod8, bgate, post1, w_pa, w_pb, w_out, seq):
    t = x2.shape[0]
    tm = min(512, seq)
    per_seq = seq // tm
    gcb = C_GATES // SSD_INNER

    def body(ya_ref, yb_ref, gt_ref, x_ref, mod_ref, bg_ref, post_ref, wpa_ref, wpb_ref, wo_ref,
             yab_ref, out1_ref, x1_ref):
        y_a = _dot(ya_ref[...], wpa_ref[...])
        y_b = _dot(yb_ref[...], wpb_ref[...])
        g = _sigmoid(gt_ref[...].astype(F32) + bg_ref[...])
        merged = g[:, :D_MODEL] * y_a + g[:, D_MODEL:] * y_b
        out1 = _dot(merged.astype(BF16), wo_ref[...])
        n = out1 * lax.rsqrt(_rowmean(out1 * out1) + EPS)
        yab_ref[...] = jnp.concatenate([y_a, y_b], axis=1).astype(BF16)
        out1_ref[...] = out1
        x1_ref[...] = x_ref[...] + mod_ref[0][2:3, :] * (n * post_ref[...])

    row = lambda w: pl.BlockSpec((tm, w), lambda i: (i, 0))
    return pl.pallas_call(
        body, name="merge_fwd", grid=(t // tm,),
        in_specs=[row(D_MODEL), row(SSD_INNER), pl.BlockSpec((tm, SSD_INNER), lambda i: (i, gcb)), row(D_MODEL),
                  pl.BlockSpec((1, 8, D_MODEL), lambda i: (i // per_seq, 0, 0)),
                  VMEM_FULL, VMEM_FULL, VMEM_FULL, VMEM_FULL, VMEM_FULL],
        out_specs=[row(SSD_INNER), row(D_MODEL), row(D_MODEL)],
        out_shape=[jax.ShapeDtypeStruct((t, SSD_INNER), BF16), jax.ShapeDtypeStruct((t, D_MODEL), F32),
                   jax.ShapeDtypeStruct((t, D_MODEL), F32)],
        compiler_params=_cp("parallel"),
    )(ya_in, yb_in, proj, x2, mod8, bgate, post1, w_pa, w_pb, w_out)


def merge_bwd(dx1, out1, yab, proj, mod8, bgate, post1, w_pa, w_pb, w_out, nb, seq):
    t = dx1.shape[0]
    tm = min(512, seq)
    per_seq = seq // tm
    gcb = C_GATES // SSD_INNER

    def body(dx1_ref, out1_ref, yab_ref, gt_ref, mod_ref, bg_ref, post_ref, wpa_ref, wpb_ref, wo_ref,
             dya_ref, dyb_ref, dgt_ref, dyab_ref, dout1_ref, mg_ref, vacc_ref, dmod_ref):
        b, s = pl.program_id(0), pl.program_id(1)

        @pl.when((b == 0) & (s == 0))
        def _():
            vacc_ref[...] = jnp.zeros_like(vacc_ref)

        @pl.when(s == 0)
        def _():
            dmod_ref[...] = jnp.zeros_like(dmod_ref)

        dx1v = dx1_ref[...]
        out1 = out1_ref[...]
        post = post_ref[...]
        rs = lax.rsqrt(_rowmean(out1 * out1) + EPS)
        n = out1 * rs
        do = dx1v * mod_ref[0][2:3, :]
        dn = do * post
        dout1 = rs * (dn - n * _rowmean(dn * n))
        dout1b = dout1.astype(BF16)
        dout1_ref[...] = dout1b
        dmerged = _dot_nt(dout1b, wo_ref[...])
        g = _sigmoid(gt_ref[...].astype(F32) + bg_ref[...])
        yab_v = yab_ref[...].astype(F32)
        gy = g * yab_v
        mg_ref[...] = (gy[:, :D_MODEL] + gy[:, D_MODEL:]).astype(BF16)
        dm2 = jnp.concatenate([dmerged, dmerged], axis=1)
        dyab = (dm2 * g).astype(BF16)
        dyab_ref[...] = dyab
        dgt = dm2 * gy * (1.0 - g)
        dgt_ref[...] = dgt.astype(BF16)
        dya_ref[...] = _dot_nt(dyab[:, :D_MODEL], wpa_ref[...]).astype(BF16)
        dyb_ref[...] = _dot_nt(dyab[:, D_MODEL:], wpb_ref[...]).astype(BF16)
        vacc_ref[...] += jnp.concatenate(
            [_colsum(dgt), jnp.concatenate([_colsum(do * n), jnp.zeros((1, D_MODEL), F32)], axis=1),
             jnp.zeros((6, SSD_INNER), F32)], axis=0)
        dmod_ref[0] += jnp.concatenate(
            [jnp.zeros((2, D_MODEL), F32), _colsum(dx1v * (n * post)), jnp.zeros((5, D_MODEL), F32)], axis=0)

    row = lambda w: pl.BlockSpec((tm, w), lambda b, s: (b * per_seq + s, 0))
    return pl.pallas_call(
        body, name="merge_bwd", grid=(nb, per_seq),
        in_specs=[row(D_MODEL), row(D_MODEL), row(SSD_INNER),
                  pl.BlockSpec((tm, SSD_INNER), lambda b, s: (b * per_seq + s, gcb)),
                  pl.BlockSpec((1, 8, D_MODEL), lambda b, s: (b, 0, 0)),
                  VMEM_FULL, VMEM_FULL, VMEM_FULL, VMEM_FULL, VMEM_FULL],
        out_specs=[row(D_MODEL), row(SSD_INNER), row(SSD_INNER), row(SSD_INNER), row(D_MODEL), row(D_MODEL),
                   pl.BlockSpec((8, SSD_INNER), lambda b, s: (0, 0)),
                   pl.BlockSpec((1, 8, D_MODEL), lambda b, s: (b, 0, 0))],
        out_shape=[jax.ShapeDtypeStruct((t, D_MODEL), BF16), jax.ShapeDtypeStruct((t, SSD_INNER), BF16),
                   jax.ShapeDtypeStruct((t, SSD_INNER), BF16), jax.ShapeDtypeStruct((t, SSD_INNER), BF16),
                   jax.ShapeDtypeStruct((t, D_MODEL), BF16), jax.ShapeDtypeStruct((t, D_MODEL), BF16),
                   jax.ShapeDtypeStruct((8, SSD_INNER), F32), jax.ShapeDtypeStruct((nb, 8, D_MODEL), F32)],
        compiler_params=_cp("arbitrary", "arbitrary"),
    )(dx1, out1, yab, proj, mod8, bgate, post1, w_pa, w_pb, w_out)


def mlp_fwd_bwd(x1, tgt, mod8, pre2, post2, w_ff1, w_ff2, nb, seq):
    t = x1.shape[0]
    tm = min(256, seq)
    per_seq = seq // tm
    fc = 1024
    nfc = D_FF // fc

    def body(x1_ref, tgt_ref, mod_ref, pre_ref, post_ref, w1_ref, w2_ref,
             dx1_ref, h2_ref, da1_ref, act_ref, dy2_ref, loss_ref, vacc_ref, dmod_ref, r_scr):
        b, s = pl.program_id(0), pl.program_id(1)
        per = fc // w1_ref.shape[2]

        def w1_cols(c):
            return jnp.concatenate([w1_ref[per * c + q] for q in range(per)], axis=1)

        @pl.when((b == 0) & (s == 0))
        def _():
            vacc_ref[...] = jnp.zeros_like(vacc_ref)
            loss_ref[...] = jnp.zeros_like(loss_ref)

        @pl.when(s == 0)
        def _():
            dmod_ref[...] = jnp.zeros_like(dmod_ref)

        m = mod_ref[0]
        sh2, sc2, g2 = m[3:4, :], m[4:5, :], m[5:6, :]
        pre, post = pre_ref[...], post_ref[...]
        x1v = x1_ref[...]
        rs1 = lax.rsqrt(_rowmean(x1v * x1v) + EPS)
        n1 = x1v * rs1
        y1 = n1 * pre
        h2b = (y1 * (1.0 + sc2) + sh2).astype(BF16)
        h2_ref[...] = h2b
        y2 = jnp.zeros((tm, D_MODEL), F32)
        for c in range(nfc):
            r = jnp.maximum(_dot(h2b, w1_cols(c)), 0.0)
            r_scr[:, c * fc:(c + 1) * fc] = r
            a = (r * r).astype(BF16)
            act_ref[:, c * fc:(c + 1) * fc] = a
            y2 = y2 + _dot(a, w2_ref[c * fc:(c + 1) * fc, :])
        rs2 = lax.rsqrt(_rowmean(y2 * y2) + EPS)
        n2 = y2 * rs2
        o2 = n2 * post
        diff = x1v + g2 * o2 - tgt_ref[...]
        loss_ref[...] += 0.5 * jnp.sum(_rowmean(diff * diff))
        dx2 = diff * (1.0 / D_MODEL)
        do2 = dx2 * g2
        dn2 = do2 * post
        dy2b = (rs2 * (dn2 - n2 * _rowmean(dn2 * n2))).astype(BF16)
        dy2_ref[...] = dy2b
        dh2 = jnp.zeros((tm, D_MODEL), F32)
        for c in range(nfc):
            dact = _dot_nt(dy2b, w2_ref[c * fc:(c + 1) * fc, :])
            da = (dact * (2.0 * r_scr[:, c * fc:(c + 1) * fc])).astype(BF16)
            da1_ref[:, c * fc:(c + 1) * fc] = da
            dh2 = dh2 + _dot_nt(da, w1_cols(c))
        dy1 = dh2 * (1.0 + sc2)
        dn1 = dy1 * pre
        dx1_ref[...] = dx2 + rs1 * (dn1 - n1 * _rowmean(dn1 * n1))
        vacc_ref[...] += jnp.concatenate([_colsum(dy1 * n1), _colsum(do2 * n2), jnp.zeros((6, D_MODEL), F32)], axis=0)
        dmod_ref[0] += jnp.concatenate(
            [jnp.zeros((3, D_MODEL), F32), _colsum(dh2), _colsum(dh2 * y1), _colsum(dx2 * o2),
             jnp.zeros((2, D_MODEL), F32)], axis=0)

    row = lambda w: pl.BlockSpec((tm, w), lambda b, s: (b * per_seq + s, 0))
    return pl.pallas_call(
        body, name="mlp_fwd_bwd", grid=(nb, per_seq),
        in_specs=[row(D_MODEL), row(D_MODEL), pl.BlockSpec((1, 8, D_MODEL), lambda b, s: (b, 0, 0)),
                  VMEM_FULL, VMEM_FULL, VMEM_FULL, VMEM_FULL],
        out_specs=[row(D_MODEL), row(D_MODEL), row(D_FF), row(D_FF), row(D_MODEL),
                   pl.BlockSpec((8, 128), lambda b, s: (0, 0)),
                   pl.BlockSpec((8, D_MODEL), lambda b, s: (0, 0)),
                   pl.BlockSpec((1, 8, D_MODEL), lambda b, s: (b, 0, 0))],
        out_shape=[jax.ShapeDtypeStruct((t, D_MODEL), F32), jax.ShapeDtypeStruct((t, D_MODEL), BF16),
                   jax.ShapeDtypeStruct((t, D_FF), BF16), jax.ShapeDtypeStruct((t, D_FF), BF16),
                   jax.ShapeDtypeStruct((t, D_MODEL), BF16), jax.ShapeDtypeStruct((8, 128), F32),
                   jax.ShapeDtypeStruct((8, D_MODEL), F32), jax.ShapeDtypeStruct((nb, 8, D_MODEL), F32)],
        scratch_shapes=[pltpu.VMEM((tm, D_FF), F32)],
        compiler_params=_cp("arbitrary", "arbitrary"),
    )(x1, tgt, mod8, pre2, post2, w_ff1, w_ff2)


_PIECES = ((C_LRU_X, 1024), (C_LRU_G, 1024), (C_Z, 2048), (C_XBC, 4096), (C_GATES, 2048))
_NP = len(_PIECES)


def in_proj_bwd(pieces, ddt, dx1, x2, mod8, pre1, w_main, w_dt, nb, seq, side=None):
    t = x2.shape[0]
    tm = min(512, seq)
    per_seq = seq // tm
    widths = [min(w, 2048) for _, w in _PIECES]
    steps = [(p, q) for p, (_, w) in enumerate(_PIECES) for q in range(w // widths[p])]
    nk = len(steps)

    def piece_spec(p):
        first = min(k for k in range(nk) if steps[k][0] == p)
        nblk = _PIECES[p][1] // widths[p]
        return pl.BlockSpec((tm, widths[p]), lambda b, s, k: (b * per_seq + s, jnp.clip(k - first, 0, nblk - 1)))

    def body(*refs):
        prefs = refs[:_NP]
        ddt_ref, dx1_ref, x_ref, mod_ref, pre_ref, w_ref, wdt_ref, gx_ref, vacc_ref, dmod_ref, acc_ref = refs[_NP:]
        b, s, k = pl.program_id(0), pl.program_id(1), pl.program_id(2)

        @pl.when((b == 0) & (s == 0) & (k == 0))
        def _():
            vacc_ref[...] = jnp.zeros_like(vacc_ref)

        @pl.when((s == 0) & (k == 0))
        def _():
            dmod_ref[...] = jnp.zeros_like(dmod_ref)

        @pl.when(k == 0)
        def _():
            acc_ref[...] = _dot_nt(ddt_ref[...], wdt_ref[...])

        for kk, (p, q) in enumerate(steps):
            @pl.when(k == kk)
            def _(p=p, q=q):
                c0 = _PIECES[p][0] + q * widths[p]
                acc_ref[...] += _dot_nt(prefs[p][...], w_ref[:, c0:c0 + widths[p]])

        @pl.when(k == nk - 1)
        def _():
            dh = acc_ref[...]
            m = mod_ref[0]
            pre = pre_ref[...]
            xv = x_ref[...]
            rs = lax.rsqrt(_rowmean(xv * xv) + EPS)
            n = xv * rs
            dy = dh * (1.0 + m[1:2, :])
            dn = dy * pre
            gx_ref[...] = dx1_ref[...] + rs * (dn - n * _rowmean(dn * n))
            vacc_ref[...] += jnp.concatenate([_colsum(dy * n), jnp.zeros((7, D_MODEL), F32)], axis=0)
            dmod_ref[0] += jnp.concatenate([_colsum(dh), _colsum(dh * (n * pre)), jnp.zeros((6, D_MODEL), F32)], axis=0)

    row = lambda w: pl.BlockSpec((tm, w), lambda b, s, k: (b * per_seq + s, 0))
    return _call(
        body, name="in_proj_bwd", grid=(nb, per_seq, nk), side=side, sem=("arbitrary", "arbitrary", "arbitrary"),
        args=(*pieces, ddt, dx1, x2, mod8, pre1, w_main, w_dt),
        in_specs=[piece_spec(p) for p in range(_NP)] + [
            row(128), row(D_MODEL), row(D_MODEL), pl.BlockSpec((1, 8, D_MODEL), lambda b, s, k: (b, 0, 0)),
            pl.BlockSpec((1, D_MODEL), lambda b, s, k: (0, 0)),
            VMEM_FULL,
            pl.BlockSpec((D_MODEL, 128), lambda b, s, k: (0, 0))],
        out_specs=[row(D_MODEL), pl.BlockSpec((8, D_MODEL), lambda b, s, k: (0, 0)),
                   pl.BlockSpec((1, 8, D_MODEL), lambda b, s, k: (b, 0, 0))],
        out_shape=[jax.ShapeDtypeStruct((t, D_MODEL), F32), jax.ShapeDtypeStruct((8, D_MODEL), F32),
                   jax.ShapeDtypeStruct((nb, 8, D_MODEL), F32)],
        scratch_shapes=[pltpu.VMEM((tm, D_MODEL), F32)])


def in_proj_wgrad(h1t, proj, dxa, dxbc, dsilu, dlg, dz, dgates, ddt, cw, seq, side=None):
    nt, _, tt = h1t.shape
    t = nt * tt
    tn = 1024
    nn = PROJ_MAIN // tn
    ns = seq // tt
    nh = t // HALO
    j_g, j_z, j_x, j_gt = C_LRU_G // tn, C_Z // tn, C_XBC // tn, C_GATES // tn
    n_x = 2 * SSD_INNER // tn
    strip = 256
    ne = tt + HALO

    def body(h_ref, cur_ref, prev_ref, next_ref, dxa_ref, dxan_ref, dxb_ref, dxbn_ref, ds_ref, dsn_ref, dlg_ref, dz_ref,
             dgt_ref, ddt_ref, cw_ref, dw_ref, dwdt_ref, dlx_ref, dxr_ref, accl_ref, accs_ref, acc_ref, accdt_ref):
        n, k = pl.program_id(0), pl.program_id(1)
        hv = h_ref[k]
        is_x = (n >= j_x) & (n < j_x + n_x)

        @pl.when(k == 0)
        def _():
            acc_ref[...] = jnp.zeros_like(acc_ref)

        @pl.when((n == 0) & (k == 0))
        def _():
            accdt_ref[...] = jnp.zeros_like(accdt_ref)
            accl_ref[...] = jnp.zeros_like(accl_ref)

        @pl.when(is_x & (k == 0))
        def _():
            accs_ref[...] = jnp.zeros_like(accs_ref)

        def conv_tile(do_ref, don_ref, out_ref, cacc_ref, act):
            first = lax.rem(k, ns) == 0
            last = lax.rem(k, ns) == ns - 1
            for c0 in range(0, tn, strip):
                cs = slice(c0, c0 + strip)
                xx = jnp.concatenate([jnp.where(first, 0.0, prev_ref[:, cs].astype(F32)), cur_ref[:, cs].astype(F32),
                                      next_ref[:, cs].astype(F32)], axis=0)
                do_ext = jnp.concatenate([do_ref[:, cs].astype(F32),
                                          jnp.where(last, 0.0, don_ref[:, cs].astype(F32))], axis=0)
                w = cw_ref[:, cs]
                xs = [xx[HALO:HALO + ne, :]] + [pltpu.roll(xx, d, axis=0)[HALO:HALO + ne, :] for d in (1, 2, 3)]
                if act:
                    dc = do_ext * jnp.concatenate([ds_ref[:, cs].astype(F32), dsn_ref[:, cs].astype(F32)], axis=0)
                else:
                    dc = do_ext
                dx = dc[:tt, :] * w[3:4, :]
                for d in (1, 2, 3):
                    dx = dx + pltpu.roll(dc, ne - d, axis=0)[:tt, :] * w[3 - d:4 - d, :]
                dxb = dx.astype(BF16)
                out_ref[:, cs] = dxb
                acc_ref[:, cs] += _dot(hv, dxb)
                dcc = dc[:tt, :]
                rows = [_colsum(dcc * xs[3 - r][:tt, :]) for r in range(4)] + [_colsum(dcc)]
                cacc_ref[:, cs] += jnp.concatenate(rows + [jnp.zeros((3, strip), F32)], axis=0)

        @pl.when(n == 0)
        def _():
            conv_tile(dxa_ref, dxan_ref, dlx_ref, accl_ref, False)
            accdt_ref[...] += _dot(hv, ddt_ref[...])

        @pl.when(is_x)
        def _():
            conv_tile(dxb_ref, dxbn_ref, dxr_ref, accs_ref, True)

        @pl.when(n == j_g)
        def _():
            acc_ref[...] += _dot(hv, dlg_ref[...])

        @pl.when((n >= j_z) & (n < j_x))
        def _():
            acc_ref[...] += _dot(hv, dz_ref[...])

        @pl.when(n >= j_gt)
        def _():
            acc_ref[...] += _dot(hv, dgt_ref[...])

        @pl.when(k == nt - 1)
        def _():
            dw_ref[...] = acc_ref[...].astype(BF16)

        @pl.when((n == 0) & (k == nt - 1))
        def _():
            dwdt_ref[...] = accdt_ref[...].astype(BF16)

    conv_n = lambda n: (n == 0) | ((n >= j_x) & (n < j_x + n_x))
    src_col = lambda n: jnp.where(n == 0, 0, jnp.clip(n, j_x, j_x + n_x - 1))
    ctile = lambda n: jnp.where(n == 0, 0, jnp.clip(n - j_x + 1, 1, n_x))
    xcol = lambda n: jnp.clip(n - j_x, 0, n_x - 1)
    on = lambda cond, k: jnp.where(cond, k, 0)
    nxt = lambda k: jnp.minimum(((k + 1) * tt) // HALO, nh - 1)
    after = lambda cond_during, cond_after, k: jnp.where(cond_during, k, jnp.where(cond_after, nt - 1, 0))
    in_specs = [
        VMEM_FULL,
        pl.BlockSpec((tt, tn), lambda n, k: (on(conv_n(n), k), src_col(n))),
        pl.BlockSpec((HALO, tn), lambda n, k: (on(conv_n(n), jnp.maximum((k * tt) // HALO - 1, 0)), src_col(n))),
        pl.BlockSpec((HALO, tn), lambda n, k: (on(conv_n(n), nxt(k)), src_col(n))),
        pl.BlockSpec((tt, tn), lambda n, k: (on(n == 0, k), 0)),
        pl.BlockSpec((HALO, tn), lambda n, k: (on(n == 0, nxt(k)), 0)),
        pl.BlockSpec((tt, tn), lambda n, k: (on((n >= j_x) & (n < j_x + n_x), k), xcol(n))),
        pl.BlockSpec((HALO, tn), lambda n, k: (on((n >= j_x) & (n < j_x + n_x), nxt(k)), xcol(n))),
        pl.BlockSpec((tt, tn), lambda n, k: (on((n >= j_x) & (n < j_x + n_x), k), xcol(n))),
        pl.BlockSpec((HALO, tn), lambda n, k: (on((n >= j_x) & (n < j_x + n_x), nxt(k)), xcol(n))),
        pl.BlockSpec((tt, tn), lambda n, k: (on(n == j_g, k), 0)),
        pl.BlockSpec((tt, tn), lambda n, k: (on((n >= j_z) & (n < j_x), k), jnp.clip(n - j_z, 0, j_x - j_z - 1))),
        pl.BlockSpec((tt, tn), lambda n, k: (on(n >= j_gt, k), jnp.clip(n - j_gt, 0, nn - j_gt - 1))),
        pl.BlockSpec((tt, 128), lambda n, k: (on(n == 0, k), 0)),
        pl.BlockSpec((4, tn), lambda n, k: (0, ctile(n)))]
    out_specs = [
        pl.BlockSpec((D_MODEL, tn), lambda n, k: (0, n)),
        pl.BlockSpec((D_MODEL, 128), lambda n, k: (0, 0)),
        pl.BlockSpec((tt, tn), lambda n, k: (after(n == 0, n > 0, k), 0)),
        pl.BlockSpec((tt, tn), lambda n, k: (after((n >= j_x) & (n < j_x + n_x), n >= j_x + n_x, k), xcol(n))),
        pl.BlockSpec((8, tn), lambda n, k: (0, 0)),
        pl.BlockSpec((8, tn), lambda n, k: (0, xcol(n)))]
    return _call(
        body, name="in_proj_wgrad", grid=(nn, nt), side=side, sem=("arbitrary", "arbitrary"),
        args=(h1t, proj, proj, proj, dxa, dxa, dxbc, dxbc, dsilu, dsilu, dlg, dz, dgates, ddt, cw),
        in_specs=in_specs, out_specs=out_specs,
        out_shape=[jax.ShapeDtypeStruct((D_MODEL, PROJ_MAIN), BF16), jax.ShapeDtypeStruct((D_MODEL, 128), BF16),
                   jax.ShapeDtypeStruct((t, D_MODEL), BF16), jax.ShapeDtypeStruct((t, 2 * SSD_INNER), BF16),
                   jax.ShapeDtypeStruct((8, D_MODEL), F32), jax.ShapeDtypeStruct((8, 2 * SSD_INNER), F32)],
        scratch_shapes=[pltpu.VMEM((D_MODEL, tn), F32), pltpu.VMEM((D_MODEL, 128), F32)])


def _log1p(u):
    w = 1.0 + u
    return jnp.log(w) - ((w - 1.0) - u) / w


def _softplus(x):
    return jnp.maximum(x, 0.0) + _log1p(jnp.exp(-jnp.abs(x)))


def _head_mask(h):
    lane = lax.broadcasted_iota(jnp.int32, (1, SSD_GW), 1)
    return (lane >= SSD_P * h) & (lane < SSD_P * (h + 1))


def _pair(p):
    return slice(2 * SSD_P * p, 2 * SSD_P * (p + 1))


def _expand4(m, g):
    lane = lax.broadcasted_iota(jnp.int32, (1, SSD_GW), 1)
    col = lambda h: m[:, 4 * g + h:4 * g + h + 1]
    return jnp.where(lane < SSD_P, col(0), jnp.where(lane < 2 * SSD_P, col(1), jnp.where(lane < 3 * SSD_P, col(2), col(3))))


def _reduce4(v, g):
    lane = lax.broadcasted_iota(jnp.int32, (1, SSD_N), 1)
    out = jnp.zeros((v.shape[0], SSD_N), F32)
    for h in range(4):
        s = jnp.sum(jnp.where(_head_mask(h), v, 0.0), axis=1, keepdims=True)
        out = out + jnp.where(lane == 4 * g + h, s, 0.0)
    return out


def _ssd_heads(dtraw, hp, tri):
    xdt = dtraw + hp[0:1, :]
    dt = _softplus(xdt)
    cs = _dot_hi(tri, dt * hp[1:2, :])
    cs_last = cs[SSD_L - 1:SSD_L, :]
    return dict(xdt=xdt, dt=dt, cs=cs, cs_t=cs.T, e=jnp.exp(cs), w=jnp.exp(cs_last - cs), el=jnp.exp(cs_last))


def _ssd_group(g, hd, xs_b, bm_b, cm_b, d_x, st, paired=False, kept=None):
    ll = SSD_L
    xs = xs_b.astype(F32)
    cs, cs_t = hd["cs"], hd["cs_t"]
    el_x = _expand4(hd["el"], g)
    e_x, w_x, dt_x = kept if kept is not None else (_expand4(hd["e"], g), _expand4(hd["w"], g), _expand4(hd["dt"], g))
    xd = xs * dt_x
    gcb = _dot_nt(cm_b, bm_b)
    ri = lax.broadcasted_iota(jnp.int32, (ll, ll), 0)
    ci = lax.broadcasted_iota(jnp.int32, (ll, ll), 1)
    dks, ms = [], []
    for h in range(4):
        k = 4 * g + h
        dk = jnp.exp(jnp.where(ri >= ci, cs[:, k:k + 1] - cs_t[k:k + 1, :], -1e30))
        dks.append(dk)
        ms.append((gcb * dk).astype(BF16))
    xdb = xd.astype(BF16)
    if paired:
        first = lax.broadcasted_iota(jnp.int32, (1, 2 * SSD_P), 1) < SSD_P
        ydiag = jnp.concatenate(
            [jnp.where(first, _dot(ms[2 * p], xdb[:, _pair(p)]), _dot(ms[2 * p + 1], xdb[:, _pair(p)]))
             for p in range(2)], axis=1)
    else:
        ydiag = jnp.zeros((ll, SSD_GW), F32)
        for h in range(4):
            ydiag = ydiag + _dot(ms[h], jnp.where(_head_mask(h), xd, 0.0).astype(BF16))
    yoff = _dot(cm_b, st.astype(BF16)) * e_x
    y = ydiag + yoff + d_x * xs
    st_new = st * el_x + _dot(bm_b.astype(F32).T.astype(BF16), (xd * w_x).astype(BF16))
    return dict(xs=xs, e_x=e_x, w_x=w_x, el_x=el_x, dt_x=dt_x, xd=xd, xdb=xdb, gcb=gcb, dks=dks, ms=ms, yoff=yoff, y=y,
                st_new=st_new)


def ssd_consts():
    hh = np.arange(SSD_N)
    tri = (hh[:, None] >= hh[None, :]).astype(np.float32)
    return jnp.asarray(tri), jnp.asarray(tri.T)


def ssd_params(dt_bias, a_log, d_skip, norm_w):
    padh = lambda v: jnp.pad(v.reshape(1, SSD_HEADS), ((0, 0), (0, SSD_N - SSD_HEADS)))
    hp = jnp.concatenate([padh(dt_bias), padh(-jnp.exp(a_log)), jnp.zeros((6, SSD_N), F32)], axis=0)
    lp = jnp.concatenate([norm_w.reshape(1, SSD_INNER), jnp.repeat(d_skip, SSD_P).reshape(1, SSD_INNER),
                          jnp.zeros((6, SSD_INNER), F32)], axis=0)
    return hp, lp


def _b_cols(g):
    return slice(SSD_INNER + g * SSD_N, SSD_INNER + (g + 1) * SSD_N)


def _c_cols(g):
    return slice(SSD_INNER + (SSD_G + g) * SSD_N, SSD_INNER + (SSD_G + g + 1) * SSD_N)


def _ssd_specs(nc, rc, cps=1):
    rows = cps * SSD_L
    return [pl.BlockSpec((rows, 2 * SSD_INNER), lambda b, c: (b * nc + rc(c), 0)),
            pl.BlockSpec((rows, SSD_INNER), lambda b, c: (b * nc + rc(c), C_Z // SSD_INNER)),
            pl.BlockSpec((rows, SSD_N), lambda b, c: (b * nc + rc(c), 0))]


def ssd_fwd(xbc, proj, dtraw, hp, lp, tri, nb, seq):
    t = xbc.shape[0]
    cps = SSD_FWD_CPS
    nc = seq // (cps * SSD_L)

    def body(xbc_ref, z_ref, dt_ref, hp_ref, lp_ref, tri_ref, y_ref, sts_ref, kept_ref, st_scr):
        @pl.when(pl.program_id(1) == 0)
        def _():
            st_scr[...] = jnp.zeros_like(st_scr)

        for cc in range(cps):
            rs = slice(cc * SSD_L, (cc + 1) * SSD_L)
            hd = _ssd_heads(dt_ref[rs, :], hp_ref[...], tri_ref[...])
            for g in range(SSD_G):
                gs = slice(g * SSD_GW, (g + 1) * SSD_GW)
                st = st_scr[g]
                sts_ref[cc, g] = st
                f = _ssd_group(g, hd, xbc_ref[rs, gs], xbc_ref[rs, _b_cols(g)], xbc_ref[rs, _c_cols(g)],
                               lp_ref[1:2, gs], st, paired=True)
                st_scr[g] = f["st_new"]
                for q, name in enumerate(("e_x", "w_x", "dt_x")):
                    kept_ref[rs, q * SSD_INNER + g * SSD_GW:q * SSD_INNER + (g + 1) * SSD_GW] = f[name]
                zf = z_ref[rs, gs].astype(F32)
                yg = f["y"] * (zf * _sigmoid(zf))
                y_ref[rs, gs] = (yg * lax.rsqrt(_rowmean(yg * yg) + EPS) * lp_ref[0:1, gs]).astype(BF16)

    return pl.pallas_call(
        body, name="ssd_fwd", grid=(nb, nc),
        in_specs=_ssd_specs(nc, lambda c: c, cps) + [VMEM_FULL, VMEM_FULL, VMEM_FULL],
        out_specs=[pl.BlockSpec((cps * SSD_L, SSD_INNER), lambda b, c: (b * nc + c, 0)),
                   pl.BlockSpec((cps, SSD_G, SSD_N, SSD_GW), lambda b, c: (b * nc + c, 0, 0, 0)),
                   pl.BlockSpec((cps * SSD_L, 3 * SSD_INNER), lambda b, c: (b * nc + c, 0))],
        out_shape=[jax.ShapeDtypeStruct((t, SSD_INNER), BF16),
                   jax.ShapeDtypeStruct((nb * nc * cps, SSD_G, SSD_N, SSD_GW), F32),
                   jax.ShapeDtypeStruct((t, 3 * SSD_INNER), F32)],
        scratch_shapes=[pltpu.VMEM((SSD_G, SSD_N, SSD_GW), F32)],
        compiler_params=_cp("arbitrary", "arbitrary"),
    )(xbc, proj, dtraw, hp, lp, tri)


def ssd_bwd(xbc, proj, dtraw, hp, lp, tri, triu, states, kept, dyn, nb, seq, side=None):
    t = xbc.shape[0]
    nc = seq // SSD_L
    ll = SSD_L

    def body(xbc_ref, z_ref, dt_ref, sts_ref, dy_ref, kept_ref, hp_ref, lp_ref, tri_ref, triu_ref,
             dxbc_ref, dz_ref, ddt_ref, hpg_ref, lpg_ref, dst_scr):
        b, c_i = pl.program_id(0), pl.program_id(1)

        @pl.when((b == 0) & (c_i == 0))
        def _():
            hpg_ref[...] = jnp.zeros_like(hpg_ref)
            lpg_ref[...] = jnp.zeros_like(lpg_ref)

        @pl.when(c_i == 0)
        def _():
            dst_scr[...] = jnp.zeros_like(dst_scr)

        hp = hp_ref[...]
        hd = _ssd_heads(dt_ref[...], hp, tri_ref[...])
        lane = lax.broadcasted_iota(jnp.int32, (1, SSD_N), 1)
        subl = lax.broadcasted_iota(jnp.int32, (SSD_N, 1), 0)
        dcs = jnp.zeros((ll, SSD_N), F32)
        dcs_t = jnp.zeros((SSD_N, ll), F32)
        last = jnp.zeros((1, SSD_N), F32)
        dxx = jnp.zeros((ll, SSD_N), F32)
        for g in range(SSD_G):
            gs = slice(g * SSD_GW, (g + 1) * SSD_GW)
            st = sts_ref[0, g]
            dst = dst_scr[g]
            bm_b, cm_b = xbc_ref[:, _b_cols(g)], xbc_ref[:, _c_cols(g)]
            d_x = lp_ref[1:2, gs]
            kept = tuple(kept_ref[:, q * SSD_INNER + g * SSD_GW:q * SSD_INNER + (g + 1) * SSD_GW] for q in range(3))
            f = _ssd_group(g, hd, xbc_ref[:, gs], bm_b, cm_b, d_x, st, kept=kept)
            xs, xd, gcb = f["xs"], f["xd"], f["gcb"]
            e_x, w_x, el_x, dt_x = f["e_x"], f["w_x"], f["el_x"], f["dt_x"]
            stb, dstb = st.astype(BF16), dst.astype(BF16)
            zf = z_ref[:, gs].astype(F32)
            sg = _sigmoid(zf)
            sz = zf * sg
            yv = f["y"]
            yg = yv * sz
            rstd = lax.rsqrt(_rowmean(yg * yg) + EPS)
            n = yg * rstd
            dyn_v = dy_ref[:, gs].astype(F32)
            dn = dyn_v * lp_ref[0:1, gs]
            dyg = rstd * (dn - n * _rowmean(dn * n))
            dy = dyg * sz
            dz_ref[:, gs] = (dyg * yv * (sg * (1.0 + zf * (1.0 - sg)))).astype(BF16)
            dyb = dy.astype(BF16)
            r_ = _dot(bm_b, dstb)
            dxd = w_x * r_
            dqb = (dy * e_x).astype(BF16)
            dcm = _dot_nt(dqb, stb)
            dst_scr[g] = dst * el_x + _dot_tn(cm_b, dqb)
            dbm = _dot_nt((xd * w_x).astype(BF16), dstb)
            xdb = f["xdb"]
            dgm = jnp.zeros((ll, ll), F32)
            for h in range(4):
                k = 4 * g + h
                hm = _head_mask(h)
                dxd = dxd + jnp.where(hm, _dot_tn(f["ms"][h], dyb), 0.0)
                dm = _dot_nt(jnp.where(hm, dy, 0.0).astype(BF16), xdb) * f["dks"][h]
                dgm = dgm + dm
                dseg = dm * gcb
                dcs = dcs + jnp.where(lane == k, jnp.sum(dseg, axis=1, keepdims=True), 0.0)
                dcs_t = dcs_t + jnp.where(subl == k, jnp.sum(dseg, axis=0, keepdims=True), 0.0)
            dgmb = dgm.astype(BF16)
            dxbc_ref[:, _c_cols(g)] = (dcm + _dot(dgmb, bm_b)).astype(BF16)
            dxbc_ref[:, _b_cols(g)] = (dbm + _dot_tn(dgmb, cm_b)).astype(BF16)
            v = _reduce4(r_ * xd * w_x, g)
            dcs = dcs + _reduce4(dy * f["yoff"], g) - v
            last = last + _colsum(v) + _reduce4(_colsum(dst * st) * el_x, g)
            dxx = dxx + _reduce4(dxd * xs, g)
            dxbc_ref[:, gs] = (d_x * dy + dxd * dt_x).astype(BF16)
            lpg_ref[0:1, gs] += _colsum(dyn_v * n)
            lpg_ref[1:2, gs] += _colsum(dy * xs)
        rowi = lax.broadcasted_iota(jnp.int32, (ll, 1), 0)
        da = _dot_hi(triu_ref[...], dcs - dcs_t.T + jnp.where(rowi == ll - 1, last, 0.0))
        ddt = (dxx + da * hp[1:2, :]) * _sigmoid(hd["xdt"])
        ddt_ref[...] = ddt
        hpg_ref[...] += jnp.concatenate([_colsum(ddt), _colsum(da * hd["dt"]), jnp.zeros((6, SSD_N), F32)], axis=0)

    rc = lambda c: nc - 1 - c
    return _call(
        body, name="ssd_bwd", grid=(nb, nc), side=side, sem=("arbitrary", "arbitrary"),
        args=(xbc, proj, dtraw, states, dyn, kept, hp, lp, tri, triu),
        in_specs=_ssd_specs(nc, rc) + [
            pl.BlockSpec((1, SSD_G, SSD_N, SSD_GW), lambda b, c: (b * nc + rc(c), 0, 0, 0)),
            pl.BlockSpec((SSD_L, SSD_INNER), lambda b, c: (b * nc + rc(c), 0)),
            pl.BlockSpec((SSD_L, 3 * SSD_INNER), lambda b, c: (b * nc + rc(c), 0)),
            VMEM_FULL, VMEM_FULL, VMEM_FULL, VMEM_FULL],
        out_specs=[pl.BlockSpec((SSD_L, 2 * SSD_INNER), lambda b, c: (b * nc + rc(c), 0)),
                   pl.BlockSpec((SSD_L, SSD_INNER), lambda b, c: (b * nc + rc(c), 0)),
                   pl.BlockSpec((SSD_L, SSD_N), lambda b, c: (b * nc + rc(c), 0)),
                   pl.BlockSpec((8, SSD_N), lambda b, c: (0, 0)),
                   pl.BlockSpec((8, SSD_INNER), lambda b, c: (0, 0))],
        out_shape=[jax.ShapeDtypeStruct((t, 2 * SSD_INNER), BF16), jax.ShapeDtypeStruct((t, SSD_INNER), BF16),
                   jax.ShapeDtypeStruct((t, SSD_N), F32), jax.ShapeDtypeStruct((8, SSD_N), F32),
                   jax.ShapeDtypeStruct((8, SSD_INNER), F32)],
        scratch_shapes=[pltpu.VMEM((SSD_G, SSD_N, SSD_GW), F32)])


def ada_fwd(c_all, w_cols, b_cols):
    def body(c_ref, w_ref, b_ref, o_ref):
        cv = c_ref[...]
        o_ref[...] = _dot_hi(cv * _sigmoid(cv), w_ref[...]) + b_ref[...]

    return pl.pallas_call(body, name="ada_fwd", out_shape=jax.ShapeDtypeStruct((c_all.shape[0], w_cols.shape[1]), F32),
                          compiler_params=pltpu.CompilerParams(vmem_limit_bytes=VMEM_LIMIT))(c_all, w_cols, b_cols)


def ada_bwd(c_all, dmod_cols, dmod_all):
    def body(c_ref, dc_ref, da_ref, gw_ref, gb_ref):
        cv = c_ref[...]
        gw_ref[...] = lax.dot_general(cv * _sigmoid(cv), dc_ref[...], (((0,), (0,)), ((), ())),
                                      precision=lax.Precision.HIGHEST, preferred_element_type=F32)
        gb_ref[...] = _colsum(da_ref[...])

    return pl.pallas_call(
        body, name="ada_bwd",
        out_shape=[jax.ShapeDtypeStruct((c_all.shape[1], dmod_cols.shape[1]), F32),
                   jax.ShapeDtypeStruct((1, dmod_all.shape[1]), F32)],
        compiler_params=pltpu.CompilerParams(vmem_limit_bytes=VMEM_LIMIT))(c_all, dmod_cols, dmod_all)


def _adam_update(g, w, m, v):
    m2 = ADAM_B1 * m + (1.0 - ADAM_B1) * g
    v2 = ADAM_B2 * v + (1.0 - ADAM_B2) * (g * g)
    m_hat = m2 / (1.0 - ADAM_B1 ** ADAM_STEP)
    v_hat = v2 / (1.0 - ADAM_B2 ** ADAM_STEP)
    return -ADAM_LR * (m_hat / (jnp.sqrt(v_hat) + ADAM_EPS) + ADAM_WD * w), m2, v2


def adamw(parts, w, m, v, name):
    n, r, c = parts.shape
    tr = r if r <= 256 else 128

    def body(p_ref, w_ref, m_ref, v_ref, g_ref, d_ref, nm_ref, nv_ref):
        g = p_ref[0].astype(F32)
        for s in range(1, n):
            g = g + p_ref[s].astype(F32)
        g_ref[0] = g
        d_ref[0], nm_ref[0], nv_ref[0] = _adam_update(g, w_ref[0], m_ref[0], v_ref[0])

    blk = pl.BlockSpec((1, tr, c), lambda i: (0, i, 0))
    return pl.pallas_call(
        body, name=name, grid=(r // tr,),
        in_specs=[pl.BlockSpec((n, tr, c), lambda i: (0, i, 0)), blk, blk, blk], out_specs=[blk] * 4,
        out_shape=[jax.ShapeDtypeStruct((1, r, c), F32)] * 4,
        compiler_params=_cp("parallel"),
    )(parts, w, m, v)


SMALL_SRC = {
    'pre_norm1': ('vin', 0, 1024), 'post_norm1': ('vmg', 1, 1024), 'b_gate': ('vmg', 0, 2048),
    'lru_conv_b': ('accl', 4, 1024), 'lru_wa': ('gwa', None, None), 'lru_ba': ('dvec', 0, 1024),
    'lru_wx': ('gwx', None, None), 'lru_bx': ('dvec', 1, 1024), 'lru_lambda': ('dvec', 2, 1024),
    'ssd_conv_b': ('accs', 4, 4096), 'ssd_dt_bias': ('hpg', 0, SSD_HEADS), 'ssd_a_log': ('hpg', 1, SSD_HEADS),
    'ssd_d': ('lpg', 1, SSD_INNER), 'ssd_norm_w': ('lpg', 0, SSD_INNER), 'pre_norm2': ('vmlp', 0, 1024),
    'post_norm2': ('vmlp', 1, 1024)}
SMALL_ACCS = ('vin', 'vmg', 'vmlp', 'dvec', 'accl', 'accs', 'hpg', 'lpg', 'gwa', 'gwx')
SMALL_RIDE = ('vmg', 'vmlp', 'dvec', 'hpg', 'lpg', 'gwa', 'gwx')


def adamw_small(gathered, params):
    names = tuple(params)
    na = len(SMALL_ACCS)

    def body(*refs):
        acc = {k: functools.reduce(lambda p, q: p + q, [refs[i][s] for s in range(NDEV)])
               for i, k in enumerate(SMALL_ACCS)}
        ins = refs[na:na + 3 * len(names)]
        outs = refs[na + 3 * len(names):]
        for j, k in enumerate(names):
            w_ref, m_ref, v_ref = ins[3 * j:3 * j + 3]
            src, row, width = SMALL_SRC[k]
            wv = w_ref[...]
            if row is None:
                g = acc[src]
            elif k == 'ssd_d':
                li = lax.broadcasted_iota(jnp.int32, (SSD_INNER, SSD_N), 0)
                hi = lax.broadcasted_iota(jnp.int32, (SSD_INNER, SSD_N), 1)
                g = _dot_hi(acc[src], jnp.where(jnp.right_shift(li, 6) == hi, 1.0, 0.0))[row:row + 1, :SSD_HEADS]
            else:
                g = acc[src][row:row + 1, :width]
            if k == 'lru_lambda':
                g = g * (-1.0 / (1.0 + jnp.exp(wv)))
            if k == 'ssd_a_log':
                g = g * (-jnp.exp(wv))
            o = outs[4 * j:4 * j + 4]
            o[0][...] = g
            o[1][...], o[2][...], o[3][...] = _adam_update(g, wv, m_ref[...], v_ref[...])
        outs[-2][...] = acc['accl'][0:4, :]
        outs[-1][...] = acc['accs'][0:4, :]

    flat = [a for k in names for a in params[k]]
    out_shape = [jax.ShapeDtypeStruct(params[k][0].shape, F32) for k in names for _ in range(4)]
    out_shape += [jax.ShapeDtypeStruct((4, D_MODEL), F32), jax.ShapeDtypeStruct((4, 2 * SSD_INNER), F32)]
    res = pl.pallas_call(body, name="adamw_small", out_shape=out_shape,
                         compiler_params=pltpu.CompilerParams(vmem_limit_bytes=VMEM_LIMIT))(
        *[gathered[k] for k in SMALL_ACCS], *flat)
    return {k: res[4 * j:4 * j + 4] for j, k in enumerate(names)}, res[-2], res[-1]


def _dev_index(px, py, pc):
    return 4 * px + 2 * py + pc


class _Exchange:
    def __init__(self, arrs):
        self.arrs = list(arrs)
        self.na = len(self.arrs)
        self.scratch = [pltpu.SemaphoreType.DMA((7 * self.na,)), pltpu.SemaphoreType.DMA((7 * self.na,)),
                        pltpu.SemaphoreType.DMA((self.na,))]


class Gather(_Exchange):
    def __init__(self, arrs):
        super().__init__(arrs)
        self.out_shape = [jax.ShapeDtypeStruct((NDEV,) + a.shape, a.dtype) for a in self.arrs]

    def _plan(self, ins, outs, sems):
        na = self.na
        send_sems, recv_sems, local_sems = sems
        x, y, c = lax.axis_index("x"), lax.axis_index("y"), lax.axis_index("c")
        me, sibling = (x, y, c), (x, y, 1 - c)
        chips = [(1 - x, y), (x, 1 - y), (1 - x, 1 - y)]

        def copy(a, k, block, to, src=None):
            dst = outs[a].at[_dev_index(*block)]
            return pltpu.make_async_remote_copy(
                src_ref=dst if src is None else src, dst_ref=dst, send_sem=send_sems.at[a * 7 + k],
                recv_sem=recv_sems.at[a * 7 + k], device_id=to, device_id_type=MESH)

        mine = [pltpu.make_async_copy(ins[a], outs[a].at[_dev_index(*me)], local_sems.at[a]) for a in range(na)]
        first = []
        for a in range(na):
            first.append(copy(a, 0, me, sibling, src=ins[a]))
            first += [copy(a, 1 + j, me, (*chip, c), src=ins[a]) for j, chip in enumerate(chips)]
        return copy, mine, first, me, sibling, chips, c

    def start(self, ins, outs, sems):
        _, mine, first, *_ = self._plan(ins, outs, sems)
        for cp in mine + first:
            cp.start()

    def finish(self, ins, outs, sems):
        copy, mine, first, me, sibling, chips, c = self._plan(ins, outs, sems)
        passed = []
        for j, chip in enumerate(chips):
            for a in range(self.na):
                copy(a, 1 + j, (*chip, c), me).wait_recv()
                cp = copy(a, 4 + j, (*chip, c), sibling)
                cp.start()
                passed.append(cp)
        for a in range(self.na):
            copy(a, 0, sibling, me).wait_recv()
            for j, chip in enumerate(chips):
                copy(a, 4 + j, (*chip, 1 - c), me).wait_recv()
        for cp in first + passed:
            cp.wait_send()
        for cp in mine:
            cp.wait()


class GatherRelay(Gather):
    def _plan(self, ins, outs, sems):
        na = self.na
        send_sems, recv_sems, local_sems = sems
        x, y, c = lax.axis_index("x"), lax.axis_index("y"), lax.axis_index("c")
        me, sibling = (x, y, c), (x, y, 1 - c)
        xn, yn, dg = (1 - x, y), (x, 1 - y), (1 - x, 1 - y)
        south = c == 0
        pick = lambda a, b: tuple(jnp.where(south, p, q) for p, q in zip(a, b))
        relay_to = pick(yn, xn)
        relay_of = pick(xn, yn)

        def copy(a, k, block, to, src=None):
            dst = outs[a].at[_dev_index(*block)]
            return pltpu.make_async_remote_copy(
                src_ref=dst if src is None else src, dst_ref=dst, send_sem=send_sems.at[a * 7 + k],
                recv_sem=recv_sems.at[a * 7 + k], device_id=to, device_id_type=MESH)

        mine = [pltpu.make_async_copy(ins[a], outs[a].at[_dev_index(*me)], local_sems.at[a]) for a in range(na)]
        first = []
        for a in range(na):
            first += [copy(a, 0, me, sibling, src=ins[a]), copy(a, 1, me, (*xn, c), src=ins[a]),
                      copy(a, 2, me, (*yn, c), src=ins[a])]
        return copy, mine, first, me, sibling, (xn, yn, dg), c, relay_to, relay_of

    def start(self, ins, outs, sems):
        _, mine, first, *_ = self._plan(ins, outs, sems)
        for cp in mine + first:
            cp.start()

    def finish(self, ins, outs, sems):
        copy, mine, first, me, sibling, (xn, yn, dg), c, relay_to, relay_of = self._plan(ins, outs, sems)
        later = []
        for a in range(self.na):
            copy(a, 1, (*xn, c), me).wait_recv()
            copy(a, 2, (*yn, c), me).wait_recv()
            later.append(copy(a, 3, (*relay_of, c), (*relay_to, c)))
            later += [copy(a, 4, (*xn, c), sibling), copy(a, 5, (*yn, c), sibling)]
            for cp in later[-3:]:
                cp.start()
        for a in range(self.na):
            copy(a, 3, (*dg, c), me).wait_recv()
            cp = copy(a, 6, (*dg, c), sibling)
            cp.start()
            later.append(cp)
        for a in range(self.na):
            copy(a, 0, sibling, me).wait_recv()
            for k, chip in ((4, xn), (5, yn), (6, dg)):
                copy(a, k, (*chip, 1 - c), me).wait_recv()
        for cp in first + later:
            cp.wait_send()
        for cp in mine:
            cp.wait()


class Scatter(_Exchange):
    def __init__(self, arrs):
        super().__init__(arrs)
        self.out_shape = [jax.ShapeDtypeStruct(a.shape, a.dtype) for a in self.arrs]

    def _plan(self, ins, outs, sems, arrivals):
        send_sems, recv_sems, local_sems = sems
        x, y, c = lax.axis_index("x"), lax.axis_index("y"), lax.axis_index("c")
        me = _dev_index(x, y, c)
        masks = [(mx, my, mc) for mx in (0, 1) for my in (0, 1) for mc in (0, 1)][1:]
        flip = lambda v, bit: 1 - v if bit else v
        mine = [pltpu.make_async_copy(ins[a].at[me], outs[a].at[me], local_sems.at[a]) for a in range(self.na)]
        sends, recvs = [], []
        for k, (mx, my, mc) in enumerate(masks):
            peer = (flip(x, mx), flip(y, my), flip(c, mc))
            pidx = _dev_index(*peer)
            for a in range(self.na):
                on = dict(send_sem=send_sems.at[a * 7 + k], recv_sem=recv_sems.at[a * 7 + k], device_id=peer,
                          device_id_type=MESH)
                sends.append(pltpu.make_async_remote_copy(src_ref=ins[a].at[pidx], dst_ref=outs[a].at[me], **on))
                if arrivals:
                    recvs.append(pltpu.make_async_remote_copy(src_ref=ins[a].at[pidx], dst_ref=outs[a].at[pidx], **on))
        return mine, sends, recvs

    def start(self, ins, outs, sems):
        mine, sends, _ = self._plan(ins, outs, sems, arrivals=False)
        for cp in mine + sends:
            cp.start()

    def finish(self, ins, outs, sems):
        mine, sends, recvs = self._plan(ins, outs, sems, arrivals=True)
        for cp in recvs:
            cp.wait_recv()
        for cp in sends:
            cp.wait_send()
        for cp in mine:
            cp.wait()


def exchange_call(ex, name):
    na = ex.na

    def body(*refs):
        ins, outs, sems = refs[:na], refs[na:2 * na], refs[2 * na:]
        ex.start(ins, outs, sems)
        ex.finish(ins, outs, sems)

    return pl.pallas_call(body, name=name, in_specs=[ANY] * na, out_specs=[ANY] * na, out_shape=ex.out_shape,
                          scratch_shapes=ex.scratch)(*ex.arrs)


def all_gather(arrs, name, relay=False):
    return exchange_call((GatherRelay if relay else Gather)(arrs), name)


def _call(body, *, name, grid, in_specs, out_specs, out_shape, scratch_shapes=(), sem, args, side=None):
    if side is None:
        outs = pl.pallas_call(body, name=name, grid=grid, in_specs=list(in_specs), out_specs=list(out_specs),
                              out_shape=list(out_shape), scratch_shapes=list(scratch_shapes),
                              compiler_params=_cp(*sem))(*args)
        return outs, []
    ni, no, ns, na = len(in_specs), len(out_specs), len(scratch_shapes), side.na

    def wrapped(*refs):
        ins, s_in = refs[:ni], refs[ni:ni + na]
        outs, s_out = refs[ni + na:ni + na + no], refs[ni + na + no:ni + 2 * na + no]
        scr, sems = refs[ni + 2 * na + no:ni + 2 * na + no + ns], refs[ni + 2 * na + no + ns:]
        pids = [pl.program_id(i) for i in range(len(grid))]
        first = functools.reduce(lambda p, q: p & q, [p == 0 for p in pids])
        last = functools.reduce(lambda p, q: p & q, [p == g - 1 for p, g in zip(pids, grid)])

        @pl.when(first)
        def _():
            side.start(s_in, s_out, sems)

        body(*ins, *outs, *scr)

        @pl.when(last)
        def _():
            side.finish(s_in, s_out, sems)

    outs = pl.pallas_call(
        wrapped, name=name, grid=grid, in_specs=list(in_specs) + [ANY] * na, out_specs=list(out_specs) + [ANY] * na,
        out_shape=list(out_shape) + side.out_shape, scratch_shapes=list(scratch_shapes) + side.scratch,
        compiler_params=_cp(*["arbitrary"] * len(grid)))(*args, *side.arrs)
    return outs[:no], outs[no:]


WEIGHTS = ('w_ada', 'b_ada', 'pre_norm1', 'post_norm1', 'w_in', 'b_gate', 'lru_conv_w', 'lru_conv_b', 'lru_wa',
           'lru_ba', 'lru_wx', 'lru_bx', 'lru_lambda', 'w_pa', 'ssd_conv_w', 'ssd_conv_b', 'ssd_dt_bias', 'ssd_a_log',
           'ssd_d', 'ssd_norm_w', 'w_pb', 'w_out', 'pre_norm2', 'post_norm2', 'w_ff1', 'w_ff2')
BIG = ('w_in', 'w_pa', 'w_pb', 'w_out', 'w_ff1', 'w_ff2')
REPL = ('pre_norm1', 'post_norm1', 'b_gate', 'lru_conv_b', 'lru_wa', 'lru_ba', 'lru_wx', 'lru_bx', 'lru_lambda',
        'ssd_conv_b', 'ssd_dt_bias', 'ssd_a_log', 'ssd_d', 'ssd_norm_w', 'pre_norm2', 'post_norm2')
LANES = 1024


def _rows(n):
    return -(-n // LANES)


def _pack(vals, total_rows):
    parts = []
    for v in vals:
        f = v.reshape(-1).astype(F32)
        parts.append(jnp.pad(f, (0, _rows(f.shape[0]) * LANES - f.shape[0])))
    flat = jnp.concatenate(parts)
    return jnp.pad(flat.reshape(-1, LANES), ((0, total_rows - flat.shape[0] // LANES), (0, 0)))


def _unpack(slab, shapes):
    out, r = [], 0
    for s in shapes:
        n = int(np.prod(s))
        out.append(slab[r:r + _rows(n)].reshape(-1)[:n].reshape(s))
        r += _rows(n)
    return out


def _block_diag4(w):
    w4 = w.reshape(4, 4, 64, 64)
    eye = jnp.eye(4, dtype=w.dtype)
    return (w4[:, :, :, None, :] * eye[None, :, None, :, None]).reshape(4, LRU_BLOCK, LRU_BLOCK)


def _diag_blocks4(m):
    m5 = m.reshape(4, 4, 64, 4, 64)
    return jnp.stack([m5[:, a, :, a, :] for a in range(4)], axis=1).reshape(LRU_HEADS, 64, 64)


def kernel(x, c, w_ada, b_ada, pre_norm1, post_norm1, w_in, b_gate, lru_conv_w, lru_conv_b, lru_wa, lru_ba, lru_wx, lru_bx, lru_lambda, w_pa, ssd_conv_w, ssd_conv_b, ssd_dt_bias, ssd_a_log, ssd_d, ssd_norm_w, w_pb, w_out, pre_norm2, post_norm2, w_ff1, w_ff2, loss_target, m_w_ada, m_b_ada, m_pre_norm1, m_post_norm1, m_w_in, m_b_gate, m_lru_conv_w, m_lru_conv_b, m_lru_wa, m_lru_ba, m_lru_wx, m_lru_bx, m_lru_lambda, m_w_pa, m_ssd_conv_w, m_ssd_conv_b, m_ssd_dt_bias, m_ssd_a_log, m_ssd_d, m_ssd_norm_w, m_w_pb, m_w_out, m_pre_norm2, m_post_norm2, m_w_ff1, m_w_ff2, v_w_ada, v_b_ada, v_pre_norm1, v_post_norm1, v_w_in, v_b_gate, v_lru_conv_w, v_lru_conv_b, v_lru_wa, v_lru_ba, v_lru_wx, v_lru_bx, v_lru_lambda, v_w_pa, v_ssd_conv_w, v_ssd_conv_b, v_ssd_dt_bias, v_ssd_a_log, v_ssd_d, v_ssd_norm_w, v_w_pb, v_w_out, v_pre_norm2, v_post_norm2, v_w_ff1, v_w_ff2):
    given = dict(locals())
    w = {k: given[k] for k in WEIGHTS}
    mom = {k: given["m_" + k] for k in WEIGHTS}
    var = {k: given["v_" + k] for k in WEIGHTS}
    nb, seq, _ = x.shape
    assert nb == 2 and seq % 512 == 0, (nb, seq)
    t = nb * seq
    me = _dev_index(lax.axis_index("x"), lax.axis_index("y"), lax.axis_index("c"))
    x2 = x.reshape(t, D_MODEL)
    tgt2 = loss_target.reshape(t, D_MODEL)
    ada_cols = w_ada.shape[2]

    slab = jnp.zeros((16, LANES), F32)
    slab = slab.at[0:nb].set(c)
    slab = slab.at[2:6, 0:lru_conv_w.shape[2]].set(lru_conv_w[0])
    slab = slab.at[6:10, 0:ssd_conv_w.shape[2]].set(ssd_conv_w[0])
    g1, gw_in = all_gather([slab, w_in.astype(BF16)], "gather_cond_w_in", relay=True)
    c_all = g1[:, 0:nb].reshape(NDEV * nb, D_MODEL)
    lru_cw = g1[:, 2:6, 0:lru_conv_w.shape[2]].transpose(1, 0, 2).reshape(4, D_MODEL)
    ssd_cw = g1[:, 6:10, 0:ssd_conv_w.shape[2]].transpose(1, 0, 2).reshape(4, 2 * SSD_INNER)
    b_cols = lax.dynamic_slice(b_ada, (0, me * ada_cols), (1, ada_cols))
    mod_cols = ada_fwd(c_all, w_ada[0], b_cols)
    (g2,) = all_gather([mod_cols], "gather_mod")
    mod_all = g2.transpose(1, 0, 2).reshape(NDEV * nb, N_MOD * D_MODEL)
    mod_mine = lax.dynamic_slice(mod_all, (me * nb, 0), (nb, N_MOD * D_MODEL)).reshape(nb, N_MOD, D_MODEL)
    mod8 = jnp.pad(mod_mine, ((0, 0), (0, 8 - N_MOD), (0, 0)))

    shard = IN_DIM // NDEV
    kd, od = DT_COL0 // shard, DT_COL0 % shard
    assert od + SSD_HEADS <= shard
    gb = gw_in[:, 0]
    w_main = jnp.concatenate([gb[k] for k in range(kd)] + [gb[kd][:, :od], gb[kd][:, od + SSD_HEADS:]]
                             + [gb[k] for k in range(kd + 1, NDEV)], axis=1)
    w_dt = jnp.pad(gb[kd][:, od:od + SSD_HEADS], ((0, 0), (0, 128 - SSD_HEADS)))

    wa_bd = _block_diag4(lru_wa[0]).astype(BF16)
    wx_bd = _block_diag4(lru_wx[0]).astype(BF16)
    lam = lru_lambda[0]
    vec = _pack([lru_ba, lru_bx, jax.nn.softplus(-lam)], 8)
    tri, triu = ssd_consts()
    hp, lp = ssd_params(ssd_dt_bias[0], ssd_a_log[0], ssd_d[0], ssd_norm_w[0])

    rest = Gather([w[k].astype(BF16) for k in BIG[1:]])
    cw_all = jnp.concatenate([lru_cw, ssd_cw], axis=1)
    cb_all = jnp.concatenate([lru_conv_b, ssd_conv_b], axis=1)
    (proj, h1t, dtraw, xa, xbc, dsilu), gw = in_proj_fwd(x2, mod8, pre_norm1, w_main, w_dt, cw_all, cb_all, seq, side=rest)
    w_pa_f = gw[0].reshape(D_MODEL, D_MODEL)
    w_pb_f = gw[1].reshape(SSD_INNER, D_MODEL)
    w_out_f = gw[2].reshape(D_MODEL, D_MODEL)
    w_ff1_f = gw[3][:, 0]
    w_ff2_f = gw[4].reshape(D_FF, D_MODEL)
    ya_in, hst, lru_gates = lru_fwd(xa, proj, wa_bd, wx_bd, vec, nb, seq)
    yb_in, states, ssd_kept = ssd_fwd(xbc, proj, dtraw, hp, lp, tri, nb, seq)
    yab, out1, x1 = merge_fwd(ya_in, yb_in, proj, x2, mod8, b_gate, post_norm1, w_pa_f, w_pb_f, w_out_f, seq)

    dx1, h2, da1, act, dy2, loss8, vacc_mlp, dmod_mlp = mlp_fwd_bwd(
        x1, tgt2, mod8, pre_norm2, post_norm2, w_ff1_f, w_ff2_f, nb, seq)
    wg = dict(out_dtype=BF16, ta=True, tm=1024, tn=1024, tk=1024)
    dw_ff1 = matmul(h2, da1, name="wgrad_ff1", blocked_out=D_FF // NDEV, **wg)
    dw_ff2 = matmul(act, dy2, name="wgrad_ff2", **wg)
    dya_in, dyb_in, dgates, dyab, dout1, merged, vacc_mg, dmod_mg = merge_bwd(
        dx1, out1, yab, proj, mod8, b_gate, post_norm1, w_pa_f, w_pb_f, w_out_f, nb, seq)
    dw_out = matmul(merged, dout1, name="wgrad_out", **wg)
    dw_pa = matmul(ya_in, dyab, name="wgrad_pa", n=D_MODEL, b_off=0, **wg)
    dw_pb = matmul(yb_in, dyab, name="wgrad_pb", n=D_MODEL, b_off=1, **wg)
    by_rows = lambda g: g.reshape(NDEV, g.shape[0] // NDEV, g.shape[1])
    (dxa, dlg, dwa_bd, dwx_bd, dvec), parts_ff = lru_bwd(
        dya_in, xa, proj, hst, lru_gates, wa_bd, wx_bd, vec, nb, seq, side=Scatter([dw_ff1, by_rows(dw_ff2)]))
    (dxbc, dz, ddt, hpg, lpg), parts_mg = ssd_bwd(xbc, proj, dtraw, hp, lp, tri, triu, states, ssd_kept, dyb_in, nb, seq,
                                                  side=Scatter([by_rows(dw_pa), by_rows(dw_pb), by_rows(dw_out)]))
    ddt_b = ddt.astype(BF16)
    accs = dict(vmg=vacc_mg, vmlp=vacc_mlp, dvec=dvec, hpg=hpg, lpg=lpg,
                gwa=_diag_blocks4(dwa_bd).reshape(LRU_HEADS * 64, 64), gwx=_diag_blocks4(dwx_bd).reshape(LRU_HEADS * 64, 64))
    (dw_main, dw_dt, dlx, dxr, acc_l, acc_s), g_small = in_proj_wgrad(
        h1t, proj, dxa, dxbc, dsilu, dlg, dz, dgates, ddt_b, cw_all, seq, side=Gather([accs[k] for k in SMALL_RIDE]))
    pieces = (dlx, dlg, dz, dxr, dgates)
    cut = lambda k: dw_main[:, k * shard - (SSD_HEADS if k > kd else 0):(k + 1) * shard - (SSD_HEADS if k >= kd else 0)]
    blk_dt = jnp.concatenate([dw_main[:, kd * shard:DT_COL0], dw_dt[:, :SSD_HEADS],
                              dw_main[:, DT_COL0:(kd + 1) * shard - SSD_HEADS]], axis=1)
    dw_blocks = jnp.stack([blk_dt if k == kd else cut(k) for k in range(NDEV)])
    (grad_x, vacc_in, dmod_in), parts_in = in_proj_bwd(pieces, ddt_b, dx1, x2, mod8, pre_norm1, w_main, w_dt, nb, seq,
                                                       side=Scatter([dw_blocks]))
    parts = dict(zip(BIG, (parts_in[0], *parts_mg, *parts_ff)))

    dmod = (dmod_in + dmod_mg + dmod_mlp)[:, :N_MOD].reshape(nb, N_MOD * D_MODEL)
    g3, g_vin, g_accl, g_accs = all_gather([jnp.pad(dmod, ((0, 8 - nb), (0, 0))), vacc_in, acc_l, acc_s], "gather_dmod")
    dmod_all = g3[:, :nb].reshape(NDEV * nb, N_MOD * D_MODEL)
    dmod_cols = lax.dynamic_slice(dmod_all, (0, me * ada_cols), (NDEV * nb, ada_cols))
    g_w_ada, g_b_ada = ada_bwd(c_all, dmod_cols, dmod_all)

    res = {}
    for k in BIG:
        res[k] = adamw(parts[k], w[k], mom[k], var[k], "adamw_" + k)
    res['w_ada'] = adamw(g_w_ada[None], w_ada, m_w_ada, v_w_ada, "adamw_w_ada")

    gathered = dict(zip(SMALL_RIDE, g_small), vin=g_vin, accl=g_accl, accs=g_accs)
    view = lambda a: a.reshape(-1, a.shape[-1])
    res_a, g_lru_cw, g_ssd_cw = adamw_small(gathered, {k: (view(w[k]), view(mom[k]), view(var[k])) for k in REPL})
    res.update(res_a)
    lcw, scw = lru_conv_w.shape[2], ssd_conv_w.shape[2]
    sharded = {'b_ada': g_b_ada[None], 'lru_conv_w': lax.dynamic_slice(g_lru_cw, (0, me * lcw), (4, lcw))[None],
               'ssd_conv_w': lax.dynamic_slice(g_ssd_cw, (0, me * scw), (4, scw))[None]}
    for k, g in sharded.items():
        as3 = lambda a: a.reshape(g.shape)
        res[k] = adamw(g, as3(w[k]), as3(mom[k]), as3(var[k]), "adamw_" + k)

    loss = lax.psum(loss8[0, 0], ("x", "y", "c"))
    outs = [[res[k][j].reshape(w[k].shape) for k in WEIGHTS] for j in range(4)]
    return (loss, grad_x.reshape(x.shape), *outs[0], *outs[1], *outs[2], *outs[3])
```

```python
import functools

import numpy as np
import jax
import jax.numpy as jnp
from jax import lax
from jax.experimental import pallas as pl
from jax.experimental.pallas import tpu as pltpu

F32 = jnp.float32
BF16 = jnp.bfloat16

D_MODEL = 1024
LRU_HEADS = 16
LRU_BLOCK = 256
LRU_C = 8.0
SSD_INNER = 2048
SSD_HEADS = 32
SSD_P = 64
SSD_G = 8
SSD_N = 128
SSD_L = 128
SSD_GW = SSD_INNER // SSD_G
D_FF = 4096
N_MOD = 6
EPS = 1e-6
NDEV = 8

C_LRU_X, C_LRU_G, C_Z, C_XBC, C_GATES, PROJ_MAIN = 0, 1024, 2048, 4096, 8192, 10240
IN_DIM = 10272
DT_COL0 = 8192
HALO = 16
SSD_FWD_CPS = 1
HT_TOK = 512

ADAM_LR, ADAM_B1, ADAM_B2, ADAM_EPS, ADAM_WD, ADAM_STEP = 0.001, 0.9, 0.999, 1e-08, 0.01, 10

VMEM_LIMIT = 60 * 1024 * 1024
MESH = pl.DeviceIdType.MESH
ANY = pl.BlockSpec(memory_space=pl.ANY)
VMEM_FULL = pl.BlockSpec(memory_space=pltpu.VMEM)


def _cp(*sem):
    return pltpu.CompilerParams(dimension_semantics=sem, vmem_limit_bytes=VMEM_LIMIT)


def _dot(a, b):
    return jnp.dot(a, b, preferred_element_type=F32)


def _dot_nt(a, b):
    return lax.dot_general(a, b, (((1,), (1,)), ((), ())), preferred_element_type=F32)


def _dot_tn(a, b):
    return lax.dot_general(a, b, (((0,), (0,)), ((), ())), preferred_element_type=F32)


def _dot_hi(a, b):
    return jnp.dot(a, b, precision=lax.Precision.HIGHEST, preferred_element_type=F32)


def _sigmoid(x):
    return 1.0 / (1.0 + jnp.exp(-x))


def _gelu_and_grad(x):
    k0, k1 = 0.7978845608028654, 0.044715
    t = jnp.tanh(k0 * (x + k1 * x * x * x))
    g = 0.5 * x * (1.0 + t)
    dg = 0.5 * (1.0 + t) + 0.5 * x * (1.0 - t * t) * k0 * (1.0 + 3.0 * k1 * x * x)
    return g, dg


def _neg_expm1(y):
    p = 1.0 + y * (1.0 / 7.0)
    p = 1.0 + y * (1.0 / 6.0) * p
    p = 1.0 + y * (1.0 / 5.0) * p
    p = 1.0 + y * (1.0 / 4.0) * p
    p = 1.0 + y * (1.0 / 3.0) * p
    p = 1.0 + y * 0.5 * p
    return jnp.where(y > -0.3, -y * p, 1.0 - jnp.exp(y))


def _colsum(v):
    return jnp.sum(v, axis=0, keepdims=True)


def _rowmean(v):
    return jnp.mean(v, axis=-1, keepdims=True)


def matmul(a, b, *, ta=False, tb=False, out_dtype=F32, tm, tn, tk, name, n=None, b_off=0, blocked_out=False):
    m = a.shape[1] if ta else a.shape[0]
    kdim = a.shape[0] if ta else a.shape[1]
    n = n or (b.shape[0] if tb else b.shape[1])
    tm, tn, tk = min(tm, m), min(tn, n), min(tk, kdim)
    nk = kdim // tk
    dn = (((0 if ta else 1,), (1 if tb else 0,)), ((), ()))
    bw = blocked_out or tn

    def body(a_ref, b_ref, o_ref, acc_ref):
        k = pl.program_id(2)
        p = lax.dot_general(a_ref[...], b_ref[...], dn, preferred_element_type=F32)

        def emit(v):
            if blocked_out:
                for q in range(tn // bw):
                    o_ref[q] = v[:, q * bw:(q + 1) * bw].astype(out_dtype)
            else:
                o_ref[...] = v.astype(out_dtype)

        if nk == 1:
            emit(p)
        else:
            @pl.when(k == 0)
            def _():
                acc_ref[...] = p

            @pl.when(k > 0)
            def _():
                acc_ref[...] += p

            @pl.when(k == nk - 1)
            def _():
                emit(acc_ref[...])

    a_spec = pl.BlockSpec((tk, tm), lambda i, j, k: (k, i)) if ta else pl.BlockSpec((tm, tk), lambda i, j, k: (i, k))
    b_spec = (pl.BlockSpec((tn, tk), lambda i, j, k: (j, k)) if tb
              else pl.BlockSpec((tk, tn), lambda i, j, k: (k, j + b_off)))
    if blocked_out:
        o_spec, o_shape = pl.BlockSpec((tn // bw, tm, bw), lambda i, j, k: (j, i, 0)), (n // bw, m, bw)
    else:
        o_spec, o_shape = pl.BlockSpec((tm, tn), lambda i, j, k: (i, j)), (m, n)
    return pl.pallas_call(
        body, name=name, grid=(m // tm, n // tn, nk),
        in_specs=[a_spec, b_spec], out_specs=o_spec,
        out_shape=jax.ShapeDtypeStruct(o_shape, out_dtype),
        scratch_shapes=[pltpu.VMEM((tm, tn), F32)],
        compiler_params=_cp("parallel", "parallel", "arbitrary"),
    )(a, b)


def _conv_tile(j, tn):
    return jnp.where(j == 0, 0, jnp.clip(j - C_XBC // tn + 1, 1, 2 * SSD_INNER // tn))


def in_proj_fwd(x2, mod8, pre1, w_main, w_dt, cw, cb, seq, side=None):
    t = x2.shape[0]
    tm = min(1024, seq)
    tn = 1024
    per_seq = seq // tm
    j_xbc = C_XBC // tn
    n_xbc = 2 * SSD_INNER // tn
    cs = 256

    def body(x_ref, mod_ref, pre_ref, w_ref, wdt_ref, cw_ref, cb_ref, proj_ref, h_ref, dt_ref, xa_ref, xbc_ref, ds_ref,
             h_scr, carry_scr):
        i, j = pl.program_id(0), pl.program_id(1)

        @pl.when(j == 0)
        def _():
            xv = x_ref[...]
            y = xv * lax.rsqrt(_rowmean(xv * xv) + EPS) * pre_ref[...]
            m = mod_ref[0]
            hf = y * (1.0 + m[1:2, :]) + m[0:1, :]
            h = hf.astype(BF16)
            h_scr[...] = h
            hft = hf.T.astype(BF16)
            for q in range(tm // HT_TOK):
                h_ref[q] = hft[:, q * HT_TOK:(q + 1) * HT_TOK]
            dt_ref[...] = _dot(h, wdt_ref[...])

        def project(c0=0, width=tn):
            pb = _dot(h_scr[...], w_ref[:, c0:c0 + width]).astype(BF16)
            proj_ref[:, c0:c0 + width] = pb
            return pb

        def conv(o_ref, slot, act):
            first = lax.rem(i, per_seq) == 0
            for c0 in range(0, tn, cs):
                cur = project(c0, cs).astype(F32)
                prev = jnp.where(first, 0.0, carry_scr[slot, :, c0:c0 + cs])
                carry_scr[slot, :, c0:c0 + cs] = cur[tm - HALO:, :]
                xx = jnp.concatenate([prev, cur], axis=0)
                w = cw_ref[:, c0:c0 + cs]
                acc = cur * w[3:4, :] + cb_ref[:, c0:c0 + cs]
                for d in (1, 2, 3):
                    acc = acc + pltpu.roll(xx, d, axis=0)[HALO:, :] * w[3 - d:4 - d, :]
                if act:
                    sg = _sigmoid(acc)
                    ds_ref[:, c0:c0 + cs] = (sg * (1.0 + acc * (1.0 - sg))).astype(BF16)
                    acc = acc * sg
                o_ref[:, c0:c0 + cs] = acc.astype(BF16)

        is_xbc = (j >= j_xbc) & (j < j_xbc + n_xbc)

        @pl.when(j == 0)
        def _():
            conv(xa_ref, 0, False)

        @pl.when(is_xbc)
        def _():
            conv(xbc_ref, j - j_xbc + 1, True)

        @pl.when((j > 0) & jnp.logical_not(is_xbc))
        def _():
            project()

    return _call(
        body, name="in_proj_fwd", grid=(t // tm, PROJ_MAIN // tn), side=side, sem=("arbitrary", "arbitrary"),
        args=(x2, mod8, pre1, w_main, w_dt, cw, cb),
        in_specs=[pl.BlockSpec((tm, D_MODEL), lambda i, j: (i, 0)),
                  pl.BlockSpec((1, 8, D_MODEL), lambda i, j: (i // per_seq, 0, 0)),
                  pl.BlockSpec((1, D_MODEL), lambda i, j: (0, 0)),
                  pl.BlockSpec((D_MODEL, tn), lambda i, j: (0, j)),
                  pl.BlockSpec((D_MODEL, 128), lambda i, j: (0, 0)),
                  pl.BlockSpec((4, tn), lambda i, j: (0, _conv_tile(j, tn))),
                  pl.BlockSpec((1, tn), lambda i, j: (0, _conv_tile(j, tn)))],
        out_specs=[pl.BlockSpec((tm, tn), lambda i, j: (i, j)),
                   pl.BlockSpec((tm // HT_TOK, D_MODEL, HT_TOK), lambda i, j: (i, 0, 0)),
                   pl.BlockSpec((tm, 128), lambda i, j: (i, 0)),
                   pl.BlockSpec((tm, tn), lambda i, j: (i, 0)),
                   pl.BlockSpec((tm, tn), lambda i, j: (i, jnp.clip(j - j_xbc, 0, n_xbc - 1))),
                   pl.BlockSpec((tm, tn), lambda i, j: (i, jnp.clip(j - j_xbc, 0, n_xbc - 1)))],
        out_shape=[jax.ShapeDtypeStruct((t, PROJ_MAIN), BF16), jax.ShapeDtypeStruct((t // HT_TOK, D_MODEL, HT_TOK), BF16),
                   jax.ShapeDtypeStruct((t, 128), F32), jax.ShapeDtypeStruct((t, D_MODEL), BF16),
                   jax.ShapeDtypeStruct((t, 2 * SSD_INNER), BF16), jax.ShapeDtypeStruct((t, 2 * SSD_INNER), BF16)],
        scratch_shapes=[pltpu.VMEM((tm, D_MODEL), BF16), pltpu.VMEM((1 + n_xbc, HALO, tn), F32)])


def _lru_gates(xa, wa_ref, wx_ref, ba, bx, sp):
    nblk = D_MODEL // LRU_BLOCK
    pr = jnp.concatenate([_dot(xa[:, j * LRU_BLOCK:(j + 1) * LRU_BLOCK], wa_ref[j]) for j in range(nblk)], axis=1) + ba
    pi = jnp.concatenate([_dot(xa[:, j * LRU_BLOCK:(j + 1) * LRU_BLOCK], wx_ref[j]) for j in range(nblk)], axis=1) + bx
    r = _sigmoid(pr)
    i = _sigmoid(pi)
    log_a = (-LRU_C * r) * sp
    return r, i, jnp.exp(log_a), _neg_expm1(2.0 * log_a)


def lru_fwd(xa, proj, wa_bd, wx_bd, vec, nb, seq):
    t = xa.shape[0]
    tc = min(512, seq)
    nk = seq // tc
    gb = C_LRU_G // D_MODEL

    def body(xa_ref, g_ref, wa_ref, wx_ref, vec_ref, ya_ref, h_ref, gates_ref, a_scr, u_scr, hc_scr):
        @pl.when(pl.program_id(1) == 0)
        def _():
            hc_scr[...] = jnp.zeros_like(hc_scr)

        xa_v = xa_ref[...]
        v = vec_ref[...]
        r, i, a, e = _lru_gates(xa_v, wa_ref, wx_ref, v[0:1, :], v[1:2, :], v[2:3, :])
        s = jnp.sqrt(e)
        gates_ref[...] = jnp.concatenate([r, i, a, s], axis=1)
        a_scr[...] = a
        u_scr[...] = s * (i * xa_v.astype(F32))
        row = lax.broadcasted_iota(jnp.int32, (8, 1), 0)

        def tile(j, h):
            r0 = pl.multiple_of(j * 8, 8)
            av, uv = a_scr[pl.ds(r0, 8), :], u_scr[pl.ds(r0, 8), :]
            for d in (1, 2, 4):
                uv = uv + av * jnp.where(row >= d, pltpu.roll(uv, d, axis=0), 0.0)
                av = av * jnp.where(row >= d, pltpu.roll(av, d, axis=0), 1.0)
            hv = uv + av * h
            h_ref[pl.ds(r0, 8), :] = hv
            return hv[7:8, :]

        hc_scr[...] = lax.fori_loop(0, tc // 8, tile, hc_scr[...], unroll=2)
        gel, _ = _gelu_and_grad(g_ref[...].astype(F32))
        ya_ref[...] = (h_ref[...] * gel).astype(BF16)

    return pl.pallas_call(
        body, name="lru_fwd", grid=(nb, nk),
        in_specs=[pl.BlockSpec((tc, D_MODEL), lambda b, k: (b * nk + k, 0)),
                  pl.BlockSpec((tc, D_MODEL), lambda b, k: (b * nk + k, gb)),
                  VMEM_FULL, VMEM_FULL, VMEM_FULL],
        out_specs=[pl.BlockSpec((tc, D_MODEL), lambda b, k: (b * nk + k, 0)),
                   pl.BlockSpec((tc, D_MODEL), lambda b, k: (b * nk + k, 0)),
                   pl.BlockSpec((tc, 4 * D_MODEL), lambda b, k: (b * nk + k, 0))],
        out_shape=[jax.ShapeDtypeStruct((t, D_MODEL), BF16), jax.ShapeDtypeStruct((t, D_MODEL), F32),
                   jax.ShapeDtypeStruct((t, 4 * D_MODEL), F32)],
        scratch_shapes=[pltpu.VMEM((tc, D_MODEL), F32), pltpu.VMEM((tc, D_MODEL), F32), pltpu.VMEM((1, D_MODEL), F32)],
        compiler_params=_cp("arbitrary", "arbitrary"),
    )(xa, proj, wa_bd, wx_bd, vec)


def lru_bwd(dya, xa, proj, h, gates, wa_bd, wx_bd, vec, nb, seq, side=None):
    t = xa.shape[0]
    tc = min(512, seq)
    nk = seq // tc
    gb = C_LRU_G // D_MODEL
    nblk = D_MODEL // LRU_BLOCK

    def chunk(b, k):
        return b * nk + (nk - 1 - k)

    def body(dya_ref, xa_ref, g_ref, h_ref, hp_ref, gates_ref, wa_ref, wx_ref, vec_ref,
             dxa_ref, dg_ref, dwa_ref, dwx_ref, dvec_ref, a_scr, dh_scr, c_scr):
        b, k = pl.program_id(0), pl.program_id(1)

        @pl.when((b == 0) & (k == 0))
        def _():
            dwa_ref[...] = jnp.zeros_like(dwa_ref)
            dwx_ref[...] = jnp.zeros_like(dwx_ref)
            dvec_ref[...] = jnp.zeros_like(dvec_ref)

        @pl.when(k == 0)
        def _():
            c_scr[...] = jnp.zeros_like(c_scr)

        xa_v = xa_ref[...]
        xaf = xa_v.astype(F32)
        v = vec_ref[...]
        sp = v[2:3, :]
        r, i = gates_ref[:, 0:D_MODEL], gates_ref[:, D_MODEL:2 * D_MODEL]
        a, s = gates_ref[:, 2 * D_MODEL:3 * D_MODEL], gates_ref[:, 3 * D_MODEL:]
        gel, dgel = _gelu_and_grad(g_ref[...].astype(F32))
        hv = h_ref[...]
        dyv = dya_ref[...].astype(F32)
        dg_ref[...] = (dyv * hv * dgel).astype(BF16)
        a_scr[...] = a
        dh_scr[...] = dyv * gel

        row8 = lax.broadcasted_iota(jnp.int32, (8, 1), 0)

        def tile(j, c):
            r0 = pl.multiple_of((tc // 8 - 1 - j) * 8, 8)
            av, dout = a_scr[pl.ds(r0, 8), :], dh_scr[pl.ds(r0, 8), :]
            zv = av * dout
            for d in (1, 2, 4):
                zv = zv + av * jnp.where(row8 < 8 - d, pltpu.roll(zv, 8 - d, axis=0), 0.0)
                av = av * jnp.where(row8 < 8 - d, pltpu.roll(av, 8 - d, axis=0), 1.0)
            zv = zv + av * c
            dh_scr[pl.ds(r0, 8), :] = dout + jnp.where(row8 < 7, pltpu.roll(zv, 7, axis=0), c)
            return zv[0:1, :]

        c_scr[...] = lax.fori_loop(0, tc // 8, tile, c_scr[...], unroll=2)
        dh = dh_scr[...]
        h_last = jnp.where(k == nk - 1, 0.0, hp_ref[HALO // 2 - 1:HALO // 2, :])
        row = lax.broadcasted_iota(jnp.int32, (tc, 1), 0)
        h_prev = jnp.where(row == 0, h_last, pltpu.roll(hv, 1, axis=0))
        da = dh * h_prev
        ix = i * xaf
        dlog_a = da * a - (dh * ix) * (a * a) / jnp.maximum(s, 1e-15)
        di = dh * s * xaf
        dpr = (dlog_a * (-LRU_C * sp)) * (r * (1.0 - r))
        dpi = di * (i * (1.0 - i))
        dprb, dpib = dpr.astype(BF16), dpi.astype(BF16)
        dxa = dh * s * i
        dxa = dxa + jnp.concatenate(
            [_dot_nt(dprb[:, j * LRU_BLOCK:(j + 1) * LRU_BLOCK], wa_ref[j])
             + _dot_nt(dpib[:, j * LRU_BLOCK:(j + 1) * LRU_BLOCK], wx_ref[j]) for j in range(nblk)], axis=1)
        dxa_ref[...] = dxa.astype(BF16)
        for j in range(nblk):
            sl = slice(j * LRU_BLOCK, (j + 1) * LRU_BLOCK)
            dwa_ref[j] += _dot_tn(xa_v[:, sl], dprb[:, sl])
            dwx_ref[j] += _dot_tn(xa_v[:, sl], dpib[:, sl])
        dvec_ref[...] += jnp.concatenate(
            [_colsum(dpr), _colsum(dpi), _colsum(dlog_a * (-LRU_C * r)), jnp.zeros((5, D_MODEL), F32)], axis=0)

    hh = HALO // 2
    return _call(
        body, name="lru_bwd", grid=(nb, nk), side=side, sem=("arbitrary", "arbitrary"),
        args=(dya, xa, proj, h, h, gates, wa_bd, wx_bd, vec),
        in_specs=[pl.BlockSpec((tc, D_MODEL), lambda b, k: (chunk(b, k), 0)),
                  pl.BlockSpec((tc, D_MODEL), lambda b, k: (chunk(b, k), 0)),
                  pl.BlockSpec((tc, D_MODEL), lambda b, k: (chunk(b, k), gb)),
                  pl.BlockSpec((tc, D_MODEL), lambda b, k: (chunk(b, k), 0)),
                  pl.BlockSpec((hh, D_MODEL), lambda b, k: (jnp.maximum(chunk(b, k) * (tc // hh) - 1, 0), 0)),
                  pl.BlockSpec((tc, 4 * D_MODEL), lambda b, k: (chunk(b, k), 0)),
                  VMEM_FULL, VMEM_FULL, VMEM_FULL],
        out_specs=[pl.BlockSpec((tc, D_MODEL), lambda b, k: (chunk(b, k), 0)),
                   pl.BlockSpec((tc, D_MODEL), lambda b, k: (chunk(b, k), 0)),
                   pl.BlockSpec((nblk, LRU_BLOCK, LRU_BLOCK), lambda b, k: (0, 0, 0)),
                   pl.BlockSpec((nblk, LRU_BLOCK, LRU_BLOCK), lambda b, k: (0, 0, 0)),
                   pl.BlockSpec((8, D_MODEL), lambda b, k: (0, 0))],
        out_shape=[jax.ShapeDtypeStruct((t, D_MODEL), BF16), jax.ShapeDtypeStruct((t, D_MODEL), BF16),
                   jax.ShapeDtypeStruct((nblk, LRU_BLOCK, LRU_BLOCK), F32),
                   jax.ShapeDtypeStruct((nblk, LRU_BLOCK, LRU_BLOCK), F32),
                   jax.ShapeDtypeStruct((8, D_MODEL), F32)],
        scratch_shapes=[pltpu.VMEM((tc, D_MODEL), F32), pltpu.VMEM((tc, D_MODEL), F32), pltpu.VMEM((1, D_MODEL), F32)])


def merge_fwd(ya_in, yb_in, proj, x2, mod8, bgate, post1, w_pa, w_pb, w_out, seq):
    t = x2.shape[0]
    tm = min(512, seq)
    per_seq = seq // tm
    gcb = C_GATES // SSD_INNER

    def body(ya_ref, yb_ref, gt_ref, x_ref, mod_ref, bg_ref, post_ref, wpa_ref, wpb_ref, wo_ref,
             yab_ref, out1_ref, x1_ref):
        y_a = _dot(ya_ref[...], wpa_ref[...])
        y_b = _dot(yb_ref[...], wpb_ref[...])
        g = _sigmoid(gt_ref[...].astype(F32) + bg_ref[...])
        merged = g[:, :D_MODEL] * y_a + g[:, D_MODEL:] * y_b
        out1 = _dot(merged.astype(BF16), wo_ref[...])
        n = out1 * lax.rsqrt(_rowmean(out1 * out1) + EPS)
        yab_ref[...] = jnp.concatenate([y_a, y_b], axis=1).astype(BF16)
        out1_ref[...] = out1
        x1_ref[...] = x_ref[...] + mod_ref[0][2:3, :] * (n * post_ref[...])

    row = lambda w: pl.BlockSpec((tm, w), lambda i: (i, 0))
    return pl.pallas_call(
        body, name="merge_fwd", grid=(t // tm,),
        in_specs=[row(D_MODEL), row(SSD_INNER), pl.BlockSpec((tm, SSD_INNER), lambda i: (i, gcb)), row(D_MODEL),
                  pl.BlockSpec((1, 8, D_MODEL), lambda i: (i // per_seq, 0, 0)),
                  VMEM_FULL, VMEM_FULL, VMEM_FULL, VMEM_FULL, VMEM_FULL],
        out_specs=[row(SSD_INNER), row(D_MODEL), row(D_MODEL)],
        out_shape=[jax.ShapeDtypeStruct((t, SSD_INNER), BF16), jax.ShapeDtypeStruct((t, D_MODEL), F32),
                   jax.ShapeDtypeStruct((t, D_MODEL), F32)],
        compiler_params=_cp("parallel"),
    )(ya_in, yb_in, proj, x2, mod8, bgate, post1, w_pa, w_pb, w_out)


def merge_bwd(dx1, out1, yab, proj, mod8, bgate, post1, w_pa, w_pb, w_out, nb, seq):
    t = dx1.shape[0]
    tm = min(512, seq)
    per_seq = seq // tm
    gcb = C_GATES // SSD_INNER

    def body(dx1_ref, out1_ref, yab_ref, gt_ref, mod_ref, bg_ref, post_ref, wpa_ref, wpb_ref, wo_ref,
             dya_ref, dyb_ref, dgt_ref, dyab_ref, dout1_ref, mg_ref, vacc_ref, dmod_ref):
        b, s = pl.program_id(0), pl.program_id(1)

        @pl.when((b == 0) & (s == 0))
        def _():
            vacc_ref[...] = jnp.zeros_like(vacc_ref)

        @pl.when(s == 0)
        def _():
            dmod_ref[...] = jnp.zeros_like(dmod_ref)

        dx1v = dx1_ref[...]
        out1 = out1_ref[...]
        post = post_ref[...]
        rs = lax.rsqrt(_rowmean(out1 * out1) + EPS)
        n = out1 * rs
        do = dx1v * mod_ref[0][2:3, :]
        dn = do * post
        dout1 = rs * (dn - n * _rowmean(dn * n))
        dout1b = dout1.astype(BF16)
        dout1_ref[...] = dout1b
        dmerged = _dot_nt(dout1b, wo_ref[...])
        g = _sigmoid(gt_ref[...].astype(F32) + bg_ref[...])
        yab_v = yab_ref[...].astype(F32)
        gy = g * yab_v
        mg_ref[...] = (gy[:, :D_MODEL] + gy[:, D_MODEL:]).astype(BF16)
        dm2 = jnp.concatenate([dmerged, dmerged], axis=1)
        dyab = (dm2 * g).astype(BF16)
        dyab_ref[...] = dyab
        dgt = dm2 * gy * (1.0 - g)
        dgt_ref[...] = dgt.astype(BF16)
        dya_ref[...] = _dot_nt(dyab[:, :D_MODEL], wpa_ref[...]).astype(BF16)
        dyb_ref[...] = _dot_nt(dyab[:, D_MODEL:], wpb_ref[...]).astype(BF16)
        vacc_ref[...] += jnp.concatenate(
            [_colsum(dgt), jnp.concatenate([_colsum(do * n), jnp.zeros((1, D_MODEL), F32)], axis=1),
             jnp.zeros((6, SSD_INNER), F32)], axis=0)
        dmod_ref[0] += jnp.concatenate(
            [jnp.zeros((2, D_MODEL), F32), _colsum(dx1v * (n * post)), jnp.zeros((5, D_MODEL), F32)], axis=0)

    row = lambda w: pl.BlockSpec((tm, w), lambda b, s: (b * per_seq + s, 0))
    return pl.pallas_call(
        body, name="merge_bwd", grid=(nb, per_seq),
        in_specs=[row(D_MODEL), row(D_MODEL), row(SSD_INNER),
                  pl.BlockSpec((tm, SSD_INNER), lambda b, s: (b * per_seq + s, gcb)),
                  pl.BlockSpec((1, 8, D_MODEL), lambda b, s: (b, 0, 0)),
                  VMEM_FULL, VMEM_FULL, VMEM_FULL, VMEM_FULL, VMEM_FULL],
        out_specs=[row(D_MODEL), row(SSD_INNER), row(SSD_INNER), row(SSD_INNER), row(D_MODEL), row(D_MODEL),
                   pl.BlockSpec((8, SSD_INNER), lambda b, s: (0, 0)),
                   pl.BlockSpec((1, 8, D_MODEL), lambda b, s: (b, 0, 0))],
        out_shape=[jax.ShapeDtypeStruct((t, D_MODEL), BF16), jax.ShapeDtypeStruct((t, SSD_INNER), BF16),
                   jax.ShapeDtypeStruct((t, SSD_INNER), BF16), jax.ShapeDtypeStruct((t, SSD_INNER), BF16),
                   jax.ShapeDtypeStruct((t, D_MODEL), BF16), jax.ShapeDtypeStruct((t, D_MODEL), BF16),
                   jax.ShapeDtypeStruct((8, SSD_INNER), F32), jax.ShapeDtypeStruct((nb, 8, D_MODEL), F32)],
        compiler_params=_cp("arbitrary", "arbitrary"),
    )(dx1, out1, yab, proj, mod8, bgate, post1, w_pa, w_pb, w_out)


def mlp_fwd_bwd(x1, tgt, mod8, pre2, post2, w_ff1, w_ff2, nb, seq):
    t = x1.shape[0]
    tm = min(256, seq)
    per_seq = seq // tm
    fc = 1024
    nfc = D_FF // fc

    def body(x1_ref, tgt_ref, mod_ref, pre_ref, post_ref, w1_ref, w2_ref,
             dx1_ref, h2_ref, da1_ref, act_ref, dy2_ref, loss_ref, vacc_ref, dmod_ref, r_scr):
        b, s = pl.program_id(0), pl.program_id(1)
        per = fc // w1_ref.shape[2]

        def w1_cols(c):
            return jnp.concatenate([w1_ref[per * c + q] for q in range(per)], axis=1)

        @pl.when((b == 0) & (s == 0))
        def _():
            vacc_ref[...] = jnp.zeros_like(vacc_ref)
            loss_ref[...] = jnp.zeros_like(loss_ref)

        @pl.when(s == 0)
        def _():
            dmod_ref[...] = jnp.zeros_like(dmod_ref)

        m = mod_ref[0]
        sh2, sc2, g2 = m[3:4, :], m[4:5, :], m[5:6, :]
        pre, post = pre_ref[...], post_ref[...]
        x1v = x1_ref[...]
        rs1 = lax.rsqrt(_rowmean(x1v * x1v) + EPS)
        n1 = x1v * rs1
        y1 = n1 * pre
        h2b = (y1 * (1.0 + sc2) + sh2).astype(BF16)
        h2_ref[...] = h2b
        y2 = jnp.zeros((tm, D_MODEL), F32)
        for c in range(nfc):
            r = jnp.maximum(_dot(h2b, w1_cols(c)), 0.0)
            r_scr[:, c * fc:(c + 1) * fc] = r
            a = (r * r).astype(BF16)
            act_ref[:, c * fc:(c + 1) * fc] = a
            y2 = y2 + _dot(a, w2_ref[c * fc:(c + 1) * fc, :])
        rs2 = lax.rsqrt(_rowmean(y2 * y2) + EPS)
        n2 = y2 * rs2
        o2 = n2 * post
        diff = x1v + g2 * o2 - tgt_ref[...]
        loss_ref[...] += 0.5 * jnp.sum(_rowmean(diff * diff))
        dx2 = diff * (1.0 / D_MODEL)
        do2 = dx2 * g2
        dn2 = do2 * post
        dy2b = (rs2 * (dn2 - n2 * _rowmean(dn2 * n2))).astype(BF16)
        dy2_ref[...] = dy2b
        dh2 = jnp.zeros((tm, D_MODEL), F32)
        for c in range(nfc):
            dact = _dot_nt(dy2b, w2_ref[c * fc:(c + 1) * fc, :])
            da = (dact * (2.0 * r_scr[:, c * fc:(c + 1) * fc])).astype(BF16)
            da1_ref[:, c * fc:(c + 1) * fc] = da
            dh2 = dh2 + _dot_nt(da, w1_cols(c))
        dy1 = dh2 * (1.0 + sc2)
        dn1 = dy1 * pre
        dx1_ref[...] = dx2 + rs1 * (dn1 - n1 * _rowmean(dn1 * n1))
        vacc_ref[...] += jnp.concatenate([_colsum(dy1 * n1), _colsum(do2 * n2), jnp.zeros((6, D_MODEL), F32)], axis=0)
        dmod_ref[0] += jnp.concatenate(
            [jnp.zeros((3, D_MODEL), F32), _colsum(dh2), _colsum(dh2 * y1), _colsum(dx2 * o2),
             jnp.zeros((2, D_MODEL), F32)], axis=0)

    row = lambda w: pl.BlockSpec((tm, w), lambda b, s: (b * per_seq + s, 0))
    return pl.pallas_call(
        body, name="mlp_fwd_bwd", grid=(nb, per_seq),
        in_specs=[row(D_MODEL), row(D_MODEL), pl.BlockSpec((1, 8, D_MODEL), lambda b, s: (b, 0, 0)),
                  VMEM_FULL, VMEM_FULL, VMEM_FULL, VMEM_FULL],
        out_specs=[row(D_MODEL), row(D_MODEL), row(D_FF), row(D_FF), row(D_MODEL),
                   pl.BlockSpec((8, 128), lambda b, s: (0, 0)),
                   pl.BlockSpec((8, D_MODEL), lambda b, s: (0, 0)),
                   pl.BlockSpec((1, 8, D_MODEL), lambda b, s: (b, 0, 0))],
        out_shape=[jax.ShapeDtypeStruct((t, D_MODEL), F32), jax.ShapeDtypeStruct((t, D_MODEL), BF16),
                   jax.ShapeDtypeStruct((t, D_FF), BF16), jax.ShapeDtypeStruct((t, D_FF), BF16),
                   jax.ShapeDtypeStruct((t, D_MODEL), BF16), jax.ShapeDtypeStruct((8, 128), F32),
                   jax.ShapeDtypeStruct((8, D_MODEL), F32), jax.ShapeDtypeStruct((nb, 8, D_MODEL), F32)],
        scratch_shapes=[pltpu.VMEM((tm, D_FF), F32)],
        compiler_params=_cp("arbitrary", "arbitrary"),
    )(x1, tgt, mod8, pre2, post2, w_ff1, w_ff2)


_PIECES = ((C_LRU_X, 1024), (C_LRU_G, 1024), (C_Z, 2048), (C_XBC, 4096), (C_GATES, 2048))
_NP = len(_PIECES)


def in_proj_bwd(pieces, ddt, dx1, x2, mod8, pre1, w_main, w_dt, nb, seq, side=None):
    t = x2.shape[0]
    tm = min(512, seq)
    per_seq = seq // tm
    widths = [min(w, 2048) for _, w in _PIECES]
    steps = [(p, q) for p, (_, w) in enumerate(_PIECES) for q in range(w // widths[p])]
    nk = len(steps)

    def piece_spec(p):
        first = min(k for k in range(nk) if steps[k][0] == p)
        nblk = _PIECES[p][1] // widths[p]
        return pl.BlockSpec((tm, widths[p]), lambda b, s, k: (b * per_seq + s, jnp.clip(k - first, 0, nblk - 1)))

    def body(*refs):
        prefs = refs[:_NP]
        ddt_ref, dx1_ref, x_ref, mod_ref, pre_ref, w_ref, wdt_ref, gx_ref, vacc_ref, dmod_ref, acc_ref = refs[_NP:]
        b, s, k = pl.program_id(0), pl.program_id(1), pl.program_id(2)

        @pl.when((b == 0) & (s == 0) & (k == 0))
        def _():
            vacc_ref[...] = jnp.zeros_like(vacc_ref)

        @pl.when((s == 0) & (k == 0))
        def _():
            dmod_ref[...] = jnp.zeros_like(dmod_ref)

        @pl.when(k == 0)
        def _():
            acc_ref[...] = _dot_nt(ddt_ref[...], wdt_ref[...])

        for kk, (p, q) in enumerate(steps):
            @pl.when(k == kk)
            def _(p=p, q=q):
                c0 = _PIECES[p][0] + q * widths[p]
                acc_ref[...] += _dot_nt(prefs[p][...], w_ref[:, c0:c0 + widths[p]])

        @pl.when(k == nk - 1)
        def _():
            dh = acc_ref[...]
            m = mod_ref[0]
            pre = pre_ref[...]
            xv = x_ref[...]
            rs = lax.rsqrt(_rowmean(xv * xv) + EPS)
            n = xv * rs
            dy = dh * (1.0 + m[1:2, :])
            dn = dy * pre
            gx_ref[...] = dx1_ref[...] + rs * (dn - n * _rowmean(dn * n))
            vacc_ref[...] += jnp.concatenate([_colsum(dy * n), jnp.zeros((7, D_MODEL), F32)], axis=0)
            dmod_ref[0] += jnp.concatenate([_colsum(dh), _colsum(dh * (n * pre)), jnp.zeros((6, D_MODEL), F32)], axis=0)

    row = lambda w: pl.BlockSpec((tm, w), lambda b, s, k: (b * per_seq + s, 0))
    return _call(
        body, name="in_proj_bwd", grid=(nb, per_seq, nk), side=side, sem=("arbitrary", "arbitrary", "arbitrary"),
        args=(*pieces, ddt, dx1, x2, mod8, pre1, w_main, w_dt),
        in_specs=[piece_spec(p) for p in range(_NP)] + [
            row(128), row(D_MODEL), row(D_MODEL), pl.BlockSpec((1, 8, D_MODEL), lambda b, s, k: (b, 0, 0)),
            pl.BlockSpec((1, D_MODEL), lambda b, s, k: (0, 0)),
            VMEM_FULL,
            pl.BlockSpec((D_MODEL, 128), lambda b, s, k: (0, 0))],
        out_specs=[row(D_MODEL), pl.BlockSpec((8, D_MODEL), lambda b, s, k: (0, 0)),
                   pl.BlockSpec((1, 8, D_MODEL), lambda b, s, k: (b, 0, 0))],
        out_shape=[jax.ShapeDtypeStruct((t, D_MODEL), F32), jax.ShapeDtypeStruct((8, D_MODEL), F32),
                   jax.ShapeDtypeStruct((nb, 8, D_MODEL), F32)],
        scratch_shapes=[pltpu.VMEM((tm, D_MODEL), F32)])


def in_proj_wgrad(h1t, proj, dxa, dxbc, dsilu, dlg, dz, dgates, ddt, cw, seq, side=None):
    nt, _, tt = h1t.shape
    t = nt * tt
    tn = 1024
    nn = PROJ_MAIN // tn
    ns = seq // tt
    nh = t // HALO
    j_g, j_z, j_x, j_gt = C_LRU_G // tn, C_Z // tn, C_XBC // tn, C_GATES // tn
    n_x = 2 * SSD_INNER // tn
    strip = 256
    ne = tt + HALO

    def body(h_ref, cur_ref, prev_ref, next_ref, dxa_ref, dxan_ref, dxb_ref, dxbn_ref, ds_ref, dsn_ref, dlg_ref, dz_ref,
             dgt_ref, ddt_ref, cw_ref, dw_ref, dwdt_ref, dlx_ref, dxr_ref, accl_ref, accs_ref, acc_ref, accdt_ref):
        n, k = pl.program_id(0), pl.program_id(1)
        hv = h_ref[k]
        is_x = (n >= j_x) & (n < j_x + n_x)

        @pl.when(k == 0)
        def _():
            acc_ref[...] = jnp.zeros_like(acc_ref)

        @pl.when((n == 0) & (k == 0))
        def _():
            accdt_ref[...] = jnp.zeros_like(accdt_ref)
            accl_ref[...] = jnp.zeros_like(accl_ref)

        @pl.when(is_x & (k == 0))
        def _():
            accs_ref[...] = jnp.zeros_like(accs_ref)

        def conv_tile(do_ref, don_ref, out_ref, cacc_ref, act):
            first = lax.rem(k, ns) == 0
            last = lax.rem(k, ns) == ns - 1
            for c0 in range(0, tn, strip):
                cs = slice(c0, c0 + strip)
                xx = jnp.concatenate([jnp.where(first, 0.0, prev_ref[:, cs].astype(F32)), cur_ref[:, cs].astype(F32),
                                      next_ref[:, cs].astype(F32)], axis=0)
                do_ext = jnp.concatenate([do_ref[:, cs].astype(F32),
                                          jnp.where(last, 0.0, don_ref[:, cs].astype(F32))], axis=0)
                w = cw_ref[:, cs]
                xs = [xx[HALO:HALO + ne, :]] + [pltpu.roll(xx, d, axis=0)[HALO:HALO + ne, :] for d in (1, 2, 3)]
                if act:
                    dc = do_ext * jnp.concatenate([ds_ref[:, cs].astype(F32), dsn_ref[:, cs].astype(F32)], axis=0)
                else:
                    dc = do_ext
                dx = dc[:tt, :] * w[3:4, :]
                for d in (1, 2, 3):
                    dx = dx + pltpu.roll(dc, ne - d, axis=0)[:tt, :] * w[3 - d:4 - d, :]
                dxb = dx.astype(BF16)
                out_ref[:, cs] = dxb
                acc_ref[:, cs] += _dot(hv, dxb)
                dcc = dc[:tt, :]
                rows = [_colsum(dcc * xs[3 - r][:tt, :]) for r in range(4)] + [_colsum(dcc)]
                cacc_ref[:, cs] += jnp.concatenate(rows + [jnp.zeros((3, strip), F32)], axis=0)

        @pl.when(n == 0)
        def _():
            conv_tile(dxa_ref, dxan_ref, dlx_ref, accl_ref, False)
            accdt_ref[...] += _dot(hv, ddt_ref[...])

        @pl.when(is_x)
        def _():
            conv_tile(dxb_ref, dxbn_ref, dxr_ref, accs_ref, True)

        @pl.when(n == j_g)
        def _():
            acc_ref[...] += _dot(hv, dlg_ref[...])

        @pl.when((n >= j_z) & (n < j_x))
        def _():
            acc_ref[...] += _dot(hv, dz_ref[...])

        @pl.when(n >= j_gt)
        def _():
            acc_ref[...] += _dot(hv, dgt_ref[...])

        @pl.when(k == nt - 1)
        def _():
            dw_ref[...] = acc_ref[...].astype(BF16)

        @pl.when((n == 0) & (k == nt - 1))
        def _():
            dwdt_ref[...] = accdt_ref[...].astype(BF16)

    conv_n = lambda n: (n == 0) | ((n >= j_x) & (n < j_x + n_x))
    src_col = lambda n: jnp.where(n == 0, 0, jnp.clip(n, j_x, j_x + n_x - 1))
    ctile = lambda n: jnp.where(n == 0, 0, jnp.clip(n - j_x + 1, 1, n_x))
    xcol = lambda n: jnp.clip(n - j_x, 0, n_x - 1)
    on = lambda cond, k: jnp.where(cond, k, 0)
    nxt = lambda k: jnp.minimum(((k + 1) * tt) // HALO, nh - 1)
    after = lambda cond_during, cond_after, k: jnp.where(cond_during, k, jnp.where(cond_after, nt - 1, 0))
    in_specs = [
        VMEM_FULL,
        pl.BlockSpec((tt, tn), lambda n, k: (on(conv_n(n), k), src_col(n))),
        pl.BlockSpec((HALO, tn), lambda n, k: (on(conv_n(n), jnp.maximum((k * tt) // HALO - 1, 0)), src_col(n))),
        pl.BlockSpec((HALO, tn), lambda n, k: (on(conv_n(n), nxt(k)), src_col(n))),
        pl.BlockSpec((tt, tn), lambda n, k: (on(n == 0, k), 0)),
        pl.BlockSpec((HALO, tn), lambda n, k: (on(n == 0, nxt(k)), 0)),
        pl.BlockSpec((tt, tn), lambda n, k: (on((n >= j_x) & (n < j_x + n_x), k), xcol(n))),
        pl.BlockSpec((HALO, tn), lambda n, k: (on((n >= j_x) & (n < j_x + n_x), nxt(k)), xcol(n))),
        pl.BlockSpec((tt, tn), lambda n, k: (on((n >= j_x) & (n < j_x + n_x), k), xcol(n))),
        pl.BlockSpec((HALO, tn), lambda n, k: (on((n >= j_x) & (n < j_x + n_x), nxt(k)), xcol(n))),
        pl.BlockSpec((tt, tn), lambda n, k: (on(n == j_g, k), 0)),
        pl.BlockSpec((tt, tn), lambda n, k: (on((n >= j_z) & (n < j_x), k), jnp.clip(n - j_z, 0, j_x - j_z - 1))),
        pl.BlockSpec((tt, tn), lambda n, k: (on(n >= j_gt, k), jnp.clip(n - j_gt, 0, nn - j_gt - 1))),
        pl.BlockSpec((tt, 128), lambda n, k: (on(n == 0, k), 0)),
        pl.BlockSpec((4, tn), lambda n, k: (0, ctile(n)))]
    out_specs = [
        pl.BlockSpec((D_MODEL, tn), lambda n, k: (0, n)),
        pl.BlockSpec((D_MODEL, 128), lambda n, k: (0, 0)),
        pl.BlockSpec((tt, tn), lambda n, k: (after(n == 0, n > 0, k), 0)),
        pl.BlockSpec((tt, tn), lambda n, k: (after((n >= j_x) & (n < j_x + n_x), n >= j_x + n_x, k), xcol(n))),
        pl.BlockSpec((8, tn), lambda n, k: (0, 0)),
        pl.BlockSpec((8, tn), lambda n, k: (0, xcol(n)))]
    return _call(
        body, name="in_proj_wgrad", grid=(nn, nt), side=side, sem=("arbitrary", "arbitrary"),
        args=(h1t, proj, proj, proj, dxa, dxa, dxbc, dxbc, dsilu, dsilu, dlg, dz, dgates, ddt, cw),
        in_specs=in_specs, out_specs=out_specs,
        out_shape=[jax.ShapeDtypeStruct((D_MODEL, PROJ_MAIN), BF16), jax.ShapeDtypeStruct((D_MODEL, 128), BF16),
                   jax.ShapeDtypeStruct((t, D_MODEL), BF16), jax.ShapeDtypeStruct((t, 2 * SSD_INNER), BF16),
                   jax.ShapeDtypeStruct((8, D_MODEL), F32), jax.ShapeDtypeStruct((8, 2 * SSD_INNER), F32)],
        scratch_shapes=[pltpu.VMEM((D_MODEL, tn), F32), pltpu.VMEM((D_MODEL, 128), F32)])


def _log1p(u):
    w = 1.0 + u
    return jnp.log(w) - ((w - 1.0) - u) / w


def _softplus(x):
    return jnp.maximum(x, 0.0) + _log1p(jnp.exp(-jnp.abs(x)))


def _head_mask(h):
    lane = lax.broadcasted_iota(jnp.int32, (1, SSD_GW), 1)
    return (lane >= SSD_P * h) & (lane < SSD_P * (h + 1))


def _pair(p):
    return slice(2 * SSD_P * p, 2 * SSD_P * (p + 1))


def _expand4(m, g):
    lane = lax.broadcasted_iota(jnp.int32, (1, SSD_GW), 1)
    col = lambda h: m[:, 4 * g + h:4 * g + h + 1]
    return jnp.where(lane < SSD_P, col(0), jnp.where(lane < 2 * SSD_P, col(1), jnp.where(lane < 3 * SSD_P, col(2), col(3))))


def _reduce4(v, g):
    lane = lax.broadcasted_iota(jnp.int32, (1, SSD_N), 1)
    out = jnp.zeros((v.shape[0], SSD_N), F32)
    for h in range(4):
        s = jnp.sum(jnp.where(_head_mask(h), v, 0.0), axis=1, keepdims=True)
        out = out + jnp.where(lane == 4 * g + h, s, 0.0)
    return out


def _ssd_heads(dtraw, hp, tri):
    xdt = dtraw + hp[0:1, :]
    dt = _softplus(xdt)
    cs = _dot_hi(tri, dt * hp[1:2, :])
    cs_last = cs[SSD_L - 1:SSD_L, :]
    return dict(xdt=xdt, dt=dt, cs=cs, cs_t=cs.T, e=jnp.exp(cs), w=jnp.exp(cs_last - cs), el=jnp.exp(cs_last))


def _ssd_group(g, hd, xs_b, bm_b, cm_b, d_x, st, paired=False, kept=None, kept_dk=None):
    ll = SSD_L
    xs = xs_b.astype(F32)
    cs, cs_t = hd["cs"], hd["cs_t"]
    el_x = _expand4(hd["el"], g)
    e_x, w_x, dt_x = kept if kept is not None else (_expand4(hd["e"], g), _expand4(hd["w"], g), _expand4(hd["dt"], g))
    xd = xs * dt_x
    gcb = _dot_nt(cm_b, bm_b)
    ri = lax.broadcasted_iota(jnp.int32, (ll, ll), 0)
    ci = lax.broadcasted_iota(jnp.int32, (ll, ll), 1)
    dks, ms = [], []
    for h in range(4):
        k = 4 * g + h
        dk = (kept_dk[h] if kept_dk is not None
              else jnp.exp(jnp.where(ri >= ci, cs[:, k:k + 1] - cs_t[k:k + 1, :], -1e30)))
        dks.append(dk)
        ms.append((gcb * dk).astype(BF16))
    xdb = xd.astype(BF16)
    if paired:
        first = lax.broadcasted_iota(jnp.int32, (1, 2 * SSD_P), 1) < SSD_P
        ydiag = jnp.concatenate(
            [jnp.where(first, _dot(ms[2 * p], xdb[:, _pair(p)]), _dot(ms[2 * p + 1], xdb[:, _pair(p)]))
             for p in range(2)], axis=1)
    else:
        ydiag = jnp.zeros((ll, SSD_GW), F32)
        for h in range(4):
            ydiag = ydiag + _dot(ms[h], jnp.where(_head_mask(h), xd, 0.0).astype(BF16))
    yoff = _dot(cm_b, st.astype(BF16)) * e_x
    y = ydiag + yoff + d_x * xs
    st_new = st * el_x + _dot(bm_b.astype(F32).T.astype(BF16), (xd * w_x).astype(BF16))
    return dict(xs=xs, e_x=e_x, w_x=w_x, el_x=el_x, dt_x=dt_x, xd=xd, xdb=xdb, gcb=gcb, dks=dks, ms=ms, yoff=yoff, y=y,
                st_new=st_new)


def ssd_consts():
    hh = np.arange(SSD_N)
    tri = (hh[:, None] >= hh[None, :]).astype(np.float32)
    return jnp.asarray(tri), jnp.asarray(tri.T)


def ssd_params(dt_bias, a_log, d_skip, norm_w):
    padh = lambda v: jnp.pad(v.reshape(1, SSD_HEADS), ((0, 0), (0, SSD_N - SSD_HEADS)))
    hp = jnp.concatenate([padh(dt_bias), padh(-jnp.exp(a_log)), jnp.zeros((6, SSD_N), F32)], axis=0)
    lp = jnp.concatenate([norm_w.reshape(1, SSD_INNER), jnp.repeat(d_skip, SSD_P).reshape(1, SSD_INNER),
                          jnp.zeros((6, SSD_INNER), F32)], axis=0)
    return hp, lp


def _b_cols(g):
    return slice(SSD_INNER + g * SSD_N, SSD_INNER + (g + 1) * SSD_N)


def _c_cols(g):
    return slice(SSD_INNER + (SSD_G + g) * SSD_N, SSD_INNER + (SSD_G + g + 1) * SSD_N)


def _ssd_specs(nc, rc, cps=1):
    rows = cps * SSD_L
    return [pl.BlockSpec((rows, 2 * SSD_INNER), lambda b, c: (b * nc + rc(c), 0)),
            pl.BlockSpec((rows, SSD_INNER), lambda b, c: (b * nc + rc(c), C_Z // SSD_INNER)),
            pl.BlockSpec((rows, SSD_N), lambda b, c: (b * nc + rc(c), 0))]


def ssd_fwd(xbc, proj, dtraw, hp, lp, tri, nb, seq):
    t = xbc.shape[0]
    cps = SSD_FWD_CPS
    nc = seq // (cps * SSD_L)

    def body(xbc_ref, z_ref, dt_ref, hp_ref, lp_ref, tri_ref, y_ref, sts_ref, kept_ref, dk_ref, st_scr):
        @pl.when(pl.program_id(1) == 0)
        def _():
            st_scr[...] = jnp.zeros_like(st_scr)

        for cc in range(cps):
            rs = slice(cc * SSD_L, (cc + 1) * SSD_L)
            hd = _ssd_heads(dt_ref[rs, :], hp_ref[...], tri_ref[...])
            for g in range(SSD_G):
                gs = slice(g * SSD_GW, (g + 1) * SSD_GW)
                st = st_scr[g]
                sts_ref[cc, g] = st
                f = _ssd_group(g, hd, xbc_ref[rs, gs], xbc_ref[rs, _b_cols(g)], xbc_ref[rs, _c_cols(g)],
                               lp_ref[1:2, gs], st, paired=True)
                st_scr[g] = f["st_new"]
                for q, name in enumerate(("e_x", "w_x", "dt_x")):
                    kept_ref[rs, q * SSD_INNER + g * SSD_GW:q * SSD_INNER + (g + 1) * SSD_GW] = f[name]
                for h in range(4):
                    dk_ref[cc, 4 * g + h] = f["dks"][h]
                zf = z_ref[rs, gs].astype(F32)
                yg = f["y"] * (zf * _sigmoid(zf))
                y_ref[rs, gs] = (yg * lax.rsqrt(_rowmean(yg * yg) + EPS) * lp_ref[0:1, gs]).astype(BF16)

    return pl.pallas_call(
        body, name="ssd_fwd", grid=(nb, nc),
        in_specs=_ssd_specs(nc, lambda c: c, cps) + [VMEM_FULL, VMEM_FULL, VMEM_FULL],
        out_specs=[pl.BlockSpec((cps * SSD_L, SSD_INNER), lambda b, c: (b * nc + c, 0)),
                   pl.BlockSpec((cps, SSD_G, SSD_N, SSD_GW), lambda b, c: (b * nc + c, 0, 0, 0)),
                   pl.BlockSpec((cps * SSD_L, 3 * SSD_INNER), lambda b, c: (b * nc + c, 0)),
                   pl.BlockSpec((cps, SSD_HEADS, SSD_L, SSD_L), lambda b, c: (b * nc + c, 0, 0, 0))],
        out_shape=[jax.ShapeDtypeStruct((t, SSD_INNER), BF16),
                   jax.ShapeDtypeStruct((nb * nc * cps, SSD_G, SSD_N, SSD_GW), F32),
                   jax.ShapeDtypeStruct((t, 3 * SSD_INNER), F32),
                   jax.ShapeDtypeStruct((nb * nc * cps, SSD_HEADS, SSD_L, SSD_L), F32)],
        scratch_shapes=[pltpu.VMEM((SSD_G, SSD_N, SSD_GW), F32)],
        compiler_params=_cp("arbitrary", "arbitrary"),
    )(xbc, proj, dtraw, hp, lp, tri)


def ssd_bwd(xbc, proj, dtraw, hp, lp, tri, triu, states, kept, kept_dk, dyn, nb, seq, side=None):
    t = xbc.shape[0]
    nc = seq // SSD_L
    ll = SSD_L

    def body(xbc_ref, z_ref, dt_ref, sts_ref, dy_ref, kept_ref, dk_ref, hp_ref, lp_ref, tri_ref, triu_ref,
             dxbc_ref, dz_ref, ddt_ref, hpg_ref, lpg_ref, dst_scr):
        b, c_i = pl.program_id(0), pl.program_id(1)

        @pl.when((b == 0) & (c_i == 0))
        def _():
            hpg_ref[...] = jnp.zeros_like(hpg_ref)
            lpg_ref[...] = jnp.zeros_like(lpg_ref)

        @pl.when(c_i == 0)
        def _():
            dst_scr[...] = jnp.zeros_like(dst_scr)

        hp = hp_ref[...]
        hd = _ssd_heads(dt_ref[...], hp, tri_ref[...])
        lane = lax.broadcasted_iota(jnp.int32, (1, SSD_N), 1)
        subl = lax.broadcasted_iota(jnp.int32, (SSD_N, 1), 0)
        dcs = jnp.zeros((ll, SSD_N), F32)
        dcs_t = jnp.zeros((SSD_N, ll), F32)
        last = jnp.zeros((1, SSD_N), F32)
        dxx = jnp.zeros((ll, SSD_N), F32)
        for g in range(SSD_G):
            gs = slice(g * SSD_GW, (g + 1) * SSD_GW)
            st = sts_ref[0, g]
            dst = dst_scr[g]
            bm_b, cm_b = xbc_ref[:, _b_cols(g)], xbc_ref[:, _c_cols(g)]
            d_x = lp_ref[1:2, gs]
            kept = tuple(kept_ref[:, q * SSD_INNER + g * SSD_GW:q * SSD_INNER + (g + 1) * SSD_GW] for q in range(3))
            f = _ssd_group(g, hd, xbc_ref[:, gs], bm_b, cm_b, d_x, st, kept=kept,
                           kept_dk=[dk_ref[0, 4 * g + h] for h in range(4)])
            xs, xd, gcb = f["xs"], f["xd"], f["gcb"]
            e_x, w_x, el_x, dt_x = f["e_x"], f["w_x"], f["el_x"], f["dt_x"]
            stb, dstb = st.astype(BF16), dst.astype(BF16)
            zf = z_ref[:, gs].astype(F32)
            sg = _sigmoid(zf)
            sz = zf * sg
            yv = f["y"]
            yg = yv * sz
            rstd = lax.rsqrt(_rowmean(yg * yg) + EPS)
            n = yg * rstd
            dyn_v = dy_ref[:, gs].astype(F32)
            dn = dyn_v * lp_ref[0:1, gs]
            dyg = rstd * (dn - n * _rowmean(dn * n))
            dy = dyg * sz
            dz_ref[:, gs] = (dyg * yv * (sg * (1.0 + zf * (1.0 - sg)))).astype(BF16)
            dyb = dy.astype(BF16)
            r_ = _dot(bm_b, dstb)
            dxd = w_x * r_
            dqb = (dy * e_x).astype(BF16)
            dcm = _dot_nt(dqb, stb)
            dst_scr[g] = dst * el_x + _dot_tn(cm_b, dqb)
            dbm = _dot_nt((xd * w_x).astype(BF16), dstb)
            xdb = f["xdb"]
            dgm = jnp.zeros((ll, ll), F32)
            for h in range(4):
                k = 4 * g + h
                hm = _head_mask(h)
                dxd = dxd + jnp.where(hm, _dot_tn(f["ms"][h], dyb), 0.0)
                dm = _dot_nt(jnp.where(hm, dy, 0.0).astype(BF16), xdb) * f["dks"][h]
                dgm = dgm + dm
                dseg = dm * gcb
                dcs = dcs + jnp.where(lane == k, jnp.sum(dseg, axis=1, keepdims=True), 0.0)
                dcs_t = dcs_t + jnp.where(subl == k, jnp.sum(dseg, axis=0, keepdims=True), 0.0)
            dgmb = dgm.astype(BF16)
            dxbc_ref[:, _c_cols(g)] = (dcm + _dot(dgmb, bm_b)).astype(BF16)
            dxbc_ref[:, _b_cols(g)] = (dbm + _dot_tn(dgmb, cm_b)).astype(BF16)
            v = _reduce4(r_ * xd * w_x, g)
            dcs = dcs + _reduce4(dy * f["yoff"], g) - v
            last = last + _colsum(v) + _reduce4(_colsum(dst * st) * el_x, g)
            dxx = dxx + _reduce4(dxd * xs, g)
            dxbc_ref[:, gs] = (d_x * dy + dxd * dt_x).astype(BF16)
            lpg_ref[0:1, gs] += _colsum(dyn_v * n)
            lpg_ref[1:2, gs] += _colsum(dy * xs)
        rowi = lax.broadcasted_iota(jnp.int32, (ll, 1), 0)
        da = _dot_hi(triu_ref[...], dcs - dcs_t.T + jnp.where(rowi == ll - 1, last, 0.0))
        ddt = (dxx + da * hp[1:2, :]) * _sigmoid(hd["xdt"])
        ddt_ref[...] = ddt
        hpg_ref[...] += jnp.concatenate([_colsum(ddt), _colsum(da * hd["dt"]), jnp.zeros((6, SSD_N), F32)], axis=0)

    rc = lambda c: nc - 1 - c
    return _call(
        body, name="ssd_bwd", grid=(nb, nc), side=side, sem=("arbitrary", "arbitrary"),
        args=(xbc, proj, dtraw, states, dyn, kept, kept_dk, hp, lp, tri, triu),
        in_specs=_ssd_specs(nc, rc) + [
            pl.BlockSpec((1, SSD_G, SSD_N, SSD_GW), lambda b, c: (b * nc + rc(c), 0, 0, 0)),
            pl.BlockSpec((SSD_L, SSD_INNER), lambda b, c: (b * nc + rc(c), 0)),
            pl.BlockSpec((SSD_L, 3 * SSD_INNER), lambda b, c: (b * nc + rc(c), 0)),
            pl.BlockSpec((1, SSD_HEADS, SSD_L, SSD_L), lambda b, c: (b * nc + rc(c), 0, 0, 0)),
            VMEM_FULL, VMEM_FULL, VMEM_FULL, VMEM_FULL],
        out_specs=[pl.BlockSpec((SSD_L, 2 * SSD_INNER), lambda b, c: (b * nc + rc(c), 0)),
                   pl.BlockSpec((SSD_L, SSD_INNER), lambda b, c: (b * nc + rc(c), 0)),
                   pl.BlockSpec((SSD_L, SSD_N), lambda b, c: (b * nc + rc(c), 0)),
                   pl.BlockSpec((8, SSD_N), lambda b, c: (0, 0)),
                   pl.BlockSpec((8, SSD_INNER), lambda b, c: (0, 0))],
        out_shape=[jax.ShapeDtypeStruct((t, 2 * SSD_INNER), BF16), jax.ShapeDtypeStruct((t, SSD_INNER), BF16),
                   jax.ShapeDtypeStruct((t, SSD_N), F32), jax.ShapeDtypeStruct((8, SSD_N), F32),
                   jax.ShapeDtypeStruct((8, SSD_INNER), F32)],
        scratch_shapes=[pltpu.VMEM((SSD_G, SSD_N, SSD_GW), F32)])


def ada_fwd(c_all, w_cols, b_cols):
    def body(c_ref, w_ref, b_ref, o_ref):
        cv = c_ref[...]
        o_ref[...] = _dot_hi(cv * _sigmoid(cv), w_ref[...]) + b_ref[...]

    return pl.pallas_call(body, name="ada_fwd", out_shape=jax.ShapeDtypeStruct((c_all.shape[0], w_cols.shape[1]), F32),
                          compiler_params=pltpu.CompilerParams(vmem_limit_bytes=VMEM_LIMIT))(c_all, w_cols, b_cols)


def ada_bwd(c_all, dmod_cols, dmod_all):
    def body(c_ref, dc_ref, da_ref, gw_ref, gb_ref):
        cv = c_ref[...]
        gw_ref[...] = lax.dot_general(cv * _sigmoid(cv), dc_ref[...], (((0,), (0,)), ((), ())),
                                      precision=lax.Precision.HIGHEST, preferred_element_type=F32)
        gb_ref[...] = _colsum(da_ref[...])

    return pl.pallas_call(
        body, name="ada_bwd",
        out_shape=[jax.ShapeDtypeStruct((c_all.shape[1], dmod_cols.shape[1]), F32),
                   jax.ShapeDtypeStruct((1, dmod_all.shape[1]), F32)],
        compiler_params=pltpu.CompilerParams(vmem_limit_bytes=VMEM_LIMIT))(c_all, dmod_cols, dmod_all)


def _adam_update(g, w, m, v):
    m2 = ADAM_B1 * m + (1.0 - ADAM_B1) * g
    v2 = ADAM_B2 * v + (1.0 - ADAM_B2) * (g * g)
    m_hat = m2 / (1.0 - ADAM_B1 ** ADAM_STEP)
    v_hat = v2 / (1.0 - ADAM_B2 ** ADAM_STEP)
    return -ADAM_LR * (m_hat / (jnp.sqrt(v_hat) + ADAM_EPS) + ADAM_WD * w), m2, v2


def adamw(parts, w, m, v, name):
    n, r, c = parts.shape
    tr = r if r <= 256 else 128

    def body(p_ref, w_ref, m_ref, v_ref, g_ref, d_ref, nm_ref, nv_ref):
        g = p_ref[0].astype(F32)
        for s in range(1, n):
            g = g + p_ref[s].astype(F32)
        g_ref[0] = g
        d_ref[0], nm_ref[0], nv_ref[0] = _adam_update(g, w_ref[0], m_ref[0], v_ref[0])

    blk = pl.BlockSpec((1, tr, c), lambda i: (0, i, 0))
    return pl.pallas_call(
        body, name=name, grid=(r // tr,),
        in_specs=[pl.BlockSpec((n, tr, c), lambda i: (0, i, 0)), blk, blk, blk], out_specs=[blk] * 4,
        out_shape=[jax.ShapeDtypeStruct((1, r, c), F32)] * 4,
        compiler_params=_cp("parallel"),
    )(parts, w, m, v)


SMALL_SRC = {
    'pre_norm1': ('vin', 0, 1024), 'post_norm1': ('vmg', 1, 1024), 'b_gate': ('vmg', 0, 2048),
    'lru_conv_b': ('accl', 4, 1024), 'lru_wa': ('gwa', None, None), 'lru_ba': ('dvec', 0, 1024),
    'lru_wx': ('gwx', None, None), 'lru_bx': ('dvec', 1, 1024), 'lru_lambda': ('dvec', 2, 1024),
    'ssd_conv_b': ('accs', 4, 4096), 'ssd_dt_bias': ('hpg', 0, SSD_HEADS), 'ssd_a_log': ('hpg', 1, SSD_HEADS),
    'ssd_d': ('lpg', 1, SSD_INNER), 'ssd_norm_w': ('lpg', 0, SSD_INNER), 'pre_norm2': ('vmlp', 0, 1024),
    'post_norm2': ('vmlp', 1, 1024)}
SMALL_ACCS = ('vin', 'vmg', 'vmlp', 'dvec', 'accl', 'accs', 'hpg', 'lpg', 'gwa', 'gwx')
SMALL_RIDE = ('vmg', 'vmlp', 'dvec', 'hpg', 'lpg', 'gwa', 'gwx')


def adamw_small(gathered, params):
    names = tuple(params)
    na = len(SMALL_ACCS)

    def body(*refs):
        acc = {k: functools.reduce(lambda p, q: p + q, [refs[i][s] for s in range(NDEV)])
               for i, k in enumerate(SMALL_ACCS)}
        ins = refs[na:na + 3 * len(names)]
        outs = refs[na + 3 * len(names):]
        for j, k in enumerate(names):
            w_ref, m_ref, v_ref = ins[3 * j:3 * j + 3]
            src, row, width = SMALL_SRC[k]
            wv = w_ref[...]
            if row is None:
                g = acc[src]
            elif k == 'ssd_d':
                li = lax.broadcasted_iota(jnp.int32, (SSD_INNER, SSD_N), 0)
                hi = lax.broadcasted_iota(jnp.int32, (SSD_INNER, SSD_N), 1)
                g = _dot_hi(acc[src], jnp.where(jnp.right_shift(li, 6) == hi, 1.0, 0.0))[row:row + 1, :SSD_HEADS]
            else:
                g = acc[src][row:row + 1, :width]
            if k == 'lru_lambda':
                g = g * (-1.0 / (1.0 + jnp.exp(wv)))
            if k == 'ssd_a_log':
                g = g * (-jnp.exp(wv))
            o = outs[4 * j:4 * j + 4]
            o[0][...] = g
            o[1][...], o[2][...], o[3][...] = _adam_update(g, wv, m_ref[...], v_ref[...])
        outs[-2][...] = acc['accl'][0:4, :]
        outs[-1][...] = acc['accs'][0:4, :]

    flat = [a for k in names for a in params[k]]
    out_shape = [jax.ShapeDtypeStruct(params[k][0].shape, F32) for k in names for _ in range(4)]
    out_shape += [jax.ShapeDtypeStruct((4, D_MODEL), F32), jax.ShapeDtypeStruct((4, 2 * SSD_INNER), F32)]
    res = pl.pallas_call(body, name="adamw_small", out_shape=out_shape,
                         compiler_params=pltpu.CompilerParams(vmem_limit_bytes=VMEM_LIMIT))(
        *[gathered[k] for k in SMALL_ACCS], *flat)
    return {k: res[4 * j:4 * j + 4] for j, k in enumerate(names)}, res[-2], res[-1]


def _dev_index(px, py, pc):
    return 4 * px + 2 * py + pc


class _Exchange:
    def __init__(self, arrs):
        self.arrs = list(arrs)
        self.na = len(self.arrs)
        self.scratch = [pltpu.SemaphoreType.DMA((7 * self.na,)), pltpu.SemaphoreType.DMA((7 * self.na,)),
                        pltpu.SemaphoreType.DMA((self.na,))]


class Gather(_Exchange):
    def __init__(self, arrs):
        super().__init__(arrs)
        self.out_shape = [jax.ShapeDtypeStruct((NDEV,) + a.shape, a.dtype) for a in self.arrs]

    def _plan(self, ins, outs, sems):
        na = self.na
        send_sems, recv_sems, local_sems = sems
        x, y, c = lax.axis_index("x"), lax.axis_index("y"), lax.axis_index("c")
        me, sibling = (x, y, c), (x, y, 1 - c)
        chips = [(1 - x, y), (x, 1 - y), (1 - x, 1 - y)]

        def copy(a, k, block, to, src=None):
            dst = outs[a].at[_dev_index(*block)]
            return pltpu.make_async_remote_copy(
                src_ref=dst if src is None else src, dst_ref=dst, send_sem=send_sems.at[a * 7 + k],
                recv_sem=recv_sems.at[a * 7 + k], device_id=to, device_id_type=MESH)

        mine = [pltpu.make_async_copy(ins[a], outs[a].at[_dev_index(*me)], local_sems.at[a]) for a in range(na)]
        first = []
        for a in range(na):
            first.append(copy(a, 0, me, sibling, src=ins[a]))
            first += [copy(a, 1 + j, me, (*chip, c), src=ins[a]) for j, chip in enumerate(chips)]
        return copy, mine, first, me, sibling, chips, c

    def start(self, ins, outs, sems):
        _, mine, first, *_ = self._plan(ins, outs, sems)
        for cp in mine + first:
            cp.start()

    def finish(self, ins, outs, sems):
        copy, mine, first, me, sibling, chips, c = self._plan(ins, outs, sems)
        passed = []
        for j, chip in enumerate(chips):
            for a in range(self.na):
                copy(a, 1 + j, (*chip, c), me).wait_recv()
                cp = copy(a, 4 + j, (*chip, c), sibling)
                cp.start()
                passed.append(cp)
        for a in range(self.na):
            copy(a, 0, sibling, me).wait_recv()
            for j, chip in enumerate(chips):
                copy(a, 4 + j, (*chip, 1 - c), me).wait_recv()
        for cp in first + passed:
            cp.wait_send()
        for cp in mine:
            cp.wait()


class GatherRelay(Gather):
    def _plan(self, ins, outs, sems):
        na = self.na
        send_sems, recv_sems, local_sems = sems
        x, y, c = lax.axis_index("x"), lax.axis_index("y"), lax.axis_index("c")
        me, sibling = (x, y, c), (x, y, 1 - c)
        xn, yn, dg = (1 - x, y), (x, 1 - y), (1 - x, 1 - y)
        south = c == 0
        pick = lambda a, b: tuple(jnp.where(south, p, q) for p, q in zip(a, b))
        relay_to = pick(yn, xn)
        relay_of = pick(xn, yn)

        def copy(a, k, block, to, src=None):
            dst = outs[a].at[_dev_index(*block)]
            return pltpu.make_async_remote_copy(
                src_ref=dst if src is None else src, dst_ref=dst, send_sem=send_sems.at[a * 7 + k],
                recv_sem=recv_sems.at[a * 7 + k], device_id=to, device_id_type=MESH)

        mine = [pltpu.make_async_copy(ins[a], outs[a].at[_dev_index(*me)], local_sems.at[a]) for a in range(na)]
        first = []
        for a in range(na):
            first += [copy(a, 0, me, sibling, src=ins[a]), copy(a, 1, me, (*xn, c), src=ins[a]),
                      copy(a, 2, me, (*yn, c), src=ins[a])]
        return copy, mine, first, me, sibling, (xn, yn, dg), c, relay_to, relay_of

    def start(self, ins, outs, sems):
        _, mine, first, *_ = self._plan(ins, outs, sems)
        for cp in mine + first:
            cp.start()

    def finish(self, ins, outs, sems):
        copy, mine, first, me, sibling, (xn, yn, dg), c, relay_to, relay_of = self._plan(ins, outs, sems)
        later = []
        for a in range(self.na):
            copy(a, 1, (*xn, c), me).wait_recv()
            copy(a, 2, (*yn, c), me).wait_recv()
            later.append(copy(a, 3, (*relay_of, c), (*relay_to, c)))
            later += [copy(a, 4, (*xn, c), sibling), copy(a, 5, (*yn, c), sibling)]
            for cp in later[-3:]:
                cp.start()
        for a in range(self.na):
            copy(a, 3, (*dg, c), me).wait_recv()
            cp = copy(a, 6, (*dg, c), sibling)
            cp.start()
            later.append(cp)
        for a in range(self.na):
            copy(a, 0, sibling, me).wait_recv()
            for k, chip in ((4, xn), (5, yn), (6, dg)):
                copy(a, k, (*chip, 1 - c), me).wait_recv()
        for cp in first + later:
            cp.wait_send()
        for cp in mine:
            cp.wait()


class Scatter(_Exchange):
    def __init__(self, arrs):
        super().__init__(arrs)
        self.out_shape = [jax.ShapeDtypeStruct(a.shape, a.dtype) for a in self.arrs]

    def _plan(self, ins, outs, sems, arrivals):
        send_sems, recv_sems, local_sems = sems
        x, y, c = lax.axis_index("x"), lax.axis_index("y"), lax.axis_index("c")
        me = _dev_index(x, y, c)
        masks = [(mx, my, mc) for mx in (0, 1) for my in (0, 1) for mc in (0, 1)][1:]
        flip = lambda v, bit: 1 - v if bit else v
        mine = [pltpu.make_async_copy(ins[a].at[me], outs[a].at[me], local_sems.at[a]) for a in range(self.na)]
        sends, recvs = [], []
        for k, (mx, my, mc) in enumerate(masks):
            peer = (flip(x, mx), flip(y, my), flip(c, mc))
            pidx = _dev_index(*peer)
            for a in range(self.na):
                on = dict(send_sem=send_sems.at[a * 7 + k], recv_sem=recv_sems.at[a * 7 + k], device_id=peer,
                          device_id_type=MESH)
                sends.append(pltpu.make_async_remote_copy(src_ref=ins[a].at[pidx], dst_ref=outs[a].at[me], **on))
                if arrivals:
                    recvs.append(pltpu.make_async_remote_copy(src_ref=ins[a].at[pidx], dst_ref=outs[a].at[pidx], **on))
        return mine, sends, recvs

    def start(self, ins, outs, sems):
        mine, sends, _ = self._plan(ins, outs, sems, arrivals=False)
        for cp in mine + sends:
            cp.start()

    def finish(self, ins, outs, sems):
        mine, sends, recvs = self._plan(ins, outs, sems, arrivals=True)
        for cp in recvs:
            cp.wait_recv()
        for cp in sends:
            cp.wait_send()
        for cp in mine:
            cp.wait()


def exchange_call(ex, name):
    na = ex.na

    def body(*refs):
        ins, outs, sems = refs[:na], refs[na:2 * na], refs[2 * na:]
        ex.start(ins, outs, sems)
        ex.finish(ins, outs, sems)

    return pl.pallas_call(body, name=name, in_specs=[ANY] * na, out_specs=[ANY] * na, out_shape=ex.out_shape,
                          scratch_shapes=ex.scratch)(*ex.arrs)


def all_gather(arrs, name, relay=False):
    return exchange_call((GatherRelay if relay else Gather)(arrs), name)


def _call(body, *, name, grid, in_specs, out_specs, out_shape, scratch_shapes=(), sem, args, side=None):
    if side is None:
        outs = pl.pallas_call(body, name=name, grid=grid, in_specs=list(in_specs), out_specs=list(out_specs),
                              out_shape=list(out_shape), scratch_shapes=list(scratch_shapes),
                              compiler_params=_cp(*sem))(*args)
        return outs, []
    ni, no, ns, na = len(in_specs), len(out_specs), len(scratch_shapes), side.na

    def wrapped(*refs):
        ins, s_in = refs[:ni], refs[ni:ni + na]
        outs, s_out = refs[ni + na:ni + na + no], refs[ni + na + no:ni + 2 * na + no]
        scr, sems = refs[ni + 2 * na + no:ni + 2 * na + no + ns], refs[ni + 2 * na + no + ns:]
        pids = [pl.program_id(i) for i in range(len(grid))]
        first = functools.reduce(lambda p, q: p & q, [p == 0 for p in pids])
        last = functools.reduce(lambda p, q: p & q, [p == g - 1 for p, g in zip(pids, grid)])

        @pl.when(first)
        def _():
            side.start(s_in, s_out, sems)

        body(*ins, *outs, *scr)

        @pl.when(last)
        def _():
            side.finish(s_in, s_out, sems)

    outs = pl.pallas_call(
        wrapped, name=name, grid=grid, in_specs=list(in_specs) + [ANY] * na, out_specs=list(out_specs) + [ANY] * na,
        out_shape=list(out_shape) + side.out_shape, scratch_shapes=list(scratch_shapes) + side.scratch,
        compiler_params=_cp(*["arbitrary"] * len(grid)))(*args, *side.arrs)
    return outs[:no], outs[no:]


WEIGHTS = ('w_ada', 'b_ada', 'pre_norm1', 'post_norm1', 'w_in', 'b_gate', 'lru_conv_w', 'lru_conv_b', 'lru_wa',
           'lru_ba', 'lru_wx', 'lru_bx', 'lru_lambda', 'w_pa', 'ssd_conv_w', 'ssd_conv_b', 'ssd_dt_bias', 'ssd_a_log',
           'ssd_d', 'ssd_norm_w', 'w_pb', 'w_out', 'pre_norm2', 'post_norm2', 'w_ff1', 'w_ff2')
BIG = ('w_in', 'w_pa', 'w_pb', 'w_out', 'w_ff1', 'w_ff2')
REPL = ('pre_norm1', 'post_norm1', 'b_gate', 'lru_conv_b', 'lru_wa', 'lru_ba', 'lru_wx', 'lru_bx', 'lru_lambda',
        'ssd_conv_b', 'ssd_dt_bias', 'ssd_a_log', 'ssd_d', 'ssd_norm_w', 'pre_norm2', 'post_norm2')
LANES = 1024


def _rows(n):
    return -(-n // LANES)


def _pack(vals, total_rows):
    parts = []
    for v in vals:
        f = v.reshape(-1).astype(F32)
        parts.append(jnp.pad(f, (0, _rows(f.shape[0]) * LANES - f.shape[0])))
    flat = jnp.concatenate(parts)
    return jnp.pad(flat.reshape(-1, LANES), ((0, total_rows - flat.shape[0] // LANES), (0, 0)))


def _unpack(slab, shapes):
    out, r = [], 0
    for s in shapes:
        n = int(np.prod(s))
        out.append(slab[r:r + _rows(n)].reshape(-1)[:n].reshape(s))
        r += _rows(n)
    return out


def _block_diag4(w):
    w4 = w.reshape(4, 4, 64, 64)
    eye = jnp.eye(4, dtype=w.dtype)
    return (w4[:, :, :, None, :] * eye[None, :, None, :, None]).reshape(4, LRU_BLOCK, LRU_BLOCK)


def _diag_blocks4(m):
    m5 = m.reshape(4, 4, 64, 4, 64)
    return jnp.stack([m5[:, a, :, a, :] for a in range(4)], axis=1).reshape(LRU_HEADS, 64, 64)


def kernel(x, c, w_ada, b_ada, pre_norm1, post_norm1, w_in, b_gate, lru_conv_w, lru_conv_b, lru_wa, lru_ba, lru_wx, lru_bx, lru_lambda, w_pa, ssd_conv_w, ssd_conv_b, ssd_dt_bias, ssd_a_log, ssd_d, ssd_norm_w, w_pb, w_out, pre_norm2, post_norm2, w_ff1, w_ff2, loss_target, m_w_ada, m_b_ada, m_pre_norm1, m_post_norm1, m_w_in, m_b_gate, m_lru_conv_w, m_lru_conv_b, m_lru_wa, m_lru_ba, m_lru_wx, m_lru_bx, m_lru_lambda, m_w_pa, m_ssd_conv_w, m_ssd_conv_b, m_ssd_dt_bias, m_ssd_a_log, m_ssd_d, m_ssd_norm_w, m_w_pb, m_w_out, m_pre_norm2, m_post_norm2, m_w_ff1, m_w_ff2, v_w_ada, v_b_ada, v_pre_norm1, v_post_norm1, v_w_in, v_b_gate, v_lru_conv_w, v_lru_conv_b, v_lru_wa, v_lru_ba, v_lru_wx, v_lru_bx, v_lru_lambda, v_w_pa, v_ssd_conv_w, v_ssd_conv_b, v_ssd_dt_bias, v_ssd_a_log, v_ssd_d, v_ssd_norm_w, v_w_pb, v_w_out, v_pre_norm2, v_post_norm2, v_w_ff1, v_w_ff2):
    given = dict(locals())
    w = {k: given[k] for k in WEIGHTS}
    mom = {k: given["m_" + k] for k in WEIGHTS}
    var = {k: given["v_" + k] for k in WEIGHTS}
    nb, seq, _ = x.shape
    assert nb == 2 and seq % 512 == 0, (nb, seq)
    t = nb * seq
    me = _dev_index(lax.axis_index("x"), lax.axis_index("y"), lax.axis_index("c"))
    x2 = x.reshape(t, D_MODEL)
    tgt2 = loss_target.reshape(t, D_MODEL)
    ada_cols = w_ada.shape[2]

    slab = jnp.zeros((16, LANES), F32)
    slab = slab.at[0:nb].set(c)
    slab = slab.at[2:6, 0:lru_conv_w.shape[2]].set(lru_conv_w[0])
    slab = slab.at[6:10, 0:ssd_conv_w.shape[2]].set(ssd_conv_w[0])
    g1, gw_in = all_gather([slab, w_in.astype(BF16)], "gather_cond_w_in", relay=True)
    c_all = g1[:, 0:nb].reshape(NDEV * nb, D_MODEL)
    lru_cw = g1[:, 2:6, 0:lru_conv_w.shape[2]].transpose(1, 0, 2).reshape(4, D_MODEL)
    ssd_cw = g1[:, 6:10, 0:ssd_conv_w.shape[2]].transpose(1, 0, 2).reshape(4, 2 * SSD_INNER)
    b_cols = lax.dynamic_slice(b_ada, (0, me * ada_cols), (1, ada_cols))
    mod_cols = ada_fwd(c_all, w_ada[0], b_cols)
    (g2,) = all_gather([mod_cols], "gather_mod")
    mod_all = g2.transpose(1, 0, 2).reshape(NDEV * nb, N_MOD * D_MODEL)
    mod_mine = lax.dynamic_slice(mod_all, (me * nb, 0), (nb, N_MOD * D_MODEL)).reshape(nb, N_MOD, D_MODEL)
    mod8 = jnp.pad(mod_mine, ((0, 0), (0, 8 - N_MOD), (0, 0)))

    shard = IN_DIM // NDEV
    kd, od = DT_COL0 // shard, DT_COL0 % shard
    assert od + SSD_HEADS <= shard
    gb = gw_in[:, 0]
    w_main = jnp.concatenate([gb[k] for k in range(kd)] + [gb[kd][:, :od], gb[kd][:, od + SSD_HEADS:]]
                             + [gb[k] for k in range(kd + 1, NDEV)], axis=1)
    w_dt = jnp.pad(gb[kd][:, od:od + SSD_HEADS], ((0, 0), (0, 128 - SSD_HEADS)))

    wa_bd = _block_diag4(lru_wa[0]).astype(BF16)
    wx_bd = _block_diag4(lru_wx[0]).astype(BF16)
    lam = lru_lambda[0]
    vec = _pack([lru_ba, lru_bx, jax.nn.softplus(-lam)], 8)
    tri, triu = ssd_consts()
    hp, lp = ssd_params(ssd_dt_bias[0], ssd_a_log[0], ssd_d[0], ssd_norm_w[0])

    rest = Gather([w[k].astype(BF16) for k in BIG[1:]])
    cw_all = jnp.concatenate([lru_cw, ssd_cw], axis=1)
    cb_all = jnp.concatenate([lru_conv_b, ssd_conv_b], axis=1)
    (proj, h1t, dtraw, xa, xbc, dsilu), gw = in_proj_fwd(x2, mod8, pre_norm1, w_main, w_dt, cw_all, cb_all, seq, side=rest)
    w_pa_f = gw[0].reshape(D_MODEL, D_MODEL)
    w_pb_f = gw[1].reshape(SSD_INNER, D_MODEL)
    w_out_f = gw[2].reshape(D_MODEL, D_MODEL)
    w_ff1_f = gw[3][:, 0]
    w_ff2_f = gw[4].reshape(D_FF, D_MODEL)
    ya_in, hst, lru_gates = lru_fwd(xa, proj, wa_bd, wx_bd, vec, nb, seq)
    yb_in, states, ssd_kept, ssd_dk = ssd_fwd(xbc, proj, dtraw, hp, lp, tri, nb, seq)
    yab, out1, x1 = merge_fwd(ya_in, yb_in, proj, x2, mod8, b_gate, post_norm1, w_pa_f, w_pb_f, w_out_f, seq)

    dx1, h2, da1, act, dy2, loss8, vacc_mlp, dmod_mlp = mlp_fwd_bwd(
        x1, tgt2, mod8, pre_norm2, post_norm2, w_ff1_f, w_ff2_f, nb, seq)
    wg = dict(out_dtype=BF16, ta=True, tm=1024, tn=1024, tk=1024)
    dw_ff1 = matmul(h2, da1, name="wgrad_ff1", blocked_out=D_FF // NDEV, **wg)
    dw_ff2 = matmul(act, dy2, name="wgrad_ff2", **wg)
    dya_in, dyb_in, dgates, dyab, dout1, merged, vacc_mg, dmod_mg = merge_bwd(
        dx1, out1, yab, proj, mod8, b_gate, post_norm1, w_pa_f, w_pb_f, w_out_f, nb, seq)
    dw_out = matmul(merged, dout1, name="wgrad_out", **wg)
    dw_pa = matmul(ya_in, dyab, name="wgrad_pa", n=D_MODEL, b_off=0, **wg)
    dw_pb = matmul(yb_in, dyab, name="wgrad_pb", n=D_MODEL, b_off=1, **wg)
    by_rows = lambda g: g.reshape(NDEV, g.shape[0] // NDEV, g.shape[1])
    (dxa, dlg, dwa_bd, dwx_bd, dvec), parts_ff = lru_bwd(
        dya_in, xa, proj, hst, lru_gates, wa_bd, wx_bd, vec, nb, seq, side=Scatter([dw_ff1, by_rows(dw_ff2)]))
    (dxbc, dz, ddt, hpg, lpg), parts_mg = ssd_bwd(xbc, proj, dtraw, hp, lp, tri, triu, states, ssd_kept, ssd_dk, dyb_in, nb, seq,
                                                  side=Scatter([by_rows(dw_pa), by_rows(dw_pb), by_rows(dw_out)]))
    ddt_b = ddt.astype(BF16)
    accs = dict(vmg=vacc_mg, vmlp=vacc_mlp, dvec=dvec, hpg=hpg, lpg=lpg,
                gwa=_diag_blocks4(dwa_bd).reshape(LRU_HEADS * 64, 64), gwx=_diag_blocks4(dwx_bd).reshape(LRU_HEADS * 64, 64))
    (dw_main, dw_dt, dlx, dxr, acc_l, acc_s), g_small = in_proj_wgrad(
        h1t, proj, dxa, dxbc, dsilu, dlg, dz, dgates, ddt_b, cw_all, seq, side=Gather([accs[k] for k in SMALL_RIDE]))
    pieces = (dlx, dlg, dz, dxr, dgates)
    cut = lambda k: dw_main[:, k * shard - (SSD_HEADS if k > kd else 0):(k + 1) * shard - (SSD_HEADS if k >= kd else 0)]
    blk_dt = jnp.concatenate([dw_main[:, kd * shard:DT_COL0], dw_dt[:, :SSD_HEADS],
                              dw_main[:, DT_COL0:(kd + 1) * shard - SSD_HEADS]], axis=1)
    dw_blocks = jnp.stack([blk_dt if k == kd else cut(k) for k in range(NDEV)])
    (grad_x, vacc_in, dmod_in), parts_in = in_proj_bwd(pieces, ddt_b, dx1, x2, mod8, pre_norm1, w_main, w_dt, nb, seq,
                                                       side=Scatter([dw_blocks]))
    parts = dict(zip(BIG, (parts_in[0], *parts_mg, *parts_ff)))

    dmod = (dmod_in + dmod_mg + dmod_mlp)[:, :N_MOD].reshape(nb, N_MOD * D_MODEL)
    g3, g_vin, g_accl, g_accs = all_gather([jnp.pad(dmod, ((0, 8 - nb), (0, 0))), vacc_in, acc_l, acc_s], "gather_dmod")
    dmod_all = g3[:, :nb].reshape(NDEV * nb, N_MOD * D_MODEL)
    dmod_cols = lax.dynamic_slice(dmod_all, (0, me * ada_cols), (NDEV * nb, ada_cols))
    g_w_ada, g_b_ada = ada_bwd(c_all, dmod_cols, dmod_all)

    res = {}
    for k in BIG:
        res[k] = adamw(parts[k], w[k], mom[k], var[k], "adamw_" + k)
    res['w_ada'] = adamw(g_w_ada[None], w_ada, m_w_ada, v_w_ada, "adamw_w_ada")

    gathered = dict(zip(SMALL_RIDE, g_small), vin=g_vin, accl=g_accl, accs=g_accs)
    view = lambda a: a.reshape(-1, a.shape[-1])
    res_a, g_lru_cw, g_ssd_cw = adamw_small(gathered, {k: (view(w[k]), view(mom[k]), view(var[k])) for k in REPL})
    res.update(res_a)
    lcw, scw = lru_conv_w.shape[2], ssd_conv_w.shape[2]
    sharded = {'b_ada': g_b_ada[None], 'lru_conv_w': lax.dynamic_slice(g_lru_cw, (0, me * lcw), (4, lcw))[None],
               'ssd_conv_w': lax.dynamic_slice(g_ssd_cw, (0, me * scw), (4, scw))[None]}
    for k, g in sharded.items():
        as3 = lambda a: a.reshape(g.shape)
        res[k] = adamw(g, as3(w[k]), as3(mom[k]), as3(var[k]), "adamw_" + k)

    loss = lax.psum(loss8[0, 0], ("x", "y", "c"))
    outs = [[res[k][j].reshape(w[k].shape) for k in WEIGHTS] for j in range(4)]
    return (loss, grad_x.reshape(x.shape), *outs[0], *outs[1], *outs[2], *outs[3])
```
